```python
import math
import jax, jax.numpy as jnp
from jax import lax
import numpy as np

D_MODEL = 1024
BATCH = 32
SEQ = 256
DEPTH = 4
DEC_BATCH = 2
DEC_SEQ = 2048
PAST_LEN = 256

GRID_W = 64
N_MIXERS = 3
N_CONV_LAYERS = (DEPTH + 2) // 3
N_NA_LAYERS = (DEPTH + 1) // 3
N_SSD_LAYERS = DEPTH // 3
RMS_EPS = 1e-6
LN_EPS = 1e-5
CONF_WIDTH = D_MODEL
CONF_K = 31
NA_HEADS = 16
NA_HEAD_DIM = D_MODEL // NA_HEADS
NA_WIDTH = NA_HEADS * NA_HEAD_DIM
NA_KH_MAX = 8
NA_KW = 16
NA_QB = 16
NA_KB = NA_QB + NA_KW
ATTN_BLOCK = 128
NEG_INF = -1e30
SSD_INNER = 2 * D_MODEL
SSD_HEAD_DIM = 64
SSD_HEADS = SSD_INNER // SSD_HEAD_DIM
SSD_STATE = 128
SSD_GROUPS = 4
SSD_GN = SSD_GROUPS * SSD_STATE
SSD_CONV = 7
SSD_CHUNK = 128

kernel_name = "hybrid_diffusion_trunk_step"

F32 = jnp.float32


def rms_norm(x, w):
    xf = x.astype(F32)
    y = xf * lax.rsqrt(jnp.mean(xf * xf, axis=-1, keepdims=True) + RMS_EPS)
    return (y * w.astype(F32)).astype(x.dtype)


def layer_norm(x, w, b):
    xf = x.astype(F32)
    mu = jnp.mean(xf, axis=-1, keepdims=True)
    xc = xf - mu
    var = jnp.mean(xc * xc, axis=-1, keepdims=True)
    return (xc * lax.rsqrt(var + LN_EPS) * w.astype(F32) + b.astype(F32)).astype(x.dtype)


def depthwise_conv(x, w, b):
    k = w.shape[0]
    pad = k // 2
    y = lax.conv_general_dilated(x, w[:, None, :].astype(x.dtype), window_strides=(1,),
                                 padding=[(pad, k - 1 - pad)],
                                 dimension_numbers=('NWC', 'WIO', 'NWC'),
                                 feature_group_count=x.shape[-1])
    return y + b


def modulated_norm(x, cond, ada_w, ada_b, norm_w):
    mod = jax.nn.silu(cond) @ ada_w + ada_b
    shift, scale, gate = jnp.split(mod[:, None, :], 3, axis=-1)
    h = rms_norm(x, norm_w) * (1 + scale) + shift
    return h, gate


def conformer_conv_mixer(h, w_in, b_in, w_dw, b_dw, ln_w, ln_b, w_out, b_out):
    v, g, z = jnp.split(h @ w_in + b_in, 3, axis=-1)
    u = v * jax.nn.sigmoid(g)
    u = depthwise_conv(u, w_dw, b_dw)
    u = jax.nn.silu(layer_norm(u, ln_w, ln_b))
    return (u * jax.nn.silu(z)) @ w_out + b_out


def na_project(h, w_in):
    B, L, _ = h.shape
    q, k, v, z = jnp.split(h @ w_in, 4, axis=-1)
    shp = (B, L, NA_HEADS, NA_HEAD_DIM)
    return q.reshape(shp), k.reshape(shp), v.reshape(shp), z


def context_attention(q, k, v):
    B, S, H, Dh = q.shape
    scale = Dh ** -0.5
    qb = q.reshape(B, S // ATTN_BLOCK, ATTN_BLOCK, H, Dh).transpose(1, 0, 2, 3, 4)

    def block(qi):
        s = jnp.einsum('bqhd,bkhd->bhqk', qi, k).astype(F32) * scale
        p = jax.nn.softmax(s, axis=-1).astype(v.dtype)
        return jnp.einsum('bhqk,bkhd->bqhd', p, v)

    o = lax.map(block, qb)
    return o.transpose(1, 0, 2, 3, 4).reshape(B, S, H, Dh)


def na_block_layout(rows):
    kh = min(NA_KH_MAX, rows)
    r = np.arange(rows)
    row_start = np.clip(r - kh // 2, 0, rows - kh)
    c = np.arange(GRID_W)
    col_start = np.clip(c - NA_KW // 2, 0, GRID_W - NA_KW)
    n_cb = GRID_W // NA_QB
    cb0 = np.arange(n_cb) * NA_QB
    kcol0 = np.clip(cb0 - NA_KW // 2, 0, GRID_W - NA_KB)
    key_row = row_start[:, None] + np.arange(kh)[None, :]
    key_col = kcol0[:, None] + np.arange(NA_KB)[None, :]
    q_col = cb0[:, None] + np.arange(NA_QB)[None, :]
    kidx = key_row[:, None, :, None] * GRID_W + key_col[None, :, None, :]
    kidx = kidx.reshape(rows, n_cb, kh * NA_KB)
    dr = key_row - r[:, None]
    dc = key_col[:, None, :] - q_col[:, :, None]
    dc_idx = np.clip(dc, -(NA_KW - 1), NA_KW - 1) + NA_KW - 1
    bidx = (dr[:, None, None, :, None] + NA_KH_MAX - 1) * (2 * NA_KW - 1) + dc_idx[None, :, :, None, :]
    bidx = bidx.reshape(rows, n_cb, NA_QB, kh * NA_KB)
    qcs = col_start[q_col][..., None]
    valid = (key_col[:, None, :] >= qcs) & (key_col[:, None, :] < qcs + NA_KW)
    valid = np.broadcast_to(valid[:, :, None, :], (n_cb, NA_QB, kh, NA_KB)).reshape(n_cb, NA_QB, kh * NA_KB)
    return kidx.astype(np.int32), bidx.astype(np.int32), valid


def neighbourhood_attention(q, k, v, k_ctx, v_ctx, rpb):
    B, T, H, Dh = q.shape
    rows = T // GRID_W
    n_cb = GRID_W // NA_QB
    kidx, bidx, valid = na_block_layout(rows)
    mask = jnp.asarray(valid)
    scale = Dh ** -0.5
    rpb_flat = rpb.reshape(H, -1)
    q_rows = q.reshape(B, rows, n_cb, NA_QB, H, Dh).transpose(1, 0, 2, 3, 4, 5)

    def row_block(args):
        qr, kidx_r, bidx_r = args
        kr = k[:, kidx_r]
        vr = v[:, kidx_r]
        s_loc = jnp.einsum('bnqhd,bnkhd->bhnqk', qr, kr).astype(F32) * scale
        s_loc = s_loc + rpb_flat[:, bidx_r].astype(F32)[None]
        s_loc = jnp.where(mask[None, None], s_loc, NEG_INF)
        s_ctx = jnp.einsum('bnqhd,bchd->bhnqc', qr, k_ctx).astype(F32) * scale
        n_loc = s_loc.shape[-1]
        p = jax.nn.softmax(jnp.concatenate([s_loc, s_ctx], axis=-1), axis=-1).astype(v.dtype)
        return (jnp.einsum('bhnqk,bnkhd->bnqhd', p[..., :n_loc], vr)
                + jnp.einsum('bhnqc,bchd->bnqhd', p[..., n_loc:], v_ctx))

    o = lax.map(row_block, (q_rows, jnp.asarray(kidx), jnp.asarray(bidx)))
    return o.transpose(1, 0, 2, 3, 4, 5).reshape(B, T, H, Dh)


def na_context_mixer(h, w_in, w_out):
    B, L, _ = h.shape
    q, k, v, z = na_project(h, w_in)
    o = context_attention(q, k, v)
    return (o.reshape(B, L, NA_WIDTH) * jax.nn.silu(z)) @ w_out, k, v


def na_latent_mixer(h, k_ctx, v_ctx, w_in, rpb, w_out):
    B, T, _ = h.shape
    q, k, v, z = na_project(h, w_in)
    o = neighbourhood_attention(q, k, v, k_ctx, v_ctx, rpb)
    return (o.reshape(B, T, NA_WIDTH) * jax.nn.silu(z)) @ w_out


def ssd_chunked(x, dt, A, Bm, Cm, h0):
    Bsz, L, H, P = x.shape
    N = Bm.shape[-1]
    Q = SSD_CHUNK
    nc = L // Q
    a_cum = jnp.cumsum((dt * A).reshape(Bsz, nc, Q, H), axis=2)
    xdt = (x.astype(F32) * dt[..., None]).reshape(Bsz, nc, Q, H, P)
    Bc = Bm.astype(F32).reshape(Bsz, nc, Q, H, N)
    Cc = Cm.astype(F32).reshape(Bsz, nc, Q, H, N)
    tri = np.tril(np.ones((Q, Q), dtype=bool))[None, None, :, :, None]
    seg = a_cum[:, :, :, None, :] - a_cum[:, :, None, :, :]
    decay = jnp.exp(jnp.where(tri, seg, -jnp.inf))
    scores = jnp.einsum('bcihn,bcjhn->bcijh', Cc, Bc) * decay
    y_diag = jnp.einsum('bcijh,bcjhp->bcihp', scores, xdt)
    decay_to_end = jnp.exp(a_cum[:, :, -1:, :] - a_cum)
    states = jnp.einsum('bcjhn,bcjh,bcjhp->bchpn', Bc, decay_to_end, xdt)
    chunk_decay = jnp.exp(a_cum[:, :, -1, :])

    def step(hs, inp):
        st, dec = inp
        return hs * dec[:, :, None, None] + st, hs

    h_final, h_in = lax.scan(step, h0.astype(F32),
                             (states.transpose(1, 0, 2, 3, 4), chunk_decay.transpose(1, 0, 2)))
    h_in = h_in.transpose(1, 0, 2, 3, 4)
    y_off = jnp.einsum('bcihn,bchpn,bcih->bcihp', Cc, h_in, jnp.exp(a_cum))
    return (y_diag + y_off).reshape(Bsz, L, H, P), h_final


def ssd_mixer(h, h0, w_in, w_conv, b_conv, a_log, dt_bias, d_skip, norm_w, w_out):
    B, L, _ = h.shape
    H, P, N, G = SSD_HEADS, SSD_HEAD_DIM, SSD_STATE, SSD_GROUPS
    z, xbc, dt_raw = jnp.split(h @ w_in, [SSD_INNER, 2 * SSD_INNER + 2 * SSD_GN], axis=-1)
    xbc = jax.nn.silu(depthwise_conv(xbc, w_conv, b_conv))
    x, Bm, Cm = jnp.split(xbc, [SSD_INNER, SSD_INNER + SSD_GN], axis=-1)
    x = x.reshape(B, L, H, P)
    Bm = jnp.repeat(Bm.reshape(B, L, G, N), H // G, axis=2)
    Cm = jnp.repeat(Cm.reshape(B, L, G, N), H // G, axis=2)
    dt = jax.nn.softplus(dt_raw.reshape(B, L, 2, H).astype(F32) + dt_bias.astype(F32))
    A = -jnp.exp(a_log.astype(F32))
    flip = lambda t: jnp.flip(t, axis=1)
    y_f, s_f = ssd_chunked(x, dt[:, :, 0], A[0], Bm, Cm, h0[:, 0])
    y_b, s_b = ssd_chunked(flip(x), flip(dt[:, :, 1]), A[1], flip(Bm), flip(Cm), h0[:, 1])
    y = y_f + flip(y_b) + x.astype(F32) * d_skip.astype(F32)[:, None]
    y = y.reshape(B, L, SSD_INNER).astype(h.dtype)
    y = rms_norm(y * jax.nn.silu(z), norm_w)
    return y @ w_out, jnp.stack([s_f, s_b], axis=1).astype(h.dtype)


def setup_inputs(seed: int = 0) -> dict:
    key = jax.random.key(seed)
    ks = iter(jax.random.split(key, 48))

    def nrm(shape, s=1.0):
        return jax.random.normal(next(ks), shape, F32) * s

    D = D_MODEL
    ssd_in_cols = 2 * SSD_INNER + 2 * SSD_GN + 2 * SSD_HEADS
    dt0 = jnp.exp(jax.random.uniform(next(ks), (N_SSD_LAYERS, 2, SSD_HEADS), F32,
                                     minval=math.log(1e-3), maxval=math.log(1e-1)))
    dt_bias = dt0 + jnp.log(-jnp.expm1(-dt0))
    a_log = jnp.log(jax.random.uniform(next(ks), (N_SSD_LAYERS, 2, SSD_HEADS), F32, minval=1.0, maxval=16.0))
    return {
        'x_prompt': nrm((BATCH, SEQ, D)),
        'x_sample': nrm((DEC_BATCH, DEC_SEQ, D)),
        'cache_k': nrm((DEC_BATCH, N_NA_LAYERS, PAST_LEN, NA_HEADS, NA_HEAD_DIM)),
        'cache_v': nrm((DEC_BATCH, N_NA_LAYERS, PAST_LEN, NA_HEADS, NA_HEAD_DIM)),
        'state_ssm': nrm((DEC_BATCH, N_SSD_LAYERS, 2, SSD_HEADS, SSD_HEAD_DIM, SSD_STATE), 0.5),
        'c': nrm((DEC_BATCH, D)),
        'c_ctx': nrm((D,)),
        'ada_w': nrm((DEPTH, D, 3 * D), D ** -0.5),
        'ada_b': nrm((DEPTH, 3 * D), 0.1),
        'norm_w': 1.0 + nrm((DEPTH, D), 0.05),
        'final_norm_w': 1.0 + nrm((D,), 0.05),
        'conv_w_in': nrm((N_CONV_LAYERS, D, 3 * CONF_WIDTH), D ** -0.5),
        'conv_b_in': nrm((N_CONV_LAYERS, 3 * CONF_WIDTH), 0.02),
        'conv_w_dw': nrm((N_CONV_LAYERS, CONF_K, CONF_WIDTH), CONF_K ** -0.5),
        'conv_b_dw': nrm((N_CONV_LAYERS, CONF_WIDTH), 0.02),
        'conv_ln_w': 1.0 + nrm((N_CONV_LAYERS, CONF_WIDTH), 0.05),
        'conv_ln_b': nrm((N_CONV_LAYERS, CONF_WIDTH), 0.02),
        'conv_w_out': nrm((N_CONV_LAYERS, CONF_WIDTH, D), CONF_WIDTH ** -0.5),
        'conv_b_out': nrm((N_CONV_LAYERS, D), 0.02),
        'na_w_in': nrm((N_NA_LAYERS, D, 4 * NA_WIDTH), D ** -0.5),
        'na_rpb': nrm((N_NA_LAYERS, NA_HEADS, 2 * NA_KH_MAX - 1, 2 * NA_KW - 1), 0.1),
        'na_w_out': nrm((N_NA_LAYERS, NA_WIDTH, D), NA_WIDTH ** -0.5),
        'ssd_w_in': nrm((N_SSD_LAYERS, D, ssd_in_cols), D ** -0.5),
        'ssd_w_conv': nrm((N_SSD_LAYERS, SSD_CONV, SSD_INNER + 2 * SSD_GN), SSD_CONV ** -0.5),
        'ssd_b_conv': nrm((N_SSD_LAYERS, SSD_INNER + 2 * SSD_GN), 0.02),
        'ssd_a_log': a_log,
        'ssd_dt_bias': dt_bias,
        'ssd_d': 1.0 + nrm((N_SSD_LAYERS, SSD_HEADS), 0.1),
        'ssd_norm_w': 1.0 + nrm((N_SSD_LAYERS, SSD_INNER), 0.05),
        'ssd_w_out': nrm((N_SSD_LAYERS, SSD_INNER, D), SSD_INNER ** -0.5),
    }


def reference(x_prompt, x_sample, cache_k, cache_v, state_ssm, c, c_ctx,
              ada_w, ada_b, norm_w, final_norm_w,
              conv_w_in, conv_b_in, conv_w_dw, conv_b_dw, conv_ln_w, conv_ln_b, conv_w_out, conv_b_out,
              na_w_in, na_rpb, na_w_out,
              ssd_w_in, ssd_w_conv, ssd_b_conv, ssd_a_log, ssd_dt_bias, ssd_d, ssd_norm_w, ssd_w_out):
    ctx = x_prompt
    lat = x_sample
    cond_ctx = c_ctx[None, :]
    new_k, new_v, new_s = [], [], []
    for i in range(DEPTH):
        kind = i % N_MIXERS
        j = i // N_MIXERS
        h_ctx, g_ctx = modulated_norm(ctx, cond_ctx, ada_w[i], ada_b[i], norm_w[i])
        h_lat, g_lat = modulated_norm(lat, c, ada_w[i], ada_b[i], norm_w[i])
        if kind == 0:
            p = (conv_w_in[j], conv_b_in[j], conv_w_dw[j], conv_b_dw[j],
                 conv_ln_w[j], conv_ln_b[j], conv_w_out[j], conv_b_out[j])
            y_ctx = conformer_conv_mixer(h_ctx, *p)
            y_lat = conformer_conv_mixer(h_lat, *p)
        elif kind == 1:
            y_ctx, k_c, v_c = na_context_mixer(h_ctx, na_w_in[j], na_w_out[j])
            new_k.append(k_c)
            new_v.append(v_c)
            y_lat = na_latent_mixer(h_lat, cache_k[:, j], cache_v[:, j], na_w_in[j], na_rpb[j], na_w_out[j])
        else:
            p = (ssd_w_in[j], ssd_w_conv[j], ssd_b_conv[j], ssd_a_log[j], ssd_dt_bias[j],
                 ssd_d[j], ssd_norm_w[j], ssd_w_out[j])
            h0 = jnp.zeros((ctx.shape[0], 2, SSD_HEADS, SSD_HEAD_DIM, SSD_STATE), ctx.dtype)
            y_ctx, s_c = ssd_mixer(h_ctx, h0, *p)
            new_s.append(s_c)
            y_lat, _ = ssd_mixer(h_lat, state_ssm[:, j], *p)
        ctx = ctx + g_ctx * y_ctx
        lat = lat + g_lat * y_lat
    y_prompt = rms_norm(ctx, final_norm_w)
    y_sample = rms_norm(lat, final_norm_w)
    new_cache_k = jnp.stack(new_k, axis=1)
    new_cache_v = jnp.stack(new_v, axis=1)
    new_state_ssm = jnp.stack(new_s, axis=1)
    return (y_prompt, y_sample, new_cache_k, new_cache_v, new_state_ssm)
```

```python
import functools

import jax
import jax.numpy as jnp
from jax import lax
from jax.experimental import pallas as pl
from jax.experimental.pallas import tpu as pltpu

F32 = jnp.float32
BF16 = jnp.bfloat16

D = 1024
DEPTH = 4
SEQ = 256
DEC_SEQ = 2048
GRID_W = 64
RMS_EPS = 1e-6
LN_EPS = 1e-5
CONF_K = 31
NA_HEADS = 16
NA_HEAD_DIM = 64
NA_KH = 8
NA_KW = 16
NEG_INF = -1e30
SSD_INNER = 2048
SSD_HEADS = 32
SSD_HEAD_DIM = 64
SSD_STATE = 128
SSD_GROUPS = 4
SSD_GN = SSD_GROUPS * SSD_STATE
SSD_CONV = 7
SSD_CHUNK = 128
SSD_XBC = SSD_INNER + 2 * SSD_GN

LANES = 128
TOKEN_TILE = 256
CONV_HALO = 16
SSD_HALO = 8
VMEM_LIMIT = 56 * 1024 * 1024


def _cparams(n_axes):
    return pltpu.CompilerParams(dimension_semantics=("arbitrary",) * n_axes,
                                vmem_limit_bytes=VMEM_LIMIT)


def _const_spec(shape):
    nd = len(shape)
    return pl.BlockSpec(shape, lambda *_: (0,) * nd, pipeline_mode=pl.Buffered(1))


def _sigmoid(x):
    return 1.0 / (1.0 + jnp.exp(-x))


def _silu(x):
    return x * _sigmoid(x)


def _rms(x, w):
    ms = jnp.mean(x * x, axis=-1, keepdims=True)
    return x * lax.rsqrt(ms + RMS_EPS) * w


def _modnorm(x, nw, mod):
    return _rms(x, nw) * (1.0 + mod[:, D:2 * D]) + mod[:, :D]


def _mod_spec(layer, row_fn):
    return pl.BlockSpec((None, None, 1, 3 * D), lambda i, *_: (layer, row_fn(i), 0, 0))


def _mod_kernel(cond_ref, w_ref, b_ref, o_ref):
    s = _silu(cond_ref[...]).astype(BF16)
    o_ref[...] = jnp.dot(s, w_ref[...].astype(BF16), preferred_element_type=F32) + b_ref[...]


def _modulation(cond8, ada_w, ada_b):
    tn = 1024
    return pl.pallas_call(
        _mod_kernel,
        grid=(DEPTH, 3 * D // tn),
        in_specs=[pl.BlockSpec((8, D), lambda i, j: (0, 0)),
                  pl.BlockSpec((None, D, tn), lambda i, j: (i, 0, j)),
                  pl.BlockSpec((None, 1, tn), lambda i, j: (i, 0, j))],
        out_specs=pl.BlockSpec((None, 8, tn), lambda i, j: (i, 0, j)),
        out_shape=jax.ShapeDtypeStruct((DEPTH, 8, 3 * D), F32),
        compiler_params=_cparams(2),
        name="adaln_mod",
    )(cond8, ada_w, ada_b.reshape(DEPTH, 1, 3 * D))


def _conv_in_kernel(x_ref, nw_ref, mod_ref, w_ref, b_ref, u_ref, sz_ref):
    hb = _modnorm(x_ref[...], nw_ref[...], mod_ref[...]).astype(BF16)
    v = jnp.dot(hb, w_ref[:, 0:D], preferred_element_type=F32) + b_ref[:, 0:D]
    g = jnp.dot(hb, w_ref[:, D:2 * D], preferred_element_type=F32) + b_ref[:, D:2 * D]
    u_ref[...] = v * _sigmoid(g)
    z = jnp.dot(hb, w_ref[:, 2 * D:3 * D], preferred_element_type=F32) + b_ref[:, 2 * D:3 * D]
    sz_ref[...] = _silu(z)


def _conv_core_kernel(tiles_per_seq, final, *refs):
    refs = list(refs)
    u_ref = refs.pop(0)
    if tiles_per_seq > 1:
        up_ref = refs.pop(0)
        un_ref = refs.pop(0)
    sz_ref, x_ref, mod_ref, wdw_ref, bdw_ref, lnw_ref, lnb_ref, wo_ref, bo_ref = refs[:9]
    refs = refs[9:]
    if final:
        fw_ref = refs.pop(0)
    o_ref, ubuf, cbuf = refs
    T = u_ref.shape[0]
    H = CONV_HALO

    ubuf[H:H + T, :] = u_ref[...]
    if tiles_per_seq > 1:
        t = pl.program_id(0) % tiles_per_seq
        ubuf[0:H, :] = jnp.where(t > 0, up_ref[...], 0.0)
        ubuf[H + T:H + T + H, :] = jnp.where(t < tiles_per_seq - 1, un_ref[...], 0.0)
    else:
        ubuf[0:H, :] = jnp.zeros((H, D), F32)
        ubuf[H + T:H + T + H, :] = jnp.zeros((H, D), F32)

    RB = 128

    def col_block(cb, carry):
        c0 = pl.multiple_of(cb * LANES, LANES)
        for rb in range(T // RB):
            acc = jnp.zeros((RB, LANES), F32)
            for k in range(CONF_K):
                r0 = rb * RB + H - CONF_K // 2 + k
                acc = acc + ubuf[pl.ds(r0, RB), pl.ds(c0, LANES)] * wdw_ref[k:k + 1, pl.ds(c0, LANES)]
            cbuf[rb * RB:(rb + 1) * RB, pl.ds(c0, LANES)] = acc + bdw_ref[:, pl.ds(c0, LANES)]
        return carry

    lax.fori_loop(0, D // LANES, col_block, 0)

    c = cbuf[...]
    mu = jnp.mean(c, axis=-1, keepdims=True)
    xc = c - mu
    var = jnp.mean(xc * xc, axis=-1, keepdims=True)
    ln = xc * lax.rsqrt(var + LN_EPS) * lnw_ref[...] + lnb_ref[...]
    g = (_silu(ln) * sz_ref[...]).astype(BF16)
    y = jnp.dot(g, wo_ref[...], preferred_element_type=F32) + bo_ref[...]
    xn = x_ref[...] + mod_ref[:, 2 * D:3 * D] * y
    if final:
        xn = _rms(xn, fw_ref[...])
    o_ref[...] = xn


def _conv_layer(x, seq_len, layer, row_fn, mod, nw, w_in, b_in, w_dw, b_dw, ln_w, ln_b, w_out, b_out, final_w):
    M = x.shape[0]
    tm = TOKEN_TILE
    row = lambda i: (i, 0)
    u, sz = pl.pallas_call(
        _conv_in_kernel,
        grid=(M // tm,),
        in_specs=[pl.BlockSpec((tm, D), row), _const_spec((1, D)), _mod_spec(layer, lambda i: row_fn(i * tm)),
                  _const_spec((D, 3 * D)), _const_spec((1, 3 * D))],
        out_specs=[pl.BlockSpec((tm, D), row), pl.BlockSpec((tm, D), row)],
        out_shape=[jax.ShapeDtypeStruct((M, D), F32)] * 2,
        compiler_params=_cparams(1),
        name="conv_in",
    )(x, nw, mod, w_in, b_in)

    T = TOKEN_TILE
    tps = seq_len // T
    H = CONV_HALO
    n_halo_blocks = M // H
    in_specs = [pl.BlockSpec((T, D), row)]
    args = [u]
    if tps > 1:
        in_specs += [pl.BlockSpec((H, D), lambda i: (jnp.maximum(i * (T // H) - 1, 0), 0)),
                     pl.BlockSpec((H, D), lambda i: (jnp.minimum((i + 1) * (T // H), n_halo_blocks - 1), 0))]
        args += [u, u]
    in_specs += [pl.BlockSpec((T, D), row), pl.BlockSpec((T, D), row), _mod_spec(layer, lambda i: row_fn(i * T)),
                 _const_spec((CONF_K + 1, D)), _const_spec((1, D)), _const_spec((1, D)), _const_spec((1, D)),
                 _const_spec((D, D)), _const_spec((1, D))]
    args += [sz, x, mod, w_dw, b_dw, ln_w, ln_b, w_out, b_out]
    final = final_w is not None
    if final:
        in_specs.append(_const_spec((1, D)))
        args.append(final_w)
    return pl.pallas_call(
        functools.partial(_conv_core_kernel, tps, final),
        grid=(M // T,),
        in_specs=in_specs,
        out_specs=pl.BlockSpec((T, D), row),
        out_shape=jax.ShapeDtypeStruct((M, D), F32),
        scratch_shapes=[pltpu.VMEM((T + 2 * H, D), F32), pltpu.VMEM((T, D), F32)],
        compiler_params=_cparams(1),
        name="conv_core",
    )(*args)


def _prep_conv(p, j):
    w_dw = jnp.pad(p["conv_w_dw"][j], ((0, 1), (0, 0)))
    return (p["conv_w_in"][j].astype(BF16), p["conv_b_in"][j][None], w_dw, p["conv_b_dw"][j][None],
            p["conv_ln_w"][j][None], p["conv_ln_b"][j][None], p["conv_w_out"][j].astype(BF16), p["conv_b_out"][j][None])


def _out_proj_kernel(g_ref, x_ref, mod_ref, w_ref, o_ref):
    y = jnp.dot(g_ref[...].astype(BF16), w_ref[...], preferred_element_type=F32)
    o_ref[...] = x_ref[...] + mod_ref[:, 2 * D:3 * D] * y


def _out_proj(g, x, layer, row_fn, mod, w_out):
    M, K = g.shape
    tm = TOKEN_TILE
    row = lambda i: (i, 0)
    return pl.pallas_call(
        _out_proj_kernel,
        grid=(M // tm,),
        in_specs=[pl.BlockSpec((tm, K), row), pl.BlockSpec((tm, D), row), _mod_spec(layer, lambda i: row_fn(i * tm)),
                  _const_spec((K, D))],
        out_specs=pl.BlockSpec((tm, D), row),
        out_shape=jax.ShapeDtypeStruct((M, D), F32),
        compiler_params=_cparams(1),
        name="out_proj",
    )(g, x, mod, w_out)


NA_SCALE = NA_HEAD_DIM ** -0.5
NA_ROWS = 32
NA_WIN = NA_KH * GRID_W
NT_DIMS = (((1,), (1,)), ((), ()))


def _na_in_kernel(x_ref, nw_ref, mod_ref, w_ref, q_ref, k_ref, v_ref, sz_ref):
    hb = _modnorm(x_ref[...], nw_ref[...], mod_ref[...]).astype(BF16)
    q_ref[...] = jnp.dot(hb, w_ref[:, 0:D], preferred_element_type=F32).astype(q_ref.dtype)
    k_ref[...] = jnp.dot(hb, w_ref[:, D:2 * D], preferred_element_type=F32).astype(k_ref.dtype)
    v_ref[...] = jnp.dot(hb, w_ref[:, 2 * D:3 * D], preferred_element_type=F32).astype(v_ref.dtype)
    sz_ref[...] = _silu(jnp.dot(hb, w_ref[:, 3 * D:4 * D], preferred_element_type=F32))


def _na_in(x, layer, row_fn, mod, nw, w_in, kv_dtype):
    M = x.shape[0]
    tm = TOKEN_TILE
    row = lambda i: (i, 0)
    blk = pl.BlockSpec((tm, D), row)
    return pl.pallas_call(
        _na_in_kernel,
        grid=(M // tm,),
        in_specs=[blk, _const_spec((1, D)), _mod_spec(layer, lambda i: row_fn(i * tm)), _const_spec((D, 4 * D))],
        out_specs=[blk, blk, blk, blk],
        out_shape=[jax.ShapeDtypeStruct((M, D), BF16), jax.ShapeDtypeStruct((M, D), kv_dtype),
                   jax.ShapeDtypeStruct((M, D), kv_dtype), jax.ShapeDtypeStruct((M, D), F32)],
        compiler_params=_cparams(1),
        name="na_in",
    )(x, nw, mod, w_in)


def _head_masks():
    lane = lax.broadcasted_iota(jnp.int32, (1, LANES), 1)
    return lane < NA_HEAD_DIM


def _na_ctx_kernel(q_ref, k_ref, v_ref, sz_ref, o_ref):
    lo = _head_masks()
    for p in range(NA_HEADS // 2):
        cols = slice(p * LANES, (p + 1) * LANES)
        qp = q_ref[:, cols]
        kp = k_ref[:, cols].astype(BF16)
        vp = v_ref[:, cols].astype(BF16)
        outs = []
        for hh in range(2):
            qm = jnp.where(lo if hh == 0 else jnp.logical_not(lo), qp, jnp.zeros_like(qp))
            s = lax.dot_general(qm, kp, NT_DIMS, preferred_element_type=F32) * NA_SCALE
            e = jnp.exp(s - jnp.max(s, axis=-1, keepdims=True))
            l = jnp.sum(e, axis=-1, keepdims=True)
            outs.append(jnp.dot(e.astype(BF16), vp, preferred_element_type=F32) / l)
        o_ref[:, cols] = jnp.where(lo, outs[0], outs[1]) * sz_ref[:, cols]


def _na_ctx(q, k, v, sz, seq_len):
    M = q.shape[0]
    blk = pl.BlockSpec((seq_len, D), lambda i: (i, 0))
    return pl.pallas_call(
        _na_ctx_kernel,
        grid=(M // seq_len,),
        in_specs=[blk, blk, blk, blk],
        out_specs=blk,
        out_shape=jax.ShapeDtypeStruct((M, D), F32),
        compiler_params=_cparams(1),
        name="na_ctx_attn",
    )(q, k, v, sz)


def _rpb_kernel(rpb_ref, o_ref):
    idx = pl.program_id(0)
    h = pl.program_id(1)
    c = lax.broadcasted_iota(jnp.int32, (GRID_W, LANES), 0)
    lane = lax.broadcasted_iota(jnp.int32, (GRID_W, LANES), 1)
    kc = lane & (GRID_W - 1)
    upper = lax.broadcasted_iota(jnp.int32, (1, LANES), 1) >= GRID_W
    cs = jnp.clip(c - NA_KW // 2, 0, GRID_W - NA_KW)
    valid = (kc >= cs) & (kc < cs + NA_KW)
    d = kc - c
    n_dc = 2 * NA_KW - 1
    tiles = [jnp.zeros((GRID_W, LANES), F32) for _ in range(NA_KH // 2)]
    for dd in range(-(NA_KW - 1), NA_KW):
        hit = d == dd
        for jj in range(NA_KH // 2):
            base = h * ((2 * NA_KH - 1) * n_dc) + (2 * jj + NA_KH - 1 - idx) * n_dc + (dd + NA_KW - 1)
            val = jnp.where(upper, rpb_ref[base + n_dc], rpb_ref[base])
            tiles[jj] = jnp.where(hit, val, tiles[jj])
    for jj in range(NA_KH // 2):
        o_ref[:, jj * LANES:(jj + 1) * LANES] = jnp.where(valid, tiles[jj], NEG_INF)


def _rpb_table(rpb):
    return pl.pallas_call(
        _rpb_kernel,
        grid=(NA_KH, NA_HEADS),
        in_specs=[pl.BlockSpec(memory_space=pltpu.SMEM)],
        out_specs=pl.BlockSpec((None, None, GRID_W, NA_WIN), lambda i, h: (i, h, 0, 0)),
        out_shape=jax.ShapeDtypeStruct((NA_KH, NA_HEADS, GRID_W, NA_WIN), F32),
        compiler_params=_cparams(2),
        name="na_rpb_table",
    )(rpb.reshape(-1))


def _na_row_start(r):
    return jnp.clip(r - NA_KH // 2, 0, NA_ROWS - NA_KH)


def _na_lat_kernel(q_ref, k_ref, v_ref, ck_ref, cv_ref, bias_ref, sz_ref, o_ref):
    lo = _head_masks()
    k0 = pl.multiple_of(_na_row_start(pl.program_id(1)) * GRID_W, GRID_W)
    for p in range(NA_HEADS // 2):
        cols = slice(p * LANES, (p + 1) * LANES)
        qp = q_ref[:, cols]
        kw = k_ref[pl.ds(k0, NA_WIN), cols]
        vw = v_ref[pl.ds(k0, NA_WIN), cols]
        ck = ck_ref[:, cols]
        cv = cv_ref[:, cols]
        outs = []
        for hh in range(2):
            qm = jnp.where(lo if hh == 0 else jnp.logical_not(lo), qp, jnp.zeros_like(qp))
            s_loc = lax.dot_general(qm, kw, NT_DIMS, preferred_element_type=F32) * NA_SCALE + bias_ref[2 * p + hh]
            s_ctx = lax.dot_general(qm, ck, NT_DIMS, preferred_element_type=F32) * NA_SCALE
            m = jnp.maximum(jnp.max(s_loc, axis=-1, keepdims=True), jnp.max(s_ctx, axis=-1, keepdims=True))
            e_loc = jnp.exp(s_loc - m)
            e_ctx = jnp.exp(s_ctx - m)
            l = jnp.sum(e_loc, axis=-1, keepdims=True) + jnp.sum(e_ctx, axis=-1, keepdims=True)
            o = (jnp.dot(e_loc.astype(BF16), vw, preferred_element_type=F32)
                 + jnp.dot(e_ctx.astype(BF16), cv, preferred_element_type=F32))
            outs.append(o / l)
        o_ref[:, cols] = jnp.where(lo, outs[0], outs[1]) * sz_ref[:, cols]


def _na_lat(q, k, v, ck, cv, bias, sz, n_lat):
    T = NA_ROWS * GRID_W
    past = ck.shape[1]
    qblk = pl.BlockSpec((GRID_W, D), lambda b, r: (b * NA_ROWS + r, 0))
    kvblk = pl.BlockSpec((T, D), lambda b, r: (b, 0))
    cblk = pl.BlockSpec((None, past, D), lambda b, r: (b, 0, 0))
    bblk = pl.BlockSpec((None, NA_HEADS, GRID_W, NA_WIN), lambda b, r: (r - _na_row_start(r), 0, 0, 0))
    return pl.pallas_call(
        _na_lat_kernel,
        grid=(n_lat, NA_ROWS),
        in_specs=[qblk, kvblk, kvblk, cblk, cblk, bblk, qblk],
        out_specs=qblk,
        out_shape=jax.ShapeDtypeStruct((n_lat * T, D), F32),
        compiler_params=_cparams(2),
        name="na_lat_attn",
    )(q, k, v, ck, cv, bias, sz)


def _na_layer(xc, xl, layer, mod, nw, p, n_lat):
    j = layer // 3
    w_in = p["na_w_in"][j].astype(BF16)
    w_out = p["na_w_out"][j].astype(BF16)
    row_ctx = lambda tok: 0
    row_lat = lambda tok: 1 + tok // (NA_ROWS * GRID_W)
    qc, kc, vc, szc = _na_in(xc, layer, row_ctx, mod, nw, w_in, F32)
    ql, kl, vl, szl = _na_in(xl, layer, row_lat, mod, nw, w_in, BF16)
    gc = _na_ctx(qc, kc, vc, szc, SEQ)
    past = p["cache_k"].shape[2]
    ck = p["cache_k"][:n_lat, j].reshape(n_lat, past, D).astype(BF16)
    cv = p["cache_v"][:n_lat, j].reshape(n_lat, past, D).astype(BF16)
    gl = _na_lat(ql, kl, vl, ck, cv, _rpb_table(p["na_rpb"][j]), szl, n_lat)
    return (_out_proj(gc, xc, layer, row_ctx, mod, w_out), _out_proj(gl, xl, layer, row_lat, mod, w_out), kc, vc)


SSD_IN_COLS = SSD_INNER + SSD_XBC + LANES


def _softplus(x):
    return jnp.maximum(x, 0.0) + jnp.log(1.0 + jnp.exp(-jnp.abs(x)))


def _ssd_in_kernel(x_ref, nw_ref, mod_ref, w_ref, dtb_ref, sz_ref, xbc_ref, dt_ref):
    hb = _modnorm(x_ref[...], nw_ref[...], mod_ref[...]).astype(BF16)
    for j in range(SSD_INNER // D):
        sz_ref[:, j * D:(j + 1) * D] = _silu(jnp.dot(hb, w_ref[:, j * D:(j + 1) * D], preferred_element_type=F32))
    for j in range(SSD_XBC // D):
        c0 = SSD_INNER + j * D
        xbc_ref[:, j * D:(j + 1) * D] = jnp.dot(hb, w_ref[:, c0:c0 + D], preferred_element_type=F32)
    c0 = SSD_INNER + SSD_XBC
    raw = jnp.dot(hb, w_ref[:, c0:c0 + LANES], preferred_element_type=F32) + dtb_ref[...]
    lane = lax.broadcasted_iota(jnp.int32, (1, LANES), 1)
    dt_ref[...] = jnp.where(lane < 2 * SSD_HEADS, _softplus(raw), 0.0)


def _ssd_in(x, layer, row_fn, mod, nw, w_in, dt_bias):
    M = x.shape[0]
    tm = TOKEN_TILE
    row = lambda i: (i, 0)
    return pl.pallas_call(
        _ssd_in_kernel,
        grid=(M // tm,),
        in_specs=[pl.BlockSpec((tm, D), row), _const_spec((1, D)), _mod_spec(layer, lambda i: row_fn(i * tm)),
                  _const_spec((D, SSD_IN_COLS)), _const_spec((1, LANES))],
        out_specs=[pl.BlockSpec((tm, SSD_INNER), row), pl.BlockSpec((tm, SSD_XBC), row), pl.BlockSpec((tm, LANES), row)],
        out_shape=[jax.ShapeDtypeStruct((M, SSD_INNER), F32), jax.ShapeDtypeStruct((M, SSD_XBC), F32),
                   jax.ShapeDtypeStruct((M, LANES), F32)],
        compiler_params=_cparams(1),
        name="ssd_in",
    )(x, nw, mod, w_in, dt_bias)


def _cumsum_rows(a, reverse):
    n = a.shape[0]
    row = lax.broadcasted_iota(jnp.int32, a.shape, 0)
    k = 1
    while k < n:
        if reverse:
            a = a + jnp.where(row < n - k, pltpu.roll(a, n - k, 0), 0.0)
        else:
            a = a + jnp.where(row >= k, pltpu.roll(a, k, 0), 0.0)
        k *= 2
    return a


def _ssd_core_kernel(nc, has_h0, emit_state, *refs):
    refs = list(refs)
    dir_refs = [refs[0:4], refs[4:8]]
    wc_ref, bc_ref, alog_ref, dsk_ref = refs[8:12]
    refs = refs[12:]
    if has_h0:
        h0_ref = refs.pop(0)
    y_refs = [refs.pop(0), refs.pop(0)]
    if emit_state:
        so_ref = refs.pop(0)
    st_ref, xpad_ref = refs
    Q, N, P, HL = SSD_CHUNK, SSD_STATE, SSD_HEAD_DIM, SSD_HALO
    c = pl.program_id(1)
    n_blk = SSD_INNER // LANES

    @pl.when(c == 0)
    def _init():
        if has_h0:
            for dr in range(2):
                for j in range(n_blk):
                    st_ref[dr, :, j * LANES:(j + 1) * LANES] = h0_ref[dr, j * LANES:(j + 1) * LANES, :].T
        else:
            st_ref[...] = jnp.zeros(st_ref.shape, F32)

    lo = _head_masks()
    A = -jnp.exp(alog_ref[...])
    ri = lax.broadcasted_iota(jnp.int32, (Q, Q), 0)
    ci = lax.broadcasted_iota(jnp.int32, (Q, Q), 1)

    for dr in range(2):
        x_ref, xp_ref, xn_ref, dt_ref = dir_refs[dr]
        y_ref = y_refs[dr]
        cc = c if dr == 0 else nc - 1 - c
        xpad_ref[dr, HL:HL + Q, :] = x_ref[...]
        xpad_ref[dr, 0:HL, :] = jnp.where(cc > 0, xp_ref[...], 0.0)
        xpad_ref[dr, HL + Q:HL + Q + HL, :] = jnp.where(cc < nc - 1, xn_ref[...], 0.0)

        def conv_cols(c0, dr=dr):
            acc = jnp.zeros((Q, LANES), F32)
            for k in range(SSD_CONV):
                r0 = HL - SSD_CONV // 2 + k
                acc = acc + xpad_ref[dr, r0:r0 + Q, c0:c0 + LANES] * wc_ref[k:k + 1, c0:c0 + LANES]
            return _silu(acc + bc_ref[:, c0:c0 + LANES])

        dt = dt_ref[...]
        cum = _cumsum_rows(dt * A, reverse=(dr == 1))
        cum_t = cum.T
        tot_row = Q - 1 if dr == 0 else 0
        mask = (ri >= ci) if dr == 0 else (ri <= ci)

        for g in range(SSD_GROUPS):
            bg_t = conv_cols(SSD_INNER + g * N).T.astype(BF16)
            cg = conv_cols(SSD_INNER + SSD_GN + g * N).astype(BF16)
            cb = jnp.dot(cg, bg_t, preferred_element_type=F32)
            for pp in range(SSD_HEADS // SSD_GROUPS // 2):
                head = g * (SSD_HEADS // SSD_GROUPS) + 2 * pp
                c0 = head * P
                li = dr * SSD_HEADS + head
                xpair = conv_cols(c0)
                dt_cols = [dt[:, li:li + 1], dt[:, li + 1:li + 2]]
                cum_cols = [cum[:, li:li + 1], cum[:, li + 1:li + 2]]
                xdt = xpair * jnp.where(lo, dt_cols[0], dt_cols[1])
                xdt_b = xdt.astype(BF16)
                cum_x = jnp.where(lo, cum_cols[0], cum_cols[1])
                yd = []
                for hh in range(2):
                    seg = cum_cols[hh] - cum_t[li + hh:li + hh + 1, :]
                    decay = jnp.exp(jnp.where(mask, seg, -jnp.inf))
                    yd.append(jnp.dot((cb * decay).astype(BF16), xdt_b, preferred_element_type=F32))
                st = st_ref[dr, :, c0:c0 + LANES]
                y = (jnp.where(lo, yd[0], yd[1])
                     + jnp.dot(cg, st.astype(BF16), preferred_element_type=F32) * jnp.exp(cum_x))
                if dr == 0:
                    y = y + xpair * dsk_ref[:, c0:c0 + LANES]
                y_ref[:, c0:c0 + LANES] = y
                tot = cum_x[tot_row:tot_row + 1, :]
                xw = (xdt * jnp.exp(tot - cum_x)).astype(BF16)
                st_ref[dr, :, c0:c0 + LANES] = st * jnp.exp(tot) + jnp.dot(bg_t, xw, preferred_element_type=F32)

    if emit_state:
        @pl.when(c == nc - 1)
        def _emit():
            for dr in range(2):
                for j in range(n_blk):
                    so_ref[dr, j * LANES:(j + 1) * LANES, :] = st_ref[dr, :, j * LANES:(j + 1) * LANES].T


def _ssd_core(xbc, dtp, w_conv, b_conv, a_log, d_skip, h0, n_seq, seq_len, emit_state):
    Q, HL = SSD_CHUNK, SSD_HALO
    nc = seq_len // Q
    M = n_seq * seq_len
    hpc = Q // HL
    n_hblk = M // HL

    def chunk(dr):
        return (lambda b, c: b * nc + c) if dr == 0 else (lambda b, c: b * nc + nc - 1 - c)

    in_specs, args = [], []
    for dr in range(2):
        ch = chunk(dr)
        in_specs += [pl.BlockSpec((Q, SSD_XBC), lambda b, c, ch=ch: (ch(b, c), 0)),
                     pl.BlockSpec((HL, SSD_XBC), lambda b, c, ch=ch: (jnp.maximum(ch(b, c) * hpc - 1, 0), 0)),
                     pl.BlockSpec((HL, SSD_XBC), lambda b, c, ch=ch: (jnp.minimum((ch(b, c) + 1) * hpc, n_hblk - 1), 0)),
                     pl.BlockSpec((Q, LANES), lambda b, c, ch=ch: (ch(b, c), 0))]
        args += [xbc, xbc, xbc, dtp]
    in_specs += [_const_spec((SSD_CONV + 1, SSD_XBC)), _const_spec((1, SSD_XBC)), _const_spec((1, LANES)),
                 _const_spec((1, SSD_INNER))]
    args += [w_conv, b_conv, a_log, d_skip]
    st_blk = pl.BlockSpec((None, 2, SSD_INNER, SSD_STATE), lambda b, c: (b, 0, 0, 0))
    has_h0 = h0 is not None
    if has_h0:
        in_specs.append(st_blk)
        args.append(h0)
    out_specs = [pl.BlockSpec((Q, SSD_INNER), lambda b, c: (chunk(0)(b, c), 0)),
                 pl.BlockSpec((Q, SSD_INNER), lambda b, c: (chunk(1)(b, c), 0))]
    out_shape = [jax.ShapeDtypeStruct((M, SSD_INNER), F32)] * 2
    if emit_state:
        out_specs.append(st_blk)
        out_shape.append(jax.ShapeDtypeStruct((n_seq, 2, SSD_INNER, SSD_STATE), F32))
    return pl.pallas_call(
        functools.partial(_ssd_core_kernel, nc, has_h0, emit_state),
        grid=(n_seq, nc),
        in_specs=in_specs,
        out_specs=out_specs,
        out_shape=out_shape,
        scratch_shapes=[pltpu.VMEM((2, SSD_STATE, SSD_INNER), F32), pltpu.VMEM((2, Q + 2 * HL, SSD_XBC), F32)],
        compiler_params=_cparams(2),
        name="ssd_core",
    )(*args)


def _ssd_out_kernel(yf_ref, yb_ref, sz_ref, x_ref, mod_ref, nw_ref, w_ref, o_ref):
    g = _rms((yf_ref[...] + yb_ref[...]) * sz_ref[...], nw_ref[...]).astype(BF16)
    o_ref[...] = x_ref[...] + mod_ref[:, 2 * D:3 * D] * jnp.dot(g, w_ref[...], preferred_element_type=F32)


def _ssd_out(yf, yb, sz, x, layer, row_fn, mod, norm_w, w_out):
    M = x.shape[0]
    tm = TOKEN_TILE
    row = lambda i: (i, 0)
    wide = pl.BlockSpec((tm, SSD_INNER), row)
    return pl.pallas_call(
        _ssd_out_kernel,
        grid=(M // tm,),
        in_specs=[wide, wide, wide, pl.BlockSpec((tm, D), row), _mod_spec(layer, lambda i: row_fn(i * tm)),
                  _const_spec((1, SSD_INNER)), _const_spec((SSD_INNER, D))],
        out_specs=pl.BlockSpec((tm, D), row),
        out_shape=jax.ShapeDtypeStruct((M, D), F32),
        compiler_params=_cparams(1),
        name="ssd_out",
    )(yf, yb, sz, x, mod, norm_w, w_out)


def _ssd_layer(xc, xl, layer, mod, nw, p, n_lat):
    j = layer // 3
    w_in = jnp.pad(p["ssd_w_in"][j].astype(BF16), ((0, 0), (0, LANES - 2 * SSD_HEADS)))
    dt_bias = jnp.pad(p["ssd_dt_bias"][j].reshape(1, -1), ((0, 0), (0, LANES - 2 * SSD_HEADS)))
    a_log = jnp.pad(p["ssd_a_log"][j].reshape(1, -1), ((0, 0), (0, LANES - 2 * SSD_HEADS)))
    w_conv = jnp.pad(p["ssd_w_conv"][j], ((0, 1), (0, 0)))
    b_conv = p["ssd_b_conv"][j][None]
    d_skip = jnp.repeat(p["ssd_d"][j], SSD_HEAD_DIM)[None]
    norm_w = p["ssd_norm_w"][j][None]
    w_out = p["ssd_w_out"][j].astype(BF16)
    row_ctx = lambda tok: 0
    row_lat = lambda tok: 1 + tok // DEC_SEQ
    n_ctx = xc.shape[0] // SEQ
    szc, xbc_c, dt_c = _ssd_in(xc, layer, row_ctx, mod, nw, w_in, dt_bias)
    szl, xbc_l, dt_l = _ssd_in(xl, layer, row_lat, mod, nw, w_in, dt_bias)
    yfc, ybc, s_new = _ssd_core(xbc_c, dt_c, w_conv, b_conv, a_log, d_skip, None, n_ctx, SEQ, True)
    h0 = p["state_ssm"][:n_lat, j].reshape(n_lat, 2, SSD_INNER, SSD_STATE)
    yfl, ybl = _ssd_core(xbc_l, dt_l, w_conv, b_conv, a_log, d_skip, h0, n_lat, DEC_SEQ, False)
    oc = _ssd_out(yfc, ybc, szc, xc, layer, row_ctx, mod, norm_w, w_out)
    ol = _ssd_out(yfl, ybl, szl, xl, layer, row_lat, mod, norm_w, w_out)
    return oc, ol, s_new


def kernel(x_prompt, x_sample, cache_k, cache_v, state_ssm, c, c_ctx, ada_w, ada_b, norm_w, final_norm_w, conv_w_in, conv_b_in, conv_w_dw, conv_b_dw, conv_ln_w, conv_ln_b, conv_w_out, conv_b_out, na_w_in, na_rpb, na_w_out, ssd_w_in, ssd_w_conv, ssd_b_conv, ssd_a_log, ssd_dt_bias, ssd_d, ssd_norm_w, ssd_w_out):
    p = dict(cache_k=cache_k, cache_v=cache_v, state_ssm=state_ssm,
             conv_w_in=conv_w_in, conv_b_in=conv_b_in, conv_w_dw=conv_w_dw, conv_b_dw=conv_b_dw,
             conv_ln_w=conv_ln_w, conv_ln_b=conv_ln_b, conv_w_out=conv_w_out, conv_b_out=conv_b_out,
             na_w_in=na_w_in, na_rpb=na_rpb, na_w_out=na_w_out,
             ssd_w_in=ssd_w_in, ssd_w_conv=ssd_w_conv, ssd_b_conv=ssd_b_conv, ssd_a_log=ssd_a_log,
             ssd_dt_bias=ssd_dt_bias, ssd_d=ssd_d, ssd_norm_w=ssd_norm_w, ssd_w_out=ssd_w_out)
    n_ctx, n_lat = x_prompt.shape[0], x_sample.shape[0]
    assert x_prompt.shape[1:] == (SEQ, D) and x_sample.shape[1:] == (DEC_SEQ, D)
    assert (DEPTH - 1) % 3 == 0, "the final RMSNorm is fused into a trailing Conformer layer"
    xc = x_prompt.reshape(n_ctx * SEQ, D)
    xl = x_sample.reshape(n_lat * DEC_SEQ, D)
    cond8 = jnp.concatenate([c_ctx[None], c, jnp.zeros((8 - 1 - n_lat, D), F32)], axis=0)
    mod = _modulation(cond8, ada_w, ada_b).reshape(DEPTH, 8, 1, 3 * D)
    row_ctx = lambda tok: 0
    row_lat = lambda tok: 1 + tok // DEC_SEQ
    new_k, new_v, new_s = [], [], []
    for i in range(DEPTH):
        kind, j = i % 3, i // 3
        nw = norm_w[i][None]
        if kind == 0:
            cp = _prep_conv(p, j)
            fw = final_norm_w[None] if i == DEPTH - 1 else None
            xc = _conv_layer(xc, SEQ, i, row_ctx, mod, nw, *cp, fw)
            xl = _conv_layer(xl, DEC_SEQ, i, row_lat, mod, nw, *cp, fw)
        elif kind == 1:
            xc, xl, k_c, v_c = _na_layer(xc, xl, i, mod, nw, p, n_lat)
            new_k.append(k_c.reshape(n_ctx, SEQ, NA_HEADS, NA_HEAD_DIM))
            new_v.append(v_c.reshape(n_ctx, SEQ, NA_HEADS, NA_HEAD_DIM))
        else:
            xc, xl, s_c = _ssd_layer(xc, xl, i, mod, nw, p, n_lat)
            new_s.append(s_c.reshape(n_ctx, 2, SSD_HEADS, SSD_HEAD_DIM, SSD_STATE))
    return (xc.reshape(n_ctx, SEQ, D), xl.reshape(n_lat, DEC_SEQ, D),
            jnp.stack(new_k, axis=1), jnp.stack(new_v, axis=1), jnp.stack(new_s, axis=1))
```

```python
import functools

import jax
import jax.numpy as jnp
from jax import lax
from jax.experimental import pallas as pl
from jax.experimental.pallas import tpu as pltpu

F32 = jnp.float32
BF16 = jnp.bfloat16

D = 1024
DEPTH = 4
SEQ = 256
DEC_SEQ = 2048
GRID_W = 64
RMS_EPS = 1e-6
LN_EPS = 1e-5
CONF_K = 31
NA_HEADS = 16
NA_HEAD_DIM = 64
NA_KH = 8
NA_KW = 16
NEG_INF = -1e30
SSD_INNER = 2048
SSD_HEADS = 32
SSD_HEAD_DIM = 64
SSD_STATE = 128
SSD_GROUPS = 4
SSD_GN = SSD_GROUPS * SSD_STATE
SSD_CONV = 7
SSD_CHUNK = 128
SSD_XBC = SSD_INNER + 2 * SSD_GN

LANES = 128
SUBLANES = 8
TOKEN_TILE = 256
CONV_HALO = 16
SSD_HALO = 8
VMEM_LIMIT = 56 * 1024 * 1024


def _cparams(n_axes):
    return pltpu.CompilerParams(dimension_semantics=("arbitrary",) * n_axes,
                                vmem_limit_bytes=VMEM_LIMIT)


def _const_spec(shape):
    nd = len(shape)
    return pl.BlockSpec(shape, lambda *_: (0,) * nd, pipeline_mode=pl.Buffered(1))


def _sigmoid(x):
    return 1.0 / (1.0 + jnp.exp(-x))


def _silu(x):
    return x * _sigmoid(x)


def _rms(x, w):
    ms = jnp.mean(x * x, axis=-1, keepdims=True)
    return x * lax.rsqrt(ms + RMS_EPS) * w


def _modnorm(x, nw, mod):
    return _rms(x, nw) * (1.0 + mod[:, D:2 * D]) + mod[:, :D]


def _mod_spec(layer, row_fn):
    return pl.BlockSpec((None, None, 1, 3 * D), lambda i, *_: (layer, row_fn(i), 0, 0))


def _mod_kernel(cond_ref, w_ref, b_ref, o_ref):
    s = _silu(cond_ref[...]).astype(BF16)
    o_ref[...] = jnp.dot(s, w_ref[...].astype(BF16), preferred_element_type=F32) + b_ref[...]


def _modulation(cond8, ada_w, ada_b):
    tn = 1024
    return pl.pallas_call(
        _mod_kernel,
        grid=(DEPTH, 3 * D // tn),
        in_specs=[pl.BlockSpec((8, D), lambda i, j: (0, 0)),
                  pl.BlockSpec((None, D, tn), lambda i, j: (i, 0, j)),
                  pl.BlockSpec((None, 1, tn), lambda i, j: (i, 0, j))],
        out_specs=pl.BlockSpec((None, 8, tn), lambda i, j: (i, 0, j)),
        out_shape=jax.ShapeDtypeStruct((DEPTH, 8, 3 * D), F32),
        compiler_params=_cparams(2),
        name="adaln_mod",
    )(cond8, ada_w, ada_b.reshape(DEPTH, 1, 3 * D))


def _conv_in_kernel(x_ref, nw_ref, mod_ref, w_ref, b_ref, u_ref, sz_ref):
    hb = _modnorm(x_ref[...], nw_ref[...], mod_ref[...]).astype(BF16)
    v = jnp.dot(hb, w_ref[:, 0:D], preferred_element_type=F32) + b_ref[:, 0:D]
    g = jnp.dot(hb, w_ref[:, D:2 * D], preferred_element_type=F32) + b_ref[:, D:2 * D]
    u_ref[...] = v * _sigmoid(g)
    z = jnp.dot(hb, w_ref[:, 2 * D:3 * D], preferred_element_type=F32) + b_ref[:, 2 * D:3 * D]
    sz_ref[...] = _silu(z)


def _conv_core_kernel(tiles_per_seq, final, *refs):
    refs = list(refs)
    u_ref = refs.pop(0)
    if tiles_per_seq > 1:
        up_ref = refs.pop(0)
        un_ref = refs.pop(0)
    sz_ref, x_ref, mod_ref, wdw_ref, bdw_ref, lnw_ref, lnb_ref, wo_ref, bo_ref = refs[:9]
    refs = refs[9:]
    if final:
        fw_ref = refs.pop(0)
    o_ref, ubuf, cbuf = refs
    T = u_ref.shape[0]
    H = CONV_HALO

    ubuf[H:H + T, :] = u_ref[...]
    if tiles_per_seq > 1:
        t = pl.program_id(0) % tiles_per_seq
        ubuf[0:H, :] = jnp.where(t > 0, up_ref[...], 0.0)
        ubuf[H + T:H + T + H, :] = jnp.where(t < tiles_per_seq - 1, un_ref[...], 0.0)
    else:
        ubuf[0:H, :] = jnp.zeros((H, D), F32)
        ubuf[H + T:H + T + H, :] = jnp.zeros((H, D), F32)

    RB = 128

    def col_block(cb, carry):
        c0 = pl.multiple_of(cb * LANES, LANES)
        for rb in range(T // RB):
            acc = jnp.zeros((RB, LANES), F32)
            span = RB + SUBLANES * ((CONF_K - 1) // SUBLANES)
            for s in range(SUBLANES):
                v = ubuf[pl.ds(rb * RB + s, span), pl.ds(c0, LANES)]
                for q in range(span // SUBLANES - RB // SUBLANES + 1):
                    k = SUBLANES * q + s - (H - CONF_K // 2)
                    if 0 <= k < CONF_K:
                        acc = acc + v[SUBLANES * q:SUBLANES * q + RB] * wdw_ref[k:k + 1, pl.ds(c0, LANES)]
            cbuf[rb * RB:(rb + 1) * RB, pl.ds(c0, LANES)] = acc + bdw_ref[:, pl.ds(c0, LANES)]
        return carry

    lax.fori_loop(0, D // LANES, col_block, 0)

    c = cbuf[...]
    mu = jnp.mean(c, axis=-1, keepdims=True)
    xc = c - mu
    var = jnp.mean(xc * xc, axis=-1, keepdims=True)
    ln = xc * lax.rsqrt(var + LN_EPS) * lnw_ref[...] + lnb_ref[...]
    g = (_silu(ln) * sz_ref[...]).astype(BF16)
    y = jnp.dot(g, wo_ref[...], preferred_element_type=F32) + bo_ref[...]
    xn = x_ref[...] + mod_ref[:, 2 * D:3 * D] * y
    if final:
        xn = _rms(xn, fw_ref[...])
    o_ref[...] = xn


def _conv_layer(x, seq_len, layer, row_fn, mod, nw, w_in, b_in, w_dw, b_dw, ln_w, ln_b, w_out, b_out, final_w):
    M = x.shape[0]
    tm = TOKEN_TILE
    row = lambda i: (i, 0)
    u, sz = pl.pallas_call(
        _conv_in_kernel,
        grid=(M // tm,),
        in_specs=[pl.BlockSpec((tm, D), row), _const_spec((1, D)), _mod_spec(layer, lambda i: row_fn(i * tm)),
                  _const_spec((D, 3 * D)), _const_spec((1, 3 * D))],
        out_specs=[pl.BlockSpec((tm, D), row), pl.BlockSpec((tm, D), row)],
        out_shape=[jax.ShapeDtypeStruct((M, D), F32)] * 2,
        compiler_params=_cparams(1),
        name="conv_in",
    )(x, nw, mod, w_in, b_in)

    T = TOKEN_TILE
    tps = seq_len // T
    H = CONV_HALO
    n_halo_blocks = M // H
    in_specs = [pl.BlockSpec((T, D), row)]
    args = [u]
    if tps > 1:
        in_specs += [pl.BlockSpec((H, D), lambda i: (jnp.maximum(i * (T // H) - 1, 0), 0)),
                     pl.BlockSpec((H, D), lambda i: (jnp.minimum((i + 1) * (T // H), n_halo_blocks - 1), 0))]
        args += [u, u]
    in_specs += [pl.BlockSpec((T, D), row), pl.BlockSpec((T, D), row), _mod_spec(layer, lambda i: row_fn(i * T)),
                 _const_spec((CONF_K + 1, D)), _const_spec((1, D)), _const_spec((1, D)), _const_spec((1, D)),
                 _const_spec((D, D)), _const_spec((1, D))]
    args += [sz, x, mod, w_dw, b_dw, ln_w, ln_b, w_out, b_out]
    final = final_w is not None
    if final:
        in_specs.append(_const_spec((1, D)))
        args.append(final_w)
    return pl.pallas_call(
        functools.partial(_conv_core_kernel, tps, final),
        grid=(M // T,),
        in_specs=in_specs,
        out_specs=pl.BlockSpec((T, D), row),
        out_shape=jax.ShapeDtypeStruct((M, D), F32),
        scratch_shapes=[pltpu.VMEM((T + 2 * H, D), F32), pltpu.VMEM((T, D), F32)],
        compiler_params=_cparams(1),
        name="conv_core",
    )(*args)


def _prep_conv(p, j):
    w_dw = jnp.pad(p["conv_w_dw"][j], ((0, 1), (0, 0)))
    return (p["conv_w_in"][j].astype(BF16), p["conv_b_in"][j][None], w_dw, p["conv_b_dw"][j][None],
            p["conv_ln_w"][j][None], p["conv_ln_b"][j][None], p["conv_w_out"][j].astype(BF16), p["conv_b_out"][j][None])


def _out_proj_kernel(g_ref, x_ref, mod_ref, w_ref, o_ref):
    y = jnp.dot(g_ref[...].astype(BF16), w_ref[...], preferred_element_type=F32)
    o_ref[...] = x_ref[...] + mod_ref[:, 2 * D:3 * D] * y


def _out_proj(g, x, layer, row_fn, mod, w_out):
    M, K = g.shape
    tm = TOKEN_TILE
    row = lambda i: (i, 0)
    return pl.pallas_call(
        _out_proj_kernel,
        grid=(M // tm,),
        in_specs=[pl.BlockSpec((tm, K), row), pl.BlockSpec((tm, D), row), _mod_spec(layer, lambda i: row_fn(i * tm)),
                  _const_spec((K, D))],
        out_specs=pl.BlockSpec((tm, D), row),
        out_shape=jax.ShapeDtypeStruct((M, D), F32),
        compiler_params=_cparams(1),
        name="out_proj",
    )(g, x, mod, w_out)


NA_SCALE = NA_HEAD_DIM ** -0.5
NA_ROWS = 32
NA_WIN = NA_KH * GRID_W
NT_DIMS = (((1,), (1,)), ((), ()))


def _na_in_kernel(x_ref, nw_ref, mod_ref, w_ref, q_ref, k_ref, v_ref, sz_ref):
    hb = _modnorm(x_ref[...], nw_ref[...], mod_ref[...]).astype(BF16)
    q_ref[...] = jnp.dot(hb, w_ref[:, 0:D], preferred_element_type=F32).astype(q_ref.dtype)
    k_ref[...] = jnp.dot(hb, w_ref[:, D:2 * D], preferred_element_type=F32).astype(k_ref.dtype)
    v_ref[...] = jnp.dot(hb, w_ref[:, 2 * D:3 * D], preferred_element_type=F32).astype(v_ref.dtype)
    sz_ref[...] = _silu(jnp.dot(hb, w_ref[:, 3 * D:4 * D], preferred_element_type=F32))


def _na_in(x, layer, row_fn, mod, nw, w_in, kv_dtype):
    M = x.shape[0]
    tm = TOKEN_TILE
    row = lambda i: (i, 0)
    blk = pl.BlockSpec((tm, D), row)
    return pl.pallas_call(
        _na_in_kernel,
        grid=(M // tm,),
        in_specs=[blk, _const_spec((1, D)), _mod_spec(layer, lambda i: row_fn(i * tm)), _const_spec((D, 4 * D))],
        out_specs=[blk, blk, blk, blk],
        out_shape=[jax.ShapeDtypeStruct((M, D), BF16), jax.ShapeDtypeStruct((M, D), kv_dtype),
                   jax.ShapeDtypeStruct((M, D), kv_dtype), jax.ShapeDtypeStruct((M, D), F32)],
        compiler_params=_cparams(1),
        name="na_in",
    )(x, nw, mod, w_in)


def _head_masks():
    lane = lax.broadcasted_iota(jnp.int32, (1, LANES), 1)
    return lane < NA_HEAD_DIM


def _na_ctx_kernel(q_ref, k_ref, v_ref, sz_ref, o_ref):
    lo = _head_masks()
    for p in range(NA_HEADS // 2):
        cols = slice(p * LANES, (p + 1) * LANES)
        qp = q_ref[:, cols]
        kp = k_ref[:, cols].astype(BF16)
        vp = v_ref[:, cols].astype(BF16)
        n = qp.shape[0]
        zero = jnp.zeros_like(qp)
        qs = jnp.concatenate([jnp.where(lo, qp, zero), jnp.where(lo, zero, qp)], axis=0)
        s = lax.dot_general(qs, kp, NT_DIMS, preferred_element_type=F32) * NA_SCALE
        e = jnp.exp(s - jnp.max(s, axis=-1, keepdims=True))
        l = jnp.sum(e, axis=-1, keepdims=True)
        o = jnp.dot(e.astype(BF16), vp, preferred_element_type=F32) / l
        o_ref[:, cols] = jnp.where(lo, o[:n], o[n:]) * sz_ref[:, cols]


def _na_ctx(q, k, v, sz, seq_len):
    M = q.shape[0]
    blk = pl.BlockSpec((seq_len, D), lambda i: (i, 0))
    return pl.pallas_call(
        _na_ctx_kernel,
        grid=(M // seq_len,),
        in_specs=[blk, blk, blk, blk],
        out_specs=blk,
        out_shape=jax.ShapeDtypeStruct((M, D), F32),
        compiler_params=_cparams(1),
        name="na_ctx_attn",
    )(q, k, v, sz)


def _rpb_kernel(rpb_ref, o_ref):
    h = pl.program_id(0)
    c = lax.broadcasted_iota(jnp.int32, (GRID_W, LANES), 0)
    lane = lax.broadcasted_iota(jnp.int32, (GRID_W, LANES), 1)
    kc = lane & (GRID_W - 1)
    upper = lax.broadcasted_iota(jnp.int32, (1, LANES), 1) >= GRID_W
    cs = jnp.clip(c - NA_KW // 2, 0, GRID_W - NA_KW)
    valid = (kc >= cs) & (kc < cs + NA_KW)
    d = kc - c
    n_dc = 2 * NA_KW - 1
    n_pair = 2 * NA_KH - 2
    for dri in range(n_pair):
        tile = jnp.zeros((GRID_W, LANES), F32)
        for dd in range(-(NA_KW - 1), NA_KW):
            base = h * ((2 * NA_KH - 1) * n_dc) + dri * n_dc + (dd + NA_KW - 1)
            tile = jnp.where(d == dd, jnp.where(upper, rpb_ref[base + n_dc], rpb_ref[base]), tile)
        tile = jnp.where(valid, tile, NEG_INF)
        for idx in range(NA_KH):
            jj2 = dri - (NA_KH - 1) + idx
            if jj2 % 2 == 0 and 0 <= jj2 // 2 < NA_KH // 2:
                jj = jj2 // 2
                o_ref[idx, :, jj * LANES:(jj + 1) * LANES] = tile


def _rpb_table(rpb):
    return pl.pallas_call(
        _rpb_kernel,
        grid=(NA_HEADS,),
        in_specs=[pl.BlockSpec(memory_space=pltpu.SMEM)],
        out_specs=pl.BlockSpec((NA_KH, None, GRID_W, NA_WIN), lambda h: (0, h, 0, 0)),
        out_shape=jax.ShapeDtypeStruct((NA_KH, NA_HEADS, GRID_W, NA_WIN), F32),
        compiler_params=_cparams(1),
        name="na_rpb_table",
    )(rpb.reshape(-1))


def _na_row_start(r):
    return jnp.clip(r - NA_KH // 2, 0, NA_ROWS - NA_KH)


def _na_lat_kernel(q_ref, k_ref, v_ref, ck_ref, cv_ref, bias_ref, sz_ref, o_ref):
    lo = _head_masks()
    k0 = pl.multiple_of(_na_row_start(pl.program_id(1)) * GRID_W, GRID_W)
    for p in range(NA_HEADS // 2):
        cols = slice(p * LANES, (p + 1) * LANES)
        qp = q_ref[:, cols]
        kw = k_ref[pl.ds(k0, NA_WIN), cols]
        vw = v_ref[pl.ds(k0, NA_WIN), cols]
        ck = ck_ref[:, cols]
        cv = cv_ref[:, cols]
        zero = jnp.zeros_like(qp)
        qs = jnp.concatenate([jnp.where(lo, qp, zero), jnp.where(lo, zero, qp)], axis=0)
        bias = bias_ref[2 * p:2 * p + 2].reshape(2 * GRID_W, NA_WIN)
        s_loc = lax.dot_general(qs, kw, NT_DIMS, preferred_element_type=F32) * NA_SCALE + bias
        s_ctx = lax.dot_general(qs, ck, NT_DIMS, preferred_element_type=F32) * NA_SCALE
        m = jnp.maximum(jnp.max(s_loc, axis=-1, keepdims=True), jnp.max(s_ctx, axis=-1, keepdims=True))
        e_loc = jnp.exp(s_loc - m)
        e_ctx = jnp.exp(s_ctx - m)
        l = jnp.sum(e_loc, axis=-1, keepdims=True) + jnp.sum(e_ctx, axis=-1, keepdims=True)
        o = (jnp.dot(e_loc.astype(BF16), vw, preferred_element_type=F32)
             + jnp.dot(e_ctx.astype(BF16), cv, preferred_element_type=F32)) / l
        o_ref[:, cols] = jnp.where(lo, o[:GRID_W], o[GRID_W:]) * sz_ref[:, cols]


def _na_lat(q, k, v, ck, cv, bias, sz, n_lat):
    T = NA_ROWS * GRID_W
    past = ck.shape[1]
    qblk = pl.BlockSpec((GRID_W, D), lambda b, r: (b * NA_ROWS + r, 0))
    kvblk = pl.BlockSpec((T, D), lambda b, r: (b, 0))
    cblk = pl.BlockSpec((None, past, D), lambda b, r: (b, 0, 0))
    bblk = pl.BlockSpec((None, NA_HEADS, GRID_W, NA_WIN), lambda b, r: (r - _na_row_start(r), 0, 0, 0))
    return pl.pallas_call(
        _na_lat_kernel,
        grid=(n_lat, NA_ROWS),
        in_specs=[qblk, kvblk, kvblk, cblk, cblk, bblk, qblk],
        out_specs=qblk,
        out_shape=jax.ShapeDtypeStruct((n_lat * T, D), F32),
        compiler_params=_cparams(2),
        name="na_lat_attn",
    )(q, k, v, ck, cv, bias, sz)


def _na_layer(xc, xl, layer, mod, nw, p, n_lat):
    j = layer // 3
    w_in = p["na_w_in"][j].astype(BF16)
    w_out = p["na_w_out"][j].astype(BF16)
    row_ctx = lambda tok: 0
    row_lat = lambda tok: 1 + tok // (NA_ROWS * GRID_W)
    qc, kc, vc, szc = _na_in(xc, layer, row_ctx, mod, nw, w_in, F32)
    ql, kl, vl, szl = _na_in(xl, layer, row_lat, mod, nw, w_in, BF16)
    gc = _na_ctx(qc, kc, vc, szc, SEQ)
    past = p["cache_k"].shape[2]
    ck = p["cache_k"][:n_lat, j].reshape(n_lat, past, D).astype(BF16)
    cv = p["cache_v"][:n_lat, j].reshape(n_lat, past, D).astype(BF16)
    gl = _na_lat(ql, kl, vl, ck, cv, _rpb_table(p["na_rpb"][j]), szl, n_lat)
    return (_out_proj(gc, xc, layer, row_ctx, mod, w_out), _out_proj(gl, xl, layer, row_lat, mod, w_out), kc, vc)


SSD_IN_COLS = SSD_INNER + SSD_XBC + LANES


def _softplus(x):
    return jnp.maximum(x, 0.0) + jnp.log(1.0 + jnp.exp(-jnp.abs(x)))


def _ssd_in_kernel(tiles_per_seq, *refs):
    refs = list(refs)
    x_ref = refs.pop(0)
    if tiles_per_seq > 1:
        xp_ref = refs.pop(0)
        xn_ref = refs.pop(0)
    nw_ref, mod_ref, w_ref, dtb_ref, wc_ref, bc_ref, sz_ref, xbc_ref, dt_ref, xpad_ref = refs
    tm = x_ref.shape[0]
    HL = SSD_HALO
    RB = 128
    hb = _modnorm(x_ref[...], nw_ref[...], mod_ref[...]).astype(BF16)
    if tiles_per_seq > 1:
        t = pl.program_id(0) % tiles_per_seq
        xh = jnp.concatenate([xp_ref[...], xn_ref[...]], axis=0)
        hh = _modnorm(xh, nw_ref[...], mod_ref[...]).astype(BF16)
    for j in range(SSD_XBC // D):
        c0 = SSD_INNER + j * D
        cols = slice(j * D, (j + 1) * D)
        xpad_ref[HL:HL + tm, cols] = jnp.dot(hb, w_ref[:, c0:c0 + D], preferred_element_type=F32)
        if tiles_per_seq > 1:
            rawh = jnp.dot(hh, w_ref[:, c0:c0 + D], preferred_element_type=F32)
            xpad_ref[0:HL, cols] = jnp.where(t > 0, rawh[0:HL], 0.0)
            xpad_ref[HL + tm:HL + tm + HL, cols] = jnp.where(t < tiles_per_seq - 1, rawh[HL:2 * HL], 0.0)
        else:
            xpad_ref[0:HL, cols] = jnp.zeros((HL, D), F32)
            xpad_ref[HL + tm:HL + tm + HL, cols] = jnp.zeros((HL, D), F32)
        for cb in range(D // LANES):
            lc = slice(j * D + cb * LANES, j * D + (cb + 1) * LANES)
            for rb in range(tm // RB):
                acc = jnp.zeros((RB, LANES), F32)
                for k in range(SSD_CONV):
                    r0 = rb * RB + HL - SSD_CONV // 2 + k
                    acc = acc + xpad_ref[r0:r0 + RB, lc] * wc_ref[k:k + 1, lc]
                xbc_ref[rb * RB:(rb + 1) * RB, lc] = _silu(acc + bc_ref[:, lc])
    for j in range(SSD_INNER // D):
        sz_ref[:, j * D:(j + 1) * D] = _silu(jnp.dot(hb, w_ref[:, j * D:(j + 1) * D], preferred_element_type=F32))
    c0 = SSD_INNER + SSD_XBC
    raw = jnp.dot(hb, w_ref[:, c0:c0 + LANES], preferred_element_type=F32) + dtb_ref[...]
    lane = lax.broadcasted_iota(jnp.int32, (1, LANES), 1)
    dt_ref[...] = jnp.where(lane < 2 * SSD_HEADS, _softplus(raw), 0.0)


def _ssd_in(x, seq_len, layer, row_fn, mod, nw, w_in, dt_bias, w_conv, b_conv):
    M = x.shape[0]
    tm = TOKEN_TILE
    HL = SSD_HALO
    tps = seq_len // tm
    n_hblk = M // HL
    row = lambda i: (i, 0)
    in_specs = [pl.BlockSpec((tm, D), row)]
    args = [x]
    if tps > 1:
        in_specs += [pl.BlockSpec((HL, D), lambda i: (jnp.maximum(i * (tm // HL) - 1, 0), 0)),
                     pl.BlockSpec((HL, D), lambda i: (jnp.minimum((i + 1) * (tm // HL), n_hblk - 1), 0))]
        args += [x, x]
    in_specs += [_const_spec((1, D)), _mod_spec(layer, lambda i: row_fn(i * tm)),
                 _const_spec((D, SSD_IN_COLS)), _const_spec((1, LANES)),
                 _const_spec((SSD_CONV + 1, SSD_XBC)), _const_spec((1, SSD_XBC))]
    args += [nw, mod, w_in, dt_bias, w_conv, b_conv]
    return pl.pallas_call(
        functools.partial(_ssd_in_kernel, tps),
        grid=(M // tm,),
        in_specs=in_specs,
        out_specs=[pl.BlockSpec((tm, SSD_INNER), row), pl.BlockSpec((tm, SSD_XBC), row), pl.BlockSpec((tm, LANES), row)],
        out_shape=[jax.ShapeDtypeStruct((M, SSD_INNER), F32), jax.ShapeDtypeStruct((M, SSD_XBC), F32),
                   jax.ShapeDtypeStruct((M, LANES), F32)],
        scratch_shapes=[pltpu.VMEM((tm + 2 * HL, SSD_XBC), F32)],
        compiler_params=_cparams(1),
        name="ssd_in",
    )(*args)


def _cumsum_rows(a, reverse):
    n = a.shape[0]
    row = lax.broadcasted_iota(jnp.int32, a.shape, 0)
    k = 1
    while k < n:
        if reverse:
            a = a + jnp.where(row < n - k, pltpu.roll(a, n - k, 0), 0.0)
        else:
            a = a + jnp.where(row >= k, pltpu.roll(a, k, 0), 0.0)
        k *= 2
    return a


def _ssd_core_kernel(nc, has_h0, emit_state, *refs):
    refs = list(refs)
    dir_refs = [refs[0:2], refs[2:4]]
    alog_ref, dsk_ref = refs[4:6]
    refs = refs[6:]
    if has_h0:
        h0_ref = refs.pop(0)
    y_refs = [refs.pop(0), refs.pop(0)]
    if emit_state:
        so_ref = refs.pop(0)
    (st_ref,) = refs
    Q, N, P = SSD_CHUNK, SSD_STATE, SSD_HEAD_DIM
    c = pl.program_id(1)
    n_blk = SSD_INNER // LANES

    @pl.when(c == 0)
    def _init():
        if has_h0:
            for dr in range(2):
                for j in range(n_blk):
                    st_ref[dr, :, j * LANES:(j + 1) * LANES] = h0_ref[dr, j * LANES:(j + 1) * LANES, :].T
        else:
            st_ref[...] = jnp.zeros(st_ref.shape, F32)

    lo = _head_masks()
    A = -jnp.exp(alog_ref[...])
    ri = lax.broadcasted_iota(jnp.int32, (Q, Q), 0)
    ci = lax.broadcasted_iota(jnp.int32, (Q, Q), 1)

    for dr in range(2):
        x_ref, dt_ref = dir_refs[dr]
        y_ref = y_refs[dr]

        def conv_cols(c0, x_ref=x_ref):
            return x_ref[:, c0:c0 + LANES]

        dt = dt_ref[...]
        cum = _cumsum_rows(dt * A, reverse=(dr == 1))
        cum_t = cum.T
        tot_row = Q - 1 if dr == 0 else 0
        mask = (ri >= ci) if dr == 0 else (ri <= ci)

        for g in range(SSD_GROUPS):
            bg_t = conv_cols(SSD_INNER + g * N).T.astype(BF16)
            cg = conv_cols(SSD_INNER + SSD_GN + g * N).astype(BF16)
            cb = jnp.dot(cg, bg_t, preferred_element_type=F32)
            for pp in range(SSD_HEADS // SSD_GROUPS // 2):
                head = g * (SSD_HEADS // SSD_GROUPS) + 2 * pp
                c0 = head * P
                li = dr * SSD_HEADS + head
                xpair = conv_cols(c0)
                dt_cols = [dt[:, li:li + 1], dt[:, li + 1:li + 2]]
                cum_cols = [cum[:, li:li + 1], cum[:, li + 1:li + 2]]
                xdt = xpair * jnp.where(lo, dt_cols[0], dt_cols[1])
                xdt_b = xdt.astype(BF16)
                cum_x = jnp.where(lo, cum_cols[0], cum_cols[1])
                yd = []
                for hh in range(2):
                    seg = cum_cols[hh] - cum_t[li + hh:li + hh + 1, :]
                    decay = jnp.exp(jnp.where(mask, seg, -jnp.inf))
                    yd.append(jnp.dot((cb * decay).astype(BF16), xdt_b, preferred_element_type=F32))
                st = st_ref[dr, :, c0:c0 + LANES]
                y = (jnp.where(lo, yd[0], yd[1])
                     + jnp.dot(cg, st.astype(BF16), preferred_element_type=F32) * jnp.exp(cum_x))
                if dr == 0:
                    y = y + xpair * dsk_ref[:, c0:c0 + LANES]
                y_ref[:, c0:c0 + LANES] = y
                tot = cum_x[tot_row:tot_row + 1, :]
                xw = (xdt * jnp.exp(tot - cum_x)).astype(BF16)
                st_ref[dr, :, c0:c0 + LANES] = st * jnp.exp(tot) + jnp.dot(bg_t, xw, preferred_element_type=F32)

    if emit_state:
        @pl.when(c == nc - 1)
        def _emit():
            for dr in range(2):
                for j in range(n_blk):
                    so_ref[dr, j * LANES:(j + 1) * LANES, :] = st_ref[dr, :, j * LANES:(j + 1) * LANES].T


def _ssd_core(xbc, dtp, a_log, d_skip, h0, n_seq, seq_len, emit_state):
    Q = SSD_CHUNK
    nc = seq_len // Q
    M = n_seq * seq_len

    def chunk(dr):
        return (lambda b, c: b * nc + c) if dr == 0 else (lambda b, c: b * nc + nc - 1 - c)

    in_specs, args = [], []
    for dr in range(2):
        ch = chunk(dr)
        in_specs += [pl.BlockSpec((Q, SSD_XBC), lambda b, c, ch=ch: (ch(b, c), 0)),
                     pl.BlockSpec((Q, LANES), lambda b, c, ch=ch: (ch(b, c), 0))]
        args += [xbc, dtp]
    in_specs += [_const_spec((1, LANES)), _const_spec((1, SSD_INNER))]
    args += [a_log, d_skip]
    st_blk = pl.BlockSpec((None, 2, SSD_INNER, SSD_STATE), lambda b, c: (b, 0, 0, 0))
    has_h0 = h0 is not None
    if has_h0:
        in_specs.append(st_blk)
        args.append(h0)
    out_specs = [pl.BlockSpec((Q, SSD_INNER), lambda b, c: (chunk(0)(b, c), 0)),
                 pl.BlockSpec((Q, SSD_INNER), lambda b, c: (chunk(1)(b, c), 0))]
    out_shape = [jax.ShapeDtypeStruct((M, SSD_INNER), F32)] * 2
    if emit_state:
        out_specs.append(st_blk)
        out_shape.append(jax.ShapeDtypeStruct((n_seq, 2, SSD_INNER, SSD_STATE), F32))
    return pl.pallas_call(
        functools.partial(_ssd_core_kernel, nc, has_h0, emit_state),
        grid=(n_seq, nc),
        in_specs=in_specs,
        out_specs=out_specs,
        out_shape=out_shape,
        scratch_shapes=[pltpu.VMEM((2, SSD_STATE, SSD_INNER), F32)],
        compiler_params=_cparams(2),
        name="ssd_core",
    )(*args)


def _ssd_out_kernel(yf_ref, yb_ref, sz_ref, x_ref, mod_ref, nw_ref, w_ref, o_ref):
    g = _rms((yf_ref[...] + yb_ref[...]) * sz_ref[...], nw_ref[...]).astype(BF16)
    o_ref[...] = x_ref[...] + mod_ref[:, 2 * D:3 * D] * jnp.dot(g, w_ref[...], preferred_element_type=F32)


def _ssd_out(yf, yb, sz, x, layer, row_fn, mod, norm_w, w_out):
    M = x.shape[0]
    tm = TOKEN_TILE
    row = lambda i: (i, 0)
    wide = pl.BlockSpec((tm, SSD_INNER), row)
    return pl.pallas_call(
        _ssd_out_kernel,
        grid=(M // tm,),
        in_specs=[wide, wide, wide, pl.BlockSpec((tm, D), row), _mod_spec(layer, lambda i: row_fn(i * tm)),
                  _const_spec((1, SSD_INNER)), _const_spec((SSD_INNER, D))],
        out_specs=pl.BlockSpec((tm, D), row),
        out_shape=jax.ShapeDtypeStruct((M, D), F32),
        compiler_params=_cparams(1),
        name="ssd_out",
    )(yf, yb, sz, x, mod, norm_w, w_out)


def _ssd_layer(xc, xl, layer, mod, nw, p, n_lat):
    j = layer // 3
    w_in = jnp.pad(p["ssd_w_in"][j].astype(BF16), ((0, 0), (0, LANES - 2 * SSD_HEADS)))
    dt_bias = jnp.pad(p["ssd_dt_bias"][j].reshape(1, -1), ((0, 0), (0, LANES - 2 * SSD_HEADS)))
    a_log = jnp.pad(p["ssd_a_log"][j].reshape(1, -1), ((0, 0), (0, LANES - 2 * SSD_HEADS)))
    w_conv = jnp.pad(p["ssd_w_conv"][j], ((0, 1), (0, 0)))
    b_conv = p["ssd_b_conv"][j][None]
    d_skip = jnp.repeat(p["ssd_d"][j], SSD_HEAD_DIM)[None]
    norm_w = p["ssd_norm_w"][j][None]
    w_out = p["ssd_w_out"][j].astype(BF16)
    row_ctx = lambda tok: 0
    row_lat = lambda tok: 1 + tok // DEC_SEQ
    n_ctx = xc.shape[0] // SEQ
    szc, xbc_c, dt_c = _ssd_in(xc, SEQ, layer, row_ctx, mod, nw, w_in, dt_bias, w_conv, b_conv)
    szl, xbc_l, dt_l = _ssd_in(xl, DEC_SEQ, layer, row_lat, mod, nw, w_in, dt_bias, w_conv, b_conv)
    yfc, ybc, s_new = _ssd_core(xbc_c, dt_c, a_log, d_skip, None, n_ctx, SEQ, True)
    h0 = p["state_ssm"][:n_lat, j].reshape(n_lat, 2, SSD_INNER, SSD_STATE)
    yfl, ybl = _ssd_core(xbc_l, dt_l, a_log, d_skip, h0, n_lat, DEC_SEQ, False)
    oc = _ssd_out(yfc, ybc, szc, xc, layer, row_ctx, mod, norm_w, w_out)
    ol = _ssd_out(yfl, ybl, szl, xl, layer, row_lat, mod, norm_w, w_out)
    return oc, ol, s_new


def kernel(x_prompt, x_sample, cache_k, cache_v, state_ssm, c, c_ctx, ada_w, ada_b, norm_w, final_norm_w, conv_w_in, conv_b_in, conv_w_dw, conv_b_dw, conv_ln_w, conv_ln_b, conv_w_out, conv_b_out, na_w_in, na_rpb, na_w_out, ssd_w_in, ssd_w_conv, ssd_b_conv, ssd_a_log, ssd_dt_bias, ssd_d, ssd_norm_w, ssd_w_out):
    p = dict(cache_k=cache_k, cache_v=cache_v, state_ssm=state_ssm,
             conv_w_in=conv_w_in, conv_b_in=conv_b_in, conv_w_dw=conv_w_dw, conv_b_dw=conv_b_dw,
             conv_ln_w=conv_ln_w, conv_ln_b=conv_ln_b, conv_w_out=conv_w_out, conv_b_out=conv_b_out,
             na_w_in=na_w_in, na_rpb=na_rpb, na_w_out=na_w_out,
             ssd_w_in=ssd_w_in, ssd_w_conv=ssd_w_conv, ssd_b_conv=ssd_b_conv, ssd_a_log=ssd_a_log,
             ssd_dt_bias=ssd_dt_bias, ssd_d=ssd_d, ssd_norm_w=ssd_norm_w, ssd_w_out=ssd_w_out)
    n_ctx, n_lat = x_prompt.shape[0], x_sample.shape[0]
    assert x_prompt.shape[1:] == (SEQ, D) and x_sample.shape[1:] == (DEC_SEQ, D)
    assert (DEPTH - 1) % 3 == 0, "the final RMSNorm is fused into a trailing Conformer layer"
    xc = x_prompt.reshape(n_ctx * SEQ, D)
    xl = x_sample.reshape(n_lat * DEC_SEQ, D)
    cond8 = jnp.concatenate([c_ctx[None], c, jnp.zeros((8 - 1 - n_lat, D), F32)], axis=0)
    mod = _modulation(cond8, ada_w, ada_b).reshape(DEPTH, 8, 1, 3 * D)
    row_ctx = lambda tok: 0
    row_lat = lambda tok: 1 + tok // DEC_SEQ
    new_k, new_v, new_s = [], [], []
    for i in range(DEPTH):
        kind, j = i % 3, i // 3
        nw = norm_w[i][None]
        if kind == 0:
            cp = _prep_conv(p, j)
            fw = final_norm_w[None] if i == DEPTH - 1 else None
            xc = _conv_layer(xc, SEQ, i, row_ctx, mod, nw, *cp, fw)
            xl = _conv_layer(xl, DEC_SEQ, i, row_lat, mod, nw, *cp, fw)
        elif kind == 1:
            xc, xl, k_c, v_c = _na_layer(xc, xl, i, mod, nw, p, n_lat)
            new_k.append(k_c.reshape(n_ctx, SEQ, NA_HEADS, NA_HEAD_DIM))
            new_v.append(v_c.reshape(n_ctx, SEQ, NA_HEADS, NA_HEAD_DIM))
        else:
            xc, xl, s_c = _ssd_layer(xc, xl, i, mod, nw, p, n_lat)
            new_s.append(s_c.reshape(n_ctx, 2, SSD_HEADS, SSD_HEAD_DIM, SSD_STATE))
    return (xc.reshape(n_ctx, SEQ, D), xl.reshape(n_lat, DEC_SEQ, D),
            jnp.stack(new_k, axis=1), jnp.stack(new_v, axis=1), jnp.stack(new_s, axis=1))
```

```python
import functools

import jax
import jax.numpy as jnp
from jax import lax
from jax.experimental import pallas as pl
from jax.experimental.pallas import tpu as pltpu

F32 = jnp.float32
BF16 = jnp.bfloat16

D = 1024
DEPTH = 4
SEQ = 256
DEC_SEQ = 2048
GRID_W = 64
RMS_EPS = 1e-6
LN_EPS = 1e-5
CONF_K = 31
NA_HEADS = 16
NA_HEAD_DIM = 64
NA_KH = 8
NA_KW = 16
NEG_INF = -1e30
SSD_INNER = 2048
SSD_HEADS = 32
SSD_HEAD_DIM = 64
SSD_STATE = 128
SSD_GROUPS = 4
SSD_GN = SSD_GROUPS * SSD_STATE
SSD_CONV = 7
SSD_CHUNK = 128
SSD_XBC = SSD_INNER + 2 * SSD_GN

LANES = 128
SUBLANES = 8
TOKEN_TILE = 256
CONV_HALO = 16
SSD_HALO = 8
VMEM_LIMIT = 56 * 1024 * 1024


def _cparams(n_axes):
    return pltpu.CompilerParams(dimension_semantics=("arbitrary",) * n_axes,
                                vmem_limit_bytes=VMEM_LIMIT)


def _const_spec(shape):
    nd = len(shape)
    return pl.BlockSpec(shape, lambda *_: (0,) * nd, pipeline_mode=pl.Buffered(1))


def _sigmoid(x):
    return 1.0 / (1.0 + jnp.exp(-x))


def _silu(x):
    return x * _sigmoid(x)


def _rms(x, w):
    ms = jnp.mean(x * x, axis=-1, keepdims=True)
    return x * lax.rsqrt(ms + RMS_EPS) * w


def _modnorm(x, nw, mod):
    return _rms(x, nw) * (1.0 + mod[:, D:2 * D]) + mod[:, :D]


def _mod_spec(layer, row_fn):
    return pl.BlockSpec((None, None, 1, 3 * D), lambda i, *_: (layer, row_fn(i), 0, 0))


def _mod_kernel(cond_ref, w_ref, b_ref, o_ref):
    s = _silu(cond_ref[...]).astype(BF16)
    o_ref[...] = jnp.dot(s, w_ref[...].astype(BF16), preferred_element_type=F32) + b_ref[...]


def _modulation(cond8, ada_w, ada_b):
    tn = 1024
    return pl.pallas_call(
        _mod_kernel,
        grid=(DEPTH, 3 * D // tn),
        in_specs=[pl.BlockSpec((8, D), lambda i, j: (0, 0)),
                  pl.BlockSpec((None, D, tn), lambda i, j: (i, 0, j)),
                  pl.BlockSpec((None, 1, tn), lambda i, j: (i, 0, j))],
        out_specs=pl.BlockSpec((None, 8, tn), lambda i, j: (i, 0, j)),
        out_shape=jax.ShapeDtypeStruct((DEPTH, 8, 3 * D), F32),
        compiler_params=_cparams(2),
        name="adaln_mod",
    )(cond8, ada_w, ada_b.reshape(DEPTH, 1, 3 * D))


CONV_CHUNK = 256


def _conv_kernel(tiles_per_seq, final, *refs):
    refs = list(refs)
    x_ref = refs.pop(0)
    if tiles_per_seq > 1:
        xp_ref = refs.pop(0)
        xn_ref = refs.pop(0)
    nw_ref, mod_ref, wi_ref, bi_ref, wdw_ref, bdw_ref, lnw_ref, lnb_ref, wo_ref, bo_ref = refs[:10]
    refs = refs[10:]
    if final:
        fw_ref = refs.pop(0)
    o_ref, ubuf, cbuf = refs
    T = x_ref.shape[0]
    H = CONV_HALO
    RB = 128
    span = RB + 2 * H

    hb = _modnorm(x_ref[...], nw_ref[...], mod_ref[...]).astype(BF16)
    if tiles_per_seq > 1:
        t = pl.program_id(0) % tiles_per_seq
        hh = _modnorm(jnp.concatenate([xp_ref[...], xn_ref[...]], axis=0), nw_ref[...], mod_ref[...]).astype(BF16)

    def glu(h, cols):
        v = jnp.dot(h, wi_ref[:, cols], preferred_element_type=F32) + bi_ref[:, cols]
        gcols = slice(D + cols.start, D + cols.stop)
        g = jnp.dot(h, wi_ref[:, gcols], preferred_element_type=F32) + bi_ref[:, gcols]
        return v * _sigmoid(g)

    for ch in range(D // CONV_CHUNK):
        cols = slice(ch * CONV_CHUNK, (ch + 1) * CONV_CHUNK)
        ubuf[H:H + T, cols] = glu(hb, cols)
        if tiles_per_seq > 1:
            uh = glu(hh, cols)
            ubuf[0:H, cols] = jnp.where(t > 0, uh[0:H], 0.0)
            ubuf[H + T:H + T + H, cols] = jnp.where(t < tiles_per_seq - 1, uh[H:2 * H], 0.0)
        else:
            ubuf[0:H, cols] = jnp.zeros((H, CONV_CHUNK), F32)
            ubuf[H + T:H + T + H, cols] = jnp.zeros((H, CONV_CHUNK), F32)
        for cb in range(CONV_CHUNK // LANES):
            lc = slice(cols.start + cb * LANES, cols.start + (cb + 1) * LANES)
            for rb in range(T // RB):
                acc = jnp.zeros((RB, LANES), F32)
                win = ubuf[rb * RB:rb * RB + span, lc]
                for s in range(SUBLANES):
                    v = win if s == 0 else pltpu.roll(win, span - s, 0)
                    for q in range((span - RB) // SUBLANES):
                        k = SUBLANES * q + s - (H - CONF_K // 2)
                        if 0 <= k < CONF_K:
                            acc = acc + v[SUBLANES * q:SUBLANES * q + RB] * wdw_ref[k:k + 1, lc]
                cbuf[rb * RB:(rb + 1) * RB, lc] = acc + bdw_ref[:, lc]

    sz = _silu(jnp.dot(hb, wi_ref[:, 2 * D:3 * D], preferred_element_type=F32) + bi_ref[:, 2 * D:3 * D])
    c = cbuf[...]
    mu = jnp.mean(c, axis=-1, keepdims=True)
    xc = c - mu
    var = jnp.mean(xc * xc, axis=-1, keepdims=True)
    ln = xc * lax.rsqrt(var + LN_EPS) * lnw_ref[...] + lnb_ref[...]
    g = (_silu(ln) * sz).astype(BF16)
    y = jnp.dot(g, wo_ref[...], preferred_element_type=F32) + bo_ref[...]
    xn = x_ref[...] + mod_ref[:, 2 * D:3 * D] * y
    if final:
        xn = _rms(xn, fw_ref[...])
    o_ref[...] = xn


def _conv_layer(x, seq_len, layer, row_fn, mod, nw, w_in, b_in, w_dw, b_dw, ln_w, ln_b, w_out, b_out, final_w):
    M = x.shape[0]
    T = TOKEN_TILE
    tps = seq_len // T
    H = CONV_HALO
    n_halo_blocks = M // H
    row = lambda i: (i, 0)
    in_specs = [pl.BlockSpec((T, D), row)]
    args = [x]
    if tps > 1:
        in_specs += [pl.BlockSpec((H, D), lambda i: (jnp.maximum(i * (T // H) - 1, 0), 0)),
                     pl.BlockSpec((H, D), lambda i: (jnp.minimum((i + 1) * (T // H), n_halo_blocks - 1), 0))]
        args += [x, x]
    in_specs += [_const_spec((1, D)), _mod_spec(layer, lambda i: row_fn(i * T)),
                 _const_spec((D, 3 * D)), _const_spec((1, 3 * D)),
                 _const_spec((CONF_K + 1, D)), _const_spec((1, D)), _const_spec((1, D)), _const_spec((1, D)),
                 _const_spec((D, D)), _const_spec((1, D))]
    args += [nw, mod, w_in, b_in, w_dw, b_dw, ln_w, ln_b, w_out, b_out]
    final = final_w is not None
    if final:
        in_specs.append(_const_spec((1, D)))
        args.append(final_w)
    return pl.pallas_call(
        functools.partial(_conv_kernel, tps, final),
        grid=(M // T,),
        in_specs=in_specs,
        out_specs=pl.BlockSpec((T, D), row),
        out_shape=jax.ShapeDtypeStruct((M, D), F32),
        scratch_shapes=[pltpu.VMEM((T + 2 * H, D), F32), pltpu.VMEM((T, D), F32)],
        compiler_params=_cparams(1),
        name="conv_layer",
    )(*args)


def _prep_conv(p, j):
    w_dw = jnp.pad(p["conv_w_dw"][j], ((0, 1), (0, 0)))
    return (p["conv_w_in"][j].astype(BF16), p["conv_b_in"][j][None], w_dw, p["conv_b_dw"][j][None],
            p["conv_ln_w"][j][None], p["conv_ln_b"][j][None], p["conv_w_out"][j].astype(BF16), p["conv_b_out"][j][None])


def _out_proj_kernel(g_ref, x_ref, mod_ref, w_ref, o_ref):
    y = jnp.dot(g_ref[...].astype(BF16), w_ref[...], preferred_element_type=F32)
    o_ref[...] = x_ref[...] + mod_ref[:, 2 * D:3 * D] * y


def _out_proj(g, x, layer, row_fn, mod, w_out):
    M, K = g.shape
    tm = TOKEN_TILE
    row = lambda i: (i, 0)
    return pl.pallas_call(
        _out_proj_kernel,
        grid=(M // tm,),
        in_specs=[pl.BlockSpec((tm, K), row), pl.BlockSpec((tm, D), row), _mod_spec(layer, lambda i: row_fn(i * tm)),
                  _const_spec((K, D))],
        out_specs=pl.BlockSpec((tm, D), row),
        out_shape=jax.ShapeDtypeStruct((M, D), F32),
        compiler_params=_cparams(1),
        name="out_proj",
    )(g, x, mod, w_out)


NA_SCALE = NA_HEAD_DIM ** -0.5
NA_ROWS = 32
NA_WIN = NA_KH * GRID_W
NT_DIMS = (((1,), (1,)), ((), ()))


def _na_in_kernel(x_ref, nw_ref, mod_ref, w_ref, q_ref, k_ref, v_ref, sz_ref):
    hb = _modnorm(x_ref[...], nw_ref[...], mod_ref[...]).astype(BF16)
    q_ref[...] = jnp.dot(hb, w_ref[:, 0:D], preferred_element_type=F32).astype(q_ref.dtype)
    k_ref[...] = jnp.dot(hb, w_ref[:, D:2 * D], preferred_element_type=F32).astype(k_ref.dtype)
    v_ref[...] = jnp.dot(hb, w_ref[:, 2 * D:3 * D], preferred_element_type=F32).astype(v_ref.dtype)
    sz_ref[...] = _silu(jnp.dot(hb, w_ref[:, 3 * D:4 * D], preferred_element_type=F32))


def _na_in(x, layer, row_fn, mod, nw, w_in, kv_dtype):
    M = x.shape[0]
    tm = TOKEN_TILE
    row = lambda i: (i, 0)
    blk = pl.BlockSpec((tm, D), row)
    return pl.pallas_call(
        _na_in_kernel,
        grid=(M // tm,),
        in_specs=[blk, _const_spec((1, D)), _mod_spec(layer, lambda i: row_fn(i * tm)), _const_spec((D, 4 * D))],
        out_specs=[blk, blk, blk, blk],
        out_shape=[jax.ShapeDtypeStruct((M, D), BF16), jax.ShapeDtypeStruct((M, D), kv_dtype),
                   jax.ShapeDtypeStruct((M, D), kv_dtype), jax.ShapeDtypeStruct((M, D), F32)],
        compiler_params=_cparams(1),
        name="na_in",
    )(x, nw, mod, w_in)


def _head_masks():
    lane = lax.broadcasted_iota(jnp.int32, (1, LANES), 1)
    return lane < NA_HEAD_DIM


def _na_ctx_kernel(q_ref, k_ref, v_ref, sz_ref, o_ref):
    lo = _head_masks()
    for p in range(NA_HEADS // 2):
        cols = slice(p * LANES, (p + 1) * LANES)
        qp = q_ref[:, cols]
        kp = k_ref[:, cols].astype(BF16)
        vp = v_ref[:, cols].astype(BF16)
        n = qp.shape[0]
        zero = jnp.zeros_like(qp)
        qs = jnp.concatenate([jnp.where(lo, qp, zero), jnp.where(lo, zero, qp)], axis=0)
        s = lax.dot_general(qs, kp, NT_DIMS, preferred_element_type=F32) * NA_SCALE
        e = jnp.exp(s - jnp.max(s, axis=-1, keepdims=True))
        l = jnp.sum(e, axis=-1, keepdims=True)
        o = jnp.dot(e.astype(BF16), vp, preferred_element_type=F32) / l
        o_ref[:, cols] = jnp.where(lo, o[:n], o[n:]) * sz_ref[:, cols]


def _na_ctx(q, k, v, sz, seq_len):
    M = q.shape[0]
    blk = pl.BlockSpec((seq_len, D), lambda i: (i, 0))
    return pl.pallas_call(
        _na_ctx_kernel,
        grid=(M // seq_len,),
        in_specs=[blk, blk, blk, blk],
        out_specs=blk,
        out_shape=jax.ShapeDtypeStruct((M, D), F32),
        compiler_params=_cparams(1),
        name="na_ctx_attn",
    )(q, k, v, sz)


def _rpb_kernel(rpb_ref, o_ref):
    h = pl.program_id(0)
    c = lax.broadcasted_iota(jnp.int32, (GRID_W, LANES), 0)
    lane = lax.broadcasted_iota(jnp.int32, (GRID_W, LANES), 1)
    kc = lane & (GRID_W - 1)
    upper = lax.broadcasted_iota(jnp.int32, (1, LANES), 1) >= GRID_W
    cs = jnp.clip(c - NA_KW // 2, 0, GRID_W - NA_KW)
    valid = (kc >= cs) & (kc < cs + NA_KW)
    d = kc - c
    n_dc = 2 * NA_KW - 1
    n_pair = 2 * NA_KH - 2
    for dri in range(n_pair):
        tile = jnp.zeros((GRID_W, LANES), F32)
        for dd in range(-(NA_KW - 1), NA_KW):
            base = h * ((2 * NA_KH - 1) * n_dc) + dri * n_dc + (dd + NA_KW - 1)
            tile = jnp.where(d == dd, jnp.where(upper, rpb_ref[base + n_dc], rpb_ref[base]), tile)
        tile = jnp.where(valid, tile, NEG_INF)
        for idx in range(NA_KH):
            jj2 = dri - (NA_KH - 1) + idx
            if jj2 % 2 == 0 and 0 <= jj2 // 2 < NA_KH // 2:
                jj = jj2 // 2
                o_ref[idx, :, jj * LANES:(jj + 1) * LANES] = tile


def _rpb_table(rpb):
    return pl.pallas_call(
        _rpb_kernel,
        grid=(NA_HEADS,),
        in_specs=[pl.BlockSpec(memory_space=pltpu.SMEM)],
        out_specs=pl.BlockSpec((NA_KH, None, GRID_W, NA_WIN), lambda h: (0, h, 0, 0)),
        out_shape=jax.ShapeDtypeStruct((NA_KH, NA_HEADS, GRID_W, NA_WIN), F32),
        compiler_params=_cparams(1),
        name="na_rpb_table",
    )(rpb.reshape(-1))


def _na_row_start(r):
    return jnp.clip(r - NA_KH // 2, 0, NA_ROWS - NA_KH)


def _na_lat_kernel(q_ref, k_ref, v_ref, ck_ref, cv_ref, bias_ref, sz_ref, o_ref):
    lo = _head_masks()
    k0 = pl.multiple_of(_na_row_start(pl.program_id(1)) * GRID_W, GRID_W)
    for p in range(NA_HEADS // 2):
        cols = slice(p * LANES, (p + 1) * LANES)
        qp = q_ref[:, cols]
        kw = k_ref[pl.ds(k0, NA_WIN), cols]
        vw = v_ref[pl.ds(k0, NA_WIN), cols]
        ck = ck_ref[:, cols]
        cv = cv_ref[:, cols]
        zero = jnp.zeros_like(qp)
        qs = jnp.concatenate([jnp.where(lo, qp, zero), jnp.where(lo, zero, qp)], axis=0)
        bias = bias_ref[2 * p:2 * p + 2].reshape(2 * GRID_W, NA_WIN)
        s_loc = lax.dot_general(qs, kw, NT_DIMS, preferred_element_type=F32) * NA_SCALE + bias
        s_ctx = lax.dot_general(qs, ck, NT_DIMS, preferred_element_type=F32) * NA_SCALE
        m = jnp.maximum(jnp.max(s_loc, axis=-1, keepdims=True), jnp.max(s_ctx, axis=-1, keepdims=True))
        e_loc = jnp.exp(s_loc - m)
        e_ctx = jnp.exp(s_ctx - m)
        l = jnp.sum(e_loc, axis=-1, keepdims=True) + jnp.sum(e_ctx, axis=-1, keepdims=True)
        o = (jnp.dot(e_loc.astype(BF16), vw, preferred_element_type=F32)
             + jnp.dot(e_ctx.astype(BF16), cv, preferred_element_type=F32)) / l
        o_ref[:, cols] = jnp.where(lo, o[:GRID_W], o[GRID_W:]) * sz_ref[:, cols]


def _na_lat(q, k, v, ck, cv, bias, sz, n_lat):
    T = NA_ROWS * GRID_W
    past = ck.shape[1]
    qblk = pl.BlockSpec((GRID_W, D), lambda b, r: (b * NA_ROWS + r, 0))
    kvblk = pl.BlockSpec((T, D), lambda b, r: (b, 0))
    cblk = pl.BlockSpec((None, past, D), lambda b, r: (b, 0, 0))
    bblk = pl.BlockSpec((None, NA_HEADS, GRID_W, NA_WIN), lambda b, r: (r - _na_row_start(r), 0, 0, 0))
    return pl.pallas_call(
        _na_lat_kernel,
        grid=(n_lat, NA_ROWS),
        in_specs=[qblk, kvblk, kvblk, cblk, cblk, bblk, qblk],
        out_specs=qblk,
        out_shape=jax.ShapeDtypeStruct((n_lat * T, D), F32),
        compiler_params=_cparams(2),
        name="na_lat_attn",
    )(q, k, v, ck, cv, bias, sz)


def _na_layer(xc, xl, layer, mod, nw, p, n_lat):
    j = layer // 3
    w_in = p["na_w_in"][j].astype(BF16)
    w_out = p["na_w_out"][j].astype(BF16)
    row_ctx = lambda tok: 0
    row_lat = lambda tok: 1 + tok // (NA_ROWS * GRID_W)
    qc, kc, vc, szc = _na_in(xc, layer, row_ctx, mod, nw, w_in, F32)
    ql, kl, vl, szl = _na_in(xl, layer, row_lat, mod, nw, w_in, BF16)
    gc = _na_ctx(qc, kc, vc, szc, SEQ)
    past = p["cache_k"].shape[2]
    ck = p["cache_k"][:n_lat, j].reshape(n_lat, past, D).astype(BF16)
    cv = p["cache_v"][:n_lat, j].reshape(n_lat, past, D).astype(BF16)
    gl = _na_lat(ql, kl, vl, ck, cv, _rpb_table(p["na_rpb"][j]), szl, n_lat)
    return (_out_proj(gc, xc, layer, row_ctx, mod, w_out), _out_proj(gl, xl, layer, row_lat, mod, w_out), kc, vc)


SSD_IN_COLS = SSD_INNER + SSD_XBC + LANES


def _softplus(x):
    return jnp.maximum(x, 0.0) + jnp.log(1.0 + jnp.exp(-jnp.abs(x)))


def _ssd_in_kernel(tiles_per_seq, *refs):
    refs = list(refs)
    x_ref = refs.pop(0)
    if tiles_per_seq > 1:
        xp_ref = refs.pop(0)
        xn_ref = refs.pop(0)
    nw_ref, mod_ref, w_ref, dtb_ref, wc_ref, bc_ref, sz_ref, xbc_ref, dt_ref, xpad_ref = refs
    tm = x_ref.shape[0]
    HL = SSD_HALO
    RB = 128
    hb = _modnorm(x_ref[...], nw_ref[...], mod_ref[...]).astype(BF16)
    if tiles_per_seq > 1:
        t = pl.program_id(0) % tiles_per_seq
        xh = jnp.concatenate([xp_ref[...], xn_ref[...]], axis=0)
        hh = _modnorm(xh, nw_ref[...], mod_ref[...]).astype(BF16)
    for j in range(SSD_XBC // D):
        c0 = SSD_INNER + j * D
        cols = slice(j * D, (j + 1) * D)
        xpad_ref[HL:HL + tm, cols] = jnp.dot(hb, w_ref[:, c0:c0 + D], preferred_element_type=F32)
        if tiles_per_seq > 1:
            rawh = jnp.dot(hh, w_ref[:, c0:c0 + D], preferred_element_type=F32)
            xpad_ref[0:HL, cols] = jnp.where(t > 0, rawh[0:HL], 0.0)
            xpad_ref[HL + tm:HL + tm + HL, cols] = jnp.where(t < tiles_per_seq - 1, rawh[HL:2 * HL], 0.0)
        else:
            xpad_ref[0:HL, cols] = jnp.zeros((HL, D), F32)
            xpad_ref[HL + tm:HL + tm + HL, cols] = jnp.zeros((HL, D), F32)
        for cb in range(D // LANES):
            lc = slice(j * D + cb * LANES, j * D + (cb + 1) * LANES)
            for rb in range(tm // RB):
                acc = jnp.zeros((RB, LANES), F32)
                for k in range(SSD_CONV):
                    r0 = rb * RB + HL - SSD_CONV // 2 + k
                    acc = acc + xpad_ref[r0:r0 + RB, lc] * wc_ref[k:k + 1, lc]
                xbc_ref[rb * RB:(rb + 1) * RB, lc] = _silu(acc + bc_ref[:, lc])
    for j in range(SSD_INNER // D):
        sz_ref[:, j * D:(j + 1) * D] = _silu(jnp.dot(hb, w_ref[:, j * D:(j + 1) * D], preferred_element_type=F32))
    c0 = SSD_INNER + SSD_XBC
    raw = jnp.dot(hb, w_ref[:, c0:c0 + LANES], preferred_element_type=F32) + dtb_ref[...]
    lane = lax.broadcasted_iota(jnp.int32, (1, LANES), 1)
    dt_ref[...] = jnp.where(lane < 2 * SSD_HEADS, _softplus(raw), 0.0)


def _ssd_in(x, seq_len, layer, row_fn, mod, nw, w_in, dt_bias, w_conv, b_conv):
    M = x.shape[0]
    tm = TOKEN_TILE
    HL = SSD_HALO
    tps = seq_len // tm
    n_hblk = M // HL
    row = lambda i: (i, 0)
    in_specs = [pl.BlockSpec((tm, D), row)]
    args = [x]
    if tps > 1:
        in_specs += [pl.BlockSpec((HL, D), lambda i: (jnp.maximum(i * (tm // HL) - 1, 0), 0)),
                     pl.BlockSpec((HL, D), lambda i: (jnp.minimum((i + 1) * (tm // HL), n_hblk - 1), 0))]
        args += [x, x]
    in_specs += [_const_spec((1, D)), _mod_spec(layer, lambda i: row_fn(i * tm)),
                 _const_spec((D, SSD_IN_COLS)), _const_spec((1, LANES)),
                 _const_spec((SSD_CONV + 1, SSD_XBC)), _const_spec((1, SSD_XBC))]
    args += [nw, mod, w_in, dt_bias, w_conv, b_conv]
    return pl.pallas_call(
        functools.partial(_ssd_in_kernel, tps),
        grid=(M // tm,),
        in_specs=in_specs,
        out_specs=[pl.BlockSpec((tm, SSD_INNER), row), pl.BlockSpec((tm, SSD_XBC), row), pl.BlockSpec((tm, LANES), row)],
        out_shape=[jax.ShapeDtypeStruct((M, SSD_INNER), F32), jax.ShapeDtypeStruct((M, SSD_XBC), F32),
                   jax.ShapeDtypeStruct((M, LANES), F32)],
        scratch_shapes=[pltpu.VMEM((tm + 2 * HL, SSD_XBC), F32)],
        compiler_params=_cparams(1),
        name="ssd_in",
    )(*args)


def _cumsum_rows(a, reverse):
    n = a.shape[0]
    row = lax.broadcasted_iota(jnp.int32, a.shape, 0)
    k = 1
    while k < n:
        if reverse:
            a = a + jnp.where(row < n - k, pltpu.roll(a, n - k, 0), 0.0)
        else:
            a = a + jnp.where(row >= k, pltpu.roll(a, k, 0), 0.0)
        k *= 2
    return a


def _ssd_core_kernel(nc, has_h0, emit_state, *refs):
    refs = list(refs)
    dir_refs = [refs[0:2], refs[2:4]]
    alog_ref, dsk_ref = refs[4:6]
    refs = refs[6:]
    if has_h0:
        h0_ref = refs.pop(0)
    y_refs = [refs.pop(0), refs.pop(0)]
    if emit_state:
        so_ref = refs.pop(0)
    (st_ref,) = refs
    Q, N, P = SSD_CHUNK, SSD_STATE, SSD_HEAD_DIM
    c = pl.program_id(1)
    n_blk = SSD_INNER // LANES

    @pl.when(c == 0)
    def _init():
        if has_h0:
            for dr in range(2):
                for j in range(n_blk):
                    st_ref[dr, :, j * LANES:(j + 1) * LANES] = h0_ref[dr, j * LANES:(j + 1) * LANES, :].T
        else:
            st_ref[...] = jnp.zeros(st_ref.shape, F32)

    lo = _head_masks()
    A = -jnp.exp(alog_ref[...])
    ri = lax.broadcasted_iota(jnp.int32, (Q, Q), 0)
    ci = lax.broadcasted_iota(jnp.int32, (Q, Q), 1)

    for dr in range(2):
        x_ref, dt_ref = dir_refs[dr]
        y_ref = y_refs[dr]

        def conv_cols(c0, x_ref=x_ref):
            return x_ref[:, c0:c0 + LANES]

        dt = dt_ref[...]
        cum = _cumsum_rows(dt * A, reverse=(dr == 1))
        cum_t = cum.T
        tot_row = Q - 1 if dr == 0 else 0
        mask = (ri >= ci) if dr == 0 else (ri <= ci)

        for g in range(SSD_GROUPS):
            bg_t = conv_cols(SSD_INNER + g * N).T.astype(BF16)
            cg = conv_cols(SSD_INNER + SSD_GN + g * N).astype(BF16)
            cb = jnp.dot(cg, bg_t, preferred_element_type=F32)
            for pp in range(SSD_HEADS // SSD_GROUPS // 2):
                head = g * (SSD_HEADS // SSD_GROUPS) + 2 * pp
                c0 = head * P
                li = dr * SSD_HEADS + head
                xpair = conv_cols(c0)
                dt_cols = [dt[:, li:li + 1], dt[:, li + 1:li + 2]]
                cum_cols = [cum[:, li:li + 1], cum[:, li + 1:li + 2]]
                xdt = xpair * jnp.where(lo, dt_cols[0], dt_cols[1])
                xdt_b = xdt.astype(BF16)
                cum_x = jnp.where(lo, cum_cols[0], cum_cols[1])
                yd = []
                for hh in range(2):
                    seg = cum_cols[hh] - cum_t[li + hh:li + hh + 1, :]
                    decay = jnp.exp(jnp.where(mask, seg, -jnp.inf))
                    yd.append(jnp.dot((cb * decay).astype(BF16), xdt_b, preferred_element_type=F32))
                st = st_ref[dr, :, c0:c0 + LANES]
                y = (jnp.where(lo, yd[0], yd[1])
                     + jnp.dot(cg, st.astype(BF16), preferred_element_type=F32) * jnp.exp(cum_x))
                if dr == 0:
                    y = y + xpair * dsk_ref[:, c0:c0 + LANES]
                y_ref[:, c0:c0 + LANES] = y
                tot = cum_x[tot_row:tot_row + 1, :]
                xw = (xdt * jnp.exp(tot - cum_x)).astype(BF16)
                st_ref[dr, :, c0:c0 + LANES] = st * jnp.exp(tot) + jnp.dot(bg_t, xw, preferred_element_type=F32)

    if emit_state:
        @pl.when(c == nc - 1)
        def _emit():
            for dr in range(2):
                for j in range(n_blk):
                    so_ref[dr, j * LANES:(j + 1) * LANES, :] = st_ref[dr, :, j * LANES:(j + 1) * LANES].T


def _ssd_core(xbc, dtp, a_log, d_skip, h0, n_seq, seq_len, emit_state):
    Q = SSD_CHUNK
    nc = seq_len // Q
    M = n_seq * seq_len

    def chunk(dr):
        return (lambda b, c: b * nc + c) if dr == 0 else (lambda b, c: b * nc + nc - 1 - c)

    in_specs, args = [], []
    for dr in range(2):
        ch = chunk(dr)
        in_specs += [pl.BlockSpec((Q, SSD_XBC), lambda b, c, ch=ch: (ch(b, c), 0)),
                     pl.BlockSpec((Q, LANES), lambda b, c, ch=ch: (ch(b, c), 0))]
        args += [xbc, dtp]
    in_specs += [_const_spec((1, LANES)), _const_spec((1, SSD_INNER))]
    args += [a_log, d_skip]
    st_blk = pl.BlockSpec((None, 2, SSD_INNER, SSD_STATE), lambda b, c: (b, 0, 0, 0))
    has_h0 = h0 is not None
    if has_h0:
        in_specs.append(st_blk)
        args.append(h0)
    out_specs = [pl.BlockSpec((Q, SSD_INNER), lambda b, c: (chunk(0)(b, c), 0)),
                 pl.BlockSpec((Q, SSD_INNER), lambda b, c: (chunk(1)(b, c), 0))]
    out_shape = [jax.ShapeDtypeStruct((M, SSD_INNER), F32)] * 2
    if emit_state:
        out_specs.append(st_blk)
        out_shape.append(jax.ShapeDtypeStruct((n_seq, 2, SSD_INNER, SSD_STATE), F32))
    return pl.pallas_call(
        functools.partial(_ssd_core_kernel, nc, has_h0, emit_state),
        grid=(n_seq, nc),
        in_specs=in_specs,
        out_specs=out_specs,
        out_shape=out_shape,
        scratch_shapes=[pltpu.VMEM((2, SSD_STATE, SSD_INNER), F32)],
        compiler_params=_cparams(2),
        name="ssd_core",
    )(*args)


def _ssd_out_kernel(yf_ref, yb_ref, sz_ref, x_ref, mod_ref, nw_ref, w_ref, o_ref):
    g = _rms((yf_ref[...] + yb_ref[...]) * sz_ref[...], nw_ref[...]).astype(BF16)
    o_ref[...] = x_ref[...] + mod_ref[:, 2 * D:3 * D] * jnp.dot(g, w_ref[...], preferred_element_type=F32)


def _ssd_out(yf, yb, sz, x, layer, row_fn, mod, norm_w, w_out):
    M = x.shape[0]
    tm = TOKEN_TILE
    row = lambda i: (i, 0)
    wide = pl.BlockSpec((tm, SSD_INNER), row)
    return pl.pallas_call(
        _ssd_out_kernel,
        grid=(M // tm,),
        in_specs=[wide, wide, wide, pl.BlockSpec((tm, D), row), _mod_spec(layer, lambda i: row_fn(i * tm)),
                  _const_spec((1, SSD_INNER)), _const_spec((SSD_INNER, D))],
        out_specs=pl.BlockSpec((tm, D), row),
        out_shape=jax.ShapeDtypeStruct((M, D), F32),
        compiler_params=_cparams(1),
        name="ssd_out",
    )(yf, yb, sz, x, mod, norm_w, w_out)


def _ssd_layer(xc, xl, layer, mod, nw, p, n_lat):
    j = layer // 3
    w_in = jnp.pad(p["ssd_w_in"][j].astype(BF16), ((0, 0), (0, LANES - 2 * SSD_HEADS)))
    dt_bias = jnp.pad(p["ssd_dt_bias"][j].reshape(1, -1), ((0, 0), (0, LANES - 2 * SSD_HEADS)))
    a_log = jnp.pad(p["ssd_a_log"][j].reshape(1, -1), ((0, 0), (0, LANES - 2 * SSD_HEADS)))
    w_conv = jnp.pad(p["ssd_w_conv"][j], ((0, 1), (0, 0)))
    b_conv = p["ssd_b_conv"][j][None]
    d_skip = jnp.repeat(p["ssd_d"][j], SSD_HEAD_DIM)[None]
    norm_w = p["ssd_norm_w"][j][None]
    w_out = p["ssd_w_out"][j].astype(BF16)
    row_ctx = lambda tok: 0
    row_lat = lambda tok: 1 + tok // DEC_SEQ
    n_ctx = xc.shape[0] // SEQ
    szc, xbc_c, dt_c = _ssd_in(xc, SEQ, layer, row_ctx, mod, nw, w_in, dt_bias, w_conv, b_conv)
    szl, xbc_l, dt_l = _ssd_in(xl, DEC_SEQ, layer, row_lat, mod, nw, w_in, dt_bias, w_conv, b_conv)
    yfc, ybc, s_new = _ssd_core(xbc_c, dt_c, a_log, d_skip, None, n_ctx, SEQ, True)
    h0 = p["state_ssm"][:n_lat, j].reshape(n_lat, 2, SSD_INNER, SSD_STATE)
    yfl, ybl = _ssd_core(xbc_l, dt_l, a_log, d_skip, h0, n_lat, DEC_SEQ, False)
    oc = _ssd_out(yfc, ybc, szc, xc, layer, row_ctx, mod, norm_w, w_out)
    ol = _ssd_out(yfl, ybl, szl, xl, layer, row_lat, mod, norm_w, w_out)
    return oc, ol, s_new


def kernel(x_prompt, x_sample, cache_k, cache_v, state_ssm, c, c_ctx, ada_w, ada_b, norm_w, final_norm_w, conv_w_in, conv_b_in, conv_w_dw, conv_b_dw, conv_ln_w, conv_ln_b, conv_w_out, conv_b_out, na_w_in, na_rpb, na_w_out, ssd_w_in, ssd_w_conv, ssd_b_conv, ssd_a_log, ssd_dt_bias, ssd_d, ssd_norm_w, ssd_w_out):
    p = dict(cache_k=cache_k, cache_v=cache_v, state_ssm=state_ssm,
             conv_w_in=conv_w_in, conv_b_in=conv_b_in, conv_w_dw=conv_w_dw, conv_b_dw=conv_b_dw,
             conv_ln_w=conv_ln_w, conv_ln_b=conv_ln_b, conv_w_out=conv_w_out, conv_b_out=conv_b_out,
             na_w_in=na_w_in, na_rpb=na_rpb, na_w_out=na_w_out,
             ssd_w_in=ssd_w_in, ssd_w_conv=ssd_w_conv, ssd_b_conv=ssd_b_conv, ssd_a_log=ssd_a_log,
             ssd_dt_bias=ssd_dt_bias, ssd_d=ssd_d, ssd_norm_w=ssd_norm_w, ssd_w_out=ssd_w_out)
    n_ctx, n_lat = x_prompt.shape[0], x_sample.shape[0]
    assert x_prompt.shape[1:] == (SEQ, D) and x_sample.shape[1:] == (DEC_SEQ, D)
    assert (DEPTH - 1) % 3 == 0, "the final RMSNorm is fused into a trailing Conformer layer"
    xc = x_prompt.reshape(n_ctx * SEQ, D)
    xl = x_sample.reshape(n_lat * DEC_SEQ, D)
    cond8 = jnp.concatenate([c_ctx[None], c, jnp.zeros((8 - 1 - n_lat, D), F32)], axis=0)
    mod = _modulation(cond8, ada_w, ada_b).reshape(DEPTH, 8, 1, 3 * D)
    row_ctx = lambda tok: 0
    row_lat = lambda tok: 1 + tok // DEC_SEQ
    new_k, new_v, new_s = [], [], []
    for i in range(DEPTH):
        kind, j = i % 3, i // 3
        nw = norm_w[i][None]
        if kind == 0:
            cp = _prep_conv(p, j)
            fw = final_norm_w[None] if i == DEPTH - 1 else None
            xc = _conv_layer(xc, SEQ, i, row_ctx, mod, nw, *cp, fw)
            xl = _conv_layer(xl, DEC_SEQ, i, row_lat, mod, nw, *cp, fw)
        elif kind == 1:
            xc, xl, k_c, v_c = _na_layer(xc, xl, i, mod, nw, p, n_lat)
            new_k.append(k_c.reshape(n_ctx, SEQ, NA_HEADS, NA_HEAD_DIM))
            new_v.append(v_c.reshape(n_ctx, SEQ, NA_HEADS, NA_HEAD_DIM))
        else:
            xc, xl, s_c = _ssd_layer(xc, xl, i, mod, nw, p, n_lat)
            new_s.append(s_c.reshape(n_ctx, 2, SSD_HEADS, SSD_HEAD_DIM, SSD_STATE))
    return (xc.reshape(n_ctx, SEQ, D), xl.reshape(n_lat, DEC_SEQ, D),
            jnp.stack(new_k, axis=1), jnp.stack(new_v, axis=1), jnp.stack(new_s, axis=1))
```

```python
import functools

import jax
import jax.numpy as jnp
from jax import lax
from jax.experimental import pallas as pl
from jax.experimental.pallas import tpu as pltpu

F32 = jnp.float32
BF16 = jnp.bfloat16

D = 1024
DEPTH = 4
SEQ = 256
DEC_SEQ = 2048
GRID_W = 64
RMS_EPS = 1e-6
LN_EPS = 1e-5
CONF_K = 31
NA_HEADS = 16
NA_HEAD_DIM = 64
NA_KH = 8
NA_KW = 16
NEG_INF = -1e30
SSD_INNER = 2048
SSD_HEADS = 32
SSD_HEAD_DIM = 64
SSD_STATE = 128
SSD_GROUPS = 4
SSD_GN = SSD_GROUPS * SSD_STATE
SSD_CONV = 7
SSD_CHUNK = 128
SSD_XBC = SSD_INNER + 2 * SSD_GN

LANES = 128
SUBLANES = 8
TOKEN_TILE = 256
CONV_HALO = 16
SSD_HALO = 8
VMEM_LIMIT = 56 * 1024 * 1024


def _cparams(n_axes):
    return pltpu.CompilerParams(dimension_semantics=("arbitrary",) * n_axes,
                                vmem_limit_bytes=VMEM_LIMIT)


def _const_spec(shape):
    nd = len(shape)
    return pl.BlockSpec(shape, lambda *_: (0,) * nd, pipeline_mode=pl.Buffered(1))


def _sigmoid(x):
    return 1.0 / (1.0 + jnp.exp(-x))


def _silu(x):
    return x * _sigmoid(x)


def _rms(x, w):
    ms = jnp.mean(x * x, axis=-1, keepdims=True)
    return x * lax.rsqrt(ms + RMS_EPS) * w


def _modnorm(x, nw, mod):
    return _rms(x, nw) * (1.0 + mod[:, D:2 * D]) + mod[:, :D]


def _mod_spec(layer, row_fn):
    return pl.BlockSpec((None, None, 1, 3 * D), lambda i, *_: (layer, row_fn(i), 0, 0))


def _mod_kernel(cond_ref, w_ref, b_ref, o_ref):
    s = _silu(cond_ref[...]).astype(BF16)
    o_ref[...] = jnp.dot(s, w_ref[...].astype(BF16), preferred_element_type=F32) + b_ref[...]


def _modulation(cond8, ada_w, ada_b):
    tn = 1024
    return pl.pallas_call(
        _mod_kernel,
        grid=(DEPTH, 3 * D // tn),
        in_specs=[pl.BlockSpec((8, D), lambda i, j: (0, 0)),
                  pl.BlockSpec((None, D, tn), lambda i, j: (i, 0, j)),
                  pl.BlockSpec((None, 1, tn), lambda i, j: (i, 0, j))],
        out_specs=pl.BlockSpec((None, 8, tn), lambda i, j: (i, 0, j)),
        out_shape=jax.ShapeDtypeStruct((DEPTH, 8, 3 * D), F32),
        compiler_params=_cparams(2),
        name="adaln_mod",
    )(cond8, ada_w, ada_b.reshape(DEPTH, 1, 3 * D))


CONV_CHUNK = 256


def _conv_kernel(tiles_per_seq, final, *refs):
    refs = list(refs)
    x_ref = refs.pop(0)
    if tiles_per_seq > 1:
        xp_ref = refs.pop(0)
        xn_ref = refs.pop(0)
    nw_ref, mod_ref, wi_ref, bi_ref, wdw_ref, bdw_ref, lnw_ref, lnb_ref, wo_ref, bo_ref = refs[:10]
    refs = refs[10:]
    if final:
        fw_ref = refs.pop(0)
    o_ref, ubuf, cbuf = refs
    T = x_ref.shape[0]
    H = CONV_HALO
    RB = 128

    hb = _modnorm(x_ref[...], nw_ref[...], mod_ref[...]).astype(BF16)
    if tiles_per_seq > 1:
        t = pl.program_id(0) % tiles_per_seq
        hh = _modnorm(jnp.concatenate([xp_ref[...], xn_ref[...]], axis=0), nw_ref[...], mod_ref[...]).astype(BF16)

    def glu(h, cols):
        v = jnp.dot(h, wi_ref[:, cols], preferred_element_type=F32) + bi_ref[:, cols]
        gcols = slice(D + cols.start, D + cols.stop)
        g = jnp.dot(h, wi_ref[:, gcols], preferred_element_type=F32) + bi_ref[:, gcols]
        return v * _sigmoid(g)

    for ch in range(D // CONV_CHUNK):
        cols = slice(ch * CONV_CHUNK, (ch + 1) * CONV_CHUNK)
        u_main = glu(hb, cols)
        if tiles_per_seq > 1:
            uh = glu(hh, cols)
            u_prev = jnp.where(t > 0, uh[0:H], 0.0)
            u_next = jnp.where(t < tiles_per_seq - 1, uh[H:2 * H], 0.0)
        else:
            u_prev = u_next = jnp.zeros((H, CONV_CHUNK), F32)
        for cb in range(CONV_CHUNK // LANES):
            blk = ch * (CONV_CHUNK // LANES) + cb
            lc = slice(blk * LANES, (blk + 1) * LANES)
            cc = slice(cb * LANES, (cb + 1) * LANES)
            ubuf[blk, 0:H, :] = u_prev[:, cc]
            ubuf[blk, H:H + T, :] = u_main[:, cc]
            ubuf[blk, H + T:H + T + H, :] = u_next[:, cc]
            for rb in range(T // RB):
                acc = jnp.zeros((RB, LANES), F32)
                for k in range(CONF_K):
                    r0 = rb * RB + H - CONF_K // 2 + k
                    acc = acc + ubuf[blk, r0:r0 + RB, :] * wdw_ref[k:k + 1, lc]
                cbuf[rb * RB:(rb + 1) * RB, lc] = acc + bdw_ref[:, lc]

    sz = _silu(jnp.dot(hb, wi_ref[:, 2 * D:3 * D], preferred_element_type=F32) + bi_ref[:, 2 * D:3 * D])
    c = cbuf[...]
    mu = jnp.mean(c, axis=-1, keepdims=True)
    xc = c - mu
    var = jnp.mean(xc * xc, axis=-1, keepdims=True)
    ln = xc * lax.rsqrt(var + LN_EPS) * lnw_ref[...] + lnb_ref[...]
    g = (_silu(ln) * sz).astype(BF16)
    y = jnp.dot(g, wo_ref[...], preferred_element_type=F32) + bo_ref[...]
    xn = x_ref[...] + mod_ref[:, 2 * D:3 * D] * y
    if final:
        xn = _rms(xn, fw_ref[...])
    o_ref[...] = xn


def _conv_layer(x, seq_len, layer, row_fn, mod, nw, w_in, b_in, w_dw, b_dw, ln_w, ln_b, w_out, b_out, final_w):
    M = x.shape[0]
    T = TOKEN_TILE
    tps = seq_len // T
    H = CONV_HALO
    n_halo_blocks = M // H
    row = lambda i: (i, 0)
    in_specs = [pl.BlockSpec((T, D), row)]
    args = [x]
    if tps > 1:
        in_specs += [pl.BlockSpec((H, D), lambda i: (jnp.maximum(i * (T // H) - 1, 0), 0)),
                     pl.BlockSpec((H, D), lambda i: (jnp.minimum((i + 1) * (T // H), n_halo_blocks - 1), 0))]
        args += [x, x]
    in_specs += [_const_spec((1, D)), _mod_spec(layer, lambda i: row_fn(i * T)),
                 _const_spec((D, 3 * D)), _const_spec((1, 3 * D)),
                 _const_spec((CONF_K + 1, D)), _const_spec((1, D)), _const_spec((1, D)), _const_spec((1, D)),
                 _const_spec((D, D)), _const_spec((1, D))]
    args += [nw, mod, w_in, b_in, w_dw, b_dw, ln_w, ln_b, w_out, b_out]
    final = final_w is not None
    if final:
        in_specs.append(_const_spec((1, D)))
        args.append(final_w)
    return pl.pallas_call(
        functools.partial(_conv_kernel, tps, final),
        grid=(M // T,),
        in_specs=in_specs,
        out_specs=pl.BlockSpec((T, D), row),
        out_shape=jax.ShapeDtypeStruct((M, D), F32),
        scratch_shapes=[pltpu.VMEM((D // LANES, T + 2 * H, LANES), F32), pltpu.VMEM((T, D), F32)],
        compiler_params=_cparams(1),
        name="conv_layer",
    )(*args)


def _prep_conv(p, j):
    w_dw = jnp.pad(p["conv_w_dw"][j], ((0, 1), (0, 0)))
    return (p["conv_w_in"][j].astype(BF16), p["conv_b_in"][j][None], w_dw, p["conv_b_dw"][j][None],
            p["conv_ln_w"][j][None], p["conv_ln_b"][j][None], p["conv_w_out"][j].astype(BF16), p["conv_b_out"][j][None])


def _out_proj_kernel(g_ref, x_ref, mod_ref, w_ref, o_ref):
    y = jnp.dot(g_ref[...].astype(BF16), w_ref[...], preferred_element_type=F32)
    o_ref[...] = x_ref[...] + mod_ref[:, 2 * D:3 * D] * y


def _out_proj(g, x, layer, row_fn, mod, w_out):
    M, K = g.shape
    tm = TOKEN_TILE
    row = lambda i: (i, 0)
    return pl.pallas_call(
        _out_proj_kernel,
        grid=(M // tm,),
        in_specs=[pl.BlockSpec((tm, K), row), pl.BlockSpec((tm, D), row), _mod_spec(layer, lambda i: row_fn(i * tm)),
                  _const_spec((K, D))],
        out_specs=pl.BlockSpec((tm, D), row),
        out_shape=jax.ShapeDtypeStruct((M, D), F32),
        compiler_params=_cparams(1),
        name="out_proj",
    )(g, x, mod, w_out)


NA_SCALE = NA_HEAD_DIM ** -0.5
NA_ROWS = 32
NA_WIN = NA_KH * GRID_W
NT_DIMS = (((1,), (1,)), ((), ()))


def _na_in_kernel(x_ref, nw_ref, mod_ref, w_ref, q_ref, k_ref, v_ref, sz_ref):
    hb = _modnorm(x_ref[...], nw_ref[...], mod_ref[...]).astype(BF16)
    q_ref[...] = jnp.dot(hb, w_ref[:, 0:D], preferred_element_type=F32).astype(q_ref.dtype)
    k_ref[...] = jnp.dot(hb, w_ref[:, D:2 * D], preferred_element_type=F32).astype(k_ref.dtype)
    v_ref[...] = jnp.dot(hb, w_ref[:, 2 * D:3 * D], preferred_element_type=F32).astype(v_ref.dtype)
    sz_ref[...] = _silu(jnp.dot(hb, w_ref[:, 3 * D:4 * D], preferred_element_type=F32))


def _na_in(x, layer, row_fn, mod, nw, w_in, kv_dtype):
    M = x.shape[0]
    tm = TOKEN_TILE
    row = lambda i: (i, 0)
    blk = pl.BlockSpec((tm, D), row)
    return pl.pallas_call(
        _na_in_kernel,
        grid=(M // tm,),
        in_specs=[blk, _const_spec((1, D)), _mod_spec(layer, lambda i: row_fn(i * tm)), _const_spec((D, 4 * D))],
        out_specs=[blk, blk, blk, blk],
        out_shape=[jax.ShapeDtypeStruct((M, D), BF16), jax.ShapeDtypeStruct((M, D), kv_dtype),
                   jax.ShapeDtypeStruct((M, D), kv_dtype), jax.ShapeDtypeStruct((M, D), F32)],
        compiler_params=_cparams(1),
        name="na_in",
    )(x, nw, mod, w_in)


def _head_masks():
    lane = lax.broadcasted_iota(jnp.int32, (1, LANES), 1)
    return lane < NA_HEAD_DIM


def _na_ctx_kernel(q_ref, k_ref, v_ref, sz_ref, o_ref):
    lo = _head_masks()
    for p in range(NA_HEADS // 2):
        cols = slice(p * LANES, (p + 1) * LANES)
        qp = q_ref[:, cols]
        kp = k_ref[:, cols].astype(BF16)
        vp = v_ref[:, cols].astype(BF16)
        n = qp.shape[0]
        zero = jnp.zeros_like(qp)
        qs = jnp.concatenate([jnp.where(lo, qp, zero), jnp.where(lo, zero, qp)], axis=0)
        s = lax.dot_general(qs, kp, NT_DIMS, preferred_element_type=F32) * NA_SCALE
        e = jnp.exp(s - jnp.max(s, axis=-1, keepdims=True))
        l = jnp.sum(e, axis=-1, keepdims=True)
        o = jnp.dot(e.astype(BF16), vp, preferred_element_type=F32) / l
        o_ref[:, cols] = jnp.where(lo, o[:n], o[n:]) * sz_ref[:, cols]


def _na_ctx(q, k, v, sz, seq_len):
    M = q.shape[0]
    blk = pl.BlockSpec((seq_len, D), lambda i: (i, 0))
    return pl.pallas_call(
        _na_ctx_kernel,
        grid=(M // seq_len,),
        in_specs=[blk, blk, blk, blk],
        out_specs=blk,
        out_shape=jax.ShapeDtypeStruct((M, D), F32),
        compiler_params=_cparams(1),
        name="na_ctx_attn",
    )(q, k, v, sz)


def _rpb_kernel(rpb_ref, o_ref):
    h = pl.program_id(0)
    c = lax.broadcasted_iota(jnp.int32, (GRID_W, LANES), 0)
    lane = lax.broadcasted_iota(jnp.int32, (GRID_W, LANES), 1)
    kc = lane & (GRID_W - 1)
    upper = lax.broadcasted_iota(jnp.int32, (1, LANES), 1) >= GRID_W
    cs = jnp.clip(c - NA_KW // 2, 0, GRID_W - NA_KW)
    valid = (kc >= cs) & (kc < cs + NA_KW)
    d = kc - c
    n_dc = 2 * NA_KW - 1
    n_pair = 2 * NA_KH - 2
    for dri in range(n_pair):
        tile = jnp.zeros((GRID_W, LANES), F32)
        for dd in range(-(NA_KW - 1), NA_KW):
            base = h * ((2 * NA_KH - 1) * n_dc) + dri * n_dc + (dd + NA_KW - 1)
            tile = jnp.where(d == dd, jnp.where(upper, rpb_ref[base + n_dc], rpb_ref[base]), tile)
        tile = jnp.where(valid, tile, NEG_INF)
        for idx in range(NA_KH):
            jj2 = dri - (NA_KH - 1) + idx
            if jj2 % 2 == 0 and 0 <= jj2 // 2 < NA_KH // 2:
                jj = jj2 // 2
                o_ref[idx, :, jj * LANES:(jj + 1) * LANES] = tile


def _rpb_table(rpb):
    return pl.pallas_call(
        _rpb_kernel,
        grid=(NA_HEADS,),
        in_specs=[pl.BlockSpec(memory_space=pltpu.SMEM)],
        out_specs=pl.BlockSpec((NA_KH, None, GRID_W, NA_WIN), lambda h: (0, h, 0, 0)),
        out_shape=jax.ShapeDtypeStruct((NA_KH, NA_HEADS, GRID_W, NA_WIN), F32),
        compiler_params=_cparams(1),
        name="na_rpb_table",
    )(rpb.reshape(-1))


def _na_row_start(r):
    return jnp.clip(r - NA_KH // 2, 0, NA_ROWS - NA_KH)


def _na_lat_kernel(q_ref, k_ref, v_ref, ck_ref, cv_ref, bias_ref, sz_ref, o_ref):
    lo = _head_masks()
    k0 = pl.multiple_of(_na_row_start(pl.program_id(1)) * GRID_W, GRID_W)
    for p in range(NA_HEADS // 2):
        cols = slice(p * LANES, (p + 1) * LANES)
        qp = q_ref[:, cols]
        kw = k_ref[pl.ds(k0, NA_WIN), cols]
        vw = v_ref[pl.ds(k0, NA_WIN), cols]
        ck = ck_ref[:, cols]
        cv = cv_ref[:, cols]
        zero = jnp.zeros_like(qp)
        qs = jnp.concatenate([jnp.where(lo, qp, zero), jnp.where(lo, zero, qp)], axis=0)
        bias = bias_ref[2 * p:2 * p + 2].reshape(2 * GRID_W, NA_WIN)
        s_loc = lax.dot_general(qs, kw, NT_DIMS, preferred_element_type=F32) * NA_SCALE + bias
        s_ctx = lax.dot_general(qs, ck, NT_DIMS, preferred_element_type=F32) * NA_SCALE
        m = jnp.maximum(jnp.max(s_loc, axis=-1, keepdims=True), jnp.max(s_ctx, axis=-1, keepdims=True))
        e_loc = jnp.exp(s_loc - m)
        e_ctx = jnp.exp(s_ctx - m)
        l = jnp.sum(e_loc, axis=-1, keepdims=True) + jnp.sum(e_ctx, axis=-1, keepdims=True)
        o = (jnp.dot(e_loc.astype(BF16), vw, preferred_element_type=F32)
             + jnp.dot(e_ctx.astype(BF16), cv, preferred_element_type=F32)) / l
        o_ref[:, cols] = jnp.where(lo, o[:GRID_W], o[GRID_W:]) * sz_ref[:, cols]


def _na_lat(q, k, v, ck, cv, bias, sz, n_lat):
    T = NA_ROWS * GRID_W
    past = ck.shape[1]
    qblk = pl.BlockSpec((GRID_W, D), lambda b, r: (b * NA_ROWS + r, 0))
    kvblk = pl.BlockSpec((T, D), lambda b, r: (b, 0))
    cblk = pl.BlockSpec((None, past, D), lambda b, r: (b, 0, 0))
    bblk = pl.BlockSpec((None, NA_HEADS, GRID_W, NA_WIN), lambda b, r: (r - _na_row_start(r), 0, 0, 0))
    return pl.pallas_call(
        _na_lat_kernel,
        grid=(n_lat, NA_ROWS),
        in_specs=[qblk, kvblk, kvblk, cblk, cblk, bblk, qblk],
        out_specs=qblk,
        out_shape=jax.ShapeDtypeStruct((n_lat * T, D), F32),
        compiler_params=_cparams(2),
        name="na_lat_attn",
    )(q, k, v, ck, cv, bias, sz)


def _na_layer(xc, xl, layer, mod, nw, p, n_lat):
    j = layer // 3
    w_in = p["na_w_in"][j].astype(BF16)
    w_out = p["na_w_out"][j].astype(BF16)
    row_ctx = lambda tok: 0
    row_lat = lambda tok: 1 + tok // (NA_ROWS * GRID_W)
    qc, kc, vc, szc = _na_in(xc, layer, row_ctx, mod, nw, w_in, F32)
    ql, kl, vl, szl = _na_in(xl, layer, row_lat, mod, nw, w_in, BF16)
    gc = _na_ctx(qc, kc, vc, szc, SEQ)
    past = p["cache_k"].shape[2]
    ck = p["cache_k"][:n_lat, j].reshape(n_lat, past, D).astype(BF16)
    cv = p["cache_v"][:n_lat, j].reshape(n_lat, past, D).astype(BF16)
    gl = _na_lat(ql, kl, vl, ck, cv, _rpb_table(p["na_rpb"][j]), szl, n_lat)
    return (_out_proj(gc, xc, layer, row_ctx, mod, w_out), _out_proj(gl, xl, layer, row_lat, mod, w_out), kc, vc)


SSD_IN_COLS = SSD_INNER + SSD_XBC + LANES


def _softplus(x):
    return jnp.maximum(x, 0.0) + jnp.log(1.0 + jnp.exp(-jnp.abs(x)))


def _ssd_in_kernel(tiles_per_seq, *refs):
    refs = list(refs)
    x_ref = refs.pop(0)
    if tiles_per_seq > 1:
        xp_ref = refs.pop(0)
        xn_ref = refs.pop(0)
    nw_ref, mod_ref, w_ref, dtb_ref, wc_ref, bc_ref, sz_ref, xbc_ref, dt_ref, xpad_ref = refs
    tm = x_ref.shape[0]
    HL = SSD_HALO
    RB = 128
    hb = _modnorm(x_ref[...], nw_ref[...], mod_ref[...]).astype(BF16)
    if tiles_per_seq > 1:
        t = pl.program_id(0) % tiles_per_seq
        xh = jnp.concatenate([xp_ref[...], xn_ref[...]], axis=0)
        hh = _modnorm(xh, nw_ref[...], mod_ref[...]).astype(BF16)
    for j in range(SSD_XBC // D):
        c0 = SSD_INNER + j * D
        raw = jnp.dot(hb, w_ref[:, c0:c0 + D], preferred_element_type=F32)
        if tiles_per_seq > 1:
            rawh = jnp.dot(hh, w_ref[:, c0:c0 + D], preferred_element_type=F32)
            raw_prev = jnp.where(t > 0, rawh[0:HL], 0.0)
            raw_next = jnp.where(t < tiles_per_seq - 1, rawh[HL:2 * HL], 0.0)
        else:
            raw_prev = raw_next = jnp.zeros((HL, D), F32)
        for cb in range(D // LANES):
            blk = j * (D // LANES) + cb
            lc = slice(blk * LANES, (blk + 1) * LANES)
            cc = slice(cb * LANES, (cb + 1) * LANES)
            xpad_ref[blk, 0:HL, :] = raw_prev[:, cc]
            xpad_ref[blk, HL:HL + tm, :] = raw[:, cc]
            xpad_ref[blk, HL + tm:HL + tm + HL, :] = raw_next[:, cc]
            for rb in range(tm // RB):
                acc = jnp.zeros((RB, LANES), F32)
                for k in range(SSD_CONV):
                    r0 = rb * RB + HL - SSD_CONV // 2 + k
                    acc = acc + xpad_ref[blk, r0:r0 + RB, :] * wc_ref[k:k + 1, lc]
                xbc_ref[rb * RB:(rb + 1) * RB, lc] = _silu(acc + bc_ref[:, lc])
    for j in range(SSD_INNER // D):
        sz_ref[:, j * D:(j + 1) * D] = _silu(jnp.dot(hb, w_ref[:, j * D:(j + 1) * D], preferred_element_type=F32))
    c0 = SSD_INNER + SSD_XBC
    raw = jnp.dot(hb, w_ref[:, c0:c0 + LANES], preferred_element_type=F32) + dtb_ref[...]
    lane = lax.broadcasted_iota(jnp.int32, (1, LANES), 1)
    dt_ref[...] = jnp.where(lane < 2 * SSD_HEADS, _softplus(raw), 0.0)


def _ssd_in(x, seq_len, layer, row_fn, mod, nw, w_in, dt_bias, w_conv, b_conv):
    M = x.shape[0]
    tm = TOKEN_TILE
    HL = SSD_HALO
    tps = seq_len // tm
    n_hblk = M // HL
    row = lambda i: (i, 0)
    in_specs = [pl.BlockSpec((tm, D), row)]
    args = [x]
    if tps > 1:
        in_specs += [pl.BlockSpec((HL, D), lambda i: (jnp.maximum(i * (tm // HL) - 1, 0), 0)),
                     pl.BlockSpec((HL, D), lambda i: (jnp.minimum((i + 1) * (tm // HL), n_hblk - 1), 0))]
        args += [x, x]
    in_specs += [_const_spec((1, D)), _mod_spec(layer, lambda i: row_fn(i * tm)),
                 _const_spec((D, SSD_IN_COLS)), _const_spec((1, LANES)),
                 _const_spec((SSD_CONV + 1, SSD_XBC)), _const_spec((1, SSD_XBC))]
    args += [nw, mod, w_in, dt_bias, w_conv, b_conv]
    return pl.pallas_call(
        functools.partial(_ssd_in_kernel, tps),
        grid=(M // tm,),
        in_specs=in_specs,
        out_specs=[pl.BlockSpec((tm, SSD_INNER), row), pl.BlockSpec((tm, SSD_XBC), row), pl.BlockSpec((tm, LANES), row)],
        out_shape=[jax.ShapeDtypeStruct((M, SSD_INNER), F32), jax.ShapeDtypeStruct((M, SSD_XBC), F32),
                   jax.ShapeDtypeStruct((M, LANES), F32)],
        scratch_shapes=[pltpu.VMEM((SSD_XBC // LANES, tm + 2 * HL, LANES), F32)],
        compiler_params=_cparams(1),
        name="ssd_in",
    )(*args)


def _cumsum_rows(a, reverse):
    n = a.shape[0]
    row = lax.broadcasted_iota(jnp.int32, a.shape, 0)
    k = 1
    while k < n:
        if reverse:
            a = a + jnp.where(row < n - k, pltpu.roll(a, n - k, 0), 0.0)
        else:
            a = a + jnp.where(row >= k, pltpu.roll(a, k, 0), 0.0)
        k *= 2
    return a


def _ssd_core_kernel(nc, has_h0, emit_state, *refs):
    refs = list(refs)
    dir_refs = [refs[0:2], refs[2:4]]
    alog_ref, dsk_ref = refs[4:6]
    refs = refs[6:]
    if has_h0:
        h0_ref = refs.pop(0)
    y_refs = [refs.pop(0), refs.pop(0)]
    if emit_state:
        so_ref = refs.pop(0)
    (st_ref,) = refs
    Q, N, P = SSD_CHUNK, SSD_STATE, SSD_HEAD_DIM
    c = pl.program_id(1)
    n_blk = SSD_INNER // LANES

    @pl.when(c == 0)
    def _init():
        if has_h0:
            for dr in range(2):
                for j in range(n_blk):
                    st_ref[dr, :, j * LANES:(j + 1) * LANES] = h0_ref[dr, j * LANES:(j + 1) * LANES, :].T
        else:
            st_ref[...] = jnp.zeros(st_ref.shape, F32)

    lo = _head_masks()
    A = -jnp.exp(alog_ref[...])
    ri = lax.broadcasted_iota(jnp.int32, (Q, Q), 0)
    ci = lax.broadcasted_iota(jnp.int32, (Q, Q), 1)

    for dr in range(2):
        x_ref, dt_ref = dir_refs[dr]
        y_ref = y_refs[dr]

        def conv_cols(c0, x_ref=x_ref):
            return x_ref[:, c0:c0 + LANES]

        dt = dt_ref[...]
        cum = _cumsum_rows(dt * A, reverse=(dr == 1))
        cum_t = cum.T
        tot_row = Q - 1 if dr == 0 else 0
        mask = (ri >= ci) if dr == 0 else (ri <= ci)

        for g in range(SSD_GROUPS):
            bg_t = conv_cols(SSD_INNER + g * N).T.astype(BF16)
            cg = conv_cols(SSD_INNER + SSD_GN + g * N).astype(BF16)
            cb = jnp.dot(cg, bg_t, preferred_element_type=F32)
            for pp in range(SSD_HEADS // SSD_GROUPS // 2):
                head = g * (SSD_HEADS // SSD_GROUPS) + 2 * pp
                c0 = head * P
                li = dr * SSD_HEADS + head
                xpair = conv_cols(c0)
                dt_cols = [dt[:, li:li + 1], dt[:, li + 1:li + 2]]
                cum_cols = [cum[:, li:li + 1], cum[:, li + 1:li + 2]]
                xdt = xpair * jnp.where(lo, dt_cols[0], dt_cols[1])
                xdt_b = xdt.astype(BF16)
                cum_x = jnp.where(lo, cum_cols[0], cum_cols[1])
                yd = []
                for hh in range(2):
                    seg = cum_cols[hh] - cum_t[li + hh:li + hh + 1, :]
                    decay = jnp.exp(jnp.where(mask, seg, -jnp.inf))
                    yd.append(jnp.dot((cb * decay).astype(BF16), xdt_b, preferred_element_type=F32))
                st = st_ref[dr, :, c0:c0 + LANES]
                y = (jnp.where(lo, yd[0], yd[1])
                     + jnp.dot(cg, st.astype(BF16), preferred_element_type=F32) * jnp.exp(cum_x))
                if dr == 0:
                    y = y + xpair * dsk_ref[:, c0:c0 + LANES]
                y_ref[:, c0:c0 + LANES] = y
                tot = cum_x[tot_row:tot_row + 1, :]
                xw = (xdt * jnp.exp(tot - cum_x)).astype(BF16)
                st_ref[dr, :, c0:c0 + LANES] = st * jnp.exp(tot) + jnp.dot(bg_t, xw, preferred_element_type=F32)

    if emit_state:
        @pl.when(c == nc - 1)
        def _emit():
            for dr in range(2):
                for j in range(n_blk):
                    so_ref[dr, j * LANES:(j + 1) * LANES, :] = st_ref[dr, :, j * LANES:(j + 1) * LANES].T


def _ssd_core(xbc, dtp, a_log, d_skip, h0, n_seq, seq_len, emit_state):
    Q = SSD_CHUNK
    nc = seq_len // Q
    M = n_seq * seq_len

    def chunk(dr):
        return (lambda b, c: b * nc + c) if dr == 0 else (lambda b, c: b * nc + nc - 1 - c)

    in_specs, args = [], []
    for dr in range(2):
        ch = chunk(dr)
        in_specs += [pl.BlockSpec((Q, SSD_XBC), lambda b, c, ch=ch: (ch(b, c), 0)),
                     pl.BlockSpec((Q, LANES), lambda b, c, ch=ch: (ch(b, c), 0))]
        args += [xbc, dtp]
    in_specs += [_const_spec((1, LANES)), _const_spec((1, SSD_INNER))]
    args += [a_log, d_skip]
    st_blk = pl.BlockSpec((None, 2, SSD_INNER, SSD_STATE), lambda b, c: (b, 0, 0, 0))
    has_h0 = h0 is not None
    if has_h0:
        in_specs.append(st_blk)
        args.append(h0)
    out_specs = [pl.BlockSpec((Q, SSD_INNER), lambda b, c: (chunk(0)(b, c), 0)),
                 pl.BlockSpec((Q, SSD_INNER), lambda b, c: (chunk(1)(b, c), 0))]
    out_shape = [jax.ShapeDtypeStruct((M, SSD_INNER), F32)] * 2
    if emit_state:
        out_specs.append(st_blk)
        out_shape.append(jax.ShapeDtypeStruct((n_seq, 2, SSD_INNER, SSD_STATE), F32))
    return pl.pallas_call(
        functools.partial(_ssd_core_kernel, nc, has_h0, emit_state),
        grid=(n_seq, nc),
        in_specs=in_specs,
        out_specs=out_specs,
        out_shape=out_shape,
        scratch_shapes=[pltpu.VMEM((2, SSD_STATE, SSD_INNER), F32)],
        compiler_params=_cparams(2),
        name="ssd_core",
    )(*args)


def _ssd_out_kernel(yf_ref, yb_ref, sz_ref, x_ref, mod_ref, nw_ref, w_ref, o_ref):
    g = _rms((yf_ref[...] + yb_ref[...]) * sz_ref[...], nw_ref[...]).astype(BF16)
    o_ref[...] = x_ref[...] + mod_ref[:, 2 * D:3 * D] * jnp.dot(g, w_ref[...], preferred_element_type=F32)


def _ssd_out(yf, yb, sz, x, layer, row_fn, mod, norm_w, w_out):
    M = x.shape[0]
    tm = TOKEN_TILE
    row = lambda i: (i, 0)
    wide = pl.BlockSpec((tm, SSD_INNER), row)
    return pl.pallas_call(
        _ssd_out_kernel,
        grid=(M // tm,),
        in_specs=[wide, wide, wide, pl.BlockSpec((tm, D), row), _mod_spec(layer, lambda i: row_fn(i * tm)),
                  _const_spec((1, SSD_INNER)), _const_spec((SSD_INNER, D))],
        out_specs=pl.BlockSpec((tm, D), row),
        out_shape=jax.ShapeDtypeStruct((M, D), F32),
        compiler_params=_cparams(1),
        name="ssd_out",
    )(yf, yb, sz, x, mod, norm_w, w_out)


def _ssd_layer(xc, xl, layer, mod, nw, p, n_lat):
    j = layer // 3
    w_in = jnp.pad(p["ssd_w_in"][j].astype(BF16), ((0, 0), (0, LANES - 2 * SSD_HEADS)))
    dt_bias = jnp.pad(p["ssd_dt_bias"][j].reshape(1, -1), ((0, 0), (0, LANES - 2 * SSD_HEADS)))
    a_log = jnp.pad(p["ssd_a_log"][j].reshape(1, -1), ((0, 0), (0, LANES - 2 * SSD_HEADS)))
    w_conv = jnp.pad(p["ssd_w_conv"][j], ((0, 1), (0, 0)))
    b_conv = p["ssd_b_conv"][j][None]
    d_skip = jnp.repeat(p["ssd_d"][j], SSD_HEAD_DIM)[None]
    norm_w = p["ssd_norm_w"][j][None]
    w_out = p["ssd_w_out"][j].astype(BF16)
    row_ctx = lambda tok: 0
    row_lat = lambda tok: 1 + tok // DEC_SEQ
    n_ctx = xc.shape[0] // SEQ
    szc, xbc_c, dt_c = _ssd_in(xc, SEQ, layer, row_ctx, mod, nw, w_in, dt_bias, w_conv, b_conv)
    szl, xbc_l, dt_l = _ssd_in(xl, DEC_SEQ, layer, row_lat, mod, nw, w_in, dt_bias, w_conv, b_conv)
    yfc, ybc, s_new = _ssd_core(xbc_c, dt_c, a_log, d_skip, None, n_ctx, SEQ, True)
    h0 = p["state_ssm"][:n_lat, j].reshape(n_lat, 2, SSD_INNER, SSD_STATE)
    yfl, ybl = _ssd_core(xbc_l, dt_l, a_log, d_skip, h0, n_lat, DEC_SEQ, False)
    oc = _ssd_out(yfc, ybc, szc, xc, layer, row_ctx, mod, norm_w, w_out)
    ol = _ssd_out(yfl, ybl, szl, xl, layer, row_lat, mod, norm_w, w_out)
    return oc, ol, s_new


def kernel(x_prompt, x_sample, cache_k, cache_v, state_ssm, c, c_ctx, ada_w, ada_b, norm_w, final_norm_w, conv_w_in, conv_b_in, conv_w_dw, conv_b_dw, conv_ln_w, conv_ln_b, conv_w_out, conv_b_out, na_w_in, na_rpb, na_w_out, ssd_w_in, ssd_w_conv, ssd_b_conv, ssd_a_log, ssd_dt_bias, ssd_d, ssd_norm_w, ssd_w_out):
    p = dict(cache_k=cache_k, cache_v=cache_v, state_ssm=state_ssm,
             conv_w_in=conv_w_in, conv_b_in=conv_b_in, conv_w_dw=conv_w_dw, conv_b_dw=conv_b_dw,
             conv_ln_w=conv_ln_w, conv_ln_b=conv_ln_b, conv_w_out=conv_w_out, conv_b_out=conv_b_out,
             na_w_in=na_w_in, na_rpb=na_rpb, na_w_out=na_w_out,
             ssd_w_in=ssd_w_in, ssd_w_conv=ssd_w_conv, ssd_b_conv=ssd_b_conv, ssd_a_log=ssd_a_log,
             ssd_dt_bias=ssd_dt_bias, ssd_d=ssd_d, ssd_norm_w=ssd_norm_w, ssd_w_out=ssd_w_out)
    n_ctx, n_lat = x_prompt.shape[0], x_sample.shape[0]
    assert x_prompt.shape[1:] == (SEQ, D) and x_sample.shape[1:] == (DEC_SEQ, D)
    assert (DEPTH - 1) % 3 == 0, "the final RMSNorm is fused into a trailing Conformer layer"
    xc = x_prompt.reshape(n_ctx * SEQ, D)
    xl = x_sample.reshape(n_lat * DEC_SEQ, D)
    cond8 = jnp.concatenate([c_ctx[None], c, jnp.zeros((8 - 1 - n_lat, D), F32)], axis=0)
    mod = _modulation(cond8, ada_w, ada_b).reshape(DEPTH, 8, 1, 3 * D)
    row_ctx = lambda tok: 0
    row_lat = lambda tok: 1 + tok // DEC_SEQ
    new_k, new_v, new_s = [], [], []
    for i in range(DEPTH):
        kind, j = i % 3, i // 3
        nw = norm_w[i][None]
        if kind == 0:
            cp = _prep_conv(p, j)
            fw = final_norm_w[None] if i == DEPTH - 1 else None
            xc = _conv_layer(xc, SEQ, i, row_ctx, mod, nw, *cp, fw)
            xl = _conv_layer(xl, DEC_SEQ, i, row_lat, mod, nw, *cp, fw)
        elif kind == 1:
            xc, xl, k_c, v_c = _na_layer(xc, xl, i, mod, nw, p, n_lat)
            new_k.append(k_c.reshape(n_ctx, SEQ, NA_HEADS, NA_HEAD_DIM))
            new_v.append(v_c.reshape(n_ctx, SEQ, NA_HEADS, NA_HEAD_DIM))
        else:
            xc, xl, s_c = _ssd_layer(xc, xl, i, mod, nw, p, n_lat)
            new_s.append(s_c.reshape(n_ctx, 2, SSD_HEADS, SSD_HEAD_DIM, SSD_STATE))
    return (xc.reshape(n_ctx, SEQ, D), xl.reshape(n_lat, DEC_SEQ, D),
            jnp.stack(new_k, axis=1), jnp.stack(new_v, axis=1), jnp.stack(new_s, axis=1))
```

```python
import functools

import jax
import jax.numpy as jnp
from jax import lax
from jax.experimental import pallas as pl
from jax.experimental.pallas import tpu as pltpu

F32 = jnp.float32
BF16 = jnp.bfloat16

D = 1024
DEPTH = 4
SEQ = 256
DEC_SEQ = 2048
GRID_W = 64
RMS_EPS = 1e-6
LN_EPS = 1e-5
CONF_K = 31
NA_HEADS = 16
NA_HEAD_DIM = 64
NA_KH = 8
NA_KW = 16
NEG_INF = -1e30
SSD_INNER = 2048
SSD_HEADS = 32
SSD_HEAD_DIM = 64
SSD_STATE = 128
SSD_GROUPS = 4
SSD_GN = SSD_GROUPS * SSD_STATE
SSD_CONV = 7
SSD_CHUNK = 128
SSD_XBC = SSD_INNER + 2 * SSD_GN

LANES = 128
SUBLANES = 8
TOKEN_TILE = 256
CONV_HALO = 16
SSD_HALO = 8
VMEM_LIMIT = 56 * 1024 * 1024


def _cparams(n_axes):
    return pltpu.CompilerParams(dimension_semantics=("arbitrary",) * n_axes,
                                vmem_limit_bytes=VMEM_LIMIT)


def _const_spec(shape):
    nd = len(shape)
    return pl.BlockSpec(shape, lambda *_: (0,) * nd, pipeline_mode=pl.Buffered(1))


def _sigmoid(x):
    return 1.0 / (1.0 + jnp.exp(-x))


def _silu(x):
    return x * _sigmoid(x)


def _rms(x, w):
    ms = jnp.mean(x * x, axis=-1, keepdims=True)
    return x * lax.rsqrt(ms + RMS_EPS) * w


def _modnorm(x, nw, mod):
    return _rms(x, nw) * (1.0 + mod[:, D:2 * D]) + mod[:, :D]


def _mod_spec(layer, row_fn):
    return pl.BlockSpec((None, None, 1, 3 * D), lambda i, *_: (layer, row_fn(i), 0, 0))


def _mod_kernel(cond_ref, w_ref, b_ref, o_ref):
    s = _silu(cond_ref[...]).astype(BF16)
    o_ref[...] = jnp.dot(s, w_ref[...].astype(BF16), preferred_element_type=F32) + b_ref[...]


def _modulation(cond8, ada_w, ada_b):
    tn = 1024
    return pl.pallas_call(
        _mod_kernel,
        grid=(DEPTH, 3 * D // tn),
        in_specs=[pl.BlockSpec((8, D), lambda i, j: (0, 0)),
                  pl.BlockSpec((None, D, tn), lambda i, j: (i, 0, j)),
                  pl.BlockSpec((None, 1, tn), lambda i, j: (i, 0, j))],
        out_specs=pl.BlockSpec((None, 8, tn), lambda i, j: (i, 0, j)),
        out_shape=jax.ShapeDtypeStruct((DEPTH, 8, 3 * D), F32),
        compiler_params=_cparams(2),
        name="adaln_mod",
    )(cond8, ada_w, ada_b.reshape(DEPTH, 1, 3 * D))


CONV_CHUNK = 256


def _conv_kernel(tiles_per_seq, final, *refs):
    refs = list(refs)
    x_ref = refs.pop(0)
    if tiles_per_seq > 1:
        xp_ref = refs.pop(0)
        xn_ref = refs.pop(0)
    nw_ref, mod_ref, wi_ref, bi_ref, wdw_ref, bdw_ref, lnw_ref, lnb_ref, wo_ref, bo_ref = refs[:10]
    refs = refs[10:]
    if final:
        fw_ref = refs.pop(0)
    o_ref, ubuf, cbuf = refs
    T = x_ref.shape[0]
    H = CONV_HALO
    RB = 128

    hb = _modnorm(x_ref[...], nw_ref[...], mod_ref[...]).astype(BF16)
    if tiles_per_seq > 1:
        t = pl.program_id(0) % tiles_per_seq
        hh = _modnorm(jnp.concatenate([xp_ref[...], xn_ref[...]], axis=0), nw_ref[...], mod_ref[...]).astype(BF16)

    def glu(h, cols):
        v = jnp.dot(h, wi_ref[:, cols], preferred_element_type=F32) + bi_ref[:, cols]
        gcols = slice(D + cols.start, D + cols.stop)
        g = jnp.dot(h, wi_ref[:, gcols], preferred_element_type=F32) + bi_ref[:, gcols]
        return v * _sigmoid(g)

    for ch in range(D // CONV_CHUNK):
        cols = slice(ch * CONV_CHUNK, (ch + 1) * CONV_CHUNK)
        u_main = glu(hb, cols)
        if tiles_per_seq > 1:
            uh = glu(hh, cols)
            u_prev = jnp.where(t > 0, uh[0:H], 0.0)
            u_next = jnp.where(t < tiles_per_seq - 1, uh[H:2 * H], 0.0)
        else:
            u_prev = u_next = jnp.zeros((H, CONV_CHUNK), F32)
        for cb in range(CONV_CHUNK // LANES):
            blk = ch * (CONV_CHUNK // LANES) + cb
            lc = slice(blk * LANES, (blk + 1) * LANES)
            cc = slice(cb * LANES, (cb + 1) * LANES)
            ubuf[blk, 0:H, :] = u_prev[:, cc]
            ubuf[blk, H:H + T, :] = u_main[:, cc]
            ubuf[blk, H + T:H + T + H, :] = u_next[:, cc]
            for rb in range(T // RB):
                acc = jnp.zeros((RB, LANES), F32)
                for k in range(CONF_K):
                    r0 = rb * RB + H - CONF_K // 2 + k
                    acc = acc + ubuf[blk, r0:r0 + RB, :] * wdw_ref[k:k + 1, lc]
                cbuf[rb * RB:(rb + 1) * RB, lc] = acc + bdw_ref[:, lc]

    sz = _silu(jnp.dot(hb, wi_ref[:, 2 * D:3 * D], preferred_element_type=F32) + bi_ref[:, 2 * D:3 * D])
    c = cbuf[...]
    mu = jnp.mean(c, axis=-1, keepdims=True)
    xc = c - mu
    var = jnp.mean(xc * xc, axis=-1, keepdims=True)
    ln = xc * lax.rsqrt(var + LN_EPS) * lnw_ref[...] + lnb_ref[...]
    g = (_silu(ln) * sz).astype(BF16)
    y = jnp.dot(g, wo_ref[...], preferred_element_type=F32) + bo_ref[...]
    xn = x_ref[...] + mod_ref[:, 2 * D:3 * D] * y
    if final:
        xn = _rms(xn, fw_ref[...])
    o_ref[...] = xn


def _conv_layer(x, seq_len, layer, row_fn, mod, nw, w_in, b_in, w_dw, b_dw, ln_w, ln_b, w_out, b_out, final_w):
    M = x.shape[0]
    T = TOKEN_TILE
    tps = seq_len // T
    H = CONV_HALO
    n_halo_blocks = M // H
    row = lambda i: (i, 0)
    in_specs = [pl.BlockSpec((T, D), row)]
    args = [x]
    if tps > 1:
        in_specs += [pl.BlockSpec((H, D), lambda i: (jnp.maximum(i * (T // H) - 1, 0), 0)),
                     pl.BlockSpec((H, D), lambda i: (jnp.minimum((i + 1) * (T // H), n_halo_blocks - 1), 0))]
        args += [x, x]
    in_specs += [_const_spec((1, D)), _mod_spec(layer, lambda i: row_fn(i * T)),
                 _const_spec((D, 3 * D)), _const_spec((1, 3 * D)),
                 _const_spec((CONF_K + 1, D)), _const_spec((1, D)), _const_spec((1, D)), _const_spec((1, D)),
                 _const_spec((D, D)), _const_spec((1, D))]
    args += [nw, mod, w_in, b_in, w_dw, b_dw, ln_w, ln_b, w_out, b_out]
    final = final_w is not None
    if final:
        in_specs.append(_const_spec((1, D)))
        args.append(final_w)
    return pl.pallas_call(
        functools.partial(_conv_kernel, tps, final),
        grid=(M // T,),
        in_specs=in_specs,
        out_specs=pl.BlockSpec((T, D), row),
        out_shape=jax.ShapeDtypeStruct((M, D), F32),
        scratch_shapes=[pltpu.VMEM((D // LANES, T + 2 * H, LANES), F32), pltpu.VMEM((T, D), F32)],
        compiler_params=_cparams(1),
        name="conv_layer",
    )(*args)


def _prep_conv(p, j):
    w_dw = jnp.pad(p["conv_w_dw"][j], ((0, 1), (0, 0)))
    return (p["conv_w_in"][j].astype(BF16), p["conv_b_in"][j][None], w_dw, p["conv_b_dw"][j][None],
            p["conv_ln_w"][j][None], p["conv_ln_b"][j][None], p["conv_w_out"][j].astype(BF16), p["conv_b_out"][j][None])


def _out_proj_kernel(g_ref, x_ref, mod_ref, w_ref, o_ref):
    y = jnp.dot(g_ref[...].astype(BF16), w_ref[...], preferred_element_type=F32)
    o_ref[...] = x_ref[...] + mod_ref[:, 2 * D:3 * D] * y


def _out_proj(g, x, layer, row_fn, mod, w_out):
    M, K = g.shape
    tm = TOKEN_TILE
    row = lambda i: (i, 0)
    return pl.pallas_call(
        _out_proj_kernel,
        grid=(M // tm,),
        in_specs=[pl.BlockSpec((tm, K), row), pl.BlockSpec((tm, D), row), _mod_spec(layer, lambda i: row_fn(i * tm)),
                  _const_spec((K, D))],
        out_specs=pl.BlockSpec((tm, D), row),
        out_shape=jax.ShapeDtypeStruct((M, D), F32),
        compiler_params=_cparams(1),
        name="out_proj",
    )(g, x, mod, w_out)


NA_SCALE = NA_HEAD_DIM ** -0.5
NA_ROWS = 32
NA_WIN = NA_KH * GRID_W
NT_DIMS = (((1,), (1,)), ((), ()))


def _na_in_kernel(x_ref, nw_ref, mod_ref, w_ref, q_ref, k_ref, v_ref, sz_ref):
    hb = _modnorm(x_ref[...], nw_ref[...], mod_ref[...]).astype(BF16)
    q_ref[...] = jnp.dot(hb, w_ref[:, 0:D], preferred_element_type=F32).astype(q_ref.dtype)
    k_ref[...] = jnp.dot(hb, w_ref[:, D:2 * D], preferred_element_type=F32).astype(k_ref.dtype)
    v_ref[...] = jnp.dot(hb, w_ref[:, 2 * D:3 * D], preferred_element_type=F32).astype(v_ref.dtype)
    sz_ref[...] = _silu(jnp.dot(hb, w_ref[:, 3 * D:4 * D], preferred_element_type=F32))


def _na_in(x, layer, row_fn, mod, nw, w_in, kv_dtype):
    M = x.shape[0]
    tm = TOKEN_TILE
    row = lambda i: (i, 0)
    blk = pl.BlockSpec((tm, D), row)
    return pl.pallas_call(
        _na_in_kernel,
        grid=(M // tm,),
        in_specs=[blk, _const_spec((1, D)), _mod_spec(layer, lambda i: row_fn(i * tm)), _const_spec((D, 4 * D))],
        out_specs=[blk, blk, blk, blk],
        out_shape=[jax.ShapeDtypeStruct((M, D), BF16), jax.ShapeDtypeStruct((M, D), kv_dtype),
                   jax.ShapeDtypeStruct((M, D), kv_dtype), jax.ShapeDtypeStruct((M, D), F32)],
        compiler_params=_cparams(1),
        name="na_in",
    )(x, nw, mod, w_in)


def _head_masks():
    lane = lax.broadcasted_iota(jnp.int32, (1, LANES), 1)
    return lane < NA_HEAD_DIM


def _na_ctx_kernel(q_ref, k_ref, v_ref, sz_ref, o_ref):
    lo = _head_masks()
    for p in range(NA_HEADS // 2):
        cols = slice(p * LANES, (p + 1) * LANES)
        qp = q_ref[:, cols]
        kp = k_ref[:, cols].astype(BF16)
        vp = v_ref[:, cols].astype(BF16)
        n = qp.shape[0]
        zero = jnp.zeros_like(qp)
        qs = jnp.concatenate([jnp.where(lo, qp, zero), jnp.where(lo, zero, qp)], axis=0)
        s = lax.dot_general(qs, kp, NT_DIMS, preferred_element_type=F32) * NA_SCALE
        e = jnp.exp(s - jnp.max(s, axis=-1, keepdims=True))
        l = jnp.sum(e, axis=-1, keepdims=True)
        o = jnp.dot(e.astype(BF16), vp, preferred_element_type=F32) / l
        o_ref[:, cols] = jnp.where(lo, o[:n], o[n:]) * sz_ref[:, cols]


def _na_ctx(q, k, v, sz, seq_len):
    M = q.shape[0]
    blk = pl.BlockSpec((seq_len, D), lambda i: (i, 0))
    return pl.pallas_call(
        _na_ctx_kernel,
        grid=(M // seq_len,),
        in_specs=[blk, blk, blk, blk],
        out_specs=blk,
        out_shape=jax.ShapeDtypeStruct((M, D), F32),
        compiler_params=_cparams(1),
        name="na_ctx_attn",
    )(q, k, v, sz)


def _rpb_kernel(rpb_ref, o_ref):
    h = pl.program_id(0)
    c = lax.broadcasted_iota(jnp.int32, (GRID_W, LANES), 0)
    lane = lax.broadcasted_iota(jnp.int32, (GRID_W, LANES), 1)
    kc = lane & (GRID_W - 1)
    upper = lax.broadcasted_iota(jnp.int32, (1, LANES), 1) >= GRID_W
    cs = jnp.clip(c - NA_KW // 2, 0, GRID_W - NA_KW)
    valid = (kc >= cs) & (kc < cs + NA_KW)
    d = kc - c
    n_dc = 2 * NA_KW - 1
    n_pair = 2 * NA_KH - 2
    for dri in range(n_pair):
        tile = jnp.zeros((GRID_W, LANES), F32)
        for dd in range(-(NA_KW - 1), NA_KW):
            base = h * ((2 * NA_KH - 1) * n_dc) + dri * n_dc + (dd + NA_KW - 1)
            tile = jnp.where(d == dd, jnp.where(upper, rpb_ref[base + n_dc], rpb_ref[base]), tile)
        tile = jnp.where(valid, tile, NEG_INF)
        for idx in range(NA_KH):
            jj2 = dri - (NA_KH - 1) + idx
            if jj2 % 2 == 0 and 0 <= jj2 // 2 < NA_KH // 2:
                jj = jj2 // 2
                o_ref[idx, :, jj * LANES:(jj + 1) * LANES] = tile


def _rpb_table(rpb):
    return pl.pallas_call(
        _rpb_kernel,
        grid=(NA_HEADS,),
        in_specs=[pl.BlockSpec(memory_space=pltpu.SMEM)],
        out_specs=pl.BlockSpec((NA_KH, None, GRID_W, NA_WIN), lambda h: (0, h, 0, 0)),
        out_shape=jax.ShapeDtypeStruct((NA_KH, NA_HEADS, GRID_W, NA_WIN), F32),
        compiler_params=_cparams(1),
        name="na_rpb_table",
    )(rpb.reshape(-1))


def _na_row_start(r):
    return jnp.clip(r - NA_KH // 2, 0, NA_ROWS - NA_KH)


def _na_lat_kernel(q_ref, k_ref, v_ref, ck_ref, cv_ref, bias_ref, sz_ref, o_ref):
    lo = _head_masks()
    k0 = pl.multiple_of(_na_row_start(pl.program_id(1)) * GRID_W, GRID_W)
    for p in range(NA_HEADS // 2):
        cols = slice(p * LANES, (p + 1) * LANES)
        qp = q_ref[:, cols]
        kw = k_ref[pl.ds(k0, NA_WIN), cols]
        vw = v_ref[pl.ds(k0, NA_WIN), cols]
        ck = ck_ref[:, cols]
        cv = cv_ref[:, cols]
        zero = jnp.zeros_like(qp)
        qs = jnp.concatenate([jnp.where(lo, qp, zero), jnp.where(lo, zero, qp)], axis=0)
        bias = bias_ref[2 * p:2 * p + 2].reshape(2 * GRID_W, NA_WIN)
        s_loc = lax.dot_general(qs, kw, NT_DIMS, preferred_element_type=F32) * NA_SCALE + bias
        s_ctx = lax.dot_general(qs, ck, NT_DIMS, preferred_element_type=F32) * NA_SCALE
        m = jnp.maximum(jnp.max(s_loc, axis=-1, keepdims=True), jnp.max(s_ctx, axis=-1, keepdims=True))
        e_loc = jnp.exp(s_loc - m)
        e_ctx = jnp.exp(s_ctx - m)
        l = jnp.sum(e_loc, axis=-1, keepdims=True) + jnp.sum(e_ctx, axis=-1, keepdims=True)
        o = (jnp.dot(e_loc.astype(BF16), vw, preferred_element_type=F32)
             + jnp.dot(e_ctx.astype(BF16), cv, preferred_element_type=F32)) / l
        o_ref[:, cols] = jnp.where(lo, o[:GRID_W], o[GRID_W:]) * sz_ref[:, cols]


def _na_lat(q, k, v, ck, cv, bias, sz, n_lat):
    T = NA_ROWS * GRID_W
    past = ck.shape[1]
    qblk = pl.BlockSpec((GRID_W, D), lambda b, r: (b * NA_ROWS + r, 0))
    kvblk = pl.BlockSpec((T, D), lambda b, r: (b, 0))
    cblk = pl.BlockSpec((None, past, D), lambda b, r: (b, 0, 0))
    bblk = pl.BlockSpec((None, NA_HEADS, GRID_W, NA_WIN), lambda b, r: (r - _na_row_start(r), 0, 0, 0))
    return pl.pallas_call(
        _na_lat_kernel,
        grid=(n_lat, NA_ROWS),
        in_specs=[qblk, kvblk, kvblk, cblk, cblk, bblk, qblk],
        out_specs=qblk,
        out_shape=jax.ShapeDtypeStruct((n_lat * T, D), F32),
        compiler_params=_cparams(2),
        name="na_lat_attn",
    )(q, k, v, ck, cv, bias, sz)


def _na_layer(xc, xl, layer, mod, nw, p, n_lat):
    j = layer // 3
    w_in = p["na_w_in"][j].astype(BF16)
    w_out = p["na_w_out"][j].astype(BF16)
    row_ctx = lambda tok: 0
    row_lat = lambda tok: 1 + tok // (NA_ROWS * GRID_W)
    qc, kc, vc, szc = _na_in(xc, layer, row_ctx, mod, nw, w_in, F32)
    ql, kl, vl, szl = _na_in(xl, layer, row_lat, mod, nw, w_in, BF16)
    gc = _na_ctx(qc, kc, vc, szc, SEQ)
    past = p["cache_k"].shape[2]
    ck = p["cache_k"][:n_lat, j].reshape(n_lat, past, D).astype(BF16)
    cv = p["cache_v"][:n_lat, j].reshape(n_lat, past, D).astype(BF16)
    gl = _na_lat(ql, kl, vl, ck, cv, _rpb_table(p["na_rpb"][j]), szl, n_lat)
    return (_out_proj(gc, xc, layer, row_ctx, mod, w_out), _out_proj(gl, xl, layer, row_lat, mod, w_out), kc, vc)


SSD_IN_COLS = SSD_INNER + SSD_XBC + LANES


def _softplus(x):
    return jnp.maximum(x, 0.0) + jnp.log(1.0 + jnp.exp(-jnp.abs(x)))


def _ssd_in_kernel(tiles_per_seq, *refs):
    refs = list(refs)
    x_ref = refs.pop(0)
    if tiles_per_seq > 1:
        xp_ref = refs.pop(0)
        xn_ref = refs.pop(0)
    nw_ref, mod_ref, w_ref, dtb_ref, wc_ref, bc_ref, sz_ref, xbc_ref, dt_ref, xpad_ref = refs
    hb = _modnorm(x_ref[...], nw_ref[...], mod_ref[...]).astype(BF16)
    halo = None
    if tiles_per_seq > 1:
        t = pl.program_id(0) % tiles_per_seq
        xh = jnp.concatenate([xp_ref[...], xn_ref[...]], axis=0)
        halo = (_modnorm(xh, nw_ref[...], mod_ref[...]).astype(BF16), t > 0, t < tiles_per_seq - 1)
    dt_ref[...] = _ssd_project(hb, halo, w_ref, dtb_ref, wc_ref, bc_ref, xpad_ref, sz_ref, xbc_ref)


def _ssd_project(hb, halo, w_ref, dtb_ref, wc_ref, bc_ref, xpad_ref, sz_ref, xbc_ref):
    tm = hb.shape[0]
    HL = SSD_HALO
    RB = 128
    for j in range(SSD_XBC // D):
        c0 = SSD_INNER + j * D
        raw = jnp.dot(hb, w_ref[:, c0:c0 + D], preferred_element_type=F32)
        if halo is not None:
            rawh = jnp.dot(halo[0], w_ref[:, c0:c0 + D], preferred_element_type=F32)
            raw_prev = jnp.where(halo[1], rawh[0:HL], 0.0)
            raw_next = jnp.where(halo[2], rawh[HL:2 * HL], 0.0)
        else:
            raw_prev = raw_next = jnp.zeros((HL, D), F32)
        for cb in range(D // LANES):
            blk = j * (D // LANES) + cb
            lc = slice(blk * LANES, (blk + 1) * LANES)
            cc = slice(cb * LANES, (cb + 1) * LANES)
            xpad_ref[blk, 0:HL, :] = raw_prev[:, cc]
            xpad_ref[blk, HL:HL + tm, :] = raw[:, cc]
            xpad_ref[blk, HL + tm:HL + tm + HL, :] = raw_next[:, cc]
            for rb in range(tm // RB):
                acc = jnp.zeros((RB, LANES), F32)
                for k in range(SSD_CONV):
                    r0 = rb * RB + HL - SSD_CONV // 2 + k
                    acc = acc + xpad_ref[blk, r0:r0 + RB, :] * wc_ref[k:k + 1, lc]
                xbc_ref[rb * RB:(rb + 1) * RB, lc] = _silu(acc + bc_ref[:, lc])
    for j in range(SSD_INNER // D):
        sz_ref[:, j * D:(j + 1) * D] = _silu(jnp.dot(hb, w_ref[:, j * D:(j + 1) * D], preferred_element_type=F32))
    c0 = SSD_INNER + SSD_XBC
    raw = jnp.dot(hb, w_ref[:, c0:c0 + LANES], preferred_element_type=F32) + dtb_ref[...]
    lane = lax.broadcasted_iota(jnp.int32, (1, LANES), 1)
    return jnp.where(lane < 2 * SSD_HEADS, _softplus(raw), 0.0)


def _ssd_in(x, seq_len, layer, row_fn, mod, nw, w_in, dt_bias, w_conv, b_conv):
    M = x.shape[0]
    tm = TOKEN_TILE
    HL = SSD_HALO
    tps = seq_len // tm
    n_hblk = M // HL
    row = lambda i: (i, 0)
    in_specs = [pl.BlockSpec((tm, D), row)]
    args = [x]
    if tps > 1:
        in_specs += [pl.BlockSpec((HL, D), lambda i: (jnp.maximum(i * (tm // HL) - 1, 0), 0)),
                     pl.BlockSpec((HL, D), lambda i: (jnp.minimum((i + 1) * (tm // HL), n_hblk - 1), 0))]
        args += [x, x]
    in_specs += [_const_spec((1, D)), _mod_spec(layer, lambda i: row_fn(i * tm)),
                 _const_spec((D, SSD_IN_COLS)), _const_spec((1, LANES)),
                 _const_spec((SSD_CONV + 1, SSD_XBC)), _const_spec((1, SSD_XBC))]
    args += [nw, mod, w_in, dt_bias, w_conv, b_conv]
    return pl.pallas_call(
        functools.partial(_ssd_in_kernel, tps),
        grid=(M // tm,),
        in_specs=in_specs,
        out_specs=[pl.BlockSpec((tm, SSD_INNER), row), pl.BlockSpec((tm, SSD_XBC), row), pl.BlockSpec((tm, LANES), row)],
        out_shape=[jax.ShapeDtypeStruct((M, SSD_INNER), F32), jax.ShapeDtypeStruct((M, SSD_XBC), F32),
                   jax.ShapeDtypeStruct((M, LANES), F32)],
        scratch_shapes=[pltpu.VMEM((SSD_XBC // LANES, tm + 2 * HL, LANES), F32)],
        compiler_params=_cparams(1),
        name="ssd_in",
    )(*args)


def _cumsum_rows(a, reverse):
    n = a.shape[0]
    row = lax.broadcasted_iota(jnp.int32, a.shape, 0)
    k = 1
    while k < n:
        if reverse:
            a = a + jnp.where(row < n - k, pltpu.roll(a, n - k, 0), 0.0)
        else:
            a = a + jnp.where(row >= k, pltpu.roll(a, k, 0), 0.0)
        k *= 2
    return a


def _ssd_core_kernel(nc, has_h0, emit_state, *refs):
    refs = list(refs)
    dir_refs = [refs[0:2], refs[2:4]]
    alog_ref, dsk_ref = refs[4:6]
    refs = refs[6:]
    if has_h0:
        h0_ref = refs.pop(0)
    y_refs = [refs.pop(0), refs.pop(0)]
    if emit_state:
        so_ref = refs.pop(0)
    (st_ref,) = refs
    Q, N, P = SSD_CHUNK, SSD_STATE, SSD_HEAD_DIM
    c = pl.program_id(1)
    n_blk = SSD_INNER // LANES

    @pl.when(c == 0)
    def _init():
        if has_h0:
            for dr in range(2):
                for j in range(n_blk):
                    st_ref[dr, :, j * LANES:(j + 1) * LANES] = h0_ref[dr, j * LANES:(j + 1) * LANES, :].T
        else:
            st_ref[...] = jnp.zeros(st_ref.shape, F32)

    A = -jnp.exp(alog_ref[...])
    for dr in range(2):
        x_ref, dt_ref = dir_refs[dr]
        y_ref = y_refs[dr]

        def x_at(c0, x_ref=x_ref):
            return x_ref[:, c0:c0 + LANES]

        def y_put(c0, y, y_ref=y_ref):
            y_ref[:, c0:c0 + LANES] = y

        _ssd_chunk(dr, x_at, dt_ref[...], A, st_ref, y_put, dsk_ref)

    if emit_state:
        @pl.when(c == nc - 1)
        def _emit():
            _ssd_emit_state(st_ref, so_ref)


def _ssd_emit_state(st_ref, so_ref):
    for dr in range(2):
        for j in range(SSD_INNER // LANES):
            so_ref[dr, j * LANES:(j + 1) * LANES, :] = st_ref[dr, :, j * LANES:(j + 1) * LANES].T


def _ssd_chunk(dr, x_at, dt, A, st_ref, y_put, dsk_ref):
    Q, N, P = SSD_CHUNK, SSD_STATE, SSD_HEAD_DIM
    lo = _head_masks()
    ri = lax.broadcasted_iota(jnp.int32, (Q, Q), 0)
    ci = lax.broadcasted_iota(jnp.int32, (Q, Q), 1)
    cum = _cumsum_rows(dt * A, reverse=(dr == 1))
    cum_t = cum.T
    tot_row = Q - 1 if dr == 0 else 0
    mask = (ri >= ci) if dr == 0 else (ri <= ci)
    for g in range(SSD_GROUPS):
        bg_t = x_at(SSD_INNER + g * N).T.astype(BF16)
        cg = x_at(SSD_INNER + SSD_GN + g * N).astype(BF16)
        cb = jnp.dot(cg, bg_t, preferred_element_type=F32)
        for pp in range(SSD_HEADS // SSD_GROUPS // 2):
            head = g * (SSD_HEADS // SSD_GROUPS) + 2 * pp
            c0 = head * P
            li = dr * SSD_HEADS + head
            xpair = x_at(c0)
            dt_cols = [dt[:, li:li + 1], dt[:, li + 1:li + 2]]
            cum_cols = [cum[:, li:li + 1], cum[:, li + 1:li + 2]]
            xdt = xpair * jnp.where(lo, dt_cols[0], dt_cols[1])
            xdt_b = xdt.astype(BF16)
            cum_x = jnp.where(lo, cum_cols[0], cum_cols[1])
            yd = []
            for hh in range(2):
                seg = cum_cols[hh] - cum_t[li + hh:li + hh + 1, :]
                decay = jnp.exp(jnp.where(mask, seg, -jnp.inf))
                yd.append(jnp.dot((cb * decay).astype(BF16), xdt_b, preferred_element_type=F32))
            st = st_ref[dr, :, c0:c0 + LANES]
            y = (jnp.where(lo, yd[0], yd[1])
                 + jnp.dot(cg, st.astype(BF16), preferred_element_type=F32) * jnp.exp(cum_x))
            if dr == 0:
                y = y + xpair * dsk_ref[:, c0:c0 + LANES]
            y_put(c0, y)
            tot = cum_x[tot_row:tot_row + 1, :]
            xw = (xdt * jnp.exp(tot - cum_x)).astype(BF16)
            st_ref[dr, :, c0:c0 + LANES] = st * jnp.exp(tot) + jnp.dot(bg_t, xw, preferred_element_type=F32)


def _ssd_seq_kernel(x_ref, nw_ref, mod_ref, w_ref, dtb_ref, wc_ref, bc_ref, alog_ref, dsk_ref, gnw_ref, wo_ref,
                    o_ref, so_ref, xpad_ref, xbc_ref, sz_ref, y_ref, st_ref):
    Q = SSD_CHUNK
    nc = x_ref.shape[0] // Q
    x = x_ref[...]
    hb = _modnorm(x, nw_ref[...], mod_ref[...]).astype(BF16)
    dt = _ssd_project(hb, None, w_ref, dtb_ref, wc_ref, bc_ref, xpad_ref, sz_ref, xbc_ref)
    st_ref[...] = jnp.zeros(st_ref.shape, F32)
    A = -jnp.exp(alog_ref[...])
    for dr in range(2):
        for ck in (range(nc) if dr == 0 else reversed(range(nc))):
            rows = slice(ck * Q, (ck + 1) * Q)

            def x_at(c0, rows=rows):
                return xbc_ref[rows, c0:c0 + LANES]

            def y_put(c0, y, rows=rows, dr=dr):
                if dr == 0:
                    y_ref[rows, c0:c0 + LANES] = y
                else:
                    y_ref[rows, c0:c0 + LANES] += y

            _ssd_chunk(dr, x_at, dt[rows], A, st_ref, y_put, dsk_ref)
    _ssd_emit_state(st_ref, so_ref)
    g = _rms(y_ref[...] * sz_ref[...], gnw_ref[...]).astype(BF16)
    o_ref[...] = x + mod_ref[:, 2 * D:3 * D] * jnp.dot(g, wo_ref[...], preferred_element_type=F32)


def _ssd_seq(x, seq_len, layer, row_fn, mod, nw, w_in, dt_bias, w_conv, b_conv, a_log, d_skip, norm_w, w_out):
    M = x.shape[0]
    n_seq = M // seq_len
    HL = SSD_HALO
    row = lambda i: (i, 0)
    st_blk = pl.BlockSpec((None, 2, SSD_INNER, SSD_STATE), lambda i: (i, 0, 0, 0))
    return pl.pallas_call(
        _ssd_seq_kernel,
        grid=(n_seq,),
        in_specs=[pl.BlockSpec((seq_len, D), row), _const_spec((1, D)), _mod_spec(layer, lambda i: row_fn(i * seq_len)),
                  _const_spec((D, SSD_IN_COLS)), _const_spec((1, LANES)),
                  _const_spec((SSD_CONV + 1, SSD_XBC)), _const_spec((1, SSD_XBC)),
                  _const_spec((1, LANES)), _const_spec((1, SSD_INNER)), _const_spec((1, SSD_INNER)),
                  _const_spec((SSD_INNER, D))],
        out_specs=[pl.BlockSpec((seq_len, D), row), st_blk],
        out_shape=[jax.ShapeDtypeStruct((M, D), F32), jax.ShapeDtypeStruct((n_seq, 2, SSD_INNER, SSD_STATE), F32)],
        scratch_shapes=[pltpu.VMEM((SSD_XBC // LANES, seq_len + 2 * HL, LANES), F32),
                        pltpu.VMEM((seq_len, SSD_XBC), F32), pltpu.VMEM((seq_len, SSD_INNER), F32),
                        pltpu.VMEM((seq_len, SSD_INNER), F32), pltpu.VMEM((2, SSD_STATE, SSD_INNER), F32)],
        compiler_params=_cparams(1),
        name="ssd_seq",
    )(x, nw, mod, w_in, dt_bias, w_conv, b_conv, a_log, d_skip, norm_w, w_out)


def _ssd_core(xbc, dtp, a_log, d_skip, h0, n_seq, seq_len, emit_state):
    Q = SSD_CHUNK
    nc = seq_len // Q
    M = n_seq * seq_len

    def chunk(dr):
        return (lambda b, c: b * nc + c) if dr == 0 else (lambda b, c: b * nc + nc - 1 - c)

    in_specs, args = [], []
    for dr in range(2):
        ch = chunk(dr)
        in_specs += [pl.BlockSpec((Q, SSD_XBC), lambda b, c, ch=ch: (ch(b, c), 0)),
                     pl.BlockSpec((Q, LANES), lambda b, c, ch=ch: (ch(b, c), 0))]
        args += [xbc, dtp]
    in_specs += [_const_spec((1, LANES)), _const_spec((1, SSD_INNER))]
    args += [a_log, d_skip]
    st_blk = pl.BlockSpec((None, 2, SSD_INNER, SSD_STATE), lambda b, c: (b, 0, 0, 0))
    has_h0 = h0 is not None
    if has_h0:
        in_specs.append(st_blk)
        args.append(h0)
    out_specs = [pl.BlockSpec((Q, SSD_INNER), lambda b, c: (chunk(0)(b, c), 0)),
                 pl.BlockSpec((Q, SSD_INNER), lambda b, c: (chunk(1)(b, c), 0))]
    out_shape = [jax.ShapeDtypeStruct((M, SSD_INNER), F32)] * 2
    if emit_state:
        out_specs.append(st_blk)
        out_shape.append(jax.ShapeDtypeStruct((n_seq, 2, SSD_INNER, SSD_STATE), F32))
    return pl.pallas_call(
        functools.partial(_ssd_core_kernel, nc, has_h0, emit_state),
        grid=(n_seq, nc),
        in_specs=in_specs,
        out_specs=out_specs,
        out_shape=out_shape,
        scratch_shapes=[pltpu.VMEM((2, SSD_STATE, SSD_INNER), F32)],
        compiler_params=_cparams(2),
        name="ssd_core",
    )(*args)


def _ssd_out_kernel(yf_ref, yb_ref, sz_ref, x_ref, mod_ref, nw_ref, w_ref, o_ref):
    g = _rms((yf_ref[...] + yb_ref[...]) * sz_ref[...], nw_ref[...]).astype(BF16)
    o_ref[...] = x_ref[...] + mod_ref[:, 2 * D:3 * D] * jnp.dot(g, w_ref[...], preferred_element_type=F32)


def _ssd_out(yf, yb, sz, x, layer, row_fn, mod, norm_w, w_out):
    M = x.shape[0]
    tm = TOKEN_TILE
    row = lambda i: (i, 0)
    wide = pl.BlockSpec((tm, SSD_INNER), row)
    return pl.pallas_call(
        _ssd_out_kernel,
        grid=(M // tm,),
        in_specs=[wide, wide, wide, pl.BlockSpec((tm, D), row), _mod_spec(layer, lambda i: row_fn(i * tm)),
                  _const_spec((1, SSD_INNER)), _const_spec((SSD_INNER, D))],
        out_specs=pl.BlockSpec((tm, D), row),
        out_shape=jax.ShapeDtypeStruct((M, D), F32),
        compiler_params=_cparams(1),
        name="ssd_out",
    )(yf, yb, sz, x, mod, norm_w, w_out)


def _ssd_layer(xc, xl, layer, mod, nw, p, n_lat):
    j = layer // 3
    w_in = jnp.pad(p["ssd_w_in"][j].astype(BF16), ((0, 0), (0, LANES - 2 * SSD_HEADS)))
    dt_bias = jnp.pad(p["ssd_dt_bias"][j].reshape(1, -1), ((0, 0), (0, LANES - 2 * SSD_HEADS)))
    a_log = jnp.pad(p["ssd_a_log"][j].reshape(1, -1), ((0, 0), (0, LANES - 2 * SSD_HEADS)))
    w_conv = jnp.pad(p["ssd_w_conv"][j], ((0, 1), (0, 0)))
    b_conv = p["ssd_b_conv"][j][None]
    d_skip = jnp.repeat(p["ssd_d"][j], SSD_HEAD_DIM)[None]
    norm_w = p["ssd_norm_w"][j][None]
    w_out = p["ssd_w_out"][j].astype(BF16)
    row_ctx = lambda tok: 0
    row_lat = lambda tok: 1 + tok // DEC_SEQ
    oc, s_new = _ssd_seq(xc, SEQ, layer, row_ctx, mod, nw, w_in, dt_bias, w_conv, b_conv, a_log, d_skip, norm_w, w_out)
    szl, xbc_l, dt_l = _ssd_in(xl, DEC_SEQ, layer, row_lat, mod, nw, w_in, dt_bias, w_conv, b_conv)
    h0 = p["state_ssm"][:n_lat, j].reshape(n_lat, 2, SSD_INNER, SSD_STATE)
    yfl, ybl = _ssd_core(xbc_l, dt_l, a_log, d_skip, h0, n_lat, DEC_SEQ, False)
    ol = _ssd_out(yfl, ybl, szl, xl, layer, row_lat, mod, norm_w, w_out)
    return oc, ol, s_new


def kernel(x_prompt, x_sample, cache_k, cache_v, state_ssm, c, c_ctx, ada_w, ada_b, norm_w, final_norm_w, conv_w_in, conv_b_in, conv_w_dw, conv_b_dw, conv_ln_w, conv_ln_b, conv_w_out, conv_b_out, na_w_in, na_rpb, na_w_out, ssd_w_in, ssd_w_conv, ssd_b_conv, ssd_a_log, ssd_dt_bias, ssd_d, ssd_norm_w, ssd_w_out):
    p = dict(cache_k=cache_k, cache_v=cache_v, state_ssm=state_ssm,
             conv_w_in=conv_w_in, conv_b_in=conv_b_in, conv_w_dw=conv_w_dw, conv_b_dw=conv_b_dw,
             conv_ln_w=conv_ln_w, conv_ln_b=conv_ln_b, conv_w_out=conv_w_out, conv_b_out=conv_b_out,
             na_w_in=na_w_in, na_rpb=na_rpb, na_w_out=na_w_out,
             ssd_w_in=ssd_w_in, ssd_w_conv=ssd_w_conv, ssd_b_conv=ssd_b_conv, ssd_a_log=ssd_a_log,
             ssd_dt_bias=ssd_dt_bias, ssd_d=ssd_d, ssd_norm_w=ssd_norm_w, ssd_w_out=ssd_w_out)
    n_ctx, n_lat = x_prompt.shape[0], x_sample.shape[0]
    assert x_prompt.shape[1:] == (SEQ, D) and x_sample.shape[1:] == (DEC_SEQ, D)
    assert (DEPTH - 1) % 3 == 0, "the final RMSNorm is fused into a trailing Conformer layer"
    xc = x_prompt.reshape(n_ctx * SEQ, D)
    xl = x_sample.reshape(n_lat * DEC_SEQ, D)
    cond8 = jnp.concatenate([c_ctx[None], c, jnp.zeros((8 - 1 - n_lat, D), F32)], axis=0)
    mod = _modulation(cond8, ada_w, ada_b).reshape(DEPTH, 8, 1, 3 * D)
    row_ctx = lambda tok: 0
    row_lat = lambda tok: 1 + tok // DEC_SEQ
    new_k, new_v, new_s = [], [], []
    for i in range(DEPTH):
        kind, j = i % 3, i // 3
        nw = norm_w[i][None]
        if kind == 0:
            cp = _prep_conv(p, j)
            fw = final_norm_w[None] if i == DEPTH - 1 else None
            xc = _conv_layer(xc, SEQ, i, row_ctx, mod, nw, *cp, fw)
            xl = _conv_layer(xl, DEC_SEQ, i, row_lat, mod, nw, *cp, fw)
        elif kind == 1:
            xc, xl, k_c, v_c = _na_layer(xc, xl, i, mod, nw, p, n_lat)
            new_k.append(k_c.reshape(n_ctx, SEQ, NA_HEADS, NA_HEAD_DIM))
            new_v.append(v_c.reshape(n_ctx, SEQ, NA_HEADS, NA_HEAD_DIM))
        else:
            xc, xl, s_c = _ssd_layer(xc, xl, i, mod, nw, p, n_lat)
            new_s.append(s_c.reshape(n_ctx, 2, SSD_HEADS, SSD_HEAD_DIM, SSD_STATE))
    return (xc.reshape(n_ctx, SEQ, D), xl.reshape(n_lat, DEC_SEQ, D),
            jnp.stack(new_k, axis=1), jnp.stack(new_v, axis=1), jnp.stack(new_s, axis=1))
```

```python
import functools

import jax
import jax.numpy as jnp
from jax import lax
from jax.experimental import pallas as pl
from jax.experimental.pallas import tpu as pltpu

F32 = jnp.float32
BF16 = jnp.bfloat16

D = 1024
DEPTH = 4
SEQ = 256
DEC_SEQ = 2048
GRID_W = 64
RMS_EPS = 1e-6
LN_EPS = 1e-5
CONF_K = 31
NA_HEADS = 16
NA_HEAD_DIM = 64
NA_KH = 8
NA_KW = 16
NEG_INF = -1e30
SSD_INNER = 2048
SSD_HEADS = 32
SSD_HEAD_DIM = 64
SSD_STATE = 128
SSD_GROUPS = 4
SSD_GN = SSD_GROUPS * SSD_STATE
SSD_CONV = 7
SSD_CHUNK = 128
SSD_XBC = SSD_INNER + 2 * SSD_GN

LANES = 128
SUBLANES = 8
TOKEN_TILE = 256
CONV_HALO = 16
SSD_HALO = 8
VMEM_LIMIT = 56 * 1024 * 1024


def _cparams(n_axes):
    return pltpu.CompilerParams(dimension_semantics=("arbitrary",) * n_axes,
                                vmem_limit_bytes=VMEM_LIMIT)


def _const_spec(shape):
    nd = len(shape)
    return pl.BlockSpec(shape, lambda *_: (0,) * nd, pipeline_mode=pl.Buffered(1))


def _sigmoid(x):
    return 1.0 / (1.0 + jnp.exp(-x))


def _silu(x):
    return x * _sigmoid(x)


def _rms(x, w):
    ms = jnp.mean(x * x, axis=-1, keepdims=True)
    return x * lax.rsqrt(ms + RMS_EPS) * w


def _modnorm(x, nw, mod):
    return _rms(x, nw) * (1.0 + mod[:, D:2 * D]) + mod[:, :D]


def _mod_spec(layer, row_fn):
    return pl.BlockSpec((None, None, 1, 3 * D), lambda i, *_: (layer, row_fn(i), 0, 0))


def _mod_kernel(cond_ref, w_ref, b_ref, o_ref):
    s = _silu(cond_ref[...]).astype(BF16)
    o_ref[...] = jnp.dot(s, w_ref[...].astype(BF16), preferred_element_type=F32) + b_ref[...]


def _modulation(cond8, ada_w, ada_b):
    tn = 1024
    return pl.pallas_call(
        _mod_kernel,
        grid=(DEPTH, 3 * D // tn),
        in_specs=[pl.BlockSpec((8, D), lambda i, j: (0, 0)),
                  pl.BlockSpec((None, D, tn), lambda i, j: (i, 0, j)),
                  pl.BlockSpec((None, 1, tn), lambda i, j: (i, 0, j))],
        out_specs=pl.BlockSpec((None, 8, tn), lambda i, j: (i, 0, j)),
        out_shape=jax.ShapeDtypeStruct((DEPTH, 8, 3 * D), F32),
        compiler_params=_cparams(2),
        name="adaln_mod",
    )(cond8, ada_w, ada_b.reshape(DEPTH, 1, 3 * D))


CONV_CHUNK = 256


def _conv_kernel(tiles_per_seq, final, *refs):
    refs = list(refs)
    x_ref = refs.pop(0)
    if tiles_per_seq > 1:
        xp_ref = refs.pop(0)
        xn_ref = refs.pop(0)
    nw_ref, mod_ref, wi_ref, bi_ref, wdw_ref, bdw_ref, lnw_ref, lnb_ref, wo_ref, bo_ref = refs[:10]
    refs = refs[10:]
    if final:
        fw_ref = refs.pop(0)
    o_ref, ubuf, cbuf = refs
    T = x_ref.shape[0]
    H = CONV_HALO
    RB = 128

    hb = _modnorm(x_ref[...], nw_ref[...], mod_ref[...]).astype(BF16)
    if tiles_per_seq > 1:
        t = pl.program_id(0) % tiles_per_seq
        hh = _modnorm(jnp.concatenate([xp_ref[...], xn_ref[...]], axis=0), nw_ref[...], mod_ref[...]).astype(BF16)

    def glu(h, cols):
        v = jnp.dot(h, wi_ref[:, cols], preferred_element_type=F32) + bi_ref[:, cols]
        gcols = slice(D + cols.start, D + cols.stop)
        g = jnp.dot(h, wi_ref[:, gcols], preferred_element_type=F32) + bi_ref[:, gcols]
        return v * _sigmoid(g)

    for ch in range(D // CONV_CHUNK):
        cols = slice(ch * CONV_CHUNK, (ch + 1) * CONV_CHUNK)
        u_main = glu(hb, cols)
        if tiles_per_seq > 1:
            uh = glu(hh, cols)
            u_prev = jnp.where(t > 0, uh[0:H], 0.0)
            u_next = jnp.where(t < tiles_per_seq - 1, uh[H:2 * H], 0.0)
        else:
            u_prev = u_next = jnp.zeros((H, CONV_CHUNK), F32)
        for cb in range(CONV_CHUNK // LANES):
            blk = ch * (CONV_CHUNK // LANES) + cb
            lc = slice(blk * LANES, (blk + 1) * LANES)
            cc = slice(cb * LANES, (cb + 1) * LANES)
            ubuf[blk, 0:H, :] = u_prev[:, cc]
            ubuf[blk, H:H + T, :] = u_main[:, cc]
            ubuf[blk, H + T:H + T + H, :] = u_next[:, cc]
            for rb in range(T // RB):
                acc = jnp.zeros((RB, LANES), F32)
                for k in range(CONF_K):
                    r0 = rb * RB + H - CONF_K // 2 + k
                    acc = acc + ubuf[blk, r0:r0 + RB, :] * wdw_ref[k:k + 1, lc]
                cbuf[rb * RB:(rb + 1) * RB, lc] = acc + bdw_ref[:, lc]

    sz = _silu(jnp.dot(hb, wi_ref[:, 2 * D:3 * D], preferred_element_type=F32) + bi_ref[:, 2 * D:3 * D])
    c = cbuf[...]
    mu = jnp.mean(c, axis=-1, keepdims=True)
    xc = c - mu
    var = jnp.mean(xc * xc, axis=-1, keepdims=True)
    ln = xc * lax.rsqrt(var + LN_EPS) * lnw_ref[...] + lnb_ref[...]
    g = (_silu(ln) * sz).astype(BF16)
    y = jnp.dot(g, wo_ref[...], preferred_element_type=F32) + bo_ref[...]
    xn = x_ref[...] + mod_ref[:, 2 * D:3 * D] * y
    if final:
        xn = _rms(xn, fw_ref[...])
    o_ref[...] = xn


def _conv_layer(x, seq_len, layer, row_fn, mod, nw, w_in, b_in, w_dw, b_dw, ln_w, ln_b, w_out, b_out, final_w):
    M = x.shape[0]
    T = TOKEN_TILE
    tps = seq_len // T
    H = CONV_HALO
    n_halo_blocks = M // H
    row = lambda i: (i, 0)
    in_specs = [pl.BlockSpec((T, D), row)]
    args = [x]
    if tps > 1:
        in_specs += [pl.BlockSpec((H, D), lambda i: (jnp.maximum(i * (T // H) - 1, 0), 0)),
                     pl.BlockSpec((H, D), lambda i: (jnp.minimum((i + 1) * (T // H), n_halo_blocks - 1), 0))]
        args += [x, x]
    in_specs += [_const_spec((1, D)), _mod_spec(layer, lambda i: row_fn(i * T)),
                 _const_spec((D, 3 * D)), _const_spec((1, 3 * D)),
                 _const_spec((CONF_K + 1, D)), _const_spec((1, D)), _const_spec((1, D)), _const_spec((1, D)),
                 _const_spec((D, D)), _const_spec((1, D))]
    args += [nw, mod, w_in, b_in, w_dw, b_dw, ln_w, ln_b, w_out, b_out]
    final = final_w is not None
    if final:
        in_specs.append(_const_spec((1, D)))
        args.append(final_w)
    return pl.pallas_call(
        functools.partial(_conv_kernel, tps, final),
        grid=(M // T,),
        in_specs=in_specs,
        out_specs=pl.BlockSpec((T, D), row),
        out_shape=jax.ShapeDtypeStruct((M, D), F32),
        scratch_shapes=[pltpu.VMEM((D // LANES, T + 2 * H, LANES), F32), pltpu.VMEM((T, D), F32)],
        compiler_params=_cparams(1),
        name="conv_layer",
    )(*args)


def _prep_conv(p, j):
    w_dw = jnp.pad(p["conv_w_dw"][j], ((0, 1), (0, 0)))
    return (p["conv_w_in"][j].astype(BF16), p["conv_b_in"][j][None], w_dw, p["conv_b_dw"][j][None],
            p["conv_ln_w"][j][None], p["conv_ln_b"][j][None], p["conv_w_out"][j].astype(BF16), p["conv_b_out"][j][None])


def _out_proj_kernel(g_ref, x_ref, mod_ref, w_ref, o_ref):
    y = jnp.dot(g_ref[...].astype(BF16), w_ref[...], preferred_element_type=F32)
    o_ref[...] = x_ref[...] + mod_ref[:, 2 * D:3 * D] * y


def _out_proj(g, x, layer, row_fn, mod, w_out):
    M, K = g.shape
    tm = TOKEN_TILE
    row = lambda i: (i, 0)
    return pl.pallas_call(
        _out_proj_kernel,
        grid=(M // tm,),
        in_specs=[pl.BlockSpec((tm, K), row), pl.BlockSpec((tm, D), row), _mod_spec(layer, lambda i: row_fn(i * tm)),
                  _const_spec((K, D))],
        out_specs=pl.BlockSpec((tm, D), row),
        out_shape=jax.ShapeDtypeStruct((M, D), F32),
        compiler_params=_cparams(1),
        name="out_proj",
    )(g, x, mod, w_out)


NA_SCALE = NA_HEAD_DIM ** -0.5
NA_ROWS = 32
NA_WIN = NA_KH * GRID_W
NT_DIMS = (((1,), (1,)), ((), ()))


def _na_in_kernel(x_ref, nw_ref, mod_ref, w_ref, q_ref, k_ref, v_ref, sz_ref):
    hb = _modnorm(x_ref[...], nw_ref[...], mod_ref[...]).astype(BF16)
    q_ref[...] = jnp.dot(hb, w_ref[:, 0:D], preferred_element_type=F32).astype(q_ref.dtype)
    k_ref[...] = jnp.dot(hb, w_ref[:, D:2 * D], preferred_element_type=F32).astype(k_ref.dtype)
    v_ref[...] = jnp.dot(hb, w_ref[:, 2 * D:3 * D], preferred_element_type=F32).astype(v_ref.dtype)
    sz_ref[...] = _silu(jnp.dot(hb, w_ref[:, 3 * D:4 * D], preferred_element_type=F32))


def _na_in(x, layer, row_fn, mod, nw, w_in, kv_dtype):
    M = x.shape[0]
    tm = TOKEN_TILE
    row = lambda i: (i, 0)
    blk = pl.BlockSpec((tm, D), row)
    return pl.pallas_call(
        _na_in_kernel,
        grid=(M // tm,),
        in_specs=[blk, _const_spec((1, D)), _mod_spec(layer, lambda i: row_fn(i * tm)), _const_spec((D, 4 * D))],
        out_specs=[blk, blk, blk, blk],
        out_shape=[jax.ShapeDtypeStruct((M, D), BF16), jax.ShapeDtypeStruct((M, D), kv_dtype),
                   jax.ShapeDtypeStruct((M, D), kv_dtype), jax.ShapeDtypeStruct((M, D), F32)],
        compiler_params=_cparams(1),
        name="na_in",
    )(x, nw, mod, w_in)


def _head_masks():
    lane = lax.broadcasted_iota(jnp.int32, (1, LANES), 1)
    return lane < NA_HEAD_DIM


def _na_seq_kernel(x_ref, nw_ref, mod_ref, wi_ref, wo_ref, o_ref, k_ref, v_ref, g_ref):
    n = x_ref.shape[0]
    lo = _head_masks()
    x = x_ref[...]
    hb = _modnorm(x, nw_ref[...], mod_ref[...]).astype(BF16)
    hpc = CONV_CHUNK // NA_HEAD_DIM
    for ch in range(D // CONV_CHUNK):
        cols = slice(ch * CONV_CHUNK, (ch + 1) * CONV_CHUNK)
        q = jnp.dot(hb, wi_ref[:, cols], preferred_element_type=F32).astype(BF16)
        k = jnp.dot(hb, wi_ref[:, D + cols.start:D + cols.stop], preferred_element_type=F32)
        v = jnp.dot(hb, wi_ref[:, 2 * D + cols.start:2 * D + cols.stop], preferred_element_type=F32)
        sz = _silu(jnp.dot(hb, wi_ref[:, 3 * D + cols.start:3 * D + cols.stop], preferred_element_type=F32))
        k_ref[:, ch * hpc:(ch + 1) * hpc, :] = k.reshape(n, hpc, NA_HEAD_DIM)
        v_ref[:, ch * hpc:(ch + 1) * hpc, :] = v.reshape(n, hpc, NA_HEAD_DIM)
        kb = k.astype(BF16)
        vb = v.astype(BF16)
        for pp in range(CONV_CHUNK // LANES):
            pc = slice(pp * LANES, (pp + 1) * LANES)
            qp = q[:, pc]
            zero = jnp.zeros_like(qp)
            qs = jnp.concatenate([jnp.where(lo, qp, zero), jnp.where(lo, zero, qp)], axis=0)
            s = lax.dot_general(qs, kb[:, pc], NT_DIMS, preferred_element_type=F32) * NA_SCALE
            e = jnp.exp(s - jnp.max(s, axis=-1, keepdims=True))
            l = jnp.sum(e, axis=-1, keepdims=True)
            o = jnp.dot(e.astype(BF16), vb[:, pc], preferred_element_type=F32) / l
            g = jnp.where(lo, o[:n], o[n:]) * sz[:, pc]
            g_ref[:, cols.start + pp * LANES:cols.start + (pp + 1) * LANES] = g.astype(BF16)
    y = jnp.dot(g_ref[...], wo_ref[...], preferred_element_type=F32)
    o_ref[...] = x + mod_ref[:, 2 * D:3 * D] * y


def _na_seq(x, seq_len, layer, row_fn, mod, nw, w_in, w_out):
    M = x.shape[0]
    row = lambda i: (i, 0)
    kv_blk = pl.BlockSpec((seq_len, NA_HEADS, NA_HEAD_DIM), lambda i: (i, 0, 0))
    kv_shape = jax.ShapeDtypeStruct((M, NA_HEADS, NA_HEAD_DIM), F32)
    return pl.pallas_call(
        _na_seq_kernel,
        grid=(M // seq_len,),
        in_specs=[pl.BlockSpec((seq_len, D), row), _const_spec((1, D)), _mod_spec(layer, lambda i: row_fn(i * seq_len)),
                  _const_spec((D, 4 * D)), _const_spec((D, D))],
        out_specs=[pl.BlockSpec((seq_len, D), row), kv_blk, kv_blk],
        out_shape=[jax.ShapeDtypeStruct((M, D), F32), kv_shape, kv_shape],
        scratch_shapes=[pltpu.VMEM((seq_len, D), BF16)],
        compiler_params=_cparams(1),
        name="na_seq",
    )(x, nw, mod, w_in, w_out)


def _rpb_kernel(rpb_ref, o_ref):
    h = pl.program_id(0)
    c = lax.broadcasted_iota(jnp.int32, (GRID_W, LANES), 0)
    lane = lax.broadcasted_iota(jnp.int32, (GRID_W, LANES), 1)
    kc = lane & (GRID_W - 1)
    upper = lax.broadcasted_iota(jnp.int32, (1, LANES), 1) >= GRID_W
    cs = jnp.clip(c - NA_KW // 2, 0, GRID_W - NA_KW)
    valid = (kc >= cs) & (kc < cs + NA_KW)
    d = kc - c
    n_dc = 2 * NA_KW - 1
    n_pair = 2 * NA_KH - 2
    for dri in range(n_pair):
        tile = jnp.zeros((GRID_W, LANES), F32)
        for dd in range(-(NA_KW - 1), NA_KW):
            base = h * ((2 * NA_KH - 1) * n_dc) + dri * n_dc + (dd + NA_KW - 1)
            tile = jnp.where(d == dd, jnp.where(upper, rpb_ref[base + n_dc], rpb_ref[base]), tile)
        tile = jnp.where(valid, tile, NEG_INF)
        for idx in range(NA_KH):
            jj2 = dri - (NA_KH - 1) + idx
            if jj2 % 2 == 0 and 0 <= jj2 // 2 < NA_KH // 2:
                jj = jj2 // 2
                o_ref[idx, :, jj * LANES:(jj + 1) * LANES] = tile


def _rpb_table(rpb):
    return pl.pallas_call(
        _rpb_kernel,
        grid=(NA_HEADS,),
        in_specs=[pl.BlockSpec(memory_space=pltpu.SMEM)],
        out_specs=pl.BlockSpec((NA_KH, None, GRID_W, NA_WIN), lambda h: (0, h, 0, 0)),
        out_shape=jax.ShapeDtypeStruct((NA_KH, NA_HEADS, GRID_W, NA_WIN), F32),
        compiler_params=_cparams(1),
        name="na_rpb_table",
    )(rpb.reshape(-1))


def _na_row_start(r):
    return jnp.clip(r - NA_KH // 2, 0, NA_ROWS - NA_KH)


def _na_lat_kernel(q_ref, k_ref, v_ref, ck_ref, cv_ref, bias_ref, sz_ref, o_ref):
    lo = _head_masks()
    k0 = pl.multiple_of(_na_row_start(pl.program_id(1)) * GRID_W, GRID_W)
    for p in range(NA_HEADS // 2):
        cols = slice(p * LANES, (p + 1) * LANES)
        qp = q_ref[:, cols]
        kw = k_ref[pl.ds(k0, NA_WIN), cols]
        vw = v_ref[pl.ds(k0, NA_WIN), cols]
        ck = ck_ref[:, cols]
        cv = cv_ref[:, cols]
        zero = jnp.zeros_like(qp)
        qs = jnp.concatenate([jnp.where(lo, qp, zero), jnp.where(lo, zero, qp)], axis=0)
        bias = bias_ref[2 * p:2 * p + 2].reshape(2 * GRID_W, NA_WIN)
        s_loc = lax.dot_general(qs, kw, NT_DIMS, preferred_element_type=F32) * NA_SCALE + bias
        s_ctx = lax.dot_general(qs, ck, NT_DIMS, preferred_element_type=F32) * NA_SCALE
        m = jnp.maximum(jnp.max(s_loc, axis=-1, keepdims=True), jnp.max(s_ctx, axis=-1, keepdims=True))
        e_loc = jnp.exp(s_loc - m)
        e_ctx = jnp.exp(s_ctx - m)
        l = jnp.sum(e_loc, axis=-1, keepdims=True) + jnp.sum(e_ctx, axis=-1, keepdims=True)
        o = (jnp.dot(e_loc.astype(BF16), vw, preferred_element_type=F32)
             + jnp.dot(e_ctx.astype(BF16), cv, preferred_element_type=F32)) / l
        o_ref[:, cols] = jnp.where(lo, o[:GRID_W], o[GRID_W:]) * sz_ref[:, cols]


def _na_lat(q, k, v, ck, cv, bias, sz, n_lat):
    T = NA_ROWS * GRID_W
    past = ck.shape[1]
    qblk = pl.BlockSpec((GRID_W, D), lambda b, r: (b * NA_ROWS + r, 0))
    kvblk = pl.BlockSpec((T, D), lambda b, r: (b, 0))
    cblk = pl.BlockSpec((None, past, D), lambda b, r: (b, 0, 0))
    bblk = pl.BlockSpec((None, NA_HEADS, GRID_W, NA_WIN), lambda b, r: (r - _na_row_start(r), 0, 0, 0))
    return pl.pallas_call(
        _na_lat_kernel,
        grid=(n_lat, NA_ROWS),
        in_specs=[qblk, kvblk, kvblk, cblk, cblk, bblk, qblk],
        out_specs=qblk,
        out_shape=jax.ShapeDtypeStruct((n_lat * T, D), F32),
        compiler_params=_cparams(2),
        name="na_lat_attn",
    )(q, k, v, ck, cv, bias, sz)


def _na_layer(xc, xl, layer, mod, nw, p, n_lat):
    j = layer // 3
    w_in = p["na_w_in"][j].astype(BF16)
    w_out = p["na_w_out"][j].astype(BF16)
    row_ctx = lambda tok: 0
    row_lat = lambda tok: 1 + tok // (NA_ROWS * GRID_W)
    oc, kc, vc = _na_seq(xc, SEQ, layer, row_ctx, mod, nw, w_in, w_out)
    ql, kl, vl, szl = _na_in(xl, layer, row_lat, mod, nw, w_in, BF16)
    past = p["cache_k"].shape[2]
    ck = p["cache_k"][:n_lat, j].reshape(n_lat, past, D).astype(BF16)
    cv = p["cache_v"][:n_lat, j].reshape(n_lat, past, D).astype(BF16)
    gl = _na_lat(ql, kl, vl, ck, cv, _rpb_table(p["na_rpb"][j]), szl, n_lat)
    return oc, _out_proj(gl, xl, layer, row_lat, mod, w_out), kc, vc


SSD_IN_COLS = SSD_INNER + SSD_XBC + LANES


def _softplus(x):
    return jnp.maximum(x, 0.0) + jnp.log(1.0 + jnp.exp(-jnp.abs(x)))


def _ssd_in_kernel(tiles_per_seq, *refs):
    refs = list(refs)
    x_ref = refs.pop(0)
    if tiles_per_seq > 1:
        xp_ref = refs.pop(0)
        xn_ref = refs.pop(0)
    nw_ref, mod_ref, w_ref, dtb_ref, wc_ref, bc_ref, sz_ref, xbc_ref, dt_ref, xpad_ref = refs
    hb = _modnorm(x_ref[...], nw_ref[...], mod_ref[...]).astype(BF16)
    halo = None
    if tiles_per_seq > 1:
        t = pl.program_id(0) % tiles_per_seq
        xh = jnp.concatenate([xp_ref[...], xn_ref[...]], axis=0)
        halo = (_modnorm(xh, nw_ref[...], mod_ref[...]).astype(BF16), t > 0, t < tiles_per_seq - 1)
    dt_ref[...] = _ssd_project(hb, halo, w_ref, dtb_ref, wc_ref, bc_ref, xpad_ref, sz_ref, xbc_ref)


def _ssd_project(hb, halo, w_ref, dtb_ref, wc_ref, bc_ref, xpad_ref, sz_ref, xbc_ref):
    tm = hb.shape[0]
    HL = SSD_HALO
    RB = 128
    for j in range(SSD_XBC // D):
        c0 = SSD_INNER + j * D
        raw = jnp.dot(hb, w_ref[:, c0:c0 + D], preferred_element_type=F32)
        if halo is not None:
            rawh = jnp.dot(halo[0], w_ref[:, c0:c0 + D], preferred_element_type=F32)
            raw_prev = jnp.where(halo[1], rawh[0:HL], 0.0)
            raw_next = jnp.where(halo[2], rawh[HL:2 * HL], 0.0)
        else:
            raw_prev = raw_next = jnp.zeros((HL, D), F32)
        for cb in range(D // LANES):
            blk = j * (D // LANES) + cb
            lc = slice(blk * LANES, (blk + 1) * LANES)
            cc = slice(cb * LANES, (cb + 1) * LANES)
            xpad_ref[blk, 0:HL, :] = raw_prev[:, cc]
            xpad_ref[blk, HL:HL + tm, :] = raw[:, cc]
            xpad_ref[blk, HL + tm:HL + tm + HL, :] = raw_next[:, cc]
            for rb in range(tm // RB):
                acc = jnp.zeros((RB, LANES), F32)
                for k in range(SSD_CONV):
                    r0 = rb * RB + HL - SSD_CONV // 2 + k
                    acc = acc + xpad_ref[blk, r0:r0 + RB, :] * wc_ref[k:k + 1, lc]
                xbc_ref[rb * RB:(rb + 1) * RB, lc] = _silu(acc + bc_ref[:, lc])
    for j in range(SSD_INNER // D):
        sz_ref[:, j * D:(j + 1) * D] = _silu(jnp.dot(hb, w_ref[:, j * D:(j + 1) * D], preferred_element_type=F32))
    c0 = SSD_INNER + SSD_XBC
    raw = jnp.dot(hb, w_ref[:, c0:c0 + LANES], preferred_element_type=F32) + dtb_ref[...]
    lane = lax.broadcasted_iota(jnp.int32, (1, LANES), 1)
    return jnp.where(lane < 2 * SSD_HEADS, _softplus(raw), 0.0)


def _ssd_in(x, seq_len, layer, row_fn, mod, nw, w_in, dt_bias, w_conv, b_conv):
    M = x.shape[0]
    tm = TOKEN_TILE
    HL = SSD_HALO
    tps = seq_len // tm
    n_hblk = M // HL
    row = lambda i: (i, 0)
    in_specs = [pl.BlockSpec((tm, D), row)]
    args = [x]
    if tps > 1:
        in_specs += [pl.BlockSpec((HL, D), lambda i: (jnp.maximum(i * (tm // HL) - 1, 0), 0)),
                     pl.BlockSpec((HL, D), lambda i: (jnp.minimum((i + 1) * (tm // HL), n_hblk - 1), 0))]
        args += [x, x]
    in_specs += [_const_spec((1, D)), _mod_spec(layer, lambda i: row_fn(i * tm)),
                 _const_spec((D, SSD_IN_COLS)), _const_spec((1, LANES)),
                 _const_spec((SSD_CONV + 1, SSD_XBC)), _const_spec((1, SSD_XBC))]
    args += [nw, mod, w_in, dt_bias, w_conv, b_conv]
    return pl.pallas_call(
        functools.partial(_ssd_in_kernel, tps),
        grid=(M // tm,),
        in_specs=in_specs,
        out_specs=[pl.BlockSpec((tm, SSD_INNER), row), pl.BlockSpec((tm, SSD_XBC), row), pl.BlockSpec((tm, LANES), row)],
        out_shape=[jax.ShapeDtypeStruct((M, SSD_INNER), F32), jax.ShapeDtypeStruct((M, SSD_XBC), F32),
                   jax.ShapeDtypeStruct((M, LANES), F32)],
        scratch_shapes=[pltpu.VMEM((SSD_XBC // LANES, tm + 2 * HL, LANES), F32)],
        compiler_params=_cparams(1),
        name="ssd_in",
    )(*args)


def _cumsum_rows(a, reverse):
    n = a.shape[0]
    row = lax.broadcasted_iota(jnp.int32, a.shape, 0)
    k = 1
    while k < n:
        if reverse:
            a = a + jnp.where(row < n - k, pltpu.roll(a, n - k, 0), 0.0)
        else:
            a = a + jnp.where(row >= k, pltpu.roll(a, k, 0), 0.0)
        k *= 2
    return a


def _ssd_core_kernel(nc, has_h0, emit_state, *refs):
    refs = list(refs)
    dir_refs = [refs[0:2], refs[2:4]]
    alog_ref, dsk_ref = refs[4:6]
    refs = refs[6:]
    if has_h0:
        h0_ref = refs.pop(0)
    y_refs = [refs.pop(0), refs.pop(0)]
    if emit_state:
        so_ref = refs.pop(0)
    (st_ref,) = refs
    Q, N, P = SSD_CHUNK, SSD_STATE, SSD_HEAD_DIM
    c = pl.program_id(1)
    n_blk = SSD_INNER // LANES

    @pl.when(c == 0)
    def _init():
        if has_h0:
            for dr in range(2):
                for j in range(n_blk):
                    st_ref[dr, :, j * LANES:(j + 1) * LANES] = h0_ref[dr, j * LANES:(j + 1) * LANES, :].T
        else:
            st_ref[...] = jnp.zeros(st_ref.shape, F32)

    A = -jnp.exp(alog_ref[...])
    for dr in range(2):
        x_ref, dt_ref = dir_refs[dr]
        y_ref = y_refs[dr]

        def x_at(c0, x_ref=x_ref):
            return x_ref[:, c0:c0 + LANES]

        def y_put(c0, y, y_ref=y_ref):
            y_ref[:, c0:c0 + LANES] = y

        _ssd_chunk(dr, x_at, dt_ref[...], A, st_ref, y_put, dsk_ref)

    if emit_state:
        @pl.when(c == nc - 1)
        def _emit():
            _ssd_emit_state(st_ref, so_ref)


def _ssd_emit_state(st_ref, so_ref):
    for dr in range(2):
        for j in range(SSD_INNER // LANES):
            so_ref[dr, j * LANES:(j + 1) * LANES, :] = st_ref[dr, :, j * LANES:(j + 1) * LANES].T


def _ssd_chunk(dr, x_at, dt, A, st_ref, y_put, dsk_ref):
    Q, N, P = SSD_CHUNK, SSD_STATE, SSD_HEAD_DIM
    lo = _head_masks()
    ri = lax.broadcasted_iota(jnp.int32, (Q, Q), 0)
    ci = lax.broadcasted_iota(jnp.int32, (Q, Q), 1)
    cum = _cumsum_rows(dt * A, reverse=(dr == 1))
    cum_t = cum.T
    tot_row = Q - 1 if dr == 0 else 0
    mask = (ri >= ci) if dr == 0 else (ri <= ci)
    for g in range(SSD_GROUPS):
        bg_t = x_at(SSD_INNER + g * N).T.astype(BF16)
        cg = x_at(SSD_INNER + SSD_GN + g * N).astype(BF16)
        cb = jnp.dot(cg, bg_t, preferred_element_type=F32)
        for pp in range(SSD_HEADS // SSD_GROUPS // 2):
            head = g * (SSD_HEADS // SSD_GROUPS) + 2 * pp
            c0 = head * P
            li = dr * SSD_HEADS + head
            xpair = x_at(c0)
            dt_cols = [dt[:, li:li + 1], dt[:, li + 1:li + 2]]
            cum_cols = [cum[:, li:li + 1], cum[:, li + 1:li + 2]]
            xdt = xpair * jnp.where(lo, dt_cols[0], dt_cols[1])
            xdt_b = xdt.astype(BF16)
            cum_x = jnp.where(lo, cum_cols[0], cum_cols[1])
            yd = []
            for hh in range(2):
                seg = cum_cols[hh] - cum_t[li + hh:li + hh + 1, :]
                decay = jnp.exp(jnp.where(mask, seg, -jnp.inf))
                yd.append(jnp.dot((cb * decay).astype(BF16), xdt_b, preferred_element_type=F32))
            st = st_ref[dr, :, c0:c0 + LANES]
            y = (jnp.where(lo, yd[0], yd[1])
                 + jnp.dot(cg, st.astype(BF16), preferred_element_type=F32) * jnp.exp(cum_x))
            if dr == 0:
                y = y + xpair * dsk_ref[:, c0:c0 + LANES]
            y_put(c0, y)
            tot = cum_x[tot_row:tot_row + 1, :]
            xw = (xdt * jnp.exp(tot - cum_x)).astype(BF16)
            st_ref[dr, :, c0:c0 + LANES] = st * jnp.exp(tot) + jnp.dot(bg_t, xw, preferred_element_type=F32)


def _ssd_seq_kernel(x_ref, nw_ref, mod_ref, w_ref, dtb_ref, wc_ref, bc_ref, alog_ref, dsk_ref, gnw_ref, wo_ref,
                    o_ref, so_ref, xpad_ref, xbc_ref, sz_ref, y_ref, st_ref):
    Q = SSD_CHUNK
    nc = x_ref.shape[0] // Q
    x = x_ref[...]
    hb = _modnorm(x, nw_ref[...], mod_ref[...]).astype(BF16)
    dt = _ssd_project(hb, None, w_ref, dtb_ref, wc_ref, bc_ref, xpad_ref, sz_ref, xbc_ref)
    st_ref[...] = jnp.zeros(st_ref.shape, F32)
    A = -jnp.exp(alog_ref[...])
    for dr in range(2):
        for ck in (range(nc) if dr == 0 else reversed(range(nc))):
            rows = slice(ck * Q, (ck + 1) * Q)

            def x_at(c0, rows=rows):
                return xbc_ref[rows, c0:c0 + LANES]

            def y_put(c0, y, rows=rows, dr=dr):
                if dr == 0:
                    y_ref[rows, c0:c0 + LANES] = y
                else:
                    y_ref[rows, c0:c0 + LANES] += y

            _ssd_chunk(dr, x_at, dt[rows], A, st_ref, y_put, dsk_ref)
    _ssd_emit_state(st_ref, so_ref)
    g = _rms(y_ref[...] * sz_ref[...], gnw_ref[...]).astype(BF16)
    o_ref[...] = x + mod_ref[:, 2 * D:3 * D] * jnp.dot(g, wo_ref[...], preferred_element_type=F32)


def _ssd_seq(x, seq_len, layer, row_fn, mod, nw, w_in, dt_bias, w_conv, b_conv, a_log, d_skip, norm_w, w_out):
    M = x.shape[0]
    n_seq = M // seq_len
    HL = SSD_HALO
    row = lambda i: (i, 0)
    st_blk = pl.BlockSpec((None, 2, SSD_INNER, SSD_STATE), lambda i: (i, 0, 0, 0))
    return pl.pallas_call(
        _ssd_seq_kernel,
        grid=(n_seq,),
        in_specs=[pl.BlockSpec((seq_len, D), row), _const_spec((1, D)), _mod_spec(layer, lambda i: row_fn(i * seq_len)),
                  _const_spec((D, SSD_IN_COLS)), _const_spec((1, LANES)),
                  _const_spec((SSD_CONV + 1, SSD_XBC)), _const_spec((1, SSD_XBC)),
                  _const_spec((1, LANES)), _const_spec((1, SSD_INNER)), _const_spec((1, SSD_INNER)),
                  _const_spec((SSD_INNER, D))],
        out_specs=[pl.BlockSpec((seq_len, D), row), st_blk],
        out_shape=[jax.ShapeDtypeStruct((M, D), F32), jax.ShapeDtypeStruct((n_seq, 2, SSD_INNER, SSD_STATE), F32)],
        scratch_shapes=[pltpu.VMEM((SSD_XBC // LANES, seq_len + 2 * HL, LANES), F32),
                        pltpu.VMEM((seq_len, SSD_XBC), F32), pltpu.VMEM((seq_len, SSD_INNER), F32),
                        pltpu.VMEM((seq_len, SSD_INNER), F32), pltpu.VMEM((2, SSD_STATE, SSD_INNER), F32)],
        compiler_params=_cparams(1),
        name="ssd_seq",
    )(x, nw, mod, w_in, dt_bias, w_conv, b_conv, a_log, d_skip, norm_w, w_out)


def _ssd_core(xbc, dtp, a_log, d_skip, h0, n_seq, seq_len, emit_state):
    Q = SSD_CHUNK
    nc = seq_len // Q
    M = n_seq * seq_len

    def chunk(dr):
        return (lambda b, c: b * nc + c) if dr == 0 else (lambda b, c: b * nc + nc - 1 - c)

    in_specs, args = [], []
    for dr in range(2):
        ch = chunk(dr)
        in_specs += [pl.BlockSpec((Q, SSD_XBC), lambda b, c, ch=ch: (ch(b, c), 0)),
                     pl.BlockSpec((Q, LANES), lambda b, c, ch=ch: (ch(b, c), 0))]
        args += [xbc, dtp]
    in_specs += [_const_spec((1, LANES)), _const_spec((1, SSD_INNER))]
    args += [a_log, d_skip]
    st_blk = pl.BlockSpec((None, 2, SSD_INNER, SSD_STATE), lambda b, c: (b, 0, 0, 0))
    has_h0 = h0 is not None
    if has_h0:
        in_specs.append(st_blk)
        args.append(h0)
    out_specs = [pl.BlockSpec((Q, SSD_INNER), lambda b, c: (chunk(0)(b, c), 0)),
                 pl.BlockSpec((Q, SSD_INNER), lambda b, c: (chunk(1)(b, c), 0))]
    out_shape = [jax.ShapeDtypeStruct((M, SSD_INNER), F32)] * 2
    if emit_state:
        out_specs.append(st_blk)
        out_shape.append(jax.ShapeDtypeStruct((n_seq, 2, SSD_INNER, SSD_STATE), F32))
    return pl.pallas_call(
        functools.partial(_ssd_core_kernel, nc, has_h0, emit_state),
        grid=(n_seq, nc),
        in_specs=in_specs,
        out_specs=out_specs,
        out_shape=out_shape,
        scratch_shapes=[pltpu.VMEM((2, SSD_STATE, SSD_INNER), F32)],
        compiler_params=_cparams(2),
        name="ssd_core",
    )(*args)


def _ssd_out_kernel(yf_ref, yb_ref, sz_ref, x_ref, mod_ref, nw_ref, w_ref, o_ref):
    g = _rms((yf_ref[...] + yb_ref[...]) * sz_ref[...], nw_ref[...]).astype(BF16)
    o_ref[...] = x_ref[...] + mod_ref[:, 2 * D:3 * D] * jnp.dot(g, w_ref[...], preferred_element_type=F32)


def _ssd_out(yf, yb, sz, x, layer, row_fn, mod, norm_w, w_out):
    M = x.shape[0]
    tm = TOKEN_TILE
    row = lambda i: (i, 0)
    wide = pl.BlockSpec((tm, SSD_INNER), row)
    return pl.pallas_call(
        _ssd_out_kernel,
        grid=(M // tm,),
        in_specs=[wide, wide, wide, pl.BlockSpec((tm, D), row), _mod_spec(layer, lambda i: row_fn(i * tm)),
                  _const_spec((1, SSD_INNER)), _const_spec((SSD_INNER, D))],
        out_specs=pl.BlockSpec((tm, D), row),
        out_shape=jax.ShapeDtypeStruct((M, D), F32),
        compiler_params=_cparams(1),
        name="ssd_out",
    )(yf, yb, sz, x, mod, norm_w, w_out)


def _ssd_layer(xc, xl, layer, mod, nw, p, n_lat):
    j = layer // 3
    w_in = jnp.pad(p["ssd_w_in"][j].astype(BF16), ((0, 0), (0, LANES - 2 * SSD_HEADS)))
    dt_bias = jnp.pad(p["ssd_dt_bias"][j].reshape(1, -1), ((0, 0), (0, LANES - 2 * SSD_HEADS)))
    a_log = jnp.pad(p["ssd_a_log"][j].reshape(1, -1), ((0, 0), (0, LANES - 2 * SSD_HEADS)))
    w_conv = jnp.pad(p["ssd_w_conv"][j], ((0, 1), (0, 0)))
    b_conv = p["ssd_b_conv"][j][None]
    d_skip = jnp.repeat(p["ssd_d"][j], SSD_HEAD_DIM)[None]
    norm_w = p["ssd_norm_w"][j][None]
    w_out = p["ssd_w_out"][j].astype(BF16)
    row_ctx = lambda tok: 0
    row_lat = lambda tok: 1 + tok // DEC_SEQ
    oc, s_new = _ssd_seq(xc, SEQ, layer, row_ctx, mod, nw, w_in, dt_bias, w_conv, b_conv, a_log, d_skip, norm_w, w_out)
    szl, xbc_l, dt_l = _ssd_in(xl, DEC_SEQ, layer, row_lat, mod, nw, w_in, dt_bias, w_conv, b_conv)
    h0 = p["state_ssm"][:n_lat, j].reshape(n_lat, 2, SSD_INNER, SSD_STATE)
    yfl, ybl = _ssd_core(xbc_l, dt_l, a_log, d_skip, h0, n_lat, DEC_SEQ, False)
    ol = _ssd_out(yfl, ybl, szl, xl, layer, row_lat, mod, norm_w, w_out)
    return oc, ol, s_new


def kernel(x_prompt, x_sample, cache_k, cache_v, state_ssm, c, c_ctx, ada_w, ada_b, norm_w, final_norm_w, conv_w_in, conv_b_in, conv_w_dw, conv_b_dw, conv_ln_w, conv_ln_b, conv_w_out, conv_b_out, na_w_in, na_rpb, na_w_out, ssd_w_in, ssd_w_conv, ssd_b_conv, ssd_a_log, ssd_dt_bias, ssd_d, ssd_norm_w, ssd_w_out):
    p = dict(cache_k=cache_k, cache_v=cache_v, state_ssm=state_ssm,
             conv_w_in=conv_w_in, conv_b_in=conv_b_in, conv_w_dw=conv_w_dw, conv_b_dw=conv_b_dw,
             conv_ln_w=conv_ln_w, conv_ln_b=conv_ln_b, conv_w_out=conv_w_out, conv_b_out=conv_b_out,
             na_w_in=na_w_in, na_rpb=na_rpb, na_w_out=na_w_out,
             ssd_w_in=ssd_w_in, ssd_w_conv=ssd_w_conv, ssd_b_conv=ssd_b_conv, ssd_a_log=ssd_a_log,
             ssd_dt_bias=ssd_dt_bias, ssd_d=ssd_d, ssd_norm_w=ssd_norm_w, ssd_w_out=ssd_w_out)
    n_ctx, n_lat = x_prompt.shape[0], x_sample.shape[0]
    assert x_prompt.shape[1:] == (SEQ, D) and x_sample.shape[1:] == (DEC_SEQ, D)
    assert (DEPTH - 1) % 3 == 0, "the final RMSNorm is fused into a trailing Conformer layer"
    xc = x_prompt.reshape(n_ctx * SEQ, D)
    xl = x_sample.reshape(n_lat * DEC_SEQ, D)
    cond8 = jnp.concatenate([c_ctx[None], c, jnp.zeros((8 - 1 - n_lat, D), F32)], axis=0)
    mod = _modulation(cond8, ada_w, ada_b).reshape(DEPTH, 8, 1, 3 * D)
    row_ctx = lambda tok: 0
    row_lat = lambda tok: 1 + tok // DEC_SEQ
    new_k, new_v, new_s = [], [], []
    for i in range(DEPTH):
        kind, j = i % 3, i // 3
        nw = norm_w[i][None]
        if kind == 0:
            cp = _prep_conv(p, j)
            fw = final_norm_w[None] if i == DEPTH - 1 else None
            xc = _conv_layer(xc, SEQ, i, row_ctx, mod, nw, *cp, fw)
            xl = _conv_layer(xl, DEC_SEQ, i, row_lat, mod, nw, *cp, fw)
        elif kind == 1:
            xc, xl, k_c, v_c = _na_layer(xc, xl, i, mod, nw, p, n_lat)
            new_k.append(k_c.reshape(n_ctx, SEQ, NA_HEADS, NA_HEAD_DIM))
            new_v.append(v_c.reshape(n_ctx, SEQ, NA_HEADS, NA_HEAD_DIM))
        else:
            xc, xl, s_c = _ssd_layer(xc, xl, i, mod, nw, p, n_lat)
            new_s.append(s_c.reshape(n_ctx, 2, SSD_HEADS, SSD_HEAD_DIM, SSD_STATE))
    return (xc.reshape(n_ctx, SEQ, D), xl.reshape(n_lat, DEC_SEQ, D),
            jnp.stack(new_k, axis=1), jnp.stack(new_v, axis=1), jnp.stack(new_s, axis=1))
```

```python
import functools

import jax
import jax.numpy as jnp
from jax import lax
from jax.experimental import pallas as pl
from jax.experimental.pallas import tpu as pltpu

F32 = jnp.float32
BF16 = jnp.bfloat16

D = 1024
DEPTH = 4
SEQ = 256
DEC_SEQ = 2048
GRID_W = 64
RMS_EPS = 1e-6
LN_EPS = 1e-5
LOG2E = 1.4426950408889634
CONF_K = 31
NA_HEADS = 16
NA_HEAD_DIM = 64
NA_KH = 8
NA_KW = 16
NEG_INF = -1e30
SSD_INNER = 2048
SSD_HEADS = 32
SSD_HEAD_DIM = 64
SSD_STATE = 128
SSD_GROUPS = 4
SSD_GN = SSD_GROUPS * SSD_STATE
SSD_CONV = 7
SSD_CHUNK = 128
SSD_XBC = SSD_INNER + 2 * SSD_GN

LANES = 128
SUBLANES = 8
TOKEN_TILE = 256
CONV_HALO = 16
SSD_HALO = 8
VMEM_LIMIT = 56 * 1024 * 1024


def _cparams(n_axes):
    return pltpu.CompilerParams(dimension_semantics=("arbitrary",) * n_axes,
                                vmem_limit_bytes=VMEM_LIMIT)


def _const_spec(shape):
    nd = len(shape)
    return pl.BlockSpec(shape, lambda *_: (0,) * nd, pipeline_mode=pl.Buffered(1))


def _sigmoid(x):
    return 1.0 / (1.0 + jnp.exp(-x))


def _silu(x):
    return x * _sigmoid(x)


def _rms(x, w):
    ms = jnp.mean(x * x, axis=-1, keepdims=True)
    return x * lax.rsqrt(ms + RMS_EPS) * w


def _modnorm(x, nw, mod):
    return _rms(x, nw) * (1.0 + mod[:, D:2 * D]) + mod[:, :D]


def _mod_spec(layer, row_fn):
    return pl.BlockSpec((None, None, 1, 3 * D), lambda i, *_: (layer, row_fn(i), 0, 0))


def _mod_kernel(cond_ref, w_ref, b_ref, o_ref):
    s = _silu(cond_ref[...]).astype(BF16)
    o_ref[...] = jnp.dot(s, w_ref[...].astype(BF16), preferred_element_type=F32) + b_ref[...]


def _modulation(cond8, ada_w, ada_b):
    tn = 1024
    return pl.pallas_call(
        _mod_kernel,
        grid=(DEPTH, 3 * D // tn),
        in_specs=[pl.BlockSpec((8, D), lambda i, j: (0, 0)),
                  pl.BlockSpec((None, D, tn), lambda i, j: (i, 0, j)),
                  pl.BlockSpec((None, 1, tn), lambda i, j: (i, 0, j))],
        out_specs=pl.BlockSpec((None, 8, tn), lambda i, j: (i, 0, j)),
        out_shape=jax.ShapeDtypeStruct((DEPTH, 8, 3 * D), F32),
        compiler_params=_cparams(2),
        name="adaln_mod",
    )(cond8, ada_w, ada_b.reshape(DEPTH, 1, 3 * D))


CONV_CHUNK = 256


def _conv_kernel(tiles_per_seq, final, *refs):
    refs = list(refs)
    x_ref = refs.pop(0)
    if tiles_per_seq > 1:
        xp_ref = refs.pop(0)
        xn_ref = refs.pop(0)
    nw_ref, mod_ref, wi_ref, bi_ref, wdw_ref, bdw_ref, lnw_ref, lnb_ref, wo_ref, bo_ref = refs[:10]
    refs = refs[10:]
    if final:
        fw_ref = refs.pop(0)
    o_ref, ubuf, cbuf = refs
    T = x_ref.shape[0]
    H = CONV_HALO
    RB = 128

    hb = _modnorm(x_ref[...], nw_ref[...], mod_ref[...]).astype(BF16)
    if tiles_per_seq > 1:
        t = pl.program_id(0) % tiles_per_seq
        hh = _modnorm(jnp.concatenate([xp_ref[...], xn_ref[...]], axis=0), nw_ref[...], mod_ref[...]).astype(BF16)

    def glu(h, cols):
        v = jnp.dot(h, wi_ref[:, cols], preferred_element_type=F32) + bi_ref[:, cols]
        gcols = slice(D + cols.start, D + cols.stop)
        g = jnp.dot(h, wi_ref[:, gcols], preferred_element_type=F32) + bi_ref[:, gcols]
        return v * _sigmoid(g)

    for ch in range(D // CONV_CHUNK):
        cols = slice(ch * CONV_CHUNK, (ch + 1) * CONV_CHUNK)
        u_main = glu(hb, cols)
        if tiles_per_seq > 1:
            uh = glu(hh, cols)
            u_prev = jnp.where(t > 0, uh[0:H], 0.0)
            u_next = jnp.where(t < tiles_per_seq - 1, uh[H:2 * H], 0.0)
        else:
            u_prev = u_next = jnp.zeros((H, CONV_CHUNK), F32)
        for cb in range(CONV_CHUNK // LANES):
            blk = ch * (CONV_CHUNK // LANES) + cb
            lc = slice(blk * LANES, (blk + 1) * LANES)
            cc = slice(cb * LANES, (cb + 1) * LANES)
            ubuf[blk, 0:H, :] = u_prev[:, cc]
            ubuf[blk, H:H + T, :] = u_main[:, cc]
            ubuf[blk, H + T:H + T + H, :] = u_next[:, cc]
            for rb in range(T // RB):
                acc = jnp.zeros((RB, LANES), F32)
                for k in range(CONF_K):
                    r0 = rb * RB + H - CONF_K // 2 + k
                    acc = acc + ubuf[blk, r0:r0 + RB, :] * wdw_ref[k:k + 1, lc]
                cbuf[rb * RB:(rb + 1) * RB, lc] = acc + bdw_ref[:, lc]

    sz = _silu(jnp.dot(hb, wi_ref[:, 2 * D:3 * D], preferred_element_type=F32) + bi_ref[:, 2 * D:3 * D])
    c = cbuf[...]
    mu = jnp.mean(c, axis=-1, keepdims=True)
    xc = c - mu
    var = jnp.mean(xc * xc, axis=-1, keepdims=True)
    ln = xc * lax.rsqrt(var + LN_EPS) * lnw_ref[...] + lnb_ref[...]
    g = (_silu(ln) * sz).astype(BF16)
    y = jnp.dot(g, wo_ref[...], preferred_element_type=F32) + bo_ref[...]
    xn = x_ref[...] + mod_ref[:, 2 * D:3 * D] * y
    if final:
        xn = _rms(xn, fw_ref[...])
    o_ref[...] = xn


def _conv_layer(x, seq_len, layer, row_fn, mod, nw, w_in, b_in, w_dw, b_dw, ln_w, ln_b, w_out, b_out, final_w):
    M = x.shape[0]
    T = TOKEN_TILE
    tps = seq_len // T
    H = CONV_HALO
    n_halo_blocks = M // H
    row = lambda i: (i, 0)
    in_specs = [pl.BlockSpec((T, D), row)]
    args = [x]
    if tps > 1:
        in_specs += [pl.BlockSpec((H, D), lambda i: (jnp.maximum(i * (T // H) - 1, 0), 0)),
                     pl.BlockSpec((H, D), lambda i: (jnp.minimum((i + 1) * (T // H), n_halo_blocks - 1), 0))]
        args += [x, x]
    in_specs += [_const_spec((1, D)), _mod_spec(layer, lambda i: row_fn(i * T)),
                 _const_spec((D, 3 * D)), _const_spec((1, 3 * D)),
                 _const_spec((CONF_K + 1, D)), _const_spec((1, D)), _const_spec((1, D)), _const_spec((1, D)),
                 _const_spec((D, D)), _const_spec((1, D))]
    args += [nw, mod, w_in, b_in, w_dw, b_dw, ln_w, ln_b, w_out, b_out]
    final = final_w is not None
    if final:
        in_specs.append(_const_spec((1, D)))
        args.append(final_w)
    return pl.pallas_call(
        functools.partial(_conv_kernel, tps, final),
        grid=(M // T,),
        in_specs=in_specs,
        out_specs=pl.BlockSpec((T, D), row),
        out_shape=jax.ShapeDtypeStruct((M, D), F32),
        scratch_shapes=[pltpu.VMEM((D // LANES, T + 2 * H, LANES), F32), pltpu.VMEM((T, D), F32)],
        compiler_params=_cparams(1),
        name="conv_layer",
    )(*args)


def _prep_conv(p, j):
    w_dw = jnp.pad(p["conv_w_dw"][j], ((0, 1), (0, 0)))
    return (p["conv_w_in"][j].astype(BF16), p["conv_b_in"][j][None], w_dw, p["conv_b_dw"][j][None],
            p["conv_ln_w"][j][None], p["conv_ln_b"][j][None], p["conv_w_out"][j].astype(BF16), p["conv_b_out"][j][None])


def _out_proj_kernel(g_ref, x_ref, mod_ref, w_ref, o_ref):
    y = jnp.dot(g_ref[...].astype(BF16), w_ref[...], preferred_element_type=F32)
    o_ref[...] = x_ref[...] + mod_ref[:, 2 * D:3 * D] * y


def _out_proj(g, x, layer, row_fn, mod, w_out):
    M, K = g.shape
    tm = TOKEN_TILE
    row = lambda i: (i, 0)
    return pl.pallas_call(
        _out_proj_kernel,
        grid=(M // tm,),
        in_specs=[pl.BlockSpec((tm, K), row), pl.BlockSpec((tm, D), row), _mod_spec(layer, lambda i: row_fn(i * tm)),
                  _const_spec((K, D))],
        out_specs=pl.BlockSpec((tm, D), row),
        out_shape=jax.ShapeDtypeStruct((M, D), F32),
        compiler_params=_cparams(1),
        name="out_proj",
    )(g, x, mod, w_out)


NA_SCALE = NA_HEAD_DIM ** -0.5
NA_ROWS = 32
NA_WIN = NA_KH * GRID_W
NT_DIMS = (((1,), (1,)), ((), ()))


def _na_in_kernel(x_ref, nw_ref, mod_ref, w_ref, q_ref, k_ref, v_ref, sz_ref):
    hb = _modnorm(x_ref[...], nw_ref[...], mod_ref[...]).astype(BF16)
    q_ref[...] = jnp.dot(hb, w_ref[:, 0:D], preferred_element_type=F32).astype(q_ref.dtype)
    k_ref[...] = jnp.dot(hb, w_ref[:, D:2 * D], preferred_element_type=F32).astype(k_ref.dtype)
    v_ref[...] = jnp.dot(hb, w_ref[:, 2 * D:3 * D], preferred_element_type=F32).astype(v_ref.dtype)
    sz_ref[...] = _silu(jnp.dot(hb, w_ref[:, 3 * D:4 * D], preferred_element_type=F32))


def _na_in(x, layer, row_fn, mod, nw, w_in, kv_dtype):
    M = x.shape[0]
    tm = TOKEN_TILE
    row = lambda i: (i, 0)
    blk = pl.BlockSpec((tm, D), row)
    return pl.pallas_call(
        _na_in_kernel,
        grid=(M // tm,),
        in_specs=[blk, _const_spec((1, D)), _mod_spec(layer, lambda i: row_fn(i * tm)), _const_spec((D, 4 * D))],
        out_specs=[blk, blk, blk, blk],
        out_shape=[jax.ShapeDtypeStruct((M, D), BF16), jax.ShapeDtypeStruct((M, D), kv_dtype),
                   jax.ShapeDtypeStruct((M, D), kv_dtype), jax.ShapeDtypeStruct((M, D), F32)],
        compiler_params=_cparams(1),
        name="na_in",
    )(x, nw, mod, w_in)


def _head_masks():
    lane = lax.broadcasted_iota(jnp.int32, (1, LANES), 1)
    return lane < NA_HEAD_DIM


def _na_seq_kernel(x_ref, nw_ref, mod_ref, wi_ref, wo_ref, o_ref, k_ref, v_ref, g_ref):
    n = x_ref.shape[0]
    lo = _head_masks()
    x = x_ref[...]
    hb = _modnorm(x, nw_ref[...], mod_ref[...]).astype(BF16)
    hpc = CONV_CHUNK // NA_HEAD_DIM
    for ch in range(D // CONV_CHUNK):
        cols = slice(ch * CONV_CHUNK, (ch + 1) * CONV_CHUNK)
        q = jnp.dot(hb, wi_ref[:, cols], preferred_element_type=F32).astype(BF16)
        k = jnp.dot(hb, wi_ref[:, D + cols.start:D + cols.stop], preferred_element_type=F32)
        v = jnp.dot(hb, wi_ref[:, 2 * D + cols.start:2 * D + cols.stop], preferred_element_type=F32)
        sz = _silu(jnp.dot(hb, wi_ref[:, 3 * D + cols.start:3 * D + cols.stop], preferred_element_type=F32))
        k_ref[:, ch * hpc:(ch + 1) * hpc, :] = k.reshape(n, hpc, NA_HEAD_DIM)
        v_ref[:, ch * hpc:(ch + 1) * hpc, :] = v.reshape(n, hpc, NA_HEAD_DIM)
        kb = k.astype(BF16)
        vb = v.astype(BF16)
        for pp in range(CONV_CHUNK // LANES):
            pc = slice(pp * LANES, (pp + 1) * LANES)
            qp = q[:, pc]
            zero = jnp.zeros_like(qp)
            qs = jnp.concatenate([jnp.where(lo, qp, zero), jnp.where(lo, zero, qp)], axis=0)
            s = lax.dot_general(qs, kb[:, pc], NT_DIMS, preferred_element_type=F32) * NA_SCALE
            e = jnp.exp(s - jnp.max(s, axis=-1, keepdims=True))
            l = jnp.sum(e, axis=-1, keepdims=True)
            o = jnp.dot(e.astype(BF16), vb[:, pc], preferred_element_type=F32) / l
            g = jnp.where(lo, o[:n], o[n:]) * sz[:, pc]
            g_ref[:, cols.start + pp * LANES:cols.start + (pp + 1) * LANES] = g.astype(BF16)
    y = jnp.dot(g_ref[...], wo_ref[...], preferred_element_type=F32)
    o_ref[...] = x + mod_ref[:, 2 * D:3 * D] * y


def _na_seq(x, seq_len, layer, row_fn, mod, nw, w_in, w_out):
    M = x.shape[0]
    row = lambda i: (i, 0)
    kv_blk = pl.BlockSpec((seq_len, NA_HEADS, NA_HEAD_DIM), lambda i: (i, 0, 0))
    kv_shape = jax.ShapeDtypeStruct((M, NA_HEADS, NA_HEAD_DIM), F32)
    return pl.pallas_call(
        _na_seq_kernel,
        grid=(M // seq_len,),
        in_specs=[pl.BlockSpec((seq_len, D), row), _const_spec((1, D)), _mod_spec(layer, lambda i: row_fn(i * seq_len)),
                  _const_spec((D, 4 * D)), _const_spec((D, D))],
        out_specs=[pl.BlockSpec((seq_len, D), row), kv_blk, kv_blk],
        out_shape=[jax.ShapeDtypeStruct((M, D), F32), kv_shape, kv_shape],
        scratch_shapes=[pltpu.VMEM((seq_len, D), BF16)],
        compiler_params=_cparams(1),
        name="na_seq",
    )(x, nw, mod, w_in, w_out)


def _rpb_kernel(rpb_ref, o_ref):
    h = pl.program_id(0)
    c = lax.broadcasted_iota(jnp.int32, (GRID_W, LANES), 0)
    lane = lax.broadcasted_iota(jnp.int32, (GRID_W, LANES), 1)
    kc = lane & (GRID_W - 1)
    upper = lax.broadcasted_iota(jnp.int32, (1, LANES), 1) >= GRID_W
    cs = jnp.clip(c - NA_KW // 2, 0, GRID_W - NA_KW)
    valid = (kc >= cs) & (kc < cs + NA_KW)
    d = kc - c
    n_dc = 2 * NA_KW - 1
    n_pair = 2 * NA_KH - 2
    for dri in range(n_pair):
        tile = jnp.zeros((GRID_W, LANES), F32)
        for dd in range(-(NA_KW - 1), NA_KW):
            base = h * ((2 * NA_KH - 1) * n_dc) + dri * n_dc + (dd + NA_KW - 1)
            tile = jnp.where(d == dd, jnp.where(upper, rpb_ref[base + n_dc], rpb_ref[base]), tile)
        tile = jnp.where(valid, tile, NEG_INF)
        for idx in range(NA_KH):
            jj2 = dri - (NA_KH - 1) + idx
            if jj2 % 2 == 0 and 0 <= jj2 // 2 < NA_KH // 2:
                jj = jj2 // 2
                o_ref[idx, :, jj * LANES:(jj + 1) * LANES] = tile


def _rpb_table(rpb):
    return pl.pallas_call(
        _rpb_kernel,
        grid=(NA_HEADS,),
        in_specs=[pl.BlockSpec(memory_space=pltpu.SMEM)],
        out_specs=pl.BlockSpec((NA_KH, None, GRID_W, NA_WIN), lambda h: (0, h, 0, 0)),
        out_shape=jax.ShapeDtypeStruct((NA_KH, NA_HEADS, GRID_W, NA_WIN), F32),
        compiler_params=_cparams(1),
        name="na_rpb_table",
    )(rpb.reshape(-1))


def _na_row_start(r):
    return jnp.clip(r - NA_KH // 2, 0, NA_ROWS - NA_KH)


def _na_lat_kernel(q_ref, k_ref, v_ref, ck_ref, cv_ref, bias_ref, sz_ref, o_ref):
    lo = _head_masks()
    k0 = pl.multiple_of(_na_row_start(pl.program_id(1)) * GRID_W, GRID_W)
    for p in range(NA_HEADS // 2):
        cols = slice(p * LANES, (p + 1) * LANES)
        qp = q_ref[:, cols]
        kw = k_ref[pl.ds(k0, NA_WIN), cols]
        vw = v_ref[pl.ds(k0, NA_WIN), cols]
        ck = ck_ref[:, cols]
        cv = cv_ref[:, cols]
        zero = jnp.zeros_like(qp)
        qs = jnp.concatenate([jnp.where(lo, qp, zero), jnp.where(lo, zero, qp)], axis=0)
        bias = bias_ref[2 * p:2 * p + 2].reshape(2 * GRID_W, NA_WIN)
        s_loc = lax.dot_general(qs, kw, NT_DIMS, preferred_element_type=F32) * NA_SCALE + bias
        s_ctx = lax.dot_general(qs, ck, NT_DIMS, preferred_element_type=F32) * NA_SCALE
        m = jnp.maximum(jnp.max(s_loc, axis=-1, keepdims=True), jnp.max(s_ctx, axis=-1, keepdims=True))
        e_loc = jnp.exp(s_loc - m)
        e_ctx = jnp.exp(s_ctx - m)
        l = jnp.sum(e_loc, axis=-1, keepdims=True) + jnp.sum(e_ctx, axis=-1, keepdims=True)
        o = (jnp.dot(e_loc.astype(BF16), vw, preferred_element_type=F32)
             + jnp.dot(e_ctx.astype(BF16), cv, preferred_element_type=F32)) / l
        o_ref[:, cols] = jnp.where(lo, o[:GRID_W], o[GRID_W:]) * sz_ref[:, cols]


def _na_lat(q, k, v, ck, cv, bias, sz, n_lat):
    T = NA_ROWS * GRID_W
    past = ck.shape[1]
    qblk = pl.BlockSpec((GRID_W, D), lambda b, r: (b * NA_ROWS + r, 0))
    kvblk = pl.BlockSpec((T, D), lambda b, r: (b, 0))
    cblk = pl.BlockSpec((None, past, D), lambda b, r: (b, 0, 0))
    bblk = pl.BlockSpec((None, NA_HEADS, GRID_W, NA_WIN), lambda b, r: (r - _na_row_start(r), 0, 0, 0))
    return pl.pallas_call(
        _na_lat_kernel,
        grid=(n_lat, NA_ROWS),
        in_specs=[qblk, kvblk, kvblk, cblk, cblk, bblk, qblk],
        out_specs=qblk,
        out_shape=jax.ShapeDtypeStruct((n_lat * T, D), F32),
        compiler_params=_cparams(2),
        name="na_lat_attn",
    )(q, k, v, ck, cv, bias, sz)


def _na_layer(xc, xl, layer, mod, nw, p, n_lat):
    j = layer // 3
    w_in = p["na_w_in"][j].astype(BF16)
    w_out = p["na_w_out"][j].astype(BF16)
    row_ctx = lambda tok: 0
    row_lat = lambda tok: 1 + tok // (NA_ROWS * GRID_W)
    oc, kc, vc = _na_seq(xc, SEQ, layer, row_ctx, mod, nw, w_in, w_out)
    ql, kl, vl, szl = _na_in(xl, layer, row_lat, mod, nw, w_in, BF16)
    past = p["cache_k"].shape[2]
    ck = p["cache_k"][:n_lat, j].reshape(n_lat, past, D).astype(BF16)
    cv = p["cache_v"][:n_lat, j].reshape(n_lat, past, D).astype(BF16)
    gl = _na_lat(ql, kl, vl, ck, cv, _rpb_table(p["na_rpb"][j]), szl, n_lat)
    return oc, _out_proj(gl, xl, layer, row_lat, mod, w_out), kc, vc


SSD_IN_COLS = SSD_INNER + SSD_XBC


def _softplus(x):
    return jnp.maximum(x, 0.0) + jnp.log(1.0 + jnp.exp(-jnp.abs(x)))


def _ssd_in_kernel(tiles_per_seq, *refs):
    refs = list(refs)
    x_ref = refs.pop(0)
    if tiles_per_seq > 1:
        xp_ref = refs.pop(0)
        xn_ref = refs.pop(0)
    nw_ref, mod_ref, w_ref, wdt_ref, dtb_ref, wc_ref, bc_ref, sz_ref, xbc_ref, dt_ref, xpad_ref = refs
    hb = _modnorm(x_ref[...], nw_ref[...], mod_ref[...]).astype(BF16)
    halo = None
    if tiles_per_seq > 1:
        t = pl.program_id(0) % tiles_per_seq
        xh = jnp.concatenate([xp_ref[...], xn_ref[...]], axis=0)
        halo = (_modnorm(xh, nw_ref[...], mod_ref[...]).astype(BF16), t > 0, t < tiles_per_seq - 1)
    dt_ref[...] = _ssd_project(hb, halo, w_ref, wdt_ref, dtb_ref, wc_ref, bc_ref, xpad_ref, sz_ref, xbc_ref)


def _ssd_project(hb, halo, w_ref, wdt_ref, dtb_ref, wc_ref, bc_ref, xpad_ref, sz_ref, xbc_ref):
    tm = hb.shape[0]
    HL = SSD_HALO
    RB = 128
    for j in range(SSD_XBC // D):
        c0 = SSD_INNER + j * D
        raw = jnp.dot(hb, w_ref[:, c0:c0 + D], preferred_element_type=F32)
        if halo is not None:
            rawh = jnp.dot(halo[0], w_ref[:, c0:c0 + D], preferred_element_type=F32)
            raw_prev = jnp.where(halo[1], rawh[0:HL], 0.0)
            raw_next = jnp.where(halo[2], rawh[HL:2 * HL], 0.0)
        else:
            raw_prev = raw_next = jnp.zeros((HL, D), F32)
        for cb in range(D // LANES):
            blk = j * (D // LANES) + cb
            lc = slice(blk * LANES, (blk + 1) * LANES)
            cc = slice(cb * LANES, (cb + 1) * LANES)
            xpad_ref[blk, 0:HL, :] = raw_prev[:, cc]
            xpad_ref[blk, HL:HL + tm, :] = raw[:, cc]
            xpad_ref[blk, HL + tm:HL + tm + HL, :] = raw_next[:, cc]
            for rb in range(tm // RB):
                acc = jnp.zeros((RB, LANES), F32)
                for k in range(SSD_CONV):
                    r0 = rb * RB + HL - SSD_CONV // 2 + k
                    acc = acc + xpad_ref[blk, r0:r0 + RB, :] * wc_ref[k:k + 1, lc]
                xbc_ref[rb * RB:(rb + 1) * RB, lc] = _silu(acc + bc_ref[:, lc])
    for j in range(SSD_INNER // D):
        sz_ref[:, j * D:(j + 1) * D] = _silu(jnp.dot(hb, w_ref[:, j * D:(j + 1) * D], preferred_element_type=F32))
    raw = jnp.dot(hb, wdt_ref[...], preferred_element_type=F32) + dtb_ref[...]
    lane = lax.broadcasted_iota(jnp.int32, (1, LANES), 1)
    return jnp.where(lane < 2 * SSD_HEADS, _softplus(raw), 0.0)


def _ssd_in(x, seq_len, layer, row_fn, mod, nw, w_in, w_dt, dt_bias, w_conv, b_conv):
    M = x.shape[0]
    tm = TOKEN_TILE
    HL = SSD_HALO
    tps = seq_len // tm
    n_hblk = M // HL
    row = lambda i: (i, 0)
    in_specs = [pl.BlockSpec((tm, D), row)]
    args = [x]
    if tps > 1:
        in_specs += [pl.BlockSpec((HL, D), lambda i: (jnp.maximum(i * (tm // HL) - 1, 0), 0)),
                     pl.BlockSpec((HL, D), lambda i: (jnp.minimum((i + 1) * (tm // HL), n_hblk - 1), 0))]
        args += [x, x]
    in_specs += [_const_spec((1, D)), _mod_spec(layer, lambda i: row_fn(i * tm)),
                 _const_spec((D, SSD_IN_COLS)), _const_spec((D, LANES)), _const_spec((1, LANES)),
                 _const_spec((SSD_CONV + 1, SSD_XBC)), _const_spec((1, SSD_XBC))]
    args += [nw, mod, w_in, w_dt, dt_bias, w_conv, b_conv]
    return pl.pallas_call(
        functools.partial(_ssd_in_kernel, tps),
        grid=(M // tm,),
        in_specs=in_specs,
        out_specs=[pl.BlockSpec((tm, SSD_INNER), row), pl.BlockSpec((tm, SSD_XBC), row), pl.BlockSpec((tm, LANES), row)],
        out_shape=[jax.ShapeDtypeStruct((M, SSD_INNER), F32), jax.ShapeDtypeStruct((M, SSD_XBC), F32),
                   jax.ShapeDtypeStruct((M, LANES), F32)],
        scratch_shapes=[pltpu.VMEM((SSD_XBC // LANES, tm + 2 * HL, LANES), F32)],
        compiler_params=_cparams(1),
        name="ssd_in",
    )(*args)


def _cumsum_rows(a, reverse):
    n = a.shape[0]
    row = lax.broadcasted_iota(jnp.int32, a.shape, 0)
    k = 1
    while k < n:
        if reverse:
            a = a + jnp.where(row < n - k, pltpu.roll(a, n - k, 0), 0.0)
        else:
            a = a + jnp.where(row >= k, pltpu.roll(a, k, 0), 0.0)
        k *= 2
    return a


def _ssd_core_kernel(nc, has_h0, emit_state, *refs):
    refs = list(refs)
    dir_refs = [refs[0:2], refs[2:4]]
    alog_ref, dsk_ref = refs[4:6]
    refs = refs[6:]
    if has_h0:
        h0_ref = refs.pop(0)
    y_refs = [refs.pop(0), refs.pop(0)]
    if emit_state:
        so_ref = refs.pop(0)
    (st_ref,) = refs
    Q, N, P = SSD_CHUNK, SSD_STATE, SSD_HEAD_DIM
    c = pl.program_id(1)
    n_blk = SSD_INNER // LANES

    @pl.when(c == 0)
    def _init():
        if has_h0:
            for dr in range(2):
                for j in range(n_blk):
                    st_ref[dr, :, j * LANES:(j + 1) * LANES] = h0_ref[dr, j * LANES:(j + 1) * LANES, :].T
        else:
            st_ref[...] = jnp.zeros(st_ref.shape, F32)

    A = -jnp.exp(alog_ref[...])
    for dr in range(2):
        x_ref, dt_ref = dir_refs[dr]
        y_ref = y_refs[dr]

        def x_at(c0, x_ref=x_ref):
            return x_ref[:, c0:c0 + LANES]

        def y_put(c0, y, y_ref=y_ref):
            y_ref[:, c0:c0 + LANES] = y

        _ssd_chunk(dr, x_at, dt_ref[...], A, st_ref, y_put, dsk_ref)

    if emit_state:
        @pl.when(c == nc - 1)
        def _emit():
            _ssd_emit_state(st_ref, so_ref)


def _ssd_emit_state(st_ref, so_ref):
    for dr in range(2):
        for j in range(SSD_INNER // LANES):
            so_ref[dr, j * LANES:(j + 1) * LANES, :] = st_ref[dr, :, j * LANES:(j + 1) * LANES].T


def _ssd_chunk(dr, x_at, dt, A, st_ref, y_put, dsk_ref):
    Q, N, P = SSD_CHUNK, SSD_STATE, SSD_HEAD_DIM
    lo = _head_masks()
    ri = lax.broadcasted_iota(jnp.int32, (Q, Q), 0)
    ci = lax.broadcasted_iota(jnp.int32, (Q, Q), 1)
    cum = _cumsum_rows(dt * A, reverse=(dr == 1)) * LOG2E
    cum_t = cum.T
    dt_t = dt.T
    tot_row = Q - 1 if dr == 0 else 0
    mask = (ri >= ci) if dr == 0 else (ri <= ci)
    for g in range(SSD_GROUPS):
        bg_t = x_at(SSD_INNER + g * N).T
        cg = x_at(SSD_INNER + SSD_GN + g * N).astype(BF16)
        cb = jnp.dot(cg, bg_t.astype(BF16), preferred_element_type=F32)
        for pp in range(SSD_HEADS // SSD_GROUPS // 2):
            head = g * (SSD_HEADS // SSD_GROUPS) + 2 * pp
            c0 = head * P
            li = dr * SSD_HEADS + head
            xpair = x_at(c0)
            xpair_b = xpair.astype(BF16)
            cum_cols = [cum[:, li:li + 1], cum[:, li + 1:li + 2]]
            cum_x = jnp.where(lo, cum_cols[0], cum_cols[1])
            yd, sd = [], []
            for hh in range(2):
                dt_row = dt_t[li + hh:li + hh + 1, :]
                cum_row = cum_t[li + hh:li + hh + 1, :]
                decay = jnp.exp2(jnp.where(mask, cum_cols[hh] - cum_row, -jnp.inf))
                yd.append(jnp.dot((cb * decay * dt_row).astype(BF16), xpair_b, preferred_element_type=F32))
                to_end = jnp.exp2(cum_row[:, tot_row:tot_row + 1] - cum_row)
                sd.append(jnp.dot((bg_t * (dt_row * to_end)).astype(BF16), xpair_b, preferred_element_type=F32))
            st = st_ref[dr, :, c0:c0 + LANES]
            y = (jnp.where(lo, yd[0], yd[1])
                 + jnp.dot(cg, st.astype(BF16), preferred_element_type=F32) * jnp.exp2(cum_x))
            if dr == 0:
                y = y + xpair * dsk_ref[:, c0:c0 + LANES]
            y_put(c0, y)
            tot = cum_x[tot_row:tot_row + 1, :]
            st_ref[dr, :, c0:c0 + LANES] = st * jnp.exp2(tot) + jnp.where(lo, sd[0], sd[1])


def _ssd_seq_kernel(x_ref, nw_ref, mod_ref, w_ref, wdt_ref, dtb_ref, wc_ref, bc_ref, alog_ref, dsk_ref, gnw_ref, wo_ref,
                    o_ref, so_ref, xpad_ref, xbc_ref, sz_ref, y_ref, st_ref):
    Q = SSD_CHUNK
    nc = x_ref.shape[0] // Q
    x = x_ref[...]
    hb = _modnorm(x, nw_ref[...], mod_ref[...]).astype(BF16)
    dt = _ssd_project(hb, None, w_ref, wdt_ref, dtb_ref, wc_ref, bc_ref, xpad_ref, sz_ref, xbc_ref)
    st_ref[...] = jnp.zeros(st_ref.shape, F32)
    A = -jnp.exp(alog_ref[...])
    for dr in range(2):
        for ck in (range(nc) if dr == 0 else reversed(range(nc))):
            rows = slice(ck * Q, (ck + 1) * Q)

            def x_at(c0, rows=rows):
                return xbc_ref[rows, c0:c0 + LANES]

            def y_put(c0, y, rows=rows, dr=dr):
                if dr == 0:
                    y_ref[rows, c0:c0 + LANES] = y
                else:
                    y_ref[rows, c0:c0 + LANES] += y

            _ssd_chunk(dr, x_at, dt[rows], A, st_ref, y_put, dsk_ref)
    _ssd_emit_state(st_ref, so_ref)
    g = _rms(y_ref[...] * sz_ref[...], gnw_ref[...]).astype(BF16)
    o_ref[...] = x + mod_ref[:, 2 * D:3 * D] * jnp.dot(g, wo_ref[...], preferred_element_type=F32)


def _ssd_seq(x, seq_len, layer, row_fn, mod, nw, w_in, w_dt, dt_bias, w_conv, b_conv, a_log, d_skip, norm_w, w_out):
    M = x.shape[0]
    n_seq = M // seq_len
    HL = SSD_HALO
    row = lambda i: (i, 0)
    st_blk = pl.BlockSpec((None, 2, SSD_INNER, SSD_STATE), lambda i: (i, 0, 0, 0))
    return pl.pallas_call(
        _ssd_seq_kernel,
        grid=(n_seq,),
        in_specs=[pl.BlockSpec((seq_len, D), row), _const_spec((1, D)), _mod_spec(layer, lambda i: row_fn(i * seq_len)),
                  _const_spec((D, SSD_IN_COLS)), _const_spec((D, LANES)), _const_spec((1, LANES)),
                  _const_spec((SSD_CONV + 1, SSD_XBC)), _const_spec((1, SSD_XBC)),
                  _const_spec((1, LANES)), _const_spec((1, SSD_INNER)), _const_spec((1, SSD_INNER)),
                  _const_spec((SSD_INNER, D))],
        out_specs=[pl.BlockSpec((seq_len, D), row), st_blk],
        out_shape=[jax.ShapeDtypeStruct((M, D), F32), jax.ShapeDtypeStruct((n_seq, 2, SSD_INNER, SSD_STATE), F32)],
        scratch_shapes=[pltpu.VMEM((SSD_XBC // LANES, seq_len + 2 * HL, LANES), F32),
                        pltpu.VMEM((seq_len, SSD_XBC), F32), pltpu.VMEM((seq_len, SSD_INNER), F32),
                        pltpu.VMEM((seq_len, SSD_INNER), F32), pltpu.VMEM((2, SSD_STATE, SSD_INNER), F32)],
        compiler_params=_cparams(1),
        name="ssd_seq",
    )(x, nw, mod, w_in, w_dt, dt_bias, w_conv, b_conv, a_log, d_skip, norm_w, w_out)


def _ssd_core(xbc, dtp, a_log, d_skip, h0, n_seq, seq_len, emit_state):
    Q = SSD_CHUNK
    nc = seq_len // Q
    M = n_seq * seq_len

    def chunk(dr):
        return (lambda b, c: b * nc + c) if dr == 0 else (lambda b, c: b * nc + nc - 1 - c)

    in_specs, args = [], []
    for dr in range(2):
        ch = chunk(dr)
        in_specs += [pl.BlockSpec((Q, SSD_XBC), lambda b, c, ch=ch: (ch(b, c), 0)),
                     pl.BlockSpec((Q, LANES), lambda b, c, ch=ch: (ch(b, c), 0))]
        args += [xbc, dtp]
    in_specs += [_const_spec((1, LANES)), _const_spec((1, SSD_INNER))]
    args += [a_log, d_skip]
    st_blk = pl.BlockSpec((None, 2, SSD_INNER, SSD_STATE), lambda b, c: (b, 0, 0, 0))
    has_h0 = h0 is not None
    if has_h0:
        in_specs.append(st_blk)
        args.append(h0)
    out_specs = [pl.BlockSpec((Q, SSD_INNER), lambda b, c: (chunk(0)(b, c), 0)),
                 pl.BlockSpec((Q, SSD_INNER), lambda b, c: (chunk(1)(b, c), 0))]
    out_shape = [jax.ShapeDtypeStruct((M, SSD_INNER), F32)] * 2
    if emit_state:
        out_specs.append(st_blk)
        out_shape.append(jax.ShapeDtypeStruct((n_seq, 2, SSD_INNER, SSD_STATE), F32))
    return pl.pallas_call(
        functools.partial(_ssd_core_kernel, nc, has_h0, emit_state),
        grid=(n_seq, nc),
        in_specs=in_specs,
        out_specs=out_specs,
        out_shape=out_shape,
        scratch_shapes=[pltpu.VMEM((2, SSD_STATE, SSD_INNER), F32)],
        compiler_params=_cparams(2),
        name="ssd_core",
    )(*args)


def _ssd_out_kernel(yf_ref, yb_ref, sz_ref, x_ref, mod_ref, nw_ref, w_ref, o_ref):
    g = _rms((yf_ref[...] + yb_ref[...]) * sz_ref[...], nw_ref[...]).astype(BF16)
    o_ref[...] = x_ref[...] + mod_ref[:, 2 * D:3 * D] * jnp.dot(g, w_ref[...], preferred_element_type=F32)


def _ssd_out(yf, yb, sz, x, layer, row_fn, mod, norm_w, w_out):
    M = x.shape[0]
    tm = TOKEN_TILE
    row = lambda i: (i, 0)
    wide = pl.BlockSpec((tm, SSD_INNER), row)
    return pl.pallas_call(
        _ssd_out_kernel,
        grid=(M // tm,),
        in_specs=[wide, wide, wide, pl.BlockSpec((tm, D), row), _mod_spec(layer, lambda i: row_fn(i * tm)),
                  _const_spec((1, SSD_INNER)), _const_spec((SSD_INNER, D))],
        out_specs=pl.BlockSpec((tm, D), row),
        out_shape=jax.ShapeDtypeStruct((M, D), F32),
        compiler_params=_cparams(1),
        name="ssd_out",
    )(yf, yb, sz, x, mod, norm_w, w_out)


def _ssd_layer(xc, xl, layer, mod, nw, p, n_lat):
    j = layer // 3
    w_in = p["ssd_w_in"][j][:, :SSD_IN_COLS].astype(BF16)
    w_dt = jnp.pad(p["ssd_w_in"][j][:, SSD_IN_COLS:], ((0, 0), (0, LANES - 2 * SSD_HEADS))).astype(BF16)
    dt_bias = jnp.pad(p["ssd_dt_bias"][j].reshape(1, -1), ((0, 0), (0, LANES - 2 * SSD_HEADS)))
    a_log = jnp.pad(p["ssd_a_log"][j].reshape(1, -1), ((0, 0), (0, LANES - 2 * SSD_HEADS)))
    w_conv = jnp.pad(p["ssd_w_conv"][j], ((0, 1), (0, 0)))
    b_conv = p["ssd_b_conv"][j][None]
    d_skip = jnp.repeat(p["ssd_d"][j], SSD_HEAD_DIM)[None]
    norm_w = p["ssd_norm_w"][j][None]
    w_out = p["ssd_w_out"][j].astype(BF16)
    row_ctx = lambda tok: 0
    row_lat = lambda tok: 1 + tok // DEC_SEQ
    oc, s_new = _ssd_seq(xc, SEQ, layer, row_ctx, mod, nw, w_in, w_dt, dt_bias, w_conv, b_conv, a_log, d_skip, norm_w, w_out)
    szl, xbc_l, dt_l = _ssd_in(xl, DEC_SEQ, layer, row_lat, mod, nw, w_in, w_dt, dt_bias, w_conv, b_conv)
    h0 = p["state_ssm"][:n_lat, j].reshape(n_lat, 2, SSD_INNER, SSD_STATE)
    yfl, ybl = _ssd_core(xbc_l, dt_l, a_log, d_skip, h0, n_lat, DEC_SEQ, False)
    ol = _ssd_out(yfl, ybl, szl, xl, layer, row_lat, mod, norm_w, w_out)
    return oc, ol, s_new


def kernel(x_prompt, x_sample, cache_k, cache_v, state_ssm, c, c_ctx, ada_w, ada_b, norm_w, final_norm_w, conv_w_in, conv_b_in, conv_w_dw, conv_b_dw, conv_ln_w, conv_ln_b, conv_w_out, conv_b_out, na_w_in, na_rpb, na_w_out, ssd_w_in, ssd_w_conv, ssd_b_conv, ssd_a_log, ssd_dt_bias, ssd_d, ssd_norm_w, ssd_w_out):
    p = dict(cache_k=cache_k, cache_v=cache_v, state_ssm=state_ssm,
             conv_w_in=conv_w_in, conv_b_in=conv_b_in, conv_w_dw=conv_w_dw, conv_b_dw=conv_b_dw,
             conv_ln_w=conv_ln_w, conv_ln_b=conv_ln_b, conv_w_out=conv_w_out, conv_b_out=conv_b_out,
             na_w_in=na_w_in, na_rpb=na_rpb, na_w_out=na_w_out,
             ssd_w_in=ssd_w_in, ssd_w_conv=ssd_w_conv, ssd_b_conv=ssd_b_conv, ssd_a_log=ssd_a_log,
             ssd_dt_bias=ssd_dt_bias, ssd_d=ssd_d, ssd_norm_w=ssd_norm_w, ssd_w_out=ssd_w_out)
    n_ctx, n_lat = x_prompt.shape[0], x_sample.shape[0]
    assert x_prompt.shape[1:] == (SEQ, D) and x_sample.shape[1:] == (DEC_SEQ, D)
    assert (DEPTH - 1) % 3 == 0, "the final RMSNorm is fused into a trailing Conformer layer"
    xc = x_prompt.reshape(n_ctx * SEQ, D)
    xl = x_sample.reshape(n_lat * DEC_SEQ, D)
    cond8 = jnp.concatenate([c_ctx[None], c, jnp.zeros((8 - 1 - n_lat, D), F32)], axis=0)
    mod = _modulation(cond8, ada_w, ada_b).reshape(DEPTH, 8, 1, 3 * D)
    row_ctx = lambda tok: 0
    row_lat = lambda tok: 1 + tok // DEC_SEQ
    new_k, new_v, new_s = [], [], []
    for i in range(DEPTH):
        kind, j = i % 3, i // 3
        nw = norm_w[i][None]
        if kind == 0:
            cp = _prep_conv(p, j)
            fw = final_norm_w[None] if i == DEPTH - 1 else None
            xc = _conv_layer(xc, SEQ, i, row_ctx, mod, nw, *cp, fw)
            xl = _conv_layer(xl, DEC_SEQ, i, row_lat, mod, nw, *cp, fw)
        elif kind == 1:
            xc, xl, k_c, v_c = _na_layer(xc, xl, i, mod, nw, p, n_lat)
            new_k.append(k_c.reshape(n_ctx, SEQ, NA_HEADS, NA_HEAD_DIM))
            new_v.append(v_c.reshape(n_ctx, SEQ, NA_HEADS, NA_HEAD_DIM))
        else:
            xc, xl, s_c = _ssd_layer(xc, xl, i, mod, nw, p, n_lat)
            new_s.append(s_c.reshape(n_ctx, 2, SSD_HEADS, SSD_HEAD_DIM, SSD_STATE))
    return (xc.reshape(n_ctx, SEQ, D), xl.reshape(n_lat, DEC_SEQ, D),
            jnp.stack(new_k, axis=1), jnp.stack(new_v, axis=1), jnp.stack(new_s, axis=1))
```

```python
import functools

import jax
import jax.numpy as jnp
from jax import lax
from jax.experimental import pallas as pl
from jax.experimental.pallas import tpu as pltpu

F32 = jnp.float32
BF16 = jnp.bfloat16

D = 1024
DEPTH = 4
SEQ = 256
DEC_SEQ = 2048
GRID_W = 64
RMS_EPS = 1e-6
LN_EPS = 1e-5
LOG2E = 1.4426950408889634
CONF_K = 31
NA_HEADS = 16
NA_HEAD_DIM = 64
NA_KH = 8
NA_KW = 16
NEG_INF = -1e30
SSD_INNER = 2048
SSD_HEADS = 32
SSD_HEAD_DIM = 64
SSD_STATE = 128
SSD_GROUPS = 4
SSD_GN = SSD_GROUPS * SSD_STATE
SSD_CONV = 7
SSD_CHUNK = 128
SSD_XBC = SSD_INNER + 2 * SSD_GN

LANES = 128
SUBLANES = 8
TOKEN_TILE = 256
CONV_HALO = 16
SSD_HALO = 8
VMEM_LIMIT = 56 * 1024 * 1024


def _cparams(n_axes):
    return pltpu.CompilerParams(dimension_semantics=("arbitrary",) * n_axes,
                                vmem_limit_bytes=VMEM_LIMIT)


def _const_spec(shape):
    nd = len(shape)
    return pl.BlockSpec(shape, lambda *_: (0,) * nd, pipeline_mode=pl.Buffered(1))


def _sigmoid(x):
    return 1.0 / (1.0 + jnp.exp(-x))


def _silu(x):
    return x * _sigmoid(x)


def _rms(x, w):
    ms = jnp.mean(x * x, axis=-1, keepdims=True)
    return x * lax.rsqrt(ms + RMS_EPS) * w


def _modnorm(x, nw, mod):
    return _rms(x, nw) * (1.0 + mod[:, D:2 * D]) + mod[:, :D]


def _mod_spec(layer, row_fn):
    return pl.BlockSpec((None, None, 1, 3 * D), lambda i, *_: (layer, row_fn(i), 0, 0))


def _mod_kernel(cond_ref, w_ref, b_ref, o_ref):
    s = _silu(cond_ref[...]).astype(BF16)
    o_ref[...] = jnp.dot(s, w_ref[...].astype(BF16), preferred_element_type=F32) + b_ref[...]


def _modulation(cond8, ada_w, ada_b):
    tn = 1024
    return pl.pallas_call(
        _mod_kernel,
        grid=(DEPTH, 3 * D // tn),
        in_specs=[pl.BlockSpec((8, D), lambda i, j: (0, 0)),
                  pl.BlockSpec((None, D, tn), lambda i, j: (i, 0, j)),
                  pl.BlockSpec((None, 1, tn), lambda i, j: (i, 0, j))],
        out_specs=pl.BlockSpec((None, 8, tn), lambda i, j: (i, 0, j)),
        out_shape=jax.ShapeDtypeStruct((DEPTH, 8, 3 * D), F32),
        compiler_params=_cparams(2),
        name="adaln_mod",
    )(cond8, ada_w, ada_b.reshape(DEPTH, 1, 3 * D))


CONV_CHUNK = 256


def _conv_kernel(tiles_per_seq, final, *refs):
    refs = list(refs)
    x_ref = refs.pop(0)
    if tiles_per_seq > 1:
        xp_ref = refs.pop(0)
        xn_ref = refs.pop(0)
    nw_ref, mod_ref, wi_ref, bi_ref, wdw_ref, bdw_ref, lnw_ref, lnb_ref, wo_ref, bo_ref = refs[:10]
    refs = refs[10:]
    if final:
        fw_ref = refs.pop(0)
    o_ref, ubuf, cbuf = refs
    T = x_ref.shape[0]
    H = CONV_HALO
    RB = 128

    hb = _modnorm(x_ref[...], nw_ref[...], mod_ref[...]).astype(BF16)
    if tiles_per_seq > 1:
        t = pl.program_id(0) % tiles_per_seq
        hh = _modnorm(jnp.concatenate([xp_ref[...], xn_ref[...]], axis=0), nw_ref[...], mod_ref[...]).astype(BF16)

    def glu(h, cols):
        v = jnp.dot(h, wi_ref[:, cols], preferred_element_type=F32) + bi_ref[:, cols]
        gcols = slice(D + cols.start, D + cols.stop)
        g = jnp.dot(h, wi_ref[:, gcols], preferred_element_type=F32) + bi_ref[:, gcols]
        return v * _sigmoid(g)

    for ch in range(D // CONV_CHUNK):
        cols = slice(ch * CONV_CHUNK, (ch + 1) * CONV_CHUNK)
        u_main = glu(hb, cols)
        if tiles_per_seq > 1:
            uh = glu(hh, cols)
            u_prev = jnp.where(t > 0, uh[0:H], 0.0)
            u_next = jnp.where(t < tiles_per_seq - 1, uh[H:2 * H], 0.0)
        else:
            u_prev = u_next = jnp.zeros((H, CONV_CHUNK), F32)
        for cb in range(CONV_CHUNK // LANES):
            blk = ch * (CONV_CHUNK // LANES) + cb
            lc = slice(blk * LANES, (blk + 1) * LANES)
            cc = slice(cb * LANES, (cb + 1) * LANES)
            ubuf[blk, 0:H, :] = u_prev[:, cc]
            ubuf[blk, H:H + T, :] = u_main[:, cc]
            ubuf[blk, H + T:H + T + H, :] = u_next[:, cc]
            for rb in range(T // RB):
                acc = jnp.zeros((RB, LANES), F32)
                for k in range(CONF_K):
                    r0 = rb * RB + H - CONF_K // 2 + k
                    acc = acc + ubuf[blk, r0:r0 + RB, :] * wdw_ref[k:k + 1, lc]
                cbuf[rb * RB:(rb + 1) * RB, lc] = acc + bdw_ref[:, lc]

    sz = _silu(jnp.dot(hb, wi_ref[:, 2 * D:3 * D], preferred_element_type=F32) + bi_ref[:, 2 * D:3 * D])
    c = cbuf[...]
    mu = jnp.mean(c, axis=-1, keepdims=True)
    xc = c - mu
    var = jnp.mean(xc * xc, axis=-1, keepdims=True)
    ln = xc * lax.rsqrt(var + LN_EPS) * lnw_ref[...] + lnb_ref[...]
    g = (_silu(ln) * sz).astype(BF16)
    y = jnp.dot(g, wo_ref[...], preferred_element_type=F32) + bo_ref[...]
    xn = x_ref[...] + mod_ref[:, 2 * D:3 * D] * y
    if final:
        xn = _rms(xn, fw_ref[...])
    o_ref[...] = xn


def _conv_layer(x, seq_len, layer, row_fn, mod, nw, w_in, b_in, w_dw, b_dw, ln_w, ln_b, w_out, b_out, final_w):
    M = x.shape[0]
    T = TOKEN_TILE
    tps = seq_len // T
    H = CONV_HALO
    n_halo_blocks = M // H
    row = lambda i: (i, 0)
    in_specs = [pl.BlockSpec((T, D), row)]
    args = [x]
    if tps > 1:
        in_specs += [pl.BlockSpec((H, D), lambda i: (jnp.maximum(i * (T // H) - 1, 0), 0)),
                     pl.BlockSpec((H, D), lambda i: (jnp.minimum((i + 1) * (T // H), n_halo_blocks - 1), 0))]
        args += [x, x]
    in_specs += [_const_spec((1, D)), _mod_spec(layer, lambda i: row_fn(i * T)),
                 _const_spec((D, 3 * D)), _const_spec((1, 3 * D)),
                 _const_spec((CONF_K + 1, D)), _const_spec((1, D)), _const_spec((1, D)), _const_spec((1, D)),
                 _const_spec((D, D)), _const_spec((1, D))]
    args += [nw, mod, w_in, b_in, w_dw, b_dw, ln_w, ln_b, w_out, b_out]
    final = final_w is not None
    if final:
        in_specs.append(_const_spec((1, D)))
        args.append(final_w)
    return pl.pallas_call(
        functools.partial(_conv_kernel, tps, final),
        grid=(M // T,),
        in_specs=in_specs,
        out_specs=pl.BlockSpec((T, D), row),
        out_shape=jax.ShapeDtypeStruct((M, D), F32),
        scratch_shapes=[pltpu.VMEM((D // LANES, T + 2 * H, LANES), F32), pltpu.VMEM((T, D), F32)],
        compiler_params=_cparams(1),
        name="conv_layer",
    )(*args)


def _prep_conv(p, j):
    w_dw = jnp.pad(p["conv_w_dw"][j], ((0, 1), (0, 0)))
    return (p["conv_w_in"][j].astype(BF16), p["conv_b_in"][j][None], w_dw, p["conv_b_dw"][j][None],
            p["conv_ln_w"][j][None], p["conv_ln_b"][j][None], p["conv_w_out"][j].astype(BF16), p["conv_b_out"][j][None])


def _out_proj_kernel(g_ref, x_ref, mod_ref, w_ref, o_ref):
    y = jnp.dot(g_ref[...].astype(BF16), w_ref[...], preferred_element_type=F32)
    o_ref[...] = x_ref[...] + mod_ref[:, 2 * D:3 * D] * y


def _out_proj(g, x, layer, row_fn, mod, w_out):
    M, K = g.shape
    tm = TOKEN_TILE
    row = lambda i: (i, 0)
    return pl.pallas_call(
        _out_proj_kernel,
        grid=(M // tm,),
        in_specs=[pl.BlockSpec((tm, K), row), pl.BlockSpec((tm, D), row), _mod_spec(layer, lambda i: row_fn(i * tm)),
                  _const_spec((K, D))],
        out_specs=pl.BlockSpec((tm, D), row),
        out_shape=jax.ShapeDtypeStruct((M, D), F32),
        compiler_params=_cparams(1),
        name="out_proj",
    )(g, x, mod, w_out)


NA_SCALE = NA_HEAD_DIM ** -0.5
NA_ROWS = 32
NA_WIN = NA_KH * GRID_W
NT_DIMS = (((1,), (1,)), ((), ()))


def _na_in_kernel(x_ref, nw_ref, mod_ref, w_ref, q_ref, k_ref, v_ref, sz_ref):
    hb = _modnorm(x_ref[...], nw_ref[...], mod_ref[...]).astype(BF16)
    q_ref[...] = jnp.dot(hb, w_ref[:, 0:D], preferred_element_type=F32).astype(q_ref.dtype)
    k_ref[...] = jnp.dot(hb, w_ref[:, D:2 * D], preferred_element_type=F32).astype(k_ref.dtype)
    v_ref[...] = jnp.dot(hb, w_ref[:, 2 * D:3 * D], preferred_element_type=F32).astype(v_ref.dtype)
    sz_ref[...] = _silu(jnp.dot(hb, w_ref[:, 3 * D:4 * D], preferred_element_type=F32))


def _na_in(x, layer, row_fn, mod, nw, w_in, kv_dtype):
    M = x.shape[0]
    tm = TOKEN_TILE
    row = lambda i: (i, 0)
    blk = pl.BlockSpec((tm, D), row)
    return pl.pallas_call(
        _na_in_kernel,
        grid=(M // tm,),
        in_specs=[blk, _const_spec((1, D)), _mod_spec(layer, lambda i: row_fn(i * tm)), _const_spec((D, 4 * D))],
        out_specs=[blk, blk, blk, blk],
        out_shape=[jax.ShapeDtypeStruct((M, D), BF16), jax.ShapeDtypeStruct((M, D), kv_dtype),
                   jax.ShapeDtypeStruct((M, D), kv_dtype), jax.ShapeDtypeStruct((M, D), F32)],
        compiler_params=_cparams(1),
        name="na_in",
    )(x, nw, mod, w_in)


def _head_masks():
    lane = lax.broadcasted_iota(jnp.int32, (1, LANES), 1)
    return lane < NA_HEAD_DIM


def _na_seq_kernel(x_ref, nw_ref, mod_ref, wi_ref, wo_ref, o_ref, k_ref, v_ref, g_ref):
    n = x_ref.shape[0]
    lo = _head_masks()
    x = x_ref[...]
    hb = _modnorm(x, nw_ref[...], mod_ref[...]).astype(BF16)
    hpc = CONV_CHUNK // NA_HEAD_DIM
    for ch in range(D // CONV_CHUNK):
        cols = slice(ch * CONV_CHUNK, (ch + 1) * CONV_CHUNK)
        q = jnp.dot(hb, wi_ref[:, cols], preferred_element_type=F32).astype(BF16)
        k = jnp.dot(hb, wi_ref[:, D + cols.start:D + cols.stop], preferred_element_type=F32)
        v = jnp.dot(hb, wi_ref[:, 2 * D + cols.start:2 * D + cols.stop], preferred_element_type=F32)
        sz = _silu(jnp.dot(hb, wi_ref[:, 3 * D + cols.start:3 * D + cols.stop], preferred_element_type=F32))
        k_ref[:, ch * hpc:(ch + 1) * hpc, :] = k.reshape(n, hpc, NA_HEAD_DIM)
        v_ref[:, ch * hpc:(ch + 1) * hpc, :] = v.reshape(n, hpc, NA_HEAD_DIM)
        kb = k.astype(BF16)
        vb = v.astype(BF16)
        for pp in range(CONV_CHUNK // LANES):
            pc = slice(pp * LANES, (pp + 1) * LANES)
            qp = q[:, pc]
            zero = jnp.zeros_like(qp)
            qs = jnp.concatenate([jnp.where(lo, qp, zero), jnp.where(lo, zero, qp)], axis=0)
            s = lax.dot_general(qs, kb[:, pc], NT_DIMS, preferred_element_type=F32) * NA_SCALE
            e = jnp.exp(s - jnp.max(s, axis=-1, keepdims=True))
            l = jnp.sum(e, axis=-1, keepdims=True)
            o = jnp.dot(e.astype(BF16), vb[:, pc], preferred_element_type=F32) / l
            g = jnp.where(lo, o[:n], o[n:]) * sz[:, pc]
            g_ref[:, cols.start + pp * LANES:cols.start + (pp + 1) * LANES] = g.astype(BF16)
    y = jnp.dot(g_ref[...], wo_ref[...], preferred_element_type=F32)
    o_ref[...] = x + mod_ref[:, 2 * D:3 * D] * y


def _na_seq(x, seq_len, layer, row_fn, mod, nw, w_in, w_out):
    M = x.shape[0]
    row = lambda i: (i, 0)
    kv_blk = pl.BlockSpec((seq_len, NA_HEADS, NA_HEAD_DIM), lambda i: (i, 0, 0))
    kv_shape = jax.ShapeDtypeStruct((M, NA_HEADS, NA_HEAD_DIM), F32)
    return pl.pallas_call(
        _na_seq_kernel,
        grid=(M // seq_len,),
        in_specs=[pl.BlockSpec((seq_len, D), row), _const_spec((1, D)), _mod_spec(layer, lambda i: row_fn(i * seq_len)),
                  _const_spec((D, 4 * D)), _const_spec((D, D))],
        out_specs=[pl.BlockSpec((seq_len, D), row), kv_blk, kv_blk],
        out_shape=[jax.ShapeDtypeStruct((M, D), F32), kv_shape, kv_shape],
        scratch_shapes=[pltpu.VMEM((seq_len, D), BF16)],
        compiler_params=_cparams(1),
        name="na_seq",
    )(x, nw, mod, w_in, w_out)


def _rpb_kernel(rpb_ref, o_ref):
    h = pl.program_id(0)
    c = lax.broadcasted_iota(jnp.int32, (GRID_W, LANES), 0)
    lane = lax.broadcasted_iota(jnp.int32, (GRID_W, LANES), 1)
    kc = lane & (GRID_W - 1)
    upper = lax.broadcasted_iota(jnp.int32, (1, LANES), 1) >= GRID_W
    cs = jnp.clip(c - NA_KW // 2, 0, GRID_W - NA_KW)
    valid = (kc >= cs) & (kc < cs + NA_KW)
    n_pair = 2 * NA_KH - 2

    def toeplitz(dri, lane0):
        row = rpb_ref[pl.ds(h * (2 * NA_KH - 1) + dri, 1), :]
        rows = jnp.broadcast_to(row, (GRID_W, LANES))
        return pltpu.roll(rows, (LANES - (NA_KW - 1) + lane0) % LANES, 1, stride=1, stride_axis=0)

    for dri in range(n_pair):
        tile = jnp.where(upper, toeplitz(dri + 1, GRID_W), toeplitz(dri, 0))
        tile = jnp.where(valid, tile, NEG_INF)
        for idx in range(NA_KH):
            jj2 = dri - (NA_KH - 1) + idx
            if jj2 % 2 == 0 and 0 <= jj2 // 2 < NA_KH // 2:
                jj = jj2 // 2
                o_ref[idx, :, jj * LANES:(jj + 1) * LANES] = tile


def _rpb_table(rpb):
    n_rows = NA_HEADS * (2 * NA_KH - 1)
    rows = jnp.pad(rpb.reshape(n_rows, 2 * NA_KW - 1), ((0, 0), (0, LANES - (2 * NA_KW - 1))))
    return pl.pallas_call(
        _rpb_kernel,
        grid=(NA_HEADS,),
        in_specs=[_const_spec((n_rows, LANES))],
        out_specs=pl.BlockSpec((NA_KH, None, GRID_W, NA_WIN), lambda h: (0, h, 0, 0)),
        out_shape=jax.ShapeDtypeStruct((NA_KH, NA_HEADS, GRID_W, NA_WIN), F32),
        compiler_params=_cparams(1),
        name="na_rpb_table",
    )(rows)


def _na_row_start(r):
    return jnp.clip(r - NA_KH // 2, 0, NA_ROWS - NA_KH)


def _na_lat_kernel(q_ref, k_ref, v_ref, ck_ref, cv_ref, bias_ref, sz_ref, o_ref):
    lo = _head_masks()
    k0 = pl.multiple_of(_na_row_start(pl.program_id(1)) * GRID_W, GRID_W)
    for p in range(NA_HEADS // 2):
        cols = slice(p * LANES, (p + 1) * LANES)
        qp = q_ref[:, cols]
        kw = k_ref[pl.ds(k0, NA_WIN), cols]
        vw = v_ref[pl.ds(k0, NA_WIN), cols]
        ck = ck_ref[:, cols]
        cv = cv_ref[:, cols]
        zero = jnp.zeros_like(qp)
        qs = jnp.concatenate([jnp.where(lo, qp, zero), jnp.where(lo, zero, qp)], axis=0)
        bias = bias_ref[2 * p:2 * p + 2].reshape(2 * GRID_W, NA_WIN)
        s_loc = lax.dot_general(qs, kw, NT_DIMS, preferred_element_type=F32) * NA_SCALE + bias
        s_ctx = lax.dot_general(qs, ck, NT_DIMS, preferred_element_type=F32) * NA_SCALE
        m = jnp.maximum(jnp.max(s_loc, axis=-1, keepdims=True), jnp.max(s_ctx, axis=-1, keepdims=True))
        e_loc = jnp.exp(s_loc - m)
        e_ctx = jnp.exp(s_ctx - m)
        l = jnp.sum(e_loc, axis=-1, keepdims=True) + jnp.sum(e_ctx, axis=-1, keepdims=True)
        o = (jnp.dot(e_loc.astype(BF16), vw, preferred_element_type=F32)
             + jnp.dot(e_ctx.astype(BF16), cv, preferred_element_type=F32)) / l
        o_ref[:, cols] = jnp.where(lo, o[:GRID_W], o[GRID_W:]) * sz_ref[:, cols]


def _na_lat(q, k, v, ck, cv, bias, sz, n_lat):
    T = NA_ROWS * GRID_W
    past = ck.shape[1]
    qblk = pl.BlockSpec((GRID_W, D), lambda b, r: (b * NA_ROWS + r, 0))
    kvblk = pl.BlockSpec((T, D), lambda b, r: (b, 0))
    cblk = pl.BlockSpec((None, past, D), lambda b, r: (b, 0, 0))
    bblk = pl.BlockSpec((None, NA_HEADS, GRID_W, NA_WIN), lambda b, r: (r - _na_row_start(r), 0, 0, 0))
    return pl.pallas_call(
        _na_lat_kernel,
        grid=(n_lat, NA_ROWS),
        in_specs=[qblk, kvblk, kvblk, cblk, cblk, bblk, qblk],
        out_specs=qblk,
        out_shape=jax.ShapeDtypeStruct((n_lat * T, D), F32),
        compiler_params=_cparams(2),
        name="na_lat_attn",
    )(q, k, v, ck, cv, bias, sz)


def _na_layer(xc, xl, layer, mod, nw, p, n_lat):
    j = layer // 3
    w_in = p["na_w_in"][j].astype(BF16)
    w_out = p["na_w_out"][j].astype(BF16)
    row_ctx = lambda tok: 0
    row_lat = lambda tok: 1 + tok // (NA_ROWS * GRID_W)
    oc, kc, vc = _na_seq(xc, SEQ, layer, row_ctx, mod, nw, w_in, w_out)
    ql, kl, vl, szl = _na_in(xl, layer, row_lat, mod, nw, w_in, BF16)
    past = p["cache_k"].shape[2]
    ck = p["cache_k"][:n_lat, j].reshape(n_lat, past, D).astype(BF16)
    cv = p["cache_v"][:n_lat, j].reshape(n_lat, past, D).astype(BF16)
    gl = _na_lat(ql, kl, vl, ck, cv, _rpb_table(p["na_rpb"][j]), szl, n_lat)
    return oc, _out_proj(gl, xl, layer, row_lat, mod, w_out), kc, vc


SSD_IN_COLS = SSD_INNER + SSD_XBC


def _softplus(x):
    return jnp.maximum(x, 0.0) + jnp.log(1.0 + jnp.exp(-jnp.abs(x)))


def _ssd_in_kernel(tiles_per_seq, *refs):
    refs = list(refs)
    x_ref = refs.pop(0)
    if tiles_per_seq > 1:
        xp_ref = refs.pop(0)
        xn_ref = refs.pop(0)
    nw_ref, mod_ref, wz_ref, wx_ref, wdt_ref, dtb_ref, wc_ref, bc_ref, sz_ref, xbc_ref, dt_ref, xpad_ref = refs
    hb = _modnorm(x_ref[...], nw_ref[...], mod_ref[...]).astype(BF16)
    halo = None
    if tiles_per_seq > 1:
        t = pl.program_id(0) % tiles_per_seq
        xh = jnp.concatenate([xp_ref[...], xn_ref[...]], axis=0)
        halo = (_modnorm(xh, nw_ref[...], mod_ref[...]).astype(BF16), t > 0, t < tiles_per_seq - 1)
    dt_ref[...] = _ssd_project(hb, halo, wz_ref, wx_ref, wdt_ref, dtb_ref, wc_ref, bc_ref, xpad_ref, sz_ref, xbc_ref)


def _ssd_project(hb, halo, wz_ref, wx_ref, wdt_ref, dtb_ref, wc_ref, bc_ref, xpad_ref, sz_ref, xbc_ref):
    tm = hb.shape[0]
    HL = SSD_HALO
    RB = 128
    for j in range(SSD_XBC // D):
        c0 = j * D
        raw = lax.dot_general(hb, wx_ref[c0:c0 + D, :], NT_DIMS, preferred_element_type=F32)
        if halo is not None:
            rawh = lax.dot_general(halo[0], wx_ref[c0:c0 + D, :], NT_DIMS, preferred_element_type=F32)
            raw_prev = jnp.where(halo[1], rawh[0:HL], 0.0)
            raw_next = jnp.where(halo[2], rawh[HL:2 * HL], 0.0)
        else:
            raw_prev = raw_next = jnp.zeros((HL, D), F32)
        for cb in range(D // LANES):
            blk = j * (D // LANES) + cb
            lc = slice(blk * LANES, (blk + 1) * LANES)
            cc = slice(cb * LANES, (cb + 1) * LANES)
            xpad_ref[blk, 0:HL, :] = raw_prev[:, cc]
            xpad_ref[blk, HL:HL + tm, :] = raw[:, cc]
            xpad_ref[blk, HL + tm:HL + tm + HL, :] = raw_next[:, cc]
            for rb in range(tm // RB):
                acc = jnp.zeros((RB, LANES), F32)
                for k in range(SSD_CONV):
                    r0 = rb * RB + HL - SSD_CONV // 2 + k
                    acc = acc + xpad_ref[blk, r0:r0 + RB, :] * wc_ref[k:k + 1, lc]
                xbc_ref[rb * RB:(rb + 1) * RB, lc] = _silu(acc + bc_ref[:, lc])
    for j in range(SSD_INNER // D):
        sz_ref[:, j * D:(j + 1) * D] = _silu(lax.dot_general(hb, wz_ref[j * D:(j + 1) * D, :], NT_DIMS,
                                                             preferred_element_type=F32))
    raw = lax.dot_general(hb, wdt_ref[...], NT_DIMS, preferred_element_type=F32) + dtb_ref[...]
    lane = lax.broadcasted_iota(jnp.int32, (1, LANES), 1)
    return jnp.where(lane < 2 * SSD_HEADS, _softplus(raw), 0.0)


def _ssd_in(x, seq_len, layer, row_fn, mod, nw, w_z, w_x, w_dt, dt_bias, w_conv, b_conv):
    M = x.shape[0]
    tm = TOKEN_TILE
    HL = SSD_HALO
    tps = seq_len // tm
    n_hblk = M // HL
    row = lambda i: (i, 0)
    in_specs = [pl.BlockSpec((tm, D), row)]
    args = [x]
    if tps > 1:
        in_specs += [pl.BlockSpec((HL, D), lambda i: (jnp.maximum(i * (tm // HL) - 1, 0), 0)),
                     pl.BlockSpec((HL, D), lambda i: (jnp.minimum((i + 1) * (tm // HL), n_hblk - 1), 0))]
        args += [x, x]
    in_specs += [_const_spec((1, D)), _mod_spec(layer, lambda i: row_fn(i * tm)),
                 _const_spec((SSD_INNER, D)), _const_spec((SSD_XBC, D)), _const_spec((LANES, D)), _const_spec((1, LANES)),
                 _const_spec((SSD_CONV + 1, SSD_XBC)), _const_spec((1, SSD_XBC))]
    args += [nw, mod, w_z, w_x, w_dt, dt_bias, w_conv, b_conv]
    return pl.pallas_call(
        functools.partial(_ssd_in_kernel, tps),
        grid=(M // tm,),
        in_specs=in_specs,
        out_specs=[pl.BlockSpec((tm, SSD_INNER), row), pl.BlockSpec((tm, SSD_XBC), row), pl.BlockSpec((tm, LANES), row)],
        out_shape=[jax.ShapeDtypeStruct((M, SSD_INNER), F32), jax.ShapeDtypeStruct((M, SSD_XBC), F32),
                   jax.ShapeDtypeStruct((M, LANES), F32)],
        scratch_shapes=[pltpu.VMEM((SSD_XBC // LANES, tm + 2 * HL, LANES), F32)],
        compiler_params=_cparams(1),
        name="ssd_in",
    )(*args)


def _cumsum_rows(a, reverse):
    n = a.shape[0]
    row = lax.broadcasted_iota(jnp.int32, a.shape, 0)
    k = 1
    while k < n:
        if reverse:
            a = a + jnp.where(row < n - k, pltpu.roll(a, n - k, 0), 0.0)
        else:
            a = a + jnp.where(row >= k, pltpu.roll(a, k, 0), 0.0)
        k *= 2
    return a


def _ssd_core_kernel(nc, has_h0, emit_state, *refs):
    refs = list(refs)
    dir_refs = [refs[0:2], refs[2:4]]
    alog_ref, dsk_ref = refs[4:6]
    refs = refs[6:]
    if has_h0:
        h0_ref = refs.pop(0)
    y_refs = [refs.pop(0), refs.pop(0)]
    if emit_state:
        so_ref = refs.pop(0)
    (st_ref,) = refs
    Q, N, P = SSD_CHUNK, SSD_STATE, SSD_HEAD_DIM
    c = pl.program_id(1)
    n_blk = SSD_INNER // LANES

    @pl.when(c == 0)
    def _init():
        if has_h0:
            for dr in range(2):
                for j in range(n_blk):
                    st_ref[dr, :, j * LANES:(j + 1) * LANES] = h0_ref[dr, j * LANES:(j + 1) * LANES, :].T
        else:
            st_ref[...] = jnp.zeros(st_ref.shape, F32)

    A = -jnp.exp(alog_ref[...])
    for dr in range(2):
        x_ref, dt_ref = dir_refs[dr]
        y_ref = y_refs[dr]

        def x_at(c0, x_ref=x_ref):
            return x_ref[:, c0:c0 + LANES]

        def y_put(c0, y, y_ref=y_ref):
            y_ref[:, c0:c0 + LANES] = y

        _ssd_chunk(dr, x_at, dt_ref[...], A, st_ref, y_put, dsk_ref)

    if emit_state:
        @pl.when(c == nc - 1)
        def _emit():
            _ssd_emit_state(st_ref, so_ref)


def _ssd_emit_state(st_ref, so_ref):
    for dr in range(2):
        for j in range(SSD_INNER // LANES):
            so_ref[dr, j * LANES:(j + 1) * LANES, :] = st_ref[dr, :, j * LANES:(j + 1) * LANES].T


def _ssd_chunk(dr, x_at, dt, A, st_ref, y_put, dsk_ref):
    Q, N, P = SSD_CHUNK, SSD_STATE, SSD_HEAD_DIM
    lo = _head_masks()
    ri = lax.broadcasted_iota(jnp.int32, (Q, Q), 0)
    ci = lax.broadcasted_iota(jnp.int32, (Q, Q), 1)
    cum = _cumsum_rows(dt * A, reverse=(dr == 1)) * LOG2E
    cum_t = cum.T
    dt_t = dt.T
    tot_row = Q - 1 if dr == 0 else 0
    mask = (ri >= ci) if dr == 0 else (ri <= ci)
    for g in range(SSD_GROUPS):
        bg_t = x_at(SSD_INNER + g * N).T
        cg = x_at(SSD_INNER + SSD_GN + g * N).astype(BF16)
        cb = jnp.dot(cg, bg_t.astype(BF16), preferred_element_type=F32)
        for pp in range(SSD_HEADS // SSD_GROUPS // 2):
            head = g * (SSD_HEADS // SSD_GROUPS) + 2 * pp
            c0 = head * P
            li = dr * SSD_HEADS + head
            xpair = x_at(c0)
            xpair_b = xpair.astype(BF16)
            cum_cols = [cum[:, li:li + 1], cum[:, li + 1:li + 2]]
            cum_x = jnp.where(lo, cum_cols[0], cum_cols[1])
            yd, sd = [], []
            for hh in range(2):
                dt_row = dt_t[li + hh:li + hh + 1, :]
                cum_row = cum_t[li + hh:li + hh + 1, :]
                decay = jnp.exp2(jnp.where(mask, cum_cols[hh] - cum_row, -jnp.inf))
                yd.append(jnp.dot((cb * decay * dt_row).astype(BF16), xpair_b, preferred_element_type=F32))
                to_end = jnp.exp2(cum_row[:, tot_row:tot_row + 1] - cum_row)
                sd.append(jnp.dot((bg_t * (dt_row * to_end)).astype(BF16), xpair_b, preferred_element_type=F32))
            st = st_ref[dr, :, c0:c0 + LANES]
            y = (jnp.where(lo, yd[0], yd[1])
                 + jnp.dot(cg, st.astype(BF16), preferred_element_type=F32) * jnp.exp2(cum_x))
            if dr == 0:
                y = y + xpair * dsk_ref[:, c0:c0 + LANES]
            y_put(c0, y)
            tot = cum_x[tot_row:tot_row + 1, :]
            st_ref[dr, :, c0:c0 + LANES] = st * jnp.exp2(tot) + jnp.where(lo, sd[0], sd[1])


def _ssd_seq_kernel(x_ref, nw_ref, mod_ref, wz_ref, wx_ref, wdt_ref, dtb_ref, wc_ref, bc_ref, alog_ref, dsk_ref, gnw_ref, wo_ref,
                    o_ref, so_ref, xpad_ref, xbc_ref, sz_ref, y_ref, st_ref):
    Q = SSD_CHUNK
    nc = x_ref.shape[0] // Q
    x = x_ref[...]
    hb = _modnorm(x, nw_ref[...], mod_ref[...]).astype(BF16)
    dt = _ssd_project(hb, None, wz_ref, wx_ref, wdt_ref, dtb_ref, wc_ref, bc_ref, xpad_ref, sz_ref, xbc_ref)
    st_ref[...] = jnp.zeros(st_ref.shape, F32)
    A = -jnp.exp(alog_ref[...])
    for dr in range(2):
        for ck in (range(nc) if dr == 0 else reversed(range(nc))):
            rows = slice(ck * Q, (ck + 1) * Q)

            def x_at(c0, rows=rows):
                return xbc_ref[rows, c0:c0 + LANES]

            def y_put(c0, y, rows=rows, dr=dr):
                if dr == 0:
                    y_ref[rows, c0:c0 + LANES] = y
                else:
                    y_ref[rows, c0:c0 + LANES] += y

            _ssd_chunk(dr, x_at, dt[rows], A, st_ref, y_put, dsk_ref)
    _ssd_emit_state(st_ref, so_ref)
    g = _rms(y_ref[...] * sz_ref[...], gnw_ref[...]).astype(BF16)
    o_ref[...] = x + mod_ref[:, 2 * D:3 * D] * jnp.dot(g, wo_ref[...], preferred_element_type=F32)


def _ssd_seq(x, seq_len, layer, row_fn, mod, nw, w_z, w_x, w_dt, dt_bias, w_conv, b_conv, a_log, d_skip, norm_w, w_out):
    M = x.shape[0]
    n_seq = M // seq_len
    HL = SSD_HALO
    row = lambda i: (i, 0)
    st_blk = pl.BlockSpec((None, 2, SSD_INNER, SSD_STATE), lambda i: (i, 0, 0, 0))
    return pl.pallas_call(
        _ssd_seq_kernel,
        grid=(n_seq,),
        in_specs=[pl.BlockSpec((seq_len, D), row), _const_spec((1, D)), _mod_spec(layer, lambda i: row_fn(i * seq_len)),
                  _const_spec((SSD_INNER, D)), _const_spec((SSD_XBC, D)), _const_spec((LANES, D)), _const_spec((1, LANES)),
                  _const_spec((SSD_CONV + 1, SSD_XBC)), _const_spec((1, SSD_XBC)),
                  _const_spec((1, LANES)), _const_spec((1, SSD_INNER)), _const_spec((1, SSD_INNER)),
                  _const_spec((SSD_INNER, D))],
        out_specs=[pl.BlockSpec((seq_len, D), row), st_blk],
        out_shape=[jax.ShapeDtypeStruct((M, D), F32), jax.ShapeDtypeStruct((n_seq, 2, SSD_INNER, SSD_STATE), F32)],
        scratch_shapes=[pltpu.VMEM((SSD_XBC // LANES, seq_len + 2 * HL, LANES), F32),
                        pltpu.VMEM((seq_len, SSD_XBC), F32), pltpu.VMEM((seq_len, SSD_INNER), F32),
                        pltpu.VMEM((seq_len, SSD_INNER), F32), pltpu.VMEM((2, SSD_STATE, SSD_INNER), F32)],
        compiler_params=_cparams(1),
        name="ssd_seq",
    )(x, nw, mod, w_z, w_x, w_dt, dt_bias, w_conv, b_conv, a_log, d_skip, norm_w, w_out)


def _ssd_core(xbc, dtp, a_log, d_skip, h0, n_seq, seq_len, emit_state):
    Q = SSD_CHUNK
    nc = seq_len // Q
    M = n_seq * seq_len

    def chunk(dr):
        return (lambda b, c: b * nc + c) if dr == 0 else (lambda b, c: b * nc + nc - 1 - c)

    in_specs, args = [], []
    for dr in range(2):
        ch = chunk(dr)
        in_specs += [pl.BlockSpec((Q, SSD_XBC), lambda b, c, ch=ch: (ch(b, c), 0)),
                     pl.BlockSpec((Q, LANES), lambda b, c, ch=ch: (ch(b, c), 0))]
        args += [xbc, dtp]
    in_specs += [_const_spec((1, LANES)), _const_spec((1, SSD_INNER))]
    args += [a_log, d_skip]
    st_blk = pl.BlockSpec((None, 2, SSD_INNER, SSD_STATE), lambda b, c: (b, 0, 0, 0))
    has_h0 = h0 is not None
    if has_h0:
        in_specs.append(st_blk)
        args.append(h0)
    out_specs = [pl.BlockSpec((Q, SSD_INNER), lambda b, c: (chunk(0)(b, c), 0)),
                 pl.BlockSpec((Q, SSD_INNER), lambda b, c: (chunk(1)(b, c), 0))]
    out_shape = [jax.ShapeDtypeStruct((M, SSD_INNER), F32)] * 2
    if emit_state:
        out_specs.append(st_blk)
        out_shape.append(jax.ShapeDtypeStruct((n_seq, 2, SSD_INNER, SSD_STATE), F32))
    return pl.pallas_call(
        functools.partial(_ssd_core_kernel, nc, has_h0, emit_state),
        grid=(n_seq, nc),
        in_specs=in_specs,
        out_specs=out_specs,
        out_shape=out_shape,
        scratch_shapes=[pltpu.VMEM((2, SSD_STATE, SSD_INNER), F32)],
        compiler_params=_cparams(2),
        name="ssd_core",
    )(*args)


def _ssd_out_kernel(yf_ref, yb_ref, sz_ref, x_ref, mod_ref, nw_ref, w_ref, o_ref):
    g = _rms((yf_ref[...] + yb_ref[...]) * sz_ref[...], nw_ref[...]).astype(BF16)
    o_ref[...] = x_ref[...] + mod_ref[:, 2 * D:3 * D] * jnp.dot(g, w_ref[...], preferred_element_type=F32)


def _ssd_out(yf, yb, sz, x, layer, row_fn, mod, norm_w, w_out):
    M = x.shape[0]
    tm = TOKEN_TILE
    row = lambda i: (i, 0)
    wide = pl.BlockSpec((tm, SSD_INNER), row)
    return pl.pallas_call(
        _ssd_out_kernel,
        grid=(M // tm,),
        in_specs=[wide, wide, wide, pl.BlockSpec((tm, D), row), _mod_spec(layer, lambda i: row_fn(i * tm)),
                  _const_spec((1, SSD_INNER)), _const_spec((SSD_INNER, D))],
        out_specs=pl.BlockSpec((tm, D), row),
        out_shape=jax.ShapeDtypeStruct((M, D), F32),
        compiler_params=_cparams(1),
        name="ssd_out",
    )(yf, yb, sz, x, mod, norm_w, w_out)


def _ssd_layer(xc, xl, layer, mod, nw, p, n_lat):
    j = layer // 3
    w_t = p["ssd_w_in"][j].T
    w_z = w_t[:SSD_INNER].astype(BF16)
    w_x = w_t[SSD_INNER:SSD_IN_COLS].astype(BF16)
    w_dt = jnp.pad(w_t[SSD_IN_COLS:], ((0, LANES - 2 * SSD_HEADS), (0, 0))).astype(BF16)
    dt_bias = jnp.pad(p["ssd_dt_bias"][j].reshape(1, -1), ((0, 0), (0, LANES - 2 * SSD_HEADS)))
    a_log = jnp.pad(p["ssd_a_log"][j].reshape(1, -1), ((0, 0), (0, LANES - 2 * SSD_HEADS)))
    w_conv = jnp.pad(p["ssd_w_conv"][j], ((0, 1), (0, 0)))
    b_conv = p["ssd_b_conv"][j][None]
    d_skip = jnp.repeat(p["ssd_d"][j], SSD_HEAD_DIM)[None]
    norm_w = p["ssd_norm_w"][j][None]
    w_out = p["ssd_w_out"][j].astype(BF16)
    row_ctx = lambda tok: 0
    row_lat = lambda tok: 1 + tok // DEC_SEQ
    oc, s_new = _ssd_seq(xc, SEQ, layer, row_ctx, mod, nw, w_z, w_x, w_dt, dt_bias, w_conv, b_conv, a_log, d_skip, norm_w, w_out)
    szl, xbc_l, dt_l = _ssd_in(xl, DEC_SEQ, layer, row_lat, mod, nw, w_z, w_x, w_dt, dt_bias, w_conv, b_conv)
    h0 = p["state_ssm"][:n_lat, j].reshape(n_lat, 2, SSD_INNER, SSD_STATE)
    yfl, ybl = _ssd_core(xbc_l, dt_l, a_log, d_skip, h0, n_lat, DEC_SEQ, False)
    ol = _ssd_out(yfl, ybl, szl, xl, layer, row_lat, mod, norm_w, w_out)
    return oc, ol, s_new


def kernel(x_prompt, x_sample, cache_k, cache_v, state_ssm, c, c_ctx, ada_w, ada_b, norm_w, final_norm_w, conv_w_in, conv_b_in, conv_w_dw, conv_b_dw, conv_ln_w, conv_ln_b, conv_w_out, conv_b_out, na_w_in, na_rpb, na_w_out, ssd_w_in, ssd_w_conv, ssd_b_conv, ssd_a_log, ssd_dt_bias, ssd_d, ssd_norm_w, ssd_w_out):
    p = dict(cache_k=cache_k, cache_v=cache_v, state_ssm=state_ssm,
             conv_w_in=conv_w_in, conv_b_in=conv_b_in, conv_w_dw=conv_w_dw, conv_b_dw=conv_b_dw,
             conv_ln_w=conv_ln_w, conv_ln_b=conv_ln_b, conv_w_out=conv_w_out, conv_b_out=conv_b_out,
             na_w_in=na_w_in, na_rpb=na_rpb, na_w_out=na_w_out,
             ssd_w_in=ssd_w_in, ssd_w_conv=ssd_w_conv, ssd_b_conv=ssd_b_conv, ssd_a_log=ssd_a_log,
             ssd_dt_bias=ssd_dt_bias, ssd_d=ssd_d, ssd_norm_w=ssd_norm_w, ssd_w_out=ssd_w_out)
    n_ctx, n_lat = x_prompt.shape[0], x_sample.shape[0]
    assert x_prompt.shape[1:] == (SEQ, D) and x_sample.shape[1:] == (DEC_SEQ, D)
    assert (DEPTH - 1) % 3 == 0, "the final RMSNorm is fused into a trailing Conformer layer"
    xc = x_prompt.reshape(n_ctx * SEQ, D)
    xl = x_sample.reshape(n_lat * DEC_SEQ, D)
    cond8 = jnp.concatenate([c_ctx[None], c, jnp.zeros((8 - 1 - n_lat, D), F32)], axis=0)
    mod = _modulation(cond8, ada_w, ada_b).reshape(DEPTH, 8, 1, 3 * D)
    row_ctx = lambda tok: 0
    row_lat = lambda tok: 1 + tok // DEC_SEQ
    new_k, new_v, new_s = [], [], []
    for i in range(DEPTH):
        kind, j = i % 3, i // 3
        nw = norm_w[i][None]
        if kind == 0:
            cp = _prep_conv(p, j)
            fw = final_norm_w[None] if i == DEPTH - 1 else None
            xc = _conv_layer(xc, SEQ, i, row_ctx, mod, nw, *cp, fw)
            xl = _conv_layer(xl, DEC_SEQ, i, row_lat, mod, nw, *cp, fw)
        elif kind == 1:
            xc, xl, k_c, v_c = _na_layer(xc, xl, i, mod, nw, p, n_lat)
            new_k.append(k_c.reshape(n_ctx, SEQ, NA_HEADS, NA_HEAD_DIM))
            new_v.append(v_c.reshape(n_ctx, SEQ, NA_HEADS, NA_HEAD_DIM))
        else:
            xc, xl, s_c = _ssd_layer(xc, xl, i, mod, nw, p, n_lat)
            new_s.append(s_c.reshape(n_ctx, 2, SSD_HEADS, SSD_HEAD_DIM, SSD_STATE))
    return (xc.reshape(n_ctx, SEQ, D), xl.reshape(n_lat, DEC_SEQ, D),
            jnp.stack(new_k, axis=1), jnp.stack(new_v, axis=1), jnp.stack(new_s, axis=1))
```

```python
import functools

import jax
import jax.numpy as jnp
from jax import lax
from jax.experimental import pallas as pl
from jax.experimental.pallas import tpu as pltpu

F32 = jnp.float32
BF16 = jnp.bfloat16

D = 1024
DEPTH = 4
SEQ = 256
DEC_SEQ = 2048
GRID_W = 64
RMS_EPS = 1e-6
LN_EPS = 1e-5
LOG2E = 1.4426950408889634
CONF_K = 31
NA_HEADS = 16
NA_HEAD_DIM = 64
NA_KH = 8
NA_KW = 16
NEG_INF = -1e30
SSD_INNER = 2048
SSD_HEADS = 32
SSD_HEAD_DIM = 64
SSD_STATE = 128
SSD_GROUPS = 4
SSD_GN = SSD_GROUPS * SSD_STATE
SSD_CONV = 7
SSD_CHUNK = 128
SSD_XBC = SSD_INNER + 2 * SSD_GN

LANES = 128
SUBLANES = 8
TOKEN_TILE = 256
CONV_HALO = 16
SSD_HALO = 8
VMEM_LIMIT = 56 * 1024 * 1024


def _cparams(n_axes):
    return pltpu.CompilerParams(dimension_semantics=("arbitrary",) * n_axes,
                                vmem_limit_bytes=VMEM_LIMIT)


def _const_spec(shape):
    nd = len(shape)
    return pl.BlockSpec(shape, lambda *_: (0,) * nd, pipeline_mode=pl.Buffered(1))


def _sigmoid(x):
    return 1.0 / (1.0 + jnp.exp(-x))


def _silu(x):
    return x * _sigmoid(x)


def _rms(x, w):
    ms = jnp.mean(x * x, axis=-1, keepdims=True)
    return x * lax.rsqrt(ms + RMS_EPS) * w


def _modnorm(x, nw, mod):
    return _rms(x, nw) * (1.0 + mod[:, D:2 * D]) + mod[:, :D]


def _mod_spec(layer, row_fn):
    return pl.BlockSpec((None, None, 1, 3 * D), lambda i, *_: (layer, row_fn(i), 0, 0))


def _mod_kernel(cond_ref, w_ref, b_ref, o_ref):
    s = _silu(cond_ref[...]).astype(BF16)
    o_ref[...] = jnp.dot(s, w_ref[...].astype(BF16), preferred_element_type=F32) + b_ref[...]


def _modulation(cond8, ada_w, ada_b):
    tn = 1024
    return pl.pallas_call(
        _mod_kernel,
        grid=(DEPTH, 3 * D // tn),
        in_specs=[pl.BlockSpec((8, D), lambda i, j: (0, 0)),
                  pl.BlockSpec((None, D, tn), lambda i, j: (i, 0, j)),
                  pl.BlockSpec((None, 1, tn), lambda i, j: (i, 0, j))],
        out_specs=pl.BlockSpec((None, 8, tn), lambda i, j: (i, 0, j)),
        out_shape=jax.ShapeDtypeStruct((DEPTH, 8, 3 * D), F32),
        compiler_params=_cparams(2),
        name="adaln_mod",
    )(cond8, ada_w, ada_b.reshape(DEPTH, 1, 3 * D))


CONV_CHUNK = 256


def _conv_kernel(tiles_per_seq, final, *refs):
    refs = list(refs)
    x_ref = refs.pop(0)
    if tiles_per_seq > 1:
        xp_ref = refs.pop(0)
        xn_ref = refs.pop(0)
    nw_ref, mod_ref, wi_ref, bi_ref, wdw_ref, bdw_ref, lnw_ref, lnb_ref, wo_ref, bo_ref = refs[:10]
    refs = refs[10:]
    if final:
        fw_ref = refs.pop(0)
    o_ref, ubuf, cbuf = refs
    T = x_ref.shape[0]
    H = CONV_HALO
    RB = 128

    hb = _modnorm(x_ref[...], nw_ref[...], mod_ref[...]).astype(BF16)
    if tiles_per_seq > 1:
        t = pl.program_id(0) % tiles_per_seq
        hh = _modnorm(jnp.concatenate([xp_ref[...], xn_ref[...]], axis=0), nw_ref[...], mod_ref[...]).astype(BF16)

    def glu(h, cols):
        v = jnp.dot(h, wi_ref[:, cols], preferred_element_type=F32) + bi_ref[:, cols]
        gcols = slice(D + cols.start, D + cols.stop)
        g = jnp.dot(h, wi_ref[:, gcols], preferred_element_type=F32) + bi_ref[:, gcols]
        return v * _sigmoid(g)

    for ch in range(D // CONV_CHUNK):
        cols = slice(ch * CONV_CHUNK, (ch + 1) * CONV_CHUNK)
        u_main = glu(hb, cols)
        if tiles_per_seq > 1:
            uh = glu(hh, cols)
            u_prev = jnp.where(t > 0, uh[0:H], 0.0)
            u_next = jnp.where(t < tiles_per_seq - 1, uh[H:2 * H], 0.0)
        else:
            u_prev = u_next = jnp.zeros((H, CONV_CHUNK), F32)
        for cb in range(CONV_CHUNK // LANES):
            blk = ch * (CONV_CHUNK // LANES) + cb
            lc = slice(blk * LANES, (blk + 1) * LANES)
            cc = slice(cb * LANES, (cb + 1) * LANES)
            ubuf[blk, 0:H, :] = u_prev[:, cc]
            ubuf[blk, H:H + T, :] = u_main[:, cc]
            ubuf[blk, H + T:H + T + H, :] = u_next[:, cc]
            for rb in range(T // RB):
                acc = jnp.zeros((RB, LANES), F32)
                for k in range(CONF_K):
                    r0 = rb * RB + H - CONF_K // 2 + k
                    acc = acc + ubuf[blk, r0:r0 + RB, :] * wdw_ref[k:k + 1, lc]
                cbuf[rb * RB:(rb + 1) * RB, lc] = acc + bdw_ref[:, lc]

    sz = _silu(jnp.dot(hb, wi_ref[:, 2 * D:3 * D], preferred_element_type=F32) + bi_ref[:, 2 * D:3 * D])
    c = cbuf[...]
    mu = jnp.mean(c, axis=-1, keepdims=True)
    xc = c - mu
    var = jnp.mean(xc * xc, axis=-1, keepdims=True)
    ln = xc * lax.rsqrt(var + LN_EPS) * lnw_ref[...] + lnb_ref[...]
    g = (_silu(ln) * sz).astype(BF16)
    y = jnp.dot(g, wo_ref[...], preferred_element_type=F32) + bo_ref[...]
    xn = x_ref[...] + mod_ref[:, 2 * D:3 * D] * y
    if final:
        xn = _rms(xn, fw_ref[...])
    o_ref[...] = xn


def _conv_layer(x, seq_len, layer, row_fn, mod, nw, w_in, b_in, w_dw, b_dw, ln_w, ln_b, w_out, b_out, final_w):
    M = x.shape[0]
    T = TOKEN_TILE
    tps = seq_len // T
    H = CONV_HALO
    n_halo_blocks = M // H
    row = lambda i: (i, 0)
    in_specs = [pl.BlockSpec((T, D), row)]
    args = [x]
    if tps > 1:
        in_specs += [pl.BlockSpec((H, D), lambda i: (jnp.maximum(i * (T // H) - 1, 0), 0)),
                     pl.BlockSpec((H, D), lambda i: (jnp.minimum((i + 1) * (T // H), n_halo_blocks - 1), 0))]
        args += [x, x]
    in_specs += [_const_spec((1, D)), _mod_spec(layer, lambda i: row_fn(i * T)),
                 _const_spec((D, 3 * D)), _const_spec((1, 3 * D)),
                 _const_spec((CONF_K + 1, D)), _const_spec((1, D)), _const_spec((1, D)), _const_spec((1, D)),
                 _const_spec((D, D)), _const_spec((1, D))]
    args += [nw, mod, w_in, b_in, w_dw, b_dw, ln_w, ln_b, w_out, b_out]
    final = final_w is not None
    if final:
        in_specs.append(_const_spec((1, D)))
        args.append(final_w)
    return pl.pallas_call(
        functools.partial(_conv_kernel, tps, final),
        grid=(M // T,),
        in_specs=in_specs,
        out_specs=pl.BlockSpec((T, D), row),
        out_shape=jax.ShapeDtypeStruct((M, D), F32),
        scratch_shapes=[pltpu.VMEM((D // LANES, T + 2 * H, LANES), F32), pltpu.VMEM((T, D), F32)],
        compiler_params=_cparams(1),
        name="conv_layer",
    )(*args)


def _prep_conv(p, j):
    w_dw = jnp.pad(p["conv_w_dw"][j], ((0, 1), (0, 0)))
    return (p["conv_w_in"][j].astype(BF16), p["conv_b_in"][j][None], w_dw, p["conv_b_dw"][j][None],
            p["conv_ln_w"][j][None], p["conv_ln_b"][j][None], p["conv_w_out"][j].astype(BF16), p["conv_b_out"][j][None])


def _out_proj_kernel(g_ref, x_ref, mod_ref, w_ref, o_ref):
    y = jnp.dot(g_ref[...].astype(BF16), w_ref[...], preferred_element_type=F32)
    o_ref[...] = x_ref[...] + mod_ref[:, 2 * D:3 * D] * y


def _out_proj(g, x, layer, row_fn, mod, w_out):
    M, K = g.shape
    tm = TOKEN_TILE
    row = lambda i: (i, 0)
    return pl.pallas_call(
        _out_proj_kernel,
        grid=(M // tm,),
        in_specs=[pl.BlockSpec((tm, K), row), pl.BlockSpec((tm, D), row), _mod_spec(layer, lambda i: row_fn(i * tm)),
                  _const_spec((K, D))],
        out_specs=pl.BlockSpec((tm, D), row),
        out_shape=jax.ShapeDtypeStruct((M, D), F32),
        compiler_params=_cparams(1),
        name="out_proj",
    )(g, x, mod, w_out)


NA_SCALE = NA_HEAD_DIM ** -0.5
NA_ROWS = 32
NA_WIN = NA_KH * GRID_W
NT_DIMS = (((1,), (1,)), ((), ()))


def _na_in_kernel(x_ref, nw_ref, mod_ref, w_ref, q_ref, k_ref, v_ref, sz_ref):
    hb = _modnorm(x_ref[...], nw_ref[...], mod_ref[...]).astype(BF16)
    q_ref[...] = jnp.dot(hb, w_ref[:, 0:D], preferred_element_type=F32).astype(q_ref.dtype)
    k_ref[...] = jnp.dot(hb, w_ref[:, D:2 * D], preferred_element_type=F32).astype(k_ref.dtype)
    v_ref[...] = jnp.dot(hb, w_ref[:, 2 * D:3 * D], preferred_element_type=F32).astype(v_ref.dtype)
    sz_ref[...] = _silu(jnp.dot(hb, w_ref[:, 3 * D:4 * D], preferred_element_type=F32))


def _na_in(x, layer, row_fn, mod, nw, w_in, kv_dtype):
    M = x.shape[0]
    tm = TOKEN_TILE
    row = lambda i: (i, 0)
    blk = pl.BlockSpec((tm, D), row)
    return pl.pallas_call(
        _na_in_kernel,
        grid=(M // tm,),
        in_specs=[blk, _const_spec((1, D)), _mod_spec(layer, lambda i: row_fn(i * tm)), _const_spec((D, 4 * D))],
        out_specs=[blk, blk, blk, blk],
        out_shape=[jax.ShapeDtypeStruct((M, D), BF16), jax.ShapeDtypeStruct((M, D), kv_dtype),
                   jax.ShapeDtypeStruct((M, D), kv_dtype), jax.ShapeDtypeStruct((M, D), F32)],
        compiler_params=_cparams(1),
        name="na_in",
    )(x, nw, mod, w_in)


def _head_masks():
    lane = lax.broadcasted_iota(jnp.int32, (1, LANES), 1)
    return lane < NA_HEAD_DIM


def _na_seq_kernel(x_ref, nw_ref, mod_ref, wi_ref, wo_ref, o_ref, k_ref, v_ref, g_ref):
    n = x_ref.shape[0]
    lo = _head_masks()
    x = x_ref[...]
    hb = _modnorm(x, nw_ref[...], mod_ref[...]).astype(BF16)
    hpc = CONV_CHUNK // NA_HEAD_DIM
    for ch in range(D // CONV_CHUNK):
        cols = slice(ch * CONV_CHUNK, (ch + 1) * CONV_CHUNK)
        q = jnp.dot(hb, wi_ref[:, cols], preferred_element_type=F32).astype(BF16)
        k = jnp.dot(hb, wi_ref[:, D + cols.start:D + cols.stop], preferred_element_type=F32)
        v = jnp.dot(hb, wi_ref[:, 2 * D + cols.start:2 * D + cols.stop], preferred_element_type=F32)
        sz = _silu(jnp.dot(hb, wi_ref[:, 3 * D + cols.start:3 * D + cols.stop], preferred_element_type=F32))
        k_ref[:, ch * hpc:(ch + 1) * hpc, :] = k.reshape(n, hpc, NA_HEAD_DIM)
        v_ref[:, ch * hpc:(ch + 1) * hpc, :] = v.reshape(n, hpc, NA_HEAD_DIM)
        kb = k.astype(BF16)
        vb = v.astype(BF16)
        for pp in range(CONV_CHUNK // LANES):
            pc = slice(pp * LANES, (pp + 1) * LANES)
            qp = q[:, pc]
            zero = jnp.zeros_like(qp)
            qs = jnp.concatenate([jnp.where(lo, qp, zero), jnp.where(lo, zero, qp)], axis=0)
            s = lax.dot_general(qs, kb[:, pc], NT_DIMS, preferred_element_type=F32) * NA_SCALE
            e = jnp.exp(s - jnp.max(s, axis=-1, keepdims=True))
            l = jnp.sum(e, axis=-1, keepdims=True)
            o = jnp.dot(e.astype(BF16), vb[:, pc], preferred_element_type=F32) / l
            g = jnp.where(lo, o[:n], o[n:]) * sz[:, pc]
            g_ref[:, cols.start + pp * LANES:cols.start + (pp + 1) * LANES] = g.astype(BF16)
    y = jnp.dot(g_ref[...], wo_ref[...], preferred_element_type=F32)
    o_ref[...] = x + mod_ref[:, 2 * D:3 * D] * y


def _na_seq(x, seq_len, layer, row_fn, mod, nw, w_in, w_out):
    M = x.shape[0]
    row = lambda i: (i, 0)
    kv_blk = pl.BlockSpec((seq_len, NA_HEADS, NA_HEAD_DIM), lambda i: (i, 0, 0))
    kv_shape = jax.ShapeDtypeStruct((M, NA_HEADS, NA_HEAD_DIM), F32)
    return pl.pallas_call(
        _na_seq_kernel,
        grid=(M // seq_len,),
        in_specs=[pl.BlockSpec((seq_len, D), row), _const_spec((1, D)), _mod_spec(layer, lambda i: row_fn(i * seq_len)),
                  _const_spec((D, 4 * D)), _const_spec((D, D))],
        out_specs=[pl.BlockSpec((seq_len, D), row), kv_blk, kv_blk],
        out_shape=[jax.ShapeDtypeStruct((M, D), F32), kv_shape, kv_shape],
        scratch_shapes=[pltpu.VMEM((seq_len, D), BF16)],
        compiler_params=_cparams(1),
        name="na_seq",
    )(x, nw, mod, w_in, w_out)


def _rpb_kernel(rpb_ref, o_ref):
    h = pl.program_id(0)
    c = lax.broadcasted_iota(jnp.int32, (GRID_W, LANES), 0)
    lane = lax.broadcasted_iota(jnp.int32, (GRID_W, LANES), 1)
    kc = lane & (GRID_W - 1)
    upper = lax.broadcasted_iota(jnp.int32, (1, LANES), 1) >= GRID_W
    cs = jnp.clip(c - NA_KW // 2, 0, GRID_W - NA_KW)
    valid = (kc >= cs) & (kc < cs + NA_KW)
    n_pair = 2 * NA_KH - 2

    def toeplitz(dri, lane0):
        row = rpb_ref[pl.ds(h * (2 * NA_KH - 1) + dri, 1), :]
        rows = jnp.broadcast_to(row, (GRID_W, LANES))
        return pltpu.roll(rows, (LANES - (NA_KW - 1) + lane0) % LANES, 1, stride=1, stride_axis=0)

    for dri in range(n_pair):
        tile = jnp.where(upper, toeplitz(dri + 1, GRID_W), toeplitz(dri, 0))
        tile = jnp.where(valid, tile, NEG_INF)
        for idx in range(NA_KH):
            jj2 = dri - (NA_KH - 1) + idx
            if jj2 % 2 == 0 and 0 <= jj2 // 2 < NA_KH // 2:
                jj = jj2 // 2
                o_ref[idx, :, jj * LANES:(jj + 1) * LANES] = tile


def _rpb_table(rpb):
    n_rows = NA_HEADS * (2 * NA_KH - 1)
    rows = jnp.pad(rpb.reshape(n_rows, 2 * NA_KW - 1), ((0, 0), (0, LANES - (2 * NA_KW - 1))))
    return pl.pallas_call(
        _rpb_kernel,
        grid=(NA_HEADS,),
        in_specs=[_const_spec((n_rows, LANES))],
        out_specs=pl.BlockSpec((NA_KH, None, GRID_W, NA_WIN), lambda h: (0, h, 0, 0)),
        out_shape=jax.ShapeDtypeStruct((NA_KH, NA_HEADS, GRID_W, NA_WIN), F32),
        compiler_params=_cparams(1),
        name="na_rpb_table",
    )(rows)


def _na_row_start(r):
    return jnp.clip(r - NA_KH // 2, 0, NA_ROWS - NA_KH)


def _na_lat_kernel(q_ref, k_ref, v_ref, ck_ref, cv_ref, bias_ref, sz_ref, o_ref):
    lo = _head_masks()
    k0 = pl.multiple_of(_na_row_start(pl.program_id(1)) * GRID_W, GRID_W)
    for p in range(NA_HEADS // 2):
        cols = slice(p * LANES, (p + 1) * LANES)
        qp = q_ref[:, cols]
        kw = k_ref[pl.ds(k0, NA_WIN), cols]
        vw = v_ref[pl.ds(k0, NA_WIN), cols]
        ck = ck_ref[:, cols]
        cv = cv_ref[:, cols]
        zero = jnp.zeros_like(qp)
        qs = jnp.concatenate([jnp.where(lo, qp, zero), jnp.where(lo, zero, qp)], axis=0)
        bias = bias_ref[2 * p:2 * p + 2].reshape(2 * GRID_W, NA_WIN)
        s_loc = lax.dot_general(qs, kw, NT_DIMS, preferred_element_type=F32) * NA_SCALE + bias
        s_ctx = lax.dot_general(qs, ck, NT_DIMS, preferred_element_type=F32) * NA_SCALE
        m = jnp.maximum(jnp.max(s_loc, axis=-1, keepdims=True), jnp.max(s_ctx, axis=-1, keepdims=True))
        e_loc = jnp.exp(s_loc - m)
        e_ctx = jnp.exp(s_ctx - m)
        l = jnp.sum(e_loc, axis=-1, keepdims=True) + jnp.sum(e_ctx, axis=-1, keepdims=True)
        o = (jnp.dot(e_loc.astype(BF16), vw, preferred_element_type=F32)
             + jnp.dot(e_ctx.astype(BF16), cv, preferred_element_type=F32)) / l
        o_ref[:, cols] = jnp.where(lo, o[:GRID_W], o[GRID_W:]) * sz_ref[:, cols]


def _na_lat(q, k, v, ck, cv, bias, sz, n_lat):
    T = NA_ROWS * GRID_W
    past = ck.shape[1]
    qblk = pl.BlockSpec((GRID_W, D), lambda b, r: (b * NA_ROWS + r, 0))
    kvblk = pl.BlockSpec((T, D), lambda b, r: (b, 0))
    cblk = pl.BlockSpec((None, past, D), lambda b, r: (b, 0, 0))
    bblk = pl.BlockSpec((None, NA_HEADS, GRID_W, NA_WIN), lambda b, r: (r - _na_row_start(r), 0, 0, 0))
    return pl.pallas_call(
        _na_lat_kernel,
        grid=(n_lat, NA_ROWS),
        in_specs=[qblk, kvblk, kvblk, cblk, cblk, bblk, qblk],
        out_specs=qblk,
        out_shape=jax.ShapeDtypeStruct((n_lat * T, D), F32),
        compiler_params=_cparams(2),
        name="na_lat_attn",
    )(q, k, v, ck, cv, bias, sz)


def _na_layer(xc, xl, layer, mod, nw, p, n_lat):
    j = layer // 3
    w_in = p["na_w_in"][j].astype(BF16)
    w_out = p["na_w_out"][j].astype(BF16)
    row_ctx = lambda tok: 0
    row_lat = lambda tok: 1 + tok // (NA_ROWS * GRID_W)
    oc, kc, vc = _na_seq(xc, SEQ, layer, row_ctx, mod, nw, w_in, w_out)
    ql, kl, vl, szl = _na_in(xl, layer, row_lat, mod, nw, w_in, BF16)
    past = p["cache_k"].shape[2]
    ck = p["cache_k"][:n_lat, j].reshape(n_lat, past, D).astype(BF16)
    cv = p["cache_v"][:n_lat, j].reshape(n_lat, past, D).astype(BF16)
    gl = _na_lat(ql, kl, vl, ck, cv, _rpb_table(p["na_rpb"][j]), szl, n_lat)
    return oc, _out_proj(gl, xl, layer, row_lat, mod, w_out), kc, vc


SSD_IN_COLS = SSD_INNER + SSD_XBC


def _softplus(x):
    return jnp.maximum(x, 0.0) + jnp.log(1.0 + jnp.exp(-jnp.abs(x)))


def _ssd_in_kernel(tiles_per_seq, *refs):
    refs = list(refs)
    x_ref = refs.pop(0)
    if tiles_per_seq > 1:
        xp_ref = refs.pop(0)
        xn_ref = refs.pop(0)
    nw_ref, mod_ref, wz_ref, wx_ref, wdt_ref, dtb_ref, wc_ref, bc_ref, sz_ref, xbc_ref, dt_ref, xpad_ref = refs
    hb = _modnorm(x_ref[...], nw_ref[...], mod_ref[...]).astype(BF16)
    halo = None
    if tiles_per_seq > 1:
        t = pl.program_id(0) % tiles_per_seq
        xh = jnp.concatenate([xp_ref[...], xn_ref[...]], axis=0)
        halo = (_modnorm(xh, nw_ref[...], mod_ref[...]).astype(BF16), t > 0, t < tiles_per_seq - 1)
    dt_ref[...] = _ssd_project(hb, halo, wz_ref, wx_ref, wdt_ref, dtb_ref, wc_ref, bc_ref, xpad_ref, sz_ref, xbc_ref)


def _ssd_project(hb, halo, wz_ref, wx_ref, wdt_ref, dtb_ref, wc_ref, bc_ref, xpad_ref, sz_ref, xbc_ref):
    tm = hb.shape[0]
    HL = SSD_HALO
    RB = 128
    for j in range(SSD_XBC // D):
        c0 = j * D
        raw = jnp.dot(hb, wx_ref[:, c0:c0 + D], preferred_element_type=F32)
        if halo is not None:
            rawh = jnp.dot(halo[0], wx_ref[:, c0:c0 + D], preferred_element_type=F32)
            raw_prev = jnp.where(halo[1], rawh[0:HL], 0.0)
            raw_next = jnp.where(halo[2], rawh[HL:2 * HL], 0.0)
        else:
            raw_prev = raw_next = jnp.zeros((HL, D), F32)
        for cb in range(D // LANES):
            blk = j * (D // LANES) + cb
            lc = slice(blk * LANES, (blk + 1) * LANES)
            cc = slice(cb * LANES, (cb + 1) * LANES)
            xpad_ref[blk, 0:HL, :] = raw_prev[:, cc]
            xpad_ref[blk, HL:HL + tm, :] = raw[:, cc]
            xpad_ref[blk, HL + tm:HL + tm + HL, :] = raw_next[:, cc]
            for rb in range(tm // RB):
                acc = jnp.zeros((RB, LANES), F32)
                for k in range(SSD_CONV):
                    r0 = rb * RB + HL - SSD_CONV // 2 + k
                    acc = acc + xpad_ref[blk, r0:r0 + RB, :] * wc_ref[k:k + 1, lc]
                xbc_ref[rb * RB:(rb + 1) * RB, lc] = _silu(acc + bc_ref[:, lc])
    for j in range(SSD_INNER // D):
        sz_ref[:, j * D:(j + 1) * D] = _silu(jnp.dot(hb, wz_ref[:, j * D:(j + 1) * D], preferred_element_type=F32))
    raw = jnp.dot(hb, wdt_ref[...], preferred_element_type=F32) + dtb_ref[...]
    lane = lax.broadcasted_iota(jnp.int32, (1, LANES), 1)
    return jnp.where(lane < 2 * SSD_HEADS, _softplus(raw), 0.0)


def _ssd_in(x, seq_len, layer, row_fn, mod, nw, w_z, w_x, w_dt, dt_bias, w_conv, b_conv):
    M = x.shape[0]
    tm = TOKEN_TILE
    HL = SSD_HALO
    tps = seq_len // tm
    n_hblk = M // HL
    row = lambda i: (i, 0)
    in_specs = [pl.BlockSpec((tm, D), row)]
    args = [x]
    if tps > 1:
        in_specs += [pl.BlockSpec((HL, D), lambda i: (jnp.maximum(i * (tm // HL) - 1, 0), 0)),
                     pl.BlockSpec((HL, D), lambda i: (jnp.minimum((i + 1) * (tm // HL), n_hblk - 1), 0))]
        args += [x, x]
    in_specs += [_const_spec((1, D)), _mod_spec(layer, lambda i: row_fn(i * tm)),
                 _const_spec((D, SSD_INNER)), _const_spec((D, SSD_XBC)), _const_spec((D, LANES)), _const_spec((1, LANES)),
                 _const_spec((SSD_CONV + 1, SSD_XBC)), _const_spec((1, SSD_XBC))]
    args += [nw, mod, w_z, w_x, w_dt, dt_bias, w_conv, b_conv]
    return pl.pallas_call(
        functools.partial(_ssd_in_kernel, tps),
        grid=(M // tm,),
        in_specs=in_specs,
        out_specs=[pl.BlockSpec((tm, SSD_INNER), row), pl.BlockSpec((tm, SSD_XBC), row), pl.BlockSpec((tm, LANES), row)],
        out_shape=[jax.ShapeDtypeStruct((M, SSD_INNER), F32), jax.ShapeDtypeStruct((M, SSD_XBC), F32),
                   jax.ShapeDtypeStruct((M, LANES), F32)],
        scratch_shapes=[pltpu.VMEM((SSD_XBC // LANES, tm + 2 * HL, LANES), F32)],
        compiler_params=_cparams(1),
        name="ssd_in",
    )(*args)


def _cumsum_rows(a, reverse):
    n = a.shape[0]
    row = lax.broadcasted_iota(jnp.int32, a.shape, 0)
    k = 1
    while k < n:
        if reverse:
            a = a + jnp.where(row < n - k, pltpu.roll(a, n - k, 0), 0.0)
        else:
            a = a + jnp.where(row >= k, pltpu.roll(a, k, 0), 0.0)
        k *= 2
    return a


def _ssd_core_kernel(nc, has_h0, emit_state, *refs):
    refs = list(refs)
    dir_refs = [refs[0:2], refs[2:4]]
    alog_ref, dsk_ref = refs[4:6]
    refs = refs[6:]
    if has_h0:
        h0_ref = refs.pop(0)
    y_refs = [refs.pop(0), refs.pop(0)]
    if emit_state:
        so_ref = refs.pop(0)
    (st_ref,) = refs
    Q, N, P = SSD_CHUNK, SSD_STATE, SSD_HEAD_DIM
    c = pl.program_id(1)
    n_blk = SSD_INNER // LANES

    @pl.when(c == 0)
    def _init():
        if has_h0:
            for dr in range(2):
                for j in range(n_blk):
                    st_ref[dr, :, j * LANES:(j + 1) * LANES] = h0_ref[dr, j * LANES:(j + 1) * LANES, :].T
        else:
            st_ref[...] = jnp.zeros(st_ref.shape, F32)

    A = -jnp.exp(alog_ref[...])
    for dr in range(2):
        x_ref, dt_ref = dir_refs[dr]
        y_ref = y_refs[dr]

        def x_at(c0, x_ref=x_ref):
            return x_ref[:, c0:c0 + LANES]

        def y_put(c0, y, y_ref=y_ref):
            y_ref[:, c0:c0 + LANES] = y

        _ssd_chunk(dr, x_at, dt_ref[...], A, st_ref, y_put, dsk_ref)

    if emit_state:
        @pl.when(c == nc - 1)
        def _emit():
            _ssd_emit_state(st_ref, so_ref)


def _ssd_emit_state(st_ref, so_ref):
    for dr in range(2):
        for j in range(SSD_INNER // LANES):
            so_ref[dr, j * LANES:(j + 1) * LANES, :] = st_ref[dr, :, j * LANES:(j + 1) * LANES].T


def _ssd_chunk(dr, x_at, dt, A, st_ref, y_put, dsk_ref):
    Q, N, P = SSD_CHUNK, SSD_STATE, SSD_HEAD_DIM
    lo = _head_masks()
    ri = lax.broadcasted_iota(jnp.int32, (Q, Q), 0)
    ci = lax.broadcasted_iota(jnp.int32, (Q, Q), 1)
    cum = _cumsum_rows(dt * A, reverse=(dr == 1)) * LOG2E
    cum_t = cum.T
    dt_t = dt.T
    tot_row = Q - 1 if dr == 0 else 0
    mask = (ri >= ci) if dr == 0 else (ri <= ci)
    for g in range(SSD_GROUPS):
        bg_t = x_at(SSD_INNER + g * N).T
        cg = x_at(SSD_INNER + SSD_GN + g * N).astype(BF16)
        cb = jnp.dot(cg, bg_t.astype(BF16), preferred_element_type=F32)
        for pp in range(SSD_HEADS // SSD_GROUPS // 2):
            head = g * (SSD_HEADS // SSD_GROUPS) + 2 * pp
            c0 = head * P
            li = dr * SSD_HEADS + head
            xpair = x_at(c0)
            xpair_b = xpair.astype(BF16)
            cum_cols = [cum[:, li:li + 1], cum[:, li + 1:li + 2]]
            cum_x = jnp.where(lo, cum_cols[0], cum_cols[1])
            yd, sd = [], []
            for hh in range(2):
                dt_row = dt_t[li + hh:li + hh + 1, :]
                cum_row = cum_t[li + hh:li + hh + 1, :]
                decay = jnp.exp2(jnp.where(mask, cum_cols[hh] - cum_row, -jnp.inf))
                yd.append(jnp.dot((cb * decay * dt_row).astype(BF16), xpair_b, preferred_element_type=F32))
                to_end = jnp.exp2(cum_row[:, tot_row:tot_row + 1] - cum_row)
                sd.append(jnp.dot((bg_t * (dt_row * to_end)).astype(BF16), xpair_b, preferred_element_type=F32))
            st = st_ref[dr, :, c0:c0 + LANES]
            y = (jnp.where(lo, yd[0], yd[1])
                 + jnp.dot(cg, st.astype(BF16), preferred_element_type=F32) * jnp.exp2(cum_x))
            if dr == 0:
                y = y + xpair * dsk_ref[:, c0:c0 + LANES]
            y_put(c0, y)
            tot = cum_x[tot_row:tot_row + 1, :]
            st_ref[dr, :, c0:c0 + LANES] = st * jnp.exp2(tot) + jnp.where(lo, sd[0], sd[1])


def _ssd_seq_kernel(x_ref, nw_ref, mod_ref, wz_ref, wx_ref, wdt_ref, dtb_ref, wc_ref, bc_ref, alog_ref, dsk_ref, gnw_ref, wo_ref,
                    o_ref, so_ref, xpad_ref, xbc_ref, sz_ref, y_ref, st_ref):
    Q = SSD_CHUNK
    nc = x_ref.shape[0] // Q
    x = x_ref[...]
    hb = _modnorm(x, nw_ref[...], mod_ref[...]).astype(BF16)
    dt = _ssd_project(hb, None, wz_ref, wx_ref, wdt_ref, dtb_ref, wc_ref, bc_ref, xpad_ref, sz_ref, xbc_ref)
    st_ref[...] = jnp.zeros(st_ref.shape, F32)
    A = -jnp.exp(alog_ref[...])
    for dr in range(2):
        for ck in (range(nc) if dr == 0 else reversed(range(nc))):
            rows = slice(ck * Q, (ck + 1) * Q)

            def x_at(c0, rows=rows):
                return xbc_ref[rows, c0:c0 + LANES]

            def y_put(c0, y, rows=rows, dr=dr):
                if dr == 0:
                    y_ref[rows, c0:c0 + LANES] = y
                else:
                    y_ref[rows, c0:c0 + LANES] += y

            _ssd_chunk(dr, x_at, dt[rows], A, st_ref, y_put, dsk_ref)
    _ssd_emit_state(st_ref, so_ref)
    g = _rms(y_ref[...] * sz_ref[...], gnw_ref[...]).astype(BF16)
    o_ref[...] = x + mod_ref[:, 2 * D:3 * D] * jnp.dot(g, wo_ref[...], preferred_element_type=F32)


def _ssd_seq(x, seq_len, layer, row_fn, mod, nw, w_z, w_x, w_dt, dt_bias, w_conv, b_conv, a_log, d_skip, norm_w, w_out):
    M = x.shape[0]
    n_seq = M // seq_len
    HL = SSD_HALO
    row = lambda i: (i, 0)
    st_blk = pl.BlockSpec((None, 2, SSD_INNER, SSD_STATE), lambda i: (i, 0, 0, 0))
    return pl.pallas_call(
        _ssd_seq_kernel,
        grid=(n_seq,),
        in_specs=[pl.BlockSpec((seq_len, D), row), _const_spec((1, D)), _mod_spec(layer, lambda i: row_fn(i * seq_len)),
                  _const_spec((D, SSD_INNER)), _const_spec((D, SSD_XBC)), _const_spec((D, LANES)), _const_spec((1, LANES)),
                  _const_spec((SSD_CONV + 1, SSD_XBC)), _const_spec((1, SSD_XBC)),
                  _const_spec((1, LANES)), _const_spec((1, SSD_INNER)), _const_spec((1, SSD_INNER)),
                  _const_spec((SSD_INNER, D))],
        out_specs=[pl.BlockSpec((seq_len, D), row), st_blk],
        out_shape=[jax.ShapeDtypeStruct((M, D), F32), jax.ShapeDtypeStruct((n_seq, 2, SSD_INNER, SSD_STATE), F32)],
        scratch_shapes=[pltpu.VMEM((SSD_XBC // LANES, seq_len + 2 * HL, LANES), F32),
                        pltpu.VMEM((seq_len, SSD_XBC), F32), pltpu.VMEM((seq_len, SSD_INNER), F32),
                        pltpu.VMEM((seq_len, SSD_INNER), F32), pltpu.VMEM((2, SSD_STATE, SSD_INNER), F32)],
        compiler_params=_cparams(1),
        name="ssd_seq",
    )(x, nw, mod, w_z, w_x, w_dt, dt_bias, w_conv, b_conv, a_log, d_skip, norm_w, w_out)


def _ssd_core(xbc, dtp, a_log, d_skip, h0, n_seq, seq_len, emit_state):
    Q = SSD_CHUNK
    nc = seq_len // Q
    M = n_seq * seq_len

    def chunk(dr):
        return (lambda b, c: b * nc + c) if dr == 0 else (lambda b, c: b * nc + nc - 1 - c)

    in_specs, args = [], []
    for dr in range(2):
        ch = chunk(dr)
        in_specs += [pl.BlockSpec((Q, SSD_XBC), lambda b, c, ch=ch: (ch(b, c), 0)),
                     pl.BlockSpec((Q, LANES), lambda b, c, ch=ch: (ch(b, c), 0))]
        args += [xbc, dtp]
    in_specs += [_const_spec((1, LANES)), _const_spec((1, SSD_INNER))]
    args += [a_log, d_skip]
    st_blk = pl.BlockSpec((None, 2, SSD_INNER, SSD_STATE), lambda b, c: (b, 0, 0, 0))
    has_h0 = h0 is not None
    if has_h0:
        in_specs.append(st_blk)
        args.append(h0)
    out_specs = [pl.BlockSpec((Q, SSD_INNER), lambda b, c: (chunk(0)(b, c), 0)),
                 pl.BlockSpec((Q, SSD_INNER), lambda b, c: (chunk(1)(b, c), 0))]
    out_shape = [jax.ShapeDtypeStruct((M, SSD_INNER), F32)] * 2
    if emit_state:
        out_specs.append(st_blk)
        out_shape.append(jax.ShapeDtypeStruct((n_seq, 2, SSD_INNER, SSD_STATE), F32))
    return pl.pallas_call(
        functools.partial(_ssd_core_kernel, nc, has_h0, emit_state),
        grid=(n_seq, nc),
        in_specs=in_specs,
        out_specs=out_specs,
        out_shape=out_shape,
        scratch_shapes=[pltpu.VMEM((2, SSD_STATE, SSD_INNER), F32)],
        compiler_params=_cparams(2),
        name="ssd_core",
    )(*args)


def _ssd_out_kernel(yf_ref, yb_ref, sz_ref, x_ref, mod_ref, nw_ref, w_ref, o_ref):
    g = _rms((yf_ref[...] + yb_ref[...]) * sz_ref[...], nw_ref[...]).astype(BF16)
    o_ref[...] = x_ref[...] + mod_ref[:, 2 * D:3 * D] * jnp.dot(g, w_ref[...], preferred_element_type=F32)


def _ssd_out(yf, yb, sz, x, layer, row_fn, mod, norm_w, w_out):
    M = x.shape[0]
    tm = TOKEN_TILE
    row = lambda i: (i, 0)
    wide = pl.BlockSpec((tm, SSD_INNER), row)
    return pl.pallas_call(
        _ssd_out_kernel,
        grid=(M // tm,),
        in_specs=[wide, wide, wide, pl.BlockSpec((tm, D), row), _mod_spec(layer, lambda i: row_fn(i * tm)),
                  _const_spec((1, SSD_INNER)), _const_spec((SSD_INNER, D))],
        out_specs=pl.BlockSpec((tm, D), row),
        out_shape=jax.ShapeDtypeStruct((M, D), F32),
        compiler_params=_cparams(1),
        name="ssd_out",
    )(yf, yb, sz, x, mod, norm_w, w_out)


def _ssd_layer(xc, xl, layer, mod, nw, p, n_lat):
    j = layer // 3
    w_z = p["ssd_w_in"][j][:, :SSD_INNER].astype(BF16)
    w_x = p["ssd_w_in"][j][:, SSD_INNER:SSD_IN_COLS].astype(BF16)
    w_dt = jnp.pad(p["ssd_w_in"][j][:, SSD_IN_COLS:], ((0, 0), (0, LANES - 2 * SSD_HEADS))).astype(BF16)
    dt_bias = jnp.pad(p["ssd_dt_bias"][j].reshape(1, -1), ((0, 0), (0, LANES - 2 * SSD_HEADS)))
    a_log = jnp.pad(p["ssd_a_log"][j].reshape(1, -1), ((0, 0), (0, LANES - 2 * SSD_HEADS)))
    w_conv = jnp.pad(p["ssd_w_conv"][j], ((0, 1), (0, 0)))
    b_conv = p["ssd_b_conv"][j][None]
    d_skip = jnp.repeat(p["ssd_d"][j], SSD_HEAD_DIM)[None]
    norm_w = p["ssd_norm_w"][j][None]
    w_out = p["ssd_w_out"][j].astype(BF16)
    row_ctx = lambda tok: 0
    row_lat = lambda tok: 1 + tok // DEC_SEQ
    oc, s_new = _ssd_seq(xc, SEQ, layer, row_ctx, mod, nw, w_z, w_x, w_dt, dt_bias, w_conv, b_conv, a_log, d_skip, norm_w, w_out)
    szl, xbc_l, dt_l = _ssd_in(xl, DEC_SEQ, layer, row_lat, mod, nw, w_z, w_x, w_dt, dt_bias, w_conv, b_conv)
    h0 = p["state_ssm"][:n_lat, j].reshape(n_lat, 2, SSD_INNER, SSD_STATE)
    yfl, ybl = _ssd_core(xbc_l, dt_l, a_log, d_skip, h0, n_lat, DEC_SEQ, False)
    ol = _ssd_out(yfl, ybl, szl, xl, layer, row_lat, mod, norm_w, w_out)
    return oc, ol, s_new


def kernel(x_prompt, x_sample, cache_k, cache_v, state_ssm, c, c_ctx, ada_w, ada_b, norm_w, final_norm_w, conv_w_in, conv_b_in, conv_w_dw, conv_b_dw, conv_ln_w, conv_ln_b, conv_w_out, conv_b_out, na_w_in, na_rpb, na_w_out, ssd_w_in, ssd_w_conv, ssd_b_conv, ssd_a_log, ssd_dt_bias, ssd_d, ssd_norm_w, ssd_w_out):
    p = dict(cache_k=cache_k, cache_v=cache_v, state_ssm=state_ssm,
             conv_w_in=conv_w_in, conv_b_in=conv_b_in, conv_w_dw=conv_w_dw, conv_b_dw=conv_b_dw,
             conv_ln_w=conv_ln_w, conv_ln_b=conv_ln_b, conv_w_out=conv_w_out, conv_b_out=conv_b_out,
             na_w_in=na_w_in, na_rpb=na_rpb, na_w_out=na_w_out,
             ssd_w_in=ssd_w_in, ssd_w_conv=ssd_w_conv, ssd_b_conv=ssd_b_conv, ssd_a_log=ssd_a_log,
             ssd_dt_bias=ssd_dt_bias, ssd_d=ssd_d, ssd_norm_w=ssd_norm_w, ssd_w_out=ssd_w_out)
    n_ctx, n_lat = x_prompt.shape[0], x_sample.shape[0]
    assert x_prompt.shape[1:] == (SEQ, D) and x_sample.shape[1:] == (DEC_SEQ, D)
    assert (DEPTH - 1) % 3 == 0, "the final RMSNorm is fused into a trailing Conformer layer"
    xc = x_prompt.reshape(n_ctx * SEQ, D)
    xl = x_sample.reshape(n_lat * DEC_SEQ, D)
    cond8 = jnp.concatenate([c_ctx[None], c, jnp.zeros((8 - 1 - n_lat, D), F32)], axis=0)
    mod = _modulation(cond8, ada_w, ada_b).reshape(DEPTH, 8, 1, 3 * D)
    row_ctx = lambda tok: 0
    row_lat = lambda tok: 1 + tok // DEC_SEQ
    new_k, new_v, new_s = [], [], []
    for i in range(DEPTH):
        kind, j = i % 3, i // 3
        nw = norm_w[i][None]
        if kind == 0:
            cp = _prep_conv(p, j)
            fw = final_norm_w[None] if i == DEPTH - 1 else None
            xc = _conv_layer(xc, SEQ, i, row_ctx, mod, nw, *cp, fw)
            xl = _conv_layer(xl, DEC_SEQ, i, row_lat, mod, nw, *cp, fw)
        elif kind == 1:
            xc, xl, k_c, v_c = _na_layer(xc, xl, i, mod, nw, p, n_lat)
            new_k.append(k_c.reshape(n_ctx, SEQ, NA_HEADS, NA_HEAD_DIM))
            new_v.append(v_c.reshape(n_ctx, SEQ, NA_HEADS, NA_HEAD_DIM))
        else:
            xc, xl, s_c = _ssd_layer(xc, xl, i, mod, nw, p, n_lat)
            new_s.append(s_c.reshape(n_ctx, 2, SSD_HEADS, SSD_HEAD_DIM, SSD_STATE))
    return (xc.reshape(n_ctx, SEQ, D), xl.reshape(n_lat, DEC_SEQ, D),
            jnp.stack(new_k, axis=1), jnp.stack(new_v, axis=1), jnp.stack(new_s, axis=1))
```

```python
import functools

import jax
import jax.numpy as jnp
from jax import lax
from jax.experimental import pallas as pl
from jax.experimental.pallas import tpu as pltpu

F32 = jnp.float32
BF16 = jnp.bfloat16

D = 1024
DEPTH = 4
SEQ = 256
DEC_SEQ = 2048
GRID_W = 64
RMS_EPS = 1e-6
LN_EPS = 1e-5
LOG2E = 1.4426950408889634
CONF_K = 31
NA_HEADS = 16
NA_HEAD_DIM = 64
NA_KH = 8
NA_KW = 16
NEG_INF = -1e30
SSD_INNER = 2048
SSD_HEADS = 32
SSD_HEAD_DIM = 64
SSD_STATE = 128
SSD_GROUPS = 4
SSD_GN = SSD_GROUPS * SSD_STATE
SSD_CONV = 7
SSD_CHUNK = 128
SSD_XBC = SSD_INNER + 2 * SSD_GN

LANES = 128
SUBLANES = 8
TOKEN_TILE = 256
W_CHUNK = 512
CONV_HALO = 16
SSD_HALO = 8
VMEM_LIMIT = 56 * 1024 * 1024


def _cparams(n_axes):
    return pltpu.CompilerParams(dimension_semantics=("arbitrary",) * n_axes,
                                vmem_limit_bytes=VMEM_LIMIT)


def _const_spec(shape):
    nd = len(shape)
    return pl.BlockSpec(shape, lambda *_: (0,) * nd, pipeline_mode=pl.Buffered(1))


def _wsplit(w):
    K, N = w.shape
    return w.astype(BF16).reshape(K, N // W_CHUNK, W_CHUNK).transpose(1, 0, 2)


def _wspec(K, N):
    return _const_spec((N // W_CHUNK, K, W_CHUNK))


def _wdot(h, w_ref, c0, n):
    outs = []
    c = c0
    while c < c0 + n:
        j, off = divmod(c, W_CHUNK)
        m = min(W_CHUNK - off, c0 + n - c)
        outs.append(jnp.dot(h, w_ref[j, :, off:off + m], preferred_element_type=F32))
        c += m
    return outs[0] if len(outs) == 1 else jnp.concatenate(outs, axis=1)


def _sigmoid(x):
    return 1.0 / (1.0 + jnp.exp(-x))


def _silu(x):
    return x * _sigmoid(x)


def _rms(x, w):
    ms = jnp.mean(x * x, axis=-1, keepdims=True)
    return x * lax.rsqrt(ms + RMS_EPS) * w


def _modnorm(x, nw, mod):
    return _rms(x, nw) * (1.0 + mod[:, D:2 * D]) + mod[:, :D]


def _mod_spec(layer, row_fn):
    return pl.BlockSpec((None, None, 1, 3 * D), lambda i, *_: (layer, row_fn(i), 0, 0))


def _mod_kernel(cond_ref, w_ref, b_ref, o_ref):
    s = _silu(cond_ref[...]).astype(BF16)
    o_ref[...] = jnp.dot(s, w_ref[...].astype(BF16), preferred_element_type=F32) + b_ref[...]


def _modulation(cond8, ada_w, ada_b):
    tn = 1024
    return pl.pallas_call(
        _mod_kernel,
        grid=(DEPTH, 3 * D // tn),
        in_specs=[pl.BlockSpec((8, D), lambda i, j: (0, 0)),
                  pl.BlockSpec((None, D, tn), lambda i, j: (i, 0, j)),
                  pl.BlockSpec((None, 1, tn), lambda i, j: (i, 0, j))],
        out_specs=pl.BlockSpec((None, 8, tn), lambda i, j: (i, 0, j)),
        out_shape=jax.ShapeDtypeStruct((DEPTH, 8, 3 * D), F32),
        compiler_params=_cparams(2),
        name="adaln_mod",
    )(cond8, ada_w, ada_b.reshape(DEPTH, 1, 3 * D))


CONV_CHUNK = 256


def _conv_kernel(tiles_per_seq, final, *refs):
    refs = list(refs)
    x_ref = refs.pop(0)
    if tiles_per_seq > 1:
        xp_ref = refs.pop(0)
        xn_ref = refs.pop(0)
    nw_ref, mod_ref, wi_ref, bi_ref, wdw_ref, bdw_ref, lnw_ref, lnb_ref, wo_ref, bo_ref = refs[:10]
    refs = refs[10:]
    if final:
        fw_ref = refs.pop(0)
    o_ref, ubuf, cbuf = refs
    T = x_ref.shape[0]
    H = CONV_HALO
    RB = 128

    hb = _modnorm(x_ref[...], nw_ref[...], mod_ref[...]).astype(BF16)
    if tiles_per_seq > 1:
        t = pl.program_id(0) % tiles_per_seq
        hh = _modnorm(jnp.concatenate([xp_ref[...], xn_ref[...]], axis=0), nw_ref[...], mod_ref[...]).astype(BF16)

    def glu(h, cols):
        v = _wdot(h, wi_ref, cols.start, CONV_CHUNK) + bi_ref[:, cols]
        gcols = slice(D + cols.start, D + cols.stop)
        g = _wdot(h, wi_ref, gcols.start, CONV_CHUNK) + bi_ref[:, gcols]
        return v * _sigmoid(g)

    for ch in range(D // CONV_CHUNK):
        cols = slice(ch * CONV_CHUNK, (ch + 1) * CONV_CHUNK)
        u_main = glu(hb, cols)
        if tiles_per_seq > 1:
            uh = glu(hh, cols)
            u_prev = jnp.where(t > 0, uh[0:H], 0.0)
            u_next = jnp.where(t < tiles_per_seq - 1, uh[H:2 * H], 0.0)
        else:
            u_prev = u_next = jnp.zeros((H, CONV_CHUNK), F32)
        for cb in range(CONV_CHUNK // LANES):
            blk = ch * (CONV_CHUNK // LANES) + cb
            lc = slice(blk * LANES, (blk + 1) * LANES)
            cc = slice(cb * LANES, (cb + 1) * LANES)
            ubuf[blk, 0:H, :] = u_prev[:, cc]
            ubuf[blk, H:H + T, :] = u_main[:, cc]
            ubuf[blk, H + T:H + T + H, :] = u_next[:, cc]
            for rb in range(T // RB):
                acc = jnp.zeros((RB, LANES), F32)
                for k in range(CONF_K):
                    r0 = rb * RB + H - CONF_K // 2 + k
                    acc = acc + ubuf[blk, r0:r0 + RB, :] * wdw_ref[k:k + 1, lc]
                cbuf[rb * RB:(rb + 1) * RB, lc] = acc + bdw_ref[:, lc]

    sz = _silu(_wdot(hb, wi_ref, 2 * D, D) + bi_ref[:, 2 * D:3 * D])
    c = cbuf[...]
    mu = jnp.mean(c, axis=-1, keepdims=True)
    xc = c - mu
    var = jnp.mean(xc * xc, axis=-1, keepdims=True)
    ln = xc * lax.rsqrt(var + LN_EPS) * lnw_ref[...] + lnb_ref[...]
    g = (_silu(ln) * sz).astype(BF16)
    y = _wdot(g, wo_ref, 0, D) + bo_ref[...]
    xn = x_ref[...] + mod_ref[:, 2 * D:3 * D] * y
    if final:
        xn = _rms(xn, fw_ref[...])
    o_ref[...] = xn


def _conv_layer(x, seq_len, layer, row_fn, mod, nw, w_in, b_in, w_dw, b_dw, ln_w, ln_b, w_out, b_out, final_w):
    M = x.shape[0]
    T = TOKEN_TILE
    tps = seq_len // T
    H = CONV_HALO
    n_halo_blocks = M // H
    row = lambda i: (i, 0)
    in_specs = [pl.BlockSpec((T, D), row)]
    args = [x]
    if tps > 1:
        in_specs += [pl.BlockSpec((H, D), lambda i: (jnp.maximum(i * (T // H) - 1, 0), 0)),
                     pl.BlockSpec((H, D), lambda i: (jnp.minimum((i + 1) * (T // H), n_halo_blocks - 1), 0))]
        args += [x, x]
    in_specs += [_const_spec((1, D)), _mod_spec(layer, lambda i: row_fn(i * T)),
                 _wspec(D, 3 * D), _const_spec((1, 3 * D)),
                 _const_spec((CONF_K + 1, D)), _const_spec((1, D)), _const_spec((1, D)), _const_spec((1, D)),
                 _wspec(D, D), _const_spec((1, D))]
    args += [nw, mod, w_in, b_in, w_dw, b_dw, ln_w, ln_b, w_out, b_out]
    final = final_w is not None
    if final:
        in_specs.append(_const_spec((1, D)))
        args.append(final_w)
    return pl.pallas_call(
        functools.partial(_conv_kernel, tps, final),
        grid=(M // T,),
        in_specs=in_specs,
        out_specs=pl.BlockSpec((T, D), row),
        out_shape=jax.ShapeDtypeStruct((M, D), F32),
        scratch_shapes=[pltpu.VMEM((D // LANES, T + 2 * H, LANES), F32), pltpu.VMEM((T, D), F32)],
        compiler_params=_cparams(1),
        name="conv_layer",
    )(*args)


def _prep_conv(p, j):
    w_dw = jnp.pad(p["conv_w_dw"][j], ((0, 1), (0, 0)))
    return (_wsplit(p["conv_w_in"][j]), p["conv_b_in"][j][None], w_dw, p["conv_b_dw"][j][None],
            p["conv_ln_w"][j][None], p["conv_ln_b"][j][None], _wsplit(p["conv_w_out"][j]), p["conv_b_out"][j][None])


def _out_proj_kernel(g_ref, x_ref, mod_ref, w_ref, o_ref):
    y = _wdot(g_ref[...].astype(BF16), w_ref, 0, D)
    o_ref[...] = x_ref[...] + mod_ref[:, 2 * D:3 * D] * y


def _out_proj(g, x, layer, row_fn, mod, w_out):
    M, K = g.shape
    tm = TOKEN_TILE
    row = lambda i: (i, 0)
    return pl.pallas_call(
        _out_proj_kernel,
        grid=(M // tm,),
        in_specs=[pl.BlockSpec((tm, K), row), pl.BlockSpec((tm, D), row), _mod_spec(layer, lambda i: row_fn(i * tm)),
                  _wspec(K, D)],
        out_specs=pl.BlockSpec((tm, D), row),
        out_shape=jax.ShapeDtypeStruct((M, D), F32),
        compiler_params=_cparams(1),
        name="out_proj",
    )(g, x, mod, w_out)


NA_SCALE = NA_HEAD_DIM ** -0.5
NA_ROWS = 32
NA_WIN = NA_KH * GRID_W
NT_DIMS = (((1,), (1,)), ((), ()))


def _na_in_kernel(x_ref, nw_ref, mod_ref, w_ref, q_ref, k_ref, v_ref, sz_ref):
    hb = _modnorm(x_ref[...], nw_ref[...], mod_ref[...]).astype(BF16)
    q_ref[...] = _wdot(hb, w_ref, 0, D).astype(q_ref.dtype)
    k_ref[...] = _wdot(hb, w_ref, D, D).astype(k_ref.dtype)
    v_ref[...] = _wdot(hb, w_ref, 2 * D, D).astype(v_ref.dtype)
    sz_ref[...] = _silu(_wdot(hb, w_ref, 3 * D, D))


def _na_in(x, layer, row_fn, mod, nw, w_in, kv_dtype):
    M = x.shape[0]
    tm = TOKEN_TILE
    row = lambda i: (i, 0)
    blk = pl.BlockSpec((tm, D), row)
    return pl.pallas_call(
        _na_in_kernel,
        grid=(M // tm,),
        in_specs=[blk, _const_spec((1, D)), _mod_spec(layer, lambda i: row_fn(i * tm)), _wspec(D, 4 * D)],
        out_specs=[blk, blk, blk, blk],
        out_shape=[jax.ShapeDtypeStruct((M, D), BF16), jax.ShapeDtypeStruct((M, D), kv_dtype),
                   jax.ShapeDtypeStruct((M, D), kv_dtype), jax.ShapeDtypeStruct((M, D), F32)],
        compiler_params=_cparams(1),
        name="na_in",
    )(x, nw, mod, w_in)


def _head_masks():
    lane = lax.broadcasted_iota(jnp.int32, (1, LANES), 1)
    return lane < NA_HEAD_DIM


def _na_seq_kernel(x_ref, nw_ref, mod_ref, wi_ref, wo_ref, o_ref, k_ref, v_ref, g_ref):
    n = x_ref.shape[0]
    lo = _head_masks()
    x = x_ref[...]
    hb = _modnorm(x, nw_ref[...], mod_ref[...]).astype(BF16)
    hpc = CONV_CHUNK // NA_HEAD_DIM
    for ch in range(D // CONV_CHUNK):
        cols = slice(ch * CONV_CHUNK, (ch + 1) * CONV_CHUNK)
        q = _wdot(hb, wi_ref, cols.start, CONV_CHUNK).astype(BF16)
        k = _wdot(hb, wi_ref, D + cols.start, CONV_CHUNK)
        v = _wdot(hb, wi_ref, 2 * D + cols.start, CONV_CHUNK)
        sz = _silu(_wdot(hb, wi_ref, 3 * D + cols.start, CONV_CHUNK))
        k_ref[:, ch * hpc:(ch + 1) * hpc, :] = k.reshape(n, hpc, NA_HEAD_DIM)
        v_ref[:, ch * hpc:(ch + 1) * hpc, :] = v.reshape(n, hpc, NA_HEAD_DIM)
        kb = k.astype(BF16)
        vb = v.astype(BF16)
        for pp in range(CONV_CHUNK // LANES):
            pc = slice(pp * LANES, (pp + 1) * LANES)
            qp = q[:, pc]
            zero = jnp.zeros_like(qp)
            qs = jnp.concatenate([jnp.where(lo, qp, zero), jnp.where(lo, zero, qp)], axis=0)
            s = lax.dot_general(qs, kb[:, pc], NT_DIMS, preferred_element_type=F32) * NA_SCALE
            e = jnp.exp(s - jnp.max(s, axis=-1, keepdims=True))
            l = jnp.sum(e, axis=-1, keepdims=True)
            o = jnp.dot(e.astype(BF16), vb[:, pc], preferred_element_type=F32) / l
            g = jnp.where(lo, o[:n], o[n:]) * sz[:, pc]
            g_ref[:, cols.start + pp * LANES:cols.start + (pp + 1) * LANES] = g.astype(BF16)
    y = _wdot(g_ref[...], wo_ref, 0, D)
    o_ref[...] = x + mod_ref[:, 2 * D:3 * D] * y


def _na_seq(x, seq_len, layer, row_fn, mod, nw, w_in, w_out):
    M = x.shape[0]
    row = lambda i: (i, 0)
    kv_blk = pl.BlockSpec((seq_len, NA_HEADS, NA_HEAD_DIM), lambda i: (i, 0, 0))
    kv_shape = jax.ShapeDtypeStruct((M, NA_HEADS, NA_HEAD_DIM), F32)
    return pl.pallas_call(
        _na_seq_kernel,
        grid=(M // seq_len,),
        in_specs=[pl.BlockSpec((seq_len, D), row), _const_spec((1, D)), _mod_spec(layer, lambda i: row_fn(i * seq_len)),
                  _wspec(D, 4 * D), _wspec(D, D)],
        out_specs=[pl.BlockSpec((seq_len, D), row), kv_blk, kv_blk],
        out_shape=[jax.ShapeDtypeStruct((M, D), F32), kv_shape, kv_shape],
        scratch_shapes=[pltpu.VMEM((seq_len, D), BF16)],
        compiler_params=_cparams(1),
        name="na_seq",
    )(x, nw, mod, w_in, w_out)


def _rpb_kernel(rpb_ref, o_ref):
    h = pl.program_id(0)
    c = lax.broadcasted_iota(jnp.int32, (GRID_W, LANES), 0)
    lane = lax.broadcasted_iota(jnp.int32, (GRID_W, LANES), 1)
    kc = lane & (GRID_W - 1)
    upper = lax.broadcasted_iota(jnp.int32, (1, LANES), 1) >= GRID_W
    cs = jnp.clip(c - NA_KW // 2, 0, GRID_W - NA_KW)
    valid = (kc >= cs) & (kc < cs + NA_KW)
    n_pair = 2 * NA_KH - 2

    def toeplitz(dri, lane0):
        row = rpb_ref[pl.ds(h * (2 * NA_KH - 1) + dri, 1), :]
        rows = jnp.broadcast_to(row, (GRID_W, LANES))
        return pltpu.roll(rows, (LANES - (NA_KW - 1) + lane0) % LANES, 1, stride=1, stride_axis=0)

    for dri in range(n_pair):
        tile = jnp.where(upper, toeplitz(dri + 1, GRID_W), toeplitz(dri, 0))
        tile = jnp.where(valid, tile, NEG_INF)
        for idx in range(NA_KH):
            jj2 = dri - (NA_KH - 1) + idx
            if jj2 % 2 == 0 and 0 <= jj2 // 2 < NA_KH // 2:
                jj = jj2 // 2
                o_ref[idx, :, jj * LANES:(jj + 1) * LANES] = tile


def _rpb_table(rpb):
    n_rows = NA_HEADS * (2 * NA_KH - 1)
    rows = jnp.pad(rpb.reshape(n_rows, 2 * NA_KW - 1), ((0, 0), (0, LANES - (2 * NA_KW - 1))))
    return pl.pallas_call(
        _rpb_kernel,
        grid=(NA_HEADS,),
        in_specs=[_const_spec((n_rows, LANES))],
        out_specs=pl.BlockSpec((NA_KH, None, GRID_W, NA_WIN), lambda h: (0, h, 0, 0)),
        out_shape=jax.ShapeDtypeStruct((NA_KH, NA_HEADS, GRID_W, NA_WIN), F32),
        compiler_params=_cparams(1),
        name="na_rpb_table",
    )(rows)


def _na_row_start(r):
    return jnp.clip(r - NA_KH // 2, 0, NA_ROWS - NA_KH)


def _na_lat_kernel(q_ref, k_ref, v_ref, ck_ref, cv_ref, bias_ref, sz_ref, o_ref):
    lo = _head_masks()
    k0 = pl.multiple_of(_na_row_start(pl.program_id(1)) * GRID_W, GRID_W)
    for p in range(NA_HEADS // 2):
        cols = slice(p * LANES, (p + 1) * LANES)
        qp = q_ref[:, cols]
        kw = k_ref[pl.ds(k0, NA_WIN), cols]
        vw = v_ref[pl.ds(k0, NA_WIN), cols]
        ck = ck_ref[:, cols]
        cv = cv_ref[:, cols]
        zero = jnp.zeros_like(qp)
        qs = jnp.concatenate([jnp.where(lo, qp, zero), jnp.where(lo, zero, qp)], axis=0)
        bias = bias_ref[2 * p:2 * p + 2].reshape(2 * GRID_W, NA_WIN)
        s_loc = lax.dot_general(qs, kw, NT_DIMS, preferred_element_type=F32) * NA_SCALE + bias
        s_ctx = lax.dot_general(qs, ck, NT_DIMS, preferred_element_type=F32) * NA_SCALE
        m = jnp.maximum(jnp.max(s_loc, axis=-1, keepdims=True), jnp.max(s_ctx, axis=-1, keepdims=True))
        e_loc = jnp.exp(s_loc - m)
        e_ctx = jnp.exp(s_ctx - m)
        l = jnp.sum(e_loc, axis=-1, keepdims=True) + jnp.sum(e_ctx, axis=-1, keepdims=True)
        o = (jnp.dot(e_loc.astype(BF16), vw, preferred_element_type=F32)
             + jnp.dot(e_ctx.astype(BF16), cv, preferred_element_type=F32)) / l
        o_ref[:, cols] = jnp.where(lo, o[:GRID_W], o[GRID_W:]) * sz_ref[:, cols]


def _na_lat(q, k, v, ck, cv, bias, sz, n_lat):
    T = NA_ROWS * GRID_W
    past = ck.shape[1]
    qblk = pl.BlockSpec((GRID_W, D), lambda b, r: (b * NA_ROWS + r, 0))
    kvblk = pl.BlockSpec((T, D), lambda b, r: (b, 0))
    cblk = pl.BlockSpec((None, past, D), lambda b, r: (b, 0, 0))
    bblk = pl.BlockSpec((None, NA_HEADS, GRID_W, NA_WIN), lambda b, r: (r - _na_row_start(r), 0, 0, 0))
    return pl.pallas_call(
        _na_lat_kernel,
        grid=(n_lat, NA_ROWS),
        in_specs=[qblk, kvblk, kvblk, cblk, cblk, bblk, qblk],
        out_specs=qblk,
        out_shape=jax.ShapeDtypeStruct((n_lat * T, D), F32),
        compiler_params=_cparams(2),
        name="na_lat_attn",
    )(q, k, v, ck, cv, bias, sz)


def _na_layer(xc, xl, layer, mod, nw, p, n_lat):
    j = layer // 3
    w_in = _wsplit(p["na_w_in"][j])
    w_out = _wsplit(p["na_w_out"][j])
    row_ctx = lambda tok: 0
    row_lat = lambda tok: 1 + tok // (NA_ROWS * GRID_W)
    oc, kc, vc = _na_seq(xc, SEQ, layer, row_ctx, mod, nw, w_in, w_out)
    ql, kl, vl, szl = _na_in(xl, layer, row_lat, mod, nw, w_in, BF16)
    past = p["cache_k"].shape[2]
    ck = p["cache_k"][:n_lat, j].reshape(n_lat, past, D).astype(BF16)
    cv = p["cache_v"][:n_lat, j].reshape(n_lat, past, D).astype(BF16)
    gl = _na_lat(ql, kl, vl, ck, cv, _rpb_table(p["na_rpb"][j]), szl, n_lat)
    return oc, _out_proj(gl, xl, layer, row_lat, mod, w_out), kc, vc


SSD_IN_COLS = SSD_INNER + SSD_XBC


def _softplus(x):
    return jnp.maximum(x, 0.0) + jnp.log(1.0 + jnp.exp(-jnp.abs(x)))


def _ssd_in_kernel(tiles_per_seq, *refs):
    refs = list(refs)
    x_ref = refs.pop(0)
    if tiles_per_seq > 1:
        xp_ref = refs.pop(0)
        xn_ref = refs.pop(0)
    nw_ref, mod_ref, wz_ref, wx_ref, wdt_ref, dtb_ref, wc_ref, bc_ref, sz_ref, xbc_ref, dt_ref, xpad_ref = refs
    hb = _modnorm(x_ref[...], nw_ref[...], mod_ref[...]).astype(BF16)
    halo = None
    if tiles_per_seq > 1:
        t = pl.program_id(0) % tiles_per_seq
        xh = jnp.concatenate([xp_ref[...], xn_ref[...]], axis=0)
        halo = (_modnorm(xh, nw_ref[...], mod_ref[...]).astype(BF16), t > 0, t < tiles_per_seq - 1)
    dt_ref[...] = _ssd_project(hb, halo, wz_ref, wx_ref, wdt_ref, dtb_ref, wc_ref, bc_ref, xpad_ref, sz_ref, xbc_ref)


def _ssd_project(hb, halo, wz_ref, wx_ref, wdt_ref, dtb_ref, wc_ref, bc_ref, xpad_ref, sz_ref, xbc_ref):
    tm = hb.shape[0]
    HL = SSD_HALO
    RB = 128
    for j in range(SSD_XBC // D):
        c0 = j * D
        raw = _wdot(hb, wx_ref, c0, D)
        if halo is not None:
            rawh = _wdot(halo[0], wx_ref, c0, D)
            raw_prev = jnp.where(halo[1], rawh[0:HL], 0.0)
            raw_next = jnp.where(halo[2], rawh[HL:2 * HL], 0.0)
        else:
            raw_prev = raw_next = jnp.zeros((HL, D), F32)
        for cb in range(D // LANES):
            blk = j * (D // LANES) + cb
            lc = slice(blk * LANES, (blk + 1) * LANES)
            cc = slice(cb * LANES, (cb + 1) * LANES)
            xpad_ref[blk, 0:HL, :] = raw_prev[:, cc]
            xpad_ref[blk, HL:HL + tm, :] = raw[:, cc]
            xpad_ref[blk, HL + tm:HL + tm + HL, :] = raw_next[:, cc]
            for rb in range(tm // RB):
                acc = jnp.zeros((RB, LANES), F32)
                for k in range(SSD_CONV):
                    r0 = rb * RB + HL - SSD_CONV // 2 + k
                    acc = acc + xpad_ref[blk, r0:r0 + RB, :] * wc_ref[k:k + 1, lc]
                xbc_ref[rb * RB:(rb + 1) * RB, lc] = _silu(acc + bc_ref[:, lc])
    for j in range(SSD_INNER // D):
        sz_ref[:, j * D:(j + 1) * D] = _silu(_wdot(hb, wz_ref, j * D, D))
    raw = jnp.dot(hb, wdt_ref[...], preferred_element_type=F32) + dtb_ref[...]
    lane = lax.broadcasted_iota(jnp.int32, (1, LANES), 1)
    return jnp.where(lane < 2 * SSD_HEADS, _softplus(raw), 0.0)


def _ssd_in(x, seq_len, layer, row_fn, mod, nw, w_z, w_x, w_dt, dt_bias, w_conv, b_conv):
    M = x.shape[0]
    tm = TOKEN_TILE
    HL = SSD_HALO
    tps = seq_len // tm
    n_hblk = M // HL
    row = lambda i: (i, 0)
    in_specs = [pl.BlockSpec((tm, D), row)]
    args = [x]
    if tps > 1:
        in_specs += [pl.BlockSpec((HL, D), lambda i: (jnp.maximum(i * (tm // HL) - 1, 0), 0)),
                     pl.BlockSpec((HL, D), lambda i: (jnp.minimum((i + 1) * (tm // HL), n_hblk - 1), 0))]
        args += [x, x]
    in_specs += [_const_spec((1, D)), _mod_spec(layer, lambda i: row_fn(i * tm)),
                 _wspec(D, SSD_INNER), _wspec(D, SSD_XBC), _const_spec((D, LANES)), _const_spec((1, LANES)),
                 _const_spec((SSD_CONV + 1, SSD_XBC)), _const_spec((1, SSD_XBC))]
    args += [nw, mod, w_z, w_x, w_dt, dt_bias, w_conv, b_conv]
    return pl.pallas_call(
        functools.partial(_ssd_in_kernel, tps),
        grid=(M // tm,),
        in_specs=in_specs,
        out_specs=[pl.BlockSpec((tm, SSD_INNER), row), pl.BlockSpec((tm, SSD_XBC), row), pl.BlockSpec((tm, LANES), row)],
        out_shape=[jax.ShapeDtypeStruct((M, SSD_INNER), F32), jax.ShapeDtypeStruct((M, SSD_XBC), F32),
                   jax.ShapeDtypeStruct((M, LANES), F32)],
        scratch_shapes=[pltpu.VMEM((SSD_XBC // LANES, tm + 2 * HL, LANES), F32)],
        compiler_params=_cparams(1),
        name="ssd_in",
    )(*args)


def _cumsum_rows(a, reverse):
    n = a.shape[0]
    row = lax.broadcasted_iota(jnp.int32, a.shape, 0)
    k = 1
    while k < n:
        if reverse:
            a = a + jnp.where(row < n - k, pltpu.roll(a, n - k, 0), 0.0)
        else:
            a = a + jnp.where(row >= k, pltpu.roll(a, k, 0), 0.0)
        k *= 2
    return a


def _ssd_core_kernel(nc, has_h0, emit_state, *refs):
    refs = list(refs)
    dir_refs = [refs[0:2], refs[2:4]]
    alog_ref, dsk_ref = refs[4:6]
    refs = refs[6:]
    if has_h0:
        h0_ref = refs.pop(0)
    y_refs = [refs.pop(0), refs.pop(0)]
    if emit_state:
        so_ref = refs.pop(0)
    (st_ref,) = refs
    Q, N, P = SSD_CHUNK, SSD_STATE, SSD_HEAD_DIM
    c = pl.program_id(1)
    n_blk = SSD_INNER // LANES

    @pl.when(c == 0)
    def _init():
        if has_h0:
            for dr in range(2):
                for j in range(n_blk):
                    st_ref[dr, :, j * LANES:(j + 1) * LANES] = h0_ref[dr, j * LANES:(j + 1) * LANES, :].T
        else:
            st_ref[...] = jnp.zeros(st_ref.shape, F32)

    A = -jnp.exp(alog_ref[...])
    for dr in range(2):
        x_ref, dt_ref = dir_refs[dr]
        y_ref = y_refs[dr]

        def x_at(c0, x_ref=x_ref):
            return x_ref[:, c0:c0 + LANES]

        def y_put(c0, y, y_ref=y_ref):
            y_ref[:, c0:c0 + LANES] = y

        _ssd_chunk(dr, x_at, dt_ref[...], A, st_ref, y_put, dsk_ref)

    if emit_state:
        @pl.when(c == nc - 1)
        def _emit():
            _ssd_emit_state(st_ref, so_ref)


def _ssd_emit_state(st_ref, so_ref):
    for dr in range(2):
        for j in range(SSD_INNER // LANES):
            so_ref[dr, j * LANES:(j + 1) * LANES, :] = st_ref[dr, :, j * LANES:(j + 1) * LANES].T


def _ssd_chunk(dr, x_at, dt, A, st_ref, y_put, dsk_ref):
    Q, N, P = SSD_CHUNK, SSD_STATE, SSD_HEAD_DIM
    lo = _head_masks()
    ri = lax.broadcasted_iota(jnp.int32, (Q, Q), 0)
    ci = lax.broadcasted_iota(jnp.int32, (Q, Q), 1)
    cum = _cumsum_rows(dt * A, reverse=(dr == 1)) * LOG2E
    cum_t = cum.T
    dt_t = dt.T
    tot_row = Q - 1 if dr == 0 else 0
    mask = (ri >= ci) if dr == 0 else (ri <= ci)
    for g in range(SSD_GROUPS):
        bg_t = x_at(SSD_INNER + g * N).T
        cg = x_at(SSD_INNER + SSD_GN + g * N).astype(BF16)
        cb = jnp.dot(cg, bg_t.astype(BF16), preferred_element_type=F32)
        for pp in range(SSD_HEADS // SSD_GROUPS // 2):
            head = g * (SSD_HEADS // SSD_GROUPS) + 2 * pp
            c0 = head * P
            li = dr * SSD_HEADS + head
            xpair = x_at(c0)
            xpair_b = xpair.astype(BF16)
            cum_cols = [cum[:, li:li + 1], cum[:, li + 1:li + 2]]
            cum_x = jnp.where(lo, cum_cols[0], cum_cols[1])
            yd, sd = [], []
            for hh in range(2):
                dt_row = dt_t[li + hh:li + hh + 1, :]
                cum_row = cum_t[li + hh:li + hh + 1, :]
                decay = jnp.exp2(jnp.where(mask, cum_cols[hh] - cum_row, -jnp.inf))
                yd.append(jnp.dot((cb * decay * dt_row).astype(BF16), xpair_b, preferred_element_type=F32))
                to_end = jnp.exp2(cum_row[:, tot_row:tot_row + 1] - cum_row)
                sd.append(jnp.dot((bg_t * (dt_row * to_end)).astype(BF16), xpair_b, preferred_element_type=F32))
            st = st_ref[dr, :, c0:c0 + LANES]
            y = (jnp.where(lo, yd[0], yd[1])
                 + jnp.dot(cg, st.astype(BF16), preferred_element_type=F32) * jnp.exp2(cum_x))
            if dr == 0:
                y = y + xpair * dsk_ref[:, c0:c0 + LANES]
            y_put(c0, y)
            tot = cum_x[tot_row:tot_row + 1, :]
            st_ref[dr, :, c0:c0 + LANES] = st * jnp.exp2(tot) + jnp.where(lo, sd[0], sd[1])


def _ssd_seq_kernel(x_ref, nw_ref, mod_ref, wz_ref, wx_ref, wdt_ref, dtb_ref, wc_ref, bc_ref, alog_ref, dsk_ref, gnw_ref, wo_ref,
                    o_ref, so_ref, xpad_ref, xbc_ref, sz_ref, y_ref, st_ref):
    Q = SSD_CHUNK
    nc = x_ref.shape[0] // Q
    x = x_ref[...]
    hb = _modnorm(x, nw_ref[...], mod_ref[...]).astype(BF16)
    dt = _ssd_project(hb, None, wz_ref, wx_ref, wdt_ref, dtb_ref, wc_ref, bc_ref, xpad_ref, sz_ref, xbc_ref)
    st_ref[...] = jnp.zeros(st_ref.shape, F32)
    A = -jnp.exp(alog_ref[...])
    for dr in range(2):
        for ck in (range(nc) if dr == 0 else reversed(range(nc))):
            rows = slice(ck * Q, (ck + 1) * Q)

            def x_at(c0, rows=rows):
                return xbc_ref[rows, c0:c0 + LANES]

            def y_put(c0, y, rows=rows, dr=dr):
                if dr == 0:
                    y_ref[rows, c0:c0 + LANES] = y
                else:
                    y_ref[rows, c0:c0 + LANES] += y

            _ssd_chunk(dr, x_at, dt[rows], A, st_ref, y_put, dsk_ref)
    _ssd_emit_state(st_ref, so_ref)
    g = _rms(y_ref[...] * sz_ref[...], gnw_ref[...]).astype(BF16)
    o_ref[...] = x + mod_ref[:, 2 * D:3 * D] * _wdot(g, wo_ref, 0, D)


def _ssd_seq(x, seq_len, layer, row_fn, mod, nw, w_z, w_x, w_dt, dt_bias, w_conv, b_conv, a_log, d_skip, norm_w, w_out):
    M = x.shape[0]
    n_seq = M // seq_len
    HL = SSD_HALO
    row = lambda i: (i, 0)
    st_blk = pl.BlockSpec((None, 2, SSD_INNER, SSD_STATE), lambda i: (i, 0, 0, 0))
    return pl.pallas_call(
        _ssd_seq_kernel,
        grid=(n_seq,),
        in_specs=[pl.BlockSpec((seq_len, D), row), _const_spec((1, D)), _mod_spec(layer, lambda i: row_fn(i * seq_len)),
                  _wspec(D, SSD_INNER), _wspec(D, SSD_XBC), _const_spec((D, LANES)), _const_spec((1, LANES)),
                  _const_spec((SSD_CONV + 1, SSD_XBC)), _const_spec((1, SSD_XBC)),
                  _const_spec((1, LANES)), _const_spec((1, SSD_INNER)), _const_spec((1, SSD_INNER)),
                  _wspec(SSD_INNER, D)],
        out_specs=[pl.BlockSpec((seq_len, D), row), st_blk],
        out_shape=[jax.ShapeDtypeStruct((M, D), F32), jax.ShapeDtypeStruct((n_seq, 2, SSD_INNER, SSD_STATE), F32)],
        scratch_shapes=[pltpu.VMEM((SSD_XBC // LANES, seq_len + 2 * HL, LANES), F32),
                        pltpu.VMEM((seq_len, SSD_XBC), F32), pltpu.VMEM((seq_len, SSD_INNER), F32),
                        pltpu.VMEM((seq_len, SSD_INNER), F32), pltpu.VMEM((2, SSD_STATE, SSD_INNER), F32)],
        compiler_params=_cparams(1),
        name="ssd_seq",
    )(x, nw, mod, w_z, w_x, w_dt, dt_bias, w_conv, b_conv, a_log, d_skip, norm_w, w_out)


def _ssd_core(xbc, dtp, a_log, d_skip, h0, n_seq, seq_len, emit_state):
    Q = SSD_CHUNK
    nc = seq_len // Q
    M = n_seq * seq_len

    def chunk(dr):
        return (lambda b, c: b * nc + c) if dr == 0 else (lambda b, c: b * nc + nc - 1 - c)

    in_specs, args = [], []
    for dr in range(2):
        ch = chunk(dr)
        in_specs += [pl.BlockSpec((Q, SSD_XBC), lambda b, c, ch=ch: (ch(b, c), 0)),
                     pl.BlockSpec((Q, LANES), lambda b, c, ch=ch: (ch(b, c), 0))]
        args += [xbc, dtp]
    in_specs += [_const_spec((1, LANES)), _const_spec((1, SSD_INNER))]
    args += [a_log, d_skip]
    st_blk = pl.BlockSpec((None, 2, SSD_INNER, SSD_STATE), lambda b, c: (b, 0, 0, 0))
    has_h0 = h0 is not None
    if has_h0:
        in_specs.append(st_blk)
        args.append(h0)
    out_specs = [pl.BlockSpec((Q, SSD_INNER), lambda b, c: (chunk(0)(b, c), 0)),
                 pl.BlockSpec((Q, SSD_INNER), lambda b, c: (chunk(1)(b, c), 0))]
    out_shape = [jax.ShapeDtypeStruct((M, SSD_INNER), F32)] * 2
    if emit_state:
        out_specs.append(st_blk)
        out_shape.append(jax.ShapeDtypeStruct((n_seq, 2, SSD_INNER, SSD_STATE), F32))
    return pl.pallas_call(
        functools.partial(_ssd_core_kernel, nc, has_h0, emit_state),
        grid=(n_seq, nc),
        in_specs=in_specs,
        out_specs=out_specs,
        out_shape=out_shape,
        scratch_shapes=[pltpu.VMEM((2, SSD_STATE, SSD_INNER), F32)],
        compiler_params=_cparams(2),
        name="ssd_core",
    )(*args)


def _ssd_out_kernel(yf_ref, yb_ref, sz_ref, x_ref, mod_ref, nw_ref, w_ref, o_ref):
    g = _rms((yf_ref[...] + yb_ref[...]) * sz_ref[...], nw_ref[...]).astype(BF16)
    o_ref[...] = x_ref[...] + mod_ref[:, 2 * D:3 * D] * _wdot(g, w_ref, 0, D)


def _ssd_out(yf, yb, sz, x, layer, row_fn, mod, norm_w, w_out):
    M = x.shape[0]
    tm = TOKEN_TILE
    row = lambda i: (i, 0)
    wide = pl.BlockSpec((tm, SSD_INNER), row)
    return pl.pallas_call(
        _ssd_out_kernel,
        grid=(M // tm,),
        in_specs=[wide, wide, wide, pl.BlockSpec((tm, D), row), _mod_spec(layer, lambda i: row_fn(i * tm)),
                  _const_spec((1, SSD_INNER)), _wspec(SSD_INNER, D)],
        out_specs=pl.BlockSpec((tm, D), row),
        out_shape=jax.ShapeDtypeStruct((M, D), F32),
        compiler_params=_cparams(1),
        name="ssd_out",
    )(yf, yb, sz, x, mod, norm_w, w_out)


def _ssd_layer(xc, xl, layer, mod, nw, p, n_lat):
    j = layer // 3
    w_z = _wsplit(p["ssd_w_in"][j][:, :SSD_INNER])
    w_x = _wsplit(p["ssd_w_in"][j][:, SSD_INNER:SSD_IN_COLS])
    w_dt = jnp.pad(p["ssd_w_in"][j][:, SSD_IN_COLS:], ((0, 0), (0, LANES - 2 * SSD_HEADS))).astype(BF16)
    dt_bias = jnp.pad(p["ssd_dt_bias"][j].reshape(1, -1), ((0, 0), (0, LANES - 2 * SSD_HEADS)))
    a_log = jnp.pad(p["ssd_a_log"][j].reshape(1, -1), ((0, 0), (0, LANES - 2 * SSD_HEADS)))
    w_conv = jnp.pad(p["ssd_w_conv"][j], ((0, 1), (0, 0)))
    b_conv = p["ssd_b_conv"][j][None]
    d_skip = jnp.repeat(p["ssd_d"][j], SSD_HEAD_DIM)[None]
    norm_w = p["ssd_norm_w"][j][None]
    w_out = _wsplit(p["ssd_w_out"][j])
    row_ctx = lambda tok: 0
    row_lat = lambda tok: 1 + tok // DEC_SEQ
    oc, s_new = _ssd_seq(xc, SEQ, layer, row_ctx, mod, nw, w_z, w_x, w_dt, dt_bias, w_conv, b_conv, a_log, d_skip, norm_w, w_out)
    szl, xbc_l, dt_l = _ssd_in(xl, DEC_SEQ, layer, row_lat, mod, nw, w_z, w_x, w_dt, dt_bias, w_conv, b_conv)
    h0 = p["state_ssm"][:n_lat, j].reshape(n_lat, 2, SSD_INNER, SSD_STATE)
    yfl, ybl = _ssd_core(xbc_l, dt_l, a_log, d_skip, h0, n_lat, DEC_SEQ, False)
    ol = _ssd_out(yfl, ybl, szl, xl, layer, row_lat, mod, norm_w, w_out)
    return oc, ol, s_new


def kernel(x_prompt, x_sample, cache_k, cache_v, state_ssm, c, c_ctx, ada_w, ada_b, norm_w, final_norm_w, conv_w_in, conv_b_in, conv_w_dw, conv_b_dw, conv_ln_w, conv_ln_b, conv_w_out, conv_b_out, na_w_in, na_rpb, na_w_out, ssd_w_in, ssd_w_conv, ssd_b_conv, ssd_a_log, ssd_dt_bias, ssd_d, ssd_norm_w, ssd_w_out):
    p = dict(cache_k=cache_k, cache_v=cache_v, state_ssm=state_ssm,
             conv_w_in=conv_w_in, conv_b_in=conv_b_in, conv_w_dw=conv_w_dw, conv_b_dw=conv_b_dw,
             conv_ln_w=conv_ln_w, conv_ln_b=conv_ln_b, conv_w_out=conv_w_out, conv_b_out=conv_b_out,
             na_w_in=na_w_in, na_rpb=na_rpb, na_w_out=na_w_out,
             ssd_w_in=ssd_w_in, ssd_w_conv=ssd_w_conv, ssd_b_conv=ssd_b_conv, ssd_a_log=ssd_a_log,
             ssd_dt_bias=ssd_dt_bias, ssd_d=ssd_d, ssd_norm_w=ssd_norm_w, ssd_w_out=ssd_w_out)
    n_ctx, n_lat = x_prompt.shape[0], x_sample.shape[0]
    assert x_prompt.shape[1:] == (SEQ, D) and x_sample.shape[1:] == (DEC_SEQ, D)
    assert (DEPTH - 1) % 3 == 0, "the final RMSNorm is fused into a trailing Conformer layer"
    xc = x_prompt.reshape(n_ctx * SEQ, D)
    xl = x_sample.reshape(n_lat * DEC_SEQ, D)
    cond8 = jnp.concatenate([c_ctx[None], c, jnp.zeros((8 - 1 - n_lat, D), F32)], axis=0)
    mod = _modulation(cond8, ada_w, ada_b).reshape(DEPTH, 8, 1, 3 * D)
    row_ctx = lambda tok: 0
    row_lat = lambda tok: 1 + tok // DEC_SEQ
    new_k, new_v, new_s = [], [], []
    for i in range(DEPTH):
        kind, j = i % 3, i // 3
        nw = norm_w[i][None]
        if kind == 0:
            cp = _prep_conv(p, j)
            fw = final_norm_w[None] if i == DEPTH - 1 else None
            xc = _conv_layer(xc, SEQ, i, row_ctx, mod, nw, *cp, fw)
            xl = _conv_layer(xl, DEC_SEQ, i, row_lat, mod, nw, *cp, fw)
        elif kind == 1:
            xc, xl, k_c, v_c = _na_layer(xc, xl, i, mod, nw, p, n_lat)
            new_k.append(k_c.reshape(n_ctx, SEQ, NA_HEADS, NA_HEAD_DIM))
            new_v.append(v_c.reshape(n_ctx, SEQ, NA_HEADS, NA_HEAD_DIM))
        else:
            xc, xl, s_c = _ssd_layer(xc, xl, i, mod, nw, p, n_lat)
            new_s.append(s_c.reshape(n_ctx, 2, SSD_HEADS, SSD_HEAD_DIM, SSD_STATE))
    return (xc.reshape(n_ctx, SEQ, D), xl.reshape(n_lat, DEC_SEQ, D),
            jnp.stack(new_k, axis=1), jnp.stack(new_v, axis=1), jnp.stack(new_s, axis=1))
```

```python
import functools

import jax
import jax.numpy as jnp
from jax import lax
from jax.experimental import pallas as pl
from jax.experimental.pallas import tpu as pltpu

F32 = jnp.float32
BF16 = jnp.bfloat16

D = 1024
DEPTH = 4
SEQ = 256
DEC_SEQ = 2048
GRID_W = 64
RMS_EPS = 1e-6
LN_EPS = 1e-5
LOG2E = 1.4426950408889634
CONF_K = 31
NA_HEADS = 16
NA_HEAD_DIM = 64
NA_KH = 8
NA_KW = 16
NEG_INF = -1e30
SSD_INNER = 2048
SSD_HEADS = 32
SSD_HEAD_DIM = 64
SSD_STATE = 128
SSD_GROUPS = 4
SSD_GN = SSD_GROUPS * SSD_STATE
SSD_CONV = 7
SSD_CHUNK = 128
SSD_XBC = SSD_INNER + 2 * SSD_GN

LANES = 128
SUBLANES = 8
TOKEN_TILE = 256
W_CHUNK = 512
CONV_HALO = 16
SSD_HALO = 8
VMEM_LIMIT = 56 * 1024 * 1024


def _cparams(n_axes):
    return pltpu.CompilerParams(dimension_semantics=("arbitrary",) * n_axes,
                                vmem_limit_bytes=VMEM_LIMIT)


def _const_spec(shape):
    nd = len(shape)
    return pl.BlockSpec(shape, lambda *_: (0,) * nd, pipeline_mode=pl.Buffered(1))


def _wsplit_kernel(transposed, w_ref, o_ref):
    w = w_ref[...]
    o_ref[...] = (w.T if transposed else w).astype(BF16)


def _wsplit(w, layer, transposed=False, col0=0, ncols=None):
    K, N = (w.shape[2], w.shape[1]) if transposed else (w.shape[1], w.shape[2])
    ncols = N if ncols is None else ncols
    b0 = col0 // W_CHUNK
    if transposed:
        in_spec = pl.BlockSpec((None, W_CHUNK, K), lambda j: (layer, j + b0, 0))
    else:
        in_spec = pl.BlockSpec((None, K, W_CHUNK), lambda j: (layer, 0, j + b0))
    return pl.pallas_call(
        functools.partial(_wsplit_kernel, transposed),
        grid=(ncols // W_CHUNK,),
        in_specs=[in_spec],
        out_specs=pl.BlockSpec((None, K, W_CHUNK), lambda j: (j, 0, 0)),
        out_shape=jax.ShapeDtypeStruct((ncols // W_CHUNK, K, W_CHUNK), BF16),
        compiler_params=_cparams(1),
        name="weight_slabs",
    )(w)


def _wspec(K, N):
    return _const_spec((N // W_CHUNK, K, W_CHUNK))


def _wdot(h, w_ref, c0, n):
    outs = []
    c = c0
    while c < c0 + n:
        j, off = divmod(c, W_CHUNK)
        m = min(W_CHUNK - off, c0 + n - c)
        outs.append(jnp.dot(h, w_ref[j, :, off:off + m], preferred_element_type=F32))
        c += m
    return outs[0] if len(outs) == 1 else jnp.concatenate(outs, axis=1)


def _sigmoid(x):
    return 1.0 / (1.0 + jnp.exp(-x))


def _silu(x):
    return x * _sigmoid(x)


def _rms(x, w):
    ms = jnp.mean(x * x, axis=-1, keepdims=True)
    return x * lax.rsqrt(ms + RMS_EPS) * w


def _modnorm(x, nw, mod):
    return _rms(x, nw) * (1.0 + mod[:, D:2 * D]) + mod[:, :D]


def _mod_spec(layer, row_fn):
    return pl.BlockSpec((None, None, 1, 3 * D), lambda i, *_: (layer, row_fn(i), 0, 0))


def _mod_kernel(cond_ref, w_ref, b_ref, o_ref):
    s = _silu(cond_ref[...]).astype(BF16)
    o_ref[...] = jnp.dot(s, w_ref[...].astype(BF16), preferred_element_type=F32) + b_ref[...]


def _modulation(cond8, ada_w, ada_b):
    tn = 1024
    return pl.pallas_call(
        _mod_kernel,
        grid=(DEPTH, 3 * D // tn),
        in_specs=[pl.BlockSpec((8, D), lambda i, j: (0, 0)),
                  pl.BlockSpec((None, D, tn), lambda i, j: (i, 0, j)),
                  pl.BlockSpec((None, 1, tn), lambda i, j: (i, 0, j))],
        out_specs=pl.BlockSpec((None, 8, tn), lambda i, j: (i, 0, j)),
        out_shape=jax.ShapeDtypeStruct((DEPTH, 8, 3 * D), F32),
        compiler_params=_cparams(2),
        name="adaln_mod",
    )(cond8, ada_w, ada_b.reshape(DEPTH, 1, 3 * D))


CONV_CHUNK = 256


def _conv_kernel(tiles_per_seq, final, *refs):
    refs = list(refs)
    x_ref = refs.pop(0)
    if tiles_per_seq > 1:
        xp_ref = refs.pop(0)
        xn_ref = refs.pop(0)
    nw_ref, mod_ref, wi_ref, bi_ref, wdw_ref, bdw_ref, lnw_ref, lnb_ref, wo_ref, bo_ref = refs[:10]
    refs = refs[10:]
    if final:
        fw_ref = refs.pop(0)
    o_ref, ubuf, cbuf = refs
    T = x_ref.shape[0]
    H = CONV_HALO
    RB = 128

    hb = _modnorm(x_ref[...], nw_ref[...], mod_ref[...]).astype(BF16)
    if tiles_per_seq > 1:
        t = pl.program_id(0) % tiles_per_seq
        hh = _modnorm(jnp.concatenate([xp_ref[...], xn_ref[...]], axis=0), nw_ref[...], mod_ref[...]).astype(BF16)

    def glu(h, cols):
        v = _wdot(h, wi_ref, cols.start, CONV_CHUNK) + bi_ref[:, cols]
        gcols = slice(D + cols.start, D + cols.stop)
        g = _wdot(h, wi_ref, gcols.start, CONV_CHUNK) + bi_ref[:, gcols]
        return v * _sigmoid(g)

    for ch in range(D // CONV_CHUNK):
        cols = slice(ch * CONV_CHUNK, (ch + 1) * CONV_CHUNK)
        u_main = glu(hb, cols)
        if tiles_per_seq > 1:
            uh = glu(hh, cols)
            u_prev = jnp.where(t > 0, uh[0:H], 0.0)
            u_next = jnp.where(t < tiles_per_seq - 1, uh[H:2 * H], 0.0)
        else:
            u_prev = u_next = jnp.zeros((H, CONV_CHUNK), F32)
        for cb in range(CONV_CHUNK // LANES):
            blk = ch * (CONV_CHUNK // LANES) + cb
            lc = slice(blk * LANES, (blk + 1) * LANES)
            cc = slice(cb * LANES, (cb + 1) * LANES)
            ubuf[blk, 0:H, :] = u_prev[:, cc]
            ubuf[blk, H:H + T, :] = u_main[:, cc]
            ubuf[blk, H + T:H + T + H, :] = u_next[:, cc]
            for rb in range(T // RB):
                acc = jnp.zeros((RB, LANES), F32)
                for k in range(CONF_K):
                    r0 = rb * RB + H - CONF_K // 2 + k
                    acc = acc + ubuf[blk, r0:r0 + RB, :] * wdw_ref[k:k + 1, lc]
                cbuf[rb * RB:(rb + 1) * RB, lc] = acc + bdw_ref[:, lc]

    sz = _silu(_wdot(hb, wi_ref, 2 * D, D) + bi_ref[:, 2 * D:3 * D])
    c = cbuf[...]
    mu = jnp.mean(c, axis=-1, keepdims=True)
    xc = c - mu
    var = jnp.mean(xc * xc, axis=-1, keepdims=True)
    ln = xc * lax.rsqrt(var + LN_EPS) * lnw_ref[...] + lnb_ref[...]
    g = (_silu(ln) * sz).astype(BF16)
    y = _wdot(g, wo_ref, 0, D) + bo_ref[...]
    xn = x_ref[...] + mod_ref[:, 2 * D:3 * D] * y
    if final:
        xn = _rms(xn, fw_ref[...])
    o_ref[...] = xn


def _conv_layer(x, seq_len, layer, row_fn, mod, nw, w_in, b_in, w_dw, b_dw, ln_w, ln_b, w_out, b_out, final_w):
    M = x.shape[0]
    T = TOKEN_TILE
    tps = seq_len // T
    H = CONV_HALO
    n_halo_blocks = M // H
    row = lambda i: (i, 0)
    in_specs = [pl.BlockSpec((T, D), row)]
    args = [x]
    if tps > 1:
        in_specs += [pl.BlockSpec((H, D), lambda i: (jnp.maximum(i * (T // H) - 1, 0), 0)),
                     pl.BlockSpec((H, D), lambda i: (jnp.minimum((i + 1) * (T // H), n_halo_blocks - 1), 0))]
        args += [x, x]
    in_specs += [_const_spec((1, D)), _mod_spec(layer, lambda i: row_fn(i * T)),
                 _wspec(D, 3 * D), _const_spec((1, 3 * D)),
                 _const_spec((CONF_K + 1, D)), _const_spec((1, D)), _const_spec((1, D)), _const_spec((1, D)),
                 _wspec(D, D), _const_spec((1, D))]
    args += [nw, mod, w_in, b_in, w_dw, b_dw, ln_w, ln_b, w_out, b_out]
    final = final_w is not None
    if final:
        in_specs.append(_const_spec((1, D)))
        args.append(final_w)
    return pl.pallas_call(
        functools.partial(_conv_kernel, tps, final),
        grid=(M // T,),
        in_specs=in_specs,
        out_specs=pl.BlockSpec((T, D), row),
        out_shape=jax.ShapeDtypeStruct((M, D), F32),
        scratch_shapes=[pltpu.VMEM((D // LANES, T + 2 * H, LANES), F32), pltpu.VMEM((T, D), F32)],
        compiler_params=_cparams(1),
        name="conv_layer",
    )(*args)


def _prep_conv(p, j):
    w_dw = jnp.pad(p["conv_w_dw"][j], ((0, 1), (0, 0)))
    return (_wsplit(p["conv_w_in"], j), p["conv_b_in"][j][None], w_dw, p["conv_b_dw"][j][None],
            p["conv_ln_w"][j][None], p["conv_ln_b"][j][None], _wsplit(p["conv_w_out"], j), p["conv_b_out"][j][None])


def _out_proj_kernel(g_ref, x_ref, mod_ref, w_ref, o_ref):
    y = _wdot(g_ref[...].astype(BF16), w_ref, 0, D)
    o_ref[...] = x_ref[...] + mod_ref[:, 2 * D:3 * D] * y


def _out_proj(g, x, layer, row_fn, mod, w_out):
    M, K = g.shape
    tm = TOKEN_TILE
    row = lambda i: (i, 0)
    return pl.pallas_call(
        _out_proj_kernel,
        grid=(M // tm,),
        in_specs=[pl.BlockSpec((tm, K), row), pl.BlockSpec((tm, D), row), _mod_spec(layer, lambda i: row_fn(i * tm)),
                  _wspec(K, D)],
        out_specs=pl.BlockSpec((tm, D), row),
        out_shape=jax.ShapeDtypeStruct((M, D), F32),
        compiler_params=_cparams(1),
        name="out_proj",
    )(g, x, mod, w_out)


NA_SCALE = NA_HEAD_DIM ** -0.5
NA_ROWS = 32
NA_WIN = NA_KH * GRID_W
NT_DIMS = (((1,), (1,)), ((), ()))


def _na_in_kernel(x_ref, nw_ref, mod_ref, w_ref, q_ref, k_ref, v_ref, sz_ref):
    hb = _modnorm(x_ref[...], nw_ref[...], mod_ref[...]).astype(BF16)
    q_ref[...] = _wdot(hb, w_ref, 0, D).astype(q_ref.dtype)
    k_ref[...] = _wdot(hb, w_ref, D, D).astype(k_ref.dtype)
    v_ref[...] = _wdot(hb, w_ref, 2 * D, D).astype(v_ref.dtype)
    sz_ref[...] = _silu(_wdot(hb, w_ref, 3 * D, D))


def _na_in(x, layer, row_fn, mod, nw, w_in, kv_dtype):
    M = x.shape[0]
    tm = TOKEN_TILE
    row = lambda i: (i, 0)
    blk = pl.BlockSpec((tm, D), row)
    return pl.pallas_call(
        _na_in_kernel,
        grid=(M // tm,),
        in_specs=[blk, _const_spec((1, D)), _mod_spec(layer, lambda i: row_fn(i * tm)), _wspec(D, 4 * D)],
        out_specs=[blk, blk, blk, blk],
        out_shape=[jax.ShapeDtypeStruct((M, D), BF16), jax.ShapeDtypeStruct((M, D), kv_dtype),
                   jax.ShapeDtypeStruct((M, D), kv_dtype), jax.ShapeDtypeStruct((M, D), F32)],
        compiler_params=_cparams(1),
        name="na_in",
    )(x, nw, mod, w_in)


def _head_masks():
    lane = lax.broadcasted_iota(jnp.int32, (1, LANES), 1)
    return lane < NA_HEAD_DIM


def _na_seq_kernel(x_ref, nw_ref, mod_ref, wi_ref, wo_ref, o_ref, k_ref, v_ref, g_ref):
    n = x_ref.shape[0]
    lo = _head_masks()
    x = x_ref[...]
    hb = _modnorm(x, nw_ref[...], mod_ref[...]).astype(BF16)
    hpc = CONV_CHUNK // NA_HEAD_DIM
    for ch in range(D // CONV_CHUNK):
        cols = slice(ch * CONV_CHUNK, (ch + 1) * CONV_CHUNK)
        q = _wdot(hb, wi_ref, cols.start, CONV_CHUNK).astype(BF16)
        k = _wdot(hb, wi_ref, D + cols.start, CONV_CHUNK)
        v = _wdot(hb, wi_ref, 2 * D + cols.start, CONV_CHUNK)
        sz = _silu(_wdot(hb, wi_ref, 3 * D + cols.start, CONV_CHUNK))
        k_ref[:, ch * hpc:(ch + 1) * hpc, :] = k.reshape(n, hpc, NA_HEAD_DIM)
        v_ref[:, ch * hpc:(ch + 1) * hpc, :] = v.reshape(n, hpc, NA_HEAD_DIM)
        kb = k.astype(BF16)
        vb = v.astype(BF16)
        for pp in range(CONV_CHUNK // LANES):
            pc = slice(pp * LANES, (pp + 1) * LANES)
            qp = q[:, pc]
            zero = jnp.zeros_like(qp)
            qs = jnp.concatenate([jnp.where(lo, qp, zero), jnp.where(lo, zero, qp)], axis=0)
            s = lax.dot_general(qs, kb[:, pc], NT_DIMS, preferred_element_type=F32) * NA_SCALE
            e = jnp.exp(s - jnp.max(s, axis=-1, keepdims=True))
            l = jnp.sum(e, axis=-1, keepdims=True)
            o = jnp.dot(e.astype(BF16), vb[:, pc], preferred_element_type=F32) / l
            g = jnp.where(lo, o[:n], o[n:]) * sz[:, pc]
            g_ref[:, cols.start + pp * LANES:cols.start + (pp + 1) * LANES] = g.astype(BF16)
    y = _wdot(g_ref[...], wo_ref, 0, D)
    o_ref[...] = x + mod_ref[:, 2 * D:3 * D] * y


def _na_seq(x, seq_len, layer, row_fn, mod, nw, w_in, w_out):
    M = x.shape[0]
    row = lambda i: (i, 0)
    kv_blk = pl.BlockSpec((seq_len, NA_HEADS, NA_HEAD_DIM), lambda i: (i, 0, 0))
    kv_shape = jax.ShapeDtypeStruct((M, NA_HEADS, NA_HEAD_DIM), F32)
    return pl.pallas_call(
        _na_seq_kernel,
        grid=(M // seq_len,),
        in_specs=[pl.BlockSpec((seq_len, D), row), _const_spec((1, D)), _mod_spec(layer, lambda i: row_fn(i * seq_len)),
                  _wspec(D, 4 * D), _wspec(D, D)],
        out_specs=[pl.BlockSpec((seq_len, D), row), kv_blk, kv_blk],
        out_shape=[jax.ShapeDtypeStruct((M, D), F32), kv_shape, kv_shape],
        scratch_shapes=[pltpu.VMEM((seq_len, D), BF16)],
        compiler_params=_cparams(1),
        name="na_seq",
    )(x, nw, mod, w_in, w_out)


def _rpb_kernel(rpb_ref, o_ref):
    h = pl.program_id(0)
    c = lax.broadcasted_iota(jnp.int32, (GRID_W, LANES), 0)
    lane = lax.broadcasted_iota(jnp.int32, (GRID_W, LANES), 1)
    kc = lane & (GRID_W - 1)
    upper = lax.broadcasted_iota(jnp.int32, (1, LANES), 1) >= GRID_W
    cs = jnp.clip(c - NA_KW // 2, 0, GRID_W - NA_KW)
    valid = (kc >= cs) & (kc < cs + NA_KW)
    n_pair = 2 * NA_KH - 2

    def toeplitz(dri, lane0):
        row = rpb_ref[pl.ds(h * (2 * NA_KH - 1) + dri, 1), :]
        rows = jnp.broadcast_to(row, (GRID_W, LANES))
        return pltpu.roll(rows, (LANES - (NA_KW - 1) + lane0) % LANES, 1, stride=1, stride_axis=0)

    for dri in range(n_pair):
        tile = jnp.where(upper, toeplitz(dri + 1, GRID_W), toeplitz(dri, 0))
        tile = jnp.where(valid, tile, NEG_INF)
        for idx in range(NA_KH):
            jj2 = dri - (NA_KH - 1) + idx
            if jj2 % 2 == 0 and 0 <= jj2 // 2 < NA_KH // 2:
                jj = jj2 // 2
                o_ref[idx, :, jj * LANES:(jj + 1) * LANES] = tile


def _rpb_table(rpb):
    n_rows = NA_HEADS * (2 * NA_KH - 1)
    rows = jnp.pad(rpb.reshape(n_rows, 2 * NA_KW - 1), ((0, 0), (0, LANES - (2 * NA_KW - 1))))
    return pl.pallas_call(
        _rpb_kernel,
        grid=(NA_HEADS,),
        in_specs=[_const_spec((n_rows, LANES))],
        out_specs=pl.BlockSpec((NA_KH, None, GRID_W, NA_WIN), lambda h: (0, h, 0, 0)),
        out_shape=jax.ShapeDtypeStruct((NA_KH, NA_HEADS, GRID_W, NA_WIN), F32),
        compiler_params=_cparams(1),
        name="na_rpb_table",
    )(rows)


def _na_row_start(r):
    return jnp.clip(r - NA_KH // 2, 0, NA_ROWS - NA_KH)


def _na_lat_kernel(q_ref, k_ref, v_ref, ck_ref, cv_ref, bias_ref, sz_ref, o_ref):
    lo = _head_masks()
    k0 = pl.multiple_of(_na_row_start(pl.program_id(1)) * GRID_W, GRID_W)
    for p in range(NA_HEADS // 2):
        cols = slice(p * LANES, (p + 1) * LANES)
        qp = q_ref[:, cols]
        kw = k_ref[pl.ds(k0, NA_WIN), cols]
        vw = v_ref[pl.ds(k0, NA_WIN), cols]
        ck = ck_ref[:, cols]
        cv = cv_ref[:, cols]
        zero = jnp.zeros_like(qp)
        qs = jnp.concatenate([jnp.where(lo, qp, zero), jnp.where(lo, zero, qp)], axis=0)
        bias = bias_ref[2 * p:2 * p + 2].reshape(2 * GRID_W, NA_WIN)
        s_loc = lax.dot_general(qs, kw, NT_DIMS, preferred_element_type=F32) * NA_SCALE + bias
        s_ctx = lax.dot_general(qs, ck, NT_DIMS, preferred_element_type=F32) * NA_SCALE
        m = jnp.maximum(jnp.max(s_loc, axis=-1, keepdims=True), jnp.max(s_ctx, axis=-1, keepdims=True))
        e_loc = jnp.exp(s_loc - m)
        e_ctx = jnp.exp(s_ctx - m)
        l = jnp.sum(e_loc, axis=-1, keepdims=True) + jnp.sum(e_ctx, axis=-1, keepdims=True)
        o = (jnp.dot(e_loc.astype(BF16), vw, preferred_element_type=F32)
             + jnp.dot(e_ctx.astype(BF16), cv, preferred_element_type=F32)) / l
        o_ref[:, cols] = jnp.where(lo, o[:GRID_W], o[GRID_W:]) * sz_ref[:, cols]


def _na_lat(q, k, v, ck, cv, bias, sz, n_lat):
    T = NA_ROWS * GRID_W
    past = ck.shape[1]
    qblk = pl.BlockSpec((GRID_W, D), lambda b, r: (b * NA_ROWS + r, 0))
    kvblk = pl.BlockSpec((T, D), lambda b, r: (b, 0))
    cblk = pl.BlockSpec((None, past, D), lambda b, r: (b, 0, 0))
    bblk = pl.BlockSpec((None, NA_HEADS, GRID_W, NA_WIN), lambda b, r: (r - _na_row_start(r), 0, 0, 0))
    return pl.pallas_call(
        _na_lat_kernel,
        grid=(n_lat, NA_ROWS),
        in_specs=[qblk, kvblk, kvblk, cblk, cblk, bblk, qblk],
        out_specs=qblk,
        out_shape=jax.ShapeDtypeStruct((n_lat * T, D), F32),
        compiler_params=_cparams(2),
        name="na_lat_attn",
    )(q, k, v, ck, cv, bias, sz)


def _na_layer(xc, xl, layer, mod, nw, p, n_lat):
    j = layer // 3
    w_in = _wsplit(p["na_w_in"], j)
    w_out = _wsplit(p["na_w_out"], j)
    row_ctx = lambda tok: 0
    row_lat = lambda tok: 1 + tok // (NA_ROWS * GRID_W)
    oc, kc, vc = _na_seq(xc, SEQ, layer, row_ctx, mod, nw, w_in, w_out)
    ql, kl, vl, szl = _na_in(xl, layer, row_lat, mod, nw, w_in, BF16)
    past = p["cache_k"].shape[2]
    ck = p["cache_k"][:n_lat, j].reshape(n_lat, past, D).astype(BF16)
    cv = p["cache_v"][:n_lat, j].reshape(n_lat, past, D).astype(BF16)
    gl = _na_lat(ql, kl, vl, ck, cv, _rpb_table(p["na_rpb"][j]), szl, n_lat)
    return oc, _out_proj(gl, xl, layer, row_lat, mod, w_out), kc, vc


SSD_IN_COLS = SSD_INNER + SSD_XBC


def _softplus(x):
    return jnp.maximum(x, 0.0) + jnp.log(1.0 + jnp.exp(-jnp.abs(x)))


def _ssd_in_kernel(tiles_per_seq, *refs):
    refs = list(refs)
    x_ref = refs.pop(0)
    if tiles_per_seq > 1:
        xp_ref = refs.pop(0)
        xn_ref = refs.pop(0)
    nw_ref, mod_ref, wz_ref, wx_ref, wdt_ref, dtb_ref, wc_ref, bc_ref, sz_ref, xbc_ref, dt_ref, xpad_ref = refs
    hb = _modnorm(x_ref[...], nw_ref[...], mod_ref[...]).astype(BF16)
    halo = None
    if tiles_per_seq > 1:
        t = pl.program_id(0) % tiles_per_seq
        xh = jnp.concatenate([xp_ref[...], xn_ref[...]], axis=0)
        halo = (_modnorm(xh, nw_ref[...], mod_ref[...]).astype(BF16), t > 0, t < tiles_per_seq - 1)
    dt_ref[...] = _ssd_project(hb, halo, wz_ref, wx_ref, wdt_ref, dtb_ref, wc_ref, bc_ref, xpad_ref, sz_ref, xbc_ref)


def _ssd_project(hb, halo, wz_ref, wx_ref, wdt_ref, dtb_ref, wc_ref, bc_ref, xpad_ref, sz_ref, xbc_ref):
    tm = hb.shape[0]
    HL = SSD_HALO
    RB = 128
    for j in range(SSD_XBC // D):
        c0 = j * D
        raw = _wdot(hb, wx_ref, c0, D)
        if halo is not None:
            rawh = _wdot(halo[0], wx_ref, c0, D)
            raw_prev = jnp.where(halo[1], rawh[0:HL], 0.0)
            raw_next = jnp.where(halo[2], rawh[HL:2 * HL], 0.0)
        else:
            raw_prev = raw_next = jnp.zeros((HL, D), F32)
        for cb in range(D // LANES):
            blk = j * (D // LANES) + cb
            lc = slice(blk * LANES, (blk + 1) * LANES)
            cc = slice(cb * LANES, (cb + 1) * LANES)
            xpad_ref[blk, 0:HL, :] = raw_prev[:, cc]
            xpad_ref[blk, HL:HL + tm, :] = raw[:, cc]
            xpad_ref[blk, HL + tm:HL + tm + HL, :] = raw_next[:, cc]
            for rb in range(tm // RB):
                acc = jnp.zeros((RB, LANES), F32)
                for k in range(SSD_CONV):
                    r0 = rb * RB + HL - SSD_CONV // 2 + k
                    acc = acc + xpad_ref[blk, r0:r0 + RB, :] * wc_ref[k:k + 1, lc]
                xbc_ref[rb * RB:(rb + 1) * RB, lc] = _silu(acc + bc_ref[:, lc])
    for j in range(SSD_INNER // D):
        sz_ref[:, j * D:(j + 1) * D] = _silu(_wdot(hb, wz_ref, j * D, D))
    raw = jnp.dot(hb, wdt_ref[...], preferred_element_type=F32) + dtb_ref[...]
    lane = lax.broadcasted_iota(jnp.int32, (1, LANES), 1)
    return jnp.where(lane < 2 * SSD_HEADS, _softplus(raw), 0.0)


def _ssd_in(x, seq_len, layer, row_fn, mod, nw, w_z, w_x, w_dt, dt_bias, w_conv, b_conv):
    M = x.shape[0]
    tm = TOKEN_TILE
    HL = SSD_HALO
    tps = seq_len // tm
    n_hblk = M // HL
    row = lambda i: (i, 0)
    in_specs = [pl.BlockSpec((tm, D), row)]
    args = [x]
    if tps > 1:
        in_specs += [pl.BlockSpec((HL, D), lambda i: (jnp.maximum(i * (tm // HL) - 1, 0), 0)),
                     pl.BlockSpec((HL, D), lambda i: (jnp.minimum((i + 1) * (tm // HL), n_hblk - 1), 0))]
        args += [x, x]
    in_specs += [_const_spec((1, D)), _mod_spec(layer, lambda i: row_fn(i * tm)),
                 _wspec(D, SSD_INNER), _wspec(D, SSD_XBC), _const_spec((D, LANES)), _const_spec((1, LANES)),
                 _const_spec((SSD_CONV + 1, SSD_XBC)), _const_spec((1, SSD_XBC))]
    args += [nw, mod, w_z, w_x, w_dt, dt_bias, w_conv, b_conv]
    return pl.pallas_call(
        functools.partial(_ssd_in_kernel, tps),
        grid=(M // tm,),
        in_specs=in_specs,
        out_specs=[pl.BlockSpec((tm, SSD_INNER), row), pl.BlockSpec((tm, SSD_XBC), row), pl.BlockSpec((tm, LANES), row)],
        out_shape=[jax.ShapeDtypeStruct((M, SSD_INNER), F32), jax.ShapeDtypeStruct((M, SSD_XBC), F32),
                   jax.ShapeDtypeStruct((M, LANES), F32)],
        scratch_shapes=[pltpu.VMEM((SSD_XBC // LANES, tm + 2 * HL, LANES), F32)],
        compiler_params=_cparams(1),
        name="ssd_in",
    )(*args)


def _cumsum_rows(a, reverse):
    n = a.shape[0]
    row = lax.broadcasted_iota(jnp.int32, a.shape, 0)
    k = 1
    while k < n:
        if reverse:
            a = a + jnp.where(row < n - k, pltpu.roll(a, n - k, 0), 0.0)
        else:
            a = a + jnp.where(row >= k, pltpu.roll(a, k, 0), 0.0)
        k *= 2
    return a


def _ssd_core_kernel(nc, has_h0, emit_state, *refs):
    refs = list(refs)
    dir_refs = [refs[0:2], refs[2:4]]
    alog_ref, dsk_ref = refs[4:6]
    refs = refs[6:]
    if has_h0:
        h0_ref = refs.pop(0)
    y_refs = [refs.pop(0), refs.pop(0)]
    if emit_state:
        so_ref = refs.pop(0)
    (st_ref,) = refs
    Q, N, P = SSD_CHUNK, SSD_STATE, SSD_HEAD_DIM
    c = pl.program_id(1)
    n_blk = SSD_INNER // LANES

    @pl.when(c == 0)
    def _init():
        if has_h0:
            for dr in range(2):
                for j in range(n_blk):
                    st_ref[dr, :, j * LANES:(j + 1) * LANES] = h0_ref[dr, j * LANES:(j + 1) * LANES, :].T
        else:
            st_ref[...] = jnp.zeros(st_ref.shape, F32)

    A = -jnp.exp(alog_ref[...])
    for dr in range(2):
        x_ref, dt_ref = dir_refs[dr]
        y_ref = y_refs[dr]

        def x_at(c0, x_ref=x_ref):
            return x_ref[:, c0:c0 + LANES]

        def y_put(c0, y, y_ref=y_ref):
            y_ref[:, c0:c0 + LANES] = y

        _ssd_chunk(dr, x_at, dt_ref[...], A, st_ref, y_put, dsk_ref)

    if emit_state:
        @pl.when(c == nc - 1)
        def _emit():
            _ssd_emit_state(st_ref, so_ref)


def _ssd_emit_state(st_ref, so_ref):
    for dr in range(2):
        for j in range(SSD_INNER // LANES):
            so_ref[dr, j * LANES:(j + 1) * LANES, :] = st_ref[dr, :, j * LANES:(j + 1) * LANES].T


def _ssd_chunk(dr, x_at, dt, A, st_ref, y_put, dsk_ref):
    Q, N, P = SSD_CHUNK, SSD_STATE, SSD_HEAD_DIM
    lo = _head_masks()
    ri = lax.broadcasted_iota(jnp.int32, (Q, Q), 0)
    ci = lax.broadcasted_iota(jnp.int32, (Q, Q), 1)
    cum = _cumsum_rows(dt * A, reverse=(dr == 1)) * LOG2E
    cum_t = cum.T
    dt_t = dt.T
    tot_row = Q - 1 if dr == 0 else 0
    mask = (ri >= ci) if dr == 0 else (ri <= ci)
    for g in range(SSD_GROUPS):
        bg_t = x_at(SSD_INNER + g * N).T
        cg = x_at(SSD_INNER + SSD_GN + g * N).astype(BF16)
        cb = jnp.dot(cg, bg_t.astype(BF16), preferred_element_type=F32)
        for pp in range(SSD_HEADS // SSD_GROUPS // 2):
            head = g * (SSD_HEADS // SSD_GROUPS) + 2 * pp
            c0 = head * P
            li = dr * SSD_HEADS + head
            xpair = x_at(c0)
            xpair_b = xpair.astype(BF16)
            cum_cols = [cum[:, li:li + 1], cum[:, li + 1:li + 2]]
            cum_x = jnp.where(lo, cum_cols[0], cum_cols[1])
            yd, sd = [], []
            for hh in range(2):
                dt_row = dt_t[li + hh:li + hh + 1, :]
                cum_row = cum_t[li + hh:li + hh + 1, :]
                decay = jnp.exp2(jnp.where(mask, cum_cols[hh] - cum_row, -jnp.inf))
                yd.append(jnp.dot((cb * decay * dt_row).astype(BF16), xpair_b, preferred_element_type=F32))
                to_end = jnp.exp2(cum_row[:, tot_row:tot_row + 1] - cum_row)
                sd.append(jnp.dot((bg_t * (dt_row * to_end)).astype(BF16), xpair_b, preferred_element_type=F32))
            st = st_ref[dr, :, c0:c0 + LANES]
            y = (jnp.where(lo, yd[0], yd[1])
                 + jnp.dot(cg, st.astype(BF16), preferred_element_type=F32) * jnp.exp2(cum_x))
            if dr == 0:
                y = y + xpair * dsk_ref[:, c0:c0 + LANES]
            y_put(c0, y)
            tot = cum_x[tot_row:tot_row + 1, :]
            st_ref[dr, :, c0:c0 + LANES] = st * jnp.exp2(tot) + jnp.where(lo, sd[0], sd[1])


def _ssd_seq_kernel(x_ref, nw_ref, mod_ref, wz_ref, wx_ref, wdt_ref, dtb_ref, wc_ref, bc_ref, alog_ref, dsk_ref, gnw_ref, wo_ref,
                    o_ref, so_ref, xpad_ref, xbc_ref, sz_ref, y_ref, st_ref):
    Q = SSD_CHUNK
    nc = x_ref.shape[0] // Q
    x = x_ref[...]
    hb = _modnorm(x, nw_ref[...], mod_ref[...]).astype(BF16)
    dt = _ssd_project(hb, None, wz_ref, wx_ref, wdt_ref, dtb_ref, wc_ref, bc_ref, xpad_ref, sz_ref, xbc_ref)
    st_ref[...] = jnp.zeros(st_ref.shape, F32)
    A = -jnp.exp(alog_ref[...])
    for dr in range(2):
        for ck in (range(nc) if dr == 0 else reversed(range(nc))):
            rows = slice(ck * Q, (ck + 1) * Q)

            def x_at(c0, rows=rows):
                return xbc_ref[rows, c0:c0 + LANES]

            def y_put(c0, y, rows=rows, dr=dr):
                if dr == 0:
                    y_ref[rows, c0:c0 + LANES] = y
                else:
                    y_ref[rows, c0:c0 + LANES] += y

            _ssd_chunk(dr, x_at, dt[rows], A, st_ref, y_put, dsk_ref)
    _ssd_emit_state(st_ref, so_ref)
    g = _rms(y_ref[...] * sz_ref[...], gnw_ref[...]).astype(BF16)
    o_ref[...] = x + mod_ref[:, 2 * D:3 * D] * _wdot(g, wo_ref, 0, D)


def _ssd_seq(x, seq_len, layer, row_fn, mod, nw, w_z, w_x, w_dt, dt_bias, w_conv, b_conv, a_log, d_skip, norm_w, w_out):
    M = x.shape[0]
    n_seq = M // seq_len
    HL = SSD_HALO
    row = lambda i: (i, 0)
    st_blk = pl.BlockSpec((None, 2, SSD_INNER, SSD_STATE), lambda i: (i, 0, 0, 0))
    return pl.pallas_call(
        _ssd_seq_kernel,
        grid=(n_seq,),
        in_specs=[pl.BlockSpec((seq_len, D), row), _const_spec((1, D)), _mod_spec(layer, lambda i: row_fn(i * seq_len)),
                  _wspec(D, SSD_INNER), _wspec(D, SSD_XBC), _const_spec((D, LANES)), _const_spec((1, LANES)),
                  _const_spec((SSD_CONV + 1, SSD_XBC)), _const_spec((1, SSD_XBC)),
                  _const_spec((1, LANES)), _const_spec((1, SSD_INNER)), _const_spec((1, SSD_INNER)),
                  _wspec(SSD_INNER, D)],
        out_specs=[pl.BlockSpec((seq_len, D), row), st_blk],
        out_shape=[jax.ShapeDtypeStruct((M, D), F32), jax.ShapeDtypeStruct((n_seq, 2, SSD_INNER, SSD_STATE), F32)],
        scratch_shapes=[pltpu.VMEM((SSD_XBC // LANES, seq_len + 2 * HL, LANES), F32),
                        pltpu.VMEM((seq_len, SSD_XBC), F32), pltpu.VMEM((seq_len, SSD_INNER), F32),
                        pltpu.VMEM((seq_len, SSD_INNER), F32), pltpu.VMEM((2, SSD_STATE, SSD_INNER), F32)],
        compiler_params=_cparams(1),
        name="ssd_seq",
    )(x, nw, mod, w_z, w_x, w_dt, dt_bias, w_conv, b_conv, a_log, d_skip, norm_w, w_out)


def _ssd_core(xbc, dtp, a_log, d_skip, h0, n_seq, seq_len, emit_state):
    Q = SSD_CHUNK
    nc = seq_len // Q
    M = n_seq * seq_len

    def chunk(dr):
        return (lambda b, c: b * nc + c) if dr == 0 else (lambda b, c: b * nc + nc - 1 - c)

    in_specs, args = [], []
    for dr in range(2):
        ch = chunk(dr)
        in_specs += [pl.BlockSpec((Q, SSD_XBC), lambda b, c, ch=ch: (ch(b, c), 0)),
                     pl.BlockSpec((Q, LANES), lambda b, c, ch=ch: (ch(b, c), 0))]
        args += [xbc, dtp]
    in_specs += [_const_spec((1, LANES)), _const_spec((1, SSD_INNER))]
    args += [a_log, d_skip]
    st_blk = pl.BlockSpec((None, 2, SSD_INNER, SSD_STATE), lambda b, c: (b, 0, 0, 0))
    has_h0 = h0 is not None
    if has_h0:
        in_specs.append(st_blk)
        args.append(h0)
    out_specs = [pl.BlockSpec((Q, SSD_INNER), lambda b, c: (chunk(0)(b, c), 0)),
                 pl.BlockSpec((Q, SSD_INNER), lambda b, c: (chunk(1)(b, c), 0))]
    out_shape = [jax.ShapeDtypeStruct((M, SSD_INNER), F32)] * 2
    if emit_state:
        out_specs.append(st_blk)
        out_shape.append(jax.ShapeDtypeStruct((n_seq, 2, SSD_INNER, SSD_STATE), F32))
    return pl.pallas_call(
        functools.partial(_ssd_core_kernel, nc, has_h0, emit_state),
        grid=(n_seq, nc),
        in_specs=in_specs,
        out_specs=out_specs,
        out_shape=out_shape,
        scratch_shapes=[pltpu.VMEM((2, SSD_STATE, SSD_INNER), F32)],
        compiler_params=_cparams(2),
        name="ssd_core",
    )(*args)


def _ssd_out_kernel(yf_ref, yb_ref, sz_ref, x_ref, mod_ref, nw_ref, w_ref, o_ref):
    g = _rms((yf_ref[...] + yb_ref[...]) * sz_ref[...], nw_ref[...]).astype(BF16)
    o_ref[...] = x_ref[...] + mod_ref[:, 2 * D:3 * D] * _wdot(g, w_ref, 0, D)


def _ssd_out(yf, yb, sz, x, layer, row_fn, mod, norm_w, w_out):
    M = x.shape[0]
    tm = TOKEN_TILE
    row = lambda i: (i, 0)
    wide = pl.BlockSpec((tm, SSD_INNER), row)
    return pl.pallas_call(
        _ssd_out_kernel,
        grid=(M // tm,),
        in_specs=[wide, wide, wide, pl.BlockSpec((tm, D), row), _mod_spec(layer, lambda i: row_fn(i * tm)),
                  _const_spec((1, SSD_INNER)), _wspec(SSD_INNER, D)],
        out_specs=pl.BlockSpec((tm, D), row),
        out_shape=jax.ShapeDtypeStruct((M, D), F32),
        compiler_params=_cparams(1),
        name="ssd_out",
    )(yf, yb, sz, x, mod, norm_w, w_out)


def _ssd_layer(xc, xl, layer, mod, nw, p, n_lat):
    j = layer // 3
    w_t = jnp.swapaxes(p["ssd_w_in"], 1, 2)
    w_z = _wsplit(w_t, j, transposed=True, col0=0, ncols=SSD_INNER)
    w_x = _wsplit(w_t, j, transposed=True, col0=SSD_INNER, ncols=SSD_XBC)
    w_dt = jnp.pad(w_t[j, SSD_IN_COLS:].T, ((0, 0), (0, LANES - 2 * SSD_HEADS))).astype(BF16)
    dt_bias = jnp.pad(p["ssd_dt_bias"][j].reshape(1, -1), ((0, 0), (0, LANES - 2 * SSD_HEADS)))
    a_log = jnp.pad(p["ssd_a_log"][j].reshape(1, -1), ((0, 0), (0, LANES - 2 * SSD_HEADS)))
    w_conv = jnp.pad(p["ssd_w_conv"][j], ((0, 1), (0, 0)))
    b_conv = p["ssd_b_conv"][j][None]
    d_skip = jnp.repeat(p["ssd_d"][j], SSD_HEAD_DIM)[None]
    norm_w = p["ssd_norm_w"][j][None]
    w_out = _wsplit(p["ssd_w_out"], j)
    row_ctx = lambda tok: 0
    row_lat = lambda tok: 1 + tok // DEC_SEQ
    oc, s_new = _ssd_seq(xc, SEQ, layer, row_ctx, mod, nw, w_z, w_x, w_dt, dt_bias, w_conv, b_conv, a_log, d_skip, norm_w, w_out)
    szl, xbc_l, dt_l = _ssd_in(xl, DEC_SEQ, layer, row_lat, mod, nw, w_z, w_x, w_dt, dt_bias, w_conv, b_conv)
    h0 = p["state_ssm"][:n_lat, j].reshape(n_lat, 2, SSD_INNER, SSD_STATE)
    yfl, ybl = _ssd_core(xbc_l, dt_l, a_log, d_skip, h0, n_lat, DEC_SEQ, False)
    ol = _ssd_out(yfl, ybl, szl, xl, layer, row_lat, mod, norm_w, w_out)
    return oc, ol, s_new


def kernel(x_prompt, x_sample, cache_k, cache_v, state_ssm, c, c_ctx, ada_w, ada_b, norm_w, final_norm_w, conv_w_in, conv_b_in, conv_w_dw, conv_b_dw, conv_ln_w, conv_ln_b, conv_w_out, conv_b_out, na_w_in, na_rpb, na_w_out, ssd_w_in, ssd_w_conv, ssd_b_conv, ssd_a_log, ssd_dt_bias, ssd_d, ssd_norm_w, ssd_w_out):
    p = dict(cache_k=cache_k, cache_v=cache_v, state_ssm=state_ssm,
             conv_w_in=conv_w_in, conv_b_in=conv_b_in, conv_w_dw=conv_w_dw, conv_b_dw=conv_b_dw,
             conv_ln_w=conv_ln_w, conv_ln_b=conv_ln_b, conv_w_out=conv_w_out, conv_b_out=conv_b_out,
             na_w_in=na_w_in, na_rpb=na_rpb, na_w_out=na_w_out,
             ssd_w_in=ssd_w_in, ssd_w_conv=ssd_w_conv, ssd_b_conv=ssd_b_conv, ssd_a_log=ssd_a_log,
             ssd_dt_bias=ssd_dt_bias, ssd_d=ssd_d, ssd_norm_w=ssd_norm_w, ssd_w_out=ssd_w_out)
    n_ctx, n_lat = x_prompt.shape[0], x_sample.shape[0]
    assert x_prompt.shape[1:] == (SEQ, D) and x_sample.shape[1:] == (DEC_SEQ, D)
    assert (DEPTH - 1) % 3 == 0, "the final RMSNorm is fused into a trailing Conformer layer"
    xc = x_prompt.reshape(n_ctx * SEQ, D)
    xl = x_sample.reshape(n_lat * DEC_SEQ, D)
    cond8 = jnp.concatenate([c_ctx[None], c, jnp.zeros((8 - 1 - n_lat, D), F32)], axis=0)
    mod = _modulation(cond8, ada_w, ada_b).reshape(DEPTH, 8, 1, 3 * D)
    row_ctx = lambda tok: 0
    row_lat = lambda tok: 1 + tok // DEC_SEQ
    new_k, new_v, new_s = [], [], []
    for i in range(DEPTH):
        kind, j = i % 3, i // 3
        nw = norm_w[i][None]
        if kind == 0:
            cp = _prep_conv(p, j)
            fw = final_norm_w[None] if i == DEPTH - 1 else None
            xc = _conv_layer(xc, SEQ, i, row_ctx, mod, nw, *cp, fw)
            xl = _conv_layer(xl, DEC_SEQ, i, row_lat, mod, nw, *cp, fw)
        elif kind == 1:
            xc, xl, k_c, v_c = _na_layer(xc, xl, i, mod, nw, p, n_lat)
            new_k.append(k_c.reshape(n_ctx, SEQ, NA_HEADS, NA_HEAD_DIM))
            new_v.append(v_c.reshape(n_ctx, SEQ, NA_HEADS, NA_HEAD_DIM))
        else:
            xc, xl, s_c = _ssd_layer(xc, xl, i, mod, nw, p, n_lat)
            new_s.append(s_c.reshape(n_ctx, 2, SSD_HEADS, SSD_HEAD_DIM, SSD_STATE))
    return (xc.reshape(n_ctx, SEQ, D), xl.reshape(n_lat, DEC_SEQ, D),
            jnp.stack(new_k, axis=1), jnp.stack(new_v, axis=1), jnp.stack(new_s, axis=1))
```

```python
import functools

import jax
import jax.numpy as jnp
from jax import lax
from jax.experimental import pallas as pl
from jax.experimental.pallas import tpu as pltpu

F32 = jnp.float32
BF16 = jnp.bfloat16

D = 1024
DEPTH = 4
SEQ = 256
DEC_SEQ = 2048
GRID_W = 64
RMS_EPS = 1e-6
LN_EPS = 1e-5
LOG2E = 1.4426950408889634
CONF_K = 31
NA_HEADS = 16
NA_HEAD_DIM = 64
NA_KH = 8
NA_KW = 16
NEG_INF = -1e30
SSD_INNER = 2048
SSD_HEADS = 32
SSD_HEAD_DIM = 64
SSD_STATE = 128
SSD_GROUPS = 4
SSD_GN = SSD_GROUPS * SSD_STATE
SSD_CONV = 7
SSD_CHUNK = 128
SSD_XBC = SSD_INNER + 2 * SSD_GN

LANES = 128
SUBLANES = 8
TOKEN_TILE = 256
W_CHUNK = 512
CONV_HALO = 16
SSD_HALO = 8
VMEM_LIMIT = 56 * 1024 * 1024


def _cparams(n_axes):
    return pltpu.CompilerParams(dimension_semantics=("arbitrary",) * n_axes,
                                vmem_limit_bytes=VMEM_LIMIT)


def _const_spec(shape):
    nd = len(shape)
    return pl.BlockSpec(shape, lambda *_: (0,) * nd, pipeline_mode=pl.Buffered(1))


W_SLABS_PER_STEP = 2


def _wsplit_kernel(w_ref, o_ref):
    for s in range(W_SLABS_PER_STEP):
        o_ref[s] = w_ref[:, s * W_CHUNK:(s + 1) * W_CHUNK].astype(BF16)


def _wsplit(w, layer, col0=0, ncols=None):
    K, N = w.shape[1], w.shape[2]
    ncols = N if ncols is None else ncols
    wide = W_SLABS_PER_STEP * W_CHUNK
    b0 = col0 // wide
    return pl.pallas_call(
        _wsplit_kernel,
        grid=(ncols // wide,),
        in_specs=[pl.BlockSpec((None, K, wide), lambda j: (layer, 0, j + b0))],
        out_specs=pl.BlockSpec((W_SLABS_PER_STEP, K, W_CHUNK), lambda j: (j, 0, 0)),
        out_shape=jax.ShapeDtypeStruct((ncols // W_CHUNK, K, W_CHUNK), BF16),
        compiler_params=_cparams(1),
        name="weight_slabs",
    )(w)


def _wspec(K, N):
    return _const_spec((N // W_CHUNK, K, W_CHUNK))


def _wdot(h, w_ref, c0, n):
    outs = []
    c = c0
    while c < c0 + n:
        j, off = divmod(c, W_CHUNK)
        m = min(W_CHUNK - off, c0 + n - c)
        outs.append(jnp.dot(h, w_ref[j, :, off:off + m], preferred_element_type=F32))
        c += m
    return outs[0] if len(outs) == 1 else jnp.concatenate(outs, axis=1)


def _sigmoid(x):
    return 1.0 / (1.0 + jnp.exp(-x))


def _silu(x):
    return x * _sigmoid(x)


def _rms(x, w):
    ms = jnp.mean(x * x, axis=-1, keepdims=True)
    return x * lax.rsqrt(ms + RMS_EPS) * w


def _modnorm(x, nw, mod):
    return _rms(x, nw) * (1.0 + mod[:, D:2 * D]) + mod[:, :D]


def _mod_spec(layer, row_fn):
    return pl.BlockSpec((None, None, 1, 3 * D), lambda i, *_: (layer, row_fn(i), 0, 0))


def _mod_kernel(cond_ref, w_ref, b_ref, o_ref):
    s = _silu(cond_ref[...]).astype(BF16)
    o_ref[...] = jnp.dot(s, w_ref[...].astype(BF16), preferred_element_type=F32) + b_ref[...]


def _modulation(cond8, ada_w, ada_b):
    tn = 1024
    return pl.pallas_call(
        _mod_kernel,
        grid=(DEPTH, 3 * D // tn),
        in_specs=[pl.BlockSpec((8, D), lambda i, j: (0, 0)),
                  pl.BlockSpec((None, D, tn), lambda i, j: (i, 0, j)),
                  pl.BlockSpec((None, 1, tn), lambda i, j: (i, 0, j))],
        out_specs=pl.BlockSpec((None, 8, tn), lambda i, j: (i, 0, j)),
        out_shape=jax.ShapeDtypeStruct((DEPTH, 8, 3 * D), F32),
        compiler_params=_cparams(2),
        name="adaln_mod",
    )(cond8, ada_w, ada_b.reshape(DEPTH, 1, 3 * D))


CONV_CHUNK = 256


def _conv_kernel(tiles_per_seq, final, *refs):
    refs = list(refs)
    x_ref = refs.pop(0)
    if tiles_per_seq > 1:
        xp_ref = refs.pop(0)
        xn_ref = refs.pop(0)
    nw_ref, mod_ref, wi_ref, bi_ref, wdw_ref, bdw_ref, lnw_ref, lnb_ref, wo_ref, bo_ref = refs[:10]
    refs = refs[10:]
    if final:
        fw_ref = refs.pop(0)
    o_ref, ubuf, cbuf = refs
    T = x_ref.shape[0]
    H = CONV_HALO
    RB = 128

    hb = _modnorm(x_ref[...], nw_ref[...], mod_ref[...]).astype(BF16)
    if tiles_per_seq > 1:
        t = pl.program_id(0) % tiles_per_seq
        hh = _modnorm(jnp.concatenate([xp_ref[...], xn_ref[...]], axis=0), nw_ref[...], mod_ref[...]).astype(BF16)

    def glu(h, cols):
        v = _wdot(h, wi_ref, cols.start, CONV_CHUNK) + bi_ref[:, cols]
        gcols = slice(D + cols.start, D + cols.stop)
        g = _wdot(h, wi_ref, gcols.start, CONV_CHUNK) + bi_ref[:, gcols]
        return v * _sigmoid(g)

    for ch in range(D // CONV_CHUNK):
        cols = slice(ch * CONV_CHUNK, (ch + 1) * CONV_CHUNK)
        u_main = glu(hb, cols)
        if tiles_per_seq > 1:
            uh = glu(hh, cols)
            u_prev = jnp.where(t > 0, uh[0:H], 0.0)
            u_next = jnp.where(t < tiles_per_seq - 1, uh[H:2 * H], 0.0)
        else:
            u_prev = u_next = jnp.zeros((H, CONV_CHUNK), F32)
        for cb in range(CONV_CHUNK // LANES):
            blk = ch * (CONV_CHUNK // LANES) + cb
            lc = slice(blk * LANES, (blk + 1) * LANES)
            cc = slice(cb * LANES, (cb + 1) * LANES)
            ubuf[blk, 0:H, :] = u_prev[:, cc]
            ubuf[blk, H:H + T, :] = u_main[:, cc]
            ubuf[blk, H + T:H + T + H, :] = u_next[:, cc]
            for rb in range(T // RB):
                acc = jnp.zeros((RB, LANES), F32)
                for k in range(CONF_K):
                    r0 = rb * RB + H - CONF_K // 2 + k
                    acc = acc + ubuf[blk, r0:r0 + RB, :] * wdw_ref[k:k + 1, lc]
                cbuf[rb * RB:(rb + 1) * RB, lc] = acc + bdw_ref[:, lc]

    sz = _silu(_wdot(hb, wi_ref, 2 * D, D) + bi_ref[:, 2 * D:3 * D])
    c = cbuf[...]
    mu = jnp.mean(c, axis=-1, keepdims=True)
    xc = c - mu
    var = jnp.mean(xc * xc, axis=-1, keepdims=True)
    ln = xc * lax.rsqrt(var + LN_EPS) * lnw_ref[...] + lnb_ref[...]
    g = (_silu(ln) * sz).astype(BF16)
    y = _wdot(g, wo_ref, 0, D) + bo_ref[...]
    xn = x_ref[...] + mod_ref[:, 2 * D:3 * D] * y
    if final:
        xn = _rms(xn, fw_ref[...])
    o_ref[...] = xn


def _conv_layer(x, seq_len, layer, row_fn, mod, nw, w_in, b_in, w_dw, b_dw, ln_w, ln_b, w_out, b_out, final_w):
    M = x.shape[0]
    T = TOKEN_TILE
    tps = seq_len // T
    H = CONV_HALO
    n_halo_blocks = M // H
    row = lambda i: (i, 0)
    in_specs = [pl.BlockSpec((T, D), row)]
    args = [x]
    if tps > 1:
        in_specs += [pl.BlockSpec((H, D), lambda i: (jnp.maximum(i * (T // H) - 1, 0), 0)),
                     pl.BlockSpec((H, D), lambda i: (jnp.minimum((i + 1) * (T // H), n_halo_blocks - 1), 0))]
        args += [x, x]
    in_specs += [_const_spec((1, D)), _mod_spec(layer, lambda i: row_fn(i * T)),
                 _wspec(D, 3 * D), _const_spec((1, 3 * D)),
                 _const_spec((CONF_K + 1, D)), _const_spec((1, D)), _const_spec((1, D)), _const_spec((1, D)),
                 _wspec(D, D), _const_spec((1, D))]
    args += [nw, mod, w_in, b_in, w_dw, b_dw, ln_w, ln_b, w_out, b_out]
    final = final_w is not None
    if final:
        in_specs.append(_const_spec((1, D)))
        args.append(final_w)
    return pl.pallas_call(
        functools.partial(_conv_kernel, tps, final),
        grid=(M // T,),
        in_specs=in_specs,
        out_specs=pl.BlockSpec((T, D), row),
        out_shape=jax.ShapeDtypeStruct((M, D), F32),
        scratch_shapes=[pltpu.VMEM((D // LANES, T + 2 * H, LANES), F32), pltpu.VMEM((T, D), F32)],
        compiler_params=_cparams(1),
        name="conv_layer",
    )(*args)


def _prep_conv(p, j):
    w_dw = jnp.pad(p["conv_w_dw"][j], ((0, 1), (0, 0)))
    return (_wsplit(p["conv_w_in"], j), p["conv_b_in"][j][None], w_dw, p["conv_b_dw"][j][None],
            p["conv_ln_w"][j][None], p["conv_ln_b"][j][None], _wsplit(p["conv_w_out"], j), p["conv_b_out"][j][None])


def _out_proj_kernel(g_ref, x_ref, mod_ref, w_ref, o_ref):
    y = _wdot(g_ref[...].astype(BF16), w_ref, 0, D)
    o_ref[...] = x_ref[...] + mod_ref[:, 2 * D:3 * D] * y


def _out_proj(g, x, layer, row_fn, mod, w_out):
    M, K = g.shape
    tm = TOKEN_TILE
    row = lambda i: (i, 0)
    return pl.pallas_call(
        _out_proj_kernel,
        grid=(M // tm,),
        in_specs=[pl.BlockSpec((tm, K), row), pl.BlockSpec((tm, D), row), _mod_spec(layer, lambda i: row_fn(i * tm)),
                  _wspec(K, D)],
        out_specs=pl.BlockSpec((tm, D), row),
        out_shape=jax.ShapeDtypeStruct((M, D), F32),
        compiler_params=_cparams(1),
        name="out_proj",
    )(g, x, mod, w_out)


NA_SCALE = NA_HEAD_DIM ** -0.5
NA_ROWS = 32
NA_WIN = NA_KH * GRID_W
NT_DIMS = (((1,), (1,)), ((), ()))


def _na_in_kernel(x_ref, nw_ref, mod_ref, w_ref, q_ref, k_ref, v_ref, sz_ref):
    hb = _modnorm(x_ref[...], nw_ref[...], mod_ref[...]).astype(BF16)
    q_ref[...] = _wdot(hb, w_ref, 0, D).astype(q_ref.dtype)
    k_ref[...] = _wdot(hb, w_ref, D, D).astype(k_ref.dtype)
    v_ref[...] = _wdot(hb, w_ref, 2 * D, D).astype(v_ref.dtype)
    sz_ref[...] = _silu(_wdot(hb, w_ref, 3 * D, D))


def _na_in(x, layer, row_fn, mod, nw, w_in, kv_dtype):
    M = x.shape[0]
    tm = TOKEN_TILE
    row = lambda i: (i, 0)
    blk = pl.BlockSpec((tm, D), row)
    return pl.pallas_call(
        _na_in_kernel,
        grid=(M // tm,),
        in_specs=[blk, _const_spec((1, D)), _mod_spec(layer, lambda i: row_fn(i * tm)), _wspec(D, 4 * D)],
        out_specs=[blk, blk, blk, blk],
        out_shape=[jax.ShapeDtypeStruct((M, D), BF16), jax.ShapeDtypeStruct((M, D), kv_dtype),
                   jax.ShapeDtypeStruct((M, D), kv_dtype), jax.ShapeDtypeStruct((M, D), F32)],
        compiler_params=_cparams(1),
        name="na_in",
    )(x, nw, mod, w_in)


def _head_masks():
    lane = lax.broadcasted_iota(jnp.int32, (1, LANES), 1)
    return lane < NA_HEAD_DIM


def _na_seq_kernel(x_ref, nw_ref, mod_ref, wi_ref, wo_ref, o_ref, k_ref, v_ref, g_ref):
    n = x_ref.shape[0]
    lo = _head_masks()
    x = x_ref[...]
    hb = _modnorm(x, nw_ref[...], mod_ref[...]).astype(BF16)
    hpc = CONV_CHUNK // NA_HEAD_DIM
    for ch in range(D // CONV_CHUNK):
        cols = slice(ch * CONV_CHUNK, (ch + 1) * CONV_CHUNK)
        q = _wdot(hb, wi_ref, cols.start, CONV_CHUNK).astype(BF16)
        k = _wdot(hb, wi_ref, D + cols.start, CONV_CHUNK)
        v = _wdot(hb, wi_ref, 2 * D + cols.start, CONV_CHUNK)
        sz = _silu(_wdot(hb, wi_ref, 3 * D + cols.start, CONV_CHUNK))
        k_ref[:, ch * hpc:(ch + 1) * hpc, :] = k.reshape(n, hpc, NA_HEAD_DIM)
        v_ref[:, ch * hpc:(ch + 1) * hpc, :] = v.reshape(n, hpc, NA_HEAD_DIM)
        kb = k.astype(BF16)
        vb = v.astype(BF16)
        for pp in range(CONV_CHUNK // LANES):
            pc = slice(pp * LANES, (pp + 1) * LANES)
            qp = q[:, pc]
            zero = jnp.zeros_like(qp)
            qs = jnp.concatenate([jnp.where(lo, qp, zero), jnp.where(lo, zero, qp)], axis=0)
            s = lax.dot_general(qs, kb[:, pc], NT_DIMS, preferred_element_type=F32) * NA_SCALE
            e = jnp.exp(s - jnp.max(s, axis=-1, keepdims=True))
            l = jnp.sum(e, axis=-1, keepdims=True)
            o = jnp.dot(e.astype(BF16), vb[:, pc], preferred_element_type=F32) / l
            g = jnp.where(lo, o[:n], o[n:]) * sz[:, pc]
            g_ref[:, cols.start + pp * LANES:cols.start + (pp + 1) * LANES] = g.astype(BF16)
    y = _wdot(g_ref[...], wo_ref, 0, D)
    o_ref[...] = x + mod_ref[:, 2 * D:3 * D] * y


def _na_seq(x, seq_len, layer, row_fn, mod, nw, w_in, w_out):
    M = x.shape[0]
    row = lambda i: (i, 0)
    kv_blk = pl.BlockSpec((seq_len, NA_HEADS, NA_HEAD_DIM), lambda i: (i, 0, 0))
    kv_shape = jax.ShapeDtypeStruct((M, NA_HEADS, NA_HEAD_DIM), F32)
    return pl.pallas_call(
        _na_seq_kernel,
        grid=(M // seq_len,),
        in_specs=[pl.BlockSpec((seq_len, D), row), _const_spec((1, D)), _mod_spec(layer, lambda i: row_fn(i * seq_len)),
                  _wspec(D, 4 * D), _wspec(D, D)],
        out_specs=[pl.BlockSpec((seq_len, D), row), kv_blk, kv_blk],
        out_shape=[jax.ShapeDtypeStruct((M, D), F32), kv_shape, kv_shape],
        scratch_shapes=[pltpu.VMEM((seq_len, D), BF16)],
        compiler_params=_cparams(1),
        name="na_seq",
    )(x, nw, mod, w_in, w_out)


def _rpb_kernel(rpb_ref, o_ref):
    h = pl.program_id(0)
    c = lax.broadcasted_iota(jnp.int32, (GRID_W, LANES), 0)
    lane = lax.broadcasted_iota(jnp.int32, (GRID_W, LANES), 1)
    kc = lane & (GRID_W - 1)
    upper = lax.broadcasted_iota(jnp.int32, (1, LANES), 1) >= GRID_W
    cs = jnp.clip(c - NA_KW // 2, 0, GRID_W - NA_KW)
    valid = (kc >= cs) & (kc < cs + NA_KW)
    n_pair = 2 * NA_KH - 2

    def toeplitz(dri, lane0):
        row = rpb_ref[pl.ds(h * (2 * NA_KH - 1) + dri, 1), :]
        rows = jnp.broadcast_to(row, (GRID_W, LANES))
        return pltpu.roll(rows, (LANES - (NA_KW - 1) + lane0) % LANES, 1, stride=1, stride_axis=0)

    for dri in range(n_pair):
        tile = jnp.where(upper, toeplitz(dri + 1, GRID_W), toeplitz(dri, 0))
        tile = jnp.where(valid, tile, NEG_INF)
        for idx in range(NA_KH):
            jj2 = dri - (NA_KH - 1) + idx
            if jj2 % 2 == 0 and 0 <= jj2 // 2 < NA_KH // 2:
                jj = jj2 // 2
                o_ref[idx, :, jj * LANES:(jj + 1) * LANES] = tile


def _rpb_table(rpb):
    n_rows = NA_HEADS * (2 * NA_KH - 1)
    rows = jnp.pad(rpb.reshape(n_rows, 2 * NA_KW - 1), ((0, 0), (0, LANES - (2 * NA_KW - 1))))
    return pl.pallas_call(
        _rpb_kernel,
        grid=(NA_HEADS,),
        in_specs=[_const_spec((n_rows, LANES))],
        out_specs=pl.BlockSpec((NA_KH, None, GRID_W, NA_WIN), lambda h: (0, h, 0, 0)),
        out_shape=jax.ShapeDtypeStruct((NA_KH, NA_HEADS, GRID_W, NA_WIN), F32),
        compiler_params=_cparams(1),
        name="na_rpb_table",
    )(rows)


def _na_row_start(r):
    return jnp.clip(r - NA_KH // 2, 0, NA_ROWS - NA_KH)


def _na_lat_kernel(q_ref, k_ref, v_ref, ck_ref, cv_ref, bias_ref, sz_ref, o_ref):
    lo = _head_masks()
    k0 = pl.multiple_of(_na_row_start(pl.program_id(1)) * GRID_W, GRID_W)
    for p in range(NA_HEADS // 2):
        cols = slice(p * LANES, (p + 1) * LANES)
        qp = q_ref[:, cols]
        kw = k_ref[pl.ds(k0, NA_WIN), cols]
        vw = v_ref[pl.ds(k0, NA_WIN), cols]
        ck = ck_ref[:, cols]
        cv = cv_ref[:, cols]
        zero = jnp.zeros_like(qp)
        qs = jnp.concatenate([jnp.where(lo, qp, zero), jnp.where(lo, zero, qp)], axis=0)
        bias = bias_ref[2 * p:2 * p + 2].reshape(2 * GRID_W, NA_WIN)
        s_loc = lax.dot_general(qs, kw, NT_DIMS, preferred_element_type=F32) * NA_SCALE + bias
        s_ctx = lax.dot_general(qs, ck, NT_DIMS, preferred_element_type=F32) * NA_SCALE
        m = jnp.maximum(jnp.max(s_loc, axis=-1, keepdims=True), jnp.max(s_ctx, axis=-1, keepdims=True))
        e_loc = jnp.exp(s_loc - m)
        e_ctx = jnp.exp(s_ctx - m)
        l = jnp.sum(e_loc, axis=-1, keepdims=True) + jnp.sum(e_ctx, axis=-1, keepdims=True)
        o = (jnp.dot(e_loc.astype(BF16), vw, preferred_element_type=F32)
             + jnp.dot(e_ctx.astype(BF16), cv, preferred_element_type=F32)) / l
        o_ref[:, cols] = jnp.where(lo, o[:GRID_W], o[GRID_W:]) * sz_ref[:, cols]


def _na_lat(q, k, v, ck, cv, bias, sz, n_lat):
    T = NA_ROWS * GRID_W
    past = ck.shape[1]
    qblk = pl.BlockSpec((GRID_W, D), lambda b, r: (b * NA_ROWS + r, 0))
    kvblk = pl.BlockSpec((T, D), lambda b, r: (b, 0))
    cblk = pl.BlockSpec((None, past, D), lambda b, r: (b, 0, 0))
    bblk = pl.BlockSpec((None, NA_HEADS, GRID_W, NA_WIN), lambda b, r: (r - _na_row_start(r), 0, 0, 0))
    return pl.pallas_call(
        _na_lat_kernel,
        grid=(n_lat, NA_ROWS),
        in_specs=[qblk, kvblk, kvblk, cblk, cblk, bblk, qblk],
        out_specs=qblk,
        out_shape=jax.ShapeDtypeStruct((n_lat * T, D), F32),
        compiler_params=_cparams(2),
        name="na_lat_attn",
    )(q, k, v, ck, cv, bias, sz)


def _na_layer(xc, xl, layer, mod, nw, p, n_lat):
    j = layer // 3
    w_in = _wsplit(p["na_w_in"], j)
    w_out = _wsplit(p["na_w_out"], j)
    row_ctx = lambda tok: 0
    row_lat = lambda tok: 1 + tok // (NA_ROWS * GRID_W)
    oc, kc, vc = _na_seq(xc, SEQ, layer, row_ctx, mod, nw, w_in, w_out)
    ql, kl, vl, szl = _na_in(xl, layer, row_lat, mod, nw, w_in, BF16)
    past = p["cache_k"].shape[2]
    ck = p["cache_k"][:n_lat, j].reshape(n_lat, past, D).astype(BF16)
    cv = p["cache_v"][:n_lat, j].reshape(n_lat, past, D).astype(BF16)
    gl = _na_lat(ql, kl, vl, ck, cv, _rpb_table(p["na_rpb"][j]), szl, n_lat)
    return oc, _out_proj(gl, xl, layer, row_lat, mod, w_out), kc, vc


SSD_IN_COLS = SSD_INNER + SSD_XBC


def _softplus(x):
    return jnp.maximum(x, 0.0) + jnp.log(1.0 + jnp.exp(-jnp.abs(x)))


def _ssd_in_kernel(tiles_per_seq, *refs):
    refs = list(refs)
    x_ref = refs.pop(0)
    if tiles_per_seq > 1:
        xp_ref = refs.pop(0)
        xn_ref = refs.pop(0)
    nw_ref, mod_ref, wz_ref, wx_ref, wdt_ref, dtb_ref, wc_ref, bc_ref, sz_ref, xbc_ref, dt_ref, xpad_ref = refs
    hb = _modnorm(x_ref[...], nw_ref[...], mod_ref[...]).astype(BF16)
    halo = None
    if tiles_per_seq > 1:
        t = pl.program_id(0) % tiles_per_seq
        xh = jnp.concatenate([xp_ref[...], xn_ref[...]], axis=0)
        halo = (_modnorm(xh, nw_ref[...], mod_ref[...]).astype(BF16), t > 0, t < tiles_per_seq - 1)
    dt_ref[...] = _ssd_project(hb, halo, wz_ref, wx_ref, wdt_ref, dtb_ref, wc_ref, bc_ref, xpad_ref, sz_ref, xbc_ref)


def _ssd_project(hb, halo, wz_ref, wx_ref, wdt_ref, dtb_ref, wc_ref, bc_ref, xpad_ref, sz_ref, xbc_ref):
    tm = hb.shape[0]
    HL = SSD_HALO
    RB = 128
    for j in range(SSD_XBC // D):
        c0 = j * D
        raw = _wdot(hb, wx_ref, c0, D)
        if halo is not None:
            rawh = _wdot(halo[0], wx_ref, c0, D)
            raw_prev = jnp.where(halo[1], rawh[0:HL], 0.0)
            raw_next = jnp.where(halo[2], rawh[HL:2 * HL], 0.0)
        else:
            raw_prev = raw_next = jnp.zeros((HL, D), F32)
        for cb in range(D // LANES):
            blk = j * (D // LANES) + cb
            lc = slice(blk * LANES, (blk + 1) * LANES)
            cc = slice(cb * LANES, (cb + 1) * LANES)
            xpad_ref[blk, 0:HL, :] = raw_prev[:, cc]
            xpad_ref[blk, HL:HL + tm, :] = raw[:, cc]
            xpad_ref[blk, HL + tm:HL + tm + HL, :] = raw_next[:, cc]
            for rb in range(tm // RB):
                acc = jnp.zeros((RB, LANES), F32)
                for k in range(SSD_CONV):
                    r0 = rb * RB + HL - SSD_CONV // 2 + k
                    acc = acc + xpad_ref[blk, r0:r0 + RB, :] * wc_ref[k:k + 1, lc]
                xbc_ref[rb * RB:(rb + 1) * RB, lc] = _silu(acc + bc_ref[:, lc])
    for j in range(SSD_INNER // D):
        sz_ref[:, j * D:(j + 1) * D] = _silu(_wdot(hb, wz_ref, j * D, D))
    raw = jnp.dot(hb, wdt_ref[...], preferred_element_type=F32) + dtb_ref[...]
    lane = lax.broadcasted_iota(jnp.int32, (1, LANES), 1)
    return jnp.where(lane < 2 * SSD_HEADS, _softplus(raw), 0.0)


def _ssd_in(x, seq_len, layer, row_fn, mod, nw, w_z, w_x, w_dt, dt_bias, w_conv, b_conv):
    M = x.shape[0]
    tm = TOKEN_TILE
    HL = SSD_HALO
    tps = seq_len // tm
    n_hblk = M // HL
    row = lambda i: (i, 0)
    in_specs = [pl.BlockSpec((tm, D), row)]
    args = [x]
    if tps > 1:
        in_specs += [pl.BlockSpec((HL, D), lambda i: (jnp.maximum(i * (tm // HL) - 1, 0), 0)),
                     pl.BlockSpec((HL, D), lambda i: (jnp.minimum((i + 1) * (tm // HL), n_hblk - 1), 0))]
        args += [x, x]
    in_specs += [_const_spec((1, D)), _mod_spec(layer, lambda i: row_fn(i * tm)),
                 _wspec(D, SSD_INNER), _wspec(D, SSD_XBC), _const_spec((D, LANES)), _const_spec((1, LANES)),
                 _const_spec((SSD_CONV + 1, SSD_XBC)), _const_spec((1, SSD_XBC))]
    args += [nw, mod, w_z, w_x, w_dt, dt_bias, w_conv, b_conv]
    return pl.pallas_call(
        functools.partial(_ssd_in_kernel, tps),
        grid=(M // tm,),
        in_specs=in_specs,
        out_specs=[pl.BlockSpec((tm, SSD_INNER), row), pl.BlockSpec((tm, SSD_XBC), row), pl.BlockSpec((tm, LANES), row)],
        out_shape=[jax.ShapeDtypeStruct((M, SSD_INNER), F32), jax.ShapeDtypeStruct((M, SSD_XBC), F32),
                   jax.ShapeDtypeStruct((M, LANES), F32)],
        scratch_shapes=[pltpu.VMEM((SSD_XBC // LANES, tm + 2 * HL, LANES), F32)],
        compiler_params=_cparams(1),
        name="ssd_in",
    )(*args)


def _cumsum_rows(a, reverse):
    n = a.shape[0]
    row = lax.broadcasted_iota(jnp.int32, a.shape, 0)
    k = 1
    while k < n:
        if reverse:
            a = a + jnp.where(row < n - k, pltpu.roll(a, n - k, 0), 0.0)
        else:
            a = a + jnp.where(row >= k, pltpu.roll(a, k, 0), 0.0)
        k *= 2
    return a


def _ssd_core_kernel(nc, has_h0, emit_state, *refs):
    refs = list(refs)
    dir_refs = [refs[0:2], refs[2:4]]
    alog_ref, dsk_ref = refs[4:6]
    refs = refs[6:]
    if has_h0:
        h0_ref = refs.pop(0)
    y_refs = [refs.pop(0), refs.pop(0)]
    if emit_state:
        so_ref = refs.pop(0)
    (st_ref,) = refs
    Q, N, P = SSD_CHUNK, SSD_STATE, SSD_HEAD_DIM
    c = pl.program_id(1)
    n_blk = SSD_INNER // LANES

    @pl.when(c == 0)
    def _init():
        if has_h0:
            for dr in range(2):
                for j in range(n_blk):
                    st_ref[dr, :, j * LANES:(j + 1) * LANES] = h0_ref[dr, j * LANES:(j + 1) * LANES, :].T
        else:
            st_ref[...] = jnp.zeros(st_ref.shape, F32)

    A = -jnp.exp(alog_ref[...])
    for dr in range(2):
        x_ref, dt_ref = dir_refs[dr]
        y_ref = y_refs[dr]

        def x_at(c0, x_ref=x_ref):
            return x_ref[:, c0:c0 + LANES]

        def y_put(c0, y, y_ref=y_ref):
            y_ref[:, c0:c0 + LANES] = y

        _ssd_chunk(dr, x_at, dt_ref[...], A, st_ref, y_put, dsk_ref)

    if emit_state:
        @pl.when(c == nc - 1)
        def _emit():
            _ssd_emit_state(st_ref, so_ref)


def _ssd_emit_state(st_ref, so_ref):
    for dr in range(2):
        for j in range(SSD_INNER // LANES):
            so_ref[dr, j * LANES:(j + 1) * LANES, :] = st_ref[dr, :, j * LANES:(j + 1) * LANES].T


def _ssd_chunk(dr, x_at, dt, A, st_ref, y_put, dsk_ref):
    Q, N, P = SSD_CHUNK, SSD_STATE, SSD_HEAD_DIM
    lo = _head_masks()
    ri = lax.broadcasted_iota(jnp.int32, (Q, Q), 0)
    ci = lax.broadcasted_iota(jnp.int32, (Q, Q), 1)
    cum = _cumsum_rows(dt * A, reverse=(dr == 1)) * LOG2E
    cum_t = cum.T
    dt_t = dt.T
    tot_row = Q - 1 if dr == 0 else 0
    mask = (ri >= ci) if dr == 0 else (ri <= ci)
    for g in range(SSD_GROUPS):
        bg_t = x_at(SSD_INNER + g * N).T
        cg = x_at(SSD_INNER + SSD_GN + g * N).astype(BF16)
        cb = jnp.dot(cg, bg_t.astype(BF16), preferred_element_type=F32)
        for pp in range(SSD_HEADS // SSD_GROUPS // 2):
            head = g * (SSD_HEADS // SSD_GROUPS) + 2 * pp
            c0 = head * P
            li = dr * SSD_HEADS + head
            xpair = x_at(c0)
            xpair_b = xpair.astype(BF16)
            cum_cols = [cum[:, li:li + 1], cum[:, li + 1:li + 2]]
            cum_x = jnp.where(lo, cum_cols[0], cum_cols[1])
            yd, sd = [], []
            for hh in range(2):
                dt_row = dt_t[li + hh:li + hh + 1, :]
                cum_row = cum_t[li + hh:li + hh + 1, :]
                decay = jnp.exp2(jnp.where(mask, cum_cols[hh] - cum_row, -jnp.inf))
                yd.append(jnp.dot((cb * decay * dt_row).astype(BF16), xpair_b, preferred_element_type=F32))
                to_end = jnp.exp2(cum_row[:, tot_row:tot_row + 1] - cum_row)
                sd.append(jnp.dot((bg_t * (dt_row * to_end)).astype(BF16), xpair_b, preferred_element_type=F32))
            st = st_ref[dr, :, c0:c0 + LANES]
            y = (jnp.where(lo, yd[0], yd[1])
                 + jnp.dot(cg, st.astype(BF16), preferred_element_type=F32) * jnp.exp2(cum_x))
            if dr == 0:
                y = y + xpair * dsk_ref[:, c0:c0 + LANES]
            y_put(c0, y)
            tot = cum_x[tot_row:tot_row + 1, :]
            st_ref[dr, :, c0:c0 + LANES] = st * jnp.exp2(tot) + jnp.where(lo, sd[0], sd[1])


def _ssd_seq_kernel(x_ref, nw_ref, mod_ref, wz_ref, wx_ref, wdt_ref, dtb_ref, wc_ref, bc_ref, alog_ref, dsk_ref, gnw_ref, wo_ref,
                    o_ref, so_ref, xpad_ref, xbc_ref, sz_ref, y_ref, st_ref):
    Q = SSD_CHUNK
    nc = x_ref.shape[0] // Q
    x = x_ref[...]
    hb = _modnorm(x, nw_ref[...], mod_ref[...]).astype(BF16)
    dt = _ssd_project(hb, None, wz_ref, wx_ref, wdt_ref, dtb_ref, wc_ref, bc_ref, xpad_ref, sz_ref, xbc_ref)
    st_ref[...] = jnp.zeros(st_ref.shape, F32)
    A = -jnp.exp(alog_ref[...])
    for dr in range(2):
        for ck in (range(nc) if dr == 0 else reversed(range(nc))):
            rows = slice(ck * Q, (ck + 1) * Q)

            def x_at(c0, rows=rows):
                return xbc_ref[rows, c0:c0 + LANES]

            def y_put(c0, y, rows=rows, dr=dr):
                if dr == 0:
                    y_ref[rows, c0:c0 + LANES] = y
                else:
                    y_ref[rows, c0:c0 + LANES] += y

            _ssd_chunk(dr, x_at, dt[rows], A, st_ref, y_put, dsk_ref)
    _ssd_emit_state(st_ref, so_ref)
    g = _rms(y_ref[...] * sz_ref[...], gnw_ref[...]).astype(BF16)
    o_ref[...] = x + mod_ref[:, 2 * D:3 * D] * _wdot(g, wo_ref, 0, D)


def _ssd_seq(x, seq_len, layer, row_fn, mod, nw, w_z, w_x, w_dt, dt_bias, w_conv, b_conv, a_log, d_skip, norm_w, w_out):
    M = x.shape[0]
    n_seq = M // seq_len
    HL = SSD_HALO
    row = lambda i: (i, 0)
    st_blk = pl.BlockSpec((None, 2, SSD_INNER, SSD_STATE), lambda i: (i, 0, 0, 0))
    return pl.pallas_call(
        _ssd_seq_kernel,
        grid=(n_seq,),
        in_specs=[pl.BlockSpec((seq_len, D), row), _const_spec((1, D)), _mod_spec(layer, lambda i: row_fn(i * seq_len)),
                  _wspec(D, SSD_INNER), _wspec(D, SSD_XBC), _const_spec((D, LANES)), _const_spec((1, LANES)),
                  _const_spec((SSD_CONV + 1, SSD_XBC)), _const_spec((1, SSD_XBC)),
                  _const_spec((1, LANES)), _const_spec((1, SSD_INNER)), _const_spec((1, SSD_INNER)),
                  _wspec(SSD_INNER, D)],
        out_specs=[pl.BlockSpec((seq_len, D), row), st_blk],
        out_shape=[jax.ShapeDtypeStruct((M, D), F32), jax.ShapeDtypeStruct((n_seq, 2, SSD_INNER, SSD_STATE), F32)],
        scratch_shapes=[pltpu.VMEM((SSD_XBC // LANES, seq_len + 2 * HL, LANES), F32),
                        pltpu.VMEM((seq_len, SSD_XBC), F32), pltpu.VMEM((seq_len, SSD_INNER), F32),
                        pltpu.VMEM((seq_len, SSD_INNER), F32), pltpu.VMEM((2, SSD_STATE, SSD_INNER), F32)],
        compiler_params=_cparams(1),
        name="ssd_seq",
    )(x, nw, mod, w_z, w_x, w_dt, dt_bias, w_conv, b_conv, a_log, d_skip, norm_w, w_out)


def _ssd_core(xbc, dtp, a_log, d_skip, h0, n_seq, seq_len, emit_state):
    Q = SSD_CHUNK
    nc = seq_len // Q
    M = n_seq * seq_len

    def chunk(dr):
        return (lambda b, c: b * nc + c) if dr == 0 else (lambda b, c: b * nc + nc - 1 - c)

    in_specs, args = [], []
    for dr in range(2):
        ch = chunk(dr)
        in_specs += [pl.BlockSpec((Q, SSD_XBC), lambda b, c, ch=ch: (ch(b, c), 0)),
                     pl.BlockSpec((Q, LANES), lambda b, c, ch=ch: (ch(b, c), 0))]
        args += [xbc, dtp]
    in_specs += [_const_spec((1, LANES)), _const_spec((1, SSD_INNER))]
    args += [a_log, d_skip]
    st_blk = pl.BlockSpec((None, 2, SSD_INNER, SSD_STATE), lambda b, c: (b, 0, 0, 0))
    has_h0 = h0 is not None
    if has_h0:
        in_specs.append(st_blk)
        args.append(h0)
    out_specs = [pl.BlockSpec((Q, SSD_INNER), lambda b, c: (chunk(0)(b, c), 0)),
                 pl.BlockSpec((Q, SSD_INNER), lambda b, c: (chunk(1)(b, c), 0))]
    out_shape = [jax.ShapeDtypeStruct((M, SSD_INNER), F32)] * 2
    if emit_state:
        out_specs.append(st_blk)
        out_shape.append(jax.ShapeDtypeStruct((n_seq, 2, SSD_INNER, SSD_STATE), F32))
    return pl.pallas_call(
        functools.partial(_ssd_core_kernel, nc, has_h0, emit_state),
        grid=(n_seq, nc),
        in_specs=in_specs,
        out_specs=out_specs,
        out_shape=out_shape,
        scratch_shapes=[pltpu.VMEM((2, SSD_STATE, SSD_INNER), F32)],
        compiler_params=_cparams(2),
        name="ssd_core",
    )(*args)


def _ssd_out_kernel(yf_ref, yb_ref, sz_ref, x_ref, mod_ref, nw_ref, w_ref, o_ref):
    g = _rms((yf_ref[...] + yb_ref[...]) * sz_ref[...], nw_ref[...]).astype(BF16)
    o_ref[...] = x_ref[...] + mod_ref[:, 2 * D:3 * D] * _wdot(g, w_ref, 0, D)


def _ssd_out(yf, yb, sz, x, layer, row_fn, mod, norm_w, w_out):
    M = x.shape[0]
    tm = TOKEN_TILE
    row = lambda i: (i, 0)
    wide = pl.BlockSpec((tm, SSD_INNER), row)
    return pl.pallas_call(
        _ssd_out_kernel,
        grid=(M // tm,),
        in_specs=[wide, wide, wide, pl.BlockSpec((tm, D), row), _mod_spec(layer, lambda i: row_fn(i * tm)),
                  _const_spec((1, SSD_INNER)), _wspec(SSD_INNER, D)],
        out_specs=pl.BlockSpec((tm, D), row),
        out_shape=jax.ShapeDtypeStruct((M, D), F32),
        compiler_params=_cparams(1),
        name="ssd_out",
    )(yf, yb, sz, x, mod, norm_w, w_out)


def _ssd_layer(xc, xl, layer, mod, nw, p, n_lat):
    j = layer // 3
    w_b = p["ssd_w_in"].astype(BF16)
    w_z = _wsplit(w_b, j, col0=0, ncols=SSD_INNER)
    w_x = _wsplit(w_b, j, col0=SSD_INNER, ncols=SSD_XBC)
    w_dt = jnp.pad(w_b[j, :, SSD_IN_COLS:], ((0, 0), (0, LANES - 2 * SSD_HEADS)))
    dt_bias = jnp.pad(p["ssd_dt_bias"][j].reshape(1, -1), ((0, 0), (0, LANES - 2 * SSD_HEADS)))
    a_log = jnp.pad(p["ssd_a_log"][j].reshape(1, -1), ((0, 0), (0, LANES - 2 * SSD_HEADS)))
    w_conv = jnp.pad(p["ssd_w_conv"][j], ((0, 1), (0, 0)))
    b_conv = p["ssd_b_conv"][j][None]
    d_skip = jnp.repeat(p["ssd_d"][j], SSD_HEAD_DIM)[None]
    norm_w = p["ssd_norm_w"][j][None]
    w_out = _wsplit(p["ssd_w_out"], j)
    row_ctx = lambda tok: 0
    row_lat = lambda tok: 1 + tok // DEC_SEQ
    oc, s_new = _ssd_seq(xc, SEQ, layer, row_ctx, mod, nw, w_z, w_x, w_dt, dt_bias, w_conv, b_conv, a_log, d_skip, norm_w, w_out)
    szl, xbc_l, dt_l = _ssd_in(xl, DEC_SEQ, layer, row_lat, mod, nw, w_z, w_x, w_dt, dt_bias, w_conv, b_conv)
    h0 = p["state_ssm"][:n_lat, j].reshape(n_lat, 2, SSD_INNER, SSD_STATE)
    yfl, ybl = _ssd_core(xbc_l, dt_l, a_log, d_skip, h0, n_lat, DEC_SEQ, False)
    ol = _ssd_out(yfl, ybl, szl, xl, layer, row_lat, mod, norm_w, w_out)
    return oc, ol, s_new


def kernel(x_prompt, x_sample, cache_k, cache_v, state_ssm, c, c_ctx, ada_w, ada_b, norm_w, final_norm_w, conv_w_in, conv_b_in, conv_w_dw, conv_b_dw, conv_ln_w, conv_ln_b, conv_w_out, conv_b_out, na_w_in, na_rpb, na_w_out, ssd_w_in, ssd_w_conv, ssd_b_conv, ssd_a_log, ssd_dt_bias, ssd_d, ssd_norm_w, ssd_w_out):
    p = dict(cache_k=cache_k, cache_v=cache_v, state_ssm=state_ssm,
             conv_w_in=conv_w_in, conv_b_in=conv_b_in, conv_w_dw=conv_w_dw, conv_b_dw=conv_b_dw,
             conv_ln_w=conv_ln_w, conv_ln_b=conv_ln_b, conv_w_out=conv_w_out, conv_b_out=conv_b_out,
             na_w_in=na_w_in, na_rpb=na_rpb, na_w_out=na_w_out,
             ssd_w_in=ssd_w_in, ssd_w_conv=ssd_w_conv, ssd_b_conv=ssd_b_conv, ssd_a_log=ssd_a_log,
             ssd_dt_bias=ssd_dt_bias, ssd_d=ssd_d, ssd_norm_w=ssd_norm_w, ssd_w_out=ssd_w_out)
    n_ctx, n_lat = x_prompt.shape[0], x_sample.shape[0]
    assert x_prompt.shape[1:] == (SEQ, D) and x_sample.shape[1:] == (DEC_SEQ, D)
    assert (DEPTH - 1) % 3 == 0, "the final RMSNorm is fused into a trailing Conformer layer"
    xc = x_prompt.reshape(n_ctx * SEQ, D)
    xl = x_sample.reshape(n_lat * DEC_SEQ, D)
    cond8 = jnp.concatenate([c_ctx[None], c, jnp.zeros((8 - 1 - n_lat, D), F32)], axis=0)
    mod = _modulation(cond8, ada_w, ada_b).reshape(DEPTH, 8, 1, 3 * D)
    row_ctx = lambda tok: 0
    row_lat = lambda tok: 1 + tok // DEC_SEQ
    new_k, new_v, new_s = [], [], []
    for i in range(DEPTH):
        kind, j = i % 3, i // 3
        nw = norm_w[i][None]
        if kind == 0:
            cp = _prep_conv(p, j)
            fw = final_norm_w[None] if i == DEPTH - 1 else None
            xc = _conv_layer(xc, SEQ, i, row_ctx, mod, nw, *cp, fw)
            xl = _conv_layer(xl, DEC_SEQ, i, row_lat, mod, nw, *cp, fw)
        elif kind == 1:
            xc, xl, k_c, v_c = _na_layer(xc, xl, i, mod, nw, p, n_lat)
            new_k.append(k_c.reshape(n_ctx, SEQ, NA_HEADS, NA_HEAD_DIM))
            new_v.append(v_c.reshape(n_ctx, SEQ, NA_HEADS, NA_HEAD_DIM))
        else:
            xc, xl, s_c = _ssd_layer(xc, xl, i, mod, nw, p, n_lat)
            new_s.append(s_c.reshape(n_ctx, 2, SSD_HEADS, SSD_HEAD_DIM, SSD_STATE))
    return (xc.reshape(n_ctx, SEQ, D), xl.reshape(n_lat, DEC_SEQ, D),
            jnp.stack(new_k, axis=1), jnp.stack(new_v, axis=1), jnp.stack(new_s, axis=1))
```

```python
import functools

import jax
import jax.numpy as jnp
from jax import lax
from jax.experimental import pallas as pl
from jax.experimental.pallas import tpu as pltpu

F32 = jnp.float32
BF16 = jnp.bfloat16

D = 1024
DEPTH = 4
SEQ = 256
DEC_SEQ = 2048
GRID_W = 64
RMS_EPS = 1e-6
LN_EPS = 1e-5
LOG2E = 1.4426950408889634
CONF_K = 31
NA_HEADS = 16
NA_HEAD_DIM = 64
NA_KH = 8
NA_KW = 16
NEG_INF = -1e30
SSD_INNER = 2048
SSD_HEADS = 32
SSD_HEAD_DIM = 64
SSD_STATE = 128
SSD_GROUPS = 4
SSD_GN = SSD_GROUPS * SSD_STATE
SSD_CONV = 7
SSD_CHUNK = 128
SSD_XBC = SSD_INNER + 2 * SSD_GN

LANES = 128
SUBLANES = 8
TOKEN_TILE = 256
W_CHUNK = 512
CONV_HALO = 16
SSD_HALO = 8
VMEM_LIMIT = 56 * 1024 * 1024


def _cparams(n_axes):
    return pltpu.CompilerParams(dimension_semantics=("arbitrary",) * n_axes,
                                vmem_limit_bytes=VMEM_LIMIT)


def _const_spec(shape):
    nd = len(shape)
    return pl.BlockSpec(shape, lambda *_: (0,) * nd, pipeline_mode=pl.Buffered(1))


W_SLABS_PER_STEP = 2


def _wsplit_kernel(w_ref, o_ref):
    for s in range(W_SLABS_PER_STEP):
        o_ref[s] = w_ref[:, s * W_CHUNK:(s + 1) * W_CHUNK].astype(BF16)


def _wsplit(w, layer, col0=0, ncols=None):
    K, N = w.shape[1], w.shape[2]
    ncols = N if ncols is None else ncols
    wide = W_SLABS_PER_STEP * W_CHUNK
    b0 = col0 // wide
    rows = TOKEN_TILE
    return pl.pallas_call(
        _wsplit_kernel,
        grid=(ncols // wide, K // rows),
        in_specs=[pl.BlockSpec((None, rows, wide), lambda j, r: (layer, r, j + b0))],
        out_specs=pl.BlockSpec((W_SLABS_PER_STEP, rows, W_CHUNK), lambda j, r: (j, r, 0)),
        out_shape=jax.ShapeDtypeStruct((ncols // W_CHUNK, K, W_CHUNK), BF16),
        compiler_params=_cparams(2),
        name="weight_slabs",
    )(w)


def _wspec(K, N):
    return _const_spec((N // W_CHUNK, K, W_CHUNK))


def _wdot(h, w_ref, c0, n):
    outs = []
    c = c0
    while c < c0 + n:
        j, off = divmod(c, W_CHUNK)
        m = min(W_CHUNK - off, c0 + n - c)
        outs.append(jnp.dot(h, w_ref[j, :, off:off + m], preferred_element_type=F32))
        c += m
    return outs[0] if len(outs) == 1 else jnp.concatenate(outs, axis=1)


def _sigmoid(x):
    return 1.0 / (1.0 + jnp.exp(-x))


def _silu(x):
    return x * _sigmoid(x)


def _rms(x, w):
    ms = jnp.mean(x * x, axis=-1, keepdims=True)
    return x * lax.rsqrt(ms + RMS_EPS) * w


def _modnorm(x, nw, mod):
    return _rms(x, nw) * (1.0 + mod[:, D:2 * D]) + mod[:, :D]


def _mod_spec(layer, row_fn):
    return pl.BlockSpec((None, None, 1, 3 * D), lambda i, *_: (layer, row_fn(i), 0, 0))


def _mod_kernel(cond_ref, w_ref, b_ref, o_ref):
    s = _silu(cond_ref[...]).astype(BF16)
    o_ref[...] = jnp.dot(s, w_ref[...].astype(BF16), preferred_element_type=F32) + b_ref[...]


def _modulation(cond8, ada_w, ada_b):
    tn = 1024
    return pl.pallas_call(
        _mod_kernel,
        grid=(DEPTH, 3 * D // tn),
        in_specs=[pl.BlockSpec((8, D), lambda i, j: (0, 0)),
                  pl.BlockSpec((None, D, tn), lambda i, j: (i, 0, j)),
                  pl.BlockSpec((None, 1, tn), lambda i, j: (i, 0, j))],
        out_specs=pl.BlockSpec((None, 8, tn), lambda i, j: (i, 0, j)),
        out_shape=jax.ShapeDtypeStruct((DEPTH, 8, 3 * D), F32),
        compiler_params=_cparams(2),
        name="adaln_mod",
    )(cond8, ada_w, ada_b.reshape(DEPTH, 1, 3 * D))


CONV_CHUNK = 256


def _conv_kernel(tiles_per_seq, final, *refs):
    refs = list(refs)
    x_ref = refs.pop(0)
    if tiles_per_seq > 1:
        xp_ref = refs.pop(0)
        xn_ref = refs.pop(0)
    nw_ref, mod_ref, wi_ref, bi_ref, wdw_ref, bdw_ref, lnw_ref, lnb_ref, wo_ref, bo_ref = refs[:10]
    refs = refs[10:]
    if final:
        fw_ref = refs.pop(0)
    o_ref, ubuf, cbuf = refs
    T = x_ref.shape[0]
    H = CONV_HALO
    RB = 128

    hb = _modnorm(x_ref[...], nw_ref[...], mod_ref[...]).astype(BF16)
    if tiles_per_seq > 1:
        t = pl.program_id(0) % tiles_per_seq
        hh = _modnorm(jnp.concatenate([xp_ref[...], xn_ref[...]], axis=0), nw_ref[...], mod_ref[...]).astype(BF16)

    def glu(h, cols):
        v = _wdot(h, wi_ref, cols.start, CONV_CHUNK) + bi_ref[:, cols]
        gcols = slice(D + cols.start, D + cols.stop)
        g = _wdot(h, wi_ref, gcols.start, CONV_CHUNK) + bi_ref[:, gcols]
        return v * _sigmoid(g)

    for ch in range(D // CONV_CHUNK):
        cols = slice(ch * CONV_CHUNK, (ch + 1) * CONV_CHUNK)
        u_main = glu(hb, cols)
        if tiles_per_seq > 1:
            uh = glu(hh, cols)
            u_prev = jnp.where(t > 0, uh[0:H], 0.0)
            u_next = jnp.where(t < tiles_per_seq - 1, uh[H:2 * H], 0.0)
        else:
            u_prev = u_next = jnp.zeros((H, CONV_CHUNK), F32)
        for cb in range(CONV_CHUNK // LANES):
            blk = ch * (CONV_CHUNK // LANES) + cb
            lc = slice(blk * LANES, (blk + 1) * LANES)
            cc = slice(cb * LANES, (cb + 1) * LANES)
            ubuf[blk, 0:H, :] = u_prev[:, cc]
            ubuf[blk, H:H + T, :] = u_main[:, cc]
            ubuf[blk, H + T:H + T + H, :] = u_next[:, cc]
            for rb in range(T // RB):
                acc = jnp.zeros((RB, LANES), F32)
                for k in range(CONF_K):
                    r0 = rb * RB + H - CONF_K // 2 + k
                    acc = acc + ubuf[blk, r0:r0 + RB, :] * wdw_ref[k:k + 1, lc]
                cbuf[rb * RB:(rb + 1) * RB, lc] = acc + bdw_ref[:, lc]

    sz = _silu(_wdot(hb, wi_ref, 2 * D, D) + bi_ref[:, 2 * D:3 * D])
    c = cbuf[...]
    mu = jnp.mean(c, axis=-1, keepdims=True)
    xc = c - mu
    var = jnp.mean(xc * xc, axis=-1, keepdims=True)
    ln = xc * lax.rsqrt(var + LN_EPS) * lnw_ref[...] + lnb_ref[...]
    g = (_silu(ln) * sz).astype(BF16)
    y = _wdot(g, wo_ref, 0, D) + bo_ref[...]
    xn = x_ref[...] + mod_ref[:, 2 * D:3 * D] * y
    if final:
        xn = _rms(xn, fw_ref[...])
    o_ref[...] = xn


def _conv_layer(x, seq_len, layer, row_fn, mod, nw, w_in, b_in, w_dw, b_dw, ln_w, ln_b, w_out, b_out, final_w):
    M = x.shape[0]
    T = TOKEN_TILE
    tps = seq_len // T
    H = CONV_HALO
    n_halo_blocks = M // H
    row = lambda i: (i, 0)
    in_specs = [pl.BlockSpec((T, D), row)]
    args = [x]
    if tps > 1:
        in_specs += [pl.BlockSpec((H, D), lambda i: (jnp.maximum(i * (T // H) - 1, 0), 0)),
                     pl.BlockSpec((H, D), lambda i: (jnp.minimum((i + 1) * (T // H), n_halo_blocks - 1), 0))]
        args += [x, x]
    in_specs += [_const_spec((1, D)), _mod_spec(layer, lambda i: row_fn(i * T)),
                 _wspec(D, 3 * D), _const_spec((1, 3 * D)),
                 _const_spec((CONF_K + 1, D)), _const_spec((1, D)), _const_spec((1, D)), _const_spec((1, D)),
                 _wspec(D, D), _const_spec((1, D))]
    args += [nw, mod, w_in, b_in, w_dw, b_dw, ln_w, ln_b, w_out, b_out]
    final = final_w is not None
    if final:
        in_specs.append(_const_spec((1, D)))
        args.append(final_w)
    return pl.pallas_call(
        functools.partial(_conv_kernel, tps, final),
        grid=(M // T,),
        in_specs=in_specs,
        out_specs=pl.BlockSpec((T, D), row),
        out_shape=jax.ShapeDtypeStruct((M, D), F32),
        scratch_shapes=[pltpu.VMEM((D // LANES, T + 2 * H, LANES), F32), pltpu.VMEM((T, D), F32)],
        compiler_params=_cparams(1),
        name="conv_layer",
    )(*args)


def _prep_conv(p, j):
    w_dw = jnp.pad(p["conv_w_dw"][j], ((0, 1), (0, 0)))
    return (_wsplit(p["conv_w_in"], j), p["conv_b_in"][j][None], w_dw, p["conv_b_dw"][j][None],
            p["conv_ln_w"][j][None], p["conv_ln_b"][j][None], _wsplit(p["conv_w_out"], j), p["conv_b_out"][j][None])


def _out_proj_kernel(g_ref, x_ref, mod_ref, w_ref, o_ref):
    y = _wdot(g_ref[...].astype(BF16), w_ref, 0, D)
    o_ref[...] = x_ref[...] + mod_ref[:, 2 * D:3 * D] * y


def _out_proj(g, x, layer, row_fn, mod, w_out):
    M, K = g.shape
    tm = TOKEN_TILE
    row = lambda i: (i, 0)
    return pl.pallas_call(
        _out_proj_kernel,
        grid=(M // tm,),
        in_specs=[pl.BlockSpec((tm, K), row), pl.BlockSpec((tm, D), row), _mod_spec(layer, lambda i: row_fn(i * tm)),
                  _wspec(K, D)],
        out_specs=pl.BlockSpec((tm, D), row),
        out_shape=jax.ShapeDtypeStruct((M, D), F32),
        compiler_params=_cparams(1),
        name="out_proj",
    )(g, x, mod, w_out)


NA_SCALE = NA_HEAD_DIM ** -0.5
NA_ROWS = 32
NA_WIN = NA_KH * GRID_W
NT_DIMS = (((1,), (1,)), ((), ()))


def _na_in_kernel(x_ref, nw_ref, mod_ref, w_ref, q_ref, k_ref, v_ref, sz_ref):
    hb = _modnorm(x_ref[...], nw_ref[...], mod_ref[...]).astype(BF16)
    q_ref[...] = _wdot(hb, w_ref, 0, D).astype(q_ref.dtype)
    k_ref[...] = _wdot(hb, w_ref, D, D).astype(k_ref.dtype)
    v_ref[...] = _wdot(hb, w_ref, 2 * D, D).astype(v_ref.dtype)
    sz_ref[...] = _silu(_wdot(hb, w_ref, 3 * D, D))


def _na_in(x, layer, row_fn, mod, nw, w_in, kv_dtype):
    M = x.shape[0]
    tm = TOKEN_TILE
    row = lambda i: (i, 0)
    blk = pl.BlockSpec((tm, D), row)
    return pl.pallas_call(
        _na_in_kernel,
        grid=(M // tm,),
        in_specs=[blk, _const_spec((1, D)), _mod_spec(layer, lambda i: row_fn(i * tm)), _wspec(D, 4 * D)],
        out_specs=[blk, blk, blk, blk],
        out_shape=[jax.ShapeDtypeStruct((M, D), BF16), jax.ShapeDtypeStruct((M, D), kv_dtype),
                   jax.ShapeDtypeStruct((M, D), kv_dtype), jax.ShapeDtypeStruct((M, D), F32)],
        compiler_params=_cparams(1),
        name="na_in",
    )(x, nw, mod, w_in)


def _head_masks():
    lane = lax.broadcasted_iota(jnp.int32, (1, LANES), 1)
    return lane < NA_HEAD_DIM


def _na_seq_kernel(x_ref, nw_ref, mod_ref, wi_ref, wo_ref, o_ref, k_ref, v_ref, g_ref):
    n = x_ref.shape[0]
    lo = _head_masks()
    x = x_ref[...]
    hb = _modnorm(x, nw_ref[...], mod_ref[...]).astype(BF16)
    hpc = CONV_CHUNK // NA_HEAD_DIM
    for ch in range(D // CONV_CHUNK):
        cols = slice(ch * CONV_CHUNK, (ch + 1) * CONV_CHUNK)
        q = _wdot(hb, wi_ref, cols.start, CONV_CHUNK).astype(BF16)
        k = _wdot(hb, wi_ref, D + cols.start, CONV_CHUNK)
        v = _wdot(hb, wi_ref, 2 * D + cols.start, CONV_CHUNK)
        sz = _silu(_wdot(hb, wi_ref, 3 * D + cols.start, CONV_CHUNK))
        k_ref[:, ch * hpc:(ch + 1) * hpc, :] = k.reshape(n, hpc, NA_HEAD_DIM)
        v_ref[:, ch * hpc:(ch + 1) * hpc, :] = v.reshape(n, hpc, NA_HEAD_DIM)
        kb = k.astype(BF16)
        vb = v.astype(BF16)
        for pp in range(CONV_CHUNK // LANES):
            pc = slice(pp * LANES, (pp + 1) * LANES)
            qp = q[:, pc]
            zero = jnp.zeros_like(qp)
            qs = jnp.concatenate([jnp.where(lo, qp, zero), jnp.where(lo, zero, qp)], axis=0)
            s = lax.dot_general(qs, kb[:, pc], NT_DIMS, preferred_element_type=F32) * NA_SCALE
            e = jnp.exp(s - jnp.max(s, axis=-1, keepdims=True))
            l = jnp.sum(e, axis=-1, keepdims=True)
            o = jnp.dot(e.astype(BF16), vb[:, pc], preferred_element_type=F32) / l
            g = jnp.where(lo, o[:n], o[n:]) * sz[:, pc]
            g_ref[:, cols.start + pp * LANES:cols.start + (pp + 1) * LANES] = g.astype(BF16)
    y = _wdot(g_ref[...], wo_ref, 0, D)
    o_ref[...] = x + mod_ref[:, 2 * D:3 * D] * y


def _na_seq(x, seq_len, layer, row_fn, mod, nw, w_in, w_out):
    M = x.shape[0]
    row = lambda i: (i, 0)
    kv_blk = pl.BlockSpec((seq_len, NA_HEADS, NA_HEAD_DIM), lambda i: (i, 0, 0))
    kv_shape = jax.ShapeDtypeStruct((M, NA_HEADS, NA_HEAD_DIM), F32)
    return pl.pallas_call(
        _na_seq_kernel,
        grid=(M // seq_len,),
        in_specs=[pl.BlockSpec((seq_len, D), row), _const_spec((1, D)), _mod_spec(layer, lambda i: row_fn(i * seq_len)),
                  _wspec(D, 4 * D), _wspec(D, D)],
        out_specs=[pl.BlockSpec((seq_len, D), row), kv_blk, kv_blk],
        out_shape=[jax.ShapeDtypeStruct((M, D), F32), kv_shape, kv_shape],
        scratch_shapes=[pltpu.VMEM((seq_len, D), BF16)],
        compiler_params=_cparams(1),
        name="na_seq",
    )(x, nw, mod, w_in, w_out)


def _rpb_kernel(rpb_ref, o_ref):
    h = pl.program_id(0)
    c = lax.broadcasted_iota(jnp.int32, (GRID_W, LANES), 0)
    lane = lax.broadcasted_iota(jnp.int32, (GRID_W, LANES), 1)
    kc = lane & (GRID_W - 1)
    upper = lax.broadcasted_iota(jnp.int32, (1, LANES), 1) >= GRID_W
    cs = jnp.clip(c - NA_KW // 2, 0, GRID_W - NA_KW)
    valid = (kc >= cs) & (kc < cs + NA_KW)
    n_pair = 2 * NA_KH - 2

    def toeplitz(dri, lane0):
        row = rpb_ref[pl.ds(h * (2 * NA_KH - 1) + dri, 1), :]
        rows = jnp.broadcast_to(row, (GRID_W, LANES))
        return pltpu.roll(rows, (LANES - (NA_KW - 1) + lane0) % LANES, 1, stride=1, stride_axis=0)

    for dri in range(n_pair):
        tile = jnp.where(upper, toeplitz(dri + 1, GRID_W), toeplitz(dri, 0))
        tile = jnp.where(valid, tile, NEG_INF)
        for idx in range(NA_KH):
            jj2 = dri - (NA_KH - 1) + idx
            if jj2 % 2 == 0 and 0 <= jj2 // 2 < NA_KH // 2:
                jj = jj2 // 2
                o_ref[idx, :, jj * LANES:(jj + 1) * LANES] = tile


def _rpb_table(rpb):
    n_rows = NA_HEADS * (2 * NA_KH - 1)
    rows = jnp.pad(rpb.reshape(n_rows, 2 * NA_KW - 1), ((0, 0), (0, LANES - (2 * NA_KW - 1))))
    return pl.pallas_call(
        _rpb_kernel,
        grid=(NA_HEADS,),
        in_specs=[_const_spec((n_rows, LANES))],
        out_specs=pl.BlockSpec((NA_KH, None, GRID_W, NA_WIN), lambda h: (0, h, 0, 0)),
        out_shape=jax.ShapeDtypeStruct((NA_KH, NA_HEADS, GRID_W, NA_WIN), F32),
        compiler_params=_cparams(1),
        name="na_rpb_table",
    )(rows)


def _na_row_start(r):
    return jnp.clip(r - NA_KH // 2, 0, NA_ROWS - NA_KH)


def _na_lat_kernel(q_ref, k_ref, v_ref, ck_ref, cv_ref, bias_ref, sz_ref, o_ref):
    lo = _head_masks()
    k0 = pl.multiple_of(_na_row_start(pl.program_id(1)) * GRID_W, GRID_W)
    for p in range(NA_HEADS // 2):
        cols = slice(p * LANES, (p + 1) * LANES)
        qp = q_ref[:, cols]
        kw = k_ref[pl.ds(k0, NA_WIN), cols]
        vw = v_ref[pl.ds(k0, NA_WIN), cols]
        ck = ck_ref[:, cols]
        cv = cv_ref[:, cols]
        zero = jnp.zeros_like(qp)
        qs = jnp.concatenate([jnp.where(lo, qp, zero), jnp.where(lo, zero, qp)], axis=0)
        bias = bias_ref[2 * p:2 * p + 2].reshape(2 * GRID_W, NA_WIN)
        s_loc = lax.dot_general(qs, kw, NT_DIMS, preferred_element_type=F32) * NA_SCALE + bias
        s_ctx = lax.dot_general(qs, ck, NT_DIMS, preferred_element_type=F32) * NA_SCALE
        m = jnp.maximum(jnp.max(s_loc, axis=-1, keepdims=True), jnp.max(s_ctx, axis=-1, keepdims=True))
        e_loc = jnp.exp(s_loc - m)
        e_ctx = jnp.exp(s_ctx - m)
        l = jnp.sum(e_loc, axis=-1, keepdims=True) + jnp.sum(e_ctx, axis=-1, keepdims=True)
        o = (jnp.dot(e_loc.astype(BF16), vw, preferred_element_type=F32)
             + jnp.dot(e_ctx.astype(BF16), cv, preferred_element_type=F32)) / l
        o_ref[:, cols] = jnp.where(lo, o[:GRID_W], o[GRID_W:]) * sz_ref[:, cols]


def _na_lat(q, k, v, ck, cv, bias, sz, n_lat):
    T = NA_ROWS * GRID_W
    past = ck.shape[1]
    qblk = pl.BlockSpec((GRID_W, D), lambda b, r: (b * NA_ROWS + r, 0))
    kvblk = pl.BlockSpec((T, D), lambda b, r: (b, 0))
    cblk = pl.BlockSpec((None, past, D), lambda b, r: (b, 0, 0))
    bblk = pl.BlockSpec((None, NA_HEADS, GRID_W, NA_WIN), lambda b, r: (r - _na_row_start(r), 0, 0, 0))
    return pl.pallas_call(
        _na_lat_kernel,
        grid=(n_lat, NA_ROWS),
        in_specs=[qblk, kvblk, kvblk, cblk, cblk, bblk, qblk],
        out_specs=qblk,
        out_shape=jax.ShapeDtypeStruct((n_lat * T, D), F32),
        compiler_params=_cparams(2),
        name="na_lat_attn",
    )(q, k, v, ck, cv, bias, sz)


def _na_layer(xc, xl, layer, mod, nw, p, n_lat):
    j = layer // 3
    w_in = _wsplit(p["na_w_in"], j)
    w_out = _wsplit(p["na_w_out"], j)
    row_ctx = lambda tok: 0
    row_lat = lambda tok: 1 + tok // (NA_ROWS * GRID_W)
    oc, kc, vc = _na_seq(xc, SEQ, layer, row_ctx, mod, nw, w_in, w_out)
    ql, kl, vl, szl = _na_in(xl, layer, row_lat, mod, nw, w_in, BF16)
    past = p["cache_k"].shape[2]
    ck = p["cache_k"][:n_lat, j].reshape(n_lat, past, D).astype(BF16)
    cv = p["cache_v"][:n_lat, j].reshape(n_lat, past, D).astype(BF16)
    gl = _na_lat(ql, kl, vl, ck, cv, _rpb_table(p["na_rpb"][j]), szl, n_lat)
    return oc, _out_proj(gl, xl, layer, row_lat, mod, w_out), kc, vc


SSD_IN_COLS = SSD_INNER + SSD_XBC


def _softplus(x):
    return jnp.maximum(x, 0.0) + jnp.log(1.0 + jnp.exp(-jnp.abs(x)))


def _ssd_in_kernel(tiles_per_seq, *refs):
    refs = list(refs)
    x_ref = refs.pop(0)
    if tiles_per_seq > 1:
        xp_ref = refs.pop(0)
        xn_ref = refs.pop(0)
    nw_ref, mod_ref, wz_ref, wx_ref, wdt_ref, dtb_ref, wc_ref, bc_ref, sz_ref, xbc_ref, dt_ref, xpad_ref = refs
    hb = _modnorm(x_ref[...], nw_ref[...], mod_ref[...]).astype(BF16)
    halo = None
    if tiles_per_seq > 1:
        t = pl.program_id(0) % tiles_per_seq
        xh = jnp.concatenate([xp_ref[...], xn_ref[...]], axis=0)
        halo = (_modnorm(xh, nw_ref[...], mod_ref[...]).astype(BF16), t > 0, t < tiles_per_seq - 1)
    dt_ref[...] = _ssd_project(hb, halo, wz_ref, wx_ref, wdt_ref, dtb_ref, wc_ref, bc_ref, xpad_ref, sz_ref, xbc_ref)


def _ssd_project(hb, halo, wz_ref, wx_ref, wdt_ref, dtb_ref, wc_ref, bc_ref, xpad_ref, sz_ref, xbc_ref):
    tm = hb.shape[0]
    HL = SSD_HALO
    RB = 128
    for j in range(SSD_XBC // D):
        c0 = j * D
        raw = _wdot(hb, wx_ref, c0, D)
        if halo is not None:
            rawh = _wdot(halo[0], wx_ref, c0, D)
            raw_prev = jnp.where(halo[1], rawh[0:HL], 0.0)
            raw_next = jnp.where(halo[2], rawh[HL:2 * HL], 0.0)
        else:
            raw_prev = raw_next = jnp.zeros((HL, D), F32)
        for cb in range(D // LANES):
            blk = j * (D // LANES) + cb
            lc = slice(blk * LANES, (blk + 1) * LANES)
            cc = slice(cb * LANES, (cb + 1) * LANES)
            xpad_ref[blk, 0:HL, :] = raw_prev[:, cc]
            xpad_ref[blk, HL:HL + tm, :] = raw[:, cc]
            xpad_ref[blk, HL + tm:HL + tm + HL, :] = raw_next[:, cc]
            for rb in range(tm // RB):
                acc = jnp.zeros((RB, LANES), F32)
                for k in range(SSD_CONV):
                    r0 = rb * RB + HL - SSD_CONV // 2 + k
                    acc = acc + xpad_ref[blk, r0:r0 + RB, :] * wc_ref[k:k + 1, lc]
                xbc_ref[rb * RB:(rb + 1) * RB, lc] = _silu(acc + bc_ref[:, lc])
    for j in range(SSD_INNER // D):
        sz_ref[:, j * D:(j + 1) * D] = _silu(_wdot(hb, wz_ref, j * D, D))
    raw = jnp.dot(hb, wdt_ref[...], preferred_element_type=F32) + dtb_ref[...]
    lane = lax.broadcasted_iota(jnp.int32, (1, LANES), 1)
    return jnp.where(lane < 2 * SSD_HEADS, _softplus(raw), 0.0)


def _ssd_in(x, seq_len, layer, row_fn, mod, nw, w_z, w_x, w_dt, dt_bias, w_conv, b_conv):
    M = x.shape[0]
    tm = TOKEN_TILE
    HL = SSD_HALO
    tps = seq_len // tm
    n_hblk = M // HL
    row = lambda i: (i, 0)
    in_specs = [pl.BlockSpec((tm, D), row)]
    args = [x]
    if tps > 1:
        in_specs += [pl.BlockSpec((HL, D), lambda i: (jnp.maximum(i * (tm // HL) - 1, 0), 0)),
                     pl.BlockSpec((HL, D), lambda i: (jnp.minimum((i + 1) * (tm // HL), n_hblk - 1), 0))]
        args += [x, x]
    in_specs += [_const_spec((1, D)), _mod_spec(layer, lambda i: row_fn(i * tm)),
                 _wspec(D, SSD_INNER), _wspec(D, SSD_XBC), _const_spec((D, LANES)), _const_spec((1, LANES)),
                 _const_spec((SSD_CONV + 1, SSD_XBC)), _const_spec((1, SSD_XBC))]
    args += [nw, mod, w_z, w_x, w_dt, dt_bias, w_conv, b_conv]
    return pl.pallas_call(
        functools.partial(_ssd_in_kernel, tps),
        grid=(M // tm,),
        in_specs=in_specs,
        out_specs=[pl.BlockSpec((tm, SSD_INNER), row), pl.BlockSpec((tm, SSD_XBC), row), pl.BlockSpec((tm, LANES), row)],
        out_shape=[jax.ShapeDtypeStruct((M, SSD_INNER), F32), jax.ShapeDtypeStruct((M, SSD_XBC), F32),
                   jax.ShapeDtypeStruct((M, LANES), F32)],
        scratch_shapes=[pltpu.VMEM((SSD_XBC // LANES, tm + 2 * HL, LANES), F32)],
        compiler_params=_cparams(1),
        name="ssd_in",
    )(*args)


def _cumsum_rows(a, reverse):
    n = a.shape[0]
    row = lax.broadcasted_iota(jnp.int32, a.shape, 0)
    k = 1
    while k < n:
        if reverse:
            a = a + jnp.where(row < n - k, pltpu.roll(a, n - k, 0), 0.0)
        else:
            a = a + jnp.where(row >= k, pltpu.roll(a, k, 0), 0.0)
        k *= 2
    return a


def _ssd_core_kernel(nc, has_h0, emit_state, *refs):
    refs = list(refs)
    dir_refs = [refs[0:2], refs[2:4]]
    alog_ref, dsk_ref = refs[4:6]
    refs = refs[6:]
    if has_h0:
        h0_ref = refs.pop(0)
    y_refs = [refs.pop(0), refs.pop(0)]
    if emit_state:
        so_ref = refs.pop(0)
    (st_ref,) = refs
    Q, N, P = SSD_CHUNK, SSD_STATE, SSD_HEAD_DIM
    c = pl.program_id(1)
    n_blk = SSD_INNER // LANES

    @pl.when(c == 0)
    def _init():
        if has_h0:
            for dr in range(2):
                for j in range(n_blk):
                    st_ref[dr, :, j * LANES:(j + 1) * LANES] = h0_ref[dr, j * LANES:(j + 1) * LANES, :].T
        else:
            st_ref[...] = jnp.zeros(st_ref.shape, F32)

    A = -jnp.exp(alog_ref[...])
    for dr in range(2):
        x_ref, dt_ref = dir_refs[dr]
        y_ref = y_refs[dr]

        def x_at(c0, x_ref=x_ref):
            return x_ref[:, c0:c0 + LANES]

        def y_put(c0, y, y_ref=y_ref):
            y_ref[:, c0:c0 + LANES] = y

        _ssd_chunk(dr, x_at, dt_ref[...], A, st_ref, y_put, dsk_ref)

    if emit_state:
        @pl.when(c == nc - 1)
        def _emit():
            _ssd_emit_state(st_ref, so_ref)


def _ssd_emit_state(st_ref, so_ref):
    for dr in range(2):
        for j in range(SSD_INNER // LANES):
            so_ref[dr, j * LANES:(j + 1) * LANES, :] = st_ref[dr, :, j * LANES:(j + 1) * LANES].T


def _ssd_chunk(dr, x_at, dt, A, st_ref, y_put, dsk_ref):
    Q, N, P = SSD_CHUNK, SSD_STATE, SSD_HEAD_DIM
    lo = _head_masks()
    ri = lax.broadcasted_iota(jnp.int32, (Q, Q), 0)
    ci = lax.broadcasted_iota(jnp.int32, (Q, Q), 1)
    cum = _cumsum_rows(dt * A, reverse=(dr == 1)) * LOG2E
    cum_t = cum.T
    dt_t = dt.T
    tot_row = Q - 1 if dr == 0 else 0
    mask = (ri >= ci) if dr == 0 else (ri <= ci)
    for g in range(SSD_GROUPS):
        bg_t = x_at(SSD_INNER + g * N).T
        cg = x_at(SSD_INNER + SSD_GN + g * N).astype(BF16)
        cb = jnp.dot(cg, bg_t.astype(BF16), preferred_element_type=F32)
        for pp in range(SSD_HEADS // SSD_GROUPS // 2):
            head = g * (SSD_HEADS // SSD_GROUPS) + 2 * pp
            c0 = head * P
            li = dr * SSD_HEADS + head
            xpair = x_at(c0)
            xpair_b = xpair.astype(BF16)
            cum_cols = [cum[:, li:li + 1], cum[:, li + 1:li + 2]]
            cum_x = jnp.where(lo, cum_cols[0], cum_cols[1])
            yd, sd = [], []
            for hh in range(2):
                dt_row = dt_t[li + hh:li + hh + 1, :]
                cum_row = cum_t[li + hh:li + hh + 1, :]
                decay = jnp.exp2(jnp.where(mask, cum_cols[hh] - cum_row, -jnp.inf))
                yd.append(jnp.dot((cb * decay * dt_row).astype(BF16), xpair_b, preferred_element_type=F32))
                to_end = jnp.exp2(cum_row[:, tot_row:tot_row + 1] - cum_row)
                sd.append(jnp.dot((bg_t * (dt_row * to_end)).astype(BF16), xpair_b, preferred_element_type=F32))
            st = st_ref[dr, :, c0:c0 + LANES]
            y = (jnp.where(lo, yd[0], yd[1])
                 + jnp.dot(cg, st.astype(BF16), preferred_element_type=F32) * jnp.exp2(cum_x))
            if dr == 0:
                y = y + xpair * dsk_ref[:, c0:c0 + LANES]
            y_put(c0, y)
            tot = cum_x[tot_row:tot_row + 1, :]
            st_ref[dr, :, c0:c0 + LANES] = st * jnp.exp2(tot) + jnp.where(lo, sd[0], sd[1])


def _ssd_seq_kernel(x_ref, nw_ref, mod_ref, wz_ref, wx_ref, wdt_ref, dtb_ref, wc_ref, bc_ref, alog_ref, dsk_ref, gnw_ref, wo_ref,
                    o_ref, so_ref, xpad_ref, xbc_ref, sz_ref, y_ref, st_ref):
    Q = SSD_CHUNK
    nc = x_ref.shape[0] // Q
    x = x_ref[...]
    hb = _modnorm(x, nw_ref[...], mod_ref[...]).astype(BF16)
    dt = _ssd_project(hb, None, wz_ref, wx_ref, wdt_ref, dtb_ref, wc_ref, bc_ref, xpad_ref, sz_ref, xbc_ref)
    st_ref[...] = jnp.zeros(st_ref.shape, F32)
    A = -jnp.exp(alog_ref[...])
    for dr in range(2):
        for ck in (range(nc) if dr == 0 else reversed(range(nc))):
            rows = slice(ck * Q, (ck + 1) * Q)

            def x_at(c0, rows=rows):
                return xbc_ref[rows, c0:c0 + LANES]

            def y_put(c0, y, rows=rows, dr=dr):
                if dr == 0:
                    y_ref[rows, c0:c0 + LANES] = y
                else:
                    y_ref[rows, c0:c0 + LANES] += y

            _ssd_chunk(dr, x_at, dt[rows], A, st_ref, y_put, dsk_ref)
    _ssd_emit_state(st_ref, so_ref)
    g = _rms(y_ref[...] * sz_ref[...], gnw_ref[...]).astype(BF16)
    o_ref[...] = x + mod_ref[:, 2 * D:3 * D] * _wdot(g, wo_ref, 0, D)


def _ssd_seq(x, seq_len, layer, row_fn, mod, nw, w_z, w_x, w_dt, dt_bias, w_conv, b_conv, a_log, d_skip, norm_w, w_out):
    M = x.shape[0]
    n_seq = M // seq_len
    HL = SSD_HALO
    row = lambda i: (i, 0)
    st_blk = pl.BlockSpec((None, 2, SSD_INNER, SSD_STATE), lambda i: (i, 0, 0, 0))
    return pl.pallas_call(
        _ssd_seq_kernel,
        grid=(n_seq,),
        in_specs=[pl.BlockSpec((seq_len, D), row), _const_spec((1, D)), _mod_spec(layer, lambda i: row_fn(i * seq_len)),
                  _wspec(D, SSD_INNER), _wspec(D, SSD_XBC), _const_spec((D, LANES)), _const_spec((1, LANES)),
                  _const_spec((SSD_CONV + 1, SSD_XBC)), _const_spec((1, SSD_XBC)),
                  _const_spec((1, LANES)), _const_spec((1, SSD_INNER)), _const_spec((1, SSD_INNER)),
                  _wspec(SSD_INNER, D)],
        out_specs=[pl.BlockSpec((seq_len, D), row), st_blk],
        out_shape=[jax.ShapeDtypeStruct((M, D), F32), jax.ShapeDtypeStruct((n_seq, 2, SSD_INNER, SSD_STATE), F32)],
        scratch_shapes=[pltpu.VMEM((SSD_XBC // LANES, seq_len + 2 * HL, LANES), F32),
                        pltpu.VMEM((seq_len, SSD_XBC), F32), pltpu.VMEM((seq_len, SSD_INNER), F32),
                        pltpu.VMEM((seq_len, SSD_INNER), F32), pltpu.VMEM((2, SSD_STATE, SSD_INNER), F32)],
        compiler_params=_cparams(1),
        name="ssd_seq",
    )(x, nw, mod, w_z, w_x, w_dt, dt_bias, w_conv, b_conv, a_log, d_skip, norm_w, w_out)


def _ssd_core(xbc, dtp, a_log, d_skip, h0, n_seq, seq_len, emit_state):
    Q = SSD_CHUNK
    nc = seq_len // Q
    M = n_seq * seq_len

    def chunk(dr):
        return (lambda b, c: b * nc + c) if dr == 0 else (lambda b, c: b * nc + nc - 1 - c)

    in_specs, args = [], []
    for dr in range(2):
        ch = chunk(dr)
        in_specs += [pl.BlockSpec((Q, SSD_XBC), lambda b, c, ch=ch: (ch(b, c), 0)),
                     pl.BlockSpec((Q, LANES), lambda b, c, ch=ch: (ch(b, c), 0))]
        args += [xbc, dtp]
    in_specs += [_const_spec((1, LANES)), _const_spec((1, SSD_INNER))]
    args += [a_log, d_skip]
    st_blk = pl.BlockSpec((None, 2, SSD_INNER, SSD_STATE), lambda b, c: (b, 0, 0, 0))
    has_h0 = h0 is not None
    if has_h0:
        in_specs.append(st_blk)
        args.append(h0)
    out_specs = [pl.BlockSpec((Q, SSD_INNER), lambda b, c: (chunk(0)(b, c), 0)),
                 pl.BlockSpec((Q, SSD_INNER), lambda b, c: (chunk(1)(b, c), 0))]
    out_shape = [jax.ShapeDtypeStruct((M, SSD_INNER), F32)] * 2
    if emit_state:
        out_specs.append(st_blk)
        out_shape.append(jax.ShapeDtypeStruct((n_seq, 2, SSD_INNER, SSD_STATE), F32))
    return pl.pallas_call(
        functools.partial(_ssd_core_kernel, nc, has_h0, emit_state),
        grid=(n_seq, nc),
        in_specs=in_specs,
        out_specs=out_specs,
        out_shape=out_shape,
        scratch_shapes=[pltpu.VMEM((2, SSD_STATE, SSD_INNER), F32)],
        compiler_params=_cparams(2),
        name="ssd_core",
    )(*args)


def _ssd_out_kernel(yf_ref, yb_ref, sz_ref, x_ref, mod_ref, nw_ref, w_ref, o_ref):
    g = _rms((yf_ref[...] + yb_ref[...]) * sz_ref[...], nw_ref[...]).astype(BF16)
    o_ref[...] = x_ref[...] + mod_ref[:, 2 * D:3 * D] * _wdot(g, w_ref, 0, D)


def _ssd_out(yf, yb, sz, x, layer, row_fn, mod, norm_w, w_out):
    M = x.shape[0]
    tm = TOKEN_TILE
    row = lambda i: (i, 0)
    wide = pl.BlockSpec((tm, SSD_INNER), row)
    return pl.pallas_call(
        _ssd_out_kernel,
        grid=(M // tm,),
        in_specs=[wide, wide, wide, pl.BlockSpec((tm, D), row), _mod_spec(layer, lambda i: row_fn(i * tm)),
                  _const_spec((1, SSD_INNER)), _wspec(SSD_INNER, D)],
        out_specs=pl.BlockSpec((tm, D), row),
        out_shape=jax.ShapeDtypeStruct((M, D), F32),
        compiler_params=_cparams(1),
        name="ssd_out",
    )(yf, yb, sz, x, mod, norm_w, w_out)


def _ssd_layer(xc, xl, layer, mod, nw, p, n_lat):
    j = layer // 3
    w_b = p["ssd_w_in"].astype(BF16)
    w_z = _wsplit(w_b, j, col0=0, ncols=SSD_INNER)
    w_x = _wsplit(w_b, j, col0=SSD_INNER, ncols=SSD_XBC)
    w_dt = jnp.pad(w_b[j, :, SSD_IN_COLS:], ((0, 0), (0, LANES - 2 * SSD_HEADS)))
    dt_bias = jnp.pad(p["ssd_dt_bias"][j].reshape(1, -1), ((0, 0), (0, LANES - 2 * SSD_HEADS)))
    a_log = jnp.pad(p["ssd_a_log"][j].reshape(1, -1), ((0, 0), (0, LANES - 2 * SSD_HEADS)))
    w_conv = jnp.pad(p["ssd_w_conv"][j], ((0, 1), (0, 0)))
    b_conv = p["ssd_b_conv"][j][None]
    d_skip = jnp.repeat(p["ssd_d"][j], SSD_HEAD_DIM)[None]
    norm_w = p["ssd_norm_w"][j][None]
    w_out = _wsplit(p["ssd_w_out"], j)
    row_ctx = lambda tok: 0
    row_lat = lambda tok: 1 + tok // DEC_SEQ
    oc, s_new = _ssd_seq(xc, SEQ, layer, row_ctx, mod, nw, w_z, w_x, w_dt, dt_bias, w_conv, b_conv, a_log, d_skip, norm_w, w_out)
    szl, xbc_l, dt_l = _ssd_in(xl, DEC_SEQ, layer, row_lat, mod, nw, w_z, w_x, w_dt, dt_bias, w_conv, b_conv)
    h0 = p["state_ssm"][:n_lat, j].reshape(n_lat, 2, SSD_INNER, SSD_STATE)
    yfl, ybl = _ssd_core(xbc_l, dt_l, a_log, d_skip, h0, n_lat, DEC_SEQ, False)
    ol = _ssd_out(yfl, ybl, szl, xl, layer, row_lat, mod, norm_w, w_out)
    return oc, ol, s_new


def kernel(x_prompt, x_sample, cache_k, cache_v, state_ssm, c, c_ctx, ada_w, ada_b, norm_w, final_norm_w, conv_w_in, conv_b_in, conv_w_dw, conv_b_dw, conv_ln_w, conv_ln_b, conv_w_out, conv_b_out, na_w_in, na_rpb, na_w_out, ssd_w_in, ssd_w_conv, ssd_b_conv, ssd_a_log, ssd_dt_bias, ssd_d, ssd_norm_w, ssd_w_out):
    p = dict(cache_k=cache_k, cache_v=cache_v, state_ssm=state_ssm,
             conv_w_in=conv_w_in, conv_b_in=conv_b_in, conv_w_dw=conv_w_dw, conv_b_dw=conv_b_dw,
             conv_ln_w=conv_ln_w, conv_ln_b=conv_ln_b, conv_w_out=conv_w_out, conv_b_out=conv_b_out,
             na_w_in=na_w_in, na_rpb=na_rpb, na_w_out=na_w_out,
             ssd_w_in=ssd_w_in, ssd_w_conv=ssd_w_conv, ssd_b_conv=ssd_b_conv, ssd_a_log=ssd_a_log,
             ssd_dt_bias=ssd_dt_bias, ssd_d=ssd_d, ssd_norm_w=ssd_norm_w, ssd_w_out=ssd_w_out)
    n_ctx, n_lat = x_prompt.shape[0], x_sample.shape[0]
    assert x_prompt.shape[1:] == (SEQ, D) and x_sample.shape[1:] == (DEC_SEQ, D)
    assert (DEPTH - 1) % 3 == 0, "the final RMSNorm is fused into a trailing Conformer layer"
    xc = x_prompt.reshape(n_ctx * SEQ, D)
    xl = x_sample.reshape(n_lat * DEC_SEQ, D)
    cond8 = jnp.concatenate([c_ctx[None], c, jnp.zeros((8 - 1 - n_lat, D), F32)], axis=0)
    mod = _modulation(cond8, ada_w, ada_b).reshape(DEPTH, 8, 1, 3 * D)
    row_ctx = lambda tok: 0
    row_lat = lambda tok: 1 + tok // DEC_SEQ
    new_k, new_v, new_s = [], [], []
    for i in range(DEPTH):
        kind, j = i % 3, i // 3
        nw = norm_w[i][None]
        if kind == 0:
            cp = _prep_conv(p, j)
            fw = final_norm_w[None] if i == DEPTH - 1 else None
            xc = _conv_layer(xc, SEQ, i, row_ctx, mod, nw, *cp, fw)
            xl = _conv_layer(xl, DEC_SEQ, i, row_lat, mod, nw, *cp, fw)
        elif kind == 1:
            xc, xl, k_c, v_c = _na_layer(xc, xl, i, mod, nw, p, n_lat)
            new_k.append(k_c.reshape(n_ctx, SEQ, NA_HEADS, NA_HEAD_DIM))
            new_v.append(v_c.reshape(n_ctx, SEQ, NA_HEADS, NA_HEAD_DIM))
        else:
            xc, xl, s_c = _ssd_layer(xc, xl, i, mod, nw, p, n_lat)
            new_s.append(s_c.reshape(n_ctx, 2, SSD_HEADS, SSD_HEAD_DIM, SSD_STATE))
    return (xc.reshape(n_ctx, SEQ, D), xl.reshape(n_lat, DEC_SEQ, D),
            jnp.stack(new_k, axis=1), jnp.stack(new_v, axis=1), jnp.stack(new_s, axis=1))
```

```python
import functools

import jax
import jax.numpy as jnp
from jax import lax
from jax.experimental import pallas as pl
from jax.experimental.pallas import tpu as pltpu

F32 = jnp.float32
BF16 = jnp.bfloat16

D = 1024
DEPTH = 4
SEQ = 256
DEC_SEQ = 2048
GRID_W = 64
RMS_EPS = 1e-6
LN_EPS = 1e-5
LOG2E = 1.4426950408889634
CONF_K = 31
NA_HEADS = 16
NA_HEAD_DIM = 64
NA_KH = 8
NA_KW = 16
NEG_INF = -1e30
SSD_INNER = 2048
SSD_HEADS = 32
SSD_HEAD_DIM = 64
SSD_STATE = 128
SSD_GROUPS = 4
SSD_GN = SSD_GROUPS * SSD_STATE
SSD_CONV = 7
SSD_CHUNK = 128
SSD_XBC = SSD_INNER + 2 * SSD_GN

LANES = 128
SUBLANES = 8
TOKEN_TILE = 256
W_CHUNK = 512
CONV_HALO = 16
SSD_HALO = 8
VMEM_LIMIT = 56 * 1024 * 1024


def _cparams(n_axes):
    return pltpu.CompilerParams(dimension_semantics=("arbitrary",) * n_axes,
                                vmem_limit_bytes=VMEM_LIMIT)


def _const_spec(shape):
    nd = len(shape)
    return pl.BlockSpec(shape, lambda *_: (0,) * nd, pipeline_mode=pl.Buffered(1))


class _W:
    def __init__(self, w, ncols=None, col0=0):
        self.w, self.col0 = w, col0
        self.ncols = w.shape[1] if ncols is None else ncols


def _operands(items):
    specs, args, groups = [], [], []
    for it in items:
        if isinstance(it, _W):
            n, b0, K = it.ncols // W_CHUNK, it.col0 // W_CHUNK, it.w.shape[0]
            for c in range(b0, b0 + n):
                specs.append(pl.BlockSpec((K, W_CHUNK), lambda *_, c=c: (0, c), pipeline_mode=pl.Buffered(1)))
                args.append(it.w)
            groups.append(n)
        else:
            specs.append(it[0])
            args.append(it[1])
            groups.append(None)
    return specs, args, groups


def _grouped(body, groups):
    def kernel_fn(*refs):
        refs = list(refs)
        packed = []
        for g in groups:
            if g is None:
                packed.append(refs.pop(0))
            else:
                packed.append(tuple(refs[:g]))
                del refs[:g]
        return body(*packed, *refs)
    return kernel_fn


def _wdot(h, w_refs, c0, n):
    outs = []
    c = c0
    while c < c0 + n:
        j, off = divmod(c, W_CHUNK)
        m = min(W_CHUNK - off, c0 + n - c)
        outs.append(jnp.dot(h, w_refs[j][:, off:off + m], preferred_element_type=F32))
        c += m
    return outs[0] if len(outs) == 1 else jnp.concatenate(outs, axis=1)


def _sigmoid(x):
    return 1.0 / (1.0 + jnp.exp(-x))


def _silu(x):
    return x * _sigmoid(x)


def _rms(x, w):
    ms = jnp.mean(x * x, axis=-1, keepdims=True)
    return x * lax.rsqrt(ms + RMS_EPS) * w


def _modnorm(x, nw, mod):
    return _rms(x, nw) * (1.0 + mod[:, D:2 * D]) + mod[:, :D]


def _mod_spec(layer, row_fn):
    return pl.BlockSpec((None, None, 1, 3 * D), lambda i, *_: (layer, row_fn(i), 0, 0))


def _mod_kernel(cond_ref, w_ref, b_ref, o_ref):
    s = _silu(cond_ref[...]).astype(BF16)
    o_ref[...] = jnp.dot(s, w_ref[...].astype(BF16), preferred_element_type=F32) + b_ref[...]


def _modulation(cond8, ada_w, ada_b):
    tn = 1024
    return pl.pallas_call(
        _mod_kernel,
        grid=(DEPTH, 3 * D // tn),
        in_specs=[pl.BlockSpec((8, D), lambda i, j: (0, 0)),
                  pl.BlockSpec((None, D, tn), lambda i, j: (i, 0, j)),
                  pl.BlockSpec((None, 1, tn), lambda i, j: (i, 0, j))],
        out_specs=pl.BlockSpec((None, 8, tn), lambda i, j: (i, 0, j)),
        out_shape=jax.ShapeDtypeStruct((DEPTH, 8, 3 * D), F32),
        compiler_params=_cparams(2),
        name="adaln_mod",
    )(cond8, ada_w, ada_b.reshape(DEPTH, 1, 3 * D))


CONV_CHUNK = 256


def _conv_kernel(tiles_per_seq, final, *refs):
    refs = list(refs)
    x_ref = refs.pop(0)
    if tiles_per_seq > 1:
        xp_ref = refs.pop(0)
        xn_ref = refs.pop(0)
    nw_ref, mod_ref, wi_ref, bi_ref, wdw_ref, bdw_ref, lnw_ref, lnb_ref, wo_ref, bo_ref = refs[:10]
    refs = refs[10:]
    if final:
        fw_ref = refs.pop(0)
    o_ref, ubuf, cbuf = refs
    T = x_ref.shape[0]
    H = CONV_HALO
    RB = 128

    hb = _modnorm(x_ref[...], nw_ref[...], mod_ref[...]).astype(BF16)
    if tiles_per_seq > 1:
        t = pl.program_id(0) % tiles_per_seq
        hh = _modnorm(jnp.concatenate([xp_ref[...], xn_ref[...]], axis=0), nw_ref[...], mod_ref[...]).astype(BF16)

    def glu(h, cols):
        v = _wdot(h, wi_ref, cols.start, CONV_CHUNK) + bi_ref[:, cols]
        gcols = slice(D + cols.start, D + cols.stop)
        g = _wdot(h, wi_ref, gcols.start, CONV_CHUNK) + bi_ref[:, gcols]
        return v * _sigmoid(g)

    for ch in range(D // CONV_CHUNK):
        cols = slice(ch * CONV_CHUNK, (ch + 1) * CONV_CHUNK)
        u_main = glu(hb, cols)
        if tiles_per_seq > 1:
            uh = glu(hh, cols)
            u_prev = jnp.where(t > 0, uh[0:H], 0.0)
            u_next = jnp.where(t < tiles_per_seq - 1, uh[H:2 * H], 0.0)
        else:
            u_prev = u_next = jnp.zeros((H, CONV_CHUNK), F32)
        for cb in range(CONV_CHUNK // LANES):
            blk = ch * (CONV_CHUNK // LANES) + cb
            lc = slice(blk * LANES, (blk + 1) * LANES)
            cc = slice(cb * LANES, (cb + 1) * LANES)
            ubuf[blk, 0:H, :] = u_prev[:, cc]
            ubuf[blk, H:H + T, :] = u_main[:, cc]
            ubuf[blk, H + T:H + T + H, :] = u_next[:, cc]
            for rb in range(T // RB):
                acc = jnp.zeros((RB, LANES), F32)
                for k in range(CONF_K):
                    r0 = rb * RB + H - CONF_K // 2 + k
                    acc = acc + ubuf[blk, r0:r0 + RB, :] * wdw_ref[k:k + 1, lc]
                cbuf[rb * RB:(rb + 1) * RB, lc] = acc + bdw_ref[:, lc]

    sz = _silu(_wdot(hb, wi_ref, 2 * D, D) + bi_ref[:, 2 * D:3 * D])
    c = cbuf[...]
    mu = jnp.mean(c, axis=-1, keepdims=True)
    xc = c - mu
    var = jnp.mean(xc * xc, axis=-1, keepdims=True)
    ln = xc * lax.rsqrt(var + LN_EPS) * lnw_ref[...] + lnb_ref[...]
    g = (_silu(ln) * sz).astype(BF16)
    y = _wdot(g, wo_ref, 0, D) + bo_ref[...]
    xn = x_ref[...] + mod_ref[:, 2 * D:3 * D] * y
    if final:
        xn = _rms(xn, fw_ref[...])
    o_ref[...] = xn


def _conv_layer(x, seq_len, layer, row_fn, mod, nw, w_in, b_in, w_dw, b_dw, ln_w, ln_b, w_out, b_out, final_w):
    M = x.shape[0]
    T = TOKEN_TILE
    tps = seq_len // T
    H = CONV_HALO
    n_halo_blocks = M // H
    row = lambda i: (i, 0)
    items = [(pl.BlockSpec((T, D), row), x)]
    if tps > 1:
        items += [(pl.BlockSpec((H, D), lambda i: (jnp.maximum(i * (T // H) - 1, 0), 0)), x),
                  (pl.BlockSpec((H, D), lambda i: (jnp.minimum((i + 1) * (T // H), n_halo_blocks - 1), 0)), x)]
    items += [(_const_spec((1, D)), nw), (_mod_spec(layer, lambda i: row_fn(i * T)), mod),
              _W(w_in), (_const_spec((1, 3 * D)), b_in),
              (_const_spec((CONF_K + 1, D)), w_dw), (_const_spec((1, D)), b_dw),
              (_const_spec((1, D)), ln_w), (_const_spec((1, D)), ln_b),
              _W(w_out), (_const_spec((1, D)), b_out)]
    final = final_w is not None
    if final:
        items.append((_const_spec((1, D)), final_w))
    in_specs, args, groups = _operands(items)
    return pl.pallas_call(
        _grouped(functools.partial(_conv_kernel, tps, final), groups),
        grid=(M // T,),
        in_specs=in_specs,
        out_specs=pl.BlockSpec((T, D), row),
        out_shape=jax.ShapeDtypeStruct((M, D), F32),
        scratch_shapes=[pltpu.VMEM((D // LANES, T + 2 * H, LANES), F32), pltpu.VMEM((T, D), F32)],
        compiler_params=_cparams(1),
        name="conv_layer",
    )(*args)


def _prep_conv(p, j):
    w_dw = jnp.pad(p["conv_w_dw"][j], ((0, 1), (0, 0)))
    return (p["conv_w_in"][j].astype(BF16), p["conv_b_in"][j][None], w_dw, p["conv_b_dw"][j][None],
            p["conv_ln_w"][j][None], p["conv_ln_b"][j][None], p["conv_w_out"][j].astype(BF16), p["conv_b_out"][j][None])


def _out_proj_kernel(g_ref, x_ref, mod_ref, w_ref, o_ref):
    y = _wdot(g_ref[...].astype(BF16), w_ref, 0, D)
    o_ref[...] = x_ref[...] + mod_ref[:, 2 * D:3 * D] * y


def _out_proj(g, x, layer, row_fn, mod, w_out):
    M, K = g.shape
    tm = TOKEN_TILE
    row = lambda i: (i, 0)
    in_specs, args, groups = _operands([
        (pl.BlockSpec((tm, K), row), g), (pl.BlockSpec((tm, D), row), x),
        (_mod_spec(layer, lambda i: row_fn(i * tm)), mod), _W(w_out)])
    return pl.pallas_call(
        _grouped(_out_proj_kernel, groups),
        grid=(M // tm,),
        in_specs=in_specs,
        out_specs=pl.BlockSpec((tm, D), row),
        out_shape=jax.ShapeDtypeStruct((M, D), F32),
        compiler_params=_cparams(1),
        name="out_proj",
    )(*args)


NA_SCALE = NA_HEAD_DIM ** -0.5
NA_ROWS = 32
NA_WIN = NA_KH * GRID_W
NT_DIMS = (((1,), (1,)), ((), ()))


def _na_in_kernel(x_ref, nw_ref, mod_ref, w_ref, q_ref, k_ref, v_ref, sz_ref):
    hb = _modnorm(x_ref[...], nw_ref[...], mod_ref[...]).astype(BF16)
    q_ref[...] = _wdot(hb, w_ref, 0, D).astype(q_ref.dtype)
    k_ref[...] = _wdot(hb, w_ref, D, D).astype(k_ref.dtype)
    v_ref[...] = _wdot(hb, w_ref, 2 * D, D).astype(v_ref.dtype)
    sz_ref[...] = _silu(_wdot(hb, w_ref, 3 * D, D))


def _na_in(x, layer, row_fn, mod, nw, w_in, kv_dtype):
    M = x.shape[0]
    tm = TOKEN_TILE
    row = lambda i: (i, 0)
    blk = pl.BlockSpec((tm, D), row)
    in_specs, args, groups = _operands([
        (blk, x), (_const_spec((1, D)), nw), (_mod_spec(layer, lambda i: row_fn(i * tm)), mod), _W(w_in)])
    return pl.pallas_call(
        _grouped(_na_in_kernel, groups),
        grid=(M // tm,),
        in_specs=in_specs,
        out_specs=[blk, blk, blk, blk],
        out_shape=[jax.ShapeDtypeStruct((M, D), BF16), jax.ShapeDtypeStruct((M, D), kv_dtype),
                   jax.ShapeDtypeStruct((M, D), kv_dtype), jax.ShapeDtypeStruct((M, D), F32)],
        compiler_params=_cparams(1),
        name="na_in",
    )(*args)


def _head_masks():
    lane = lax.broadcasted_iota(jnp.int32, (1, LANES), 1)
    return lane < NA_HEAD_DIM


def _na_seq_kernel(x_ref, nw_ref, mod_ref, wi_ref, wo_ref, o_ref, k_ref, v_ref, g_ref):
    n = x_ref.shape[0]
    lo = _head_masks()
    x = x_ref[...]
    hb = _modnorm(x, nw_ref[...], mod_ref[...]).astype(BF16)
    hpc = CONV_CHUNK // NA_HEAD_DIM
    for ch in range(D // CONV_CHUNK):
        cols = slice(ch * CONV_CHUNK, (ch + 1) * CONV_CHUNK)
        q = _wdot(hb, wi_ref, cols.start, CONV_CHUNK).astype(BF16)
        k = _wdot(hb, wi_ref, D + cols.start, CONV_CHUNK)
        v = _wdot(hb, wi_ref, 2 * D + cols.start, CONV_CHUNK)
        sz = _silu(_wdot(hb, wi_ref, 3 * D + cols.start, CONV_CHUNK))
        k_ref[:, ch * hpc:(ch + 1) * hpc, :] = k.reshape(n, hpc, NA_HEAD_DIM)
        v_ref[:, ch * hpc:(ch + 1) * hpc, :] = v.reshape(n, hpc, NA_HEAD_DIM)
        kb = k.astype(BF16)
        vb = v.astype(BF16)
        for pp in range(CONV_CHUNK // LANES):
            pc = slice(pp * LANES, (pp + 1) * LANES)
            qp = q[:, pc]
            zero = jnp.zeros_like(qp)
            qs = jnp.concatenate([jnp.where(lo, qp, zero), jnp.where(lo, zero, qp)], axis=0)
            s = lax.dot_general(qs, kb[:, pc], NT_DIMS, preferred_element_type=F32) * NA_SCALE
            e = jnp.exp(s - jnp.max(s, axis=-1, keepdims=True))
            l = jnp.sum(e, axis=-1, keepdims=True)
            o = jnp.dot(e.astype(BF16), vb[:, pc], preferred_element_type=F32) / l
            g = jnp.where(lo, o[:n], o[n:]) * sz[:, pc]
            g_ref[:, cols.start + pp * LANES:cols.start + (pp + 1) * LANES] = g.astype(BF16)
    y = _wdot(g_ref[...], wo_ref, 0, D)
    o_ref[...] = x + mod_ref[:, 2 * D:3 * D] * y


def _na_seq(x, seq_len, layer, row_fn, mod, nw, w_in, w_out):
    M = x.shape[0]
    row = lambda i: (i, 0)
    kv_blk = pl.BlockSpec((seq_len, NA_HEADS, NA_HEAD_DIM), lambda i: (i, 0, 0))
    kv_shape = jax.ShapeDtypeStruct((M, NA_HEADS, NA_HEAD_DIM), F32)
    in_specs, args, groups = _operands([
        (pl.BlockSpec((seq_len, D), row), x), (_const_spec((1, D)), nw),
        (_mod_spec(layer, lambda i: row_fn(i * seq_len)), mod), _W(w_in), _W(w_out)])
    return pl.pallas_call(
        _grouped(_na_seq_kernel, groups),
        grid=(M // seq_len,),
        in_specs=in_specs,
        out_specs=[pl.BlockSpec((seq_len, D), row), kv_blk, kv_blk],
        out_shape=[jax.ShapeDtypeStruct((M, D), F32), kv_shape, kv_shape],
        scratch_shapes=[pltpu.VMEM((seq_len, D), BF16)],
        compiler_params=_cparams(1),
        name="na_seq",
    )(*args)


def _rpb_kernel(rpb_ref, o_ref):
    h = pl.program_id(0)
    c = lax.broadcasted_iota(jnp.int32, (GRID_W, LANES), 0)
    lane = lax.broadcasted_iota(jnp.int32, (GRID_W, LANES), 1)
    kc = lane & (GRID_W - 1)
    upper = lax.broadcasted_iota(jnp.int32, (1, LANES), 1) >= GRID_W
    cs = jnp.clip(c - NA_KW // 2, 0, GRID_W - NA_KW)
    valid = (kc >= cs) & (kc < cs + NA_KW)
    n_pair = 2 * NA_KH - 2

    def toeplitz(dri, lane0):
        row = rpb_ref[pl.ds(h * (2 * NA_KH - 1) + dri, 1), :]
        rows = jnp.broadcast_to(row, (GRID_W, LANES))
        return pltpu.roll(rows, (LANES - (NA_KW - 1) + lane0) % LANES, 1, stride=1, stride_axis=0)

    for dri in range(n_pair):
        tile = jnp.where(upper, toeplitz(dri + 1, GRID_W), toeplitz(dri, 0))
        tile = jnp.where(valid, tile, NEG_INF)
        for idx in range(NA_KH):
            jj2 = dri - (NA_KH - 1) + idx
            if jj2 % 2 == 0 and 0 <= jj2 // 2 < NA_KH // 2:
                jj = jj2 // 2
                o_ref[idx, :, jj * LANES:(jj + 1) * LANES] = tile


def _rpb_table(rpb):
    n_rows = NA_HEADS * (2 * NA_KH - 1)
    rows = jnp.pad(rpb.reshape(n_rows, 2 * NA_KW - 1), ((0, 0), (0, LANES - (2 * NA_KW - 1))))
    return pl.pallas_call(
        _rpb_kernel,
        grid=(NA_HEADS,),
        in_specs=[_const_spec((n_rows, LANES))],
        out_specs=pl.BlockSpec((NA_KH, None, GRID_W, NA_WIN), lambda h: (0, h, 0, 0)),
        out_shape=jax.ShapeDtypeStruct((NA_KH, NA_HEADS, GRID_W, NA_WIN), F32),
        compiler_params=_cparams(1),
        name="na_rpb_table",
    )(rows)


def _na_row_start(r):
    return jnp.clip(r - NA_KH // 2, 0, NA_ROWS - NA_KH)


def _na_lat_kernel(q_ref, k_ref, v_ref, ck_ref, cv_ref, bias_ref, sz_ref, o_ref):
    lo = _head_masks()
    k0 = pl.multiple_of(_na_row_start(pl.program_id(1)) * GRID_W, GRID_W)
    for p in range(NA_HEADS // 2):
        cols = slice(p * LANES, (p + 1) * LANES)
        qp = q_ref[:, cols]
        kw = k_ref[pl.ds(k0, NA_WIN), cols]
        vw = v_ref[pl.ds(k0, NA_WIN), cols]
        ck = ck_ref[:, cols]
        cv = cv_ref[:, cols]
        zero = jnp.zeros_like(qp)
        qs = jnp.concatenate([jnp.where(lo, qp, zero), jnp.where(lo, zero, qp)], axis=0)
        bias = bias_ref[2 * p:2 * p + 2].reshape(2 * GRID_W, NA_WIN)
        s_loc = lax.dot_general(qs, kw, NT_DIMS, preferred_element_type=F32) * NA_SCALE + bias
        s_ctx = lax.dot_general(qs, ck, NT_DIMS, preferred_element_type=F32) * NA_SCALE
        m = jnp.maximum(jnp.max(s_loc, axis=-1, keepdims=True), jnp.max(s_ctx, axis=-1, keepdims=True))
        e_loc = jnp.exp(s_loc - m)
        e_ctx = jnp.exp(s_ctx - m)
        l = jnp.sum(e_loc, axis=-1, keepdims=True) + jnp.sum(e_ctx, axis=-1, keepdims=True)
        o = (jnp.dot(e_loc.astype(BF16), vw, preferred_element_type=F32)
             + jnp.dot(e_ctx.astype(BF16), cv, preferred_element_type=F32)) / l
        o_ref[:, cols] = jnp.where(lo, o[:GRID_W], o[GRID_W:]) * sz_ref[:, cols]


def _na_lat(q, k, v, ck, cv, bias, sz, n_lat):
    T = NA_ROWS * GRID_W
    past = ck.shape[1]
    qblk = pl.BlockSpec((GRID_W, D), lambda b, r: (b * NA_ROWS + r, 0))
    kvblk = pl.BlockSpec((T, D), lambda b, r: (b, 0))
    cblk = pl.BlockSpec((None, past, D), lambda b, r: (b, 0, 0))
    bblk = pl.BlockSpec((None, NA_HEADS, GRID_W, NA_WIN), lambda b, r: (r - _na_row_start(r), 0, 0, 0))
    return pl.pallas_call(
        _na_lat_kernel,
        grid=(n_lat, NA_ROWS),
        in_specs=[qblk, kvblk, kvblk, cblk, cblk, bblk, qblk],
        out_specs=qblk,
        out_shape=jax.ShapeDtypeStruct((n_lat * T, D), F32),
        compiler_params=_cparams(2),
        name="na_lat_attn",
    )(q, k, v, ck, cv, bias, sz)


def _na_layer(xc, xl, layer, mod, nw, p, n_lat):
    j = layer // 3
    w_in = p["na_w_in"][j].astype(BF16)
    w_out = p["na_w_out"][j].astype(BF16)
    row_ctx = lambda tok: 0
    row_lat = lambda tok: 1 + tok // (NA_ROWS * GRID_W)
    oc, kc, vc = _na_seq(xc, SEQ, layer, row_ctx, mod, nw, w_in, w_out)
    ql, kl, vl, szl = _na_in(xl, layer, row_lat, mod, nw, w_in, BF16)
    past = p["cache_k"].shape[2]
    ck = p["cache_k"][:n_lat, j].reshape(n_lat, past, D).astype(BF16)
    cv = p["cache_v"][:n_lat, j].reshape(n_lat, past, D).astype(BF16)
    gl = _na_lat(ql, kl, vl, ck, cv, _rpb_table(p["na_rpb"][j]), szl, n_lat)
    return oc, _out_proj(gl, xl, layer, row_lat, mod, w_out), kc, vc


SSD_IN_COLS = SSD_INNER + SSD_XBC


def _softplus(x):
    return jnp.maximum(x, 0.0) + jnp.log(1.0 + jnp.exp(-jnp.abs(x)))


def _ssd_in_kernel(tiles_per_seq, *refs):
    refs = list(refs)
    x_ref = refs.pop(0)
    if tiles_per_seq > 1:
        xp_ref = refs.pop(0)
        xn_ref = refs.pop(0)
    nw_ref, mod_ref, wz_ref, wx_ref, wdt_ref, dtb_ref, wc_ref, bc_ref, sz_ref, xbc_ref, dt_ref, xpad_ref = refs
    hb = _modnorm(x_ref[...], nw_ref[...], mod_ref[...]).astype(BF16)
    halo = None
    if tiles_per_seq > 1:
        t = pl.program_id(0) % tiles_per_seq
        xh = jnp.concatenate([xp_ref[...], xn_ref[...]], axis=0)
        halo = (_modnorm(xh, nw_ref[...], mod_ref[...]).astype(BF16), t > 0, t < tiles_per_seq - 1)
    dt_ref[...] = _ssd_project(hb, halo, wz_ref, wx_ref, wdt_ref, dtb_ref, wc_ref, bc_ref, xpad_ref, sz_ref, xbc_ref)


def _ssd_project(hb, halo, wz_ref, wx_ref, wdt_ref, dtb_ref, wc_ref, bc_ref, xpad_ref, sz_ref, xbc_ref):
    tm = hb.shape[0]
    HL = SSD_HALO
    RB = 128
    for j in range(SSD_XBC // D):
        c0 = j * D
        raw = _wdot(hb, wx_ref, c0, D)
        if halo is not None:
            rawh = _wdot(halo[0], wx_ref, c0, D)
            raw_prev = jnp.where(halo[1], rawh[0:HL], 0.0)
            raw_next = jnp.where(halo[2], rawh[HL:2 * HL], 0.0)
        else:
            raw_prev = raw_next = jnp.zeros((HL, D), F32)
        for cb in range(D // LANES):
            blk = j * (D // LANES) + cb
            lc = slice(blk * LANES, (blk + 1) * LANES)
            cc = slice(cb * LANES, (cb + 1) * LANES)
            xpad_ref[blk, 0:HL, :] = raw_prev[:, cc]
            xpad_ref[blk, HL:HL + tm, :] = raw[:, cc]
            xpad_ref[blk, HL + tm:HL + tm + HL, :] = raw_next[:, cc]
            for rb in range(tm // RB):
                acc = jnp.zeros((RB, LANES), F32)
                for k in range(SSD_CONV):
                    r0 = rb * RB + HL - SSD_CONV // 2 + k
                    acc = acc + xpad_ref[blk, r0:r0 + RB, :] * wc_ref[k:k + 1, lc]
                xbc_ref[rb * RB:(rb + 1) * RB, lc] = _silu(acc + bc_ref[:, lc])
    for j in range(SSD_INNER // D):
        sz_ref[:, j * D:(j + 1) * D] = _silu(_wdot(hb, wz_ref, j * D, D))
    raw = jnp.dot(hb, wdt_ref[...], preferred_element_type=F32) + dtb_ref[...]
    lane = lax.broadcasted_iota(jnp.int32, (1, LANES), 1)
    return jnp.where(lane < 2 * SSD_HEADS, _softplus(raw), 0.0)


def _ssd_in(x, seq_len, layer, row_fn, mod, nw, w_in, w_dt, dt_bias, w_conv, b_conv):
    M = x.shape[0]
    tm = TOKEN_TILE
    HL = SSD_HALO
    tps = seq_len // tm
    n_hblk = M // HL
    row = lambda i: (i, 0)
    items = [(pl.BlockSpec((tm, D), row), x)]
    if tps > 1:
        items += [(pl.BlockSpec((HL, D), lambda i: (jnp.maximum(i * (tm // HL) - 1, 0), 0)), x),
                  (pl.BlockSpec((HL, D), lambda i: (jnp.minimum((i + 1) * (tm // HL), n_hblk - 1), 0)), x)]
    items += [(_const_spec((1, D)), nw), (_mod_spec(layer, lambda i: row_fn(i * tm)), mod),
              _W(w_in, SSD_INNER, 0), _W(w_in, SSD_XBC, SSD_INNER),
              (_const_spec((D, LANES)), w_dt), (_const_spec((1, LANES)), dt_bias),
              (_const_spec((SSD_CONV + 1, SSD_XBC)), w_conv), (_const_spec((1, SSD_XBC)), b_conv)]
    in_specs, args, groups = _operands(items)
    return pl.pallas_call(
        _grouped(functools.partial(_ssd_in_kernel, tps), groups),
        grid=(M // tm,),
        in_specs=in_specs,
        out_specs=[pl.BlockSpec((tm, SSD_INNER), row), pl.BlockSpec((tm, SSD_XBC), row), pl.BlockSpec((tm, LANES), row)],
        out_shape=[jax.ShapeDtypeStruct((M, SSD_INNER), F32), jax.ShapeDtypeStruct((M, SSD_XBC), F32),
                   jax.ShapeDtypeStruct((M, LANES), F32)],
        scratch_shapes=[pltpu.VMEM((SSD_XBC // LANES, tm + 2 * HL, LANES), F32)],
        compiler_params=_cparams(1),
        name="ssd_in",
    )(*args)


def _cumsum_rows(a, reverse):
    n = a.shape[0]
    row = lax.broadcasted_iota(jnp.int32, a.shape, 0)
    k = 1
    while k < n:
        if reverse:
            a = a + jnp.where(row < n - k, pltpu.roll(a, n - k, 0), 0.0)
        else:
            a = a + jnp.where(row >= k, pltpu.roll(a, k, 0), 0.0)
        k *= 2
    return a


def _ssd_core_kernel(nc, has_h0, emit_state, *refs):
    refs = list(refs)
    dir_refs = [refs[0:2], refs[2:4]]
    alog_ref, dsk_ref = refs[4:6]
    refs = refs[6:]
    if has_h0:
        h0_ref = refs.pop(0)
    y_refs = [refs.pop(0), refs.pop(0)]
    if emit_state:
        so_ref = refs.pop(0)
    (st_ref,) = refs
    Q, N, P = SSD_CHUNK, SSD_STATE, SSD_HEAD_DIM
    c = pl.program_id(1)
    n_blk = SSD_INNER // LANES

    @pl.when(c == 0)
    def _init():
        if has_h0:
            for dr in range(2):
                for j in range(n_blk):
                    st_ref[dr, :, j * LANES:(j + 1) * LANES] = h0_ref[dr, j * LANES:(j + 1) * LANES, :].T
        else:
            st_ref[...] = jnp.zeros(st_ref.shape, F32)

    A = -jnp.exp(alog_ref[...])
    for dr in range(2):
        x_ref, dt_ref = dir_refs[dr]
        y_ref = y_refs[dr]

        def x_at(c0, x_ref=x_ref):
            return x_ref[:, c0:c0 + LANES]

        def y_put(c0, y, y_ref=y_ref):
            y_ref[:, c0:c0 + LANES] = y

        _ssd_chunk(dr, x_at, dt_ref[...], A, st_ref, y_put, dsk_ref)

    if emit_state:
        @pl.when(c == nc - 1)
        def _emit():
            _ssd_emit_state(st_ref, so_ref)


def _ssd_emit_state(st_ref, so_ref):
    for dr in range(2):
        for j in range(SSD_INNER // LANES):
            so_ref[dr, j * LANES:(j + 1) * LANES, :] = st_ref[dr, :, j * LANES:(j + 1) * LANES].T


def _ssd_chunk(dr, x_at, dt, A, st_ref, y_put, dsk_ref):
    Q, N, P = SSD_CHUNK, SSD_STATE, SSD_HEAD_DIM
    lo = _head_masks()
    ri = lax.broadcasted_iota(jnp.int32, (Q, Q), 0)
    ci = lax.broadcasted_iota(jnp.int32, (Q, Q), 1)
    cum = _cumsum_rows(dt * A, reverse=(dr == 1)) * LOG2E
    cum_t = cum.T
    dt_t = dt.T
    tot_row = Q - 1 if dr == 0 else 0
    mask = (ri >= ci) if dr == 0 else (ri <= ci)
    for g in range(SSD_GROUPS):
        bg_t = x_at(SSD_INNER + g * N).T
        cg = x_at(SSD_INNER + SSD_GN + g * N).astype(BF16)
        cb = jnp.dot(cg, bg_t.astype(BF16), preferred_element_type=F32)
        for pp in range(SSD_HEADS // SSD_GROUPS // 2):
            head = g * (SSD_HEADS // SSD_GROUPS) + 2 * pp
            c0 = head * P
            li = dr * SSD_HEADS + head
            xpair = x_at(c0)
            xpair_b = xpair.astype(BF16)
            cum_cols = [cum[:, li:li + 1], cum[:, li + 1:li + 2]]
            cum_x = jnp.where(lo, cum_cols[0], cum_cols[1])
            yd, sd = [], []
            for hh in range(2):
                dt_row = dt_t[li + hh:li + hh + 1, :]
                cum_row = cum_t[li + hh:li + hh + 1, :]
                decay = jnp.exp2(jnp.where(mask, cum_cols[hh] - cum_row, -jnp.inf))
                yd.append(jnp.dot((cb * decay * dt_row).astype(BF16), xpair_b, preferred_element_type=F32))
                to_end = jnp.exp2(cum_row[:, tot_row:tot_row + 1] - cum_row)
                sd.append(jnp.dot((bg_t * (dt_row * to_end)).astype(BF16), xpair_b, preferred_element_type=F32))
            st = st_ref[dr, :, c0:c0 + LANES]
            y = (jnp.where(lo, yd[0], yd[1])
                 + jnp.dot(cg, st.astype(BF16), preferred_element_type=F32) * jnp.exp2(cum_x))
            if dr == 0:
                y = y + xpair * dsk_ref[:, c0:c0 + LANES]
            y_put(c0, y)
            tot = cum_x[tot_row:tot_row + 1, :]
            st_ref[dr, :, c0:c0 + LANES] = st * jnp.exp2(tot) + jnp.where(lo, sd[0], sd[1])


def _ssd_seq_kernel(x_ref, nw_ref, mod_ref, wz_ref, wx_ref, wdt_ref, dtb_ref, wc_ref, bc_ref, alog_ref, dsk_ref, gnw_ref, wo_ref,
                    o_ref, so_ref, xpad_ref, xbc_ref, sz_ref, y_ref, st_ref):
    Q = SSD_CHUNK
    nc = x_ref.shape[0] // Q
    x = x_ref[...]
    hb = _modnorm(x, nw_ref[...], mod_ref[...]).astype(BF16)
    dt = _ssd_project(hb, None, wz_ref, wx_ref, wdt_ref, dtb_ref, wc_ref, bc_ref, xpad_ref, sz_ref, xbc_ref)
    st_ref[...] = jnp.zeros(st_ref.shape, F32)
    A = -jnp.exp(alog_ref[...])
    for dr in range(2):
        for ck in (range(nc) if dr == 0 else reversed(range(nc))):
            rows = slice(ck * Q, (ck + 1) * Q)

            def x_at(c0, rows=rows):
                return xbc_ref[rows, c0:c0 + LANES]

            def y_put(c0, y, rows=rows, dr=dr):
                if dr == 0:
                    y_ref[rows, c0:c0 + LANES] = y
                else:
                    y_ref[rows, c0:c0 + LANES] += y

            _ssd_chunk(dr, x_at, dt[rows], A, st_ref, y_put, dsk_ref)
    _ssd_emit_state(st_ref, so_ref)
    g = _rms(y_ref[...] * sz_ref[...], gnw_ref[...]).astype(BF16)
    o_ref[...] = x + mod_ref[:, 2 * D:3 * D] * _wdot(g, wo_ref, 0, D)


def _ssd_seq(x, seq_len, layer, row_fn, mod, nw, w_in, w_dt, dt_bias, w_conv, b_conv, a_log, d_skip, norm_w, w_out):
    M = x.shape[0]
    n_seq = M // seq_len
    HL = SSD_HALO
    row = lambda i: (i, 0)
    st_blk = pl.BlockSpec((None, 2, SSD_INNER, SSD_STATE), lambda i: (i, 0, 0, 0))
    in_specs, args, groups = _operands([
        (pl.BlockSpec((seq_len, D), row), x), (_const_spec((1, D)), nw),
        (_mod_spec(layer, lambda i: row_fn(i * seq_len)), mod),
        _W(w_in, SSD_INNER, 0), _W(w_in, SSD_XBC, SSD_INNER),
        (_const_spec((D, LANES)), w_dt), (_const_spec((1, LANES)), dt_bias),
        (_const_spec((SSD_CONV + 1, SSD_XBC)), w_conv), (_const_spec((1, SSD_XBC)), b_conv),
        (_const_spec((1, LANES)), a_log), (_const_spec((1, SSD_INNER)), d_skip),
        (_const_spec((1, SSD_INNER)), norm_w), _W(w_out)])
    return pl.pallas_call(
        _grouped(_ssd_seq_kernel, groups),
        grid=(n_seq,),
        in_specs=in_specs,
        out_specs=[pl.BlockSpec((seq_len, D), row), st_blk],
        out_shape=[jax.ShapeDtypeStruct((M, D), F32), jax.ShapeDtypeStruct((n_seq, 2, SSD_INNER, SSD_STATE), F32)],
        scratch_shapes=[pltpu.VMEM((SSD_XBC // LANES, seq_len + 2 * HL, LANES), F32),
                        pltpu.VMEM((seq_len, SSD_XBC), F32), pltpu.VMEM((seq_len, SSD_INNER), F32),
                        pltpu.VMEM((seq_len, SSD_INNER), F32), pltpu.VMEM((2, SSD_STATE, SSD_INNER), F32)],
        compiler_params=_cparams(1),
        name="ssd_seq",
    )(*args)


def _ssd_core(xbc, dtp, a_log, d_skip, h0, n_seq, seq_len, emit_state):
    Q = SSD_CHUNK
    nc = seq_len // Q
    M = n_seq * seq_len

    def chunk(dr):
        return (lambda b, c: b * nc + c) if dr == 0 else (lambda b, c: b * nc + nc - 1 - c)

    in_specs, args = [], []
    for dr in range(2):
        ch = chunk(dr)
        in_specs += [pl.BlockSpec((Q, SSD_XBC), lambda b, c, ch=ch: (ch(b, c), 0)),
                     pl.BlockSpec((Q, LANES), lambda b, c, ch=ch: (ch(b, c), 0))]
        args += [xbc, dtp]
    in_specs += [_const_spec((1, LANES)), _const_spec((1, SSD_INNER))]
    args += [a_log, d_skip]
    st_blk = pl.BlockSpec((None, 2, SSD_INNER, SSD_STATE), lambda b, c: (b, 0, 0, 0))
    has_h0 = h0 is not None
    if has_h0:
        in_specs.append(st_blk)
        args.append(h0)
    out_specs = [pl.BlockSpec((Q, SSD_INNER), lambda b, c: (chunk(0)(b, c), 0)),
                 pl.BlockSpec((Q, SSD_INNER), lambda b, c: (chunk(1)(b, c), 0))]
    out_shape = [jax.ShapeDtypeStruct((M, SSD_INNER), F32)] * 2
    if emit_state:
        out_specs.append(st_blk)
        out_shape.append(jax.ShapeDtypeStruct((n_seq, 2, SSD_INNER, SSD_STATE), F32))
    return pl.pallas_call(
        functools.partial(_ssd_core_kernel, nc, has_h0, emit_state),
        grid=(n_seq, nc),
        in_specs=in_specs,
        out_specs=out_specs,
        out_shape=out_shape,
        scratch_shapes=[pltpu.VMEM((2, SSD_STATE, SSD_INNER), F32)],
        compiler_params=_cparams(2),
        name="ssd_core",
    )(*args)


def _ssd_out_kernel(yf_ref, yb_ref, sz_ref, x_ref, mod_ref, nw_ref, w_ref, o_ref):
    g = _rms((yf_ref[...] + yb_ref[...]) * sz_ref[...], nw_ref[...]).astype(BF16)
    o_ref[...] = x_ref[...] + mod_ref[:, 2 * D:3 * D] * _wdot(g, w_ref, 0, D)


def _ssd_out(yf, yb, sz, x, layer, row_fn, mod, norm_w, w_out):
    M = x.shape[0]
    tm = TOKEN_TILE
    row = lambda i: (i, 0)
    wide = pl.BlockSpec((tm, SSD_INNER), row)
    in_specs, args, groups = _operands([
        (wide, yf), (wide, yb), (wide, sz), (pl.BlockSpec((tm, D), row), x),
        (_mod_spec(layer, lambda i: row_fn(i * tm)), mod), (_const_spec((1, SSD_INNER)), norm_w), _W(w_out)])
    return pl.pallas_call(
        _grouped(_ssd_out_kernel, groups),
        grid=(M // tm,),
        in_specs=in_specs,
        out_specs=pl.BlockSpec((tm, D), row),
        out_shape=jax.ShapeDtypeStruct((M, D), F32),
        compiler_params=_cparams(1),
        name="ssd_out",
    )(*args)


def _ssd_layer(xc, xl, layer, mod, nw, p, n_lat):
    j = layer // 3
    w_in = p["ssd_w_in"][j].astype(BF16)
    w_dt = jnp.pad(w_in[:, SSD_IN_COLS:], ((0, 0), (0, LANES - 2 * SSD_HEADS)))
    dt_bias = jnp.pad(p["ssd_dt_bias"][j].reshape(1, -1), ((0, 0), (0, LANES - 2 * SSD_HEADS)))
    a_log = jnp.pad(p["ssd_a_log"][j].reshape(1, -1), ((0, 0), (0, LANES - 2 * SSD_HEADS)))
    w_conv = jnp.pad(p["ssd_w_conv"][j], ((0, 1), (0, 0)))
    b_conv = p["ssd_b_conv"][j][None]
    d_skip = jnp.repeat(p["ssd_d"][j], SSD_HEAD_DIM)[None]
    norm_w = p["ssd_norm_w"][j][None]
    w_out = p["ssd_w_out"][j].astype(BF16)
    row_ctx = lambda tok: 0
    row_lat = lambda tok: 1 + tok // DEC_SEQ
    oc, s_new = _ssd_seq(xc, SEQ, layer, row_ctx, mod, nw, w_in, w_dt, dt_bias, w_conv, b_conv, a_log, d_skip, norm_w, w_out)
    szl, xbc_l, dt_l = _ssd_in(xl, DEC_SEQ, layer, row_lat, mod, nw, w_in, w_dt, dt_bias, w_conv, b_conv)
    h0 = p["state_ssm"][:n_lat, j].reshape(n_lat, 2, SSD_INNER, SSD_STATE)
    yfl, ybl = _ssd_core(xbc_l, dt_l, a_log, d_skip, h0, n_lat, DEC_SEQ, False)
    ol = _ssd_out(yfl, ybl, szl, xl, layer, row_lat, mod, norm_w, w_out)
    return oc, ol, s_new


def kernel(x_prompt, x_sample, cache_k, cache_v, state_ssm, c, c_ctx, ada_w, ada_b, norm_w, final_norm_w, conv_w_in, conv_b_in, conv_w_dw, conv_b_dw, conv_ln_w, conv_ln_b, conv_w_out, conv_b_out, na_w_in, na_rpb, na_w_out, ssd_w_in, ssd_w_conv, ssd_b_conv, ssd_a_log, ssd_dt_bias, ssd_d, ssd_norm_w, ssd_w_out):
    p = dict(cache_k=cache_k, cache_v=cache_v, state_ssm=state_ssm,
             conv_w_in=conv_w_in, conv_b_in=conv_b_in, conv_w_dw=conv_w_dw, conv_b_dw=conv_b_dw,
             conv_ln_w=conv_ln_w, conv_ln_b=conv_ln_b, conv_w_out=conv_w_out, conv_b_out=conv_b_out,
             na_w_in=na_w_in, na_rpb=na_rpb, na_w_out=na_w_out,
             ssd_w_in=ssd_w_in, ssd_w_conv=ssd_w_conv, ssd_b_conv=ssd_b_conv, ssd_a_log=ssd_a_log,
             ssd_dt_bias=ssd_dt_bias, ssd_d=ssd_d, ssd_norm_w=ssd_norm_w, ssd_w_out=ssd_w_out)
    n_ctx, n_lat = x_prompt.shape[0], x_sample.shape[0]
    assert x_prompt.shape[1:] == (SEQ, D) and x_sample.shape[1:] == (DEC_SEQ, D)
    assert (DEPTH - 1) % 3 == 0, "the final RMSNorm is fused into a trailing Conformer layer"
    xc = x_prompt.reshape(n_ctx * SEQ, D)
    xl = x_sample.reshape(n_lat * DEC_SEQ, D)
    cond8 = jnp.concatenate([c_ctx[None], c, jnp.zeros((8 - 1 - n_lat, D), F32)], axis=0)
    mod = _modulation(cond8, ada_w, ada_b).reshape(DEPTH, 8, 1, 3 * D)
    row_ctx = lambda tok: 0
    row_lat = lambda tok: 1 + tok // DEC_SEQ
    new_k, new_v, new_s = [], [], []
    for i in range(DEPTH):
        kind, j = i % 3, i // 3
        nw = norm_w[i][None]
        if kind == 0:
            cp = _prep_conv(p, j)
            fw = final_norm_w[None] if i == DEPTH - 1 else None
            xc = _conv_layer(xc, SEQ, i, row_ctx, mod, nw, *cp, fw)
            xl = _conv_layer(xl, DEC_SEQ, i, row_lat, mod, nw, *cp, fw)
        elif kind == 1:
            xc, xl, k_c, v_c = _na_layer(xc, xl, i, mod, nw, p, n_lat)
            new_k.append(k_c.reshape(n_ctx, SEQ, NA_HEADS, NA_HEAD_DIM))
            new_v.append(v_c.reshape(n_ctx, SEQ, NA_HEADS, NA_HEAD_DIM))
        else:
            xc, xl, s_c = _ssd_layer(xc, xl, i, mod, nw, p, n_lat)
            new_s.append(s_c.reshape(n_ctx, 2, SSD_HEADS, SSD_HEAD_DIM, SSD_STATE))
    return (xc.reshape(n_ctx, SEQ, D), xl.reshape(n_lat, DEC_SEQ, D),
            jnp.stack(new_k, axis=1), jnp.stack(new_v, axis=1), jnp.stack(new_s, axis=1))
```

```python
import functools

import jax
import jax.numpy as jnp
from jax import lax
from jax.experimental import pallas as pl
from jax.experimental.pallas import tpu as pltpu

F32 = jnp.float32
BF16 = jnp.bfloat16

D = 1024
DEPTH = 4
SEQ = 256
DEC_SEQ = 2048
GRID_W = 64
RMS_EPS = 1e-6
LN_EPS = 1e-5
LOG2E = 1.4426950408889634
CONF_K = 31
NA_HEADS = 16
NA_HEAD_DIM = 64
NA_KH = 8
NA_KW = 16
NEG_INF = -1e30
SSD_INNER = 2048
SSD_HEADS = 32
SSD_HEAD_DIM = 64
SSD_STATE = 128
SSD_GROUPS = 4
SSD_GN = SSD_GROUPS * SSD_STATE
SSD_CONV = 7
SSD_CHUNK = 128
SSD_XBC = SSD_INNER + 2 * SSD_GN

LANES = 128
TOKEN_TILE = 256
W_CHUNK = 512
CONV_HALO = 16
SSD_HALO = 8
VMEM_LIMIT = 56 * 1024 * 1024


def _cparams(n_axes):
    return pltpu.CompilerParams(dimension_semantics=("arbitrary",) * n_axes,
                                vmem_limit_bytes=VMEM_LIMIT)


def _const_spec(shape):
    nd = len(shape)
    return pl.BlockSpec(shape, lambda *_: (0,) * nd, pipeline_mode=pl.Buffered(1))


class _W:
    def __init__(self, w, ncols=None, col0=0):
        self.w, self.col0 = w, col0
        self.ncols = w.shape[1] if ncols is None else ncols


def _operands(items):
    specs, args, groups = [], [], []
    for it in items:
        if isinstance(it, _W):
            n, b0, K = it.ncols // W_CHUNK, it.col0 // W_CHUNK, it.w.shape[0]
            for c in range(b0, b0 + n):
                specs.append(pl.BlockSpec((K, W_CHUNK), lambda *_, c=c: (0, c), pipeline_mode=pl.Buffered(1)))
                args.append(it.w)
            groups.append(n)
        else:
            specs.append(it[0])
            args.append(it[1])
            groups.append(None)
    return specs, args, groups


def _grouped(body, groups):
    def kernel_fn(*refs):
        refs = list(refs)
        packed = []
        for g in groups:
            if g is None:
                packed.append(refs.pop(0))
            else:
                packed.append(tuple(refs[:g]))
                del refs[:g]
        return body(*packed, *refs)
    return kernel_fn


def _wdot(h, w_refs, c0, n):
    outs = []
    c = c0
    while c < c0 + n:
        j, off = divmod(c, W_CHUNK)
        m = min(W_CHUNK - off, c0 + n - c)
        outs.append(jnp.dot(h, w_refs[j][:, off:off + m], preferred_element_type=F32))
        c += m
    return outs[0] if len(outs) == 1 else jnp.concatenate(outs, axis=1)


def _sigmoid(x):
    return 1.0 / (1.0 + jnp.exp(-x))


def _silu(x):
    return x * _sigmoid(x)


def _rms(x, w):
    ms = jnp.mean(x * x, axis=-1, keepdims=True)
    return x * lax.rsqrt(ms + RMS_EPS) * w


def _modnorm(x, nw, mod):
    return _rms(x, nw) * (1.0 + mod[:, D:2 * D]) + mod[:, :D]


def _mod_spec(layer, row_fn):
    return pl.BlockSpec((None, None, 1, 3 * D), lambda i, *_: (layer, row_fn(i), 0, 0))


def _mod_kernel(cond_ref, w_ref, b_ref, o_ref):
    s = _silu(cond_ref[...]).astype(BF16)
    o_ref[...] = jnp.dot(s, w_ref[...].astype(BF16), preferred_element_type=F32) + b_ref[...]


def _modulation(cond8, ada_w, ada_b):
    tn = 1024
    return pl.pallas_call(
        _mod_kernel,
        grid=(DEPTH, 3 * D // tn),
        in_specs=[pl.BlockSpec((8, D), lambda i, j: (0, 0)),
                  pl.BlockSpec((None, D, tn), lambda i, j: (i, 0, j)),
                  pl.BlockSpec((None, 1, tn), lambda i, j: (i, 0, j))],
        out_specs=pl.BlockSpec((None, 8, tn), lambda i, j: (i, 0, j)),
        out_shape=jax.ShapeDtypeStruct((DEPTH, 8, 3 * D), F32),
        compiler_params=_cparams(2),
        name="adaln_mod",
    )(cond8, ada_w, ada_b.reshape(DEPTH, 1, 3 * D))


CONV_CHUNK = 256


def _conv_kernel(tiles_per_seq, final, *refs):
    refs = list(refs)
    x_ref = refs.pop(0)
    if tiles_per_seq > 1:
        xp_ref = refs.pop(0)
        xn_ref = refs.pop(0)
    nw_ref, mod_ref, wi_ref, bi_ref, wdw_ref, bdw_ref, lnw_ref, lnb_ref, wo_ref, bo_ref = refs[:10]
    refs = refs[10:]
    if final:
        fw_ref = refs.pop(0)
    o_ref, ubuf, cbuf = refs
    T = x_ref.shape[0]
    H = CONV_HALO
    RB = 128

    hb = _modnorm(x_ref[...], nw_ref[...], mod_ref[...]).astype(BF16)
    if tiles_per_seq > 1:
        t = pl.program_id(0) % tiles_per_seq
        hh = _modnorm(jnp.concatenate([xp_ref[...], xn_ref[...]], axis=0), nw_ref[...], mod_ref[...]).astype(BF16)

    def glu(h, cols):
        v = _wdot(h, wi_ref, cols.start, CONV_CHUNK) + bi_ref[:, cols]
        gcols = slice(D + cols.start, D + cols.stop)
        g = _wdot(h, wi_ref, gcols.start, CONV_CHUNK) + bi_ref[:, gcols]
        return v * _sigmoid(g)

    for ch in range(D // CONV_CHUNK):
        cols = slice(ch * CONV_CHUNK, (ch + 1) * CONV_CHUNK)
        u_main = glu(hb, cols)
        if tiles_per_seq > 1:
            uh = glu(hh, cols)
            u_prev = jnp.where(t > 0, uh[0:H], 0.0)
            u_next = jnp.where(t < tiles_per_seq - 1, uh[H:2 * H], 0.0)
        else:
            u_prev = u_next = jnp.zeros((H, CONV_CHUNK), F32)
        for cb in range(CONV_CHUNK // LANES):
            blk = ch * (CONV_CHUNK // LANES) + cb
            lc = slice(blk * LANES, (blk + 1) * LANES)
            cc = slice(cb * LANES, (cb + 1) * LANES)
            ubuf[blk, 0:H, :] = u_prev[:, cc]
            ubuf[blk, H:H + T, :] = u_main[:, cc]
            ubuf[blk, H + T:H + T + H, :] = u_next[:, cc]
            for rb in range(T // RB):
                acc = jnp.zeros((RB, LANES), F32)
                for k in range(CONF_K):
                    r0 = rb * RB + H - CONF_K // 2 + k
                    acc = acc + ubuf[blk, r0:r0 + RB, :] * wdw_ref[k:k + 1, lc]
                cbuf[rb * RB:(rb + 1) * RB, lc] = acc + bdw_ref[:, lc]

    sz = _silu(_wdot(hb, wi_ref, 2 * D, D) + bi_ref[:, 2 * D:3 * D])
    c = cbuf[...]
    mu = jnp.mean(c, axis=-1, keepdims=True)
    xc = c - mu
    var = jnp.mean(xc * xc, axis=-1, keepdims=True)
    ln = xc * lax.rsqrt(var + LN_EPS) * lnw_ref[...] + lnb_ref[...]
    g = (_silu(ln) * sz).astype(BF16)
    y = _wdot(g, wo_ref, 0, D) + bo_ref[...]
    xn = x_ref[...] + mod_ref[:, 2 * D:3 * D] * y
    if final:
        xn = _rms(xn, fw_ref[...])
    o_ref[...] = xn


def _conv_layer(x, seq_len, layer, row_fn, mod, nw, w_in, b_in, w_dw, b_dw, ln_w, ln_b, w_out, b_out, final_w):
    M = x.shape[0]
    T = TOKEN_TILE
    tps = seq_len // T
    H = CONV_HALO
    n_halo_blocks = M // H
    row = lambda i: (i, 0)
    items = [(pl.BlockSpec((T, D), row), x)]
    if tps > 1:
        items += [(pl.BlockSpec((H, D), lambda i: (jnp.maximum(i * (T // H) - 1, 0), 0)), x),
                  (pl.BlockSpec((H, D), lambda i: (jnp.minimum((i + 1) * (T // H), n_halo_blocks - 1), 0)), x)]
    items += [(_const_spec((1, D)), nw), (_mod_spec(layer, lambda i: row_fn(i * T)), mod),
              _W(w_in), (_const_spec((1, 3 * D)), b_in),
              (_const_spec((CONF_K + 1, D)), w_dw), (_const_spec((1, D)), b_dw),
              (_const_spec((1, D)), ln_w), (_const_spec((1, D)), ln_b),
              _W(w_out), (_const_spec((1, D)), b_out)]
    final = final_w is not None
    if final:
        items.append((_const_spec((1, D)), final_w))
    in_specs, args, groups = _operands(items)
    return pl.pallas_call(
        _grouped(functools.partial(_conv_kernel, tps, final), groups),
        grid=(M // T,),
        in_specs=in_specs,
        out_specs=pl.BlockSpec((T, D), row),
        out_shape=jax.ShapeDtypeStruct((M, D), F32),
        scratch_shapes=[pltpu.VMEM((D // LANES, T + 2 * H, LANES), F32), pltpu.VMEM((T, D), F32)],
        compiler_params=_cparams(1),
        name="conv_layer",
    )(*args)


def _prep_conv(p, j):
    w_dw = jnp.pad(p["conv_w_dw"][j], ((0, 1), (0, 0)))
    return (p["conv_w_in"][j].astype(BF16), p["conv_b_in"][j][None], w_dw, p["conv_b_dw"][j][None],
            p["conv_ln_w"][j][None], p["conv_ln_b"][j][None], p["conv_w_out"][j].astype(BF16), p["conv_b_out"][j][None])


def _out_proj_kernel(g_ref, x_ref, mod_ref, w_ref, o_ref):
    y = _wdot(g_ref[...].astype(BF16), w_ref, 0, D)
    o_ref[...] = x_ref[...] + mod_ref[:, 2 * D:3 * D] * y


def _out_proj(g, x, layer, row_fn, mod, w_out):
    M, K = g.shape
    tm = TOKEN_TILE
    row = lambda i: (i, 0)
    in_specs, args, groups = _operands([
        (pl.BlockSpec((tm, K), row), g), (pl.BlockSpec((tm, D), row), x),
        (_mod_spec(layer, lambda i: row_fn(i * tm)), mod), _W(w_out)])
    return pl.pallas_call(
        _grouped(_out_proj_kernel, groups),
        grid=(M // tm,),
        in_specs=in_specs,
        out_specs=pl.BlockSpec((tm, D), row),
        out_shape=jax.ShapeDtypeStruct((M, D), F32),
        compiler_params=_cparams(1),
        name="out_proj",
    )(*args)


NA_SCALE = NA_HEAD_DIM ** -0.5
NA_ROWS = 32
NA_WIN = NA_KH * GRID_W
NT_DIMS = (((1,), (1,)), ((), ()))


def _na_in_kernel(x_ref, nw_ref, mod_ref, w_ref, q_ref, k_ref, v_ref, sz_ref):
    hb = _modnorm(x_ref[...], nw_ref[...], mod_ref[...]).astype(BF16)
    q_ref[...] = _wdot(hb, w_ref, 0, D).astype(q_ref.dtype)
    k_ref[...] = _wdot(hb, w_ref, D, D).astype(k_ref.dtype)
    v_ref[...] = _wdot(hb, w_ref, 2 * D, D).astype(v_ref.dtype)
    sz_ref[...] = _silu(_wdot(hb, w_ref, 3 * D, D))


def _na_in(x, layer, row_fn, mod, nw, w_in, kv_dtype):
    M = x.shape[0]
    tm = TOKEN_TILE
    row = lambda i: (i, 0)
    blk = pl.BlockSpec((tm, D), row)
    in_specs, args, groups = _operands([
        (blk, x), (_const_spec((1, D)), nw), (_mod_spec(layer, lambda i: row_fn(i * tm)), mod), _W(w_in)])
    return pl.pallas_call(
        _grouped(_na_in_kernel, groups),
        grid=(M // tm,),
        in_specs=in_specs,
        out_specs=[blk, blk, blk, blk],
        out_shape=[jax.ShapeDtypeStruct((M, D), BF16), jax.ShapeDtypeStruct((M, D), kv_dtype),
                   jax.ShapeDtypeStruct((M, D), kv_dtype), jax.ShapeDtypeStruct((M, D), F32)],
        compiler_params=_cparams(1),
        name="na_in",
    )(*args)


def _head_masks():
    lane = lax.broadcasted_iota(jnp.int32, (1, LANES), 1)
    return lane < NA_HEAD_DIM


def _na_seq_kernel(x_ref, nw_ref, mod_ref, wi_ref, wo_ref, o_ref, k_ref, v_ref, g_ref):
    n = x_ref.shape[0]
    lo = _head_masks()
    x = x_ref[...]
    hb = _modnorm(x, nw_ref[...], mod_ref[...]).astype(BF16)
    hpc = CONV_CHUNK // NA_HEAD_DIM
    for ch in range(D // CONV_CHUNK):
        cols = slice(ch * CONV_CHUNK, (ch + 1) * CONV_CHUNK)
        q = _wdot(hb, wi_ref, cols.start, CONV_CHUNK).astype(BF16)
        k = _wdot(hb, wi_ref, D + cols.start, CONV_CHUNK)
        v = _wdot(hb, wi_ref, 2 * D + cols.start, CONV_CHUNK)
        sz = _silu(_wdot(hb, wi_ref, 3 * D + cols.start, CONV_CHUNK))
        k_ref[:, ch * hpc:(ch + 1) * hpc, :] = k.reshape(n, hpc, NA_HEAD_DIM)
        v_ref[:, ch * hpc:(ch + 1) * hpc, :] = v.reshape(n, hpc, NA_HEAD_DIM)
        kb = k.astype(BF16)
        vb = v.astype(BF16)
        for pp in range(CONV_CHUNK // LANES):
            pc = slice(pp * LANES, (pp + 1) * LANES)
            qp = q[:, pc]
            zero = jnp.zeros_like(qp)
            qs = jnp.concatenate([jnp.where(lo, qp, zero), jnp.where(lo, zero, qp)], axis=0)
            s = lax.dot_general(qs, kb[:, pc], NT_DIMS, preferred_element_type=F32) * NA_SCALE
            e = jnp.exp(s - jnp.max(s, axis=-1, keepdims=True))
            l = jnp.sum(e, axis=-1, keepdims=True)
            o = jnp.dot(e.astype(BF16), vb[:, pc], preferred_element_type=F32) / l
            g = jnp.where(lo, o[:n], o[n:]) * sz[:, pc]
            g_ref[:, cols.start + pp * LANES:cols.start + (pp + 1) * LANES] = g.astype(BF16)
    y = _wdot(g_ref[...], wo_ref, 0, D)
    o_ref[...] = x + mod_ref[:, 2 * D:3 * D] * y


def _na_seq(x, seq_len, layer, row_fn, mod, nw, w_in, w_out):
    M = x.shape[0]
    row = lambda i: (i, 0)
    kv_blk = pl.BlockSpec((seq_len, NA_HEADS, NA_HEAD_DIM), lambda i: (i, 0, 0))
    kv_shape = jax.ShapeDtypeStruct((M, NA_HEADS, NA_HEAD_DIM), F32)
    in_specs, args, groups = _operands([
        (pl.BlockSpec((seq_len, D), row), x), (_const_spec((1, D)), nw),
        (_mod_spec(layer, lambda i: row_fn(i * seq_len)), mod), _W(w_in), _W(w_out)])
    return pl.pallas_call(
        _grouped(_na_seq_kernel, groups),
        grid=(M // seq_len,),
        in_specs=in_specs,
        out_specs=[pl.BlockSpec((seq_len, D), row), kv_blk, kv_blk],
        out_shape=[jax.ShapeDtypeStruct((M, D), F32), kv_shape, kv_shape],
        scratch_shapes=[pltpu.VMEM((seq_len, D), BF16)],
        compiler_params=_cparams(1),
        name="na_seq",
    )(*args)


def _rpb_kernel(rpb_ref, o_ref):
    h = pl.program_id(0)
    c = lax.broadcasted_iota(jnp.int32, (GRID_W, LANES), 0)
    lane = lax.broadcasted_iota(jnp.int32, (GRID_W, LANES), 1)
    kc = lane & (GRID_W - 1)
    upper = lax.broadcasted_iota(jnp.int32, (1, LANES), 1) >= GRID_W
    cs = jnp.clip(c - NA_KW // 2, 0, GRID_W - NA_KW)
    valid = (kc >= cs) & (kc < cs + NA_KW)
    n_pair = 2 * NA_KH - 2

    def toeplitz(dri, lane0):
        row = rpb_ref[pl.ds(h * (2 * NA_KH - 1) + dri, 1), :]
        rows = jnp.broadcast_to(row, (GRID_W, LANES))
        return pltpu.roll(rows, (LANES - (NA_KW - 1) + lane0) % LANES, 1, stride=1, stride_axis=0)

    for dri in range(n_pair):
        tile = jnp.where(upper, toeplitz(dri + 1, GRID_W), toeplitz(dri, 0))
        tile = jnp.where(valid, tile, NEG_INF)
        for idx in range(NA_KH):
            jj2 = dri - (NA_KH - 1) + idx
            if jj2 % 2 == 0 and 0 <= jj2 // 2 < NA_KH // 2:
                jj = jj2 // 2
                o_ref[idx, :, jj * LANES:(jj + 1) * LANES] = tile


def _rpb_table(rpb):
    n_rows = NA_HEADS * (2 * NA_KH - 1)
    rows = jnp.pad(rpb.reshape(n_rows, 2 * NA_KW - 1), ((0, 0), (0, LANES - (2 * NA_KW - 1))))
    return pl.pallas_call(
        _rpb_kernel,
        grid=(NA_HEADS,),
        in_specs=[_const_spec((n_rows, LANES))],
        out_specs=pl.BlockSpec((NA_KH, None, GRID_W, NA_WIN), lambda h: (0, h, 0, 0)),
        out_shape=jax.ShapeDtypeStruct((NA_KH, NA_HEADS, GRID_W, NA_WIN), F32),
        compiler_params=_cparams(1),
        name="na_rpb_table",
    )(rows)


def _na_row_start(r):
    return jnp.clip(r - NA_KH // 2, 0, NA_ROWS - NA_KH)


def _na_lat_kernel(q_ref, k_ref, v_ref, ck_ref, cv_ref, bias_ref, sz_ref, o_ref):
    lo = _head_masks()
    k0 = pl.multiple_of(_na_row_start(pl.program_id(1)) * GRID_W, GRID_W)
    for p in range(NA_HEADS // 2):
        cols = slice(p * LANES, (p + 1) * LANES)
        qp = q_ref[:, cols]
        kw = k_ref[pl.ds(k0, NA_WIN), cols]
        vw = v_ref[pl.ds(k0, NA_WIN), cols]
        ck = ck_ref[:, cols]
        cv = cv_ref[:, cols]
        zero = jnp.zeros_like(qp)
        qs = jnp.concatenate([jnp.where(lo, qp, zero), jnp.where(lo, zero, qp)], axis=0)
        bias = bias_ref[2 * p:2 * p + 2].reshape(2 * GRID_W, NA_WIN)
        s_loc = lax.dot_general(qs, kw, NT_DIMS, preferred_element_type=F32) * NA_SCALE + bias
        s_ctx = lax.dot_general(qs, ck, NT_DIMS, preferred_element_type=F32) * NA_SCALE
        m = jnp.maximum(jnp.max(s_loc, axis=-1, keepdims=True), jnp.max(s_ctx, axis=-1, keepdims=True))
        e_loc = jnp.exp(s_loc - m)
        e_ctx = jnp.exp(s_ctx - m)
        l = jnp.sum(e_loc, axis=-1, keepdims=True) + jnp.sum(e_ctx, axis=-1, keepdims=True)
        o = (jnp.dot(e_loc.astype(BF16), vw, preferred_element_type=F32)
             + jnp.dot(e_ctx.astype(BF16), cv, preferred_element_type=F32)) / l
        o_ref[:, cols] = jnp.where(lo, o[:GRID_W], o[GRID_W:]) * sz_ref[:, cols]


def _na_lat(q, k, v, ck, cv, bias, sz, n_lat):
    T = NA_ROWS * GRID_W
    past = ck.shape[1]
    qblk = pl.BlockSpec((GRID_W, D), lambda b, r: (b * NA_ROWS + r, 0))
    kvblk = pl.BlockSpec((T, D), lambda b, r: (b, 0))
    cblk = pl.BlockSpec((None, past, D), lambda b, r: (b, 0, 0))
    bblk = pl.BlockSpec((None, NA_HEADS, GRID_W, NA_WIN), lambda b, r: (r - _na_row_start(r), 0, 0, 0))
    return pl.pallas_call(
        _na_lat_kernel,
        grid=(n_lat, NA_ROWS),
        in_specs=[qblk, kvblk, kvblk, cblk, cblk, bblk, qblk],
        out_specs=qblk,
        out_shape=jax.ShapeDtypeStruct((n_lat * T, D), F32),
        compiler_params=_cparams(2),
        name="na_lat_attn",
    )(q, k, v, ck, cv, bias, sz)


def _na_layer(xc, xl, layer, mod, nw, p, n_lat):
    j = layer // 3
    w_in = p["na_w_in"][j].astype(BF16)
    w_out = p["na_w_out"][j].astype(BF16)
    row_ctx = lambda tok: 0
    row_lat = lambda tok: 1 + tok // (NA_ROWS * GRID_W)
    oc, kc, vc = _na_seq(xc, SEQ, layer, row_ctx, mod, nw, w_in, w_out)
    ql, kl, vl, szl = _na_in(xl, layer, row_lat, mod, nw, w_in, BF16)
    past = p["cache_k"].shape[2]
    ck = p["cache_k"][:n_lat, j].reshape(n_lat, past, D).astype(BF16)
    cv = p["cache_v"][:n_lat, j].reshape(n_lat, past, D).astype(BF16)
    gl = _na_lat(ql, kl, vl, ck, cv, _rpb_table(p["na_rpb"][j]), szl, n_lat)
    return oc, _out_proj(gl, xl, layer, row_lat, mod, w_out), kc, vc


SSD_IN_COLS = SSD_INNER + SSD_XBC


def _softplus(x):
    return jnp.maximum(x, 0.0) + jnp.log(1.0 + jnp.exp(-jnp.abs(x)))


def _ssd_in_kernel(tiles_per_seq, *refs):
    refs = list(refs)
    x_ref = refs.pop(0)
    if tiles_per_seq > 1:
        xp_ref = refs.pop(0)
        xn_ref = refs.pop(0)
    nw_ref, mod_ref, wz_ref, wx_ref, wdt_ref, dtb_ref, wc_ref, bc_ref, sz_ref, xbc_ref, dt_ref, xpad_ref = refs
    hb = _modnorm(x_ref[...], nw_ref[...], mod_ref[...]).astype(BF16)
    halo = None
    if tiles_per_seq > 1:
        t = pl.program_id(0) % tiles_per_seq
        xh = jnp.concatenate([xp_ref[...], xn_ref[...]], axis=0)
        halo = (_modnorm(xh, nw_ref[...], mod_ref[...]).astype(BF16), t > 0, t < tiles_per_seq - 1)
    dt_ref[...] = _ssd_project(hb, halo, wz_ref, wx_ref, wdt_ref, dtb_ref, wc_ref, bc_ref, xpad_ref, sz_ref, xbc_ref)


def _ssd_project(hb, halo, wz_ref, wx_ref, wdt_ref, dtb_ref, wc_ref, bc_ref, xpad_ref, sz_ref, xbc_ref):
    tm = hb.shape[0]
    HL = SSD_HALO
    RB = 128
    for j in range(SSD_XBC // D):
        c0 = j * D
        raw = _wdot(hb, wx_ref, c0, D)
        if halo is not None:
            rawh = _wdot(halo[0], wx_ref, c0, D)
            raw_prev = jnp.where(halo[1], rawh[0:HL], 0.0)
            raw_next = jnp.where(halo[2], rawh[HL:2 * HL], 0.0)
        else:
            raw_prev = raw_next = jnp.zeros((HL, D), F32)
        for cb in range(D // LANES):
            blk = j * (D // LANES) + cb
            lc = slice(blk * LANES, (blk + 1) * LANES)
            cc = slice(cb * LANES, (cb + 1) * LANES)
            xpad_ref[blk, 0:HL, :] = raw_prev[:, cc]
            xpad_ref[blk, HL:HL + tm, :] = raw[:, cc]
            xpad_ref[blk, HL + tm:HL + tm + HL, :] = raw_next[:, cc]
            for rb in range(tm // RB):
                acc = jnp.zeros((RB, LANES), F32)
                for k in range(SSD_CONV):
                    r0 = rb * RB + HL - SSD_CONV // 2 + k
                    acc = acc + xpad_ref[blk, r0:r0 + RB, :] * wc_ref[k:k + 1, lc]
                xbc_ref[rb * RB:(rb + 1) * RB, lc] = _silu(acc + bc_ref[:, lc])
    for j in range(SSD_INNER // D):
        sz_ref[:, j * D:(j + 1) * D] = _silu(_wdot(hb, wz_ref, j * D, D))
    raw = jnp.dot(hb, wdt_ref[...], preferred_element_type=F32) + dtb_ref[...]
    lane = lax.broadcasted_iota(jnp.int32, (1, LANES), 1)
    return jnp.where(lane < 2 * SSD_HEADS, _softplus(raw), 0.0)


def _ssd_in(x, seq_len, layer, row_fn, mod, nw, w_in, w_dt, dt_bias, w_conv, b_conv):
    M = x.shape[0]
    tm = TOKEN_TILE
    HL = SSD_HALO
    tps = seq_len // tm
    n_hblk = M // HL
    row = lambda i: (i, 0)
    items = [(pl.BlockSpec((tm, D), row), x)]
    if tps > 1:
        items += [(pl.BlockSpec((HL, D), lambda i: (jnp.maximum(i * (tm // HL) - 1, 0), 0)), x),
                  (pl.BlockSpec((HL, D), lambda i: (jnp.minimum((i + 1) * (tm // HL), n_hblk - 1), 0)), x)]
    items += [(_const_spec((1, D)), nw), (_mod_spec(layer, lambda i: row_fn(i * tm)), mod),
              _W(w_in, SSD_INNER, 0), _W(w_in, SSD_XBC, SSD_INNER),
              (_const_spec((D, LANES)), w_dt), (_const_spec((1, LANES)), dt_bias),
              (_const_spec((SSD_CONV + 1, SSD_XBC)), w_conv), (_const_spec((1, SSD_XBC)), b_conv)]
    in_specs, args, groups = _operands(items)
    return pl.pallas_call(
        _grouped(functools.partial(_ssd_in_kernel, tps), groups),
        grid=(M // tm,),
        in_specs=in_specs,
        out_specs=[pl.BlockSpec((tm, SSD_INNER), row), pl.BlockSpec((tm, SSD_XBC), row), pl.BlockSpec((tm, LANES), row)],
        out_shape=[jax.ShapeDtypeStruct((M, SSD_INNER), F32), jax.ShapeDtypeStruct((M, SSD_XBC), F32),
                   jax.ShapeDtypeStruct((M, LANES), F32)],
        scratch_shapes=[pltpu.VMEM((SSD_XBC // LANES, tm + 2 * HL, LANES), F32)],
        compiler_params=_cparams(1),
        name="ssd_in",
    )(*args)


def _cumsum_rows(a, reverse):
    n = a.shape[0]
    row = lax.broadcasted_iota(jnp.int32, a.shape, 0)
    k = 1
    while k < n:
        if reverse:
            a = a + jnp.where(row < n - k, pltpu.roll(a, n - k, 0), 0.0)
        else:
            a = a + jnp.where(row >= k, pltpu.roll(a, k, 0), 0.0)
        k *= 2
    return a


def _ssd_core_kernel(nc, has_h0, emit_state, *refs):
    refs = list(refs)
    dir_refs = [refs[0:2], refs[2:4]]
    alog_ref, dsk_ref = refs[4:6]
    refs = refs[6:]
    if has_h0:
        h0_ref = refs.pop(0)
    y_refs = [refs.pop(0), refs.pop(0)]
    if emit_state:
        so_ref = refs.pop(0)
    (st_ref,) = refs
    c = pl.program_id(1)
    n_blk = SSD_INNER // LANES

    @pl.when(c == 0)
    def _init():
        if has_h0:
            for dr in range(2):
                for j in range(n_blk):
                    st_ref[dr, :, j * LANES:(j + 1) * LANES] = h0_ref[dr, j * LANES:(j + 1) * LANES, :].T
        else:
            st_ref[...] = jnp.zeros(st_ref.shape, F32)

    A = -jnp.exp(alog_ref[...])
    for dr in range(2):
        x_ref, dt_ref = dir_refs[dr]
        y_ref = y_refs[dr]

        def x_at(c0, x_ref=x_ref):
            return x_ref[:, c0:c0 + LANES]

        def y_put(c0, y, y_ref=y_ref):
            y_ref[:, c0:c0 + LANES] = y

        _ssd_chunk(dr, x_at, dt_ref[...], A, st_ref, y_put, dsk_ref)

    if emit_state:
        @pl.when(c == nc - 1)
        def _emit():
            _ssd_emit_state(st_ref, so_ref)


def _ssd_emit_state(st_ref, so_ref):
    for dr in range(2):
        for j in range(SSD_INNER // LANES):
            so_ref[dr, j * LANES:(j + 1) * LANES, :] = st_ref[dr, :, j * LANES:(j + 1) * LANES].T


def _ssd_group_terms(x_at):
    N = SSD_STATE
    terms = []
    for g in range(SSD_GROUPS):
        bg_t = x_at(SSD_INNER + g * N).T
        cg = x_at(SSD_INNER + SSD_GN + g * N).astype(BF16)
        terms.append((bg_t, cg, jnp.dot(cg, bg_t.astype(BF16), preferred_element_type=F32)))
    return terms


def _ssd_chunk(dr, x_at, dt, A, st_ref, y_put, dsk_ref, group_terms=None):
    Q, N, P = SSD_CHUNK, SSD_STATE, SSD_HEAD_DIM
    lo = _head_masks()
    ri = lax.broadcasted_iota(jnp.int32, (Q, Q), 0)
    ci = lax.broadcasted_iota(jnp.int32, (Q, Q), 1)
    cum = _cumsum_rows(dt * A, reverse=(dr == 1)) * LOG2E
    cum_t = cum.T
    dt_t = dt.T
    tot_row = Q - 1 if dr == 0 else 0
    mask = (ri >= ci) if dr == 0 else (ri <= ci)
    if group_terms is None:
        group_terms = _ssd_group_terms(x_at)
    for g in range(SSD_GROUPS):
        bg_t, cg, cb = group_terms[g]
        for pp in range(SSD_HEADS // SSD_GROUPS // 2):
            head = g * (SSD_HEADS // SSD_GROUPS) + 2 * pp
            c0 = head * P
            li = dr * SSD_HEADS + head
            xpair = x_at(c0)
            xpair_b = xpair.astype(BF16)
            cum_cols = [cum[:, li:li + 1], cum[:, li + 1:li + 2]]
            cum_x = jnp.where(lo, cum_cols[0], cum_cols[1])
            yd, sd = [], []
            for hh in range(2):
                dt_row = dt_t[li + hh:li + hh + 1, :]
                cum_row = cum_t[li + hh:li + hh + 1, :]
                decay = jnp.exp2(jnp.where(mask, cum_cols[hh] - cum_row, -jnp.inf))
                yd.append(jnp.dot((cb * decay * dt_row).astype(BF16), xpair_b, preferred_element_type=F32))
                to_end = jnp.exp2(cum_row[:, tot_row:tot_row + 1] - cum_row)
                sd.append(jnp.dot((bg_t * (dt_row * to_end)).astype(BF16), xpair_b, preferred_element_type=F32))
            st = st_ref[dr, :, c0:c0 + LANES]
            y = (jnp.where(lo, yd[0], yd[1])
                 + jnp.dot(cg, st.astype(BF16), preferred_element_type=F32) * jnp.exp2(cum_x))
            if dr == 0:
                y = y + xpair * dsk_ref[:, c0:c0 + LANES]
            y_put(c0, y)
            tot = cum_x[tot_row:tot_row + 1, :]
            st_ref[dr, :, c0:c0 + LANES] = st * jnp.exp2(tot) + jnp.where(lo, sd[0], sd[1])


def _ssd_seq_kernel(x_ref, nw_ref, mod_ref, wz_ref, wx_ref, wdt_ref, dtb_ref, wc_ref, bc_ref, alog_ref, dsk_ref, gnw_ref, wo_ref,
                    o_ref, so_ref, xpad_ref, xbc_ref, sz_ref, y_ref, st_ref):
    Q = SSD_CHUNK
    nc = x_ref.shape[0] // Q
    x = x_ref[...]
    hb = _modnorm(x, nw_ref[...], mod_ref[...]).astype(BF16)
    dt = _ssd_project(hb, None, wz_ref, wx_ref, wdt_ref, dtb_ref, wc_ref, bc_ref, xpad_ref, sz_ref, xbc_ref)
    st_ref[...] = jnp.zeros(st_ref.shape, F32)
    A = -jnp.exp(alog_ref[...])

    def x_at_chunk(ck):
        return lambda c0: xbc_ref[ck * Q:(ck + 1) * Q, c0:c0 + LANES]

    terms = [_ssd_group_terms(x_at_chunk(ck)) for ck in range(nc)]
    for dr in range(2):
        for ck in (range(nc) if dr == 0 else reversed(range(nc))):
            rows = slice(ck * Q, (ck + 1) * Q)

            def y_put(c0, y, rows=rows, dr=dr):
                if dr == 0:
                    y_ref[rows, c0:c0 + LANES] = y
                else:
                    y_ref[rows, c0:c0 + LANES] += y

            _ssd_chunk(dr, x_at_chunk(ck), dt[rows], A, st_ref, y_put, dsk_ref, terms[ck])
    _ssd_emit_state(st_ref, so_ref)
    g = _rms(y_ref[...] * sz_ref[...], gnw_ref[...]).astype(BF16)
    o_ref[...] = x + mod_ref[:, 2 * D:3 * D] * _wdot(g, wo_ref, 0, D)


def _ssd_seq(x, seq_len, layer, row_fn, mod, nw, w_in, w_dt, dt_bias, w_conv, b_conv, a_log, d_skip, norm_w, w_out):
    M = x.shape[0]
    n_seq = M // seq_len
    HL = SSD_HALO
    row = lambda i: (i, 0)
    st_blk = pl.BlockSpec((None, 2, SSD_INNER, SSD_STATE), lambda i: (i, 0, 0, 0))
    in_specs, args, groups = _operands([
        (pl.BlockSpec((seq_len, D), row), x), (_const_spec((1, D)), nw),
        (_mod_spec(layer, lambda i: row_fn(i * seq_len)), mod),
        _W(w_in, SSD_INNER, 0), _W(w_in, SSD_XBC, SSD_INNER),
        (_const_spec((D, LANES)), w_dt), (_const_spec((1, LANES)), dt_bias),
        (_const_spec((SSD_CONV + 1, SSD_XBC)), w_conv), (_const_spec((1, SSD_XBC)), b_conv),
        (_const_spec((1, LANES)), a_log), (_const_spec((1, SSD_INNER)), d_skip),
        (_const_spec((1, SSD_INNER)), norm_w), _W(w_out)])
    return pl.pallas_call(
        _grouped(_ssd_seq_kernel, groups),
        grid=(n_seq,),
        in_specs=in_specs,
        out_specs=[pl.BlockSpec((seq_len, D), row), st_blk],
        out_shape=[jax.ShapeDtypeStruct((M, D), F32), jax.ShapeDtypeStruct((n_seq, 2, SSD_INNER, SSD_STATE), F32)],
        scratch_shapes=[pltpu.VMEM((SSD_XBC // LANES, seq_len + 2 * HL, LANES), F32),
                        pltpu.VMEM((seq_len, SSD_XBC), F32), pltpu.VMEM((seq_len, SSD_INNER), F32),
                        pltpu.VMEM((seq_len, SSD_INNER), F32), pltpu.VMEM((2, SSD_STATE, SSD_INNER), F32)],
        compiler_params=_cparams(1),
        name="ssd_seq",
    )(*args)


def _ssd_core(xbc, dtp, a_log, d_skip, h0, n_seq, seq_len, emit_state):
    Q = SSD_CHUNK
    nc = seq_len // Q
    M = n_seq * seq_len

    def chunk(dr):
        return (lambda b, c: b * nc + c) if dr == 0 else (lambda b, c: b * nc + nc - 1 - c)

    in_specs, args = [], []
    for dr in range(2):
        ch = chunk(dr)
        in_specs += [pl.BlockSpec((Q, SSD_XBC), lambda b, c, ch=ch: (ch(b, c), 0)),
                     pl.BlockSpec((Q, LANES), lambda b, c, ch=ch: (ch(b, c), 0))]
        args += [xbc, dtp]
    in_specs += [_const_spec((1, LANES)), _const_spec((1, SSD_INNER))]
    args += [a_log, d_skip]
    st_blk = pl.BlockSpec((None, 2, SSD_INNER, SSD_STATE), lambda b, c: (b, 0, 0, 0))
    has_h0 = h0 is not None
    if has_h0:
        in_specs.append(st_blk)
        args.append(h0)
    out_specs = [pl.BlockSpec((Q, SSD_INNER), lambda b, c: (chunk(0)(b, c), 0)),
                 pl.BlockSpec((Q, SSD_INNER), lambda b, c: (chunk(1)(b, c), 0))]
    out_shape = [jax.ShapeDtypeStruct((M, SSD_INNER), F32)] * 2
    if emit_state:
        out_specs.append(st_blk)
        out_shape.append(jax.ShapeDtypeStruct((n_seq, 2, SSD_INNER, SSD_STATE), F32))
    return pl.pallas_call(
        functools.partial(_ssd_core_kernel, nc, has_h0, emit_state),
        grid=(n_seq, nc),
        in_specs=in_specs,
        out_specs=out_specs,
        out_shape=out_shape,
        scratch_shapes=[pltpu.VMEM((2, SSD_STATE, SSD_INNER), F32)],
        compiler_params=_cparams(2),
        name="ssd_core",
    )(*args)


def _ssd_out_kernel(yf_ref, yb_ref, sz_ref, x_ref, mod_ref, nw_ref, w_ref, o_ref):
    g = _rms((yf_ref[...] + yb_ref[...]) * sz_ref[...], nw_ref[...]).astype(BF16)
    o_ref[...] = x_ref[...] + mod_ref[:, 2 * D:3 * D] * _wdot(g, w_ref, 0, D)


def _ssd_out(yf, yb, sz, x, layer, row_fn, mod, norm_w, w_out):
    M = x.shape[0]
    tm = TOKEN_TILE
    row = lambda i: (i, 0)
    wide = pl.BlockSpec((tm, SSD_INNER), row)
    in_specs, args, groups = _operands([
        (wide, yf), (wide, yb), (wide, sz), (pl.BlockSpec((tm, D), row), x),
        (_mod_spec(layer, lambda i: row_fn(i * tm)), mod), (_const_spec((1, SSD_INNER)), norm_w), _W(w_out)])
    return pl.pallas_call(
        _grouped(_ssd_out_kernel, groups),
        grid=(M // tm,),
        in_specs=in_specs,
        out_specs=pl.BlockSpec((tm, D), row),
        out_shape=jax.ShapeDtypeStruct((M, D), F32),
        compiler_params=_cparams(1),
        name="ssd_out",
    )(*args)


def _ssd_layer(xc, xl, layer, mod, nw, p, n_lat):
    j = layer // 3
    w_in = p["ssd_w_in"][j].astype(BF16)
    w_dt = jnp.pad(w_in[:, SSD_IN_COLS:], ((0, 0), (0, LANES - 2 * SSD_HEADS)))
    dt_bias = jnp.pad(p["ssd_dt_bias"][j].reshape(1, -1), ((0, 0), (0, LANES - 2 * SSD_HEADS)))
    a_log = jnp.pad(p["ssd_a_log"][j].reshape(1, -1), ((0, 0), (0, LANES - 2 * SSD_HEADS)))
    w_conv = jnp.pad(p["ssd_w_conv"][j], ((0, 1), (0, 0)))
    b_conv = p["ssd_b_conv"][j][None]
    d_skip = jnp.repeat(p["ssd_d"][j], SSD_HEAD_DIM)[None]
    norm_w = p["ssd_norm_w"][j][None]
    w_out = p["ssd_w_out"][j].astype(BF16)
    row_ctx = lambda tok: 0
    row_lat = lambda tok: 1 + tok // DEC_SEQ
    oc, s_new = _ssd_seq(xc, SEQ, layer, row_ctx, mod, nw, w_in, w_dt, dt_bias, w_conv, b_conv, a_log, d_skip, norm_w, w_out)
    szl, xbc_l, dt_l = _ssd_in(xl, DEC_SEQ, layer, row_lat, mod, nw, w_in, w_dt, dt_bias, w_conv, b_conv)
    h0 = p["state_ssm"][:n_lat, j].reshape(n_lat, 2, SSD_INNER, SSD_STATE)
    yfl, ybl = _ssd_core(xbc_l, dt_l, a_log, d_skip, h0, n_lat, DEC_SEQ, False)
    ol = _ssd_out(yfl, ybl, szl, xl, layer, row_lat, mod, norm_w, w_out)
    return oc, ol, s_new


def kernel(x_prompt, x_sample, cache_k, cache_v, state_ssm, c, c_ctx, ada_w, ada_b, norm_w, final_norm_w, conv_w_in, conv_b_in, conv_w_dw, conv_b_dw, conv_ln_w, conv_ln_b, conv_w_out, conv_b_out, na_w_in, na_rpb, na_w_out, ssd_w_in, ssd_w_conv, ssd_b_conv, ssd_a_log, ssd_dt_bias, ssd_d, ssd_norm_w, ssd_w_out):
    p = dict(cache_k=cache_k, cache_v=cache_v, state_ssm=state_ssm,
             conv_w_in=conv_w_in, conv_b_in=conv_b_in, conv_w_dw=conv_w_dw, conv_b_dw=conv_b_dw,
             conv_ln_w=conv_ln_w, conv_ln_b=conv_ln_b, conv_w_out=conv_w_out, conv_b_out=conv_b_out,
             na_w_in=na_w_in, na_rpb=na_rpb, na_w_out=na_w_out,
             ssd_w_in=ssd_w_in, ssd_w_conv=ssd_w_conv, ssd_b_conv=ssd_b_conv, ssd_a_log=ssd_a_log,
             ssd_dt_bias=ssd_dt_bias, ssd_d=ssd_d, ssd_norm_w=ssd_norm_w, ssd_w_out=ssd_w_out)
    n_ctx, n_lat = x_prompt.shape[0], x_sample.shape[0]
    assert x_prompt.shape[1:] == (SEQ, D) and x_sample.shape[1:] == (DEC_SEQ, D)
    assert (DEPTH - 1) % 3 == 0, "the final RMSNorm is fused into a trailing Conformer layer"
    xc = x_prompt.reshape(n_ctx * SEQ, D)
    xl = x_sample.reshape(n_lat * DEC_SEQ, D)
    cond8 = jnp.concatenate([c_ctx[None], c, jnp.zeros((8 - 1 - n_lat, D), F32)], axis=0)
    mod = _modulation(cond8, ada_w, ada_b).reshape(DEPTH, 8, 1, 3 * D)
    row_ctx = lambda tok: 0
    row_lat = lambda tok: 1 + tok // DEC_SEQ
    new_k, new_v, new_s = [], [], []
    for i in range(DEPTH):
        kind, j = i % 3, i // 3
        nw = norm_w[i][None]
        if kind == 0:
            cp = _prep_conv(p, j)
            fw = final_norm_w[None] if i == DEPTH - 1 else None
            xc = _conv_layer(xc, SEQ, i, row_ctx, mod, nw, *cp, fw)
            xl = _conv_layer(xl, DEC_SEQ, i, row_lat, mod, nw, *cp, fw)
        elif kind == 1:
            xc, xl, k_c, v_c = _na_layer(xc, xl, i, mod, nw, p, n_lat)
            new_k.append(k_c.reshape(n_ctx, SEQ, NA_HEADS, NA_HEAD_DIM))
            new_v.append(v_c.reshape(n_ctx, SEQ, NA_HEADS, NA_HEAD_DIM))
        else:
            xc, xl, s_c = _ssd_layer(xc, xl, i, mod, nw, p, n_lat)
            new_s.append(s_c.reshape(n_ctx, 2, SSD_HEADS, SSD_HEAD_DIM, SSD_STATE))
    return (xc.reshape(n_ctx, SEQ, D), xl.reshape(n_lat, DEC_SEQ, D),
            jnp.stack(new_k, axis=1), jnp.stack(new_v, axis=1), jnp.stack(new_s, axis=1))
```

```python
import functools

import jax
import jax.numpy as jnp
from jax import lax
from jax.experimental import pallas as pl
from jax.experimental.pallas import tpu as pltpu

F32 = jnp.float32
BF16 = jnp.bfloat16

D = 1024
DEPTH = 4
SEQ = 256
DEC_SEQ = 2048
GRID_W = 64
RMS_EPS = 1e-6
LN_EPS = 1e-5
LOG2E = 1.4426950408889634
CONF_K = 31
NA_HEADS = 16
NA_HEAD_DIM = 64
NA_KH = 8
NA_KW = 16
NEG_INF = -1e30
SSD_INNER = 2048
SSD_HEADS = 32
SSD_HEAD_DIM = 64
SSD_STATE = 128
SSD_GROUPS = 4
SSD_GN = SSD_GROUPS * SSD_STATE
SSD_CONV = 7
SSD_CHUNK = 128
SSD_XBC = SSD_INNER + 2 * SSD_GN

LANES = 128
TOKEN_TILE = 256
W_CHUNK = 512
CONV_HALO = 16
SSD_HALO = 8
VMEM_LIMIT = 56 * 1024 * 1024


def _cparams(n_axes):
    return pltpu.CompilerParams(dimension_semantics=("arbitrary",) * n_axes,
                                vmem_limit_bytes=VMEM_LIMIT)


def _const_spec(shape):
    nd = len(shape)
    return pl.BlockSpec(shape, lambda *_: (0,) * nd, pipeline_mode=pl.Buffered(1))


class _W:
    def __init__(self, w, ncols=None, col0=0):
        self.w, self.col0 = w, col0
        self.ncols = w.shape[1] if ncols is None else ncols


def _operands(items):
    specs, args, groups = [], [], []
    for it in items:
        if isinstance(it, _W):
            n, b0, K = it.ncols // W_CHUNK, it.col0 // W_CHUNK, it.w.shape[0]
            for c in range(b0, b0 + n):
                specs.append(pl.BlockSpec((K, W_CHUNK), lambda *_, c=c: (0, c), pipeline_mode=pl.Buffered(1)))
                args.append(it.w)
            groups.append(n)
        else:
            specs.append(it[0])
            args.append(it[1])
            groups.append(None)
    return specs, args, groups


def _grouped(body, groups):
    def kernel_fn(*refs):
        refs = list(refs)
        packed = []
        for g in groups:
            if g is None:
                packed.append(refs.pop(0))
            else:
                packed.append(tuple(refs[:g]))
                del refs[:g]
        return body(*packed, *refs)
    return kernel_fn


def _wdot(h, w_refs, c0, n):
    outs = []
    c = c0
    while c < c0 + n:
        j, off = divmod(c, W_CHUNK)
        m = min(W_CHUNK - off, c0 + n - c)
        outs.append(jnp.dot(h, w_refs[j][:, off:off + m], preferred_element_type=F32))
        c += m
    return outs[0] if len(outs) == 1 else jnp.concatenate(outs, axis=1)


def _sigmoid(x):
    return 1.0 / (1.0 + jnp.exp(-x))


def _silu(x):
    return x * _sigmoid(x)


def _rms(x, w):
    ms = jnp.mean(x * x, axis=-1, keepdims=True)
    return x * lax.rsqrt(ms + RMS_EPS) * w


def _modnorm(x, nw, mod):
    return _rms(x, nw) * (1.0 + mod[:, D:2 * D]) + mod[:, :D]


def _mod_spec(layer, row_fn):
    return pl.BlockSpec((None, None, 1, 3 * D), lambda i, *_: (layer, row_fn(i), 0, 0))


def _mod_kernel(cond_ref, w_ref, b_ref, o_ref):
    s = _silu(cond_ref[...]).astype(BF16)
    o_ref[...] = jnp.dot(s, w_ref[...].astype(BF16), preferred_element_type=F32) + b_ref[...]


def _modulation(cond8, ada_w, ada_b):
    tn = 1024
    return pl.pallas_call(
        _mod_kernel,
        grid=(DEPTH, 3 * D // tn),
        in_specs=[pl.BlockSpec((8, D), lambda i, j: (0, 0)),
                  pl.BlockSpec((None, D, tn), lambda i, j: (i, 0, j)),
                  pl.BlockSpec((None, 1, tn), lambda i, j: (i, 0, j))],
        out_specs=pl.BlockSpec((None, 8, tn), lambda i, j: (i, 0, j)),
        out_shape=jax.ShapeDtypeStruct((DEPTH, 8, 3 * D), F32),
        compiler_params=_cparams(2),
        name="adaln_mod",
    )(cond8, ada_w, ada_b.reshape(DEPTH, 1, 3 * D))


CONV_CHUNK = 256


def _conv_kernel(tiles_per_seq, final, *refs):
    refs = list(refs)
    x_ref = refs.pop(0)
    if tiles_per_seq > 1:
        xp_ref = refs.pop(0)
        xn_ref = refs.pop(0)
    nw_ref, mod_ref, wi_ref, bi_ref, wdw_ref, bdw_ref, lnw_ref, lnb_ref, wo_ref, bo_ref = refs[:10]
    refs = refs[10:]
    if final:
        fw_ref = refs.pop(0)
    o_ref, ubuf, cbuf = refs
    T = x_ref.shape[0]
    H = CONV_HALO
    RB = 128

    hb = _modnorm(x_ref[...], nw_ref[...], mod_ref[...]).astype(BF16)
    if tiles_per_seq > 1:
        t = pl.program_id(0) % tiles_per_seq
        hh = _modnorm(jnp.concatenate([xp_ref[...], xn_ref[...]], axis=0), nw_ref[...], mod_ref[...]).astype(BF16)

    def glu(h, cols):
        v = _wdot(h, wi_ref, cols.start, CONV_CHUNK) + bi_ref[:, cols]
        gcols = slice(D + cols.start, D + cols.stop)
        g = _wdot(h, wi_ref, gcols.start, CONV_CHUNK) + bi_ref[:, gcols]
        return v * _sigmoid(g)

    for ch in range(D // CONV_CHUNK):
        cols = slice(ch * CONV_CHUNK, (ch + 1) * CONV_CHUNK)
        u_main = glu(hb, cols)
        if tiles_per_seq > 1:
            uh = glu(hh, cols)
            u_prev = jnp.where(t > 0, uh[0:H], 0.0)
            u_next = jnp.where(t < tiles_per_seq - 1, uh[H:2 * H], 0.0)
        else:
            u_prev = u_next = jnp.zeros((H, CONV_CHUNK), F32)
        for cb in range(CONV_CHUNK // LANES):
            blk = ch * (CONV_CHUNK // LANES) + cb
            lc = slice(blk * LANES, (blk + 1) * LANES)
            cc = slice(cb * LANES, (cb + 1) * LANES)
            ubuf[blk, 0:H, :] = u_prev[:, cc]
            ubuf[blk, H:H + T, :] = u_main[:, cc]
            ubuf[blk, H + T:H + T + H, :] = u_next[:, cc]
            for rb in range(T // RB):
                acc = jnp.zeros((RB, LANES), F32)
                for k in range(CONF_K):
                    r0 = rb * RB + H - CONF_K // 2 + k
                    acc = acc + ubuf[blk, r0:r0 + RB, :] * wdw_ref[k:k + 1, lc]
                cbuf[rb * RB:(rb + 1) * RB, lc] = acc + bdw_ref[:, lc]

    sz = _silu(_wdot(hb, wi_ref, 2 * D, D) + bi_ref[:, 2 * D:3 * D])
    c = cbuf[...]
    mu = jnp.mean(c, axis=-1, keepdims=True)
    xc = c - mu
    var = jnp.mean(xc * xc, axis=-1, keepdims=True)
    ln = xc * lax.rsqrt(var + LN_EPS) * lnw_ref[...] + lnb_ref[...]
    g = (_silu(ln) * sz).astype(BF16)
    y = _wdot(g, wo_ref, 0, D) + bo_ref[...]
    xn = x_ref[...] + mod_ref[:, 2 * D:3 * D] * y
    if final:
        xn = _rms(xn, fw_ref[...])
    o_ref[...] = xn


def _conv_layer(x, seq_len, layer, row_fn, mod, nw, w_in, b_in, w_dw, b_dw, ln_w, ln_b, w_out, b_out, final_w):
    M = x.shape[0]
    T = TOKEN_TILE
    tps = seq_len // T
    H = CONV_HALO
    n_halo_blocks = M // H
    row = lambda i: (i, 0)
    items = [(pl.BlockSpec((T, D), row), x)]
    if tps > 1:
        items += [(pl.BlockSpec((H, D), lambda i: (jnp.maximum(i * (T // H) - 1, 0), 0)), x),
                  (pl.BlockSpec((H, D), lambda i: (jnp.minimum((i + 1) * (T // H), n_halo_blocks - 1), 0)), x)]
    items += [(_const_spec((1, D)), nw), (_mod_spec(layer, lambda i: row_fn(i * T)), mod),
              _W(w_in), (_const_spec((1, 3 * D)), b_in),
              (_const_spec((CONF_K + 1, D)), w_dw), (_const_spec((1, D)), b_dw),
              (_const_spec((1, D)), ln_w), (_const_spec((1, D)), ln_b),
              _W(w_out), (_const_spec((1, D)), b_out)]
    final = final_w is not None
    if final:
        items.append((_const_spec((1, D)), final_w))
    in_specs, args, groups = _operands(items)
    return pl.pallas_call(
        _grouped(functools.partial(_conv_kernel, tps, final), groups),
        grid=(M // T,),
        in_specs=in_specs,
        out_specs=pl.BlockSpec((T, D), row),
        out_shape=jax.ShapeDtypeStruct((M, D), F32),
        scratch_shapes=[pltpu.VMEM((D // LANES, T + 2 * H, LANES), F32), pltpu.VMEM((T, D), F32)],
        compiler_params=_cparams(1),
        name="conv_layer",
    )(*args)


def _prep_conv(p, j):
    w_dw = jnp.pad(p["conv_w_dw"][j], ((0, 1), (0, 0)))
    return (p["conv_w_in"][j].astype(BF16), p["conv_b_in"][j][None], w_dw, p["conv_b_dw"][j][None],
            p["conv_ln_w"][j][None], p["conv_ln_b"][j][None], p["conv_w_out"][j].astype(BF16), p["conv_b_out"][j][None])


def _out_proj_kernel(g_ref, x_ref, mod_ref, w_ref, o_ref):
    y = _wdot(g_ref[...].astype(BF16), w_ref, 0, D)
    o_ref[...] = x_ref[...] + mod_ref[:, 2 * D:3 * D] * y


def _out_proj(g, x, layer, row_fn, mod, w_out):
    M, K = g.shape
    tm = TOKEN_TILE
    row = lambda i: (i, 0)
    in_specs, args, groups = _operands([
        (pl.BlockSpec((tm, K), row), g), (pl.BlockSpec((tm, D), row), x),
        (_mod_spec(layer, lambda i: row_fn(i * tm)), mod), _W(w_out)])
    return pl.pallas_call(
        _grouped(_out_proj_kernel, groups),
        grid=(M // tm,),
        in_specs=in_specs,
        out_specs=pl.BlockSpec((tm, D), row),
        out_shape=jax.ShapeDtypeStruct((M, D), F32),
        compiler_params=_cparams(1),
        name="out_proj",
    )(*args)


NA_SCALE = NA_HEAD_DIM ** -0.5
NA_ROWS = 32
NA_WIN = NA_KH * GRID_W
NT_DIMS = (((1,), (1,)), ((), ()))


def _na_in_kernel(x_ref, nw_ref, mod_ref, w_ref, q_ref, k_ref, v_ref, sz_ref):
    hb = _modnorm(x_ref[...], nw_ref[...], mod_ref[...]).astype(BF16)
    q_ref[...] = _wdot(hb, w_ref, 0, D).astype(q_ref.dtype)
    k_ref[...] = _wdot(hb, w_ref, D, D).astype(k_ref.dtype)
    v_ref[...] = _wdot(hb, w_ref, 2 * D, D).astype(v_ref.dtype)
    sz_ref[...] = _silu(_wdot(hb, w_ref, 3 * D, D))


def _na_in(x, layer, row_fn, mod, nw, w_in, kv_dtype):
    M = x.shape[0]
    tm = TOKEN_TILE
    row = lambda i: (i, 0)
    blk = pl.BlockSpec((tm, D), row)
    in_specs, args, groups = _operands([
        (blk, x), (_const_spec((1, D)), nw), (_mod_spec(layer, lambda i: row_fn(i * tm)), mod), _W(w_in)])
    return pl.pallas_call(
        _grouped(_na_in_kernel, groups),
        grid=(M // tm,),
        in_specs=in_specs,
        out_specs=[blk, blk, blk, blk],
        out_shape=[jax.ShapeDtypeStruct((M, D), BF16), jax.ShapeDtypeStruct((M, D), kv_dtype),
                   jax.ShapeDtypeStruct((M, D), kv_dtype), jax.ShapeDtypeStruct((M, D), F32)],
        compiler_params=_cparams(1),
        name="na_in",
    )(*args)


def _head_masks():
    lane = lax.broadcasted_iota(jnp.int32, (1, LANES), 1)
    return lane < NA_HEAD_DIM


def _na_seq_kernel(x_ref, nw_ref, mod_ref, wi_ref, wo_ref, o_ref, k_ref, v_ref, g_ref):
    n = x_ref.shape[0]
    lo = _head_masks()
    x = x_ref[...]
    hb = _modnorm(x, nw_ref[...], mod_ref[...]).astype(BF16)
    hpc = CONV_CHUNK // NA_HEAD_DIM
    for ch in range(D // CONV_CHUNK):
        cols = slice(ch * CONV_CHUNK, (ch + 1) * CONV_CHUNK)
        q = _wdot(hb, wi_ref, cols.start, CONV_CHUNK).astype(BF16)
        k = _wdot(hb, wi_ref, D + cols.start, CONV_CHUNK)
        v = _wdot(hb, wi_ref, 2 * D + cols.start, CONV_CHUNK)
        sz = _silu(_wdot(hb, wi_ref, 3 * D + cols.start, CONV_CHUNK))
        k_ref[:, ch * hpc:(ch + 1) * hpc, :] = k.reshape(n, hpc, NA_HEAD_DIM)
        v_ref[:, ch * hpc:(ch + 1) * hpc, :] = v.reshape(n, hpc, NA_HEAD_DIM)
        kb = k.astype(BF16)
        vb = v.astype(BF16)
        for pp in range(CONV_CHUNK // LANES):
            pc = slice(pp * LANES, (pp + 1) * LANES)
            qp = q[:, pc]
            zero = jnp.zeros_like(qp)
            qs = jnp.concatenate([jnp.where(lo, qp, zero), jnp.where(lo, zero, qp)], axis=0)
            s = lax.dot_general(qs, kb[:, pc], NT_DIMS, preferred_element_type=F32) * NA_SCALE
            e = jnp.exp(s - jnp.max(s, axis=-1, keepdims=True))
            l = jnp.sum(e, axis=-1, keepdims=True)
            o = jnp.dot(e.astype(BF16), vb[:, pc], preferred_element_type=F32) / l
            g = jnp.where(lo, o[:n], o[n:]) * sz[:, pc]
            g_ref[:, cols.start + pp * LANES:cols.start + (pp + 1) * LANES] = g.astype(BF16)
    y = _wdot(g_ref[...], wo_ref, 0, D)
    o_ref[...] = x + mod_ref[:, 2 * D:3 * D] * y


def _na_seq(x, seq_len, layer, row_fn, mod, nw, w_in, w_out):
    M = x.shape[0]
    row = lambda i: (i, 0)
    kv_blk = pl.BlockSpec((seq_len, NA_HEADS, NA_HEAD_DIM), lambda i: (i, 0, 0))
    kv_shape = jax.ShapeDtypeStruct((M, NA_HEADS, NA_HEAD_DIM), F32)
    in_specs, args, groups = _operands([
        (pl.BlockSpec((seq_len, D), row), x), (_const_spec((1, D)), nw),
        (_mod_spec(layer, lambda i: row_fn(i * seq_len)), mod), _W(w_in), _W(w_out)])
    return pl.pallas_call(
        _grouped(_na_seq_kernel, groups),
        grid=(M // seq_len,),
        in_specs=in_specs,
        out_specs=[pl.BlockSpec((seq_len, D), row), kv_blk, kv_blk],
        out_shape=[jax.ShapeDtypeStruct((M, D), F32), kv_shape, kv_shape],
        scratch_shapes=[pltpu.VMEM((seq_len, D), BF16)],
        compiler_params=_cparams(1),
        name="na_seq",
    )(*args)


def _rpb_kernel(rpb_ref, o_ref):
    h = pl.program_id(0)
    c = lax.broadcasted_iota(jnp.int32, (GRID_W, LANES), 0)
    lane = lax.broadcasted_iota(jnp.int32, (GRID_W, LANES), 1)
    kc = lane & (GRID_W - 1)
    upper = lax.broadcasted_iota(jnp.int32, (1, LANES), 1) >= GRID_W
    cs = jnp.clip(c - NA_KW // 2, 0, GRID_W - NA_KW)
    valid = (kc >= cs) & (kc < cs + NA_KW)
    n_pair = 2 * NA_KH - 2

    def toeplitz(dri, lane0):
        row = rpb_ref[pl.ds(h * (2 * NA_KH - 1) + dri, 1), :]
        rows = jnp.broadcast_to(row, (GRID_W, LANES))
        return pltpu.roll(rows, (LANES - (NA_KW - 1) + lane0) % LANES, 1, stride=1, stride_axis=0)

    for dri in range(n_pair):
        tile = jnp.where(upper, toeplitz(dri + 1, GRID_W), toeplitz(dri, 0))
        tile = jnp.where(valid, tile, NEG_INF)
        for idx in range(NA_KH):
            jj2 = dri - (NA_KH - 1) + idx
            if jj2 % 2 == 0 and 0 <= jj2 // 2 < NA_KH // 2:
                jj = jj2 // 2
                o_ref[idx, :, jj * LANES:(jj + 1) * LANES] = tile


def _rpb_table(rpb):
    n_rows = NA_HEADS * (2 * NA_KH - 1)
    rows = jnp.pad(rpb.reshape(n_rows, 2 * NA_KW - 1), ((0, 0), (0, LANES - (2 * NA_KW - 1))))
    return pl.pallas_call(
        _rpb_kernel,
        grid=(NA_HEADS,),
        in_specs=[_const_spec((n_rows, LANES))],
        out_specs=pl.BlockSpec((NA_KH, None, GRID_W, NA_WIN), lambda h: (0, h, 0, 0)),
        out_shape=jax.ShapeDtypeStruct((NA_KH, NA_HEADS, GRID_W, NA_WIN), F32),
        compiler_params=_cparams(1),
        name="na_rpb_table",
    )(rows)


def _na_row_start(r):
    return jnp.clip(r - NA_KH // 2, 0, NA_ROWS - NA_KH)


def _na_lat_kernel(q_ref, k_ref, v_ref, ck_ref, cv_ref, bias_ref, sz_ref, o_ref):
    lo = _head_masks()
    k0 = pl.multiple_of(_na_row_start(pl.program_id(1)) * GRID_W, GRID_W)
    for p in range(NA_HEADS // 2):
        cols = slice(p * LANES, (p + 1) * LANES)
        qp = q_ref[:, cols]
        kw = k_ref[pl.ds(k0, NA_WIN), cols]
        vw = v_ref[pl.ds(k0, NA_WIN), cols]
        ck = ck_ref[:, cols]
        cv = cv_ref[:, cols]
        zero = jnp.zeros_like(qp)
        qs = jnp.concatenate([jnp.where(lo, qp, zero), jnp.where(lo, zero, qp)], axis=0)
        bias = bias_ref[2 * p:2 * p + 2].reshape(2 * GRID_W, NA_WIN)
        s_loc = lax.dot_general(qs, kw, NT_DIMS, preferred_element_type=F32) * NA_SCALE + bias
        s_ctx = lax.dot_general(qs, ck, NT_DIMS, preferred_element_type=F32) * NA_SCALE
        m = jnp.maximum(jnp.max(s_loc, axis=-1, keepdims=True), jnp.max(s_ctx, axis=-1, keepdims=True))
        e_loc = jnp.exp(s_loc - m)
        e_ctx = jnp.exp(s_ctx - m)
        l = jnp.sum(e_loc, axis=-1, keepdims=True) + jnp.sum(e_ctx, axis=-1, keepdims=True)
        o = (jnp.dot(e_loc.astype(BF16), vw, preferred_element_type=F32)
             + jnp.dot(e_ctx.astype(BF16), cv, preferred_element_type=F32)) / l
        o_ref[:, cols] = jnp.where(lo, o[:GRID_W], o[GRID_W:]) * sz_ref[:, cols]


def _na_lat(q, k, v, ck, cv, bias, sz, n_lat):
    T = NA_ROWS * GRID_W
    past = ck.shape[1]
    qblk = pl.BlockSpec((GRID_W, D), lambda b, r: (b * NA_ROWS + r, 0))
    kvblk = pl.BlockSpec((T, D), lambda b, r: (b, 0))
    cblk = pl.BlockSpec((None, past, D), lambda b, r: (b, 0, 0))
    bblk = pl.BlockSpec((None, NA_HEADS, GRID_W, NA_WIN), lambda b, r: (r - _na_row_start(r), 0, 0, 0))
    return pl.pallas_call(
        _na_lat_kernel,
        grid=(n_lat, NA_ROWS),
        in_specs=[qblk, kvblk, kvblk, cblk, cblk, bblk, qblk],
        out_specs=qblk,
        out_shape=jax.ShapeDtypeStruct((n_lat * T, D), F32),
        compiler_params=_cparams(2),
        name="na_lat_attn",
    )(q, k, v, ck, cv, bias, sz)


def _na_layer(xc, xl, layer, mod, nw, p, n_lat):
    j = layer // 3
    w_in = p["na_w_in"][j].astype(BF16)
    w_out = p["na_w_out"][j].astype(BF16)
    row_ctx = lambda tok: 0
    row_lat = lambda tok: 1 + tok // (NA_ROWS * GRID_W)
    oc, kc, vc = _na_seq(xc, SEQ, layer, row_ctx, mod, nw, w_in, w_out)
    ql, kl, vl, szl = _na_in(xl, layer, row_lat, mod, nw, w_in, BF16)
    past = p["cache_k"].shape[2]
    ck = p["cache_k"][:n_lat, j].reshape(n_lat, past, D).astype(BF16)
    cv = p["cache_v"][:n_lat, j].reshape(n_lat, past, D).astype(BF16)
    gl = _na_lat(ql, kl, vl, ck, cv, _rpb_table(p["na_rpb"][j]), szl, n_lat)
    return oc, _out_proj(gl, xl, layer, row_lat, mod, w_out), kc, vc


SSD_IN_COLS = SSD_INNER + SSD_XBC


def _softplus(x):
    return jnp.maximum(x, 0.0) + jnp.log(1.0 + jnp.exp(-jnp.abs(x)))


def _ssd_in_kernel(tiles_per_seq, *refs):
    refs = list(refs)
    x_ref = refs.pop(0)
    if tiles_per_seq > 1:
        xp_ref = refs.pop(0)
        xn_ref = refs.pop(0)
    nw_ref, mod_ref, wz_ref, wx_ref, wdt_ref, dtb_ref, wc_ref, bc_ref, sz_ref, xbc_ref, dt_ref, xpad_ref = refs
    hb = _modnorm(x_ref[...], nw_ref[...], mod_ref[...]).astype(BF16)
    halo = None
    if tiles_per_seq > 1:
        t = pl.program_id(0) % tiles_per_seq
        xh = jnp.concatenate([xp_ref[...], xn_ref[...]], axis=0)
        halo = (_modnorm(xh, nw_ref[...], mod_ref[...]).astype(BF16), t > 0, t < tiles_per_seq - 1)
    dt_ref[...] = _ssd_project(hb, halo, wz_ref, wx_ref, wdt_ref, dtb_ref, wc_ref, bc_ref, xpad_ref, sz_ref, xbc_ref)


def _ssd_project(hb, halo, wz_ref, wx_ref, wdt_ref, dtb_ref, wc_ref, bc_ref, xpad_ref, sz_ref, xbc_ref):
    tm = hb.shape[0]
    HL = SSD_HALO
    RB = 128
    for j in range(SSD_XBC // D):
        c0 = j * D
        raw = _wdot(hb, wx_ref, c0, D)
        if halo is not None:
            rawh = _wdot(halo[0], wx_ref, c0, D)
            raw_prev = jnp.where(halo[1], rawh[0:HL], 0.0)
            raw_next = jnp.where(halo[2], rawh[HL:2 * HL], 0.0)
        else:
            raw_prev = raw_next = jnp.zeros((HL, D), F32)
        for cb in range(D // LANES):
            blk = j * (D // LANES) + cb
            lc = slice(blk * LANES, (blk + 1) * LANES)
            cc = slice(cb * LANES, (cb + 1) * LANES)
            xpad_ref[blk, 0:HL, :] = raw_prev[:, cc]
            xpad_ref[blk, HL:HL + tm, :] = raw[:, cc]
            xpad_ref[blk, HL + tm:HL + tm + HL, :] = raw_next[:, cc]
            for rb in range(tm // RB):
                acc = jnp.zeros((RB, LANES), F32)
                for k in range(SSD_CONV):
                    r0 = rb * RB + HL - SSD_CONV // 2 + k
                    acc = acc + xpad_ref[blk, r0:r0 + RB, :] * wc_ref[k:k + 1, lc]
                xbc_ref[rb * RB:(rb + 1) * RB, lc] = _silu(acc + bc_ref[:, lc])
    for j in range(SSD_INNER // D):
        sz_ref[:, j * D:(j + 1) * D] = _silu(_wdot(hb, wz_ref, j * D, D))
    raw = jnp.dot(hb, wdt_ref[...], preferred_element_type=F32) + dtb_ref[...]
    lane = lax.broadcasted_iota(jnp.int32, (1, LANES), 1)
    return jnp.where(lane < 2 * SSD_HEADS, _softplus(raw), 0.0)


def _ssd_in(x, seq_len, layer, row_fn, mod, nw, w_in, w_dt, dt_bias, w_conv, b_conv):
    M = x.shape[0]
    tm = TOKEN_TILE
    HL = SSD_HALO
    tps = seq_len // tm
    n_hblk = M // HL
    row = lambda i: (i, 0)
    items = [(pl.BlockSpec((tm, D), row), x)]
    if tps > 1:
        items += [(pl.BlockSpec((HL, D), lambda i: (jnp.maximum(i * (tm // HL) - 1, 0), 0)), x),
                  (pl.BlockSpec((HL, D), lambda i: (jnp.minimum((i + 1) * (tm // HL), n_hblk - 1), 0)), x)]
    items += [(_const_spec((1, D)), nw), (_mod_spec(layer, lambda i: row_fn(i * tm)), mod),
              _W(w_in, SSD_INNER, 0), _W(w_in, SSD_XBC, SSD_INNER),
              (_const_spec((D, LANES)), w_dt), (_const_spec((1, LANES)), dt_bias),
              (_const_spec((SSD_CONV + 1, SSD_XBC)), w_conv), (_const_spec((1, SSD_XBC)), b_conv)]
    in_specs, args, groups = _operands(items)
    return pl.pallas_call(
        _grouped(functools.partial(_ssd_in_kernel, tps), groups),
        grid=(M // tm,),
        in_specs=in_specs,
        out_specs=[pl.BlockSpec((tm, SSD_INNER), row), pl.BlockSpec((tm, SSD_XBC), row), pl.BlockSpec((tm, LANES), row)],
        out_shape=[jax.ShapeDtypeStruct((M, SSD_INNER), F32), jax.ShapeDtypeStruct((M, SSD_XBC), F32),
                   jax.ShapeDtypeStruct((M, LANES), F32)],
        scratch_shapes=[pltpu.VMEM((SSD_XBC // LANES, tm + 2 * HL, LANES), F32)],
        compiler_params=_cparams(1),
        name="ssd_in",
    )(*args)


def _cumsum_rows(a, reverse):
    n = a.shape[0]
    row = lax.broadcasted_iota(jnp.int32, a.shape, 0)
    k = 1
    while k < n:
        if reverse:
            a = a + jnp.where(row < n - k, pltpu.roll(a, n - k, 0), 0.0)
        else:
            a = a + jnp.where(row >= k, pltpu.roll(a, k, 0), 0.0)
        k *= 2
    return a


def _ssd_core_kernel(nc, has_h0, emit_state, *refs):
    refs = list(refs)
    dir_refs = [refs[0:2], refs[2:4]]
    alog_ref, dsk_ref = refs[4:6]
    refs = refs[6:]
    if has_h0:
        h0_ref = refs.pop(0)
    y_refs = [refs.pop(0), refs.pop(0)]
    if emit_state:
        so_ref = refs.pop(0)
    (st_ref,) = refs
    c = pl.program_id(1)
    n_blk = SSD_INNER // LANES

    @pl.when(c == 0)
    def _init():
        if has_h0:
            for dr in range(2):
                for j in range(n_blk):
                    st_ref[dr, :, j * LANES:(j + 1) * LANES] = h0_ref[dr, j * LANES:(j + 1) * LANES, :].T
        else:
            st_ref[...] = jnp.zeros(st_ref.shape, F32)

    A = -jnp.exp(alog_ref[...])
    for dr in range(2):
        x_ref, dt_ref = dir_refs[dr]
        y_ref = y_refs[dr]

        def x_at(c0, x_ref=x_ref):
            return x_ref[:, c0:c0 + LANES]

        def y_put(c0, y, y_ref=y_ref):
            y_ref[:, c0:c0 + LANES] = y

        _ssd_chunk(dr, x_at, dt_ref[...], A, st_ref, y_put, dsk_ref)

    if emit_state:
        @pl.when(c == nc - 1)
        def _emit():
            _ssd_emit_state(st_ref, so_ref)


def _ssd_emit_state(st_ref, so_ref):
    for dr in range(2):
        for j in range(SSD_INNER // LANES):
            so_ref[dr, j * LANES:(j + 1) * LANES, :] = st_ref[dr, :, j * LANES:(j + 1) * LANES].T


def _ssd_group_terms(x_at):
    N = SSD_STATE
    terms = []
    for g in range(SSD_GROUPS):
        bg_t = x_at(SSD_INNER + g * N).T
        cg = x_at(SSD_INNER + SSD_GN + g * N).astype(BF16)
        terms.append((bg_t, cg, jnp.dot(cg, bg_t.astype(BF16), preferred_element_type=F32)))
    return terms


def _ssd_chunk(dr, x_at, dt, A, st_ref, y_put, dsk_ref, group_terms=None):
    Q, N, P = SSD_CHUNK, SSD_STATE, SSD_HEAD_DIM
    lo = _head_masks()
    ri = lax.broadcasted_iota(jnp.int32, (Q, Q), 0)
    ci = lax.broadcasted_iota(jnp.int32, (Q, Q), 1)
    cum = _cumsum_rows(dt * A, reverse=(dr == 1)) * LOG2E
    cum_t = cum.T
    tot_row = Q - 1 if dr == 0 else 0
    dtw_t = dt.T * jnp.exp2(cum_t[:, tot_row:tot_row + 1] - cum_t)
    dt_t = dt.T
    mask = (ri >= ci) if dr == 0 else (ri <= ci)
    if group_terms is None:
        group_terms = _ssd_group_terms(x_at)
    for g in range(SSD_GROUPS):
        bg_t, cg, cb = group_terms[g]
        for pp in range(SSD_HEADS // SSD_GROUPS // 2):
            head = g * (SSD_HEADS // SSD_GROUPS) + 2 * pp
            c0 = head * P
            li = dr * SSD_HEADS + head
            xpair = x_at(c0)
            cum_cols = [cum[:, li:li + 1], cum[:, li + 1:li + 2]]
            cum_x = jnp.where(lo, cum_cols[0], cum_cols[1])
            lhs = []
            for hh in range(2):
                decay = jnp.exp2(jnp.where(mask, cum_cols[hh] - cum_t[li + hh:li + hh + 1, :], -jnp.inf))
                lhs.append((cb * decay * dt_t[li + hh:li + hh + 1, :]).astype(BF16))
            for hh in range(2):
                lhs.append((bg_t * dtw_t[li + hh:li + hh + 1, :]).astype(BF16))
            prod = jnp.dot(jnp.concatenate(lhs, axis=0), xpair.astype(BF16), preferred_element_type=F32)
            yd = jnp.where(lo, prod[0:Q], prod[Q:2 * Q])
            sd = jnp.where(lo, prod[2 * Q:2 * Q + N], prod[2 * Q + N:2 * Q + 2 * N])
            st = st_ref[dr, :, c0:c0 + LANES]
            y = yd + jnp.dot(cg, st.astype(BF16), preferred_element_type=F32) * jnp.exp2(cum_x)
            if dr == 0:
                y = y + xpair * dsk_ref[:, c0:c0 + LANES]
            y_put(c0, y)
            tot = cum_x[tot_row:tot_row + 1, :]
            st_ref[dr, :, c0:c0 + LANES] = st * jnp.exp2(tot) + sd


def _ssd_seq_kernel(x_ref, nw_ref, mod_ref, wz_ref, wx_ref, wdt_ref, dtb_ref, wc_ref, bc_ref, alog_ref, dsk_ref, gnw_ref, wo_ref,
                    o_ref, so_ref, xpad_ref, xbc_ref, sz_ref, y_ref, st_ref):
    Q = SSD_CHUNK
    nc = x_ref.shape[0] // Q
    x = x_ref[...]
    hb = _modnorm(x, nw_ref[...], mod_ref[...]).astype(BF16)
    dt = _ssd_project(hb, None, wz_ref, wx_ref, wdt_ref, dtb_ref, wc_ref, bc_ref, xpad_ref, sz_ref, xbc_ref)
    st_ref[...] = jnp.zeros(st_ref.shape, F32)
    A = -jnp.exp(alog_ref[...])

    def x_at_chunk(ck):
        return lambda c0: xbc_ref[ck * Q:(ck + 1) * Q, c0:c0 + LANES]

    terms = [_ssd_group_terms(x_at_chunk(ck)) for ck in range(nc)]
    for dr in range(2):
        for ck in (range(nc) if dr == 0 else reversed(range(nc))):
            rows = slice(ck * Q, (ck + 1) * Q)

            def y_put(c0, y, rows=rows, dr=dr):
                if dr == 0:
                    y_ref[rows, c0:c0 + LANES] = y
                else:
                    y_ref[rows, c0:c0 + LANES] += y

            _ssd_chunk(dr, x_at_chunk(ck), dt[rows], A, st_ref, y_put, dsk_ref, terms[ck])
    _ssd_emit_state(st_ref, so_ref)
    g = _rms(y_ref[...] * sz_ref[...], gnw_ref[...]).astype(BF16)
    o_ref[...] = x + mod_ref[:, 2 * D:3 * D] * _wdot(g, wo_ref, 0, D)


def _ssd_seq(x, seq_len, layer, row_fn, mod, nw, w_in, w_dt, dt_bias, w_conv, b_conv, a_log, d_skip, norm_w, w_out):
    M = x.shape[0]
    n_seq = M // seq_len
    HL = SSD_HALO
    row = lambda i: (i, 0)
    st_blk = pl.BlockSpec((None, 2, SSD_INNER, SSD_STATE), lambda i: (i, 0, 0, 0))
    in_specs, args, groups = _operands([
        (pl.BlockSpec((seq_len, D), row), x), (_const_spec((1, D)), nw),
        (_mod_spec(layer, lambda i: row_fn(i * seq_len)), mod),
        _W(w_in, SSD_INNER, 0), _W(w_in, SSD_XBC, SSD_INNER),
        (_const_spec((D, LANES)), w_dt), (_const_spec((1, LANES)), dt_bias),
        (_const_spec((SSD_CONV + 1, SSD_XBC)), w_conv), (_const_spec((1, SSD_XBC)), b_conv),
        (_const_spec((1, LANES)), a_log), (_const_spec((1, SSD_INNER)), d_skip),
        (_const_spec((1, SSD_INNER)), norm_w), _W(w_out)])
    return pl.pallas_call(
        _grouped(_ssd_seq_kernel, groups),
        grid=(n_seq,),
        in_specs=in_specs,
        out_specs=[pl.BlockSpec((seq_len, D), row), st_blk],
        out_shape=[jax.ShapeDtypeStruct((M, D), F32), jax.ShapeDtypeStruct((n_seq, 2, SSD_INNER, SSD_STATE), F32)],
        scratch_shapes=[pltpu.VMEM((SSD_XBC // LANES, seq_len + 2 * HL, LANES), F32),
                        pltpu.VMEM((seq_len, SSD_XBC), F32), pltpu.VMEM((seq_len, SSD_INNER), F32),
                        pltpu.VMEM((seq_len, SSD_INNER), F32), pltpu.VMEM((2, SSD_STATE, SSD_INNER), F32)],
        compiler_params=_cparams(1),
        name="ssd_seq",
    )(*args)


def _ssd_core(xbc, dtp, a_log, d_skip, h0, n_seq, seq_len, emit_state):
    Q = SSD_CHUNK
    nc = seq_len // Q
    M = n_seq * seq_len

    def chunk(dr):
        return (lambda b, c: b * nc + c) if dr == 0 else (lambda b, c: b * nc + nc - 1 - c)

    in_specs, args = [], []
    for dr in range(2):
        ch = chunk(dr)
        in_specs += [pl.BlockSpec((Q, SSD_XBC), lambda b, c, ch=ch: (ch(b, c), 0)),
                     pl.BlockSpec((Q, LANES), lambda b, c, ch=ch: (ch(b, c), 0))]
        args += [xbc, dtp]
    in_specs += [_const_spec((1, LANES)), _const_spec((1, SSD_INNER))]
    args += [a_log, d_skip]
    st_blk = pl.BlockSpec((None, 2, SSD_INNER, SSD_STATE), lambda b, c: (b, 0, 0, 0))
    has_h0 = h0 is not None
    if has_h0:
        in_specs.append(st_blk)
        args.append(h0)
    out_specs = [pl.BlockSpec((Q, SSD_INNER), lambda b, c: (chunk(0)(b, c), 0)),
                 pl.BlockSpec((Q, SSD_INNER), lambda b, c: (chunk(1)(b, c), 0))]
    out_shape = [jax.ShapeDtypeStruct((M, SSD_INNER), F32)] * 2
    if emit_state:
        out_specs.append(st_blk)
        out_shape.append(jax.ShapeDtypeStruct((n_seq, 2, SSD_INNER, SSD_STATE), F32))
    return pl.pallas_call(
        functools.partial(_ssd_core_kernel, nc, has_h0, emit_state),
        grid=(n_seq, nc),
        in_specs=in_specs,
        out_specs=out_specs,
        out_shape=out_shape,
        scratch_shapes=[pltpu.VMEM((2, SSD_STATE, SSD_INNER), F32)],
        compiler_params=_cparams(2),
        name="ssd_core",
    )(*args)


def _ssd_out_kernel(yf_ref, yb_ref, sz_ref, x_ref, mod_ref, nw_ref, w_ref, o_ref):
    g = _rms((yf_ref[...] + yb_ref[...]) * sz_ref[...], nw_ref[...]).astype(BF16)
    o_ref[...] = x_ref[...] + mod_ref[:, 2 * D:3 * D] * _wdot(g, w_ref, 0, D)


def _ssd_out(yf, yb, sz, x, layer, row_fn, mod, norm_w, w_out):
    M = x.shape[0]
    tm = TOKEN_TILE
    row = lambda i: (i, 0)
    wide = pl.BlockSpec((tm, SSD_INNER), row)
    in_specs, args, groups = _operands([
        (wide, yf), (wide, yb), (wide, sz), (pl.BlockSpec((tm, D), row), x),
        (_mod_spec(layer, lambda i: row_fn(i * tm)), mod), (_const_spec((1, SSD_INNER)), norm_w), _W(w_out)])
    return pl.pallas_call(
        _grouped(_ssd_out_kernel, groups),
        grid=(M // tm,),
        in_specs=in_specs,
        out_specs=pl.BlockSpec((tm, D), row),
        out_shape=jax.ShapeDtypeStruct((M, D), F32),
        compiler_params=_cparams(1),
        name="ssd_out",
    )(*args)


def _ssd_layer(xc, xl, layer, mod, nw, p, n_lat):
    j = layer // 3
    w_in = p["ssd_w_in"][j].astype(BF16)
    w_dt = jnp.pad(w_in[:, SSD_IN_COLS:], ((0, 0), (0, LANES - 2 * SSD_HEADS)))
    dt_bias = jnp.pad(p["ssd_dt_bias"][j].reshape(1, -1), ((0, 0), (0, LANES - 2 * SSD_HEADS)))
    a_log = jnp.pad(p["ssd_a_log"][j].reshape(1, -1), ((0, 0), (0, LANES - 2 * SSD_HEADS)))
    w_conv = jnp.pad(p["ssd_w_conv"][j], ((0, 1), (0, 0)))
    b_conv = p["ssd_b_conv"][j][None]
    d_skip = jnp.repeat(p["ssd_d"][j], SSD_HEAD_DIM)[None]
    norm_w = p["ssd_norm_w"][j][None]
    w_out = p["ssd_w_out"][j].astype(BF16)
    row_ctx = lambda tok: 0
    row_lat = lambda tok: 1 + tok // DEC_SEQ
    oc, s_new = _ssd_seq(xc, SEQ, layer, row_ctx, mod, nw, w_in, w_dt, dt_bias, w_conv, b_conv, a_log, d_skip, norm_w, w_out)
    szl, xbc_l, dt_l = _ssd_in(xl, DEC_SEQ, layer, row_lat, mod, nw, w_in, w_dt, dt_bias, w_conv, b_conv)
    h0 = p["state_ssm"][:n_lat, j].reshape(n_lat, 2, SSD_INNER, SSD_STATE)
    yfl, ybl = _ssd_core(xbc_l, dt_l, a_log, d_skip, h0, n_lat, DEC_SEQ, False)
    ol = _ssd_out(yfl, ybl, szl, xl, layer, row_lat, mod, norm_w, w_out)
    return oc, ol, s_new


def kernel(x_prompt, x_sample, cache_k, cache_v, state_ssm, c, c_ctx, ada_w, ada_b, norm_w, final_norm_w, conv_w_in, conv_b_in, conv_w_dw, conv_b_dw, conv_ln_w, conv_ln_b, conv_w_out, conv_b_out, na_w_in, na_rpb, na_w_out, ssd_w_in, ssd_w_conv, ssd_b_conv, ssd_a_log, ssd_dt_bias, ssd_d, ssd_norm_w, ssd_w_out):
    p = dict(cache_k=cache_k, cache_v=cache_v, state_ssm=state_ssm,
             conv_w_in=conv_w_in, conv_b_in=conv_b_in, conv_w_dw=conv_w_dw, conv_b_dw=conv_b_dw,
             conv_ln_w=conv_ln_w, conv_ln_b=conv_ln_b, conv_w_out=conv_w_out, conv_b_out=conv_b_out,
             na_w_in=na_w_in, na_rpb=na_rpb, na_w_out=na_w_out,
             ssd_w_in=ssd_w_in, ssd_w_conv=ssd_w_conv, ssd_b_conv=ssd_b_conv, ssd_a_log=ssd_a_log,
             ssd_dt_bias=ssd_dt_bias, ssd_d=ssd_d, ssd_norm_w=ssd_norm_w, ssd_w_out=ssd_w_out)
    n_ctx, n_lat = x_prompt.shape[0], x_sample.shape[0]
    assert x_prompt.shape[1:] == (SEQ, D) and x_sample.shape[1:] == (DEC_SEQ, D)
    assert (DEPTH - 1) % 3 == 0, "the final RMSNorm is fused into a trailing Conformer layer"
    xc = x_prompt.reshape(n_ctx * SEQ, D)
    xl = x_sample.reshape(n_lat * DEC_SEQ, D)
    cond8 = jnp.concatenate([c_ctx[None], c, jnp.zeros((8 - 1 - n_lat, D), F32)], axis=0)
    mod = _modulation(cond8, ada_w, ada_b).reshape(DEPTH, 8, 1, 3 * D)
    row_ctx = lambda tok: 0
    row_lat = lambda tok: 1 + tok // DEC_SEQ
    new_k, new_v, new_s = [], [], []
    for i in range(DEPTH):
        kind, j = i % 3, i // 3
        nw = norm_w[i][None]
        if kind == 0:
            cp = _prep_conv(p, j)
            fw = final_norm_w[None] if i == DEPTH - 1 else None
            xc = _conv_layer(xc, SEQ, i, row_ctx, mod, nw, *cp, fw)
            xl = _conv_layer(xl, DEC_SEQ, i, row_lat, mod, nw, *cp, fw)
        elif kind == 1:
            xc, xl, k_c, v_c = _na_layer(xc, xl, i, mod, nw, p, n_lat)
            new_k.append(k_c.reshape(n_ctx, SEQ, NA_HEADS, NA_HEAD_DIM))
            new_v.append(v_c.reshape(n_ctx, SEQ, NA_HEADS, NA_HEAD_DIM))
        else:
            xc, xl, s_c = _ssd_layer(xc, xl, i, mod, nw, p, n_lat)
            new_s.append(s_c.reshape(n_ctx, 2, SSD_HEADS, SSD_HEAD_DIM, SSD_STATE))
    return (xc.reshape(n_ctx, SEQ, D), xl.reshape(n_lat, DEC_SEQ, D),
            jnp.stack(new_k, axis=1), jnp.stack(new_v, axis=1), jnp.stack(new_s, axis=1))
```

```python
import functools

import jax
import jax.numpy as jnp
from jax import lax
from jax.experimental import pallas as pl
from jax.experimental.pallas import tpu as pltpu

F32 = jnp.float32
BF16 = jnp.bfloat16

D = 1024
DEPTH = 4
SEQ = 256
DEC_SEQ = 2048
GRID_W = 64
RMS_EPS = 1e-6
LN_EPS = 1e-5
LOG2E = 1.4426950408889634
CONF_K = 31
NA_HEADS = 16
NA_HEAD_DIM = 64
NA_KH = 8
NA_KW = 16
NEG_INF = -1e30
SSD_INNER = 2048
SSD_HEADS = 32
SSD_HEAD_DIM = 64
SSD_STATE = 128
SSD_GROUPS = 4
SSD_GN = SSD_GROUPS * SSD_STATE
SSD_CONV = 7
SSD_CHUNK = 128
SSD_XBC = SSD_INNER + 2 * SSD_GN

LANES = 128
TOKEN_TILE = 256
W_CHUNK = 512
CONV_HALO = 16
SSD_HALO = 8
VMEM_LIMIT = 56 * 1024 * 1024


def _cparams(n_axes):
    return pltpu.CompilerParams(dimension_semantics=("arbitrary",) * n_axes,
                                vmem_limit_bytes=VMEM_LIMIT)


def _const_spec(shape):
    nd = len(shape)
    return pl.BlockSpec(shape, lambda *_: (0,) * nd, pipeline_mode=pl.Buffered(1))


class _W:
    def __init__(self, w, ncols=None, col0=0):
        self.w, self.col0 = w, col0
        self.ncols = w.shape[1] if ncols is None else ncols


def _operands(items):
    specs, args, groups = [], [], []
    for it in items:
        if isinstance(it, _W):
            n, b0, K = it.ncols // W_CHUNK, it.col0 // W_CHUNK, it.w.shape[0]
            for c in range(b0, b0 + n):
                specs.append(pl.BlockSpec((K, W_CHUNK), lambda *_, c=c: (0, c), pipeline_mode=pl.Buffered(1)))
                args.append(it.w)
            groups.append(n)
        else:
            specs.append(it[0])
            args.append(it[1])
            groups.append(None)
    return specs, args, groups


def _grouped(body, groups):
    def kernel_fn(*refs):
        refs = list(refs)
        packed = []
        for g in groups:
            if g is None:
                packed.append(refs.pop(0))
            else:
                packed.append(tuple(refs[:g]))
                del refs[:g]
        return body(*packed, *refs)
    return kernel_fn


def _wdot(h, w_refs, c0, n):
    outs = []
    c = c0
    while c < c0 + n:
        j, off = divmod(c, W_CHUNK)
        m = min(W_CHUNK - off, c0 + n - c)
        outs.append(jnp.dot(h, w_refs[j][:, off:off + m], preferred_element_type=F32))
        c += m
    return outs[0] if len(outs) == 1 else jnp.concatenate(outs, axis=1)


def _sigmoid(x):
    return 1.0 / (1.0 + jnp.exp(-x))


def _silu(x):
    return x * _sigmoid(x)


def _rms(x, w):
    ms = jnp.mean(x * x, axis=-1, keepdims=True)
    return x * lax.rsqrt(ms + RMS_EPS) * w


def _modnorm(x, nw, mod):
    return _rms(x, nw) * (1.0 + mod[:, D:2 * D]) + mod[:, :D]


def _mod_spec(layer, row_fn):
    return pl.BlockSpec((None, None, 1, 3 * D), lambda i, *_: (layer, row_fn(i), 0, 0))


def _mod_kernel(cond_ref, w_ref, b_ref, o_ref):
    s = _silu(cond_ref[...]).astype(BF16)
    o_ref[...] = jnp.dot(s, w_ref[...].astype(BF16), preferred_element_type=F32) + b_ref[...]


def _modulation(cond8, ada_w, ada_b):
    tn = 1024
    return pl.pallas_call(
        _mod_kernel,
        grid=(DEPTH, 3 * D // tn),
        in_specs=[pl.BlockSpec((8, D), lambda i, j: (0, 0)),
                  pl.BlockSpec((None, D, tn), lambda i, j: (i, 0, j)),
                  pl.BlockSpec((None, 1, tn), lambda i, j: (i, 0, j))],
        out_specs=pl.BlockSpec((None, 8, tn), lambda i, j: (i, 0, j)),
        out_shape=jax.ShapeDtypeStruct((DEPTH, 8, 3 * D), F32),
        compiler_params=_cparams(2),
        name="adaln_mod",
    )(cond8, ada_w, ada_b.reshape(DEPTH, 1, 3 * D))


CONV_CHUNK = 256


def _conv_kernel(tiles_per_seq, final, *refs):
    refs = list(refs)
    x_ref = refs.pop(0)
    if tiles_per_seq > 1:
        xp_ref = refs.pop(0)
        xn_ref = refs.pop(0)
    nw_ref, mod_ref, wi_ref, bi_ref, wdw_ref, bdw_ref, lnw_ref, lnb_ref, wo_ref, bo_ref = refs[:10]
    refs = refs[10:]
    if final:
        fw_ref = refs.pop(0)
    o_ref, ubuf, cbuf = refs
    T = x_ref.shape[0]
    H = CONV_HALO
    RB = 128

    hb = _modnorm(x_ref[...], nw_ref[...], mod_ref[...]).astype(BF16)
    if tiles_per_seq > 1:
        t = pl.program_id(0) % tiles_per_seq
        hh = _modnorm(jnp.concatenate([xp_ref[...], xn_ref[...]], axis=0), nw_ref[...], mod_ref[...]).astype(BF16)

    def glu(h, cols):
        v = _wdot(h, wi_ref, cols.start, CONV_CHUNK) + bi_ref[:, cols]
        gcols = slice(D + cols.start, D + cols.stop)
        g = _wdot(h, wi_ref, gcols.start, CONV_CHUNK) + bi_ref[:, gcols]
        return v * _sigmoid(g)

    for ch in range(D // CONV_CHUNK):
        cols = slice(ch * CONV_CHUNK, (ch + 1) * CONV_CHUNK)
        u_main = glu(hb, cols)
        if tiles_per_seq > 1:
            uh = glu(hh, cols)
            u_prev = jnp.where(t > 0, uh[0:H], 0.0)
            u_next = jnp.where(t < tiles_per_seq - 1, uh[H:2 * H], 0.0)
        else:
            u_prev = u_next = jnp.zeros((H, CONV_CHUNK), F32)
        for cb in range(CONV_CHUNK // LANES):
            blk = ch * (CONV_CHUNK // LANES) + cb
            lc = slice(blk * LANES, (blk + 1) * LANES)
            cc = slice(cb * LANES, (cb + 1) * LANES)
            ubuf[blk, 0:H, :] = u_prev[:, cc]
            ubuf[blk, H:H + T, :] = u_main[:, cc]
            ubuf[blk, H + T:H + T + H, :] = u_next[:, cc]
            for rb in range(T // RB):
                acc = jnp.zeros((RB, LANES), F32)
                for k in range(CONF_K):
                    r0 = rb * RB + H - CONF_K // 2 + k
                    acc = acc + ubuf[blk, r0:r0 + RB, :] * wdw_ref[k:k + 1, lc]
                cbuf[rb * RB:(rb + 1) * RB, lc] = acc + bdw_ref[:, lc]

    sz = _silu(_wdot(hb, wi_ref, 2 * D, D) + bi_ref[:, 2 * D:3 * D])
    c = cbuf[...]
    mu = jnp.mean(c, axis=-1, keepdims=True)
    xc = c - mu
    var = jnp.mean(xc * xc, axis=-1, keepdims=True)
    ln = xc * lax.rsqrt(var + LN_EPS) * lnw_ref[...] + lnb_ref[...]
    g = (_silu(ln) * sz).astype(BF16)
    y = _wdot(g, wo_ref, 0, D) + bo_ref[...]
    xn = x_ref[...] + mod_ref[:, 2 * D:3 * D] * y
    if final:
        xn = _rms(xn, fw_ref[...])
    o_ref[...] = xn


def _conv_layer(x, seq_len, layer, row_fn, mod, nw, w_in, b_in, w_dw, b_dw, ln_w, ln_b, w_out, b_out, final_w):
    M = x.shape[0]
    T = TOKEN_TILE
    tps = seq_len // T
    H = CONV_HALO
    n_halo_blocks = M // H
    row = lambda i: (i, 0)
    items = [(pl.BlockSpec((T, D), row), x)]
    if tps > 1:
        items += [(pl.BlockSpec((H, D), lambda i: (jnp.maximum(i * (T // H) - 1, 0), 0)), x),
                  (pl.BlockSpec((H, D), lambda i: (jnp.minimum((i + 1) * (T // H), n_halo_blocks - 1), 0)), x)]
    items += [(_const_spec((1, D)), nw), (_mod_spec(layer, lambda i: row_fn(i * T)), mod),
              _W(w_in), (_const_spec((1, 3 * D)), b_in),
              (_const_spec((CONF_K + 1, D)), w_dw), (_const_spec((1, D)), b_dw),
              (_const_spec((1, D)), ln_w), (_const_spec((1, D)), ln_b),
              _W(w_out), (_const_spec((1, D)), b_out)]
    final = final_w is not None
    if final:
        items.append((_const_spec((1, D)), final_w))
    in_specs, args, groups = _operands(items)
    return pl.pallas_call(
        _grouped(functools.partial(_conv_kernel, tps, final), groups),
        grid=(M // T,),
        in_specs=in_specs,
        out_specs=pl.BlockSpec((T, D), row),
        out_shape=jax.ShapeDtypeStruct((M, D), F32),
        scratch_shapes=[pltpu.VMEM((D // LANES, T + 2 * H, LANES), F32), pltpu.VMEM((T, D), F32)],
        compiler_params=_cparams(1),
        name="conv_layer",
    )(*args)


def _prep_conv(p, j):
    w_dw = jnp.pad(p["conv_w_dw"][j], ((0, 1), (0, 0)))
    return (p["conv_w_in"][j].astype(BF16), p["conv_b_in"][j][None], w_dw, p["conv_b_dw"][j][None],
            p["conv_ln_w"][j][None], p["conv_ln_b"][j][None], p["conv_w_out"][j].astype(BF16), p["conv_b_out"][j][None])


def _out_proj_kernel(g_ref, x_ref, mod_ref, w_ref, o_ref):
    y = _wdot(g_ref[...].astype(BF16), w_ref, 0, D)
    o_ref[...] = x_ref[...] + mod_ref[:, 2 * D:3 * D] * y


def _out_proj(g, x, layer, row_fn, mod, w_out):
    M, K = g.shape
    tm = TOKEN_TILE
    row = lambda i: (i, 0)
    in_specs, args, groups = _operands([
        (pl.BlockSpec((tm, K), row), g), (pl.BlockSpec((tm, D), row), x),
        (_mod_spec(layer, lambda i: row_fn(i * tm)), mod), _W(w_out)])
    return pl.pallas_call(
        _grouped(_out_proj_kernel, groups),
        grid=(M // tm,),
        in_specs=in_specs,
        out_specs=pl.BlockSpec((tm, D), row),
        out_shape=jax.ShapeDtypeStruct((M, D), F32),
        compiler_params=_cparams(1),
        name="out_proj",
    )(*args)


NA_SCALE = NA_HEAD_DIM ** -0.5
NA_ROWS = 32
NA_WIN = NA_KH * GRID_W
NT_DIMS = (((1,), (1,)), ((), ()))


def _na_in_kernel(x_ref, nw_ref, mod_ref, w_ref, q_ref, k_ref, v_ref, sz_ref):
    hb = _modnorm(x_ref[...], nw_ref[...], mod_ref[...]).astype(BF16)
    q_ref[...] = _wdot(hb, w_ref, 0, D).astype(q_ref.dtype)
    k_ref[...] = _wdot(hb, w_ref, D, D).astype(k_ref.dtype)
    v_ref[...] = _wdot(hb, w_ref, 2 * D, D).astype(v_ref.dtype)
    sz_ref[...] = _silu(_wdot(hb, w_ref, 3 * D, D))


def _na_in(x, layer, row_fn, mod, nw, w_in, kv_dtype):
    M = x.shape[0]
    tm = TOKEN_TILE
    row = lambda i: (i, 0)
    blk = pl.BlockSpec((tm, D), row)
    in_specs, args, groups = _operands([
        (blk, x), (_const_spec((1, D)), nw), (_mod_spec(layer, lambda i: row_fn(i * tm)), mod), _W(w_in)])
    return pl.pallas_call(
        _grouped(_na_in_kernel, groups),
        grid=(M // tm,),
        in_specs=in_specs,
        out_specs=[blk, blk, blk, blk],
        out_shape=[jax.ShapeDtypeStruct((M, D), BF16), jax.ShapeDtypeStruct((M, D), kv_dtype),
                   jax.ShapeDtypeStruct((M, D), kv_dtype), jax.ShapeDtypeStruct((M, D), F32)],
        compiler_params=_cparams(1),
        name="na_in",
    )(*args)


def _head_masks():
    lane = lax.broadcasted_iota(jnp.int32, (1, LANES), 1)
    return lane < NA_HEAD_DIM


def _na_seq_kernel(x_ref, nw_ref, mod_ref, wi_ref, wo_ref, o_ref, k_ref, v_ref, g_ref):
    n = x_ref.shape[0]
    lo = _head_masks()
    x = x_ref[...]
    hb = _modnorm(x, nw_ref[...], mod_ref[...]).astype(BF16)
    hpc = CONV_CHUNK // NA_HEAD_DIM
    for ch in range(D // CONV_CHUNK):
        cols = slice(ch * CONV_CHUNK, (ch + 1) * CONV_CHUNK)
        q = _wdot(hb, wi_ref, cols.start, CONV_CHUNK).astype(BF16)
        k = _wdot(hb, wi_ref, D + cols.start, CONV_CHUNK)
        v = _wdot(hb, wi_ref, 2 * D + cols.start, CONV_CHUNK)
        sz = _silu(_wdot(hb, wi_ref, 3 * D + cols.start, CONV_CHUNK))
        k_ref[:, ch * hpc:(ch + 1) * hpc, :] = k.reshape(n, hpc, NA_HEAD_DIM)
        v_ref[:, ch * hpc:(ch + 1) * hpc, :] = v.reshape(n, hpc, NA_HEAD_DIM)
        kb = k.astype(BF16)
        vb = v.astype(BF16)
        for pp in range(CONV_CHUNK // LANES):
            pc = slice(pp * LANES, (pp + 1) * LANES)
            qp = q[:, pc]
            zero = jnp.zeros_like(qp)
            qs = jnp.concatenate([jnp.where(lo, qp, zero), jnp.where(lo, zero, qp)], axis=0)
            s = lax.dot_general(qs, kb[:, pc], NT_DIMS, preferred_element_type=F32) * NA_SCALE
            e = jnp.exp(s - jnp.max(s, axis=-1, keepdims=True))
            l = jnp.sum(e, axis=-1, keepdims=True)
            o = jnp.dot(e.astype(BF16), vb[:, pc], preferred_element_type=F32) / l
            g = jnp.where(lo, o[:n], o[n:]) * sz[:, pc]
            g_ref[:, cols.start + pp * LANES:cols.start + (pp + 1) * LANES] = g.astype(BF16)
    y = _wdot(g_ref[...], wo_ref, 0, D)
    o_ref[...] = x + mod_ref[:, 2 * D:3 * D] * y


def _na_seq(x, seq_len, layer, row_fn, mod, nw, w_in, w_out):
    M = x.shape[0]
    row = lambda i: (i, 0)
    kv_blk = pl.BlockSpec((seq_len, NA_HEADS, NA_HEAD_DIM), lambda i: (i, 0, 0))
    kv_shape = jax.ShapeDtypeStruct((M, NA_HEADS, NA_HEAD_DIM), F32)
    in_specs, args, groups = _operands([
        (pl.BlockSpec((seq_len, D), row), x), (_const_spec((1, D)), nw),
        (_mod_spec(layer, lambda i: row_fn(i * seq_len)), mod), _W(w_in), _W(w_out)])
    return pl.pallas_call(
        _grouped(_na_seq_kernel, groups),
        grid=(M // seq_len,),
        in_specs=in_specs,
        out_specs=[pl.BlockSpec((seq_len, D), row), kv_blk, kv_blk],
        out_shape=[jax.ShapeDtypeStruct((M, D), F32), kv_shape, kv_shape],
        scratch_shapes=[pltpu.VMEM((seq_len, D), BF16)],
        compiler_params=_cparams(1),
        name="na_seq",
    )(*args)


def _rpb_kernel(rpb_ref, o_ref):
    h = pl.program_id(0)
    c = lax.broadcasted_iota(jnp.int32, (GRID_W, LANES), 0)
    lane = lax.broadcasted_iota(jnp.int32, (GRID_W, LANES), 1)
    kc = lane & (GRID_W - 1)
    upper = lax.broadcasted_iota(jnp.int32, (1, LANES), 1) >= GRID_W
    cs = jnp.clip(c - NA_KW // 2, 0, GRID_W - NA_KW)
    valid = (kc >= cs) & (kc < cs + NA_KW)
    n_pair = 2 * NA_KH - 2

    def toeplitz(dri, lane0):
        row = rpb_ref[pl.ds(h * (2 * NA_KH - 1) + dri, 1), :]
        rows = jnp.broadcast_to(row, (GRID_W, LANES))
        return pltpu.roll(rows, (LANES - (NA_KW - 1) + lane0) % LANES, 1, stride=1, stride_axis=0)

    for dri in range(n_pair):
        tile = jnp.where(upper, toeplitz(dri + 1, GRID_W), toeplitz(dri, 0))
        tile = jnp.where(valid, tile, NEG_INF)
        for idx in range(NA_KH):
            jj2 = dri - (NA_KH - 1) + idx
            if jj2 % 2 == 0 and 0 <= jj2 // 2 < NA_KH // 2:
                jj = jj2 // 2
                o_ref[idx, :, jj * LANES:(jj + 1) * LANES] = tile


def _rpb_table(rpb):
    n_rows = NA_HEADS * (2 * NA_KH - 1)
    rows = jnp.pad(rpb.reshape(n_rows, 2 * NA_KW - 1), ((0, 0), (0, LANES - (2 * NA_KW - 1))))
    return pl.pallas_call(
        _rpb_kernel,
        grid=(NA_HEADS,),
        in_specs=[_const_spec((n_rows, LANES))],
        out_specs=pl.BlockSpec((NA_KH, None, GRID_W, NA_WIN), lambda h: (0, h, 0, 0)),
        out_shape=jax.ShapeDtypeStruct((NA_KH, NA_HEADS, GRID_W, NA_WIN), F32),
        compiler_params=_cparams(1),
        name="na_rpb_table",
    )(rows)


def _na_row_start(r):
    return jnp.clip(r - NA_KH // 2, 0, NA_ROWS - NA_KH)


def _na_lat_kernel(q_ref, k_ref, v_ref, ck_ref, cv_ref, bias_ref, sz_ref, o_ref):
    lo = _head_masks()
    k0 = pl.multiple_of(_na_row_start(pl.program_id(1)) * GRID_W, GRID_W)
    for p in range(NA_HEADS // 2):
        cols = slice(p * LANES, (p + 1) * LANES)
        qp = q_ref[:, cols]
        kw = k_ref[pl.ds(k0, NA_WIN), cols]
        vw = v_ref[pl.ds(k0, NA_WIN), cols]
        ck = ck_ref[:, cols]
        cv = cv_ref[:, cols]
        zero = jnp.zeros_like(qp)
        qs = jnp.concatenate([jnp.where(lo, qp, zero), jnp.where(lo, zero, qp)], axis=0)
        bias = bias_ref[2 * p:2 * p + 2].reshape(2 * GRID_W, NA_WIN)
        s_loc = lax.dot_general(qs, kw, NT_DIMS, preferred_element_type=F32) * NA_SCALE + bias
        s_ctx = lax.dot_general(qs, ck, NT_DIMS, preferred_element_type=F32) * NA_SCALE
        m = jnp.maximum(jnp.max(s_loc, axis=-1, keepdims=True), jnp.max(s_ctx, axis=-1, keepdims=True))
        e_loc = jnp.exp(s_loc - m)
        e_ctx = jnp.exp(s_ctx - m)
        l = jnp.sum(e_loc, axis=-1, keepdims=True) + jnp.sum(e_ctx, axis=-1, keepdims=True)
        o = (jnp.dot(e_loc.astype(BF16), vw, preferred_element_type=F32)
             + jnp.dot(e_ctx.astype(BF16), cv, preferred_element_type=F32)) / l
        o_ref[:, cols] = jnp.where(lo, o[:GRID_W], o[GRID_W:]) * sz_ref[:, cols]


def _na_lat(q, k, v, ck, cv, bias, sz, n_lat):
    T = NA_ROWS * GRID_W
    past = ck.shape[1]
    qblk = pl.BlockSpec((GRID_W, D), lambda b, r: (b * NA_ROWS + r, 0))
    kvblk = pl.BlockSpec((T, D), lambda b, r: (b, 0))
    cblk = pl.BlockSpec((None, past, D), lambda b, r: (b, 0, 0))
    bblk = pl.BlockSpec((None, NA_HEADS, GRID_W, NA_WIN), lambda b, r: (r - _na_row_start(r), 0, 0, 0))
    return pl.pallas_call(
        _na_lat_kernel,
        grid=(n_lat, NA_ROWS),
        in_specs=[qblk, kvblk, kvblk, cblk, cblk, bblk, qblk],
        out_specs=qblk,
        out_shape=jax.ShapeDtypeStruct((n_lat * T, D), F32),
        compiler_params=_cparams(2),
        name="na_lat_attn",
    )(q, k, v, ck, cv, bias, sz)


def _na_layer(xc, xl, layer, mod, nw, p, n_lat):
    j = layer // 3
    w_in = p["na_w_in"][j].astype(BF16)
    w_out = p["na_w_out"][j].astype(BF16)
    row_ctx = lambda tok: 0
    row_lat = lambda tok: 1 + tok // (NA_ROWS * GRID_W)
    oc, kc, vc = _na_seq(xc, SEQ, layer, row_ctx, mod, nw, w_in, w_out)
    ql, kl, vl, szl = _na_in(xl, layer, row_lat, mod, nw, w_in, BF16)
    past = p["cache_k"].shape[2]
    ck = p["cache_k"][:n_lat, j].reshape(n_lat, past, D).astype(BF16)
    cv = p["cache_v"][:n_lat, j].reshape(n_lat, past, D).astype(BF16)
    gl = _na_lat(ql, kl, vl, ck, cv, _rpb_table(p["na_rpb"][j]), szl, n_lat)
    return oc, _out_proj(gl, xl, layer, row_lat, mod, w_out), kc, vc


SSD_IN_COLS = SSD_INNER + SSD_XBC


def _softplus(x):
    return jnp.maximum(x, 0.0) + jnp.log(1.0 + jnp.exp(-jnp.abs(x)))


def _ssd_in_kernel(tiles_per_seq, *refs):
    refs = list(refs)
    x_ref = refs.pop(0)
    if tiles_per_seq > 1:
        xp_ref = refs.pop(0)
        xn_ref = refs.pop(0)
    nw_ref, mod_ref, wz_ref, wx_ref, wdt_ref, dtb_ref, wc_ref, bc_ref, sz_ref, xbc_ref, dt_ref, xpad_ref = refs
    hb = _modnorm(x_ref[...], nw_ref[...], mod_ref[...]).astype(BF16)
    halo = None
    if tiles_per_seq > 1:
        t = pl.program_id(0) % tiles_per_seq
        xh = jnp.concatenate([xp_ref[...], xn_ref[...]], axis=0)
        halo = (_modnorm(xh, nw_ref[...], mod_ref[...]).astype(BF16), t > 0, t < tiles_per_seq - 1)
    dt_ref[...] = _ssd_project(hb, halo, wz_ref, wx_ref, wdt_ref, dtb_ref, wc_ref, bc_ref, xpad_ref, sz_ref, xbc_ref)


def _ssd_project(hb, halo, wz_ref, wx_ref, wdt_ref, dtb_ref, wc_ref, bc_ref, xpad_ref, sz_ref, xbc_ref):
    tm = hb.shape[0]
    HL = SSD_HALO
    RB = 128
    for j in range(SSD_XBC // D):
        c0 = j * D
        raw = _wdot(hb, wx_ref, c0, D)
        if halo is not None:
            rawh = _wdot(halo[0], wx_ref, c0, D)
            raw_prev = jnp.where(halo[1], rawh[0:HL], 0.0)
            raw_next = jnp.where(halo[2], rawh[HL:2 * HL], 0.0)
        else:
            raw_prev = raw_next = jnp.zeros((HL, D), F32)
        for cb in range(D // LANES):
            blk = j * (D // LANES) + cb
            lc = slice(blk * LANES, (blk + 1) * LANES)
            cc = slice(cb * LANES, (cb + 1) * LANES)
            xpad_ref[blk, 0:HL, :] = raw_prev[:, cc]
            xpad_ref[blk, HL:HL + tm, :] = raw[:, cc]
            xpad_ref[blk, HL + tm:HL + tm + HL, :] = raw_next[:, cc]
            for rb in range(tm // RB):
                acc = jnp.zeros((RB, LANES), F32)
                for k in range(SSD_CONV):
                    r0 = rb * RB + HL - SSD_CONV // 2 + k
                    acc = acc + xpad_ref[blk, r0:r0 + RB, :] * wc_ref[k:k + 1, lc]
                xbc_ref[rb * RB:(rb + 1) * RB, lc] = _silu(acc + bc_ref[:, lc])
    for j in range(SSD_INNER // D):
        sz_ref[:, j * D:(j + 1) * D] = _silu(_wdot(hb, wz_ref, j * D, D))
    raw = jnp.dot(hb, wdt_ref[...], preferred_element_type=F32) + dtb_ref[...]
    lane = lax.broadcasted_iota(jnp.int32, (1, LANES), 1)
    return jnp.where(lane < 2 * SSD_HEADS, _softplus(raw), 0.0)


def _ssd_in(x, seq_len, layer, row_fn, mod, nw, w_in, w_dt, dt_bias, w_conv, b_conv):
    M = x.shape[0]
    tm = TOKEN_TILE
    HL = SSD_HALO
    tps = seq_len // tm
    n_hblk = M // HL
    row = lambda i: (i, 0)
    items = [(pl.BlockSpec((tm, D), row), x)]
    if tps > 1:
        items += [(pl.BlockSpec((HL, D), lambda i: (jnp.maximum(i * (tm // HL) - 1, 0), 0)), x),
                  (pl.BlockSpec((HL, D), lambda i: (jnp.minimum((i + 1) * (tm // HL), n_hblk - 1), 0)), x)]
    items += [(_const_spec((1, D)), nw), (_mod_spec(layer, lambda i: row_fn(i * tm)), mod),
              _W(w_in, SSD_INNER, 0), _W(w_in, SSD_XBC, SSD_INNER),
              (_const_spec((D, LANES)), w_dt), (_const_spec((1, LANES)), dt_bias),
              (_const_spec((SSD_CONV + 1, SSD_XBC)), w_conv), (_const_spec((1, SSD_XBC)), b_conv)]
    in_specs, args, groups = _operands(items)
    return pl.pallas_call(
        _grouped(functools.partial(_ssd_in_kernel, tps), groups),
        grid=(M // tm,),
        in_specs=in_specs,
        out_specs=[pl.BlockSpec((tm, SSD_INNER), row), pl.BlockSpec((tm, SSD_XBC), row), pl.BlockSpec((tm, LANES), row)],
        out_shape=[jax.ShapeDtypeStruct((M, SSD_INNER), F32), jax.ShapeDtypeStruct((M, SSD_XBC), F32),
                   jax.ShapeDtypeStruct((M, LANES), F32)],
        scratch_shapes=[pltpu.VMEM((SSD_XBC // LANES, tm + 2 * HL, LANES), F32)],
        compiler_params=_cparams(1),
        name="ssd_in",
    )(*args)


def _cumsum_rows(a, reverse):
    n = a.shape[0]
    row = lax.broadcasted_iota(jnp.int32, a.shape, 0)
    k = 1
    while k < n:
        if reverse:
            a = a + jnp.where(row < n - k, pltpu.roll(a, n - k, 0), 0.0)
        else:
            a = a + jnp.where(row >= k, pltpu.roll(a, k, 0), 0.0)
        k *= 2
    return a


def _ssd_core_kernel(nc, has_h0, emit_state, *refs):
    refs = list(refs)
    dir_refs = [refs[0:2], refs[2:4]]
    alog_ref, dsk_ref = refs[4:6]
    refs = refs[6:]
    if has_h0:
        h0_ref = refs.pop(0)
    y_refs = [refs.pop(0), refs.pop(0)]
    if emit_state:
        so_ref = refs.pop(0)
    (st_ref,) = refs
    c = pl.program_id(1)
    n_blk = SSD_INNER // LANES

    @pl.when(c == 0)
    def _init():
        if has_h0:
            for dr in range(2):
                for j in range(n_blk):
                    st_ref[dr, :, j * LANES:(j + 1) * LANES] = h0_ref[dr, j * LANES:(j + 1) * LANES, :].T
        else:
            st_ref[...] = jnp.zeros(st_ref.shape, F32)

    A = -jnp.exp(alog_ref[...])
    jobs = []
    for dr in range(2):
        x_ref, dt_ref = dir_refs[dr]
        y_ref = y_refs[dr]

        def x_at(c0, x_ref=x_ref):
            return x_ref[:, c0:c0 + LANES]

        def y_put(c0, y, y_ref=y_ref):
            y_ref[:, c0:c0 + LANES] = y

        jobs.append((dr, x_at, dt_ref[...], y_put, None))
    _ssd_chunks(jobs, A, st_ref, dsk_ref)

    if emit_state:
        @pl.when(c == nc - 1)
        def _emit():
            _ssd_emit_state(st_ref, so_ref)


def _ssd_emit_state(st_ref, so_ref):
    for dr in range(2):
        for j in range(SSD_INNER // LANES):
            so_ref[dr, j * LANES:(j + 1) * LANES, :] = st_ref[dr, :, j * LANES:(j + 1) * LANES].T


def _ssd_group_terms(x_at):
    N = SSD_STATE
    terms = []
    for g in range(SSD_GROUPS):
        bg_t = x_at(SSD_INNER + g * N).T
        cg = x_at(SSD_INNER + SSD_GN + g * N).astype(BF16)
        terms.append((bg_t, cg, jnp.dot(cg, bg_t.astype(BF16), preferred_element_type=F32)))
    return terms


def _ssd_chunks(jobs, A, st_ref, dsk_ref):
    Q, N, P = SSD_CHUNK, SSD_STATE, SSD_HEAD_DIM
    lo = _head_masks()
    ri = lax.broadcasted_iota(jnp.int32, (Q, Q), 0)
    ci = lax.broadcasted_iota(jnp.int32, (Q, Q), 1)
    prepared = []
    for dr, x_at, dt, y_put, group_terms in jobs:
        cum = _cumsum_rows(dt * A, reverse=(dr == 1)) * LOG2E
        prepared.append(dict(
            dr=dr, x_at=x_at, y_put=y_put, cum=cum, cum_t=cum.T, dt_t=dt.T,
            tot_row=Q - 1 if dr == 0 else 0, mask=(ri >= ci) if dr == 0 else (ri <= ci),
            terms=_ssd_group_terms(x_at) if group_terms is None else group_terms))
    for g in range(SSD_GROUPS):
        for pp in range(SSD_HEADS // SSD_GROUPS // 2):
            for job in prepared:
                _ssd_pair(job, g, pp, lo, st_ref, dsk_ref)


def _ssd_pair(job, g, pp, lo, st_ref, dsk_ref):
    Q, N, P = SSD_CHUNK, SSD_STATE, SSD_HEAD_DIM
    dr, x_at, y_put, cum, cum_t, dt_t = (job[k] for k in ("dr", "x_at", "y_put", "cum", "cum_t", "dt_t"))
    tot_row, mask = job["tot_row"], job["mask"]
    bg_t, cg, cb = job["terms"][g]
    head = g * (SSD_HEADS // SSD_GROUPS) + 2 * pp
    c0 = head * P
    li = dr * SSD_HEADS + head
    xpair = x_at(c0)
    xpair_b = xpair.astype(BF16)
    cum_cols = [cum[:, li:li + 1], cum[:, li + 1:li + 2]]
    cum_x = jnp.where(lo, cum_cols[0], cum_cols[1])
    yd, sd = [], []
    for hh in range(2):
        dt_row = dt_t[li + hh:li + hh + 1, :]
        cum_row = cum_t[li + hh:li + hh + 1, :]
        decay = jnp.exp2(jnp.where(mask, cum_cols[hh] - cum_row, -jnp.inf))
        yd.append(jnp.dot((cb * decay * dt_row).astype(BF16), xpair_b, preferred_element_type=F32))
        to_end = jnp.exp2(cum_row[:, tot_row:tot_row + 1] - cum_row)
        sd.append(jnp.dot((bg_t * (dt_row * to_end)).astype(BF16), xpair_b, preferred_element_type=F32))
    st = st_ref[dr, :, c0:c0 + LANES]
    y = (jnp.where(lo, yd[0], yd[1])
         + jnp.dot(cg, st.astype(BF16), preferred_element_type=F32) * jnp.exp2(cum_x))
    if dr == 0:
        y = y + xpair * dsk_ref[:, c0:c0 + LANES]
    y_put(c0, y)
    tot = cum_x[tot_row:tot_row + 1, :]
    st_ref[dr, :, c0:c0 + LANES] = st * jnp.exp2(tot) + jnp.where(lo, sd[0], sd[1])


def _ssd_seq_kernel(x_ref, nw_ref, mod_ref, wz_ref, wx_ref, wdt_ref, dtb_ref, wc_ref, bc_ref, alog_ref, dsk_ref, gnw_ref, wo_ref,
                    o_ref, so_ref, xpad_ref, xbc_ref, sz_ref, y_ref, yb_ref, st_ref):
    Q = SSD_CHUNK
    nc = x_ref.shape[0] // Q
    x = x_ref[...]
    hb = _modnorm(x, nw_ref[...], mod_ref[...]).astype(BF16)
    dt = _ssd_project(hb, None, wz_ref, wx_ref, wdt_ref, dtb_ref, wc_ref, bc_ref, xpad_ref, sz_ref, xbc_ref)
    st_ref[...] = jnp.zeros(st_ref.shape, F32)
    A = -jnp.exp(alog_ref[...])

    def x_at_chunk(ck):
        return lambda c0: xbc_ref[ck * Q:(ck + 1) * Q, c0:c0 + LANES]

    terms = [_ssd_group_terms(x_at_chunk(ck)) for ck in range(nc)]

    def y_put_to(ref, ck):
        def y_put(c0, y):
            ref[ck * Q:(ck + 1) * Q, c0:c0 + LANES] = y
        return y_put

    for k in range(nc):
        kb = nc - 1 - k
        _ssd_chunks([(0, x_at_chunk(k), dt[k * Q:(k + 1) * Q], y_put_to(y_ref, k), terms[k]),
                     (1, x_at_chunk(kb), dt[kb * Q:(kb + 1) * Q], y_put_to(yb_ref, kb), terms[kb])],
                    A, st_ref, dsk_ref)
    _ssd_emit_state(st_ref, so_ref)
    g = _rms((y_ref[...] + yb_ref[...]) * sz_ref[...], gnw_ref[...]).astype(BF16)
    o_ref[...] = x + mod_ref[:, 2 * D:3 * D] * _wdot(g, wo_ref, 0, D)


def _ssd_seq(x, seq_len, layer, row_fn, mod, nw, w_in, w_dt, dt_bias, w_conv, b_conv, a_log, d_skip, norm_w, w_out):
    M = x.shape[0]
    n_seq = M // seq_len
    HL = SSD_HALO
    row = lambda i: (i, 0)
    st_blk = pl.BlockSpec((None, 2, SSD_INNER, SSD_STATE), lambda i: (i, 0, 0, 0))
    in_specs, args, groups = _operands([
        (pl.BlockSpec((seq_len, D), row), x), (_const_spec((1, D)), nw),
        (_mod_spec(layer, lambda i: row_fn(i * seq_len)), mod),
        _W(w_in, SSD_INNER, 0), _W(w_in, SSD_XBC, SSD_INNER),
        (_const_spec((D, LANES)), w_dt), (_const_spec((1, LANES)), dt_bias),
        (_const_spec((SSD_CONV + 1, SSD_XBC)), w_conv), (_const_spec((1, SSD_XBC)), b_conv),
        (_const_spec((1, LANES)), a_log), (_const_spec((1, SSD_INNER)), d_skip),
        (_const_spec((1, SSD_INNER)), norm_w), _W(w_out)])
    return pl.pallas_call(
        _grouped(_ssd_seq_kernel, groups),
        grid=(n_seq,),
        in_specs=in_specs,
        out_specs=[pl.BlockSpec((seq_len, D), row), st_blk],
        out_shape=[jax.ShapeDtypeStruct((M, D), F32), jax.ShapeDtypeStruct((n_seq, 2, SSD_INNER, SSD_STATE), F32)],
        scratch_shapes=[pltpu.VMEM((SSD_XBC // LANES, seq_len + 2 * HL, LANES), F32),
                        pltpu.VMEM((seq_len, SSD_XBC), F32), pltpu.VMEM((seq_len, SSD_INNER), F32),
                        pltpu.VMEM((seq_len, SSD_INNER), F32), pltpu.VMEM((seq_len, SSD_INNER), F32),
                        pltpu.VMEM((2, SSD_STATE, SSD_INNER), F32)],
        compiler_params=_cparams(1),
        name="ssd_seq",
    )(*args)


def _ssd_core(xbc, dtp, a_log, d_skip, h0, n_seq, seq_len, emit_state):
    Q = SSD_CHUNK
    nc = seq_len // Q
    M = n_seq * seq_len

    def chunk(dr):
        return (lambda b, c: b * nc + c) if dr == 0 else (lambda b, c: b * nc + nc - 1 - c)

    in_specs, args = [], []
    for dr in range(2):
        ch = chunk(dr)
        in_specs += [pl.BlockSpec((Q, SSD_XBC), lambda b, c, ch=ch: (ch(b, c), 0)),
                     pl.BlockSpec((Q, LANES), lambda b, c, ch=ch: (ch(b, c), 0))]
        args += [xbc, dtp]
    in_specs += [_const_spec((1, LANES)), _const_spec((1, SSD_INNER))]
    args += [a_log, d_skip]
    st_blk = pl.BlockSpec((None, 2, SSD_INNER, SSD_STATE), lambda b, c: (b, 0, 0, 0))
    has_h0 = h0 is not None
    if has_h0:
        in_specs.append(st_blk)
        args.append(h0)
    out_specs = [pl.BlockSpec((Q, SSD_INNER), lambda b, c: (chunk(0)(b, c), 0)),
                 pl.BlockSpec((Q, SSD_INNER), lambda b, c: (chunk(1)(b, c), 0))]
    out_shape = [jax.ShapeDtypeStruct((M, SSD_INNER), F32)] * 2
    if emit_state:
        out_specs.append(st_blk)
        out_shape.append(jax.ShapeDtypeStruct((n_seq, 2, SSD_INNER, SSD_STATE), F32))
    return pl.pallas_call(
        functools.partial(_ssd_core_kernel, nc, has_h0, emit_state),
        grid=(n_seq, nc),
        in_specs=in_specs,
        out_specs=out_specs,
        out_shape=out_shape,
        scratch_shapes=[pltpu.VMEM((2, SSD_STATE, SSD_INNER), F32)],
        compiler_params=_cparams(2),
        name="ssd_core",
    )(*args)


def _ssd_out_kernel(yf_ref, yb_ref, sz_ref, x_ref, mod_ref, nw_ref, w_ref, o_ref):
    g = _rms((yf_ref[...] + yb_ref[...]) * sz_ref[...], nw_ref[...]).astype(BF16)
    o_ref[...] = x_ref[...] + mod_ref[:, 2 * D:3 * D] * _wdot(g, w_ref, 0, D)


def _ssd_out(yf, yb, sz, x, layer, row_fn, mod, norm_w, w_out):
    M = x.shape[0]
    tm = TOKEN_TILE
    row = lambda i: (i, 0)
    wide = pl.BlockSpec((tm, SSD_INNER), row)
    in_specs, args, groups = _operands([
        (wide, yf), (wide, yb), (wide, sz), (pl.BlockSpec((tm, D), row), x),
        (_mod_spec(layer, lambda i: row_fn(i * tm)), mod), (_const_spec((1, SSD_INNER)), norm_w), _W(w_out)])
    return pl.pallas_call(
        _grouped(_ssd_out_kernel, groups),
        grid=(M // tm,),
        in_specs=in_specs,
        out_specs=pl.BlockSpec((tm, D), row),
        out_shape=jax.ShapeDtypeStruct((M, D), F32),
        compiler_params=_cparams(1),
        name="ssd_out",
    )(*args)


def _ssd_layer(xc, xl, layer, mod, nw, p, n_lat):
    j = layer // 3
    w_in = p["ssd_w_in"][j].astype(BF16)
    w_dt = jnp.pad(w_in[:, SSD_IN_COLS:], ((0, 0), (0, LANES - 2 * SSD_HEADS)))
    dt_bias = jnp.pad(p["ssd_dt_bias"][j].reshape(1, -1), ((0, 0), (0, LANES - 2 * SSD_HEADS)))
    a_log = jnp.pad(p["ssd_a_log"][j].reshape(1, -1), ((0, 0), (0, LANES - 2 * SSD_HEADS)))
    w_conv = jnp.pad(p["ssd_w_conv"][j], ((0, 1), (0, 0)))
    b_conv = p["ssd_b_conv"][j][None]
    d_skip = jnp.repeat(p["ssd_d"][j], SSD_HEAD_DIM)[None]
    norm_w = p["ssd_norm_w"][j][None]
    w_out = p["ssd_w_out"][j].astype(BF16)
    row_ctx = lambda tok: 0
    row_lat = lambda tok: 1 + tok // DEC_SEQ
    oc, s_new = _ssd_seq(xc, SEQ, layer, row_ctx, mod, nw, w_in, w_dt, dt_bias, w_conv, b_conv, a_log, d_skip, norm_w, w_out)
    szl, xbc_l, dt_l = _ssd_in(xl, DEC_SEQ, layer, row_lat, mod, nw, w_in, w_dt, dt_bias, w_conv, b_conv)
    h0 = p["state_ssm"][:n_lat, j].reshape(n_lat, 2, SSD_INNER, SSD_STATE)
    yfl, ybl = _ssd_core(xbc_l, dt_l, a_log, d_skip, h0, n_lat, DEC_SEQ, False)
    ol = _ssd_out(yfl, ybl, szl, xl, layer, row_lat, mod, norm_w, w_out)
    return oc, ol, s_new


def kernel(x_prompt, x_sample, cache_k, cache_v, state_ssm, c, c_ctx, ada_w, ada_b, norm_w, final_norm_w, conv_w_in, conv_b_in, conv_w_dw, conv_b_dw, conv_ln_w, conv_ln_b, conv_w_out, conv_b_out, na_w_in, na_rpb, na_w_out, ssd_w_in, ssd_w_conv, ssd_b_conv, ssd_a_log, ssd_dt_bias, ssd_d, ssd_norm_w, ssd_w_out):
    p = dict(cache_k=cache_k, cache_v=cache_v, state_ssm=state_ssm,
             conv_w_in=conv_w_in, conv_b_in=conv_b_in, conv_w_dw=conv_w_dw, conv_b_dw=conv_b_dw,
             conv_ln_w=conv_ln_w, conv_ln_b=conv_ln_b, conv_w_out=conv_w_out, conv_b_out=conv_b_out,
             na_w_in=na_w_in, na_rpb=na_rpb, na_w_out=na_w_out,
             ssd_w_in=ssd_w_in, ssd_w_conv=ssd_w_conv, ssd_b_conv=ssd_b_conv, ssd_a_log=ssd_a_log,
             ssd_dt_bias=ssd_dt_bias, ssd_d=ssd_d, ssd_norm_w=ssd_norm_w, ssd_w_out=ssd_w_out)
    n_ctx, n_lat = x_prompt.shape[0], x_sample.shape[0]
    assert x_prompt.shape[1:] == (SEQ, D) and x_sample.shape[1:] == (DEC_SEQ, D)
    assert (DEPTH - 1) % 3 == 0, "the final RMSNorm is fused into a trailing Conformer layer"
    xc = x_prompt.reshape(n_ctx * SEQ, D)
    xl = x_sample.reshape(n_lat * DEC_SEQ, D)
    cond8 = jnp.concatenate([c_ctx[None], c, jnp.zeros((8 - 1 - n_lat, D), F32)], axis=0)
    mod = _modulation(cond8, ada_w, ada_b).reshape(DEPTH, 8, 1, 3 * D)
    row_ctx = lambda tok: 0
    row_lat = lambda tok: 1 + tok // DEC_SEQ
    new_k, new_v, new_s = [], [], []
    for i in range(DEPTH):
        kind, j = i % 3, i // 3
        nw = norm_w[i][None]
        if kind == 0:
            cp = _prep_conv(p, j)
            fw = final_norm_w[None] if i == DEPTH - 1 else None
            xc = _conv_layer(xc, SEQ, i, row_ctx, mod, nw, *cp, fw)
            xl = _conv_layer(xl, DEC_SEQ, i, row_lat, mod, nw, *cp, fw)
        elif kind == 1:
            xc, xl, k_c, v_c = _na_layer(xc, xl, i, mod, nw, p, n_lat)
            new_k.append(k_c.reshape(n_ctx, SEQ, NA_HEADS, NA_HEAD_DIM))
            new_v.append(v_c.reshape(n_ctx, SEQ, NA_HEADS, NA_HEAD_DIM))
        else:
            xc, xl, s_c = _ssd_layer(xc, xl, i, mod, nw, p, n_lat)
            new_s.append(s_c.reshape(n_ctx, 2, SSD_HEADS, SSD_HEAD_DIM, SSD_STATE))
    return (xc.reshape(n_ctx, SEQ, D), xl.reshape(n_lat, DEC_SEQ, D),
            jnp.stack(new_k, axis=1), jnp.stack(new_v, axis=1), jnp.stack(new_s, axis=1))
```

```python
import functools

import jax
import jax.numpy as jnp
from jax import lax
from jax.experimental import pallas as pl
from jax.experimental.pallas import tpu as pltpu

F32 = jnp.float32
BF16 = jnp.bfloat16

D = 1024
DEPTH = 4
SEQ = 256
DEC_SEQ = 2048
GRID_W = 64
RMS_EPS = 1e-6
LN_EPS = 1e-5
LOG2E = 1.4426950408889634
CONF_K = 31
NA_HEADS = 16
NA_HEAD_DIM = 64
NA_KH = 8
NA_KW = 16
NEG_INF = -1e30
SSD_INNER = 2048
SSD_HEADS = 32
SSD_HEAD_DIM = 64
SSD_STATE = 128
SSD_GROUPS = 4
SSD_GN = SSD_GROUPS * SSD_STATE
SSD_CONV = 7
SSD_CHUNK = 128
SSD_XBC = SSD_INNER + 2 * SSD_GN

LANES = 128
TOKEN_TILE = 256
W_CHUNK = 512
CONV_HALO = 16
SSD_HALO = 8
VMEM_LIMIT = 56 * 1024 * 1024


def _cparams(n_axes):
    return pltpu.CompilerParams(dimension_semantics=("arbitrary",) * n_axes,
                                vmem_limit_bytes=VMEM_LIMIT)


def _const_spec(shape):
    nd = len(shape)
    return pl.BlockSpec(shape, lambda *_: (0,) * nd, pipeline_mode=pl.Buffered(1))


class _W:
    def __init__(self, w, ncols=None, col0=0):
        self.w, self.col0 = w, col0
        self.ncols = w.shape[1] if ncols is None else ncols


def _operands(items):
    specs, args, groups = [], [], []
    for it in items:
        if isinstance(it, _W):
            n, b0, K = it.ncols // W_CHUNK, it.col0 // W_CHUNK, it.w.shape[0]
            for c in range(b0, b0 + n):
                specs.append(pl.BlockSpec((K, W_CHUNK), lambda *_, c=c: (0, c), pipeline_mode=pl.Buffered(1)))
                args.append(it.w)
            groups.append(n)
        else:
            specs.append(it[0])
            args.append(it[1])
            groups.append(None)
    return specs, args, groups


def _grouped(body, groups):
    def kernel_fn(*refs):
        refs = list(refs)
        packed = []
        for g in groups:
            if g is None:
                packed.append(refs.pop(0))
            else:
                packed.append(tuple(refs[:g]))
                del refs[:g]
        return body(*packed, *refs)
    return kernel_fn


def _wdot(h, w_refs, c0, n):
    outs = []
    c = c0
    while c < c0 + n:
        j, off = divmod(c, W_CHUNK)
        m = min(W_CHUNK - off, c0 + n - c)
        outs.append(jnp.dot(h, w_refs[j][:, off:off + m], preferred_element_type=F32))
        c += m
    return outs[0] if len(outs) == 1 else jnp.concatenate(outs, axis=1)


def _sigmoid(x):
    return 1.0 / (1.0 + jnp.exp(-x))


def _silu(x):
    return x * _sigmoid(x)


def _rms(x, w):
    ms = jnp.mean(x * x, axis=-1, keepdims=True)
    return x * lax.rsqrt(ms + RMS_EPS) * w


def _modnorm(x, nw, mod):
    return _rms(x, nw) * (1.0 + mod[:, D:2 * D]) + mod[:, :D]


def _mod_spec(layer, row_fn):
    return pl.BlockSpec((None, None, 1, 3 * D), lambda i, *_: (layer, row_fn(i), 0, 0))


def _mod_kernel(cond_ref, w_ref, b_ref, o_ref):
    s = _silu(cond_ref[...]).astype(BF16)
    o_ref[...] = jnp.dot(s, w_ref[...].astype(BF16), preferred_element_type=F32) + b_ref[...]


def _modulation(cond8, ada_w, ada_b):
    tn = 1024
    return pl.pallas_call(
        _mod_kernel,
        grid=(DEPTH, 3 * D // tn),
        in_specs=[pl.BlockSpec((8, D), lambda i, j: (0, 0)),
                  pl.BlockSpec((None, D, tn), lambda i, j: (i, 0, j)),
                  pl.BlockSpec((None, 1, tn), lambda i, j: (i, 0, j))],
        out_specs=pl.BlockSpec((None, 8, tn), lambda i, j: (i, 0, j)),
        out_shape=jax.ShapeDtypeStruct((DEPTH, 8, 3 * D), F32),
        compiler_params=_cparams(2),
        name="adaln_mod",
    )(cond8, ada_w, ada_b.reshape(DEPTH, 1, 3 * D))


CONV_CHUNK = 256


def _conv_kernel(tiles_per_seq, final, *refs):
    refs = list(refs)
    x_ref = refs.pop(0)
    if tiles_per_seq > 1:
        xp_ref = refs.pop(0)
        xn_ref = refs.pop(0)
    nw_ref, mod_ref, wi_ref, bi_ref, wdw_ref, bdw_ref, lnw_ref, lnb_ref, wo_ref, bo_ref = refs[:10]
    refs = refs[10:]
    if final:
        fw_ref = refs.pop(0)
    o_ref, ubuf, cbuf = refs
    T = x_ref.shape[0]
    H = CONV_HALO
    RB = 128

    hb = _modnorm(x_ref[...], nw_ref[...], mod_ref[...]).astype(BF16)
    if tiles_per_seq > 1:
        t = pl.program_id(0) % tiles_per_seq
        hh = _modnorm(jnp.concatenate([xp_ref[...], xn_ref[...]], axis=0), nw_ref[...], mod_ref[...]).astype(BF16)

    def glu(h, cols):
        v = _wdot(h, wi_ref, cols.start, CONV_CHUNK) + bi_ref[:, cols]
        gcols = slice(D + cols.start, D + cols.stop)
        g = _wdot(h, wi_ref, gcols.start, CONV_CHUNK) + bi_ref[:, gcols]
        return v * _sigmoid(g)

    for ch in range(D // CONV_CHUNK):
        cols = slice(ch * CONV_CHUNK, (ch + 1) * CONV_CHUNK)
        u_main = glu(hb, cols)
        if tiles_per_seq > 1:
            uh = glu(hh, cols)
            u_prev = jnp.where(t > 0, uh[0:H], 0.0)
            u_next = jnp.where(t < tiles_per_seq - 1, uh[H:2 * H], 0.0)
        else:
            u_prev = u_next = jnp.zeros((H, CONV_CHUNK), F32)
        for cb in range(CONV_CHUNK // LANES):
            blk = ch * (CONV_CHUNK // LANES) + cb
            lc = slice(blk * LANES, (blk + 1) * LANES)
            cc = slice(cb * LANES, (cb + 1) * LANES)
            ubuf[blk, 0:H, :] = u_prev[:, cc]
            ubuf[blk, H:H + T, :] = u_main[:, cc]
            ubuf[blk, H + T:H + T + H, :] = u_next[:, cc]
            for rb in range(T // RB):
                acc = jnp.zeros((RB, LANES), F32)
                for k in range(CONF_K):
                    r0 = rb * RB + H - CONF_K // 2 + k
                    acc = acc + ubuf[blk, r0:r0 + RB, :] * wdw_ref[k:k + 1, lc]
                cbuf[rb * RB:(rb + 1) * RB, lc] = acc + bdw_ref[:, lc]

    sz = _silu(_wdot(hb, wi_ref, 2 * D, D) + bi_ref[:, 2 * D:3 * D])
    c = cbuf[...]
    mu = jnp.mean(c, axis=-1, keepdims=True)
    xc = c - mu
    var = jnp.mean(xc * xc, axis=-1, keepdims=True)
    ln = xc * lax.rsqrt(var + LN_EPS) * lnw_ref[...] + lnb_ref[...]
    g = (_silu(ln) * sz).astype(BF16)
    y = _wdot(g, wo_ref, 0, D) + bo_ref[...]
    xn = x_ref[...] + mod_ref[:, 2 * D:3 * D] * y
    if final:
        xn = _rms(xn, fw_ref[...])
    o_ref[...] = xn


def _conv_layer(x, seq_len, layer, row_fn, mod, nw, w_in, b_in, w_dw, b_dw, ln_w, ln_b, w_out, b_out, final_w):
    M = x.shape[0]
    T = TOKEN_TILE
    tps = seq_len // T
    H = CONV_HALO
    n_halo_blocks = M // H
    row = lambda i: (i, 0)
    items = [(pl.BlockSpec((T, D), row), x)]
    if tps > 1:
        items += [(pl.BlockSpec((H, D), lambda i: (jnp.maximum(i * (T // H) - 1, 0), 0)), x),
                  (pl.BlockSpec((H, D), lambda i: (jnp.minimum((i + 1) * (T // H), n_halo_blocks - 1), 0)), x)]
    items += [(_const_spec((1, D)), nw), (_mod_spec(layer, lambda i: row_fn(i * T)), mod),
              _W(w_in), (_const_spec((1, 3 * D)), b_in),
              (_const_spec((CONF_K + 1, D)), w_dw), (_const_spec((1, D)), b_dw),
              (_const_spec((1, D)), ln_w), (_const_spec((1, D)), ln_b),
              _W(w_out), (_const_spec((1, D)), b_out)]
    final = final_w is not None
    if final:
        items.append((_const_spec((1, D)), final_w))
    in_specs, args, groups = _operands(items)
    return pl.pallas_call(
        _grouped(functools.partial(_conv_kernel, tps, final), groups),
        grid=(M // T,),
        in_specs=in_specs,
        out_specs=pl.BlockSpec((T, D), row),
        out_shape=jax.ShapeDtypeStruct((M, D), F32),
        scratch_shapes=[pltpu.VMEM((D // LANES, T + 2 * H, LANES), F32), pltpu.VMEM((T, D), F32)],
        compiler_params=_cparams(1),
        name="conv_layer",
    )(*args)


def _prep_conv(p, j):
    w_dw = jnp.pad(p["conv_w_dw"][j], ((0, 1), (0, 0)))
    return (p["conv_w_in"][j].astype(BF16), p["conv_b_in"][j][None], w_dw, p["conv_b_dw"][j][None],
            p["conv_ln_w"][j][None], p["conv_ln_b"][j][None], p["conv_w_out"][j].astype(BF16), p["conv_b_out"][j][None])


def _out_proj_kernel(g_ref, x_ref, mod_ref, w_ref, o_ref):
    y = _wdot(g_ref[...].astype(BF16), w_ref, 0, D)
    o_ref[...] = x_ref[...] + mod_ref[:, 2 * D:3 * D] * y


def _out_proj(g, x, layer, row_fn, mod, w_out):
    M, K = g.shape
    tm = TOKEN_TILE
    row = lambda i: (i, 0)
    in_specs, args, groups = _operands([
        (pl.BlockSpec((tm, K), row), g), (pl.BlockSpec((tm, D), row), x),
        (_mod_spec(layer, lambda i: row_fn(i * tm)), mod), _W(w_out)])
    return pl.pallas_call(
        _grouped(_out_proj_kernel, groups),
        grid=(M // tm,),
        in_specs=in_specs,
        out_specs=pl.BlockSpec((tm, D), row),
        out_shape=jax.ShapeDtypeStruct((M, D), F32),
        compiler_params=_cparams(1),
        name="out_proj",
    )(*args)


NA_SCALE = NA_HEAD_DIM ** -0.5
NA_ROWS = 32
NA_WIN = NA_KH * GRID_W
NT_DIMS = (((1,), (1,)), ((), ()))


def _na_in_kernel(x_ref, nw_ref, mod_ref, w_ref, q_ref, k_ref, v_ref, sz_ref):
    hb = _modnorm(x_ref[...], nw_ref[...], mod_ref[...]).astype(BF16)
    q_ref[...] = _wdot(hb, w_ref, 0, D).astype(q_ref.dtype)
    k_ref[...] = _wdot(hb, w_ref, D, D).astype(k_ref.dtype)
    v_ref[...] = _wdot(hb, w_ref, 2 * D, D).astype(v_ref.dtype)
    sz_ref[...] = _silu(_wdot(hb, w_ref, 3 * D, D))


def _na_in(x, layer, row_fn, mod, nw, w_in, kv_dtype):
    M = x.shape[0]
    tm = TOKEN_TILE
    row = lambda i: (i, 0)
    blk = pl.BlockSpec((tm, D), row)
    in_specs, args, groups = _operands([
        (blk, x), (_const_spec((1, D)), nw), (_mod_spec(layer, lambda i: row_fn(i * tm)), mod), _W(w_in)])
    return pl.pallas_call(
        _grouped(_na_in_kernel, groups),
        grid=(M // tm,),
        in_specs=in_specs,
        out_specs=[blk, blk, blk, blk],
        out_shape=[jax.ShapeDtypeStruct((M, D), BF16), jax.ShapeDtypeStruct((M, D), kv_dtype),
                   jax.ShapeDtypeStruct((M, D), kv_dtype), jax.ShapeDtypeStruct((M, D), F32)],
        compiler_params=_cparams(1),
        name="na_in",
    )(*args)


def _head_masks():
    lane = lax.broadcasted_iota(jnp.int32, (1, LANES), 1)
    return lane < NA_HEAD_DIM


def _na_seq_kernel(x_ref, nw_ref, mod_ref, wi_ref, wo_ref, o_ref, k_ref, v_ref, g_ref):
    n = x_ref.shape[0]
    lo = _head_masks()
    x = x_ref[...]
    hb = _modnorm(x, nw_ref[...], mod_ref[...]).astype(BF16)
    hpc = CONV_CHUNK // NA_HEAD_DIM
    staged = []
    for ch in range(D // CONV_CHUNK):
        cols = slice(ch * CONV_CHUNK, (ch + 1) * CONV_CHUNK)
        q = _wdot(hb, wi_ref, cols.start, CONV_CHUNK).astype(BF16)
        k = _wdot(hb, wi_ref, D + cols.start, CONV_CHUNK)
        v = _wdot(hb, wi_ref, 2 * D + cols.start, CONV_CHUNK)
        sz = _silu(_wdot(hb, wi_ref, 3 * D + cols.start, CONV_CHUNK))
        k_ref[:, ch * hpc:(ch + 1) * hpc, :] = k.reshape(n, hpc, NA_HEAD_DIM)
        v_ref[:, ch * hpc:(ch + 1) * hpc, :] = v.reshape(n, hpc, NA_HEAD_DIM)
        kb = k.astype(BF16)
        vb = v.astype(BF16)
        for pp in range(CONV_CHUNK // LANES):
            pc = slice(pp * LANES, (pp + 1) * LANES)
            qp = q[:, pc]
            zero = jnp.zeros_like(qp)
            qs = jnp.concatenate([jnp.where(lo, qp, zero), jnp.where(lo, zero, qp)], axis=0)
            s = lax.dot_general(qs, kb[:, pc], NT_DIMS, preferred_element_type=F32) * NA_SCALE
            staged.append((s, vb[:, pc], sz[:, pc], cols.start + pp * LANES))
    for s, vp, szp, c0 in staged:
        e = jnp.exp(s - jnp.max(s, axis=-1, keepdims=True))
        l = jnp.sum(e, axis=-1, keepdims=True)
        o = jnp.dot(e.astype(BF16), vp, preferred_element_type=F32) / l
        g_ref[:, c0:c0 + LANES] = (jnp.where(lo, o[:n], o[n:]) * szp).astype(BF16)
    y = _wdot(g_ref[...], wo_ref, 0, D)
    o_ref[...] = x + mod_ref[:, 2 * D:3 * D] * y


def _na_seq(x, seq_len, layer, row_fn, mod, nw, w_in, w_out):
    M = x.shape[0]
    row = lambda i: (i, 0)
    kv_blk = pl.BlockSpec((seq_len, NA_HEADS, NA_HEAD_DIM), lambda i: (i, 0, 0))
    kv_shape = jax.ShapeDtypeStruct((M, NA_HEADS, NA_HEAD_DIM), F32)
    in_specs, args, groups = _operands([
        (pl.BlockSpec((seq_len, D), row), x), (_const_spec((1, D)), nw),
        (_mod_spec(layer, lambda i: row_fn(i * seq_len)), mod), _W(w_in), _W(w_out)])
    return pl.pallas_call(
        _grouped(_na_seq_kernel, groups),
        grid=(M // seq_len,),
        in_specs=in_specs,
        out_specs=[pl.BlockSpec((seq_len, D), row), kv_blk, kv_blk],
        out_shape=[jax.ShapeDtypeStruct((M, D), F32), kv_shape, kv_shape],
        scratch_shapes=[pltpu.VMEM((seq_len, D), BF16)],
        compiler_params=_cparams(1),
        name="na_seq",
    )(*args)


def _rpb_kernel(rpb_ref, o_ref):
    h = pl.program_id(0)
    c = lax.broadcasted_iota(jnp.int32, (GRID_W, LANES), 0)
    lane = lax.broadcasted_iota(jnp.int32, (GRID_W, LANES), 1)
    kc = lane & (GRID_W - 1)
    upper = lax.broadcasted_iota(jnp.int32, (1, LANES), 1) >= GRID_W
    cs = jnp.clip(c - NA_KW // 2, 0, GRID_W - NA_KW)
    valid = (kc >= cs) & (kc < cs + NA_KW)
    n_pair = 2 * NA_KH - 2

    def toeplitz(dri, lane0):
        row = rpb_ref[pl.ds(h * (2 * NA_KH - 1) + dri, 1), :]
        rows = jnp.broadcast_to(row, (GRID_W, LANES))
        return pltpu.roll(rows, (LANES - (NA_KW - 1) + lane0) % LANES, 1, stride=1, stride_axis=0)

    for dri in range(n_pair):
        tile = jnp.where(upper, toeplitz(dri + 1, GRID_W), toeplitz(dri, 0))
        tile = jnp.where(valid, tile, NEG_INF)
        for idx in range(NA_KH):
            jj2 = dri - (NA_KH - 1) + idx
            if jj2 % 2 == 0 and 0 <= jj2 // 2 < NA_KH // 2:
                jj = jj2 // 2
                o_ref[idx, :, jj * LANES:(jj + 1) * LANES] = tile


def _rpb_table(rpb):
    n_rows = NA_HEADS * (2 * NA_KH - 1)
    rows = jnp.pad(rpb.reshape(n_rows, 2 * NA_KW - 1), ((0, 0), (0, LANES - (2 * NA_KW - 1))))
    return pl.pallas_call(
        _rpb_kernel,
        grid=(NA_HEADS,),
        in_specs=[_const_spec((n_rows, LANES))],
        out_specs=pl.BlockSpec((NA_KH, None, GRID_W, NA_WIN), lambda h: (0, h, 0, 0)),
        out_shape=jax.ShapeDtypeStruct((NA_KH, NA_HEADS, GRID_W, NA_WIN), F32),
        compiler_params=_cparams(1),
        name="na_rpb_table",
    )(rows)


def _na_row_start(r):
    return jnp.clip(r - NA_KH // 2, 0, NA_ROWS - NA_KH)


def _na_lat_kernel(q_ref, k_ref, v_ref, ckt_ref, cv_ref, bias_ref, sz_ref, o_ref):
    lo = _head_masks()
    k0 = pl.multiple_of(_na_row_start(pl.program_id(1)) * GRID_W, GRID_W)
    scores = []
    for p in range(NA_HEADS // 2):
        cols = slice(p * LANES, (p + 1) * LANES)
        qp = q_ref[:, cols]
        kw = k_ref[pl.ds(k0, NA_WIN), cols]
        ck_t = ckt_ref[cols, :]
        zero = jnp.zeros_like(qp)
        qs = jnp.concatenate([jnp.where(lo, qp, zero), jnp.where(lo, zero, qp)], axis=0)
        bias = bias_ref[2 * p:2 * p + 2].reshape(2 * GRID_W, NA_WIN)
        scores.append((lax.dot_general(qs, kw, NT_DIMS, preferred_element_type=F32) * NA_SCALE + bias,
                       jnp.dot(qs, ck_t, preferred_element_type=F32) * NA_SCALE))
    for p in range(NA_HEADS // 2):
        cols = slice(p * LANES, (p + 1) * LANES)
        vw = v_ref[pl.ds(k0, NA_WIN), cols]
        cv = cv_ref[:, cols]
        s_loc, s_ctx = scores[p]
        m = jnp.maximum(jnp.max(s_loc, axis=-1, keepdims=True), jnp.max(s_ctx, axis=-1, keepdims=True))
        e_loc = jnp.exp(s_loc - m)
        e_ctx = jnp.exp(s_ctx - m)
        l = jnp.sum(e_loc, axis=-1, keepdims=True) + jnp.sum(e_ctx, axis=-1, keepdims=True)
        o = (jnp.dot(e_loc.astype(BF16), vw, preferred_element_type=F32)
             + jnp.dot(e_ctx.astype(BF16), cv, preferred_element_type=F32)) / l
        o_ref[:, cols] = jnp.where(lo, o[:GRID_W], o[GRID_W:]) * sz_ref[:, cols]


def _na_lat(q, k, v, ck_t, cv, bias, sz, n_lat):
    T = NA_ROWS * GRID_W
    past = cv.shape[1]
    qblk = pl.BlockSpec((GRID_W, D), lambda b, r: (b * NA_ROWS + r, 0))
    kvblk = pl.BlockSpec((T, D), lambda b, r: (b, 0))
    cblk = pl.BlockSpec((None, past, D), lambda b, r: (b, 0, 0))
    ctblk = pl.BlockSpec((None, D, past), lambda b, r: (b, 0, 0))
    bblk = pl.BlockSpec((None, NA_HEADS, GRID_W, NA_WIN), lambda b, r: (r - _na_row_start(r), 0, 0, 0))
    return pl.pallas_call(
        _na_lat_kernel,
        grid=(n_lat, NA_ROWS),
        in_specs=[qblk, kvblk, kvblk, ctblk, cblk, bblk, qblk],
        out_specs=qblk,
        out_shape=jax.ShapeDtypeStruct((n_lat * T, D), F32),
        compiler_params=_cparams(2),
        name="na_lat_attn",
    )(q, k, v, ck_t, cv, bias, sz)


def _na_layer(xc, xl, layer, mod, nw, p, n_lat):
    j = layer // 3
    w_in = p["na_w_in"][j].astype(BF16)
    w_out = p["na_w_out"][j].astype(BF16)
    row_ctx = lambda tok: 0
    row_lat = lambda tok: 1 + tok // (NA_ROWS * GRID_W)
    oc, kc, vc = _na_seq(xc, SEQ, layer, row_ctx, mod, nw, w_in, w_out)
    ql, kl, vl, szl = _na_in(xl, layer, row_lat, mod, nw, w_in, BF16)
    past = p["cache_k"].shape[2]
    ck_t = jnp.swapaxes(p["cache_k"][:n_lat, j].reshape(n_lat, past, D), 1, 2).astype(BF16)
    cv = p["cache_v"][:n_lat, j].reshape(n_lat, past, D).astype(BF16)
    gl = _na_lat(ql, kl, vl, ck_t, cv, _rpb_table(p["na_rpb"][j]), szl, n_lat)
    return oc, _out_proj(gl, xl, layer, row_lat, mod, w_out), kc, vc


SSD_IN_COLS = SSD_INNER + SSD_XBC


def _softplus(x):
    return jnp.maximum(x, 0.0) + jnp.log(1.0 + jnp.exp(-jnp.abs(x)))


def _ssd_in_kernel(tiles_per_seq, *refs):
    refs = list(refs)
    x_ref = refs.pop(0)
    if tiles_per_seq > 1:
        xp_ref = refs.pop(0)
        xn_ref = refs.pop(0)
    nw_ref, mod_ref, wz_ref, wx_ref, wdt_ref, dtb_ref, wc_ref, bc_ref, sz_ref, xbc_ref, dt_ref, xpad_ref = refs
    hb = _modnorm(x_ref[...], nw_ref[...], mod_ref[...]).astype(BF16)
    halo = None
    if tiles_per_seq > 1:
        t = pl.program_id(0) % tiles_per_seq
        xh = jnp.concatenate([xp_ref[...], xn_ref[...]], axis=0)
        halo = (_modnorm(xh, nw_ref[...], mod_ref[...]).astype(BF16), t > 0, t < tiles_per_seq - 1)
    dt_ref[...] = _ssd_project(hb, halo, wz_ref, wx_ref, wdt_ref, dtb_ref, wc_ref, bc_ref, xpad_ref, sz_ref, xbc_ref)


def _ssd_project(hb, halo, wz_ref, wx_ref, wdt_ref, dtb_ref, wc_ref, bc_ref, xpad_ref, sz_ref, xbc_ref):
    tm = hb.shape[0]
    HL = SSD_HALO
    RB = 128
    for j in range(SSD_XBC // D):
        c0 = j * D
        raw = _wdot(hb, wx_ref, c0, D)
        if halo is not None:
            rawh = _wdot(halo[0], wx_ref, c0, D)
            raw_prev = jnp.where(halo[1], rawh[0:HL], 0.0)
            raw_next = jnp.where(halo[2], rawh[HL:2 * HL], 0.0)
        else:
            raw_prev = raw_next = jnp.zeros((HL, D), F32)
        for cb in range(D // LANES):
            blk = j * (D // LANES) + cb
            lc = slice(blk * LANES, (blk + 1) * LANES)
            cc = slice(cb * LANES, (cb + 1) * LANES)
            xpad_ref[blk, 0:HL, :] = raw_prev[:, cc]
            xpad_ref[blk, HL:HL + tm, :] = raw[:, cc]
            xpad_ref[blk, HL + tm:HL + tm + HL, :] = raw_next[:, cc]
            for rb in range(tm // RB):
                acc = jnp.zeros((RB, LANES), F32)
                for k in range(SSD_CONV):
                    r0 = rb * RB + HL - SSD_CONV // 2 + k
                    acc = acc + xpad_ref[blk, r0:r0 + RB, :] * wc_ref[k:k + 1, lc]
                xbc_ref[rb * RB:(rb + 1) * RB, lc] = _silu(acc + bc_ref[:, lc])
    for j in range(SSD_INNER // D):
        sz_ref[:, j * D:(j + 1) * D] = _silu(_wdot(hb, wz_ref, j * D, D))
    raw = jnp.dot(hb, wdt_ref[...], preferred_element_type=F32) + dtb_ref[...]
    lane = lax.broadcasted_iota(jnp.int32, (1, LANES), 1)
    return jnp.where(lane < 2 * SSD_HEADS, _softplus(raw), 0.0)


def _ssd_in(x, seq_len, layer, row_fn, mod, nw, w_in, w_dt, dt_bias, w_conv, b_conv):
    M = x.shape[0]
    tm = TOKEN_TILE
    HL = SSD_HALO
    tps = seq_len // tm
    n_hblk = M // HL
    row = lambda i: (i, 0)
    items = [(pl.BlockSpec((tm, D), row), x)]
    if tps > 1:
        items += [(pl.BlockSpec((HL, D), lambda i: (jnp.maximum(i * (tm // HL) - 1, 0), 0)), x),
                  (pl.BlockSpec((HL, D), lambda i: (jnp.minimum((i + 1) * (tm // HL), n_hblk - 1), 0)), x)]
    items += [(_const_spec((1, D)), nw), (_mod_spec(layer, lambda i: row_fn(i * tm)), mod),
              _W(w_in, SSD_INNER, 0), _W(w_in, SSD_XBC, SSD_INNER),
              (_const_spec((D, LANES)), w_dt), (_const_spec((1, LANES)), dt_bias),
              (_const_spec((SSD_CONV + 1, SSD_XBC)), w_conv), (_const_spec((1, SSD_XBC)), b_conv)]
    in_specs, args, groups = _operands(items)
    return pl.pallas_call(
        _grouped(functools.partial(_ssd_in_kernel, tps), groups),
        grid=(M // tm,),
        in_specs=in_specs,
        out_specs=[pl.BlockSpec((tm, SSD_INNER), row), pl.BlockSpec((tm, SSD_XBC), row), pl.BlockSpec((tm, LANES), row)],
        out_shape=[jax.ShapeDtypeStruct((M, SSD_INNER), F32), jax.ShapeDtypeStruct((M, SSD_XBC), F32),
                   jax.ShapeDtypeStruct((M, LANES), F32)],
        scratch_shapes=[pltpu.VMEM((SSD_XBC // LANES, tm + 2 * HL, LANES), F32)],
        compiler_params=_cparams(1),
        name="ssd_in",
    )(*args)


def _cumsum_rows(a, reverse):
    n = a.shape[0]
    row = lax.broadcasted_iota(jnp.int32, a.shape, 0)
    k = 1
    while k < n:
        if reverse:
            a = a + jnp.where(row < n - k, pltpu.roll(a, n - k, 0), 0.0)
        else:
            a = a + jnp.where(row >= k, pltpu.roll(a, k, 0), 0.0)
        k *= 2
    return a


def _ssd_core_kernel(nc, has_h0, emit_state, *refs):
    refs = list(refs)
    dir_refs = [refs[0:2], refs[2:4]]
    alog_ref, dsk_ref = refs[4:6]
    refs = refs[6:]
    if has_h0:
        h0_ref = refs.pop(0)
    y_refs = [refs.pop(0), refs.pop(0)]
    if emit_state:
        so_ref = refs.pop(0)
    (st_ref,) = refs
    c = pl.program_id(1)
    n_blk = SSD_INNER // LANES

    @pl.when(c == 0)
    def _init():
        if has_h0:
            for dr in range(2):
                for j in range(n_blk):
                    st_ref[dr, :, j * LANES:(j + 1) * LANES] = h0_ref[dr, j * LANES:(j + 1) * LANES, :].T
        else:
            st_ref[...] = jnp.zeros(st_ref.shape, F32)

    A = -jnp.exp(alog_ref[...])
    jobs = []
    for dr in range(2):
        x_ref, dt_ref = dir_refs[dr]
        y_ref = y_refs[dr]

        def x_at(c0, x_ref=x_ref):
            return x_ref[:, c0:c0 + LANES]

        def y_put(c0, y, y_ref=y_ref):
            y_ref[:, c0:c0 + LANES] = y

        jobs.append((dr, x_at, dt_ref[...], y_put, None))
    _ssd_chunks(jobs, A, st_ref, dsk_ref)

    if emit_state:
        @pl.when(c == nc - 1)
        def _emit():
            _ssd_emit_state(st_ref, so_ref)


def _ssd_emit_state(st_ref, so_ref):
    for dr in range(2):
        for j in range(SSD_INNER // LANES):
            so_ref[dr, j * LANES:(j + 1) * LANES, :] = st_ref[dr, :, j * LANES:(j + 1) * LANES].T


def _ssd_group_terms(x_at):
    N = SSD_STATE
    terms = []
    for g in range(SSD_GROUPS):
        bg_t = x_at(SSD_INNER + g * N).T
        cg = x_at(SSD_INNER + SSD_GN + g * N).astype(BF16)
        terms.append((bg_t, cg, jnp.dot(cg, bg_t.astype(BF16), preferred_element_type=F32)))
    return terms


def _ssd_chunks(jobs, A, st_ref, dsk_ref):
    Q, N, P = SSD_CHUNK, SSD_STATE, SSD_HEAD_DIM
    lo = _head_masks()
    ri = lax.broadcasted_iota(jnp.int32, (Q, Q), 0)
    ci = lax.broadcasted_iota(jnp.int32, (Q, Q), 1)
    prepared = []
    for dr, x_at, dt, y_put, group_terms in jobs:
        cum = _cumsum_rows(dt * A, reverse=(dr == 1)) * LOG2E
        prepared.append(dict(
            dr=dr, x_at=x_at, y_put=y_put, cum=cum, cum_t=cum.T, dt_t=dt.T,
            tot_row=Q - 1 if dr == 0 else 0, mask=(ri >= ci) if dr == 0 else (ri <= ci),
            terms=_ssd_group_terms(x_at) if group_terms is None else group_terms))
    for g in range(SSD_GROUPS):
        staged = [_ssd_pair_products(job, g, pp, lo, st_ref)
                  for pp in range(SSD_HEADS // SSD_GROUPS // 2) for job in prepared]
        for job, c0, xpair, cum_x, yd, y_off, sd, st in staged:
            y = yd + y_off * jnp.exp2(cum_x)
            if job["dr"] == 0:
                y = y + xpair * dsk_ref[:, c0:c0 + LANES]
            job["y_put"](c0, y)
            tot = cum_x[job["tot_row"]:job["tot_row"] + 1, :]
            st_ref[job["dr"], :, c0:c0 + LANES] = st * jnp.exp2(tot) + sd


def _ssd_pair_products(job, g, pp, lo, st_ref):
    Q, N, P = SSD_CHUNK, SSD_STATE, SSD_HEAD_DIM
    dr, x_at, y_put, cum, cum_t, dt_t = (job[k] for k in ("dr", "x_at", "y_put", "cum", "cum_t", "dt_t"))
    tot_row, mask = job["tot_row"], job["mask"]
    bg_t, cg, cb = job["terms"][g]
    head = g * (SSD_HEADS // SSD_GROUPS) + 2 * pp
    c0 = head * P
    li = dr * SSD_HEADS + head
    xpair = x_at(c0)
    xpair_b = xpair.astype(BF16)
    cum_cols = [cum[:, li:li + 1], cum[:, li + 1:li + 2]]
    cum_x = jnp.where(lo, cum_cols[0], cum_cols[1])
    yd, sd = [], []
    for hh in range(2):
        dt_row = dt_t[li + hh:li + hh + 1, :]
        cum_row = cum_t[li + hh:li + hh + 1, :]
        decay = jnp.exp2(jnp.where(mask, cum_cols[hh] - cum_row, -jnp.inf))
        yd.append(jnp.dot((cb * decay * dt_row).astype(BF16), xpair_b, preferred_element_type=F32))
        to_end = jnp.exp2(cum_row[:, tot_row:tot_row + 1] - cum_row)
        sd.append(jnp.dot((bg_t * (dt_row * to_end)).astype(BF16), xpair_b, preferred_element_type=F32))
    st = st_ref[dr, :, c0:c0 + LANES]
    y_off = jnp.dot(cg, st.astype(BF16), preferred_element_type=F32)
    return job, c0, xpair, cum_x, jnp.where(lo, yd[0], yd[1]), y_off, jnp.where(lo, sd[0], sd[1]), st


def _ssd_seq_kernel(x_ref, nw_ref, mod_ref, wz_ref, wx_ref, wdt_ref, dtb_ref, wc_ref, bc_ref, alog_ref, dsk_ref, gnw_ref, wo_ref,
                    o_ref, so_ref, xpad_ref, xbc_ref, sz_ref, y_ref, yb_ref, st_ref):
    Q = SSD_CHUNK
    nc = x_ref.shape[0] // Q
    x = x_ref[...]
    hb = _modnorm(x, nw_ref[...], mod_ref[...]).astype(BF16)
    dt = _ssd_project(hb, None, wz_ref, wx_ref, wdt_ref, dtb_ref, wc_ref, bc_ref, xpad_ref, sz_ref, xbc_ref)
    st_ref[...] = jnp.zeros(st_ref.shape, F32)
    A = -jnp.exp(alog_ref[...])

    def x_at_chunk(ck):
        return lambda c0: xbc_ref[ck * Q:(ck + 1) * Q, c0:c0 + LANES]

    terms = [_ssd_group_terms(x_at_chunk(ck)) for ck in range(nc)]

    def y_put_to(ref, ck):
        def y_put(c0, y):
            ref[ck * Q:(ck + 1) * Q, c0:c0 + LANES] = y
        return y_put

    for k in range(nc):
        kb = nc - 1 - k
        _ssd_chunks([(0, x_at_chunk(k), dt[k * Q:(k + 1) * Q], y_put_to(y_ref, k), terms[k]),
                     (1, x_at_chunk(kb), dt[kb * Q:(kb + 1) * Q], y_put_to(yb_ref, kb), terms[kb])],
                    A, st_ref, dsk_ref)
    _ssd_emit_state(st_ref, so_ref)
    g = _rms((y_ref[...] + yb_ref[...]) * sz_ref[...], gnw_ref[...]).astype(BF16)
    o_ref[...] = x + mod_ref[:, 2 * D:3 * D] * _wdot(g, wo_ref, 0, D)


def _ssd_seq(x, seq_len, layer, row_fn, mod, nw, w_in, w_dt, dt_bias, w_conv, b_conv, a_log, d_skip, norm_w, w_out):
    M = x.shape[0]
    n_seq = M // seq_len
    HL = SSD_HALO
    row = lambda i: (i, 0)
    st_blk = pl.BlockSpec((None, 2, SSD_INNER, SSD_STATE), lambda i: (i, 0, 0, 0))
    in_specs, args, groups = _operands([
        (pl.BlockSpec((seq_len, D), row), x), (_const_spec((1, D)), nw),
        (_mod_spec(layer, lambda i: row_fn(i * seq_len)), mod),
        _W(w_in, SSD_INNER, 0), _W(w_in, SSD_XBC, SSD_INNER),
        (_const_spec((D, LANES)), w_dt), (_const_spec((1, LANES)), dt_bias),
        (_const_spec((SSD_CONV + 1, SSD_XBC)), w_conv), (_const_spec((1, SSD_XBC)), b_conv),
        (_const_spec((1, LANES)), a_log), (_const_spec((1, SSD_INNER)), d_skip),
        (_const_spec((1, SSD_INNER)), norm_w), _W(w_out)])
    return pl.pallas_call(
        _grouped(_ssd_seq_kernel, groups),
        grid=(n_seq,),
        in_specs=in_specs,
        out_specs=[pl.BlockSpec((seq_len, D), row), st_blk],
        out_shape=[jax.ShapeDtypeStruct((M, D), F32), jax.ShapeDtypeStruct((n_seq, 2, SSD_INNER, SSD_STATE), F32)],
        scratch_shapes=[pltpu.VMEM((SSD_XBC // LANES, seq_len + 2 * HL, LANES), F32),
                        pltpu.VMEM((seq_len, SSD_XBC), F32), pltpu.VMEM((seq_len, SSD_INNER), F32),
                        pltpu.VMEM((seq_len, SSD_INNER), F32), pltpu.VMEM((seq_len, SSD_INNER), F32),
                        pltpu.VMEM((2, SSD_STATE, SSD_INNER), F32)],
        compiler_params=_cparams(1),
        name="ssd_seq",
    )(*args)


def _ssd_core(xbc, dtp, a_log, d_skip, h0, n_seq, seq_len, emit_state):
    Q = SSD_CHUNK
    nc = seq_len // Q
    M = n_seq * seq_len

    def chunk(dr):
        return (lambda b, c: b * nc + c) if dr == 0 else (lambda b, c: b * nc + nc - 1 - c)

    in_specs, args = [], []
    for dr in range(2):
        ch = chunk(dr)
        in_specs += [pl.BlockSpec((Q, SSD_XBC), lambda b, c, ch=ch: (ch(b, c), 0)),
                     pl.BlockSpec((Q, LANES), lambda b, c, ch=ch: (ch(b, c), 0))]
        args += [xbc, dtp]
    in_specs += [_const_spec((1, LANES)), _const_spec((1, SSD_INNER))]
    args += [a_log, d_skip]
    st_blk = pl.BlockSpec((None, 2, SSD_INNER, SSD_STATE), lambda b, c: (b, 0, 0, 0))
    has_h0 = h0 is not None
    if has_h0:
        in_specs.append(st_blk)
        args.append(h0)
    out_specs = [pl.BlockSpec((Q, SSD_INNER), lambda b, c: (chunk(0)(b, c), 0)),
                 pl.BlockSpec((Q, SSD_INNER), lambda b, c: (chunk(1)(b, c), 0))]
    out_shape = [jax.ShapeDtypeStruct((M, SSD_INNER), F32)] * 2
    if emit_state:
        out_specs.append(st_blk)
        out_shape.append(jax.ShapeDtypeStruct((n_seq, 2, SSD_INNER, SSD_STATE), F32))
    return pl.pallas_call(
        functools.partial(_ssd_core_kernel, nc, has_h0, emit_state),
        grid=(n_seq, nc),
        in_specs=in_specs,
        out_specs=out_specs,
        out_shape=out_shape,
        scratch_shapes=[pltpu.VMEM((2, SSD_STATE, SSD_INNER), F32)],
        compiler_params=_cparams(2),
        name="ssd_core",
    )(*args)


def _ssd_out_kernel(yf_ref, yb_ref, sz_ref, x_ref, mod_ref, nw_ref, w_ref, o_ref):
    g = _rms((yf_ref[...] + yb_ref[...]) * sz_ref[...], nw_ref[...]).astype(BF16)
    o_ref[...] = x_ref[...] + mod_ref[:, 2 * D:3 * D] * _wdot(g, w_ref, 0, D)


def _ssd_out(yf, yb, sz, x, layer, row_fn, mod, norm_w, w_out):
    M = x.shape[0]
    tm = TOKEN_TILE
    row = lambda i: (i, 0)
    wide = pl.BlockSpec((tm, SSD_INNER), row)
    in_specs, args, groups = _operands([
        (wide, yf), (wide, yb), (wide, sz), (pl.BlockSpec((tm, D), row), x),
        (_mod_spec(layer, lambda i: row_fn(i * tm)), mod), (_const_spec((1, SSD_INNER)), norm_w), _W(w_out)])
    return pl.pallas_call(
        _grouped(_ssd_out_kernel, groups),
        grid=(M // tm,),
        in_specs=in_specs,
        out_specs=pl.BlockSpec((tm, D), row),
        out_shape=jax.ShapeDtypeStruct((M, D), F32),
        compiler_params=_cparams(1),
        name="ssd_out",
    )(*args)


def _ssd_layer(xc, xl, layer, mod, nw, p, n_lat):
    j = layer // 3
    w_in = p["ssd_w_in"][j].astype(BF16)
    w_dt = jnp.pad(w_in[:, SSD_IN_COLS:], ((0, 0), (0, LANES - 2 * SSD_HEADS)))
    dt_bias = jnp.pad(p["ssd_dt_bias"][j].reshape(1, -1), ((0, 0), (0, LANES - 2 * SSD_HEADS)))
    a_log = jnp.pad(p["ssd_a_log"][j].reshape(1, -1), ((0, 0), (0, LANES - 2 * SSD_HEADS)))
    w_conv = jnp.pad(p["ssd_w_conv"][j], ((0, 1), (0, 0)))
    b_conv = p["ssd_b_conv"][j][None]
    d_skip = jnp.repeat(p["ssd_d"][j], SSD_HEAD_DIM)[None]
    norm_w = p["ssd_norm_w"][j][None]
    w_out = p["ssd_w_out"][j].astype(BF16)
    row_ctx = lambda tok: 0
    row_lat = lambda tok: 1 + tok // DEC_SEQ
    oc, s_new = _ssd_seq(xc, SEQ, layer, row_ctx, mod, nw, w_in, w_dt, dt_bias, w_conv, b_conv, a_log, d_skip, norm_w, w_out)
    szl, xbc_l, dt_l = _ssd_in(xl, DEC_SEQ, layer, row_lat, mod, nw, w_in, w_dt, dt_bias, w_conv, b_conv)
    h0 = p["state_ssm"][:n_lat, j].reshape(n_lat, 2, SSD_INNER, SSD_STATE)
    yfl, ybl = _ssd_core(xbc_l, dt_l, a_log, d_skip, h0, n_lat, DEC_SEQ, False)
    ol = _ssd_out(yfl, ybl, szl, xl, layer, row_lat, mod, norm_w, w_out)
    return oc, ol, s_new


def kernel(x_prompt, x_sample, cache_k, cache_v, state_ssm, c, c_ctx, ada_w, ada_b, norm_w, final_norm_w, conv_w_in, conv_b_in, conv_w_dw, conv_b_dw, conv_ln_w, conv_ln_b, conv_w_out, conv_b_out, na_w_in, na_rpb, na_w_out, ssd_w_in, ssd_w_conv, ssd_b_conv, ssd_a_log, ssd_dt_bias, ssd_d, ssd_norm_w, ssd_w_out):
    p = dict(cache_k=cache_k, cache_v=cache_v, state_ssm=state_ssm,
             conv_w_in=conv_w_in, conv_b_in=conv_b_in, conv_w_dw=conv_w_dw, conv_b_dw=conv_b_dw,
             conv_ln_w=conv_ln_w, conv_ln_b=conv_ln_b, conv_w_out=conv_w_out, conv_b_out=conv_b_out,
             na_w_in=na_w_in, na_rpb=na_rpb, na_w_out=na_w_out,
             ssd_w_in=ssd_w_in, ssd_w_conv=ssd_w_conv, ssd_b_conv=ssd_b_conv, ssd_a_log=ssd_a_log,
             ssd_dt_bias=ssd_dt_bias, ssd_d=ssd_d, ssd_norm_w=ssd_norm_w, ssd_w_out=ssd_w_out)
    n_ctx, n_lat = x_prompt.shape[0], x_sample.shape[0]
    assert x_prompt.shape[1:] == (SEQ, D) and x_sample.shape[1:] == (DEC_SEQ, D)
    assert (DEPTH - 1) % 3 == 0, "the final RMSNorm is fused into a trailing Conformer layer"
    xc = x_prompt.reshape(n_ctx * SEQ, D)
    xl = x_sample.reshape(n_lat * DEC_SEQ, D)
    cond8 = jnp.concatenate([c_ctx[None], c, jnp.zeros((8 - 1 - n_lat, D), F32)], axis=0)
    mod = _modulation(cond8, ada_w, ada_b).reshape(DEPTH, 8, 1, 3 * D)
    row_ctx = lambda tok: 0
    row_lat = lambda tok: 1 + tok // DEC_SEQ
    new_k, new_v, new_s = [], [], []
    for i in range(DEPTH):
        kind, j = i % 3, i // 3
        nw = norm_w[i][None]
        if kind == 0:
            cp = _prep_conv(p, j)
            fw = final_norm_w[None] if i == DEPTH - 1 else None
            xc = _conv_layer(xc, SEQ, i, row_ctx, mod, nw, *cp, fw)
            xl = _conv_layer(xl, DEC_SEQ, i, row_lat, mod, nw, *cp, fw)
        elif kind == 1:
            xc, xl, k_c, v_c = _na_layer(xc, xl, i, mod, nw, p, n_lat)
            new_k.append(k_c.reshape(n_ctx, SEQ, NA_HEADS, NA_HEAD_DIM))
            new_v.append(v_c.reshape(n_ctx, SEQ, NA_HEADS, NA_HEAD_DIM))
        else:
            xc, xl, s_c = _ssd_layer(xc, xl, i, mod, nw, p, n_lat)
            new_s.append(s_c.reshape(n_ctx, 2, SSD_HEADS, SSD_HEAD_DIM, SSD_STATE))
    return (xc.reshape(n_ctx, SEQ, D), xl.reshape(n_lat, DEC_SEQ, D),
            jnp.stack(new_k, axis=1), jnp.stack(new_v, axis=1), jnp.stack(new_s, axis=1))
```

```python
import functools

import jax
import jax.numpy as jnp
from jax import lax
from jax.experimental import pallas as pl
from jax.experimental.pallas import tpu as pltpu

F32 = jnp.float32
BF16 = jnp.bfloat16

D = 1024
DEPTH = 4
SEQ = 256
DEC_SEQ = 2048
GRID_W = 64
RMS_EPS = 1e-6
LN_EPS = 1e-5
LOG2E = 1.4426950408889634
CONF_K = 31
NA_HEADS = 16
NA_HEAD_DIM = 64
NA_KH = 8
NA_KW = 16
NEG_INF = -1e30
SSD_INNER = 2048
SSD_HEADS = 32
SSD_HEAD_DIM = 64
SSD_STATE = 128
SSD_GROUPS = 4
SSD_GN = SSD_GROUPS * SSD_STATE
SSD_CONV = 7
SSD_CHUNK = 128
SSD_XBC = SSD_INNER + 2 * SSD_GN

LANES = 128
TOKEN_TILE = 256
W_CHUNK = 512
CONV_HALO = 16
SSD_HALO = 8
VMEM_LIMIT = 56 * 1024 * 1024


def _cparams(n_axes):
    return pltpu.CompilerParams(dimension_semantics=("arbitrary",) * n_axes,
                                vmem_limit_bytes=VMEM_LIMIT)


def _const_spec(shape):
    nd = len(shape)
    return pl.BlockSpec(shape, lambda *_: (0,) * nd, pipeline_mode=pl.Buffered(1))


class _W:
    def __init__(self, w, ncols=None, col0=0):
        self.w, self.col0 = w, col0
        self.ncols = w.shape[1] if ncols is None else ncols


def _operands(items):
    specs, args, groups = [], [], []
    for it in items:
        if isinstance(it, _W):
            n, b0, K = it.ncols // W_CHUNK, it.col0 // W_CHUNK, it.w.shape[0]
            for c in range(b0, b0 + n):
                specs.append(pl.BlockSpec((K, W_CHUNK), lambda *_, c=c: (0, c), pipeline_mode=pl.Buffered(1)))
                args.append(it.w)
            groups.append(n)
        else:
            specs.append(it[0])
            args.append(it[1])
            groups.append(None)
    return specs, args, groups


def _grouped(body, groups):
    def kernel_fn(*refs):
        refs = list(refs)
        packed = []
        for g in groups:
            if g is None:
                packed.append(refs.pop(0))
            else:
                packed.append(tuple(refs[:g]))
                del refs[:g]
        return body(*packed, *refs)
    return kernel_fn


def _wdot(h, w_refs, c0, n):
    outs = []
    c = c0
    while c < c0 + n:
        j, off = divmod(c, W_CHUNK)
        m = min(W_CHUNK - off, c0 + n - c)
        outs.append(jnp.dot(h, w_refs[j][:, off:off + m], preferred_element_type=F32))
        c += m
    return outs[0] if len(outs) == 1 else jnp.concatenate(outs, axis=1)


def _sigmoid(x):
    return 1.0 / (1.0 + jnp.exp(-x))


def _silu(x):
    return x * _sigmoid(x)


def _rms(x, w):
    ms = jnp.mean(x * x, axis=-1, keepdims=True)
    return x * lax.rsqrt(ms + RMS_EPS) * w


def _modnorm(x, nw, mod):
    return _rms(x, nw) * (1.0 + mod[:, D:2 * D]) + mod[:, :D]


def _mod_spec(layer, row_fn):
    return pl.BlockSpec((None, None, 1, 3 * D), lambda i, *_: (layer, row_fn(i), 0, 0))


def _mod_kernel(cond_ref, w_ref, b_ref, o_ref):
    s = _silu(cond_ref[...]).astype(BF16)
    o_ref[...] = jnp.dot(s, w_ref[...].astype(BF16), preferred_element_type=F32) + b_ref[...]


def _modulation(cond8, ada_w, ada_b):
    tn = 1024
    return pl.pallas_call(
        _mod_kernel,
        grid=(DEPTH, 3 * D // tn),
        in_specs=[pl.BlockSpec((8, D), lambda i, j: (0, 0)),
                  pl.BlockSpec((None, D, tn), lambda i, j: (i, 0, j)),
                  pl.BlockSpec((None, 1, tn), lambda i, j: (i, 0, j))],
        out_specs=pl.BlockSpec((None, 8, tn), lambda i, j: (i, 0, j)),
        out_shape=jax.ShapeDtypeStruct((DEPTH, 8, 3 * D), F32),
        compiler_params=_cparams(2),
        name="adaln_mod",
    )(cond8, ada_w, ada_b.reshape(DEPTH, 1, 3 * D))


CONV_CHUNK = 256


def _conv_kernel(tiles_per_seq, final, *refs):
    refs = list(refs)
    x_ref = refs.pop(0)
    if tiles_per_seq > 1:
        xp_ref = refs.pop(0)
        xn_ref = refs.pop(0)
    nw_ref, mod_ref, wi_ref, bi_ref, wdw_ref, bdw_ref, lnw_ref, lnb_ref, wo_ref, bo_ref = refs[:10]
    refs = refs[10:]
    if final:
        fw_ref = refs.pop(0)
    o_ref, ubuf, cbuf = refs
    T = x_ref.shape[0]
    H = CONV_HALO
    RB = 128

    hb = _modnorm(x_ref[...], nw_ref[...], mod_ref[...]).astype(BF16)
    if tiles_per_seq > 1:
        t = pl.program_id(0) % tiles_per_seq
        hh = _modnorm(jnp.concatenate([xp_ref[...], xn_ref[...]], axis=0), nw_ref[...], mod_ref[...]).astype(BF16)

    def glu(h, cols):
        v = _wdot(h, wi_ref, cols.start, CONV_CHUNK) + bi_ref[:, cols]
        gcols = slice(D + cols.start, D + cols.stop)
        g = _wdot(h, wi_ref, gcols.start, CONV_CHUNK) + bi_ref[:, gcols]
        return v * _sigmoid(g)

    for ch in range(D // CONV_CHUNK):
        cols = slice(ch * CONV_CHUNK, (ch + 1) * CONV_CHUNK)
        u_main = glu(hb, cols)
        if tiles_per_seq > 1:
            uh = glu(hh, cols)
            u_prev = jnp.where(t > 0, uh[0:H], 0.0)
            u_next = jnp.where(t < tiles_per_seq - 1, uh[H:2 * H], 0.0)
        else:
            u_prev = u_next = jnp.zeros((H, CONV_CHUNK), F32)
        for cb in range(CONV_CHUNK // LANES):
            blk = ch * (CONV_CHUNK // LANES) + cb
            lc = slice(blk * LANES, (blk + 1) * LANES)
            cc = slice(cb * LANES, (cb + 1) * LANES)
            ubuf[blk, 0:H, :] = u_prev[:, cc]
            ubuf[blk, H:H + T, :] = u_main[:, cc]
            ubuf[blk, H + T:H + T + H, :] = u_next[:, cc]
            for rb in range(T // RB):
                acc = jnp.zeros((RB, LANES), F32)
                for k in range(CONF_K):
                    r0 = rb * RB + H - CONF_K // 2 + k
                    acc = acc + ubuf[blk, r0:r0 + RB, :] * wdw_ref[k:k + 1, lc]
                cbuf[rb * RB:(rb + 1) * RB, lc] = acc + bdw_ref[:, lc]

    sz = _silu(_wdot(hb, wi_ref, 2 * D, D) + bi_ref[:, 2 * D:3 * D])
    c = cbuf[...]
    mu = jnp.mean(c, axis=-1, keepdims=True)
    xc = c - mu
    var = jnp.mean(xc * xc, axis=-1, keepdims=True)
    ln = xc * lax.rsqrt(var + LN_EPS) * lnw_ref[...] + lnb_ref[...]
    g = (_silu(ln) * sz).astype(BF16)
    y = _wdot(g, wo_ref, 0, D) + bo_ref[...]
    xn = x_ref[...] + mod_ref[:, 2 * D:3 * D] * y
    if final:
        xn = _rms(xn, fw_ref[...])
    o_ref[...] = xn


def _conv_layer(x, seq_len, layer, row_fn, mod, nw, w_in, b_in, w_dw, b_dw, ln_w, ln_b, w_out, b_out, final_w):
    M = x.shape[0]
    T = TOKEN_TILE
    tps = seq_len // T
    H = CONV_HALO
    n_halo_blocks = M // H
    row = lambda i: (i, 0)
    items = [(pl.BlockSpec((T, D), row), x)]
    if tps > 1:
        items += [(pl.BlockSpec((H, D), lambda i: (jnp.maximum(i * (T // H) - 1, 0), 0)), x),
                  (pl.BlockSpec((H, D), lambda i: (jnp.minimum((i + 1) * (T // H), n_halo_blocks - 1), 0)), x)]
    items += [(_const_spec((1, D)), nw), (_mod_spec(layer, lambda i: row_fn(i * T)), mod),
              _W(w_in), (_const_spec((1, 3 * D)), b_in),
              (_const_spec((CONF_K + 1, D)), w_dw), (_const_spec((1, D)), b_dw),
              (_const_spec((1, D)), ln_w), (_const_spec((1, D)), ln_b),
              _W(w_out), (_const_spec((1, D)), b_out)]
    final = final_w is not None
    if final:
        items.append((_const_spec((1, D)), final_w))
    in_specs, args, groups = _operands(items)
    return pl.pallas_call(
        _grouped(functools.partial(_conv_kernel, tps, final), groups),
        grid=(M // T,),
        in_specs=in_specs,
        out_specs=pl.BlockSpec((T, D), row),
        out_shape=jax.ShapeDtypeStruct((M, D), F32),
        scratch_shapes=[pltpu.VMEM((D // LANES, T + 2 * H, LANES), F32), pltpu.VMEM((T, D), F32)],
        compiler_params=_cparams(1),
        name="conv_layer",
    )(*args)


def _prep_conv(p, j):
    w_dw = jnp.pad(p["conv_w_dw"][j], ((0, 1), (0, 0)))
    return (p["conv_w_in"][j].astype(BF16), p["conv_b_in"][j][None], w_dw, p["conv_b_dw"][j][None],
            p["conv_ln_w"][j][None], p["conv_ln_b"][j][None], p["conv_w_out"][j].astype(BF16), p["conv_b_out"][j][None])


def _out_proj_kernel(g_ref, x_ref, mod_ref, w_ref, o_ref):
    y = _wdot(g_ref[...].astype(BF16), w_ref, 0, D)
    o_ref[...] = x_ref[...] + mod_ref[:, 2 * D:3 * D] * y


def _out_proj(g, x, layer, row_fn, mod, w_out):
    M, K = g.shape
    tm = TOKEN_TILE
    row = lambda i: (i, 0)
    in_specs, args, groups = _operands([
        (pl.BlockSpec((tm, K), row), g), (pl.BlockSpec((tm, D), row), x),
        (_mod_spec(layer, lambda i: row_fn(i * tm)), mod), _W(w_out)])
    return pl.pallas_call(
        _grouped(_out_proj_kernel, groups),
        grid=(M // tm,),
        in_specs=in_specs,
        out_specs=pl.BlockSpec((tm, D), row),
        out_shape=jax.ShapeDtypeStruct((M, D), F32),
        compiler_params=_cparams(1),
        name="out_proj",
    )(*args)


NA_SCALE = NA_HEAD_DIM ** -0.5
NA_ROWS = 32
NA_WIN = NA_KH * GRID_W
NT_DIMS = (((1,), (1,)), ((), ()))
NA_STAGE_PAIRS = 4


def _na_in_kernel(x_ref, nw_ref, mod_ref, w_ref, q_ref, k_ref, v_ref, sz_ref):
    hb = _modnorm(x_ref[...], nw_ref[...], mod_ref[...]).astype(BF16)
    q_ref[...] = _wdot(hb, w_ref, 0, D).astype(q_ref.dtype)
    k_ref[...] = _wdot(hb, w_ref, D, D).astype(k_ref.dtype)
    v_ref[...] = _wdot(hb, w_ref, 2 * D, D).astype(v_ref.dtype)
    sz_ref[...] = _silu(_wdot(hb, w_ref, 3 * D, D))


def _na_in(x, layer, row_fn, mod, nw, w_in, kv_dtype):
    M = x.shape[0]
    tm = TOKEN_TILE
    row = lambda i: (i, 0)
    blk = pl.BlockSpec((tm, D), row)
    in_specs, args, groups = _operands([
        (blk, x), (_const_spec((1, D)), nw), (_mod_spec(layer, lambda i: row_fn(i * tm)), mod), _W(w_in)])
    return pl.pallas_call(
        _grouped(_na_in_kernel, groups),
        grid=(M // tm,),
        in_specs=in_specs,
        out_specs=[blk, blk, blk, blk],
        out_shape=[jax.ShapeDtypeStruct((M, D), BF16), jax.ShapeDtypeStruct((M, D), kv_dtype),
                   jax.ShapeDtypeStruct((M, D), kv_dtype), jax.ShapeDtypeStruct((M, D), F32)],
        compiler_params=_cparams(1),
        name="na_in",
    )(*args)


def _head_masks():
    lane = lax.broadcasted_iota(jnp.int32, (1, LANES), 1)
    return lane < NA_HEAD_DIM


def _na_seq_kernel(x_ref, nw_ref, mod_ref, wi_ref, wo_ref, o_ref, k_ref, v_ref, g_ref):
    n = x_ref.shape[0]
    lo = _head_masks()
    x = x_ref[...]
    hb = _modnorm(x, nw_ref[...], mod_ref[...]).astype(BF16)
    hpc = CONV_CHUNK // NA_HEAD_DIM
    staged = []
    n_chunks = D // CONV_CHUNK
    chunks_per_stage = NA_STAGE_PAIRS * LANES // CONV_CHUNK
    for ch in range(n_chunks):
        cols = slice(ch * CONV_CHUNK, (ch + 1) * CONV_CHUNK)
        q = _wdot(hb, wi_ref, cols.start, CONV_CHUNK).astype(BF16)
        k = _wdot(hb, wi_ref, D + cols.start, CONV_CHUNK)
        v = _wdot(hb, wi_ref, 2 * D + cols.start, CONV_CHUNK)
        sz = _silu(_wdot(hb, wi_ref, 3 * D + cols.start, CONV_CHUNK))
        k_ref[:, ch * hpc:(ch + 1) * hpc, :] = k.reshape(n, hpc, NA_HEAD_DIM)
        v_ref[:, ch * hpc:(ch + 1) * hpc, :] = v.reshape(n, hpc, NA_HEAD_DIM)
        kb = k.astype(BF16)
        vb = v.astype(BF16)
        for pp in range(CONV_CHUNK // LANES):
            pc = slice(pp * LANES, (pp + 1) * LANES)
            qp = q[:, pc]
            zero = jnp.zeros_like(qp)
            qs = jnp.concatenate([jnp.where(lo, qp, zero), jnp.where(lo, zero, qp)], axis=0)
            s = lax.dot_general(qs, kb[:, pc], NT_DIMS, preferred_element_type=F32) * NA_SCALE
            staged.append((s, vb[:, pc], sz[:, pc], cols.start + pp * LANES))
        if (ch + 1) % chunks_per_stage == 0 or ch == n_chunks - 1:
            for s, vp, szp, c0 in staged:
                e = jnp.exp(s - jnp.max(s, axis=-1, keepdims=True))
                l = jnp.sum(e, axis=-1, keepdims=True)
                o = jnp.dot(e.astype(BF16), vp, preferred_element_type=F32) / l
                g_ref[:, c0:c0 + LANES] = (jnp.where(lo, o[:n], o[n:]) * szp).astype(BF16)
            staged = []
    y = _wdot(g_ref[...], wo_ref, 0, D)
    o_ref[...] = x + mod_ref[:, 2 * D:3 * D] * y


def _na_seq(x, seq_len, layer, row_fn, mod, nw, w_in, w_out):
    M = x.shape[0]
    row = lambda i: (i, 0)
    kv_blk = pl.BlockSpec((seq_len, NA_HEADS, NA_HEAD_DIM), lambda i: (i, 0, 0))
    kv_shape = jax.ShapeDtypeStruct((M, NA_HEADS, NA_HEAD_DIM), F32)
    in_specs, args, groups = _operands([
        (pl.BlockSpec((seq_len, D), row), x), (_const_spec((1, D)), nw),
        (_mod_spec(layer, lambda i: row_fn(i * seq_len)), mod), _W(w_in), _W(w_out)])
    return pl.pallas_call(
        _grouped(_na_seq_kernel, groups),
        grid=(M // seq_len,),
        in_specs=in_specs,
        out_specs=[pl.BlockSpec((seq_len, D), row), kv_blk, kv_blk],
        out_shape=[jax.ShapeDtypeStruct((M, D), F32), kv_shape, kv_shape],
        scratch_shapes=[pltpu.VMEM((seq_len, D), BF16)],
        compiler_params=_cparams(1),
        name="na_seq",
    )(*args)


def _rpb_kernel(rpb_ref, o_ref):
    h = pl.program_id(0)
    c = lax.broadcasted_iota(jnp.int32, (GRID_W, LANES), 0)
    lane = lax.broadcasted_iota(jnp.int32, (GRID_W, LANES), 1)
    kc = lane & (GRID_W - 1)
    upper = lax.broadcasted_iota(jnp.int32, (1, LANES), 1) >= GRID_W
    cs = jnp.clip(c - NA_KW // 2, 0, GRID_W - NA_KW)
    valid = (kc >= cs) & (kc < cs + NA_KW)
    n_pair = 2 * NA_KH - 2

    def toeplitz(dri, lane0):
        row = rpb_ref[pl.ds(h * (2 * NA_KH - 1) + dri, 1), :]
        rows = jnp.broadcast_to(row, (GRID_W, LANES))
        return pltpu.roll(rows, (LANES - (NA_KW - 1) + lane0) % LANES, 1, stride=1, stride_axis=0)

    for dri in range(n_pair):
        tile = jnp.where(upper, toeplitz(dri + 1, GRID_W), toeplitz(dri, 0))
        tile = jnp.where(valid, tile, NEG_INF)
        for idx in range(NA_KH):
            jj2 = dri - (NA_KH - 1) + idx
            if jj2 % 2 == 0 and 0 <= jj2 // 2 < NA_KH // 2:
                jj = jj2 // 2
                o_ref[idx, :, jj * LANES:(jj + 1) * LANES] = tile


def _rpb_table(rpb):
    n_rows = NA_HEADS * (2 * NA_KH - 1)
    rows = jnp.pad(rpb.reshape(n_rows, 2 * NA_KW - 1), ((0, 0), (0, LANES - (2 * NA_KW - 1))))
    return pl.pallas_call(
        _rpb_kernel,
        grid=(NA_HEADS,),
        in_specs=[_const_spec((n_rows, LANES))],
        out_specs=pl.BlockSpec((NA_KH, None, GRID_W, NA_WIN), lambda h: (0, h, 0, 0)),
        out_shape=jax.ShapeDtypeStruct((NA_KH, NA_HEADS, GRID_W, NA_WIN), F32),
        compiler_params=_cparams(1),
        name="na_rpb_table",
    )(rows)


def _na_row_start(r):
    return jnp.clip(r - NA_KH // 2, 0, NA_ROWS - NA_KH)


def _na_lat_kernel(q_ref, k_ref, v_ref, ckt_ref, cv_ref, bias_ref, sz_ref, o_ref):
    lo = _head_masks()
    k0 = pl.multiple_of(_na_row_start(pl.program_id(1)) * GRID_W, GRID_W)
    n_pairs = NA_HEADS // 2
    for p0 in range(0, n_pairs, NA_STAGE_PAIRS):
        scores = []
        for p in range(p0, p0 + NA_STAGE_PAIRS):
            cols = slice(p * LANES, (p + 1) * LANES)
            qp = q_ref[:, cols]
            kw = k_ref[pl.ds(k0, NA_WIN), cols]
            ck_t = ckt_ref[cols, :]
            zero = jnp.zeros_like(qp)
            qs = jnp.concatenate([jnp.where(lo, qp, zero), jnp.where(lo, zero, qp)], axis=0)
            bias = bias_ref[2 * p:2 * p + 2].reshape(2 * GRID_W, NA_WIN)
            scores.append((lax.dot_general(qs, kw, NT_DIMS, preferred_element_type=F32) * NA_SCALE + bias,
                           jnp.dot(qs, ck_t, preferred_element_type=F32) * NA_SCALE))
        for p, (s_loc, s_ctx) in zip(range(p0, p0 + NA_STAGE_PAIRS), scores):
            cols = slice(p * LANES, (p + 1) * LANES)
            vw = v_ref[pl.ds(k0, NA_WIN), cols]
            cv = cv_ref[:, cols]
            m = jnp.maximum(jnp.max(s_loc, axis=-1, keepdims=True), jnp.max(s_ctx, axis=-1, keepdims=True))
            e_loc = jnp.exp(s_loc - m)
            e_ctx = jnp.exp(s_ctx - m)
            l = jnp.sum(e_loc, axis=-1, keepdims=True) + jnp.sum(e_ctx, axis=-1, keepdims=True)
            o = (jnp.dot(e_loc.astype(BF16), vw, preferred_element_type=F32)
                 + jnp.dot(e_ctx.astype(BF16), cv, preferred_element_type=F32)) / l
            o_ref[:, cols] = jnp.where(lo, o[:GRID_W], o[GRID_W:]) * sz_ref[:, cols]


def _na_lat(q, k, v, ck_t, cv, bias, sz, n_lat):
    T = NA_ROWS * GRID_W
    past = cv.shape[1]
    qblk = pl.BlockSpec((GRID_W, D), lambda b, r: (b * NA_ROWS + r, 0))
    kvblk = pl.BlockSpec((T, D), lambda b, r: (b, 0))
    cblk = pl.BlockSpec((None, past, D), lambda b, r: (b, 0, 0))
    ctblk = pl.BlockSpec((None, D, past), lambda b, r: (b, 0, 0))
    bblk = pl.BlockSpec((None, NA_HEADS, GRID_W, NA_WIN), lambda b, r: (r - _na_row_start(r), 0, 0, 0))
    return pl.pallas_call(
        _na_lat_kernel,
        grid=(n_lat, NA_ROWS),
        in_specs=[qblk, kvblk, kvblk, ctblk, cblk, bblk, qblk],
        out_specs=qblk,
        out_shape=jax.ShapeDtypeStruct((n_lat * T, D), F32),
        compiler_params=_cparams(2),
        name="na_lat_attn",
    )(q, k, v, ck_t, cv, bias, sz)


def _na_layer(xc, xl, layer, mod, nw, p, n_lat):
    j = layer // 3
    w_in = p["na_w_in"][j].astype(BF16)
    w_out = p["na_w_out"][j].astype(BF16)
    row_ctx = lambda tok: 0
    row_lat = lambda tok: 1 + tok // (NA_ROWS * GRID_W)
    oc, kc, vc = _na_seq(xc, SEQ, layer, row_ctx, mod, nw, w_in, w_out)
    ql, kl, vl, szl = _na_in(xl, layer, row_lat, mod, nw, w_in, BF16)
    past = p["cache_k"].shape[2]
    ck_t = jnp.swapaxes(p["cache_k"][:n_lat, j].reshape(n_lat, past, D), 1, 2).astype(BF16)
    cv = p["cache_v"][:n_lat, j].reshape(n_lat, past, D).astype(BF16)
    gl = _na_lat(ql, kl, vl, ck_t, cv, _rpb_table(p["na_rpb"][j]), szl, n_lat)
    return oc, _out_proj(gl, xl, layer, row_lat, mod, w_out), kc, vc


SSD_IN_COLS = SSD_INNER + SSD_XBC


def _softplus(x):
    return jnp.maximum(x, 0.0) + jnp.log(1.0 + jnp.exp(-jnp.abs(x)))


def _ssd_in_kernel(tiles_per_seq, *refs):
    refs = list(refs)
    x_ref = refs.pop(0)
    if tiles_per_seq > 1:
        xp_ref = refs.pop(0)
        xn_ref = refs.pop(0)
    nw_ref, mod_ref, wz_ref, wx_ref, wdt_ref, dtb_ref, wc_ref, bc_ref, sz_ref, xbc_ref, dt_ref, xpad_ref = refs
    hb = _modnorm(x_ref[...], nw_ref[...], mod_ref[...]).astype(BF16)
    halo = None
    if tiles_per_seq > 1:
        t = pl.program_id(0) % tiles_per_seq
        xh = jnp.concatenate([xp_ref[...], xn_ref[...]], axis=0)
        halo = (_modnorm(xh, nw_ref[...], mod_ref[...]).astype(BF16), t > 0, t < tiles_per_seq - 1)
    dt_ref[...] = _ssd_project(hb, halo, wz_ref, wx_ref, wdt_ref, dtb_ref, wc_ref, bc_ref, xpad_ref, sz_ref, xbc_ref)


def _ssd_project(hb, halo, wz_ref, wx_ref, wdt_ref, dtb_ref, wc_ref, bc_ref, xpad_ref, sz_ref, xbc_ref):
    tm = hb.shape[0]
    HL = SSD_HALO
    RB = 128
    for j in range(SSD_XBC // D):
        c0 = j * D
        raw = _wdot(hb, wx_ref, c0, D)
        if halo is not None:
            rawh = _wdot(halo[0], wx_ref, c0, D)
            raw_prev = jnp.where(halo[1], rawh[0:HL], 0.0)
            raw_next = jnp.where(halo[2], rawh[HL:2 * HL], 0.0)
        else:
            raw_prev = raw_next = jnp.zeros((HL, D), F32)
        for cb in range(D // LANES):
            blk = j * (D // LANES) + cb
            lc = slice(blk * LANES, (blk + 1) * LANES)
            cc = slice(cb * LANES, (cb + 1) * LANES)
            xpad_ref[blk, 0:HL, :] = raw_prev[:, cc]
            xpad_ref[blk, HL:HL + tm, :] = raw[:, cc]
            xpad_ref[blk, HL + tm:HL + tm + HL, :] = raw_next[:, cc]
            for rb in range(tm // RB):
                acc = jnp.zeros((RB, LANES), F32)
                for k in range(SSD_CONV):
                    r0 = rb * RB + HL - SSD_CONV // 2 + k
                    acc = acc + xpad_ref[blk, r0:r0 + RB, :] * wc_ref[k:k + 1, lc]
                xbc_ref[rb * RB:(rb + 1) * RB, lc] = _silu(acc + bc_ref[:, lc])
    for j in range(SSD_INNER // D):
        sz_ref[:, j * D:(j + 1) * D] = _silu(_wdot(hb, wz_ref, j * D, D))
    raw = jnp.dot(hb, wdt_ref[...], preferred_element_type=F32) + dtb_ref[...]
    lane = lax.broadcasted_iota(jnp.int32, (1, LANES), 1)
    return jnp.where(lane < 2 * SSD_HEADS, _softplus(raw), 0.0)


def _ssd_in(x, seq_len, layer, row_fn, mod, nw, w_in, w_dt, dt_bias, w_conv, b_conv):
    M = x.shape[0]
    tm = TOKEN_TILE
    HL = SSD_HALO
    tps = seq_len // tm
    n_hblk = M // HL
    row = lambda i: (i, 0)
    items = [(pl.BlockSpec((tm, D), row), x)]
    if tps > 1:
        items += [(pl.BlockSpec((HL, D), lambda i: (jnp.maximum(i * (tm // HL) - 1, 0), 0)), x),
                  (pl.BlockSpec((HL, D), lambda i: (jnp.minimum((i + 1) * (tm // HL), n_hblk - 1), 0)), x)]
    items += [(_const_spec((1, D)), nw), (_mod_spec(layer, lambda i: row_fn(i * tm)), mod),
              _W(w_in, SSD_INNER, 0), _W(w_in, SSD_XBC, SSD_INNER),
              (_const_spec((D, LANES)), w_dt), (_const_spec((1, LANES)), dt_bias),
              (_const_spec((SSD_CONV + 1, SSD_XBC)), w_conv), (_const_spec((1, SSD_XBC)), b_conv)]
    in_specs, args, groups = _operands(items)
    return pl.pallas_call(
        _grouped(functools.partial(_ssd_in_kernel, tps), groups),
        grid=(M // tm,),
        in_specs=in_specs,
        out_specs=[pl.BlockSpec((tm, SSD_INNER), row), pl.BlockSpec((tm, SSD_XBC), row), pl.BlockSpec((tm, LANES), row)],
        out_shape=[jax.ShapeDtypeStruct((M, SSD_INNER), F32), jax.ShapeDtypeStruct((M, SSD_XBC), F32),
                   jax.ShapeDtypeStruct((M, LANES), F32)],
        scratch_shapes=[pltpu.VMEM((SSD_XBC // LANES, tm + 2 * HL, LANES), F32)],
        compiler_params=_cparams(1),
        name="ssd_in",
    )(*args)


def _cumsum_rows(a, reverse):
    n = a.shape[0]
    row = lax.broadcasted_iota(jnp.int32, a.shape, 0)
    k = 1
    while k < n:
        if reverse:
            a = a + jnp.where(row < n - k, pltpu.roll(a, n - k, 0), 0.0)
        else:
            a = a + jnp.where(row >= k, pltpu.roll(a, k, 0), 0.0)
        k *= 2
    return a


def _ssd_core_kernel(nc, has_h0, emit_state, *refs):
    refs = list(refs)
    dir_refs = [refs[0:2], refs[2:4]]
    alog_ref, dsk_ref = refs[4:6]
    refs = refs[6:]
    if has_h0:
        h0_ref = refs.pop(0)
    y_refs = [refs.pop(0), refs.pop(0)]
    if emit_state:
        so_ref = refs.pop(0)
    (st_ref,) = refs
    c = pl.program_id(1)
    n_blk = SSD_INNER // LANES

    @pl.when(c == 0)
    def _init():
        if has_h0:
            for dr in range(2):
                for j in range(n_blk):
                    st_ref[dr, :, j * LANES:(j + 1) * LANES] = h0_ref[dr, j * LANES:(j + 1) * LANES, :].T
        else:
            st_ref[...] = jnp.zeros(st_ref.shape, F32)

    A = -jnp.exp(alog_ref[...])
    jobs = []
    for dr in range(2):
        x_ref, dt_ref = dir_refs[dr]
        y_ref = y_refs[dr]

        def x_at(c0, x_ref=x_ref):
            return x_ref[:, c0:c0 + LANES]

        def y_put(c0, y, y_ref=y_ref):
            y_ref[:, c0:c0 + LANES] = y

        jobs.append((dr, x_at, dt_ref[...], y_put, None))
    _ssd_chunks(jobs, A, st_ref, dsk_ref)

    if emit_state:
        @pl.when(c == nc - 1)
        def _emit():
            _ssd_emit_state(st_ref, so_ref)


def _ssd_emit_state(st_ref, so_ref):
    for dr in range(2):
        for j in range(SSD_INNER // LANES):
            so_ref[dr, j * LANES:(j + 1) * LANES, :] = st_ref[dr, :, j * LANES:(j + 1) * LANES].T


def _ssd_group_terms(x_at):
    N = SSD_STATE
    terms = []
    for g in range(SSD_GROUPS):
        bg_t = x_at(SSD_INNER + g * N).T
        cg = x_at(SSD_INNER + SSD_GN + g * N).astype(BF16)
        terms.append((bg_t, cg, jnp.dot(cg, bg_t.astype(BF16), preferred_element_type=F32)))
    return terms


def _ssd_chunks(jobs, A, st_ref, dsk_ref):
    Q, N, P = SSD_CHUNK, SSD_STATE, SSD_HEAD_DIM
    lo = _head_masks()
    ri = lax.broadcasted_iota(jnp.int32, (Q, Q), 0)
    ci = lax.broadcasted_iota(jnp.int32, (Q, Q), 1)
    prepared = []
    for dr, x_at, dt, y_put, group_terms in jobs:
        cum = _cumsum_rows(dt * A, reverse=(dr == 1)) * LOG2E
        prepared.append(dict(
            dr=dr, x_at=x_at, y_put=y_put, cum=cum, cum_t=cum.T, dt_t=dt.T,
            tot_row=Q - 1 if dr == 0 else 0, mask=(ri >= ci) if dr == 0 else (ri <= ci),
            terms=_ssd_group_terms(x_at) if group_terms is None else group_terms))
    for g in range(SSD_GROUPS):
        staged = [_ssd_pair_products(job, g, pp, lo, st_ref)
                  for pp in range(SSD_HEADS // SSD_GROUPS // 2) for job in prepared]
        for job, c0, xpair, cum_x, yd, y_off, sd, st in staged:
            y = yd + y_off * jnp.exp2(cum_x)
            if job["dr"] == 0:
                y = y + xpair * dsk_ref[:, c0:c0 + LANES]
            job["y_put"](c0, y)
            tot = cum_x[job["tot_row"]:job["tot_row"] + 1, :]
            st_ref[job["dr"], :, c0:c0 + LANES] = st * jnp.exp2(tot) + sd


def _ssd_pair_products(job, g, pp, lo, st_ref):
    Q, N, P = SSD_CHUNK, SSD_STATE, SSD_HEAD_DIM
    dr, x_at, y_put, cum, cum_t, dt_t = (job[k] for k in ("dr", "x_at", "y_put", "cum", "cum_t", "dt_t"))
    tot_row, mask = job["tot_row"], job["mask"]
    bg_t, cg, cb = job["terms"][g]
    head = g * (SSD_HEADS // SSD_GROUPS) + 2 * pp
    c0 = head * P
    li = dr * SSD_HEADS + head
    xpair = x_at(c0)
    xpair_b = xpair.astype(BF16)
    cum_cols = [cum[:, li:li + 1], cum[:, li + 1:li + 2]]
    cum_x = jnp.where(lo, cum_cols[0], cum_cols[1])
    yd, sd = [], []
    for hh in range(2):
        dt_row = dt_t[li + hh:li + hh + 1, :]
        cum_row = cum_t[li + hh:li + hh + 1, :]
        decay = jnp.exp2(jnp.where(mask, cum_cols[hh] - cum_row, -jnp.inf))
        yd.append(jnp.dot((cb * decay * dt_row).astype(BF16), xpair_b, preferred_element_type=F32))
        to_end = jnp.exp2(cum_row[:, tot_row:tot_row + 1] - cum_row)
        sd.append(jnp.dot((bg_t * (dt_row * to_end)).astype(BF16), xpair_b, preferred_element_type=F32))
    st = st_ref[dr, :, c0:c0 + LANES]
    y_off = jnp.dot(cg, st.astype(BF16), preferred_element_type=F32)
    return job, c0, xpair, cum_x, jnp.where(lo, yd[0], yd[1]), y_off, jnp.where(lo, sd[0], sd[1]), st


def _ssd_seq_kernel(x_ref, nw_ref, mod_ref, wz_ref, wx_ref, wdt_ref, dtb_ref, wc_ref, bc_ref, alog_ref, dsk_ref, gnw_ref, wo_ref,
                    o_ref, so_ref, xpad_ref, xbc_ref, sz_ref, y_ref, yb_ref, st_ref):
    Q = SSD_CHUNK
    nc = x_ref.shape[0] // Q
    x = x_ref[...]
    hb = _modnorm(x, nw_ref[...], mod_ref[...]).astype(BF16)
    dt = _ssd_project(hb, None, wz_ref, wx_ref, wdt_ref, dtb_ref, wc_ref, bc_ref, xpad_ref, sz_ref, xbc_ref)
    st_ref[...] = jnp.zeros(st_ref.shape, F32)
    A = -jnp.exp(alog_ref[...])

    def x_at_chunk(ck):
        return lambda c0: xbc_ref[ck * Q:(ck + 1) * Q, c0:c0 + LANES]

    terms = [_ssd_group_terms(x_at_chunk(ck)) for ck in range(nc)]

    def y_put_to(ref, ck):
        def y_put(c0, y):
            ref[ck * Q:(ck + 1) * Q, c0:c0 + LANES] = y
        return y_put

    for k in range(nc):
        kb = nc - 1 - k
        _ssd_chunks([(0, x_at_chunk(k), dt[k * Q:(k + 1) * Q], y_put_to(y_ref, k), terms[k]),
                     (1, x_at_chunk(kb), dt[kb * Q:(kb + 1) * Q], y_put_to(yb_ref, kb), terms[kb])],
                    A, st_ref, dsk_ref)
    _ssd_emit_state(st_ref, so_ref)
    g = _rms((y_ref[...] + yb_ref[...]) * sz_ref[...], gnw_ref[...]).astype(BF16)
    o_ref[...] = x + mod_ref[:, 2 * D:3 * D] * _wdot(g, wo_ref, 0, D)


def _ssd_seq(x, seq_len, layer, row_fn, mod, nw, w_in, w_dt, dt_bias, w_conv, b_conv, a_log, d_skip, norm_w, w_out):
    M = x.shape[0]
    n_seq = M // seq_len
    HL = SSD_HALO
    row = lambda i: (i, 0)
    st_blk = pl.BlockSpec((None, 2, SSD_INNER, SSD_STATE), lambda i: (i, 0, 0, 0))
    in_specs, args, groups = _operands([
        (pl.BlockSpec((seq_len, D), row), x), (_const_spec((1, D)), nw),
        (_mod_spec(layer, lambda i: row_fn(i * seq_len)), mod),
        _W(w_in, SSD_INNER, 0), _W(w_in, SSD_XBC, SSD_INNER),
        (_const_spec((D, LANES)), w_dt), (_const_spec((1, LANES)), dt_bias),
        (_const_spec((SSD_CONV + 1, SSD_XBC)), w_conv), (_const_spec((1, SSD_XBC)), b_conv),
        (_const_spec((1, LANES)), a_log), (_const_spec((1, SSD_INNER)), d_skip),
        (_const_spec((1, SSD_INNER)), norm_w), _W(w_out)])
    return pl.pallas_call(
        _grouped(_ssd_seq_kernel, groups),
        grid=(n_seq,),
        in_specs=in_specs,
        out_specs=[pl.BlockSpec((seq_len, D), row), st_blk],
        out_shape=[jax.ShapeDtypeStruct((M, D), F32), jax.ShapeDtypeStruct((n_seq, 2, SSD_INNER, SSD_STATE), F32)],
        scratch_shapes=[pltpu.VMEM((SSD_XBC // LANES, seq_len + 2 * HL, LANES), F32),
                        pltpu.VMEM((seq_len, SSD_XBC), F32), pltpu.VMEM((seq_len, SSD_INNER), F32),
                        pltpu.VMEM((seq_len, SSD_INNER), F32), pltpu.VMEM((seq_len, SSD_INNER), F32),
                        pltpu.VMEM((2, SSD_STATE, SSD_INNER), F32)],
        compiler_params=_cparams(1),
        name="ssd_seq",
    )(*args)


def _ssd_core(xbc, dtp, a_log, d_skip, h0, n_seq, seq_len, emit_state):
    Q = SSD_CHUNK
    nc = seq_len // Q
    M = n_seq * seq_len

    def chunk(dr):
        return (lambda b, c: b * nc + c) if dr == 0 else (lambda b, c: b * nc + nc - 1 - c)

    in_specs, args = [], []
    for dr in range(2):
        ch = chunk(dr)
        in_specs += [pl.BlockSpec((Q, SSD_XBC), lambda b, c, ch=ch: (ch(b, c), 0)),
                     pl.BlockSpec((Q, LANES), lambda b, c, ch=ch: (ch(b, c), 0))]
        args += [xbc, dtp]
    in_specs += [_const_spec((1, LANES)), _const_spec((1, SSD_INNER))]
    args += [a_log, d_skip]
    st_blk = pl.BlockSpec((None, 2, SSD_INNER, SSD_STATE), lambda b, c: (b, 0, 0, 0))
    has_h0 = h0 is not None
    if has_h0:
        in_specs.append(st_blk)
        args.append(h0)
    out_specs = [pl.BlockSpec((Q, SSD_INNER), lambda b, c: (chunk(0)(b, c), 0)),
                 pl.BlockSpec((Q, SSD_INNER), lambda b, c: (chunk(1)(b, c), 0))]
    out_shape = [jax.ShapeDtypeStruct((M, SSD_INNER), F32)] * 2
    if emit_state:
        out_specs.append(st_blk)
        out_shape.append(jax.ShapeDtypeStruct((n_seq, 2, SSD_INNER, SSD_STATE), F32))
    return pl.pallas_call(
        functools.partial(_ssd_core_kernel, nc, has_h0, emit_state),
        grid=(n_seq, nc),
        in_specs=in_specs,
        out_specs=out_specs,
        out_shape=out_shape,
        scratch_shapes=[pltpu.VMEM((2, SSD_STATE, SSD_INNER), F32)],
        compiler_params=_cparams(2),
        name="ssd_core",
    )(*args)


def _ssd_out_kernel(yf_ref, yb_ref, sz_ref, x_ref, mod_ref, nw_ref, w_ref, o_ref):
    g = _rms((yf_ref[...] + yb_ref[...]) * sz_ref[...], nw_ref[...]).astype(BF16)
    o_ref[...] = x_ref[...] + mod_ref[:, 2 * D:3 * D] * _wdot(g, w_ref, 0, D)


def _ssd_out(yf, yb, sz, x, layer, row_fn, mod, norm_w, w_out):
    M = x.shape[0]
    tm = TOKEN_TILE
    row = lambda i: (i, 0)
    wide = pl.BlockSpec((tm, SSD_INNER), row)
    in_specs, args, groups = _operands([
        (wide, yf), (wide, yb), (wide, sz), (pl.BlockSpec((tm, D), row), x),
        (_mod_spec(layer, lambda i: row_fn(i * tm)), mod), (_const_spec((1, SSD_INNER)), norm_w), _W(w_out)])
    return pl.pallas_call(
        _grouped(_ssd_out_kernel, groups),
        grid=(M // tm,),
        in_specs=in_specs,
        out_specs=pl.BlockSpec((tm, D), row),
        out_shape=jax.ShapeDtypeStruct((M, D), F32),
        compiler_params=_cparams(1),
        name="ssd_out",
    )(*args)


def _ssd_layer(xc, xl, layer, mod, nw, p, n_lat):
    j = layer // 3
    w_in = p["ssd_w_in"][j].astype(BF16)
    w_dt = jnp.pad(w_in[:, SSD_IN_COLS:], ((0, 0), (0, LANES - 2 * SSD_HEADS)))
    dt_bias = jnp.pad(p["ssd_dt_bias"][j].reshape(1, -1), ((0, 0), (0, LANES - 2 * SSD_HEADS)))
    a_log = jnp.pad(p["ssd_a_log"][j].reshape(1, -1), ((0, 0), (0, LANES - 2 * SSD_HEADS)))
    w_conv = jnp.pad(p["ssd_w_conv"][j], ((0, 1), (0, 0)))
    b_conv = p["ssd_b_conv"][j][None]
    d_skip = jnp.repeat(p["ssd_d"][j], SSD_HEAD_DIM)[None]
    norm_w = p["ssd_norm_w"][j][None]
    w_out = p["ssd_w_out"][j].astype(BF16)
    row_ctx = lambda tok: 0
    row_lat = lambda tok: 1 + tok // DEC_SEQ
    oc, s_new = _ssd_seq(xc, SEQ, layer, row_ctx, mod, nw, w_in, w_dt, dt_bias, w_conv, b_conv, a_log, d_skip, norm_w, w_out)
    szl, xbc_l, dt_l = _ssd_in(xl, DEC_SEQ, layer, row_lat, mod, nw, w_in, w_dt, dt_bias, w_conv, b_conv)
    h0 = p["state_ssm"][:n_lat, j].reshape(n_lat, 2, SSD_INNER, SSD_STATE)
    yfl, ybl = _ssd_core(xbc_l, dt_l, a_log, d_skip, h0, n_lat, DEC_SEQ, False)
    ol = _ssd_out(yfl, ybl, szl, xl, layer, row_lat, mod, norm_w, w_out)
    return oc, ol, s_new


def kernel(x_prompt, x_sample, cache_k, cache_v, state_ssm, c, c_ctx, ada_w, ada_b, norm_w, final_norm_w, conv_w_in, conv_b_in, conv_w_dw, conv_b_dw, conv_ln_w, conv_ln_b, conv_w_out, conv_b_out, na_w_in, na_rpb, na_w_out, ssd_w_in, ssd_w_conv, ssd_b_conv, ssd_a_log, ssd_dt_bias, ssd_d, ssd_norm_w, ssd_w_out):
    p = dict(cache_k=cache_k, cache_v=cache_v, state_ssm=state_ssm,
             conv_w_in=conv_w_in, conv_b_in=conv_b_in, conv_w_dw=conv_w_dw, conv_b_dw=conv_b_dw,
             conv_ln_w=conv_ln_w, conv_ln_b=conv_ln_b, conv_w_out=conv_w_out, conv_b_out=conv_b_out,
             na_w_in=na_w_in, na_rpb=na_rpb, na_w_out=na_w_out,
             ssd_w_in=ssd_w_in, ssd_w_conv=ssd_w_conv, ssd_b_conv=ssd_b_conv, ssd_a_log=ssd_a_log,
             ssd_dt_bias=ssd_dt_bias, ssd_d=ssd_d, ssd_norm_w=ssd_norm_w, ssd_w_out=ssd_w_out)
    n_ctx, n_lat = x_prompt.shape[0], x_sample.shape[0]
    assert x_prompt.shape[1:] == (SEQ, D) and x_sample.shape[1:] == (DEC_SEQ, D)
    assert (DEPTH - 1) % 3 == 0, "the final RMSNorm is fused into a trailing Conformer layer"
    xc = x_prompt.reshape(n_ctx * SEQ, D)
    xl = x_sample.reshape(n_lat * DEC_SEQ, D)
    cond8 = jnp.concatenate([c_ctx[None], c, jnp.zeros((8 - 1 - n_lat, D), F32)], axis=0)
    mod = _modulation(cond8, ada_w, ada_b).reshape(DEPTH, 8, 1, 3 * D)
    row_ctx = lambda tok: 0
    row_lat = lambda tok: 1 + tok // DEC_SEQ
    new_k, new_v, new_s = [], [], []
    for i in range(DEPTH):
        kind, j = i % 3, i // 3
        nw = norm_w[i][None]
        if kind == 0:
            cp = _prep_conv(p, j)
            fw = final_norm_w[None] if i == DEPTH - 1 else None
            xc = _conv_layer(xc, SEQ, i, row_ctx, mod, nw, *cp, fw)
            xl = _conv_layer(xl, DEC_SEQ, i, row_lat, mod, nw, *cp, fw)
        elif kind == 1:
            xc, xl, k_c, v_c = _na_layer(xc, xl, i, mod, nw, p, n_lat)
            new_k.append(k_c.reshape(n_ctx, SEQ, NA_HEADS, NA_HEAD_DIM))
            new_v.append(v_c.reshape(n_ctx, SEQ, NA_HEADS, NA_HEAD_DIM))
        else:
            xc, xl, s_c = _ssd_layer(xc, xl, i, mod, nw, p, n_lat)
            new_s.append(s_c.reshape(n_ctx, 2, SSD_HEADS, SSD_HEAD_DIM, SSD_STATE))
    return (xc.reshape(n_ctx, SEQ, D), xl.reshape(n_lat, DEC_SEQ, D),
            jnp.stack(new_k, axis=1), jnp.stack(new_v, axis=1), jnp.stack(new_s, axis=1))
```

```python
import functools

import jax
import jax.numpy as jnp
from jax import lax
from jax.experimental import pallas as pl
from jax.experimental.pallas import tpu as pltpu

F32 = jnp.float32
BF16 = jnp.bfloat16

D = 1024
DEPTH = 4
SEQ = 256
DEC_SEQ = 2048
GRID_W = 64
RMS_EPS = 1e-6
LN_EPS = 1e-5
LOG2E = 1.4426950408889634
CONF_K = 31
NA_HEADS = 16
NA_HEAD_DIM = 64
NA_KH = 8
NA_KW = 16
NEG_INF = -1e30
SSD_INNER = 2048
SSD_HEADS = 32
SSD_HEAD_DIM = 64
SSD_STATE = 128
SSD_GROUPS = 4
SSD_GN = SSD_GROUPS * SSD_STATE
SSD_CONV = 7
SSD_CHUNK = 128
SSD_XBC = SSD_INNER + 2 * SSD_GN

LANES = 128
TOKEN_TILE = 256
W_CHUNK = 512
CONV_HALO = 16
SSD_HALO = 8
VMEM_LIMIT = 56 * 1024 * 1024


def _cparams(n_axes):
    return pltpu.CompilerParams(dimension_semantics=("arbitrary",) * n_axes,
                                vmem_limit_bytes=VMEM_LIMIT)


def _const_spec(shape):
    nd = len(shape)
    return pl.BlockSpec(shape, lambda *_: (0,) * nd, pipeline_mode=pl.Buffered(1))


class _W:
    def __init__(self, w, ncols=None, col0=0):
        self.w, self.col0 = w, col0
        self.ncols = w.shape[1] if ncols is None else ncols


def _operands(items):
    specs, args, groups = [], [], []
    for it in items:
        if isinstance(it, _W):
            n, b0, K = it.ncols // W_CHUNK, it.col0 // W_CHUNK, it.w.shape[0]
            for c in range(b0, b0 + n):
                specs.append(pl.BlockSpec((K, W_CHUNK), lambda *_, c=c: (0, c), pipeline_mode=pl.Buffered(1)))
                args.append(it.w)
            groups.append(n)
        else:
            specs.append(it[0])
            args.append(it[1])
            groups.append(None)
    return specs, args, groups


def _grouped(body, groups):
    def kernel_fn(*refs):
        refs = list(refs)
        packed = []
        for g in groups:
            if g is None:
                packed.append(refs.pop(0))
            else:
                packed.append(tuple(refs[:g]))
                del refs[:g]
        return body(*packed, *refs)
    return kernel_fn


def _wdot(h, w_refs, c0, n):
    outs = []
    c = c0
    while c < c0 + n:
        j, off = divmod(c, W_CHUNK)
        m = min(W_CHUNK - off, c0 + n - c)
        outs.append(jnp.dot(h, w_refs[j][:, off:off + m], preferred_element_type=F32))
        c += m
    return outs[0] if len(outs) == 1 else jnp.concatenate(outs, axis=1)


def _sigmoid(x):
    return 1.0 / (1.0 + jnp.exp(-x))


def _silu(x):
    return x * _sigmoid(x)


def _rms(x, w):
    ms = jnp.mean(x * x, axis=-1, keepdims=True)
    return x * lax.rsqrt(ms + RMS_EPS) * w


def _modnorm(x, nw, mod):
    return _rms(x, nw * (1.0 + mod[:, D:2 * D])) + mod[:, :D]


def _mod_spec(layer, row_fn):
    return pl.BlockSpec((None, None, 1, 3 * D), lambda i, *_: (layer, row_fn(i), 0, 0))


def _mod_kernel(cond_ref, w_ref, b_ref, o_ref):
    s = _silu(cond_ref[...]).astype(BF16)
    o_ref[...] = jnp.dot(s, w_ref[...].astype(BF16), preferred_element_type=F32) + b_ref[...]


def _modulation(cond8, ada_w, ada_b):
    tn = 1024
    return pl.pallas_call(
        _mod_kernel,
        grid=(DEPTH, 3 * D // tn),
        in_specs=[pl.BlockSpec((8, D), lambda i, j: (0, 0)),
                  pl.BlockSpec((None, D, tn), lambda i, j: (i, 0, j)),
                  pl.BlockSpec((None, 1, tn), lambda i, j: (i, 0, j))],
        out_specs=pl.BlockSpec((None, 8, tn), lambda i, j: (i, 0, j)),
        out_shape=jax.ShapeDtypeStruct((DEPTH, 8, 3 * D), F32),
        compiler_params=_cparams(2),
        name="adaln_mod",
    )(cond8, ada_w, ada_b.reshape(DEPTH, 1, 3 * D))


CONV_CHUNK = 256


def _conv_kernel(tiles_per_seq, final, *refs):
    refs = list(refs)
    x_ref = refs.pop(0)
    if tiles_per_seq > 1:
        xp_ref = refs.pop(0)
        xn_ref = refs.pop(0)
    nw_ref, mod_ref, wi_ref, bi_ref, wdw_ref, bdw_ref, lnw_ref, lnb_ref, wo_ref, bo_ref = refs[:10]
    refs = refs[10:]
    if final:
        fw_ref = refs.pop(0)
    o_ref, ubuf, cbuf = refs
    T = x_ref.shape[0]
    H = CONV_HALO
    RB = 128

    hb = _modnorm(x_ref[...], nw_ref[...], mod_ref[...]).astype(BF16)
    if tiles_per_seq > 1:
        t = pl.program_id(0) % tiles_per_seq
        hh = _modnorm(jnp.concatenate([xp_ref[...], xn_ref[...]], axis=0), nw_ref[...], mod_ref[...]).astype(BF16)

    def glu(h, cols):
        v = _wdot(h, wi_ref, cols.start, CONV_CHUNK) + bi_ref[:, cols]
        gcols = slice(D + cols.start, D + cols.stop)
        g = _wdot(h, wi_ref, gcols.start, CONV_CHUNK) + bi_ref[:, gcols]
        return v * _sigmoid(g)

    h_all = jnp.concatenate([hb, hh], axis=0) if tiles_per_seq > 1 else hb
    for ch in range(D // CONV_CHUNK):
        cols = slice(ch * CONV_CHUNK, (ch + 1) * CONV_CHUNK)
        u_all = glu(h_all, cols)
        u_main = u_all[0:T]
        if tiles_per_seq > 1:
            u_prev = jnp.where(t > 0, u_all[T:T + H], 0.0)
            u_next = jnp.where(t < tiles_per_seq - 1, u_all[T + H:T + 2 * H], 0.0)
        else:
            u_prev = u_next = jnp.zeros((H, CONV_CHUNK), F32)
        for cb in range(CONV_CHUNK // LANES):
            blk = ch * (CONV_CHUNK // LANES) + cb
            lc = slice(blk * LANES, (blk + 1) * LANES)
            cc = slice(cb * LANES, (cb + 1) * LANES)
            ubuf[blk, 0:H, :] = u_prev[:, cc]
            ubuf[blk, H:H + T, :] = u_main[:, cc]
            ubuf[blk, H + T:H + T + H, :] = u_next[:, cc]
            for rb in range(T // RB):
                acc = jnp.zeros((RB, LANES), F32)
                for k in range(CONF_K):
                    r0 = rb * RB + H - CONF_K // 2 + k
                    acc = acc + ubuf[blk, r0:r0 + RB, :] * wdw_ref[k:k + 1, lc]
                cbuf[rb * RB:(rb + 1) * RB, lc] = acc + bdw_ref[:, lc]

    sz = _silu(_wdot(hb, wi_ref, 2 * D, D) + bi_ref[:, 2 * D:3 * D])
    c = cbuf[...]
    mu = jnp.mean(c, axis=-1, keepdims=True)
    xc = c - mu
    var = jnp.mean(xc * xc, axis=-1, keepdims=True)
    ln = xc * lax.rsqrt(var + LN_EPS) * lnw_ref[...] + lnb_ref[...]
    g = (_silu(ln) * sz).astype(BF16)
    y = _wdot(g, wo_ref, 0, D) + bo_ref[...]
    xn = x_ref[...] + mod_ref[:, 2 * D:3 * D] * y
    if final:
        xn = _rms(xn, fw_ref[...])
    o_ref[...] = xn


def _conv_layer(x, seq_len, layer, row_fn, mod, nw, w_in, b_in, w_dw, b_dw, ln_w, ln_b, w_out, b_out, final_w):
    M = x.shape[0]
    T = TOKEN_TILE
    tps = seq_len // T
    H = CONV_HALO
    n_halo_blocks = M // H
    row = lambda i: (i, 0)
    items = [(pl.BlockSpec((T, D), row), x)]
    if tps > 1:
        items += [(pl.BlockSpec((H, D), lambda i: (jnp.maximum(i * (T // H) - 1, 0), 0)), x),
                  (pl.BlockSpec((H, D), lambda i: (jnp.minimum((i + 1) * (T // H), n_halo_blocks - 1), 0)), x)]
    items += [(_const_spec((1, D)), nw), (_mod_spec(layer, lambda i: row_fn(i * T)), mod),
              _W(w_in), (_const_spec((1, 3 * D)), b_in),
              (_const_spec((CONF_K + 1, D)), w_dw), (_const_spec((1, D)), b_dw),
              (_const_spec((1, D)), ln_w), (_const_spec((1, D)), ln_b),
              _W(w_out), (_const_spec((1, D)), b_out)]
    final = final_w is not None
    if final:
        items.append((_const_spec((1, D)), final_w))
    in_specs, args, groups = _operands(items)
    return pl.pallas_call(
        _grouped(functools.partial(_conv_kernel, tps, final), groups),
        grid=(M // T,),
        in_specs=in_specs,
        out_specs=pl.BlockSpec((T, D), row),
        out_shape=jax.ShapeDtypeStruct((M, D), F32),
        scratch_shapes=[pltpu.VMEM((D // LANES, T + 2 * H, LANES), F32), pltpu.VMEM((T, D), F32)],
        compiler_params=_cparams(1),
        name="conv_layer",
    )(*args)


def _prep_conv(p, j):
    w_dw = jnp.pad(p["conv_w_dw"][j], ((0, 1), (0, 0)))
    return (p["conv_w_in"][j].astype(BF16), p["conv_b_in"][j][None], w_dw, p["conv_b_dw"][j][None],
            p["conv_ln_w"][j][None], p["conv_ln_b"][j][None], p["conv_w_out"][j].astype(BF16), p["conv_b_out"][j][None])


def _out_proj_kernel(g_ref, x_ref, mod_ref, w_ref, o_ref):
    y = _wdot(g_ref[...].astype(BF16), w_ref, 0, D)
    o_ref[...] = x_ref[...] + mod_ref[:, 2 * D:3 * D] * y


def _out_proj(g, x, layer, row_fn, mod, w_out):
    M, K = g.shape
    tm = TOKEN_TILE
    row = lambda i: (i, 0)
    in_specs, args, groups = _operands([
        (pl.BlockSpec((tm, K), row), g), (pl.BlockSpec((tm, D), row), x),
        (_mod_spec(layer, lambda i: row_fn(i * tm)), mod), _W(w_out)])
    return pl.pallas_call(
        _grouped(_out_proj_kernel, groups),
        grid=(M // tm,),
        in_specs=in_specs,
        out_specs=pl.BlockSpec((tm, D), row),
        out_shape=jax.ShapeDtypeStruct((M, D), F32),
        compiler_params=_cparams(1),
        name="out_proj",
    )(*args)


NA_SCALE = NA_HEAD_DIM ** -0.5
NA_ROWS = 32
NA_WIN = NA_KH * GRID_W
NT_DIMS = (((1,), (1,)), ((), ()))
NA_STAGE_PAIRS = 4


def _na_in_kernel(x_ref, nw_ref, mod_ref, w_ref, q_ref, k_ref, v_ref, sz_ref):
    hb = _modnorm(x_ref[...], nw_ref[...], mod_ref[...]).astype(BF16)
    q_ref[...] = _wdot(hb, w_ref, 0, D).astype(q_ref.dtype)
    k_ref[...] = _wdot(hb, w_ref, D, D).astype(k_ref.dtype)
    v_ref[...] = _wdot(hb, w_ref, 2 * D, D).astype(v_ref.dtype)
    sz_ref[...] = _silu(_wdot(hb, w_ref, 3 * D, D))


def _na_in(x, layer, row_fn, mod, nw, w_in, kv_dtype):
    M = x.shape[0]
    tm = TOKEN_TILE
    row = lambda i: (i, 0)
    blk = pl.BlockSpec((tm, D), row)
    in_specs, args, groups = _operands([
        (blk, x), (_const_spec((1, D)), nw), (_mod_spec(layer, lambda i: row_fn(i * tm)), mod), _W(w_in)])
    return pl.pallas_call(
        _grouped(_na_in_kernel, groups),
        grid=(M // tm,),
        in_specs=in_specs,
        out_specs=[blk, blk, blk, blk],
        out_shape=[jax.ShapeDtypeStruct((M, D), BF16), jax.ShapeDtypeStruct((M, D), kv_dtype),
                   jax.ShapeDtypeStruct((M, D), kv_dtype), jax.ShapeDtypeStruct((M, D), F32)],
        compiler_params=_cparams(1),
        name="na_in",
    )(*args)


def _head_masks():
    lane = lax.broadcasted_iota(jnp.int32, (1, LANES), 1)
    return lane < NA_HEAD_DIM


def _na_seq_kernel(x_ref, nw_ref, mod_ref, wi_ref, wo_ref, o_ref, k_ref, v_ref, g_ref):
    n = x_ref.shape[0]
    lo = _head_masks()
    x = x_ref[...]
    hb = _modnorm(x, nw_ref[...], mod_ref[...]).astype(BF16)
    hpc = CONV_CHUNK // NA_HEAD_DIM
    staged = []
    n_chunks = D // CONV_CHUNK
    chunks_per_stage = NA_STAGE_PAIRS * LANES // CONV_CHUNK
    for ch in range(n_chunks):
        cols = slice(ch * CONV_CHUNK, (ch + 1) * CONV_CHUNK)
        q = _wdot(hb, wi_ref, cols.start, CONV_CHUNK).astype(BF16)
        k = _wdot(hb, wi_ref, D + cols.start, CONV_CHUNK)
        v = _wdot(hb, wi_ref, 2 * D + cols.start, CONV_CHUNK)
        sz = _silu(_wdot(hb, wi_ref, 3 * D + cols.start, CONV_CHUNK))
        k_ref[:, ch * hpc:(ch + 1) * hpc, :] = k.reshape(n, hpc, NA_HEAD_DIM)
        v_ref[:, ch * hpc:(ch + 1) * hpc, :] = v.reshape(n, hpc, NA_HEAD_DIM)
        kb = k.astype(BF16)
        vb = v.astype(BF16)
        for pp in range(CONV_CHUNK // LANES):
            pc = slice(pp * LANES, (pp + 1) * LANES)
            qp = q[:, pc]
            zero = jnp.zeros_like(qp)
            qs = jnp.concatenate([jnp.where(lo, qp, zero), jnp.where(lo, zero, qp)], axis=0)
            s = lax.dot_general(qs, kb[:, pc], NT_DIMS, preferred_element_type=F32) * NA_SCALE
            staged.append((s, vb[:, pc], sz[:, pc], cols.start + pp * LANES))
        if (ch + 1) % chunks_per_stage == 0 or ch == n_chunks - 1:
            for s, vp, szp, c0 in staged:
                e = jnp.exp(s - jnp.max(s, axis=-1, keepdims=True))
                l = jnp.sum(e, axis=-1, keepdims=True)
                o = jnp.dot(e.astype(BF16), vp, preferred_element_type=F32) / l
                g_ref[:, c0:c0 + LANES] = (jnp.where(lo, o[:n], o[n:]) * szp).astype(BF16)
            staged = []
    y = _wdot(g_ref[...], wo_ref, 0, D)
    o_ref[...] = x + mod_ref[:, 2 * D:3 * D] * y


def _na_seq(x, seq_len, layer, row_fn, mod, nw, w_in, w_out):
    M = x.shape[0]
    row = lambda i: (i, 0)
    kv_blk = pl.BlockSpec((seq_len, NA_HEADS, NA_HEAD_DIM), lambda i: (i, 0, 0))
    kv_shape = jax.ShapeDtypeStruct((M, NA_HEADS, NA_HEAD_DIM), F32)
    in_specs, args, groups = _operands([
        (pl.BlockSpec((seq_len, D), row), x), (_const_spec((1, D)), nw),
        (_mod_spec(layer, lambda i: row_fn(i * seq_len)), mod), _W(w_in), _W(w_out)])
    return pl.pallas_call(
        _grouped(_na_seq_kernel, groups),
        grid=(M // seq_len,),
        in_specs=in_specs,
        out_specs=[pl.BlockSpec((seq_len, D), row), kv_blk, kv_blk],
        out_shape=[jax.ShapeDtypeStruct((M, D), F32), kv_shape, kv_shape],
        scratch_shapes=[pltpu.VMEM((seq_len, D), BF16)],
        compiler_params=_cparams(1),
        name="na_seq",
    )(*args)


def _rpb_kernel(rpb_ref, o_ref):
    h = pl.program_id(0)
    c = lax.broadcasted_iota(jnp.int32, (GRID_W, LANES), 0)
    lane = lax.broadcasted_iota(jnp.int32, (GRID_W, LANES), 1)
    kc = lane & (GRID_W - 1)
    upper = lax.broadcasted_iota(jnp.int32, (1, LANES), 1) >= GRID_W
    cs = jnp.clip(c - NA_KW // 2, 0, GRID_W - NA_KW)
    valid = (kc >= cs) & (kc < cs + NA_KW)
    n_pair = 2 * NA_KH - 2

    def toeplitz(dri, lane0):
        row = rpb_ref[pl.ds(h * (2 * NA_KH - 1) + dri, 1), :]
        rows = jnp.broadcast_to(row, (GRID_W, LANES))
        return pltpu.roll(rows, (LANES - (NA_KW - 1) + lane0) % LANES, 1, stride=1, stride_axis=0)

    for dri in range(n_pair):
        tile = jnp.where(upper, toeplitz(dri + 1, GRID_W), toeplitz(dri, 0))
        tile = jnp.where(valid, tile, NEG_INF)
        for idx in range(NA_KH):
            jj2 = dri - (NA_KH - 1) + idx
            if jj2 % 2 == 0 and 0 <= jj2 // 2 < NA_KH // 2:
                jj = jj2 // 2
                o_ref[idx, :, jj * LANES:(jj + 1) * LANES] = tile


def _rpb_table(rpb):
    n_rows = NA_HEADS * (2 * NA_KH - 1)
    rows = jnp.pad(rpb.reshape(n_rows, 2 * NA_KW - 1), ((0, 0), (0, LANES - (2 * NA_KW - 1))))
    return pl.pallas_call(
        _rpb_kernel,
        grid=(NA_HEADS,),
        in_specs=[_const_spec((n_rows, LANES))],
        out_specs=pl.BlockSpec((NA_KH, None, GRID_W, NA_WIN), lambda h: (0, h, 0, 0)),
        out_shape=jax.ShapeDtypeStruct((NA_KH, NA_HEADS, GRID_W, NA_WIN), F32),
        compiler_params=_cparams(1),
        name="na_rpb_table",
    )(rows)


def _na_row_start(r):
    return jnp.clip(r - NA_KH // 2, 0, NA_ROWS - NA_KH)


def _na_lat_kernel(q_ref, k_ref, v_ref, ckt_ref, cv_ref, bias_ref, sz_ref, o_ref):
    lo = _head_masks()
    k0 = pl.multiple_of(_na_row_start(pl.program_id(1)) * GRID_W, GRID_W)
    n_pairs = NA_HEADS // 2
    for p0 in range(0, n_pairs, NA_STAGE_PAIRS):
        scores = []
        for p in range(p0, p0 + NA_STAGE_PAIRS):
            cols = slice(p * LANES, (p + 1) * LANES)
            qp = q_ref[:, cols]
            kw = k_ref[pl.ds(k0, NA_WIN), cols]
            ck_t = ckt_ref[cols, :]
            zero = jnp.zeros_like(qp)
            qs = jnp.concatenate([jnp.where(lo, qp, zero), jnp.where(lo, zero, qp)], axis=0)
            bias = bias_ref[2 * p:2 * p + 2].reshape(2 * GRID_W, NA_WIN)
            scores.append((lax.dot_general(qs, kw, NT_DIMS, preferred_element_type=F32) * NA_SCALE + bias,
                           jnp.dot(qs, ck_t, preferred_element_type=F32) * NA_SCALE))
        for p, (s_loc, s_ctx) in zip(range(p0, p0 + NA_STAGE_PAIRS), scores):
            cols = slice(p * LANES, (p + 1) * LANES)
            vw = v_ref[pl.ds(k0, NA_WIN), cols]
            cv = cv_ref[:, cols]
            m = jnp.maximum(jnp.max(s_loc, axis=-1, keepdims=True), jnp.max(s_ctx, axis=-1, keepdims=True))
            e_loc = jnp.exp(s_loc - m)
            e_ctx = jnp.exp(s_ctx - m)
            l = jnp.sum(e_loc, axis=-1, keepdims=True) + jnp.sum(e_ctx, axis=-1, keepdims=True)
            o = (jnp.dot(e_loc.astype(BF16), vw, preferred_element_type=F32)
                 + jnp.dot(e_ctx.astype(BF16), cv, preferred_element_type=F32)) / l
            o_ref[:, cols] = jnp.where(lo, o[:GRID_W], o[GRID_W:]) * sz_ref[:, cols]


def _na_lat(q, k, v, ck_t, cv, bias, sz, n_lat):
    T = NA_ROWS * GRID_W
    past = cv.shape[1]
    qblk = pl.BlockSpec((GRID_W, D), lambda b, r: (b * NA_ROWS + r, 0))
    kvblk = pl.BlockSpec((T, D), lambda b, r: (b, 0))
    cblk = pl.BlockSpec((None, past, D), lambda b, r: (b, 0, 0))
    ctblk = pl.BlockSpec((None, D, past), lambda b, r: (b, 0, 0))
    bblk = pl.BlockSpec((None, NA_HEADS, GRID_W, NA_WIN), lambda b, r: (r - _na_row_start(r), 0, 0, 0))
    return pl.pallas_call(
        _na_lat_kernel,
        grid=(n_lat, NA_ROWS),
        in_specs=[qblk, kvblk, kvblk, ctblk, cblk, bblk, qblk],
        out_specs=qblk,
        out_shape=jax.ShapeDtypeStruct((n_lat * T, D), F32),
        compiler_params=_cparams(2),
        name="na_lat_attn",
    )(q, k, v, ck_t, cv, bias, sz)


def _na_layer(xc, xl, layer, mod, nw, p, n_lat):
    j = layer // 3
    w_in = p["na_w_in"][j].astype(BF16)
    w_out = p["na_w_out"][j].astype(BF16)
    row_ctx = lambda tok: 0
    row_lat = lambda tok: 1 + tok // (NA_ROWS * GRID_W)
    oc, kc, vc = _na_seq(xc, SEQ, layer, row_ctx, mod, nw, w_in, w_out)
    ql, kl, vl, szl = _na_in(xl, layer, row_lat, mod, nw, w_in, BF16)
    past = p["cache_k"].shape[2]
    ck_t = jnp.swapaxes(p["cache_k"][:n_lat, j].reshape(n_lat, past, D), 1, 2).astype(BF16)
    cv = p["cache_v"][:n_lat, j].reshape(n_lat, past, D).astype(BF16)
    gl = _na_lat(ql, kl, vl, ck_t, cv, _rpb_table(p["na_rpb"][j]), szl, n_lat)
    return oc, _out_proj(gl, xl, layer, row_lat, mod, w_out), kc, vc


SSD_IN_COLS = SSD_INNER + SSD_XBC


def _softplus(x):
    return jnp.maximum(x, 0.0) + jnp.log(1.0 + jnp.exp(-jnp.abs(x)))


def _ssd_in_kernel(tiles_per_seq, *refs):
    refs = list(refs)
    x_ref = refs.pop(0)
    if tiles_per_seq > 1:
        xp_ref = refs.pop(0)
        xn_ref = refs.pop(0)
    nw_ref, mod_ref, wz_ref, wx_ref, wdt_ref, dtb_ref, wc_ref, bc_ref, sz_ref, xbc_ref, dt_ref, xpad_ref = refs
    hb = _modnorm(x_ref[...], nw_ref[...], mod_ref[...]).astype(BF16)
    halo = None
    if tiles_per_seq > 1:
        t = pl.program_id(0) % tiles_per_seq
        xh = jnp.concatenate([xp_ref[...], xn_ref[...]], axis=0)
        halo = (_modnorm(xh, nw_ref[...], mod_ref[...]).astype(BF16), t > 0, t < tiles_per_seq - 1)
    dt_ref[...] = _ssd_project(hb, halo, wz_ref, wx_ref, wdt_ref, dtb_ref, wc_ref, bc_ref, xpad_ref, sz_ref, xbc_ref)


def _ssd_project(hb, halo, wz_ref, wx_ref, wdt_ref, dtb_ref, wc_ref, bc_ref, xpad_ref, sz_ref, xbc_ref):
    tm = hb.shape[0]
    HL = SSD_HALO
    RB = 128
    h_all = hb if halo is None else jnp.concatenate([hb, halo[0]], axis=0)
    for j in range(SSD_XBC // D):
        c0 = j * D
        raw_all = _wdot(h_all, wx_ref, c0, D)
        raw = raw_all[0:tm]
        if halo is not None:
            raw_prev = jnp.where(halo[1], raw_all[tm:tm + HL], 0.0)
            raw_next = jnp.where(halo[2], raw_all[tm + HL:tm + 2 * HL], 0.0)
        else:
            raw_prev = raw_next = jnp.zeros((HL, D), F32)
        for cb in range(D // LANES):
            blk = j * (D // LANES) + cb
            lc = slice(blk * LANES, (blk + 1) * LANES)
            cc = slice(cb * LANES, (cb + 1) * LANES)
            xpad_ref[blk, 0:HL, :] = raw_prev[:, cc]
            xpad_ref[blk, HL:HL + tm, :] = raw[:, cc]
            xpad_ref[blk, HL + tm:HL + tm + HL, :] = raw_next[:, cc]
            for rb in range(tm // RB):
                acc = jnp.zeros((RB, LANES), F32)
                for k in range(SSD_CONV):
                    r0 = rb * RB + HL - SSD_CONV // 2 + k
                    acc = acc + xpad_ref[blk, r0:r0 + RB, :] * wc_ref[k:k + 1, lc]
                xbc_ref[rb * RB:(rb + 1) * RB, lc] = _silu(acc + bc_ref[:, lc])
    for j in range(SSD_INNER // D):
        sz_ref[:, j * D:(j + 1) * D] = _silu(_wdot(hb, wz_ref, j * D, D))
    raw = jnp.dot(hb, wdt_ref[...], preferred_element_type=F32) + dtb_ref[...]
    lane = lax.broadcasted_iota(jnp.int32, (1, LANES), 1)
    return jnp.where(lane < 2 * SSD_HEADS, _softplus(raw), 0.0)


def _ssd_in(x, seq_len, layer, row_fn, mod, nw, w_in, w_dt, dt_bias, w_conv, b_conv):
    M = x.shape[0]
    tm = TOKEN_TILE
    HL = SSD_HALO
    tps = seq_len // tm
    n_hblk = M // HL
    row = lambda i: (i, 0)
    items = [(pl.BlockSpec((tm, D), row), x)]
    if tps > 1:
        items += [(pl.BlockSpec((HL, D), lambda i: (jnp.maximum(i * (tm // HL) - 1, 0), 0)), x),
                  (pl.BlockSpec((HL, D), lambda i: (jnp.minimum((i + 1) * (tm // HL), n_hblk - 1), 0)), x)]
    items += [(_const_spec((1, D)), nw), (_mod_spec(layer, lambda i: row_fn(i * tm)), mod),
              _W(w_in, SSD_INNER, 0), _W(w_in, SSD_XBC, SSD_INNER),
              (_const_spec((D, LANES)), w_dt), (_const_spec((1, LANES)), dt_bias),
              (_const_spec((SSD_CONV + 1, SSD_XBC)), w_conv), (_const_spec((1, SSD_XBC)), b_conv)]
    in_specs, args, groups = _operands(items)
    return pl.pallas_call(
        _grouped(functools.partial(_ssd_in_kernel, tps), groups),
        grid=(M // tm,),
        in_specs=in_specs,
        out_specs=[pl.BlockSpec((tm, SSD_INNER), row), pl.BlockSpec((tm, SSD_XBC), row), pl.BlockSpec((tm, LANES), row)],
        out_shape=[jax.ShapeDtypeStruct((M, SSD_INNER), F32), jax.ShapeDtypeStruct((M, SSD_XBC), F32),
                   jax.ShapeDtypeStruct((M, LANES), F32)],
        scratch_shapes=[pltpu.VMEM((SSD_XBC // LANES, tm + 2 * HL, LANES), F32)],
        compiler_params=_cparams(1),
        name="ssd_in",
    )(*args)


def _cumsum_rows(a, reverse):
    n = a.shape[0]
    row = lax.broadcasted_iota(jnp.int32, a.shape, 0)
    k = 1
    while k < n:
        if reverse:
            a = a + jnp.where(row < n - k, pltpu.roll(a, n - k, 0), 0.0)
        else:
            a = a + jnp.where(row >= k, pltpu.roll(a, k, 0), 0.0)
        k *= 2
    return a


def _ssd_core_kernel(nc, has_h0, emit_state, *refs):
    refs = list(refs)
    dir_refs = [refs[0:2], refs[2:4]]
    alog_ref, dsk_ref = refs[4:6]
    refs = refs[6:]
    if has_h0:
        h0_ref = refs.pop(0)
    y_refs = [refs.pop(0), refs.pop(0)]
    if emit_state:
        so_ref = refs.pop(0)
    (st_ref,) = refs
    c = pl.program_id(1)
    n_blk = SSD_INNER // LANES

    @pl.when(c == 0)
    def _init():
        if has_h0:
            for dr in range(2):
                for j in range(n_blk):
                    st_ref[dr, :, j * LANES:(j + 1) * LANES] = h0_ref[dr, j * LANES:(j + 1) * LANES, :].T
        else:
            st_ref[...] = jnp.zeros(st_ref.shape, F32)

    A = -jnp.exp(alog_ref[...])
    jobs = []
    for dr in range(2):
        x_ref, dt_ref = dir_refs[dr]
        y_ref = y_refs[dr]

        def x_at(c0, x_ref=x_ref):
            return x_ref[:, c0:c0 + LANES]

        def y_put(c0, y, y_ref=y_ref):
            y_ref[:, c0:c0 + LANES] = y

        jobs.append((dr, x_at, dt_ref[...], y_put, None))
    _ssd_chunks(jobs, A, st_ref, dsk_ref)

    if emit_state:
        @pl.when(c == nc - 1)
        def _emit():
            _ssd_emit_state(st_ref, so_ref)


def _ssd_emit_state(st_ref, so_ref):
    for dr in range(2):
        for j in range(SSD_INNER // LANES):
            so_ref[dr, j * LANES:(j + 1) * LANES, :] = st_ref[dr, :, j * LANES:(j + 1) * LANES].T


def _ssd_group_terms(x_at):
    N = SSD_STATE
    terms = []
    for g in range(SSD_GROUPS):
        bg_t = x_at(SSD_INNER + g * N).T
        cg = x_at(SSD_INNER + SSD_GN + g * N).astype(BF16)
        terms.append((bg_t, cg, jnp.dot(cg, bg_t.astype(BF16), preferred_element_type=F32)))
    return terms


def _ssd_chunks(jobs, A, st_ref, dsk_ref):
    Q, N, P = SSD_CHUNK, SSD_STATE, SSD_HEAD_DIM
    lo = _head_masks()
    ri = lax.broadcasted_iota(jnp.int32, (Q, Q), 0)
    ci = lax.broadcasted_iota(jnp.int32, (Q, Q), 1)
    prepared = []
    for dr, x_at, dt, y_put, group_terms in jobs:
        cum = _cumsum_rows(dt * A, reverse=(dr == 1)) * LOG2E
        prepared.append(dict(
            dr=dr, x_at=x_at, y_put=y_put, cum=cum, cum_t=cum.T, dt_t=dt.T,
            tot_row=Q - 1 if dr == 0 else 0, mask=(ri >= ci) if dr == 0 else (ri <= ci),
            terms=_ssd_group_terms(x_at) if group_terms is None else group_terms))
    for g in range(SSD_GROUPS):
        staged = [_ssd_pair_products(job, g, pp, lo, st_ref)
                  for pp in range(SSD_HEADS // SSD_GROUPS // 2) for job in prepared]
        for job, c0, xpair, cum_x, yd, y_off, sd, st in staged:
            y = yd + y_off * jnp.exp2(cum_x)
            if job["dr"] == 0:
                y = y + xpair * dsk_ref[:, c0:c0 + LANES]
            job["y_put"](c0, y)
            tot = cum_x[job["tot_row"]:job["tot_row"] + 1, :]
            st_ref[job["dr"], :, c0:c0 + LANES] = st * jnp.exp2(tot) + sd


def _ssd_pair_products(job, g, pp, lo, st_ref):
    Q, N, P = SSD_CHUNK, SSD_STATE, SSD_HEAD_DIM
    dr, x_at, y_put, cum, cum_t, dt_t = (job[k] for k in ("dr", "x_at", "y_put", "cum", "cum_t", "dt_t"))
    tot_row, mask = job["tot_row"], job["mask"]
    bg_t, cg, cb = job["terms"][g]
    head = g * (SSD_HEADS // SSD_GROUPS) + 2 * pp
    c0 = head * P
    li = dr * SSD_HEADS + head
    xpair = x_at(c0)
    xpair_b = xpair.astype(BF16)
    cum_cols = [cum[:, li:li + 1], cum[:, li + 1:li + 2]]
    cum_x = jnp.where(lo, cum_cols[0], cum_cols[1])
    yd, sd = [], []
    for hh in range(2):
        dt_row = dt_t[li + hh:li + hh + 1, :]
        cum_row = cum_t[li + hh:li + hh + 1, :]
        decay = jnp.exp2(jnp.where(mask, cum_cols[hh] - cum_row, -jnp.inf))
        yd.append(jnp.dot((cb * decay * dt_row).astype(BF16), xpair_b, preferred_element_type=F32))
        to_end = jnp.exp2(cum_row[:, tot_row:tot_row + 1] - cum_row)
        sd.append(jnp.dot((bg_t * (dt_row * to_end)).astype(BF16), xpair_b, preferred_element_type=F32))
    st = st_ref[dr, :, c0:c0 + LANES]
    y_off = jnp.dot(cg, st.astype(BF16), preferred_element_type=F32)
    return job, c0, xpair, cum_x, jnp.where(lo, yd[0], yd[1]), y_off, jnp.where(lo, sd[0], sd[1]), st


def _ssd_seq_kernel(x_ref, nw_ref, mod_ref, wz_ref, wx_ref, wdt_ref, dtb_ref, wc_ref, bc_ref, alog_ref, dsk_ref, gnw_ref, wo_ref,
                    o_ref, so_ref, xpad_ref, xbc_ref, sz_ref, y_ref, yb_ref, st_ref):
    Q = SSD_CHUNK
    nc = x_ref.shape[0] // Q
    x = x_ref[...]
    hb = _modnorm(x, nw_ref[...], mod_ref[...]).astype(BF16)
    dt = _ssd_project(hb, None, wz_ref, wx_ref, wdt_ref, dtb_ref, wc_ref, bc_ref, xpad_ref, sz_ref, xbc_ref)
    st_ref[...] = jnp.zeros(st_ref.shape, F32)
    A = -jnp.exp(alog_ref[...])

    def x_at_chunk(ck):
        return lambda c0: xbc_ref[ck * Q:(ck + 1) * Q, c0:c0 + LANES]

    terms = [_ssd_group_terms(x_at_chunk(ck)) for ck in range(nc)]

    def y_put_to(ref, ck):
        def y_put(c0, y):
            ref[ck * Q:(ck + 1) * Q, c0:c0 + LANES] = y
        return y_put

    for k in range(nc):
        kb = nc - 1 - k
        _ssd_chunks([(0, x_at_chunk(k), dt[k * Q:(k + 1) * Q], y_put_to(y_ref, k), terms[k]),
                     (1, x_at_chunk(kb), dt[kb * Q:(kb + 1) * Q], y_put_to(yb_ref, kb), terms[kb])],
                    A, st_ref, dsk_ref)
    _ssd_emit_state(st_ref, so_ref)
    g = _rms((y_ref[...] + yb_ref[...]) * sz_ref[...], gnw_ref[...]).astype(BF16)
    o_ref[...] = x + mod_ref[:, 2 * D:3 * D] * _wdot(g, wo_ref, 0, D)


def _ssd_seq(x, seq_len, layer, row_fn, mod, nw, w_in, w_dt, dt_bias, w_conv, b_conv, a_log, d_skip, norm_w, w_out):
    M = x.shape[0]
    n_seq = M // seq_len
    HL = SSD_HALO
    row = lambda i: (i, 0)
    st_blk = pl.BlockSpec((None, 2, SSD_INNER, SSD_STATE), lambda i: (i, 0, 0, 0))
    in_specs, args, groups = _operands([
        (pl.BlockSpec((seq_len, D), row), x), (_const_spec((1, D)), nw),
        (_mod_spec(layer, lambda i: row_fn(i * seq_len)), mod),
        _W(w_in, SSD_INNER, 0), _W(w_in, SSD_XBC, SSD_INNER),
        (_const_spec((D, LANES)), w_dt), (_const_spec((1, LANES)), dt_bias),
        (_const_spec((SSD_CONV + 1, SSD_XBC)), w_conv), (_const_spec((1, SSD_XBC)), b_conv),
        (_const_spec((1, LANES)), a_log), (_const_spec((1, SSD_INNER)), d_skip),
        (_const_spec((1, SSD_INNER)), norm_w), _W(w_out)])
    return pl.pallas_call(
        _grouped(_ssd_seq_kernel, groups),
        grid=(n_seq,),
        in_specs=in_specs,
        out_specs=[pl.BlockSpec((seq_len, D), row), st_blk],
        out_shape=[jax.ShapeDtypeStruct((M, D), F32), jax.ShapeDtypeStruct((n_seq, 2, SSD_INNER, SSD_STATE), F32)],
        scratch_shapes=[pltpu.VMEM((SSD_XBC // LANES, seq_len + 2 * HL, LANES), F32),
                        pltpu.VMEM((seq_len, SSD_XBC), F32), pltpu.VMEM((seq_len, SSD_INNER), F32),
                        pltpu.VMEM((seq_len, SSD_INNER), F32), pltpu.VMEM((seq_len, SSD_INNER), F32),
                        pltpu.VMEM((2, SSD_STATE, SSD_INNER), F32)],
        compiler_params=_cparams(1),
        name="ssd_seq",
    )(*args)


def _ssd_core(xbc, dtp, a_log, d_skip, h0, n_seq, seq_len, emit_state):
    Q = SSD_CHUNK
    nc = seq_len // Q
    M = n_seq * seq_len

    def chunk(dr):
        return (lambda b, c: b * nc + c) if dr == 0 else (lambda b, c: b * nc + nc - 1 - c)

    in_specs, args = [], []
    for dr in range(2):
        ch = chunk(dr)
        in_specs += [pl.BlockSpec((Q, SSD_XBC), lambda b, c, ch=ch: (ch(b, c), 0)),
                     pl.BlockSpec((Q, LANES), lambda b, c, ch=ch: (ch(b, c), 0))]
        args += [xbc, dtp]
    in_specs += [_const_spec((1, LANES)), _const_spec((1, SSD_INNER))]
    args += [a_log, d_skip]
    st_blk = pl.BlockSpec((None, 2, SSD_INNER, SSD_STATE), lambda b, c: (b, 0, 0, 0))
    has_h0 = h0 is not None
    if has_h0:
        in_specs.append(st_blk)
        args.append(h0)
    out_specs = [pl.BlockSpec((Q, SSD_INNER), lambda b, c: (chunk(0)(b, c), 0)),
                 pl.BlockSpec((Q, SSD_INNER), lambda b, c: (chunk(1)(b, c), 0))]
    out_shape = [jax.ShapeDtypeStruct((M, SSD_INNER), F32)] * 2
    if emit_state:
        out_specs.append(st_blk)
        out_shape.append(jax.ShapeDtypeStruct((n_seq, 2, SSD_INNER, SSD_STATE), F32))
    return pl.pallas_call(
        functools.partial(_ssd_core_kernel, nc, has_h0, emit_state),
        grid=(n_seq, nc),
        in_specs=in_specs,
        out_specs=out_specs,
        out_shape=out_shape,
        scratch_shapes=[pltpu.VMEM((2, SSD_STATE, SSD_INNER), F32)],
        compiler_params=_cparams(2),
        name="ssd_core",
    )(*args)


def _ssd_out_kernel(yf_ref, yb_ref, sz_ref, x_ref, mod_ref, nw_ref, w_ref, o_ref):
    g = _rms((yf_ref[...] + yb_ref[...]) * sz_ref[...], nw_ref[...]).astype(BF16)
    o_ref[...] = x_ref[...] + mod_ref[:, 2 * D:3 * D] * _wdot(g, w_ref, 0, D)


def _ssd_out(yf, yb, sz, x, layer, row_fn, mod, norm_w, w_out):
    M = x.shape[0]
    tm = TOKEN_TILE
    row = lambda i: (i, 0)
    wide = pl.BlockSpec((tm, SSD_INNER), row)
    in_specs, args, groups = _operands([
        (wide, yf), (wide, yb), (wide, sz), (pl.BlockSpec((tm, D), row), x),
        (_mod_spec(layer, lambda i: row_fn(i * tm)), mod), (_const_spec((1, SSD_INNER)), norm_w), _W(w_out)])
    return pl.pallas_call(
        _grouped(_ssd_out_kernel, groups),
        grid=(M // tm,),
        in_specs=in_specs,
        out_specs=pl.BlockSpec((tm, D), row),
        out_shape=jax.ShapeDtypeStruct((M, D), F32),
        compiler_params=_cparams(1),
        name="ssd_out",
    )(*args)


def _ssd_layer(xc, xl, layer, mod, nw, p, n_lat):
    j = layer // 3
    w_in = p["ssd_w_in"][j].astype(BF16)
    w_dt = jnp.pad(w_in[:, SSD_IN_COLS:], ((0, 0), (0, LANES - 2 * SSD_HEADS)))
    dt_bias = jnp.pad(p["ssd_dt_bias"][j].reshape(1, -1), ((0, 0), (0, LANES - 2 * SSD_HEADS)))
    a_log = jnp.pad(p["ssd_a_log"][j].reshape(1, -1), ((0, 0), (0, LANES - 2 * SSD_HEADS)))
    w_conv = jnp.pad(p["ssd_w_conv"][j], ((0, 1), (0, 0)))
    b_conv = p["ssd_b_conv"][j][None]
    d_skip = jnp.repeat(p["ssd_d"][j], SSD_HEAD_DIM)[None]
    norm_w = p["ssd_norm_w"][j][None]
    w_out = p["ssd_w_out"][j].astype(BF16)
    row_ctx = lambda tok: 0
    row_lat = lambda tok: 1 + tok // DEC_SEQ
    oc, s_new = _ssd_seq(xc, SEQ, layer, row_ctx, mod, nw, w_in, w_dt, dt_bias, w_conv, b_conv, a_log, d_skip, norm_w, w_out)
    szl, xbc_l, dt_l = _ssd_in(xl, DEC_SEQ, layer, row_lat, mod, nw, w_in, w_dt, dt_bias, w_conv, b_conv)
    h0 = p["state_ssm"][:n_lat, j].reshape(n_lat, 2, SSD_INNER, SSD_STATE)
    yfl, ybl = _ssd_core(xbc_l, dt_l, a_log, d_skip, h0, n_lat, DEC_SEQ, False)
    ol = _ssd_out(yfl, ybl, szl, xl, layer, row_lat, mod, norm_w, w_out)
    return oc, ol, s_new


def kernel(x_prompt, x_sample, cache_k, cache_v, state_ssm, c, c_ctx, ada_w, ada_b, norm_w, final_norm_w, conv_w_in, conv_b_in, conv_w_dw, conv_b_dw, conv_ln_w, conv_ln_b, conv_w_out, conv_b_out, na_w_in, na_rpb, na_w_out, ssd_w_in, ssd_w_conv, ssd_b_conv, ssd_a_log, ssd_dt_bias, ssd_d, ssd_norm_w, ssd_w_out):
    p = dict(cache_k=cache_k, cache_v=cache_v, state_ssm=state_ssm,
             conv_w_in=conv_w_in, conv_b_in=conv_b_in, conv_w_dw=conv_w_dw, conv_b_dw=conv_b_dw,
             conv_ln_w=conv_ln_w, conv_ln_b=conv_ln_b, conv_w_out=conv_w_out, conv_b_out=conv_b_out,
             na_w_in=na_w_in, na_rpb=na_rpb, na_w_out=na_w_out,
             ssd_w_in=ssd_w_in, ssd_w_conv=ssd_w_conv, ssd_b_conv=ssd_b_conv, ssd_a_log=ssd_a_log,
             ssd_dt_bias=ssd_dt_bias, ssd_d=ssd_d, ssd_norm_w=ssd_norm_w, ssd_w_out=ssd_w_out)
    n_ctx, n_lat = x_prompt.shape[0], x_sample.shape[0]
    assert x_prompt.shape[1:] == (SEQ, D) and x_sample.shape[1:] == (DEC_SEQ, D)
    assert (DEPTH - 1) % 3 == 0, "the final RMSNorm is fused into a trailing Conformer layer"
    xc = x_prompt.reshape(n_ctx * SEQ, D)
    xl = x_sample.reshape(n_lat * DEC_SEQ, D)
    cond8 = jnp.concatenate([c_ctx[None], c, jnp.zeros((8 - 1 - n_lat, D), F32)], axis=0)
    mod = _modulation(cond8, ada_w, ada_b).reshape(DEPTH, 8, 1, 3 * D)
    row_ctx = lambda tok: 0
    row_lat = lambda tok: 1 + tok // DEC_SEQ
    new_k, new_v, new_s = [], [], []
    for i in range(DEPTH):
        kind, j = i % 3, i // 3
        nw = norm_w[i][None]
        if kind == 0:
            cp = _prep_conv(p, j)
            fw = final_norm_w[None] if i == DEPTH - 1 else None
            xc = _conv_layer(xc, SEQ, i, row_ctx, mod, nw, *cp, fw)
            xl = _conv_layer(xl, DEC_SEQ, i, row_lat, mod, nw, *cp, fw)
        elif kind == 1:
            xc, xl, k_c, v_c = _na_layer(xc, xl, i, mod, nw, p, n_lat)
            new_k.append(k_c.reshape(n_ctx, SEQ, NA_HEADS, NA_HEAD_DIM))
            new_v.append(v_c.reshape(n_ctx, SEQ, NA_HEADS, NA_HEAD_DIM))
        else:
            xc, xl, s_c = _ssd_layer(xc, xl, i, mod, nw, p, n_lat)
            new_s.append(s_c.reshape(n_ctx, 2, SSD_HEADS, SSD_HEAD_DIM, SSD_STATE))
    return (xc.reshape(n_ctx, SEQ, D), xl.reshape(n_lat, DEC_SEQ, D),
            jnp.stack(new_k, axis=1), jnp.stack(new_v, axis=1), jnp.stack(new_s, axis=1))
```
